```python
import jax, jax.numpy as jnp
from jax import lax
import numpy as np

D_MODEL = 1024
BATCH = 16
SEQ = 2048
DEPTH = 1

GRID_W = 64
Q_BLOCK = 128
ROPE_THETA = 10000.0
EPS = 1e-6
H_A = 8
QK_NOPE = 64
QK_ROPE = 32
V_DIM_A = 64
Q_LORA = 256
KV_LORA = 128
H_B = 8
KV_B = 2
HD_B = 64
D_FF = 4 * D_MODEL
PLE_DIM = 256
IN_SPLITS = [Q_LORA, KV_LORA + QK_ROPE, H_B * HD_B, KV_B * HD_B, KV_B * HD_B, D_MODEL, D_MODEL]
D_IN = sum(IN_SPLITS)

kernel_name = "hybrid_mla_axial_gqa_gated_block"


def rmsnorm(x, g):
    xf = x.astype(jnp.float32)
    y = xf * lax.rsqrt(jnp.mean(xf * xf, axis=-1, keepdims=True) + EPS)
    return (y * g.astype(jnp.float32)).astype(x.dtype)


def rope_angles(pos, dim):
    inv = ROPE_THETA ** (-jnp.arange(0, dim, 2, dtype=jnp.float32) / dim)
    return pos.astype(jnp.float32)[:, None] * inv[None, :]


def apply_rope(x, ang):
    cos = jnp.cos(ang)[None, :, None, :].astype(x.dtype)
    sin = jnp.sin(ang)[None, :, None, :].astype(x.dtype)
    x1, x2 = jnp.split(x, 2, axis=-1)
    return jnp.concatenate([x1 * cos - x2 * sin, x1 * sin + x2 * cos], axis=-1)


def blocked_attention(q, k, v):
    B, S, Hk, G, dk = q.shape
    nb = S // Q_BLOCK
    qb = q.reshape(B, nb, Q_BLOCK, Hk, G, dk).transpose(1, 0, 2, 3, 4, 5)

    def one_block(qblk):
        s = jnp.einsum("bqhgd,bkhd->bhgqk", qblk, k).astype(jnp.float32)
        pr = jax.nn.softmax(s, axis=-1).astype(v.dtype)
        return jnp.einsum("bhgqk,bkhd->bqhgd", pr, v)

    o = lax.map(one_block, qb)
    return o.transpose(1, 0, 2, 3, 4, 5).reshape(B, S, Hk * G, v.shape[-1])


def _fwd_setup_inputs(seed: int = 0) -> dict:
    key = jax.random.key(seed)
    ks = jax.random.split(key, 24)
    f32 = jnp.float32

    def w(k, shape, fan_in):
        return jax.random.normal(k, shape, f32) * (fan_in ** -0.5)

    def gain(k, shape):
        return 1.0 + 0.05 * jax.random.normal(k, shape, f32)

    L = DEPTH
    return {
        "x": jax.random.normal(ks[0], (BATCH, SEQ, D_MODEL), f32),
        "p": jax.random.normal(ks[1], (DEPTH, BATCH, SEQ, PLE_DIM), f32),
        "g_mix": gain(ks[2], (L, D_MODEL)),
        "w_in": w(ks[3], (L, D_MODEL, D_IN), D_MODEL),
        "g_qa": gain(ks[4], (L, Q_LORA)),
        "w_qb": w(ks[5], (L, Q_LORA, H_A * (QK_NOPE + QK_ROPE)), Q_LORA),
        "g_kva": gain(ks[6], (L, KV_LORA)),
        "w_kvb": w(ks[7], (L, KV_LORA, H_A * (QK_NOPE + V_DIM_A)), KV_LORA),
        "g_qn": gain(ks[8], (L, HD_B)),
        "g_kn": gain(ks[9], (L, HD_B)),
        "w_oa": w(ks[10], (L, H_A * V_DIM_A, D_MODEL), H_A * V_DIM_A),
        "w_ob": w(ks[11], (L, H_B * HD_B, D_MODEL), H_B * HD_B),
        "w_o": w(ks[12], (L, D_MODEL, D_MODEL), D_MODEL),
        "g_mlp": gain(ks[13], (L, D_MODEL)),
        "w_up": w(ks[14], (L, D_MODEL, D_FF), D_MODEL),
        "w_down": w(ks[15], (L, D_FF, D_MODEL), D_FF),
        "g_ple": gain(ks[16], (L, D_MODEL)),
        "w_ple_gate": w(ks[17], (L, D_MODEL, D_MODEL), D_MODEL),
        "w_ple": w(ks[18], (L, PLE_DIM, D_MODEL), PLE_DIM),
        "g_final": gain(ks[19], (D_MODEL,)),
    }


def _fwd_reference(x, p, g_mix, w_in, g_qa, w_qb, g_kva, w_kvb, g_qn, g_kn, w_oa, w_ob, w_o,
              g_mlp, w_up, w_down, g_ple, w_ple_gate, w_ple, g_final):
    B, S, D = x.shape
    ROWS = S // GRID_W
    t = jnp.arange(S)
    row = jnp.broadcast_to(jnp.arange(ROWS)[:, None], (ROWS, GRID_W)).reshape(-1)
    col = jnp.broadcast_to(jnp.arange(GRID_W)[None, :], (ROWS, GRID_W)).reshape(-1)
    ang_1d = rope_angles(t, QK_ROPE)
    ang_row = rope_angles(row, HD_B // 2)
    ang_col = rope_angles(col, HD_B // 2)
    split_idx = list(np.cumsum(IN_SPLITS)[:-1])
    scale_a = (QK_NOPE + QK_ROPE) ** -0.5
    scale_b = HD_B ** -0.5

    for i in range(DEPTH):
        h = rmsnorm(x, g_mix[i])
        z = jnp.einsum("bsd,de->bse", h, w_in[i])
        q_lat, kv_lat, qb, kb, vb, gate_a, gate_b = jnp.split(z, split_idx, axis=-1)

        cq = rmsnorm(q_lat, g_qa[i])
        qa = jnp.einsum("bsr,re->bse", cq, w_qb[i]).reshape(B, S, H_A, QK_NOPE + QK_ROPE)
        qa_nope, qa_rope = jnp.split(qa, [QK_NOPE], axis=-1)
        qa_rope = apply_rope(qa_rope, ang_1d)
        c_kv, k_pe = jnp.split(kv_lat, [KV_LORA], axis=-1)
        c_kv = rmsnorm(c_kv, g_kva[i])
        k_pe = apply_rope(k_pe[:, :, None, :], ang_1d)
        kva = jnp.einsum("bsr,re->bse", c_kv, w_kvb[i]).reshape(B, S, H_A, QK_NOPE + V_DIM_A)
        ka_nope, va = jnp.split(kva, [QK_NOPE], axis=-1)
        qa_full = (jnp.concatenate([qa_nope, qa_rope], axis=-1) * scale_a)[:, :, :, None, :]
        ka_full = jnp.concatenate([ka_nope, jnp.broadcast_to(k_pe, (B, S, H_A, QK_ROPE))], axis=-1)
        oa = blocked_attention(qa_full, ka_full, va).reshape(B, S, H_A * V_DIM_A)
        ya = jnp.einsum("bse,ed->bsd", oa, w_oa[i])

        qb = rmsnorm(qb.reshape(B, S, H_B, HD_B), g_qn[i])
        kb = rmsnorm(kb.reshape(B, S, KV_B, HD_B), g_kn[i])
        vb = vb.reshape(B, S, KV_B, HD_B)
        qr, qc = jnp.split(qb, 2, axis=-1)
        qb = jnp.concatenate([apply_rope(qr, ang_row), apply_rope(qc, ang_col)], axis=-1)
        kr, kc = jnp.split(kb, 2, axis=-1)
        kb = jnp.concatenate([apply_rope(kr, ang_row), apply_rope(kc, ang_col)], axis=-1)
        qb = (qb * scale_b).reshape(B, S, KV_B, H_B // KV_B, HD_B)
        ob = blocked_attention(qb, kb, vb).reshape(B, S, H_B * HD_B)
        yb = jnp.einsum("bse,ed->bsd", ob, w_ob[i])

        merged = jax.nn.sigmoid(gate_a) * ya + jax.nn.sigmoid(gate_b) * yb
        x = x + jnp.einsum("bsd,de->bse", merged, w_o[i])

        h2 = rmsnorm(x, g_mlp[i])
        u = jax.nn.relu(jnp.einsum("bsd,df->bsf", h2, w_up[i]))
        x = x + jnp.einsum("bsf,fd->bsd", u * u, w_down[i])

        h3 = rmsnorm(x, g_ple[i])
        gate = jax.nn.sigmoid(jnp.einsum("bsd,de->bse", h3, w_ple_gate[i]))
        x = x + gate * jnp.einsum("bsk,kd->bsd", p[i], w_ple[i])

    return rmsnorm(x, g_final)


import jax as _jax
import jax.numpy as _jnp

TWIN_FORMAT = 'train_step'
FWD_PARAMS = ['x', 'p', 'g_mix', 'w_in', 'g_qa', 'w_qb', 'g_kva', 'w_kvb', 'g_qn', 'g_kn', 'w_oa', 'w_ob', 'w_o', 'g_mlp', 'w_up', 'w_down', 'g_ple', 'w_ple_gate', 'w_ple', 'g_final']
TWIN_WEIGHTS = ['g_mix', 'w_in', 'g_qa', 'w_qb', 'g_kva', 'w_kvb', 'g_qn', 'g_kn', 'w_oa', 'w_ob', 'w_o', 'g_mlp', 'w_up', 'w_down', 'g_ple', 'w_ple_gate', 'w_ple', 'g_final']
TWIN_DIFF_INPUT = 'x'
TWIN_INPUTS = ['x', 'p', 'g_mix', 'w_in', 'g_qa', 'w_qb', 'g_kva', 'w_kvb', 'g_qn', 'g_kn', 'w_oa', 'w_ob', 'w_o', 'g_mlp', 'w_up', 'w_down', 'g_ple', 'w_ple_gate', 'w_ple', 'g_final', 'loss_target', 'm_g_mix', 'm_w_in', 'm_g_qa', 'm_w_qb', 'm_g_kva', 'm_w_kvb', 'm_g_qn', 'm_g_kn', 'm_w_oa', 'm_w_ob', 'm_w_o', 'm_g_mlp', 'm_w_up', 'm_w_down', 'm_g_ple', 'm_w_ple_gate', 'm_w_ple', 'm_g_final', 'v_g_mix', 'v_w_in', 'v_g_qa', 'v_w_qb', 'v_g_kva', 'v_w_kvb', 'v_g_qn', 'v_g_kn', 'v_w_oa', 'v_w_ob', 'v_w_o', 'v_g_mlp', 'v_w_up', 'v_w_down', 'v_g_ple', 'v_w_ple_gate', 'v_w_ple', 'v_g_final']
TWIN_OUTPUTS = ['loss', 'grad_x', 'grad_g_mix', 'grad_w_in', 'grad_g_qa', 'grad_w_qb', 'grad_g_kva', 'grad_w_kvb', 'grad_g_qn', 'grad_g_kn', 'grad_w_oa', 'grad_w_ob', 'grad_w_o', 'grad_g_mlp', 'grad_w_up', 'grad_w_down', 'grad_g_ple', 'grad_w_ple_gate', 'grad_w_ple', 'grad_g_final', 'delta_g_mix', 'delta_w_in', 'delta_g_qa', 'delta_w_qb', 'delta_g_kva', 'delta_w_kvb', 'delta_g_qn', 'delta_g_kn', 'delta_w_oa', 'delta_w_ob', 'delta_w_o', 'delta_g_mlp', 'delta_w_up', 'delta_w_down', 'delta_g_ple', 'delta_w_ple_gate', 'delta_w_ple', 'delta_g_final', 'new_m_g_mix', 'new_m_w_in', 'new_m_g_qa', 'new_m_w_qb', 'new_m_g_kva', 'new_m_w_kvb', 'new_m_g_qn', 'new_m_g_kn', 'new_m_w_oa', 'new_m_w_ob', 'new_m_w_o', 'new_m_g_mlp', 'new_m_w_up', 'new_m_w_down', 'new_m_g_ple', 'new_m_w_ple_gate', 'new_m_w_ple', 'new_m_g_final', 'new_v_g_mix', 'new_v_w_in', 'new_v_g_qa', 'new_v_w_qb', 'new_v_g_kva', 'new_v_w_kvb', 'new_v_g_qn', 'new_v_g_kn', 'new_v_w_oa', 'new_v_w_ob', 'new_v_w_o', 'new_v_g_mlp', 'new_v_w_up', 'new_v_w_down', 'new_v_g_ple', 'new_v_w_ple_gate', 'new_v_w_ple', 'new_v_g_final']
TWIN_LEAF_KINDS = {'loss': 'loss', 'grad_x': 'grad_x', 'grad_g_mix': 'grad_w', 'grad_w_in': 'grad_w', 'grad_g_qa': 'grad_w', 'grad_w_qb': 'grad_w', 'grad_g_kva': 'grad_w', 'grad_w_kvb': 'grad_w', 'grad_g_qn': 'grad_w', 'grad_g_kn': 'grad_w', 'grad_w_oa': 'grad_w', 'grad_w_ob': 'grad_w', 'grad_w_o': 'grad_w', 'grad_g_mlp': 'grad_w', 'grad_w_up': 'grad_w', 'grad_w_down': 'grad_w', 'grad_g_ple': 'grad_w', 'grad_w_ple_gate': 'grad_w', 'grad_w_ple': 'grad_w', 'grad_g_final': 'grad_w', 'delta_g_mix': 'delta_w', 'delta_w_in': 'delta_w', 'delta_g_qa': 'delta_w', 'delta_w_qb': 'delta_w', 'delta_g_kva': 'delta_w', 'delta_w_kvb': 'delta_w', 'delta_g_qn': 'delta_w', 'delta_g_kn': 'delta_w', 'delta_w_oa': 'delta_w', 'delta_w_ob': 'delta_w', 'delta_w_o': 'delta_w', 'delta_g_mlp': 'delta_w', 'delta_w_up': 'delta_w', 'delta_w_down': 'delta_w', 'delta_g_ple': 'delta_w', 'delta_w_ple_gate': 'delta_w', 'delta_w_ple': 'delta_w', 'delta_g_final': 'delta_w', 'new_m_g_mix': 'new_m', 'new_m_w_in': 'new_m', 'new_m_g_qa': 'new_m', 'new_m_w_qb': 'new_m', 'new_m_g_kva': 'new_m', 'new_m_w_kvb': 'new_m', 'new_m_g_qn': 'new_m', 'new_m_g_kn': 'new_m', 'new_m_w_oa': 'new_m', 'new_m_w_ob': 'new_m', 'new_m_w_o': 'new_m', 'new_m_g_mlp': 'new_m', 'new_m_w_up': 'new_m', 'new_m_w_down': 'new_m', 'new_m_g_ple': 'new_m', 'new_m_w_ple_gate': 'new_m', 'new_m_w_ple': 'new_m', 'new_m_g_final': 'new_m', 'new_v_g_mix': 'new_v', 'new_v_w_in': 'new_v', 'new_v_g_qa': 'new_v', 'new_v_w_qb': 'new_v', 'new_v_g_kva': 'new_v', 'new_v_w_kvb': 'new_v', 'new_v_g_qn': 'new_v', 'new_v_g_kn': 'new_v', 'new_v_w_oa': 'new_v', 'new_v_w_ob': 'new_v', 'new_v_w_o': 'new_v', 'new_v_g_mlp': 'new_v', 'new_v_w_up': 'new_v', 'new_v_w_down': 'new_v', 'new_v_g_ple': 'new_v', 'new_v_w_ple_gate': 'new_v', 'new_v_w_ple': 'new_v', 'new_v_g_final': 'new_v'}


def _forward(args):
    return _fwd_reference(*[args[k] for k in FWD_PARAMS])


def _output_shape():
    out = _jax.eval_shape(lambda: _forward(_fwd_setup_inputs(0)))
    return out.shape, out.dtype

N_MICROBATCH = 1
ADAM_LR = 0.001
ADAM_B1 = 0.9
ADAM_B2 = 0.999
ADAM_EPS = 1e-08
ADAM_WD = 0.01
ADAM_STEP = 10
PER_EXAMPLE_BATCH_AXIS = {'x': 0, 'p': 1, 'loss_target': 0}
SHARED_INPUTS = []
_WEIGHT_DTYPES = {'g_mix': _jnp.float32, 'w_in': _jnp.float32, 'g_qa': _jnp.float32, 'w_qb': _jnp.float32, 'g_kva': _jnp.float32, 'w_kvb': _jnp.float32, 'g_qn': _jnp.float32, 'g_kn': _jnp.float32, 'w_oa': _jnp.float32, 'w_ob': _jnp.float32, 'w_o': _jnp.float32, 'g_mlp': _jnp.float32, 'w_up': _jnp.float32, 'w_down': _jnp.float32, 'g_ple': _jnp.float32, 'w_ple_gate': _jnp.float32, 'w_ple': _jnp.float32, 'g_final': _jnp.float32}
MOMENT_SCALE = {'g_mix': 2.825720e-02, 'w_in': 1.533752e-02, 'g_qa': 2.489705e-02, 'w_qb': 1.460358e-02, 'g_kva': 5.566627e-02, 'w_kvb': 1.709108e-02, 'g_qn': 3.980393e-02, 'g_kn': 4.235165e-02, 'w_oa': 1.294979e-02, 'w_ob': 1.054206e-02, 'w_o': 1.676149e-02, 'g_mlp': 1.521636e-01, 'w_up': 7.675857e-02, 'w_down': 2.124916e-01, 'g_ple': 3.162209e-02, 'w_ple_gate': 3.059183e-02, 'w_ple': 5.821850e-02, 'g_final': 3.223795e+01}


def _to_microbatches(a, axis):
    t = _jnp.moveaxis(a, axis, 0)
    t = t.reshape((N_MICROBATCH, t.shape[0] // N_MICROBATCH) + t.shape[1:])
    return _jnp.moveaxis(t, 1, axis + 1)


def setup_inputs(seed: int = 0) -> dict:
    inp = _fwd_setup_inputs(seed)
    key = _jax.random.fold_in(_jax.random.key(seed), 7919)
    shape, _ = _output_shape()
    out = dict(inp)
    out["loss_target"] = _jax.random.normal(_jax.random.fold_in(key, 0), shape, _jnp.float32)
    for i, name in enumerate(TWIN_WEIGHTS):
        w = inp[name].astype(_jnp.float32)
        if MOMENT_SCALE is None:
            s = _jnp.sqrt(_jnp.mean(_jnp.square(w)) + 1e-30)
        else:
            s = MOMENT_SCALE[name]
        km, kv = _jax.random.split(_jax.random.fold_in(key, i + 1))
        out[name] = w
        out["m_" + name] = s * _jax.random.normal(km, w.shape, _jnp.float32)
        out["v_" + name] = (s * s) * _jax.random.uniform(kv, w.shape, _jnp.float32, 0.5, 1.5)
    if N_MICROBATCH > 1:
        for name, axis in PER_EXAMPLE_BATCH_AXIS.items():
            out[name] = _to_microbatches(out[name], axis)
    return {'x': out['x'], 'p': out['p'], 'g_mix': out['g_mix'], 'w_in': out['w_in'], 'g_qa': out['g_qa'], 'w_qb': out['w_qb'], 'g_kva': out['g_kva'], 'w_kvb': out['w_kvb'], 'g_qn': out['g_qn'], 'g_kn': out['g_kn'], 'w_oa': out['w_oa'], 'w_ob': out['w_ob'], 'w_o': out['w_o'], 'g_mlp': out['g_mlp'], 'w_up': out['w_up'], 'w_down': out['w_down'], 'g_ple': out['g_ple'], 'w_ple_gate': out['w_ple_gate'], 'w_ple': out['w_ple'], 'g_final': out['g_final'], 'loss_target': out['loss_target'], 'm_g_mix': out['m_g_mix'], 'm_w_in': out['m_w_in'], 'm_g_qa': out['m_g_qa'], 'm_w_qb': out['m_w_qb'], 'm_g_kva': out['m_g_kva'], 'm_w_kvb': out['m_w_kvb'], 'm_g_qn': out['m_g_qn'], 'm_g_kn': out['m_g_kn'], 'm_w_oa': out['m_w_oa'], 'm_w_ob': out['m_w_ob'], 'm_w_o': out['m_w_o'], 'm_g_mlp': out['m_g_mlp'], 'm_w_up': out['m_w_up'], 'm_w_down': out['m_w_down'], 'm_g_ple': out['m_g_ple'], 'm_w_ple_gate': out['m_w_ple_gate'], 'm_w_ple': out['m_w_ple'], 'm_g_final': out['m_g_final'], 'v_g_mix': out['v_g_mix'], 'v_w_in': out['v_w_in'], 'v_g_qa': out['v_g_qa'], 'v_w_qb': out['v_w_qb'], 'v_g_kva': out['v_g_kva'], 'v_w_kvb': out['v_w_kvb'], 'v_g_qn': out['v_g_qn'], 'v_g_kn': out['v_g_kn'], 'v_w_oa': out['v_w_oa'], 'v_w_ob': out['v_w_ob'], 'v_w_o': out['v_w_o'], 'v_g_mlp': out['v_g_mlp'], 'v_w_up': out['v_w_up'], 'v_w_down': out['v_w_down'], 'v_g_ple': out['v_g_ple'], 'v_w_ple_gate': out['v_w_ple_gate'], 'v_w_ple': out['v_w_ple'], 'v_g_final': out['v_g_final']}


def _loss(weights, diff, rest, loss_target):
    with _jax.named_scope("forward"):
        args = {**rest, TWIN_DIFF_INPUT: diff, **{k: w.astype(_WEIGHT_DTYPES[k]) for k, w in weights.items()}}
        y = _forward(args)
    with _jax.named_scope("loss_head"):
        err = _jnp.square(y.astype(_jnp.float32) - loss_target)
        return 0.5 * _jnp.sum(_jnp.mean(err, axis=-1)) if err.ndim else 0.5 * err


def _adamw(w, g, m, v):
    m = ADAM_B1 * m + (1.0 - ADAM_B1) * g
    v = ADAM_B2 * v + (1.0 - ADAM_B2) * _jnp.square(g)
    m_hat = m / (1.0 - ADAM_B1 ** ADAM_STEP)
    v_hat = v / (1.0 - ADAM_B2 ** ADAM_STEP)
    delta = -ADAM_LR * (m_hat / (_jnp.sqrt(v_hat) + ADAM_EPS) + ADAM_WD * w)
    return delta, m, v


def reference(x, p, g_mix, w_in, g_qa, w_qb, g_kva, w_kvb, g_qn, g_kn, w_oa, w_ob, w_o, g_mlp, w_up, w_down, g_ple, w_ple_gate, w_ple, g_final, loss_target, m_g_mix, m_w_in, m_g_qa, m_w_qb, m_g_kva, m_w_kvb, m_g_qn, m_g_kn, m_w_oa, m_w_ob, m_w_o, m_g_mlp, m_w_up, m_w_down, m_g_ple, m_w_ple_gate, m_w_ple, m_g_final, v_g_mix, v_w_in, v_g_qa, v_w_qb, v_g_kva, v_w_kvb, v_g_qn, v_g_kn, v_w_oa, v_w_ob, v_w_o, v_g_mlp, v_w_up, v_w_down, v_g_ple, v_w_ple_gate, v_w_ple, v_g_final):
    given = dict(x=x, p=p, g_mix=g_mix, w_in=w_in, g_qa=g_qa, w_qb=w_qb, g_kva=g_kva, w_kvb=w_kvb, g_qn=g_qn, g_kn=g_kn, w_oa=w_oa, w_ob=w_ob, w_o=w_o, g_mlp=g_mlp, w_up=w_up, w_down=w_down, g_ple=g_ple, w_ple_gate=w_ple_gate, w_ple=w_ple, g_final=g_final, loss_target=loss_target, m_g_mix=m_g_mix, m_w_in=m_w_in, m_g_qa=m_g_qa, m_w_qb=m_w_qb, m_g_kva=m_g_kva, m_w_kvb=m_w_kvb, m_g_qn=m_g_qn, m_g_kn=m_g_kn, m_w_oa=m_w_oa, m_w_ob=m_w_ob, m_w_o=m_w_o, m_g_mlp=m_g_mlp, m_w_up=m_w_up, m_w_down=m_w_down, m_g_ple=m_g_ple, m_w_ple_gate=m_w_ple_gate, m_w_ple=m_w_ple, m_g_final=m_g_final, v_g_mix=v_g_mix, v_w_in=v_w_in, v_g_qa=v_g_qa, v_w_qb=v_w_qb, v_g_kva=v_g_kva, v_w_kvb=v_w_kvb, v_g_qn=v_g_qn, v_g_kn=v_g_kn, v_w_oa=v_w_oa, v_w_ob=v_w_ob, v_w_o=v_w_o, v_g_mlp=v_g_mlp, v_w_up=v_w_up, v_w_down=v_w_down, v_g_ple=v_g_ple, v_w_ple_gate=v_w_ple_gate, v_w_ple=v_w_ple, v_g_final=v_g_final)
    weights = {n: given[n] for n in TWIN_WEIGHTS}
    shared = {n: given[n] for n in SHARED_INPUTS}
    per_example = {n: given[n] for n in ['x', 'p']}
    grad_fn = _jax.value_and_grad(_loss, argnums=(0, 1))

    def one_microbatch(ex, loss_target):
        ex = dict(ex)
        diff = ex.pop(TWIN_DIFF_INPUT)
        return grad_fn(weights, diff, {**shared, **ex}, loss_target)

    if N_MICROBATCH == 1:
        loss, (grad_w, grad_x) = one_microbatch(per_example, given["loss_target"])
    else:
        def body(carry, xs):
            loss_sum, grad_sum = carry
            l_k, (gw_k, gx_k) = one_microbatch(xs[0], xs[1])
            with _jax.named_scope("update"):
                return (loss_sum + l_k, _jax.tree.map(_jnp.add, grad_sum, gw_k)), gx_k

        init = (_jnp.zeros((), _jnp.float32), _jax.tree.map(_jnp.zeros_like, weights))
        (loss, grad_w), grad_x = _jax.lax.scan(body, init, (per_example, given["loss_target"]))
    with _jax.named_scope("update"):
        delta_w, new_m, new_v = {}, {}, {}
        for n in TWIN_WEIGHTS:
            delta_w[n], new_m[n], new_v[n] = _adamw(weights[n], grad_w[n], given["m_" + n], given["v_" + n])
    return (loss, grad_x, *[grad_w[n] for n in TWIN_WEIGHTS], *[delta_w[n] for n in TWIN_WEIGHTS],
            *[new_m[n] for n in TWIN_WEIGHTS], *[new_v[n] for n in TWIN_WEIGHTS])
```

```python
import functools

import numpy as np
import jax
import jax.numpy as jnp
from jax import lax
from jax.experimental import pallas as pl
from jax.experimental.pallas import tpu as pltpu

F32 = jnp.float32
BF16 = jnp.bfloat16
MESH = pl.DeviceIdType.MESH

D_MODEL = 1024
GRID_W = 64
ROPE_THETA = 10000.0
EPS = 1e-6
H_A, QK_NOPE, QK_ROPE, V_DIM_A, Q_LORA, KV_LORA = 8, 64, 32, 64, 256, 128
H_B, KV_B, HD_B = 8, 2, 64
D_FF = 4096
PLE_DIM = 256
HEAD_W = 128
SCALE_A = (QK_NOPE + QK_ROPE) ** -0.5
SCALE_B = HD_B ** -0.5

ADAM_LR, ADAM_B1, ADAM_B2, ADAM_EPS, ADAM_WD, ADAM_STEP = 0.001, 0.9, 0.999, 1e-08, 0.01, 10
M_HAT_DIV = 1.0 - ADAM_B1 ** ADAM_STEP
V_HAT_DIV = 1.0 - ADAM_B2 ** ADAM_STEP

VMEM_LIMIT_BYTES = 56 * 1024 * 1024

ZC_QLAT, ZC_CKV, ZC_KPE, ZC_QB, ZC_KB, ZC_VB, ZC_GA, ZC_GB = 0, 2, 3, 4, 12, 14, 16, 24
Z_WIDTH = 32 * HEAD_W

BIG = [
    ("w_in", 1024, 3232, 1), ("w_qb", 256, 768, 1), ("w_kvb", 128, 1024, 1), ("w_oa", 512, 1024, 1),
    ("w_ob", 512, 1024, 1), ("w_o", 1024, 1024, 0), ("w_up", 1024, 4096, 1), ("w_down", 4096, 1024, 0),
    ("w_ple_gate", 1024, 1024, 0), ("w_ple", 256, 1024, 1),
]
PACK_W = 1024
PACK_ROWS = sum(r * c // 4 // PACK_W for _, r, c, _ in BIG)
PACK_TILE = 768
PACK_ROWS_PAD = -(-PACK_ROWS // PACK_TILE) * PACK_TILE

SMALL = [("g_mix", 1024), ("g_qa", 256), ("g_kva", 128), ("g_qn", 64), ("g_kn", 64), ("g_mlp", 1024),
         ("g_ple", 1024), ("g_final", 1024)]
SMALL_N = sum(n for _, n in SMALL)
LOSS_ROW0 = 40
SMALL_ROWS = 48


def _params(sem):
    return pltpu.CompilerParams(dimension_semantics=sem, vmem_limit_bytes=VMEM_LIMIT_BYTES)


def _sigmoid(v):
    return 1.0 / (1.0 + jnp.exp(-v))


def _perm(v, p_ref):
    pm = p_ref[...]
    hi = v.astype(BF16)
    lo = (v - hi.astype(F32)).astype(BF16)
    return (jnp.dot(hi, pm, preferred_element_type=F32) + jnp.dot(lo, pm, preferred_element_type=F32))


def _rms(v, g, n):
    rs = lax.rsqrt(jnp.sum(v * v, axis=-1, keepdims=True) * (1.0 / n) + EPS)
    return v * rs * g


def _rms_bwd(dy, v, g, n):
    rs = lax.rsqrt(jnp.sum(v * v, axis=-1, keepdims=True) * (1.0 / n) + EPS)
    vh = v * rs
    dyg = dy * g
    dx = rs * (dyg - vh * (jnp.sum(dyg * vh, axis=-1, keepdims=True) * (1.0 / n)))
    return dx, jnp.sum(dy * vh, axis=0, keepdims=True)


def _mm(name, a, b, *, trans_b=False, out_dtypes=(F32,), epi=None, extras=(), tm=512, tn=None):
    m, k = a.shape
    n = b.shape[0] if trans_b else b.shape[1]
    tn = n if tn is None else min(tn, n)
    tm = min(tm, m)
    assert m % tm == 0 and n % tn == 0 and (b.shape[1] if trans_b else b.shape[0]) == k
    n_ex = len(extras)
    dims = (((1,), (1,)), ((), ())) if trans_b else (((1,), (0,)), ((), ()))

    def body(a_ref, b_ref, *rest):
        acc = lax.dot_general(a_ref[...].astype(BF16), b_ref[...].astype(BF16), dims, preferred_element_type=F32)
        res = (acc,) if epi is None else epi(acc, *[e[...] for e in rest[:n_ex]])
        for o_ref, r in zip(rest[n_ex:], res):
            o_ref[...] = r.astype(o_ref.dtype)

    a_spec = pl.BlockSpec((tm, k), lambda j, i: (i, 0))
    b_spec = pl.BlockSpec((tn, k), lambda j, i: (j, 0)) if trans_b else pl.BlockSpec((k, tn), lambda j, i: (0, j))
    t_spec = pl.BlockSpec((tm, tn), lambda j, i: (i, j))
    outs = pl.pallas_call(
        body, out_shape=[jax.ShapeDtypeStruct((m, n), d) for d in out_dtypes], grid=(n // tn, m // tm),
        in_specs=[a_spec, b_spec] + [t_spec] * n_ex, out_specs=[t_spec] * len(out_dtypes),
        compiler_params=_params(("parallel", "parallel")), name=name)(a, b, *extras)
    return outs[0] if len(out_dtypes) == 1 else outs


def _mm_tn(name, a, b, *, tk=1024, tn=1024, tt=1024):
    t, k = a.shape
    n = b.shape[1]
    tk, tn, tt = min(tk, k), min(tn, n), min(tt, t)
    assert b.shape[0] == t and k % tk == 0 and n % tn == 0 and t % tt == 0

    def body(a_ref, b_ref, o_ref):
        part = lax.dot_general(a_ref[...].astype(BF16), b_ref[...].astype(BF16), (((0,), (0,)), ((), ())),
                               preferred_element_type=F32)

        @pl.when(pl.program_id(2) == 0)
        def _():
            o_ref[...] = part

        @pl.when(pl.program_id(2) != 0)
        def _():
            o_ref[...] += part

    return pl.pallas_call(
        body, out_shape=jax.ShapeDtypeStruct((k, n), F32), grid=(k // tk, n // tn, t // tt),
        in_specs=[pl.BlockSpec((tt, tk), lambda ki, ni, ti: (ti, ki)), pl.BlockSpec((tt, tn), lambda ki, ni, ti: (ti, ni))],
        out_specs=pl.BlockSpec((tk, tn), lambda ki, ni, ti: (ki, ni)),
        compiler_params=_params(("parallel", "parallel", "arbitrary")), name=name)(a, b)


def _rowwise(name, fn, ins, outs, *, consts=(), pos=(), accs=(), heads=1, tm=512, seq=None):
    t = ins[0][0].shape[0]
    tm = min(tm, t if seq is None else seq)
    assert t % tm == 0 and (seq is None or seq % tm == 0)
    n_in, n_pos, n_c, n_out, n_acc = len(ins), len(pos), len(consts), len(outs), len(accs)

    def body(*refs):
        vals = [r[...] for r in refs[:n_in + n_pos + n_c]]
        res = fn(*vals)
        o_refs = refs[n_in + n_pos + n_c:]
        for o_ref, r in zip(o_refs[:n_out], res[:n_out]):
            o_ref[...] = r.astype(o_ref.dtype)
        if n_acc:
            first = jnp.logical_and(pl.program_id(0) == 0, pl.program_id(1) == 0)

            @pl.when(first)
            def _():
                for o_ref, r in zip(o_refs[n_out:], res[n_out:]):
                    o_ref[...] = r

            @pl.when(jnp.logical_not(first))
            def _():
                for o_ref, r in zip(o_refs[n_out:], res[n_out:]):
                    o_ref[...] += r

    def tiled(width, c0, per_head):
        return pl.BlockSpec((tm, width), (lambda h, i: (i, c0 + h)) if per_head else (lambda h, i: (i, c0)))

    in_specs = [tiled(w, c0, ph) for _, w, c0, ph in ins]
    if n_pos:
        nblk = seq // tm
        in_specs += [pl.BlockSpec((tm, a.shape[1]), lambda h, i: (i % nblk, 0)) for a in pos]
    in_specs += [pl.BlockSpec(a.shape, lambda h, i: (0, 0)) for a in consts]
    out_specs = [tiled(w, 0, ph) for _, _, w, ph in outs] + [pl.BlockSpec(s, lambda h, i: (0, 0)) for s in accs]
    out_shape = [jax.ShapeDtypeStruct((t, c), d) for c, d, _, _ in outs] + [jax.ShapeDtypeStruct(s, F32) for s in accs]
    sem = ("arbitrary", "arbitrary") if n_acc else ("parallel", "parallel")
    res = pl.pallas_call(body, out_shape=out_shape, grid=(heads, t // tm), in_specs=in_specs, out_specs=out_specs,
                         compiler_params=_params(sem), name=name)(*[a for a, _, _, _ in ins], *pos, *consts)
    return res[0] if len(res) == 1 else res


def _attn_fwd(name, q, k, kc0, v, vc0, *, heads, group, nseq, seq, tq=512):
    tq = min(tq, seq)
    nq = seq // tq

    def body(q_ref, k_ref, v_ref, o_ref):
        s = lax.dot_general(q_ref[...], k_ref[...].astype(BF16), (((1,), (1,)), ((), ())), preferred_element_type=F32)
        p = jnp.exp(s - jnp.max(s, axis=-1, keepdims=True))
        inv = 1.0 / jnp.sum(p, axis=-1, keepdims=True)
        o = jnp.dot(p.astype(BF16), v_ref[...].astype(BF16), preferred_element_type=F32)
        o_ref[...] = (o * inv).astype(o_ref.dtype)

    q_spec = pl.BlockSpec((tq, HEAD_W), lambda h, b, i: (b * nq + i, h))
    return pl.pallas_call(
        body, out_shape=jax.ShapeDtypeStruct(q.shape, BF16), grid=(heads, nseq, nq),
        in_specs=[q_spec, pl.BlockSpec((seq, HEAD_W), lambda h, b, i: (b, kc0 + h // group)),
                  pl.BlockSpec((seq, HEAD_W), lambda h, b, i: (b, vc0 + h // group))],
        out_specs=q_spec, compiler_params=_params(("parallel", "parallel", "parallel")), name=name)(q, k, v)


def _attn_bwd(name, q, k, kc0, v, vc0, o, do, *, heads, group, nseq, seq, tq=256):
    tq = min(tq, seq)
    nq = seq // tq
    hk = heads // group
    t = q.shape[0]

    def body(q_ref, k_ref, v_ref, o_ref, do_ref, dq_ref, dk_ref, dv_ref, dk_acc, dv_acc):
        g, i = pl.program_id(2), pl.program_id(3)
        qv, kv, vv, dov = q_ref[...], k_ref[...].astype(BF16), v_ref[...].astype(BF16), do_ref[...]
        s = lax.dot_general(qv, kv, (((1,), (1,)), ((), ())), preferred_element_type=F32)
        p = jnp.exp(s - jnp.max(s, axis=-1, keepdims=True))
        pn = p * (1.0 / jnp.sum(p, axis=-1, keepdims=True))
        dp = lax.dot_general(dov, vv, (((1,), (1,)), ((), ())), preferred_element_type=F32)
        delta = jnp.sum(dov.astype(F32) * o_ref[...].astype(F32), axis=-1, keepdims=True)
        ds = (pn * (dp - delta)).astype(BF16)
        dq_ref[...] = jnp.dot(ds, kv, preferred_element_type=F32)
        dk_part = lax.dot_general(ds, qv, (((0,), (0,)), ((), ())), preferred_element_type=F32)
        dv_part = lax.dot_general(pn.astype(BF16), dov, (((0,), (0,)), ((), ())), preferred_element_type=F32)
        first = jnp.logical_and(g == 0, i == 0)

        @pl.when(first)
        def _():
            dk_acc[...] = dk_part
            dv_acc[...] = dv_part

        @pl.when(jnp.logical_not(first))
        def _():
            dk_acc[...] += dk_part
            dv_acc[...] += dv_part

        @pl.when(jnp.logical_and(g == group - 1, i == nq - 1))
        def _():
            dk_ref[...] = dk_acc[...].astype(dk_ref.dtype)
            dv_ref[...] = dv_acc[...].astype(dv_ref.dtype)

    q_spec = pl.BlockSpec((tq, HEAD_W), lambda kh, b, g, i: (b * nq + i, kh * group + g))
    kv_out = pl.BlockSpec((seq, HEAD_W), lambda kh, b, g, i: (b, kh))
    return pl.pallas_call(
        body,
        out_shape=[jax.ShapeDtypeStruct(q.shape, F32), jax.ShapeDtypeStruct((t, hk * HEAD_W), BF16),
                   jax.ShapeDtypeStruct((t, hk * HEAD_W), BF16)],
        grid=(hk, nseq, group, nq),
        in_specs=[q_spec, pl.BlockSpec((seq, HEAD_W), lambda kh, b, g, i: (b, kc0 + kh)),
                  pl.BlockSpec((seq, HEAD_W), lambda kh, b, g, i: (b, vc0 + kh)), q_spec, q_spec],
        out_specs=[q_spec, kv_out, kv_out],
        scratch_shapes=[pltpu.VMEM((seq, HEAD_W), F32), pltpu.VMEM((seq, HEAD_W), F32)],
        compiler_params=_params(("parallel", "parallel", "arbitrary", "arbitrary")), name=name)(q, k, v, o, do)


def _place():
    return lax.axis_index("x"), lax.axis_index("y"), lax.axis_index("c")


def _other_chips(x, y):
    return [(1 - x, y), (x, 1 - y), (1 - x, 1 - y)]


def _chip_exchange(name, src, gather):
    rows, w = src.shape[-2:]

    def body(src_ref, out_ref, send_sems, recv_sems, local_sem):
        x, y, c = _place()
        me = 2 * x + y
        mine = src_ref if gather else src_ref.at[me]
        local = pltpu.make_async_copy(mine, out_ref.at[me], local_sem)
        local.start()

        def copy(k, px, py):
            return pltpu.make_async_remote_copy(
                src_ref=src_ref if gather else src_ref.at[2 * px + py], dst_ref=out_ref.at[me],
                send_sem=send_sems.at[k], recv_sem=recv_sems.at[k], device_id=(px, py, c), device_id_type=MESH)

        sent = [copy(k, px, py) for k, (px, py) in enumerate(_other_chips(x, y))]
        for cp in sent:
            cp.start()
        for k, (px, py) in enumerate(_other_chips(x, y)):
            pltpu.make_async_remote_copy(
                src_ref=mine, dst_ref=out_ref.at[2 * px + py], send_sem=send_sems.at[k], recv_sem=recv_sems.at[k],
                device_id=(px, py, c), device_id_type=MESH).wait_recv()
        for cp in sent:
            cp.wait_send()
        local.wait()

    return pl.pallas_call(
        body, out_shape=jax.ShapeDtypeStruct((4, rows, w), src.dtype),
        in_specs=[pl.BlockSpec(memory_space=pl.ANY)], out_specs=pl.BlockSpec(memory_space=pl.ANY),
        scratch_shapes=[pltpu.SemaphoreType.DMA((3,)), pltpu.SemaphoreType.DMA((3,)), pltpu.SemaphoreType.DMA(())],
        name=name)(src)


def _pair_exchange(name, src):
    rows, w = src.shape

    def body(src_ref, out_ref, send_sem, recv_sem, local_sem):
        x, y, c = _place()
        local = pltpu.make_async_copy(src_ref, out_ref.at[c], local_sem)
        local.start()
        send = pltpu.make_async_remote_copy(src_ref=src_ref, dst_ref=out_ref.at[c], send_sem=send_sem, recv_sem=recv_sem,
                                            device_id=(x, y, 1 - c), device_id_type=MESH)
        send.start()
        pltpu.make_async_remote_copy(src_ref=src_ref, dst_ref=out_ref.at[1 - c], send_sem=send_sem, recv_sem=recv_sem,
                                     device_id=(x, y, 1 - c), device_id_type=MESH).wait_recv()
        send.wait_send()
        local.wait()

    return pl.pallas_call(
        body, out_shape=jax.ShapeDtypeStruct((2, rows, w), src.dtype),
        in_specs=[pl.BlockSpec(memory_space=pl.ANY)], out_specs=pl.BlockSpec(memory_space=pl.ANY),
        scratch_shapes=[pltpu.SemaphoreType.DMA(()), pltpu.SemaphoreType.DMA(()), pltpu.SemaphoreType.DMA(())],
        name=name)(src)


def _adamw(w, g, m, v):
    m = ADAM_B1 * m + (1.0 - ADAM_B1) * g
    v = ADAM_B2 * v + (1.0 - ADAM_B2) * (g * g)
    delta = -ADAM_LR * ((m / M_HAT_DIV) / (jnp.sqrt(v / V_HAT_DIV) + ADAM_EPS) + ADAM_WD * w)
    return delta, m, v


def _small_allreduce_adamw(part, w, m, v):
    def body(part_ref, w_ref, m_ref, v_ref, g_out, d_out, m_out, v_out, loss_out, buf, send_sems, recv_sems):
        x, y, c = _place()
        me = 4 * x + 2 * y + c
        buf[me] = part_ref[...]

        def flip(k):
            fx, fy, fc = (k >> 2) & 1, (k >> 1) & 1, k & 1
            px, py, pc = (1 - x if fx else x), (1 - y if fy else y), (1 - c if fc else c)
            return (px, py, pc), 4 * px + 2 * py + pc

        def copy(k, slot):
            return pltpu.make_async_remote_copy(
                src_ref=part_ref, dst_ref=buf.at[slot], send_sem=send_sems.at[k - 1], recv_sem=recv_sems.at[k - 1],
                device_id=flip(k)[0], device_id_type=MESH)

        sent = [copy(k, me) for k in range(1, 8)]
        for cp in sent:
            cp.start()
        for k in range(1, 8):
            copy(k, flip(k)[1]).wait_recv()
        for cp in sent:
            cp.wait_send()
        tot = buf[0]
        for j in range(1, 8):
            tot = tot + buf[j]
        delta, m_new, v_new = _adamw(w_ref[...], tot, m_ref[...], v_ref[...])
        g_out[...] = tot
        d_out[...] = delta
        m_out[...] = m_new
        v_out[...] = v_new
        loss_out[...] = jnp.sum(tot[LOSS_ROW0:LOSS_ROW0 + 8, :]).reshape(1, 1)

    vm = pl.BlockSpec(memory_space=pltpu.VMEM)
    shp = jax.ShapeDtypeStruct((SMALL_ROWS, 128), F32)
    return pl.pallas_call(
        body, out_shape=[shp, shp, shp, shp, jax.ShapeDtypeStruct((1, 1), F32)],
        in_specs=[vm, vm, vm, vm], out_specs=[vm, vm, vm, vm, vm],
        scratch_shapes=[pltpu.VMEM((8, SMALL_ROWS, 128), F32), pltpu.SemaphoreType.DMA((7,)), pltpu.SemaphoreType.DMA((7,))],
        name="small_allreduce_adamw")(part, w, m, v)


def _sum4(parts):
    _, rows, w = parts.shape

    def body(p_ref, o_ref):
        o_ref[...] = ((p_ref[0].astype(F32) + p_ref[1].astype(F32)) + p_ref[2].astype(F32)) + p_ref[3].astype(F32)

    return pl.pallas_call(
        body, out_shape=jax.ShapeDtypeStruct((rows, w), F32), grid=(rows // PACK_TILE,),
        in_specs=[pl.BlockSpec((4, PACK_TILE, w), lambda i: (0, i, 0))], out_specs=pl.BlockSpec((PACK_TILE, w), lambda i: (i, 0)),
        compiler_params=_params(("parallel",)), name="sum4")(parts)


def _adamw_big(pair, w, m, v):
    rows, wd = w.shape

    def body(p_ref, w_ref, m_ref, v_ref, g_out, d_out, m_out, v_out):
        g = p_ref[0] + p_ref[1]
        delta, m_new, v_new = _adamw(w_ref[...], g, m_ref[...], v_ref[...])
        g_out[...] = g
        d_out[...] = delta
        m_out[...] = m_new
        v_out[...] = v_new

    t_spec = pl.BlockSpec((PACK_TILE // 2, wd), lambda i: (i, 0))
    shp = jax.ShapeDtypeStruct((rows, wd), F32)
    return pl.pallas_call(
        body, out_shape=[shp] * 4, grid=(rows // (PACK_TILE // 2),),
        in_specs=[pl.BlockSpec((2, PACK_TILE // 2, wd), lambda i: (0, i, 0)), t_spec, t_spec, t_spec], out_specs=[t_spec] * 4,
        compiler_params=_params(("parallel",)), name="adamw_big")(pair, w, m, v)


def _pack_shards(shards, dtype):
    flat = jnp.concatenate([s.astype(dtype).reshape(-1, PACK_W) for s in shards], axis=0)
    return jnp.pad(flat, ((0, PACK_ROWS_PAD - PACK_ROWS), (0, 0)))


def _unpack_shards(slab):
    out, off = [], 0
    for _, r, c, ax in BIG:
        rs, cs = (r, c // 4) if ax == 1 else (r // 4, c)
        n = rs * cs // PACK_W
        out.append(slab[off:off + n].reshape(rs, cs))
        off += n
    return out


def _unpack_full(slabs):
    out, off = [], 0
    for _, r, c, ax in BIG:
        n = r * c // 4 // PACK_W
        seg = slabs[:, off:off + n]
        out.append(seg.reshape(4, r, c // 4).transpose(1, 0, 2).reshape(r, c) if ax == 1 else seg.reshape(r, c))
        off += n
    return out


def _pack_full(mats, dtype):
    segs = []
    for (_, r, c, ax), a in zip(BIG, mats):
        a = a.astype(dtype)
        a = a.reshape(r, 4, c // 4).transpose(1, 0, 2) if ax == 1 else a
        segs.append(a.reshape(4, -1, PACK_W))
    return jnp.pad(jnp.concatenate(segs, axis=1), ((0, 0), (0, PACK_ROWS_PAD - PACK_ROWS), (0, 0)))


def _pad_heads_cols(wm, heads, d):
    k = wm.shape[0]
    return jnp.pad(wm.reshape(k, heads, d), ((0, 0), (0, 0), (0, HEAD_W - d))).reshape(k, heads * HEAD_W)


def _unpad_heads_cols(wm, heads, d):
    k = wm.shape[0]
    return wm.reshape(k, heads, HEAD_W)[:, :, :d].reshape(k, heads * d)


def _win_ext(w_in):
    o = np.cumsum([0, Q_LORA, KV_LORA, QK_ROPE, H_B * HD_B, KV_B * HD_B, KV_B * HD_B, D_MODEL, D_MODEL])
    pc = lambda a, n: jnp.pad(a, ((0, 0), (0, n - a.shape[1])))
    return jnp.concatenate([
        w_in[:, o[0]:o[1]], w_in[:, o[1]:o[2]], pc(w_in[:, o[2]:o[3]], HEAD_W),
        _pad_heads_cols(w_in[:, o[3]:o[4]], H_B, HD_B), _pad_heads_cols(w_in[:, o[4]:o[5]], KV_B, HD_B),
        _pad_heads_cols(w_in[:, o[5]:o[6]], KV_B, HD_B), w_in[:, o[6]:o[7]], w_in[:, o[7]:o[8]]], axis=1)


def _win_unext(we):
    c = HEAD_W
    return jnp.concatenate([
        we[:, :ZC_CKV * c], we[:, ZC_CKV * c:ZC_KPE * c], we[:, ZC_KPE * c:ZC_KPE * c + QK_ROPE],
        _unpad_heads_cols(we[:, ZC_QB * c:ZC_KB * c], H_B, HD_B), _unpad_heads_cols(we[:, ZC_KB * c:ZC_VB * c], KV_B, HD_B),
        _unpad_heads_cols(we[:, ZC_VB * c:ZC_GA * c], KV_B, HD_B), we[:, ZC_GA * c:]], axis=1)


def _wkv_ext(w_kvb):
    wk = w_kvb.reshape(KV_LORA, H_A, QK_NOPE + V_DIM_A)
    k_cols = jnp.pad(wk[:, :, :QK_NOPE], ((0, 0), (0, 0), (0, HEAD_W - QK_NOPE))).reshape(KV_LORA, H_A * HEAD_W)
    v_cols = jnp.pad(wk[:, :, QK_NOPE:], ((0, 0), (0, 0), (0, HEAD_W - V_DIM_A))).reshape(KV_LORA, H_A * HEAD_W)
    eye = jnp.pad(jnp.eye(QK_ROPE, dtype=w_kvb.dtype), ((0, 0), (QK_NOPE, HEAD_W - QK_NOPE - QK_ROPE)))
    pe_rows = jnp.concatenate([jnp.tile(eye, (1, H_A)), jnp.zeros((QK_ROPE, H_A * HEAD_W), w_kvb.dtype)], axis=1)
    top = jnp.concatenate([k_cols, v_cols], axis=1)
    return jnp.concatenate([top, pe_rows, jnp.zeros((2 * HEAD_W - KV_LORA - QK_ROPE, 2 * H_A * HEAD_W), w_kvb.dtype)], axis=0)


def _wkv_unext(we):
    k_cols = we[:KV_LORA, :H_A * HEAD_W].reshape(KV_LORA, H_A, HEAD_W)[:, :, :QK_NOPE]
    v_cols = we[:KV_LORA, H_A * HEAD_W:].reshape(KV_LORA, H_A, HEAD_W)[:, :, :V_DIM_A]
    return jnp.concatenate([k_cols, v_cols], axis=2).reshape(KV_LORA, H_A * (QK_NOPE + V_DIM_A))


def _pad_heads_rows(wm, heads, d):
    n = wm.shape[1]
    return jnp.pad(wm.reshape(heads, d, n), ((0, 0), (0, HEAD_W - d), (0, 0))).reshape(heads * HEAD_W, n)


def _unpad_heads_rows(wm, heads, d):
    n = wm.shape[1]
    return wm.reshape(heads, HEAD_W, n)[:, :d].reshape(heads * d, n)


def _rope_tables(seq):
    def ang(pos, dim):
        inv = ROPE_THETA ** (-jnp.arange(0, dim, 2, dtype=F32) / dim)
        return pos.astype(F32)[:, None] * inv[None, :]

    def rot(dim):
        r = np.zeros((dim, dim), np.float32)
        half = dim // 2
        r[np.arange(half) + half, np.arange(half)] = -1.0
        r[np.arange(half), np.arange(half) + half] = 1.0
        return r

    def table(blocks):
        first, width = blocks[0][0], sum(2 * a.shape[1] for _, a in blocks)
        ones = lambda n: jnp.ones((seq, n), F32)
        cos = jnp.concatenate([ones(first)] + [jnp.cos(a) for _, a in blocks for _ in (0, 1)] + [ones(HEAD_W - first - width)], axis=1)
        sin = jnp.concatenate([0.0 * ones(first)] + [jnp.sin(a) for _, a in blocks for _ in (0, 1)]
                              + [0.0 * ones(HEAD_W - first - width)], axis=1)
        pm = np.zeros((HEAD_W, HEAD_W), np.float32)
        for c0, a in blocks:
            d = 2 * a.shape[1]
            pm[c0:c0 + d, c0:c0 + d] = rot(d)
        return cos, sin, jnp.asarray(pm, BF16), jnp.asarray(pm.T, BF16)

    tok = jnp.arange(seq)
    a1 = ang(tok, QK_ROPE)
    arow, acol = ang(tok // GRID_W, HD_B // 2), ang(tok % GRID_W, HD_B // 2)
    return table([(QK_NOPE, a1)]), table([(0, a1)]), table([(0, arow), (HD_B // 2, acol)])


def _local_step(x, p, tgt, gains, wts):
    nb, seq, _ = x.shape
    t = nb * seq
    x0 = x.reshape(t, D_MODEL)
    p2 = p.reshape(t, PLE_DIM)
    tg = tgt.reshape(t, D_MODEL)
    (cq_t, sq_t, pq, pq_t), (ck_t, sk_t, pk, pk_t), (cb_t, sb_t, pb, pb_t) = _rope_tables(seq)
    padg = lambda g: jnp.pad(g, ((0, 0), (0, HEAD_W - g.shape[1])))
    g_qn, g_kn = padg(gains["g_qn"]), padg(gains["g_kn"])

    win = _win_ext(wts["w_in"])
    wqb = _pad_heads_cols(wts["w_qb"], H_A, QK_NOPE + QK_ROPE)
    wkv = _wkv_ext(wts["w_kvb"])
    woa = _pad_heads_rows(wts["w_oa"], H_A, V_DIM_A)
    wob = _pad_heads_rows(wts["w_ob"], H_B, HD_B)
    wo, wup, wdown, wpg, wple = wts["w_o"], wts["w_up"], wts["w_down"], wts["w_ple_gate"], wts["w_ple"]

    norm = lambda n: (lambda v, g: (_rms(v, g, n),))
    full = lambda a: (a, a.shape[1], 0, False)

    h = _rowwise("norm_mix", norm(D_MODEL), [full(x0)], [(D_MODEL, BF16, D_MODEL, False)], consts=[gains["g_mix"]])
    z = _mm("in_proj", h, win, tn=2048)
    cq = _rowwise("norm_qa", norm(Q_LORA), [(z, Q_LORA, 0, False)], [(Q_LORA, BF16, Q_LORA, False)], consts=[gains["g_qa"]])
    qa = _mm("q_up", cq, wqb)

    def rope_fwd(scale):
        return lambda v, cos, sin, pm: ((v * cos + _perm(v, pm) * sin) * scale,)

    q_a = _rowwise("rope_qa", rope_fwd(SCALE_A), [(qa, HEAD_W, 0, True)], [(H_A * HEAD_W, BF16, HEAD_W, True)],
                   pos=[cq_t, sq_t], consts=[pq], heads=H_A, seq=seq)
    ckv = _rowwise("norm_kva", norm(KV_LORA), [(z, HEAD_W, ZC_CKV, False)], [(HEAD_W, BF16, HEAD_W, False)], consts=[gains["g_kva"]])
    kpe = _rowwise("rope_kpe", rope_fwd(1.0), [(z, HEAD_W, ZC_KPE, False)], [(HEAD_W, BF16, HEAD_W, False)],
                   pos=[ck_t, sk_t], consts=[pk], seq=seq)
    kin = jnp.concatenate([ckv, kpe], axis=1)
    kv_a = _mm("kv_up", kin, wkv, out_dtypes=(BF16,))
    o_a = _attn_fwd("attn_a_fwd", q_a, kv_a, 0, kv_a, H_A, heads=H_A, group=1, nseq=nb, seq=seq)
    ya = _mm("out_a", o_a, woa)

    def prep_fwd(scale):
        def fn(v, cos, sin, g, pm):
            yv = _rms(v, g, HD_B)
            return ((yv * cos + _perm(yv, pm) * sin) * scale,)
        return fn

    q_b = _rowwise("prep_qb", prep_fwd(SCALE_B), [(z, HEAD_W, ZC_QB, True)], [(H_B * HEAD_W, BF16, HEAD_W, True)],
                   pos=[cb_t, sb_t], consts=[g_qn, pb], heads=H_B, seq=seq)
    k_b = _rowwise("prep_kb", prep_fwd(1.0), [(z, HEAD_W, ZC_KB, True)], [(KV_B * HEAD_W, BF16, HEAD_W, True)],
                   pos=[cb_t, sb_t], consts=[g_kn, pb], heads=KV_B, seq=seq)
    o_b = _attn_fwd("attn_b_fwd", q_b, k_b, 0, z, ZC_VB, heads=H_B, group=H_B // KV_B, nseq=nb, seq=seq)
    yb = _mm("out_b", o_b, wob)

    z_ga, z_gb = (z, D_MODEL, ZC_GA // 8, False), (z, D_MODEL, ZC_GB // 8, False)
    merged = _rowwise("merge", lambda ga, gb, a, b: (_sigmoid(ga) * a + _sigmoid(gb) * b,),
                      [z_ga, z_gb, full(ya), full(yb)], [(D_MODEL, BF16, D_MODEL, False)])
    x1 = _mm("out_proj", merged, wo, epi=lambda acc, r: (r + acc,), extras=(x0,))
    h2 = _rowwise("norm_mlp", norm(D_MODEL), [full(x1)], [(D_MODEL, BF16, D_MODEL, False)], consts=[gains["g_mlp"]])

    def relu2(acc):
        u = jnp.maximum(acc, 0.0)
        return u, u * u

    u, usq = _mm("mlp_up", h2, wup, out_dtypes=(BF16, BF16), epi=relu2, tn=2048)
    x2 = _mm("mlp_down", usq, wdown, epi=lambda acc, r: (r + acc,), extras=(x1,))
    h3 = _rowwise("norm_ple", norm(D_MODEL), [full(x2)], [(D_MODEL, BF16, D_MODEL, False)], consts=[gains["g_ple"]])
    gpre = _mm("ple_gate", h3, wpg)
    pe = _mm("ple_proj", p2, wple)

    def tail(x2v, gp, pev, tv, gf):
        sg = _sigmoid(gp)
        x3 = x2v + sg * pev
        rs = lax.rsqrt(jnp.sum(x3 * x3, axis=-1, keepdims=True) * (1.0 / D_MODEL) + EPS)
        xh = x3 * rs
        err = xh * gf - tv
        dy = err * (1.0 / D_MODEL)
        dyg = dy * gf
        dx3 = rs * (dyg - xh * (jnp.sum(dyg * xh, axis=-1, keepdims=True) * (1.0 / D_MODEL)))
        return (dx3, dx3 * pev * sg * (1.0 - sg), dx3 * sg,
                jnp.sum(err * err, axis=0, keepdims=True) * (0.5 / D_MODEL), jnp.sum(dy * xh, axis=0, keepdims=True))

    dx3, dgpre, dpe, loss_part, dg_final = _rowwise(
        "tail", tail, [full(x2), full(gpre), full(pe), full(tg)],
        [(D_MODEL, F32, D_MODEL, False), (D_MODEL, BF16, D_MODEL, False), (D_MODEL, BF16, D_MODEL, False)],
        consts=[gains["g_final"].reshape(1, D_MODEL)], accs=[(1, D_MODEL), (1, D_MODEL)], tm=256)

    def norm_bwd(n, with_res):
        if with_res:
            def fn(dh, v, res, g):
                dx, dg = _rms_bwd(dh, v, g, n)
                return dx + res, dg
        else:
            def fn(dh, v, g):
                return _rms_bwd(dh, v, g, n)
        return fn

    dw = {}
    dw["w_ple"] = _mm_tn("dw_ple", p2, dpe)
    dw["w_ple_gate"] = _mm_tn("dw_ple_gate", h3, dgpre)
    dh3 = _mm("d_ple_gate", dgpre, wpg, trans_b=True)
    dx2, dg_ple = _rowwise("norm_ple_bwd", norm_bwd(D_MODEL, True), [full(dh3), full(x2), full(dx3)],
                           [(D_MODEL, F32, D_MODEL, False)], consts=[gains["g_ple"]], accs=[(1, D_MODEL)], tm=256)
    dw["w_down"] = _mm_tn("dw_down", usq, dx2)
    dupre = _mm("d_mlp_down", dx2, wdown, trans_b=True, out_dtypes=(BF16,), epi=lambda acc, uv: (acc * (2.0 * uv.astype(F32)),),
                extras=(u,), tn=2048)
    dw["w_up"] = _mm_tn("dw_up", h2, dupre)
    dh2 = _mm("d_mlp_up", dupre, wup, trans_b=True)
    dx1, dg_mlp = _rowwise("norm_mlp_bwd", norm_bwd(D_MODEL, True), [full(dh2), full(x1), full(dx2)],
                           [(D_MODEL, F32, D_MODEL, False)], consts=[gains["g_mlp"]], accs=[(1, D_MODEL)], tm=256)
    dw["w_o"] = _mm_tn("dw_o", merged, dx1)
    dmerged = _mm("d_out_proj", dx1, wo, trans_b=True)

    def merge_bwd(dm, ga, gb, a, b):
        sa, sb = _sigmoid(ga), _sigmoid(gb)
        return dm * sa, dm * sb, dm * a * sa * (1.0 - sa), dm * b * sb * (1.0 - sb)

    dya, dyb, dga, dgb = _rowwise("merge_bwd", merge_bwd, [full(dmerged), z_ga, z_gb, full(ya), full(yb)],
                                  [(D_MODEL, BF16, D_MODEL, False)] * 4, tm=256)
    dw["w_oa"] = _unpad_heads_rows(_mm_tn("dw_oa", o_a, dya), H_A, V_DIM_A)
    dw["w_ob"] = _unpad_heads_rows(_mm_tn("dw_ob", o_b, dyb), H_B, HD_B)
    do_a = _mm("d_out_a", dya, woa, trans_b=True, out_dtypes=(BF16,))
    do_b = _mm("d_out_b", dyb, wob, trans_b=True, out_dtypes=(BF16,))
    dq_a, dk_a, dv_a = _attn_bwd("attn_a_bwd", q_a, kv_a, 0, kv_a, H_A, o_a, do_a, heads=H_A, group=1, nseq=nb, seq=seq)
    dq_b, dk_b, dv_b = _attn_bwd("attn_b_bwd", q_b, k_b, 0, z, ZC_VB, o_b, do_b, heads=H_B, group=H_B // KV_B, nseq=nb, seq=seq)

    def rope_bwd(scale):
        return lambda d, cos, sin, pm_t: ((d * cos + _perm(d * sin, pm_t)) * scale,)

    dqa = _rowwise("rope_qa_bwd", rope_bwd(SCALE_A), [(dq_a, HEAD_W, 0, True)], [(H_A * HEAD_W, BF16, HEAD_W, True)],
                   pos=[cq_t, sq_t], consts=[pq_t], heads=H_A, seq=seq)
    dw["w_qb"] = _unpad_heads_cols(_mm_tn("dw_qb", cq, dqa), H_A, QK_NOPE + QK_ROPE)
    dcq = _mm("d_q_up", dqa, wqb, trans_b=True)
    dq_lat, dg_qa = _rowwise("norm_qa_bwd", norm_bwd(Q_LORA, False), [full(dcq), (z, Q_LORA, 0, False)],
                             [(Q_LORA, BF16, Q_LORA, False)], consts=[gains["g_qa"]], accs=[(1, Q_LORA)])
    dkv_a = jnp.concatenate([dk_a, dv_a], axis=1)
    dw["w_kvb"] = _wkv_unext(_mm_tn("dw_kv", kin, dkv_a))
    dkin = _mm("d_kv_up", dkv_a, wkv, trans_b=True)
    dckv, dg_kva = _rowwise("norm_kva_bwd", norm_bwd(KV_LORA, False), [(dkin, HEAD_W, 0, False), (z, HEAD_W, ZC_CKV, False)],
                            [(HEAD_W, BF16, HEAD_W, False)], consts=[gains["g_kva"]], accs=[(1, KV_LORA)])
    dkpe = _rowwise("rope_kpe_bwd", rope_bwd(1.0), [(dkin, HEAD_W, 1, False)], [(HEAD_W, BF16, HEAD_W, False)],
                    pos=[ck_t, sk_t], consts=[pk_t], seq=seq)

    def prep_bwd(scale):
        def fn(d, v, cos, sin, g, pm_t):
            dyv = (d * cos + _perm(d * sin, pm_t)) * scale
            return _rms_bwd(dyv, v, g, HD_B)
        return fn

    dqb, dg_qn = _rowwise("prep_qb_bwd", prep_bwd(SCALE_B), [(dq_b, HEAD_W, 0, True), (z, HEAD_W, ZC_QB, True)],
                          [(H_B * HEAD_W, BF16, HEAD_W, True)], pos=[cb_t, sb_t], consts=[g_qn, pb_t], accs=[(1, HEAD_W)],
                          heads=H_B, seq=seq)
    dkb, dg_kn = _rowwise("prep_kb_bwd", prep_bwd(1.0), [(dk_b, HEAD_W, 0, True), (z, HEAD_W, ZC_KB, True)],
                          [(KV_B * HEAD_W, BF16, HEAD_W, True)], pos=[cb_t, sb_t], consts=[g_kn, pb_t], accs=[(1, HEAD_W)],
                          heads=KV_B, seq=seq)

    dz = jnp.concatenate([dq_lat, dckv, dkpe, dqb, dkb, dv_b, dga, dgb], axis=1)
    dw["w_in"] = _win_unext(_mm_tn("dw_in", h, dz))
    dh = _mm("d_in_proj", dz, win, trans_b=True)
    dx0, dg_mix = _rowwise("norm_mix_bwd", norm_bwd(D_MODEL, True), [full(dh), full(x0), full(dx1)],
                           [(D_MODEL, F32, D_MODEL, False)], consts=[gains["g_mix"]], accs=[(1, D_MODEL)], tm=256)

    dg = {"g_mix": dg_mix, "g_qa": dg_qa, "g_kva": dg_kva, "g_qn": dg_qn[:, :HD_B], "g_kn": dg_kn[:, :HD_B],
          "g_mlp": dg_mlp, "g_ple": dg_ple, "g_final": dg_final}
    return loss_part, dx0.reshape(nb, seq, D_MODEL), dg, dw


def _pack_small(vals, loss_part=None):
    flat = jnp.concatenate([vals[n].reshape(1, -1) for n, _ in SMALL], axis=1)
    loss = jnp.zeros((1, 8 * 128), F32) if loss_part is None else loss_part
    gap = jnp.zeros((1, LOSS_ROW0 * 128 - SMALL_N), F32)
    return jnp.concatenate([flat, gap, loss], axis=1).reshape(SMALL_ROWS, 128)


def _unpack_small(slab, like):
    flat, out, off = slab.reshape(-1), {}, 0
    for n, k in SMALL:
        out[n] = flat[off:off + k].reshape(like[n].shape)
        off += k
    return out


def kernel(x, p, g_mix, w_in, g_qa, w_qb, g_kva, w_kvb, g_qn, g_kn, w_oa, w_ob, w_o, g_mlp, w_up, w_down, g_ple, w_ple_gate, w_ple, g_final, loss_target, m_g_mix, m_w_in, m_g_qa, m_w_qb, m_g_kva, m_w_kvb, m_g_qn, m_g_kn, m_w_oa, m_w_ob, m_w_o, m_g_mlp, m_w_up, m_w_down, m_g_ple, m_w_ple_gate, m_w_ple, m_g_final, v_g_mix, v_w_in, v_g_qa, v_w_qb, v_g_kva, v_w_kvb, v_g_qn, v_g_kn, v_w_oa, v_w_ob, v_w_o, v_g_mlp, v_w_up, v_w_down, v_g_ple, v_w_ple_gate, v_w_ple, v_g_final):
    given = dict(locals())
    order = ["g_mix", "w_in", "g_qa", "w_qb", "g_kva", "w_kvb", "g_qn", "g_kn", "w_oa", "w_ob", "w_o", "g_mlp", "w_up",
             "w_down", "g_ple", "w_ple_gate", "w_ple", "g_final"]
    big_names = [n for n, _, _, _ in BIG]
    shard = lambda prefix: [given[prefix + n][0] for n in big_names]

    gathered = _chip_exchange("weight_all_gather", _pack_shards(shard(""), BF16), gather=True)
    wts = dict(zip(big_names, _unpack_full(gathered)))
    gains = {n: given[n].reshape(1, -1) for n, _ in SMALL}

    loss_part, grad_x, dg, dw = _local_step(x, p[0], loss_target, gains, wts)

    small = lambda prefix: _pack_small({n: given[prefix + n] for n, _ in SMALL})
    g_s, d_s, m_s, v_s, loss = _small_allreduce_adamw(_pack_small(dg, loss_part), small(""), small("m_"), small("v_"))

    parts = _chip_exchange("grad_reduce_scatter", _pack_full([dw[n] for n in big_names], BF16), gather=False)
    pair = _pair_exchange("grad_pair_exchange", _sum4(parts))
    g_b, d_b, m_b, v_b = _adamw_big(pair, _pack_shards(shard(""), F32), _pack_shards(shard("m_"), F32), _pack_shards(shard("v_"), F32))

    res = {}
    for key, s_slab, b_slab in (("grad_", g_s, g_b), ("delta_", d_s, d_b), ("new_m_", m_s, m_b), ("new_v_", v_s, v_b)):
        sm = _unpack_small(s_slab, given)
        bg = dict(zip(big_names, _unpack_shards(b_slab)))
        for n in order:
            res[key + n] = sm[n] if n in sm else bg[n][None]
    outs = [loss.reshape(()), grad_x]
    for key in ("grad_", "delta_", "new_m_", "new_v_"):
        outs += [res[key + n] for n in order]
    return tuple(outs)
```

```python
import functools

import numpy as np
import jax
import jax.numpy as jnp
from jax import lax
from jax.experimental import pallas as pl
from jax.experimental.pallas import tpu as pltpu

F32 = jnp.float32
BF16 = jnp.bfloat16
MESH = pl.DeviceIdType.MESH

D_MODEL = 1024
GRID_W = 64
ROPE_THETA = 10000.0
EPS = 1e-6
H_A, QK_NOPE, QK_ROPE, V_DIM_A, Q_LORA, KV_LORA = 8, 64, 32, 64, 256, 128
H_B, KV_B, HD_B = 8, 2, 64
D_FF = 4096
PLE_DIM = 256
HEAD_W = 128
SCALE_A = (QK_NOPE + QK_ROPE) ** -0.5
SCALE_B = HD_B ** -0.5

ADAM_LR, ADAM_B1, ADAM_B2, ADAM_EPS, ADAM_WD, ADAM_STEP = 0.001, 0.9, 0.999, 1e-08, 0.01, 10
M_HAT_DIV = 1.0 - ADAM_B1 ** ADAM_STEP
V_HAT_DIV = 1.0 - ADAM_B2 ** ADAM_STEP

VMEM_LIMIT_BYTES = 56 * 1024 * 1024

ZC_QLAT, ZC_CKV, ZC_KPE, ZC_QB, ZC_KB, ZC_VB, ZC_GA, ZC_GB = 0, 2, 3, 4, 12, 14, 16, 24
Z_WIDTH = 32 * HEAD_W

BIG = [
    ("w_in", 1024, 3232, 1), ("w_qb", 256, 768, 1), ("w_kvb", 128, 1024, 1), ("w_oa", 512, 1024, 1),
    ("w_ob", 512, 1024, 1), ("w_o", 1024, 1024, 0), ("w_up", 1024, 4096, 1), ("w_down", 4096, 1024, 0),
    ("w_ple_gate", 1024, 1024, 0), ("w_ple", 256, 1024, 1),
]
BIG_BY_NAME = {e[0]: e for e in BIG}
PACK_W = 1024
PACK_ALIGN = 128
EARLY = ["w_in", "w_qb", "w_kvb"]
LATE_A = ["w_oa", "w_ob", "w_o", "w_up"]
LATE_B = ["w_down", "w_ple_gate", "w_ple"]
PAIR_CHUNKS = 8

SMALL = [("g_mix", 1024), ("g_qa", 256), ("g_kva", 128), ("g_qn", 64), ("g_kn", 64), ("g_mlp", 1024),
         ("g_ple", 1024), ("g_final", 1024)]
SMALL_N = sum(n for _, n in SMALL)
LOSS_ROW0 = 40
SMALL_ROWS = 48


def _params(sem):
    return pltpu.CompilerParams(dimension_semantics=sem, vmem_limit_bytes=VMEM_LIMIT_BYTES)


def _sigmoid(v):
    return 1.0 / (1.0 + jnp.exp(-v))


def _perm(v, p_ref):
    pm = p_ref[...]
    hi = v.astype(BF16)
    lo = (v - hi.astype(F32)).astype(BF16)
    return (jnp.dot(hi, pm, preferred_element_type=F32) + jnp.dot(lo, pm, preferred_element_type=F32))


def _rms(v, g, n):
    rs = lax.rsqrt(jnp.sum(v * v, axis=-1, keepdims=True) * (1.0 / n) + EPS)
    return v * rs * g


def _rms_bwd(dy, v, g, n):
    rs = lax.rsqrt(jnp.sum(v * v, axis=-1, keepdims=True) * (1.0 / n) + EPS)
    vh = v * rs
    dyg = dy * g
    dx = rs * (dyg - vh * (jnp.sum(dyg * vh, axis=-1, keepdims=True) * (1.0 / n)))
    return dx, jnp.sum(dy * vh, axis=0, keepdims=True)


def _mm(name, a, b, *, trans_b=False, out_dtypes=(F32,), epi=None, extras=(), tm=512, tn=None):
    m, k = a.shape
    n = b.shape[0] if trans_b else b.shape[1]
    tn = n if tn is None else min(tn, n)
    tm = min(tm, m)
    assert m % tm == 0 and n % tn == 0 and (b.shape[1] if trans_b else b.shape[0]) == k
    n_ex = len(extras)
    dims = (((1,), (1,)), ((), ())) if trans_b else (((1,), (0,)), ((), ()))

    def body(a_ref, b_ref, *rest):
        acc = lax.dot_general(a_ref[...].astype(BF16), b_ref[...].astype(BF16), dims, preferred_element_type=F32)
        res = (acc,) if epi is None else epi(acc, *[e[...] for e in rest[:n_ex]])
        for o_ref, r in zip(rest[n_ex:], res):
            o_ref[...] = r.astype(o_ref.dtype)

    a_spec = pl.BlockSpec((tm, k), lambda j, i: (i, 0))
    b_spec = pl.BlockSpec((tn, k), lambda j, i: (j, 0)) if trans_b else pl.BlockSpec((k, tn), lambda j, i: (0, j))
    t_spec = pl.BlockSpec((tm, tn), lambda j, i: (i, j))
    outs = pl.pallas_call(
        body, out_shape=[jax.ShapeDtypeStruct((m, n), d) for d in out_dtypes], grid=(n // tn, m // tm),
        in_specs=[a_spec, b_spec] + [t_spec] * n_ex, out_specs=[t_spec] * len(out_dtypes),
        compiler_params=_params(("parallel", "parallel")), name=name)(a, b, *extras)
    return outs[0] if len(out_dtypes) == 1 else outs


def _mm_tn(name, a, b, *, tk=1024, tn=1024, tt=1024):
    t, k = a.shape
    n = b.shape[1]
    tk, tn, tt = min(tk, k), min(tn, n), min(tt, t)
    assert b.shape[0] == t and k % tk == 0 and n % tn == 0 and t % tt == 0

    def body(a_ref, b_ref, o_ref):
        part = lax.dot_general(a_ref[...].astype(BF16), b_ref[...].astype(BF16), (((0,), (0,)), ((), ())),
                               preferred_element_type=F32)

        @pl.when(pl.program_id(2) == 0)
        def _():
            o_ref[...] = part

        @pl.when(pl.program_id(2) != 0)
        def _():
            o_ref[...] += part

    return pl.pallas_call(
        body, out_shape=jax.ShapeDtypeStruct((k, n), F32), grid=(k // tk, n // tn, t // tt),
        in_specs=[pl.BlockSpec((tt, tk), lambda ki, ni, ti: (ti, ki)), pl.BlockSpec((tt, tn), lambda ki, ni, ti: (ti, ni))],
        out_specs=pl.BlockSpec((tk, tn), lambda ki, ni, ti: (ki, ni)),
        compiler_params=_params(("parallel", "parallel", "arbitrary")), name=name)(a, b)


def _rowwise(name, fn, ins, outs, *, consts=(), pos=(), accs=(), heads=1, tm=512, seq=None):
    t = ins[0][0].shape[0]
    tm = min(tm, t if seq is None else seq)
    assert t % tm == 0 and (seq is None or seq % tm == 0)
    n_in, n_pos, n_c, n_out, n_acc = len(ins), len(pos), len(consts), len(outs), len(accs)

    def body(*refs):
        vals = [r[...] for r in refs[:n_in + n_pos + n_c]]
        res = fn(*vals)
        o_refs = refs[n_in + n_pos + n_c:]
        for o_ref, r in zip(o_refs[:n_out], res[:n_out]):
            o_ref[...] = r.astype(o_ref.dtype)
        if n_acc:
            first = jnp.logical_and(pl.program_id(0) == 0, pl.program_id(1) == 0)

            @pl.when(first)
            def _():
                for o_ref, r in zip(o_refs[n_out:], res[n_out:]):
                    o_ref[...] = r

            @pl.when(jnp.logical_not(first))
            def _():
                for o_ref, r in zip(o_refs[n_out:], res[n_out:]):
                    o_ref[...] += r

    def tiled(width, c0, per_head):
        return pl.BlockSpec((tm, width), (lambda h, i: (i, c0 + h)) if per_head else (lambda h, i: (i, c0)))

    in_specs = [tiled(w, c0, ph) for _, w, c0, ph in ins]
    if n_pos:
        nblk = seq // tm
        in_specs += [pl.BlockSpec((tm, a.shape[1]), lambda h, i: (i % nblk, 0)) for a in pos]
    in_specs += [pl.BlockSpec(a.shape, lambda h, i: (0, 0)) for a in consts]
    out_specs = [tiled(w, 0, ph) for _, _, w, ph in outs] + [pl.BlockSpec(s, lambda h, i: (0, 0)) for s in accs]
    out_shape = [jax.ShapeDtypeStruct((t, c), d) for c, d, _, _ in outs] + [jax.ShapeDtypeStruct(s, F32) for s in accs]
    sem = ("arbitrary", "arbitrary") if n_acc else ("parallel", "parallel")
    res = pl.pallas_call(body, out_shape=out_shape, grid=(heads, t // tm), in_specs=in_specs, out_specs=out_specs,
                         compiler_params=_params(sem), name=name)(*[a for a, _, _, _ in ins], *pos, *consts)
    return res[0] if len(res) == 1 else res


def _ride(body, grid, rider):
    if rider is None:
        return body, [], [], [], []
    n_sem = len(rider.scratch)

    def wrapped(*refs):
        ids = [pl.program_id(a) for a in range(len(grid))]
        n_in = len(refs) - n_sem - 2 - rider.n_core_out - rider.n_core_scratch
        core_in, src = refs[:n_in], refs[n_in]
        core_out = refs[n_in + 1:n_in + 1 + rider.n_core_out]
        dst = refs[n_in + 1 + rider.n_core_out]
        core_scr = refs[n_in + 2 + rider.n_core_out:len(refs) - n_sem]
        sems = refs[len(refs) - n_sem:]

        @pl.when(functools.reduce(jnp.logical_and, [a == 0 for a in ids]))
        def _():
            rider.start(src, dst, *sems)

        body(*core_in, *core_out, *core_scr)

        @pl.when(functools.reduce(jnp.logical_and, [a == n - 1 for a, n in zip(ids, grid)]))
        def _():
            rider.finish(src, dst, *sems)

    hbm = pl.BlockSpec(memory_space=pl.ANY)
    return wrapped, [rider.src], [hbm], [rider.out_shape], list(rider.scratch)


def _attn_fwd(name, q, k, kc0, v, vc0, *, heads, group, nseq, seq, tq=512, rider=None):
    tq = min(tq, seq)
    nq = seq // tq
    grid = (heads, nseq, nq)

    def body(q_ref, k_ref, v_ref, o_ref):
        s = lax.dot_general(q_ref[...], k_ref[...].astype(BF16), (((1,), (1,)), ((), ())), preferred_element_type=F32)
        p = jnp.exp(s - jnp.max(s, axis=-1, keepdims=True))
        inv = 1.0 / jnp.sum(p, axis=-1, keepdims=True)
        o = jnp.dot(p.astype(BF16), v_ref[...].astype(BF16), preferred_element_type=F32)
        o_ref[...] = (o * inv).astype(o_ref.dtype)

    if rider is not None:
        rider.n_core_out, rider.n_core_scratch = 1, 0
    body, x_in, x_spec, x_out, x_scr = _ride(body, grid, rider)
    q_spec = pl.BlockSpec((tq, HEAD_W), lambda h, b, i: (b * nq + i, h))
    sem = ("parallel",) * 3 if rider is None else ("arbitrary",) * 3
    res = pl.pallas_call(
        body, out_shape=[jax.ShapeDtypeStruct(q.shape, BF16)] + x_out, grid=grid,
        in_specs=[q_spec, pl.BlockSpec((seq, HEAD_W), lambda h, b, i: (b, kc0 + h // group)),
                  pl.BlockSpec((seq, HEAD_W), lambda h, b, i: (b, vc0 + h // group))] + x_spec,
        out_specs=[q_spec] + x_spec, scratch_shapes=x_scr, compiler_params=_params(sem), name=name)(q, k, v, *x_in)
    return res[0] if rider is None else res


def _attn_bwd(name, q, k, kc0, v, vc0, o, do, *, heads, group, nseq, seq, tq=256, rider=None):
    tq = min(tq, seq)
    nq = seq // tq
    hk = heads // group
    t = q.shape[0]
    grid = (hk, nseq, group, nq)

    def body(q_ref, k_ref, v_ref, o_ref, do_ref, dq_ref, dk_ref, dv_ref, dk_acc, dv_acc):
        g, i = pl.program_id(2), pl.program_id(3)
        qv, kv, vv, dov = q_ref[...], k_ref[...].astype(BF16), v_ref[...].astype(BF16), do_ref[...]
        s = lax.dot_general(qv, kv, (((1,), (1,)), ((), ())), preferred_element_type=F32)
        p = jnp.exp(s - jnp.max(s, axis=-1, keepdims=True))
        pn = p * (1.0 / jnp.sum(p, axis=-1, keepdims=True))
        dp = lax.dot_general(dov, vv, (((1,), (1,)), ((), ())), preferred_element_type=F32)
        delta = jnp.sum(dov.astype(F32) * o_ref[...].astype(F32), axis=-1, keepdims=True)
        ds = (pn * (dp - delta)).astype(BF16)
        dq_ref[...] = jnp.dot(ds, kv, preferred_element_type=F32)
        dk_part = lax.dot_general(ds, qv, (((0,), (0,)), ((), ())), preferred_element_type=F32)
        dv_part = lax.dot_general(pn.astype(BF16), dov, (((0,), (0,)), ((), ())), preferred_element_type=F32)
        first = jnp.logical_and(g == 0, i == 0)

        @pl.when(first)
        def _():
            dk_acc[...] = dk_part
            dv_acc[...] = dv_part

        @pl.when(jnp.logical_not(first))
        def _():
            dk_acc[...] += dk_part
            dv_acc[...] += dv_part

        @pl.when(jnp.logical_and(g == group - 1, i == nq - 1))
        def _():
            dk_ref[...] = dk_acc[...].astype(dk_ref.dtype)
            dv_ref[...] = dv_acc[...].astype(dv_ref.dtype)

    if rider is not None:
        rider.n_core_out, rider.n_core_scratch = 3, 2
    body, x_in, x_spec, x_out, x_scr = _ride(body, grid, rider)
    q_spec = pl.BlockSpec((tq, HEAD_W), lambda kh, b, g, i: (b * nq + i, kh * group + g))
    kv_out = pl.BlockSpec((seq, HEAD_W), lambda kh, b, g, i: (b, kh))
    sem = ("parallel", "parallel", "arbitrary", "arbitrary") if rider is None else ("arbitrary",) * 4
    return pl.pallas_call(
        body,
        out_shape=[jax.ShapeDtypeStruct(q.shape, F32), jax.ShapeDtypeStruct((t, hk * HEAD_W), BF16),
                   jax.ShapeDtypeStruct((t, hk * HEAD_W), BF16)] + x_out,
        grid=grid,
        in_specs=[q_spec, pl.BlockSpec((seq, HEAD_W), lambda kh, b, g, i: (b, kc0 + kh)),
                  pl.BlockSpec((seq, HEAD_W), lambda kh, b, g, i: (b, vc0 + kh)), q_spec, q_spec] + x_spec,
        out_specs=[q_spec, kv_out, kv_out] + x_spec,
        scratch_shapes=[pltpu.VMEM((seq, HEAD_W), F32), pltpu.VMEM((seq, HEAD_W), F32)] + x_scr,
        compiler_params=_params(sem), name=name)(q, k, v, o, do, *x_in)


def _place():
    return lax.axis_index("x"), lax.axis_index("y"), lax.axis_index("c")


def _other_chips(x, y):
    return [(1 - x, y), (x, 1 - y), (1 - x, 1 - y)]


class _Exchange:
    def __init__(self, kind, src):
        assert kind in ("gather", "scatter", "pair")
        self.kind, self.src = kind, src
        self.rows, w = src.shape[-2:]
        self.n = PAIR_CHUNKS if kind == "pair" else 3
        self.out_shape = jax.ShapeDtypeStruct((2 if kind == "pair" else 4, self.rows, w), src.dtype)
        self.scratch = [pltpu.SemaphoreType.DMA((self.n,)), pltpu.SemaphoreType.DMA((self.n,)), pltpu.SemaphoreType.DMA(())]
        self.n_core_out = self.n_core_scratch = 0

    def _copies(self, src_ref, out_ref, send_sems, recv_sems, landing):
        x, y, c = _place()

        def remote(k, s, d, to):
            return pltpu.make_async_remote_copy(src_ref=s, dst_ref=d, send_sem=send_sems.at[k], recv_sem=recv_sems.at[k],
                                                device_id=to, device_id_type=MESH)

        if self.kind == "pair":
            ck = self.rows // self.n
            chunk = lambda k: pl.ds(k * ck, ck)
            slot = 1 - c if landing else c
            return [remote(k, src_ref.at[chunk(k)], out_ref.at[slot, chunk(k)], (x, y, 1 - c)) for k in range(self.n)]
        me = 2 * x + y
        part = (lambda j: src_ref) if self.kind == "gather" else (lambda j: src_ref.at[j])
        if landing:
            return [remote(k, part(me), out_ref.at[2 * px + py], (px, py, c)) for k, (px, py) in enumerate(_other_chips(x, y))]
        return [remote(k, part(2 * px + py), out_ref.at[me], (px, py, c)) for k, (px, py) in enumerate(_other_chips(x, y))]

    def _local(self, src_ref, out_ref, local_sem):
        x, y, c = _place()
        if self.kind == "pair":
            return pltpu.make_async_copy(src_ref, out_ref.at[c], local_sem)
        me = 2 * x + y
        return pltpu.make_async_copy(src_ref if self.kind == "gather" else src_ref.at[me], out_ref.at[me], local_sem)

    def start(self, src_ref, out_ref, send_sems, recv_sems, local_sem):
        self._local(src_ref, out_ref, local_sem).start()
        for mine in self._copies(src_ref, out_ref, send_sems, recv_sems, False):
            mine.start()

    def finish(self, src_ref, out_ref, send_sems, recv_sems, local_sem):
        for landed in self._copies(src_ref, out_ref, send_sems, recv_sems, True):
            landed.wait_recv()
        for mine in self._copies(src_ref, out_ref, send_sems, recv_sems, False):
            mine.wait_send()
        self._local(src_ref, out_ref, local_sem).wait()


def _exchange(name, kind, src):
    ex = _Exchange(kind, src)

    def body(src_ref, out_ref, *sems):
        ex.start(src_ref, out_ref, *sems)
        ex.finish(src_ref, out_ref, *sems)

    hbm = pl.BlockSpec(memory_space=pl.ANY)
    return pl.pallas_call(body, out_shape=ex.out_shape, in_specs=[hbm], out_specs=hbm, scratch_shapes=ex.scratch, name=name)(src)


def _adamw(w, g, m, v):
    m = ADAM_B1 * m + (1.0 - ADAM_B1) * g
    v = ADAM_B2 * v + (1.0 - ADAM_B2) * (g * g)
    delta = -ADAM_LR * ((m / M_HAT_DIV) / (jnp.sqrt(v / V_HAT_DIV) + ADAM_EPS) + ADAM_WD * w)
    return delta, m, v


def _small_allreduce_adamw(part, w, m, v):
    def body(part_ref, w_ref, m_ref, v_ref, g_out, d_out, m_out, v_out, loss_out, buf, send_sems, recv_sems):
        x, y, c = _place()
        me = 4 * x + 2 * y + c
        buf[me] = part_ref[...]

        def flip(k):
            fx, fy, fc = (k >> 2) & 1, (k >> 1) & 1, k & 1
            px, py, pc = (1 - x if fx else x), (1 - y if fy else y), (1 - c if fc else c)
            return (px, py, pc), 4 * px + 2 * py + pc

        def copy(k, slot):
            return pltpu.make_async_remote_copy(
                src_ref=part_ref, dst_ref=buf.at[slot], send_sem=send_sems.at[k - 1], recv_sem=recv_sems.at[k - 1],
                device_id=flip(k)[0], device_id_type=MESH)

        sent = [copy(k, me) for k in range(1, 8)]
        for cp in sent:
            cp.start()
        for k in range(1, 8):
            copy(k, flip(k)[1]).wait_recv()
        for cp in sent:
            cp.wait_send()
        tot = buf[0]
        for j in range(1, 8):
            tot = tot + buf[j]
        delta, m_new, v_new = _adamw(w_ref[...], tot, m_ref[...], v_ref[...])
        g_out[...] = tot
        d_out[...] = delta
        m_out[...] = m_new
        v_out[...] = v_new
        loss_out[...] = jnp.sum(tot[LOSS_ROW0:LOSS_ROW0 + 8, :]).reshape(1, 1)

    vm = pl.BlockSpec(memory_space=pltpu.VMEM)
    shp = jax.ShapeDtypeStruct((SMALL_ROWS, 128), F32)
    return pl.pallas_call(
        body, out_shape=[shp, shp, shp, shp, jax.ShapeDtypeStruct((1, 1), F32)],
        in_specs=[vm, vm, vm, vm], out_specs=[vm, vm, vm, vm, vm],
        scratch_shapes=[pltpu.VMEM((8, SMALL_ROWS, 128), F32), pltpu.SemaphoreType.DMA((7,)), pltpu.SemaphoreType.DMA((7,))],
        name="small_allreduce_adamw")(part, w, m, v)


def _row_tile(rows, cap):
    return max(t for t in range(16, min(rows, cap) + 1, 16) if rows % t == 0)


def _sum4(name, parts):
    _, rows, w = parts.shape
    tr = _row_tile(rows, 512)

    def body(p_ref, o_ref):
        o_ref[...] = ((p_ref[0].astype(F32) + p_ref[1].astype(F32)) + p_ref[2].astype(F32)) + p_ref[3].astype(F32)

    return pl.pallas_call(
        body, out_shape=jax.ShapeDtypeStruct((rows, w), F32), grid=(rows // tr,),
        in_specs=[pl.BlockSpec((4, tr, w), lambda i: (0, i, 0))], out_specs=pl.BlockSpec((tr, w), lambda i: (i, 0)),
        compiler_params=_params(("parallel",)), name=name)(parts)


def _pair_add(name, pair):
    _, rows, w = pair.shape
    tr = _row_tile(rows, 512)

    def body(p_ref, o_ref):
        o_ref[...] = p_ref[0] + p_ref[1]

    return pl.pallas_call(
        body, out_shape=jax.ShapeDtypeStruct((rows, w), F32), grid=(rows // tr,),
        in_specs=[pl.BlockSpec((2, tr, w), lambda i: (0, i, 0))], out_specs=pl.BlockSpec((tr, w), lambda i: (i, 0)),
        compiler_params=_params(("parallel",)), name=name)(pair)


def _adamw_shard(name, g, w, m, v):
    rows, cols = w.shape
    tr = _row_tile(rows, 256)

    def body(g_ref, w_ref, m_ref, v_ref, d_out, m_out, v_out):
        delta, m_new, v_new = _adamw(w_ref[...], g_ref[...], m_ref[...], v_ref[...])
        d_out[...] = delta
        m_out[...] = m_new
        v_out[...] = v_new

    t_spec = pl.BlockSpec((tr, cols), lambda i: (i, 0))
    shp = jax.ShapeDtypeStruct((rows, cols), F32)
    return pl.pallas_call(body, out_shape=[shp] * 3, grid=(rows // tr,), in_specs=[t_spec] * 4, out_specs=[t_spec] * 3,
                          compiler_params=_params(("parallel",)), name=name)(g, w, m, v)


def _shard_shape(name):
    _, r, c, ax = BIG_BY_NAME[name]
    return (r, c // 4) if ax == 1 else (r // 4, c)


def _pad_rows(a, axis):
    pad = [(0, 0)] * a.ndim
    pad[axis] = (0, -a.shape[axis] % PACK_ALIGN)
    return jnp.pad(a, pad)


def _pack_shards(names, shards, dtype):
    return _pad_rows(jnp.concatenate([s.astype(dtype).reshape(-1, PACK_W) for s in shards], axis=0), 0)


def _unpack_shards(names, slab):
    out, off = [], 0
    for name in names:
        rs, cs = _shard_shape(name)
        n = rs * cs // PACK_W
        out.append(slab[off:off + n].reshape(rs, cs))
        off += n
    return out


def _unpack_full(names, slabs):
    out, off = [], 0
    for name in names:
        _, r, c, ax = BIG_BY_NAME[name]
        n = r * c // 4 // PACK_W
        seg = slabs[:, off:off + n]
        out.append(seg.reshape(4, r, c // 4).transpose(1, 0, 2).reshape(r, c) if ax == 1 else seg.reshape(r, c))
        off += n
    return out


def _pack_full(names, mats, dtype):
    segs = []
    for name, a in zip(names, mats):
        _, r, c, ax = BIG_BY_NAME[name]
        a = a.astype(dtype)
        a = a.reshape(r, 4, c // 4).transpose(1, 0, 2) if ax == 1 else a
        segs.append(a.reshape(4, -1, PACK_W))
    return _pad_rows(jnp.concatenate(segs, axis=1), 1)


def _pad_heads_cols(wm, heads, d):
    k = wm.shape[0]
    return jnp.pad(wm.reshape(k, heads, d), ((0, 0), (0, 0), (0, HEAD_W - d))).reshape(k, heads * HEAD_W)


def _unpad_heads_cols(wm, heads, d):
    k = wm.shape[0]
    return wm.reshape(k, heads, HEAD_W)[:, :, :d].reshape(k, heads * d)


def _win_ext(w_in):
    o = np.cumsum([0, Q_LORA, KV_LORA, QK_ROPE, H_B * HD_B, KV_B * HD_B, KV_B * HD_B, D_MODEL, D_MODEL])
    pc = lambda a, n: jnp.pad(a, ((0, 0), (0, n - a.shape[1])))
    return jnp.concatenate([
        w_in[:, o[0]:o[1]], w_in[:, o[1]:o[2]], pc(w_in[:, o[2]:o[3]], HEAD_W),
        _pad_heads_cols(w_in[:, o[3]:o[4]], H_B, HD_B), _pad_heads_cols(w_in[:, o[4]:o[5]], KV_B, HD_B),
        _pad_heads_cols(w_in[:, o[5]:o[6]], KV_B, HD_B), w_in[:, o[6]:o[7]], w_in[:, o[7]:o[8]]], axis=1)


def _win_unext(we):
    c = HEAD_W
    return jnp.concatenate([
        we[:, :ZC_CKV * c], we[:, ZC_CKV * c:ZC_KPE * c], we[:, ZC_KPE * c:ZC_KPE * c + QK_ROPE],
        _unpad_heads_cols(we[:, ZC_QB * c:ZC_KB * c], H_B, HD_B), _unpad_heads_cols(we[:, ZC_KB * c:ZC_VB * c], KV_B, HD_B),
        _unpad_heads_cols(we[:, ZC_VB * c:ZC_GA * c], KV_B, HD_B), we[:, ZC_GA * c:]], axis=1)


def _wkv_ext(w_kvb):
    wk = w_kvb.reshape(KV_LORA, H_A, QK_NOPE + V_DIM_A)
    k_cols = jnp.pad(wk[:, :, :QK_NOPE], ((0, 0), (0, 0), (0, HEAD_W - QK_NOPE))).reshape(KV_LORA, H_A * HEAD_W)
    v_cols = jnp.pad(wk[:, :, QK_NOPE:], ((0, 0), (0, 0), (0, HEAD_W - V_DIM_A))).reshape(KV_LORA, H_A * HEAD_W)
    eye = jnp.pad(jnp.eye(QK_ROPE, dtype=w_kvb.dtype), ((0, 0), (QK_NOPE, HEAD_W - QK_NOPE - QK_ROPE)))
    pe_rows = jnp.concatenate([jnp.tile(eye, (1, H_A)), jnp.zeros((QK_ROPE, H_A * HEAD_W), w_kvb.dtype)], axis=1)
    top = jnp.concatenate([k_cols, v_cols], axis=1)
    return jnp.concatenate([top, pe_rows, jnp.zeros((2 * HEAD_W - KV_LORA - QK_ROPE, 2 * H_A * HEAD_W), w_kvb.dtype)], axis=0)


def _wkv_unext(we):
    k_cols = we[:KV_LORA, :H_A * HEAD_W].reshape(KV_LORA, H_A, HEAD_W)[:, :, :QK_NOPE]
    v_cols = we[:KV_LORA, H_A * HEAD_W:].reshape(KV_LORA, H_A, HEAD_W)[:, :, :V_DIM_A]
    return jnp.concatenate([k_cols, v_cols], axis=2).reshape(KV_LORA, H_A * (QK_NOPE + V_DIM_A))


def _pad_heads_rows(wm, heads, d):
    n = wm.shape[1]
    return jnp.pad(wm.reshape(heads, d, n), ((0, 0), (0, HEAD_W - d), (0, 0))).reshape(heads * HEAD_W, n)


def _unpad_heads_rows(wm, heads, d):
    n = wm.shape[1]
    return wm.reshape(heads, HEAD_W, n)[:, :d].reshape(heads * d, n)


def _rope_tables(seq):
    def ang(pos, dim):
        inv = ROPE_THETA ** (-jnp.arange(0, dim, 2, dtype=F32) / dim)
        return pos.astype(F32)[:, None] * inv[None, :]

    def rot(dim):
        r = np.zeros((dim, dim), np.float32)
        half = dim // 2
        r[np.arange(half) + half, np.arange(half)] = -1.0
        r[np.arange(half), np.arange(half) + half] = 1.0
        return r

    def table(blocks):
        first, width = blocks[0][0], sum(2 * a.shape[1] for _, a in blocks)
        ones = lambda n: jnp.ones((seq, n), F32)
        cos = jnp.concatenate([ones(first)] + [jnp.cos(a) for _, a in blocks for _ in (0, 1)] + [ones(HEAD_W - first - width)], axis=1)
        sin = jnp.concatenate([0.0 * ones(first)] + [jnp.sin(a) for _, a in blocks for _ in (0, 1)]
                              + [0.0 * ones(HEAD_W - first - width)], axis=1)
        pm = np.zeros((HEAD_W, HEAD_W), np.float32)
        for c0, a in blocks:
            d = 2 * a.shape[1]
            pm[c0:c0 + d, c0:c0 + d] = rot(d)
        return cos, sin, jnp.asarray(pm, BF16), jnp.asarray(pm.T, BF16)

    tok = jnp.arange(seq)
    a1 = ang(tok, QK_ROPE)
    arow, acol = ang(tok // GRID_W, HD_B // 2), ang(tok % GRID_W, HD_B // 2)
    return table([(QK_NOPE, a1)]), table([(0, a1)]), table([(0, arow), (HD_B // 2, acol)])


def _local_step(x, p, tgt, gains, wts, ride=None):
    nb, seq, _ = x.shape
    t = nb * seq
    x0 = x.reshape(t, D_MODEL)
    p2 = p.reshape(t, PLE_DIM)
    tg = tgt.reshape(t, D_MODEL)
    (cq_t, sq_t, pq, pq_t), (ck_t, sk_t, pk, pk_t), (cb_t, sb_t, pb, pb_t) = _rope_tables(seq)
    padg = lambda g: jnp.pad(g, ((0, 0), (0, HEAD_W - g.shape[1])))
    g_qn, g_kn = padg(gains["g_qn"]), padg(gains["g_kn"])

    win = _win_ext(wts["w_in"])
    wqb = _pad_heads_cols(wts["w_qb"], H_A, QK_NOPE + QK_ROPE)
    wkv = _wkv_ext(wts["w_kvb"])

    norm = lambda n: (lambda v, g: (_rms(v, g, n),))
    full = lambda a: (a, a.shape[1], 0, False)

    h = _rowwise("norm_mix", norm(D_MODEL), [full(x0)], [(D_MODEL, BF16, D_MODEL, False)], consts=[gains["g_mix"]])
    z = _mm("in_proj", h, win, tn=2048)
    cq = _rowwise("norm_qa", norm(Q_LORA), [(z, Q_LORA, 0, False)], [(Q_LORA, BF16, Q_LORA, False)], consts=[gains["g_qa"]])
    qa = _mm("q_up", cq, wqb)

    def rope_fwd(scale):
        return lambda v, cos, sin, pm: ((v * cos + _perm(v, pm) * sin) * scale,)

    q_a = _rowwise("rope_qa", rope_fwd(SCALE_A), [(qa, HEAD_W, 0, True)], [(H_A * HEAD_W, BF16, HEAD_W, True)],
                   pos=[cq_t, sq_t], consts=[pq], heads=H_A, seq=seq)
    ckv = _rowwise("norm_kva", norm(KV_LORA), [(z, HEAD_W, ZC_CKV, False)], [(HEAD_W, BF16, HEAD_W, False)], consts=[gains["g_kva"]])
    kpe = _rowwise("rope_kpe", rope_fwd(1.0), [(z, HEAD_W, ZC_KPE, False)], [(HEAD_W, BF16, HEAD_W, False)],
                   pos=[ck_t, sk_t], consts=[pk], seq=seq)
    kin = jnp.concatenate([ckv, kpe], axis=1)
    kv_a = _mm("kv_up", kin, wkv, out_dtypes=(BF16,))
    o_a = _attn_fwd("attn_a_fwd", q_a, kv_a, 0, kv_a, H_A, heads=H_A, group=1, nseq=nb, seq=seq,
                    rider=ride and ride["gather_a"])

    def prep_fwd(scale):
        def fn(v, cos, sin, g, pm):
            yv = _rms(v, g, HD_B)
            return ((yv * cos + _perm(yv, pm) * sin) * scale,)
        return fn

    q_b = _rowwise("prep_qb", prep_fwd(SCALE_B), [(z, HEAD_W, ZC_QB, True)], [(H_B * HEAD_W, BF16, HEAD_W, True)],
                   pos=[cb_t, sb_t], consts=[g_qn, pb], heads=H_B, seq=seq)
    k_b = _rowwise("prep_kb", prep_fwd(1.0), [(z, HEAD_W, ZC_KB, True)], [(KV_B * HEAD_W, BF16, HEAD_W, True)],
                   pos=[cb_t, sb_t], consts=[g_kn, pb], heads=KV_B, seq=seq)
    o_b = _attn_fwd("attn_b_fwd", q_b, k_b, 0, z, ZC_VB, heads=H_B, group=H_B // KV_B, nseq=nb, seq=seq,
                    rider=ride and ride["gather_b"])
    if ride is not None:
        (o_a, got_a), (o_b, got_b) = o_a, o_b
        wts = {**wts, **ride["late_weights"](got_a, got_b)}
    woa = _pad_heads_rows(wts["w_oa"], H_A, V_DIM_A)
    wob = _pad_heads_rows(wts["w_ob"], H_B, HD_B)
    wo, wup, wdown, wpg, wple = wts["w_o"], wts["w_up"], wts["w_down"], wts["w_ple_gate"], wts["w_ple"]
    ya = _mm("out_a", o_a, woa)
    yb = _mm("out_b", o_b, wob)

    z_ga, z_gb = (z, D_MODEL, ZC_GA // 8, False), (z, D_MODEL, ZC_GB // 8, False)
    merged = _rowwise("merge", lambda ga, gb, a, b: (_sigmoid(ga) * a + _sigmoid(gb) * b,),
                      [z_ga, z_gb, full(ya), full(yb)], [(D_MODEL, BF16, D_MODEL, False)])
    x1 = _mm("out_proj", merged, wo, epi=lambda acc, r: (r + acc,), extras=(x0,))
    h2 = _rowwise("norm_mlp", norm(D_MODEL), [full(x1)], [(D_MODEL, BF16, D_MODEL, False)], consts=[gains["g_mlp"]])

    def relu2(acc):
        u = jnp.maximum(acc, 0.0)
        return u, u * u

    u, usq = _mm("mlp_up", h2, wup, out_dtypes=(BF16, BF16), epi=relu2, tn=2048)
    x2 = _mm("mlp_down", usq, wdown, epi=lambda acc, r: (r + acc,), extras=(x1,))
    h3 = _rowwise("norm_ple", norm(D_MODEL), [full(x2)], [(D_MODEL, BF16, D_MODEL, False)], consts=[gains["g_ple"]])
    gpre = _mm("ple_gate", h3, wpg)
    pe = _mm("ple_proj", p2, wple)

    def tail(x2v, gp, pev, tv, gf):
        sg = _sigmoid(gp)
        x3 = x2v + sg * pev
        rs = lax.rsqrt(jnp.sum(x3 * x3, axis=-1, keepdims=True) * (1.0 / D_MODEL) + EPS)
        xh = x3 * rs
        err = xh * gf - tv
        dy = err * (1.0 / D_MODEL)
        dyg = dy * gf
        dx3 = rs * (dyg - xh * (jnp.sum(dyg * xh, axis=-1, keepdims=True) * (1.0 / D_MODEL)))
        return (dx3, dx3 * pev * sg * (1.0 - sg), dx3 * sg,
                jnp.sum(err * err, axis=0, keepdims=True) * (0.5 / D_MODEL), jnp.sum(dy * xh, axis=0, keepdims=True))

    dx3, dgpre, dpe, loss_part, dg_final = _rowwise(
        "tail", tail, [full(x2), full(gpre), full(pe), full(tg)],
        [(D_MODEL, F32, D_MODEL, False), (D_MODEL, BF16, D_MODEL, False), (D_MODEL, BF16, D_MODEL, False)],
        consts=[gains["g_final"].reshape(1, D_MODEL)], accs=[(1, D_MODEL), (1, D_MODEL)], tm=256)

    def norm_bwd(n, with_res):
        if with_res:
            def fn(dh, v, res, g):
                dx, dg = _rms_bwd(dh, v, g, n)
                return dx + res, dg
        else:
            def fn(dh, v, g):
                return _rms_bwd(dh, v, g, n)
        return fn

    dw = {}
    dw["w_ple"] = _mm_tn("dw_ple", p2, dpe)
    dw["w_ple_gate"] = _mm_tn("dw_ple_gate", h3, dgpre)
    dh3 = _mm("d_ple_gate", dgpre, wpg, trans_b=True)
    dx2, dg_ple = _rowwise("norm_ple_bwd", norm_bwd(D_MODEL, True), [full(dh3), full(x2), full(dx3)],
                           [(D_MODEL, F32, D_MODEL, False)], consts=[gains["g_ple"]], accs=[(1, D_MODEL)], tm=256)
    dw["w_down"] = _mm_tn("dw_down", usq, dx2)
    dupre = _mm("d_mlp_down", dx2, wdown, trans_b=True, out_dtypes=(BF16,), epi=lambda acc, uv: (acc * (2.0 * uv.astype(F32)),),
                extras=(u,), tn=2048)
    dw["w_up"] = _mm_tn("dw_up", h2, dupre)
    dh2 = _mm("d_mlp_up", dupre, wup, trans_b=True)
    dx1, dg_mlp = _rowwise("norm_mlp_bwd", norm_bwd(D_MODEL, True), [full(dh2), full(x1), full(dx2)],
                           [(D_MODEL, F32, D_MODEL, False)], consts=[gains["g_mlp"]], accs=[(1, D_MODEL)], tm=256)
    dw["w_o"] = _mm_tn("dw_o", merged, dx1)
    dmerged = _mm("d_out_proj", dx1, wo, trans_b=True)

    def merge_bwd(dm, ga, gb, a, b):
        sa, sb = _sigmoid(ga), _sigmoid(gb)
        return dm * sa, dm * sb, dm * a * sa * (1.0 - sa), dm * b * sb * (1.0 - sb)

    dya, dyb, dga, dgb = _rowwise("merge_bwd", merge_bwd, [full(dmerged), z_ga, z_gb, full(ya), full(yb)],
                                  [(D_MODEL, BF16, D_MODEL, False)] * 4, tm=256)
    dw["w_oa"] = _unpad_heads_rows(_mm_tn("dw_oa", o_a, dya), H_A, V_DIM_A)
    dw["w_ob"] = _unpad_heads_rows(_mm_tn("dw_ob", o_b, dyb), H_B, HD_B)
    do_a = _mm("d_out_a", dya, woa, trans_b=True, out_dtypes=(BF16,))
    do_b = _mm("d_out_b", dyb, wob, trans_b=True, out_dtypes=(BF16,))
    res_a = _attn_bwd("attn_a_bwd", q_a, kv_a, 0, kv_a, H_A, o_a, do_a, heads=H_A, group=1, nseq=nb, seq=seq,
                      rider=ride and ride["scatter"](dw))
    dq_a, dk_a, dv_a = res_a[:3]
    res_b = _attn_bwd("attn_b_bwd", q_b, k_b, 0, z, ZC_VB, o_b, do_b, heads=H_B, group=H_B // KV_B, nseq=nb, seq=seq,
                      rider=ride and ride["pair"](res_a[3]))
    dq_b, dk_b, dv_b = res_b[:3]
    if ride is not None:
        ride["out"]["pair_late"] = res_b[3]

    def rope_bwd(scale):
        return lambda d, cos, sin, pm_t: ((d * cos + _perm(d * sin, pm_t)) * scale,)

    dqa = _rowwise("rope_qa_bwd", rope_bwd(SCALE_A), [(dq_a, HEAD_W, 0, True)], [(H_A * HEAD_W, BF16, HEAD_W, True)],
                   pos=[cq_t, sq_t], consts=[pq_t], heads=H_A, seq=seq)
    dw["w_qb"] = _unpad_heads_cols(_mm_tn("dw_qb", cq, dqa), H_A, QK_NOPE + QK_ROPE)
    dcq = _mm("d_q_up", dqa, wqb, trans_b=True)
    dq_lat, dg_qa = _rowwise("norm_qa_bwd", norm_bwd(Q_LORA, False), [full(dcq), (z, Q_LORA, 0, False)],
                             [(Q_LORA, BF16, Q_LORA, False)], consts=[gains["g_qa"]], accs=[(1, Q_LORA)])
    dkv_a = jnp.concatenate([dk_a, dv_a], axis=1)
    dw["w_kvb"] = _wkv_unext(_mm_tn("dw_kv", kin, dkv_a))
    dkin = _mm("d_kv_up", dkv_a, wkv, trans_b=True)
    dckv, dg_kva = _rowwise("norm_kva_bwd", norm_bwd(KV_LORA, False), [(dkin, HEAD_W, 0, False), (z, HEAD_W, ZC_CKV, False)],
                            [(HEAD_W, BF16, HEAD_W, False)], consts=[gains["g_kva"]], accs=[(1, KV_LORA)])
    dkpe = _rowwise("rope_kpe_bwd", rope_bwd(1.0), [(dkin, HEAD_W, 1, False)], [(HEAD_W, BF16, HEAD_W, False)],
                    pos=[ck_t, sk_t], consts=[pk_t], seq=seq)

    def prep_bwd(scale):
        def fn(d, v, cos, sin, g, pm_t):
            dyv = (d * cos + _perm(d * sin, pm_t)) * scale
            return _rms_bwd(dyv, v, g, HD_B)
        return fn

    dqb, dg_qn = _rowwise("prep_qb_bwd", prep_bwd(SCALE_B), [(dq_b, HEAD_W, 0, True), (z, HEAD_W, ZC_QB, True)],
                          [(H_B * HEAD_W, BF16, HEAD_W, True)], pos=[cb_t, sb_t], consts=[g_qn, pb_t], accs=[(1, HEAD_W)],
                          heads=H_B, seq=seq)
    dkb, dg_kn = _rowwise("prep_kb_bwd", prep_bwd(1.0), [(dk_b, HEAD_W, 0, True), (z, HEAD_W, ZC_KB, True)],
                          [(KV_B * HEAD_W, BF16, HEAD_W, True)], pos=[cb_t, sb_t], consts=[g_kn, pb_t], accs=[(1, HEAD_W)],
                          heads=KV_B, seq=seq)

    dz = jnp.concatenate([dq_lat, dckv, dkpe, dqb, dkb, dv_b, dga, dgb], axis=1)
    dw["w_in"] = _win_unext(_mm_tn("dw_in", h, dz))
    dh = _mm("d_in_proj", dz, win, trans_b=True)
    dx0, dg_mix = _rowwise("norm_mix_bwd", norm_bwd(D_MODEL, True), [full(dh), full(x0), full(dx1)],
                           [(D_MODEL, F32, D_MODEL, False)], consts=[gains["g_mix"]], accs=[(1, D_MODEL)], tm=256)

    dg = {"g_mix": dg_mix, "g_qa": dg_qa, "g_kva": dg_kva, "g_qn": dg_qn[:, :HD_B], "g_kn": dg_kn[:, :HD_B],
          "g_mlp": dg_mlp, "g_ple": dg_ple, "g_final": dg_final}
    return loss_part, dx0.reshape(nb, seq, D_MODEL), dg, dw


def _pack_small(vals, loss_part=None):
    flat = jnp.concatenate([vals[n].reshape(1, -1) for n, _ in SMALL], axis=1)
    loss = jnp.zeros((1, 8 * 128), F32) if loss_part is None else loss_part
    gap = jnp.zeros((1, LOSS_ROW0 * 128 - SMALL_N), F32)
    return jnp.concatenate([flat, gap, loss], axis=1).reshape(SMALL_ROWS, 128)


def _unpack_small(slab, like):
    flat, out, off = slab.reshape(-1), {}, 0
    for n, k in SMALL:
        out[n] = flat[off:off + k].reshape(like[n].shape)
        off += k
    return out


def kernel(x, p, g_mix, w_in, g_qa, w_qb, g_kva, w_kvb, g_qn, g_kn, w_oa, w_ob, w_o, g_mlp, w_up, w_down, g_ple, w_ple_gate, w_ple, g_final, loss_target, m_g_mix, m_w_in, m_g_qa, m_w_qb, m_g_kva, m_w_kvb, m_g_qn, m_g_kn, m_w_oa, m_w_ob, m_w_o, m_g_mlp, m_w_up, m_w_down, m_g_ple, m_w_ple_gate, m_w_ple, m_g_final, v_g_mix, v_w_in, v_g_qa, v_w_qb, v_g_kva, v_w_kvb, v_g_qn, v_g_kn, v_w_oa, v_w_ob, v_w_o, v_g_mlp, v_w_up, v_w_down, v_g_ple, v_w_ple_gate, v_w_ple, v_g_final):
    given = dict(locals())
    order = ["g_mix", "w_in", "g_qa", "w_qb", "g_kva", "w_kvb", "g_qn", "g_kn", "w_oa", "w_ob", "w_o", "g_mlp", "w_up",
             "w_down", "g_ple", "w_ple_gate", "w_ple", "g_final"]
    big_names = [n for n, _, _, _ in BIG]
    local = lambda prefix, names: [given[prefix + n][0] for n in names]
    late = LATE_A + LATE_B

    early_w = _exchange("weight_gather_early", "gather", _pack_shards(EARLY, local("", EARLY), BF16))
    wts = dict(zip(EARLY, _unpack_full(EARLY, early_w)))
    gains = {n: given[n].reshape(1, -1) for n, _ in SMALL}
    ride = {
        "gather_a": _Exchange("gather", _pack_shards(LATE_A, local("", LATE_A), BF16)),
        "gather_b": _Exchange("gather", _pack_shards(LATE_B, local("", LATE_B), BF16)),
        "late_weights": lambda ga, gb: {**dict(zip(LATE_A, _unpack_full(LATE_A, ga))), **dict(zip(LATE_B, _unpack_full(LATE_B, gb)))},
        "scatter": lambda dw: _Exchange("scatter", _pack_full(late, [dw[n] for n in late], BF16)),
        "pair": lambda parts: _Exchange("pair", _sum4("sum4_late", parts)),
        "out": {},
    }
    loss_part, grad_x, dg, dw = _local_step(x, p[0], loss_target, gains, wts, ride)

    small = lambda prefix: _pack_small({n: given[prefix + n] for n, _ in SMALL})
    g_s, d_s, m_s, v_s, loss = _small_allreduce_adamw(_pack_small(dg, loss_part), small(""), small("m_"), small("v_"))

    parts = _exchange("grad_scatter_early", "scatter", _pack_full(EARLY, [dw[n] for n in EARLY], BF16))
    pair_early = _exchange("grad_pair_early", "pair", _sum4("sum4_early", parts))
    grads = dict(zip(EARLY, _unpack_shards(EARLY, _pair_add("pair_add_early", pair_early))))
    grads.update(zip(late, _unpack_shards(late, _pair_add("pair_add_late", ride["out"]["pair_late"]))))

    res = {}
    for key, slab in (("grad_", g_s), ("delta_", d_s), ("new_m_", m_s), ("new_v_", v_s)):
        for n, val in _unpack_small(slab, given).items():
            res[key + n] = val
    for n in big_names:
        d_w, m_w, v_w = _adamw_shard("adamw_" + n, grads[n], given[n][0], given["m_" + n][0], given["v_" + n][0])
        res["grad_" + n], res["delta_" + n], res["new_m_" + n], res["new_v_" + n] = grads[n][None], d_w[None], m_w[None], v_w[None]
    outs = [loss.reshape(()), grad_x]
    for key in ("grad_", "delta_", "new_m_", "new_v_"):
        outs += [res[key + n] for n in order]
    return tuple(outs)
```

```python
import functools

import numpy as np
import jax
import jax.numpy as jnp
from jax import lax
from jax.experimental import pallas as pl
from jax.experimental.pallas import tpu as pltpu

F32 = jnp.float32
BF16 = jnp.bfloat16
MESH = pl.DeviceIdType.MESH

D_MODEL = 1024
GRID_W = 64
ROPE_THETA = 10000.0
EPS = 1e-6
H_A, QK_NOPE, QK_ROPE, V_DIM_A, Q_LORA, KV_LORA = 8, 64, 32, 64, 256, 128
H_B, KV_B, HD_B = 8, 2, 64
D_FF = 4096
PLE_DIM = 256
HEAD_W = 128
SCALE_A = (QK_NOPE + QK_ROPE) ** -0.5
SCALE_B = HD_B ** -0.5

ADAM_LR, ADAM_B1, ADAM_B2, ADAM_EPS, ADAM_WD, ADAM_STEP = 0.001, 0.9, 0.999, 1e-08, 0.01, 10
M_HAT_DIV = 1.0 - ADAM_B1 ** ADAM_STEP
V_HAT_DIV = 1.0 - ADAM_B2 ** ADAM_STEP

VMEM_LIMIT_BYTES = 56 * 1024 * 1024

ZC_QLAT, ZC_CKV, ZC_KPE, ZC_QB, ZC_KB, ZC_VB, ZC_GA, ZC_GB = 0, 2, 3, 4, 12, 14, 16, 24
Z_WIDTH = 32 * HEAD_W

BIG = [
    ("w_in", 1024, 3232, 1), ("w_qb", 256, 768, 1), ("w_kvb", 128, 1024, 1), ("w_oa", 512, 1024, 1),
    ("w_ob", 512, 1024, 1), ("w_o", 1024, 1024, 0), ("w_up", 1024, 4096, 1), ("w_down", 4096, 1024, 0),
    ("w_ple_gate", 1024, 1024, 0), ("w_ple", 256, 1024, 1),
]
BIG_BY_NAME = {e[0]: e for e in BIG}
PACK_W = 1024
PACK_ALIGN = 64
EARLY = ["w_in", "w_qb", "w_kvb"]
LATE_A = ["w_oa", "w_ob", "w_o", "w_up"]
LATE_B = ["w_down", "w_ple_gate", "w_ple"]

SMALL = [("g_mix", 1024), ("g_qa", 256), ("g_kva", 128), ("g_qn", 64), ("g_kn", 64), ("g_mlp", 1024),
         ("g_ple", 1024), ("g_final", 1024)]
SMALL_N = sum(n for _, n in SMALL)
LOSS_ROW0 = 40
SMALL_ROWS = 48


def _params(sem):
    return pltpu.CompilerParams(dimension_semantics=sem, vmem_limit_bytes=VMEM_LIMIT_BYTES)


def _sigmoid(v):
    return 1.0 / (1.0 + jnp.exp(-v))


def _perm(v, p_ref):
    pm = p_ref[...]
    hi = v.astype(BF16)
    lo = (v - hi.astype(F32)).astype(BF16)
    return (jnp.dot(hi, pm, preferred_element_type=F32) + jnp.dot(lo, pm, preferred_element_type=F32))


def _rms(v, g, n):
    rs = lax.rsqrt(jnp.sum(v * v, axis=-1, keepdims=True) * (1.0 / n) + EPS)
    return v * rs * g


def _rms_bwd(dy, v, g, n):
    rs = lax.rsqrt(jnp.sum(v * v, axis=-1, keepdims=True) * (1.0 / n) + EPS)
    vh = v * rs
    dyg = dy * g
    dx = rs * (dyg - vh * (jnp.sum(dyg * vh, axis=-1, keepdims=True) * (1.0 / n)))
    return dx, jnp.sum(dy * vh, axis=0, keepdims=True)


def _ride(body, grid, rider):
    if rider is None:
        return body, [], [], [], []
    n_sem = len(rider.scratch)

    def wrapped(*refs):
        ids = [pl.program_id(a) for a in range(len(grid))]
        n_in = len(refs) - n_sem - 2 - rider.n_core_out - rider.n_core_scratch
        core_in, src = refs[:n_in], refs[n_in]
        core_out = refs[n_in + 1:n_in + 1 + rider.n_core_out]
        dst = refs[n_in + 1 + rider.n_core_out]
        core_scr = refs[n_in + 2 + rider.n_core_out:len(refs) - n_sem]
        sems = refs[len(refs) - n_sem:]

        @pl.when(functools.reduce(jnp.logical_and, [a == 0 for a in ids]))
        def _():
            rider.start(src, dst, *sems)

        body(*core_in, *core_out, *core_scr)

        @pl.when(functools.reduce(jnp.logical_and, [a == n - 1 for a, n in zip(ids, grid)]))
        def _():
            rider.finish(src, dst, *sems)

    hbm = pl.BlockSpec(memory_space=pl.ANY)
    return wrapped, [rider.src], [hbm], [rider.out_shape], list(rider.scratch)


def _mm(name, a, b, *, trans_b=False, out_dtypes=(F32,), epi=None, extras=(), tm=512, tn=None, rider=None):
    m, k = a.shape
    n = b.shape[0] if trans_b else b.shape[1]
    tn = n if tn is None else min(tn, n)
    tm = min(tm, m)
    assert m % tm == 0 and n % tn == 0 and (b.shape[1] if trans_b else b.shape[0]) == k
    n_ex = len(extras)
    dims = (((1,), (1,)), ((), ())) if trans_b else (((1,), (0,)), ((), ()))

    def body(a_ref, b_ref, *rest):
        acc = lax.dot_general(a_ref[...].astype(BF16), b_ref[...].astype(BF16), dims, preferred_element_type=F32)
        res = (acc,) if epi is None else epi(acc, *[e[...] for e in rest[:n_ex]])
        for o_ref, r in zip(rest[n_ex:], res):
            o_ref[...] = r.astype(o_ref.dtype)

    grid = (n // tn, m // tm)
    if rider is not None:
        rider.n_core_out, rider.n_core_scratch = len(out_dtypes), 0
    body, x_in, x_spec, x_out, x_scr = _ride(body, grid, rider)
    a_spec = pl.BlockSpec((tm, k), lambda j, i: (i, 0))
    b_spec = pl.BlockSpec((tn, k), lambda j, i: (j, 0)) if trans_b else pl.BlockSpec((k, tn), lambda j, i: (0, j))
    t_spec = pl.BlockSpec((tm, tn), lambda j, i: (i, j))
    outs = pl.pallas_call(
        body, out_shape=[jax.ShapeDtypeStruct((m, n), d) for d in out_dtypes] + x_out, grid=grid,
        in_specs=[a_spec, b_spec] + [t_spec] * n_ex + x_spec, out_specs=[t_spec] * len(out_dtypes) + x_spec,
        scratch_shapes=x_scr, compiler_params=_params(("parallel", "parallel") if rider is None else ("arbitrary", "arbitrary")),
        name=name)(a, b, *extras, *x_in)
    return outs[0] if len(outs) == 1 else outs


def _mm_tn(name, a, b, *, tk=1024, tn=1024, tt=1024):
    t, k = a.shape
    n = b.shape[1]
    tk, tn, tt = min(tk, k), min(tn, n), min(tt, t)
    assert b.shape[0] == t and k % tk == 0 and n % tn == 0 and t % tt == 0

    def body(a_ref, b_ref, o_ref):
        part = lax.dot_general(a_ref[...].astype(BF16), b_ref[...].astype(BF16), (((0,), (0,)), ((), ())),
                               preferred_element_type=F32)

        @pl.when(pl.program_id(2) == 0)
        def _():
            o_ref[...] = part

        @pl.when(pl.program_id(2) != 0)
        def _():
            o_ref[...] += part

    return pl.pallas_call(
        body, out_shape=jax.ShapeDtypeStruct((k, n), F32), grid=(k // tk, n // tn, t // tt),
        in_specs=[pl.BlockSpec((tt, tk), lambda ki, ni, ti: (ti, ki)), pl.BlockSpec((tt, tn), lambda ki, ni, ti: (ti, ni))],
        out_specs=pl.BlockSpec((tk, tn), lambda ki, ni, ti: (ki, ni)),
        compiler_params=_params(("parallel", "parallel", "arbitrary")), name=name)(a, b)


def _rowwise(name, fn, ins, outs, *, consts=(), pos=(), accs=(), heads=1, tm=512, seq=None):
    t = ins[0][0].shape[0]
    tm = min(tm, t if seq is None else seq)
    assert t % tm == 0 and (seq is None or seq % tm == 0)
    n_in, n_pos, n_c, n_out, n_acc = len(ins), len(pos), len(consts), len(outs), len(accs)

    def body(*refs):
        vals = [r[...] for r in refs[:n_in + n_pos + n_c]]
        res = fn(*vals)
        o_refs = refs[n_in + n_pos + n_c:]
        for o_ref, r in zip(o_refs[:n_out], res[:n_out]):
            o_ref[...] = r.astype(o_ref.dtype)
        if n_acc:
            first = jnp.logical_and(pl.program_id(0) == 0, pl.program_id(1) == 0)

            @pl.when(first)
            def _():
                for o_ref, r in zip(o_refs[n_out:], res[n_out:]):
                    o_ref[...] = r

            @pl.when(jnp.logical_not(first))
            def _():
                for o_ref, r in zip(o_refs[n_out:], res[n_out:]):
                    o_ref[...] += r

    def tiled(width, c0, per_head):
        return pl.BlockSpec((tm, width), (lambda h, i: (i, c0 + h)) if per_head else (lambda h, i: (i, c0)))

    in_specs = [tiled(w, c0, ph) for _, w, c0, ph in ins]
    if n_pos:
        nblk = seq // tm
        in_specs += [pl.BlockSpec((tm, a.shape[1]), lambda h, i: (i % nblk, 0)) for a in pos]
    in_specs += [pl.BlockSpec(a.shape, lambda h, i: (0, 0)) for a in consts]
    out_specs = [tiled(w, 0, ph) for _, _, w, ph in outs] + [pl.BlockSpec(s, lambda h, i: (0, 0)) for s in accs]
    out_shape = [jax.ShapeDtypeStruct((t, c), d) for c, d, _, _ in outs] + [jax.ShapeDtypeStruct(s, F32) for s in accs]
    sem = ("arbitrary", "arbitrary") if n_acc else ("parallel", "parallel")
    res = pl.pallas_call(body, out_shape=out_shape, grid=(heads, t // tm), in_specs=in_specs, out_specs=out_specs,
                         compiler_params=_params(sem), name=name)(*[a for a, _, _, _ in ins], *pos, *consts)
    return res[0] if len(res) == 1 else res


def _attn_fwd(name, q, k, kc0, v, vc0, *, heads, group, nseq, seq, tq=512, rider=None):
    tq = min(tq, seq)
    nq = seq // tq
    grid = (heads, nseq, nq)

    def body(q_ref, k_ref, v_ref, o_ref):
        s = lax.dot_general(q_ref[...], k_ref[...].astype(BF16), (((1,), (1,)), ((), ())), preferred_element_type=F32)
        p = jnp.exp(s - jnp.max(s, axis=-1, keepdims=True))
        inv = 1.0 / jnp.sum(p, axis=-1, keepdims=True)
        o = jnp.dot(p.astype(BF16), v_ref[...].astype(BF16), preferred_element_type=F32)
        o_ref[...] = (o * inv).astype(o_ref.dtype)

    if rider is not None:
        rider.n_core_out, rider.n_core_scratch = 1, 0
    body, x_in, x_spec, x_out, x_scr = _ride(body, grid, rider)
    q_spec = pl.BlockSpec((tq, HEAD_W), lambda h, b, i: (b * nq + i, h))
    sem = ("parallel",) * 3 if rider is None else ("arbitrary",) * 3
    res = pl.pallas_call(
        body, out_shape=[jax.ShapeDtypeStruct(q.shape, BF16)] + x_out, grid=grid,
        in_specs=[q_spec, pl.BlockSpec((seq, HEAD_W), lambda h, b, i: (b, kc0 + h // group)),
                  pl.BlockSpec((seq, HEAD_W), lambda h, b, i: (b, vc0 + h // group))] + x_spec,
        out_specs=[q_spec] + x_spec, scratch_shapes=x_scr, compiler_params=_params(sem), name=name)(q, k, v, *x_in)
    return res[0] if rider is None else res


def _attn_bwd(name, q, k, kc0, v, vc0, o, do, *, heads, group, nseq, seq, tq=256, rider=None):
    tq = min(tq, seq)
    nq = seq // tq
    hk = heads // group
    t = q.shape[0]
    grid = (hk, nseq, group, nq)

    def body(q_ref, k_ref, v_ref, o_ref, do_ref, dq_ref, dk_ref, dv_ref, dk_acc, dv_acc):
        g, i = pl.program_id(2), pl.program_id(3)
        qv, kv, vv, dov = q_ref[...], k_ref[...].astype(BF16), v_ref[...].astype(BF16), do_ref[...]
        s = lax.dot_general(qv, kv, (((1,), (1,)), ((), ())), preferred_element_type=F32)
        p = jnp.exp(s - jnp.max(s, axis=-1, keepdims=True))
        pn = p * (1.0 / jnp.sum(p, axis=-1, keepdims=True))
        dp = lax.dot_general(dov, vv, (((1,), (1,)), ((), ())), preferred_element_type=F32)
        delta = jnp.sum(dov.astype(F32) * o_ref[...].astype(F32), axis=-1, keepdims=True)
        ds = (pn * (dp - delta)).astype(BF16)
        dq_ref[...] = jnp.dot(ds, kv, preferred_element_type=F32)
        dk_part = lax.dot_general(ds, qv, (((0,), (0,)), ((), ())), preferred_element_type=F32)
        dv_part = lax.dot_general(pn.astype(BF16), dov, (((0,), (0,)), ((), ())), preferred_element_type=F32)
        first = jnp.logical_and(g == 0, i == 0)

        @pl.when(first)
        def _():
            dk_acc[...] = dk_part
            dv_acc[...] = dv_part

        @pl.when(jnp.logical_not(first))
        def _():
            dk_acc[...] += dk_part
            dv_acc[...] += dv_part

        @pl.when(jnp.logical_and(g == group - 1, i == nq - 1))
        def _():
            dk_ref[...] = dk_acc[...].astype(dk_ref.dtype)
            dv_ref[...] = dv_acc[...].astype(dv_ref.dtype)

    if rider is not None:
        rider.n_core_out, rider.n_core_scratch = 3, 2
    body, x_in, x_spec, x_out, x_scr = _ride(body, grid, rider)
    q_spec = pl.BlockSpec((tq, HEAD_W), lambda kh, b, g, i: (b * nq + i, kh * group + g))
    kv_out = pl.BlockSpec((seq, HEAD_W), lambda kh, b, g, i: (b, kh))
    sem = ("parallel", "parallel", "arbitrary", "arbitrary") if rider is None else ("arbitrary",) * 4
    return pl.pallas_call(
        body,
        out_shape=[jax.ShapeDtypeStruct(q.shape, F32), jax.ShapeDtypeStruct((t, hk * HEAD_W), BF16),
                   jax.ShapeDtypeStruct((t, hk * HEAD_W), BF16)] + x_out,
        grid=grid,
        in_specs=[q_spec, pl.BlockSpec((seq, HEAD_W), lambda kh, b, g, i: (b, kc0 + kh)),
                  pl.BlockSpec((seq, HEAD_W), lambda kh, b, g, i: (b, vc0 + kh)), q_spec, q_spec] + x_spec,
        out_specs=[q_spec, kv_out, kv_out] + x_spec,
        scratch_shapes=[pltpu.VMEM((seq, HEAD_W), F32), pltpu.VMEM((seq, HEAD_W), F32)] + x_scr,
        compiler_params=_params(sem), name=name)(q, k, v, o, do, *x_in)


def _place():
    return lax.axis_index("x"), lax.axis_index("y"), lax.axis_index("c")


def _other_chips(x, y):
    return [(1 - x, y), (x, 1 - y), (1 - x, 1 - y)]


class _Exchange:
    def __init__(self, kind, src):
        assert kind in ("gather", "scatter")
        self.kind, self.src = kind, src
        rows, w = src.shape[-2:]
        self.out_shape = jax.ShapeDtypeStruct((4, rows, w), src.dtype)
        self.scratch = [pltpu.SemaphoreType.DMA((3,)), pltpu.SemaphoreType.DMA((3,)), pltpu.SemaphoreType.DMA(())]
        self.n_core_out = self.n_core_scratch = 0

    def _copies(self, src_ref, out_ref, send_sems, recv_sems, landing):
        x, y, c = _place()

        def remote(k, s, d, to):
            return pltpu.make_async_remote_copy(src_ref=s, dst_ref=d, send_sem=send_sems.at[k], recv_sem=recv_sems.at[k],
                                                device_id=to, device_id_type=MESH)

        me = 2 * x + y
        part = (lambda j: src_ref) if self.kind == "gather" else (lambda j: src_ref.at[j])
        if landing:
            return [remote(k, part(me), out_ref.at[2 * px + py], (px, py, c)) for k, (px, py) in enumerate(_other_chips(x, y))]
        return [remote(k, part(2 * px + py), out_ref.at[me], (px, py, c)) for k, (px, py) in enumerate(_other_chips(x, y))]

    def _local(self, src_ref, out_ref, local_sem):
        x, y, _ = _place()
        me = 2 * x + y
        return pltpu.make_async_copy(src_ref if self.kind == "gather" else src_ref.at[me], out_ref.at[me], local_sem)

    def start(self, src_ref, out_ref, send_sems, recv_sems, local_sem):
        self._local(src_ref, out_ref, local_sem).start()
        for mine in self._copies(src_ref, out_ref, send_sems, recv_sems, False):
            mine.start()

    def finish(self, src_ref, out_ref, send_sems, recv_sems, local_sem):
        for landed in self._copies(src_ref, out_ref, send_sems, recv_sems, True):
            landed.wait_recv()
        for mine in self._copies(src_ref, out_ref, send_sems, recv_sems, False):
            mine.wait_send()
        self._local(src_ref, out_ref, local_sem).wait()


def _exchange(name, kind, src):
    ex = _Exchange(kind, src)

    def body(src_ref, out_ref, *sems):
        ex.start(src_ref, out_ref, *sems)
        ex.finish(src_ref, out_ref, *sems)

    hbm = pl.BlockSpec(memory_space=pl.ANY)
    return pl.pallas_call(body, out_shape=ex.out_shape, in_specs=[hbm], out_specs=hbm, scratch_shapes=ex.scratch, name=name)(src)


def _adamw(w, g, m, v):
    m = ADAM_B1 * m + (1.0 - ADAM_B1) * g
    v = ADAM_B2 * v + (1.0 - ADAM_B2) * (g * g)
    delta = -ADAM_LR * ((m / M_HAT_DIV) / (jnp.sqrt(v / V_HAT_DIV) + ADAM_EPS) + ADAM_WD * w)
    return delta, m, v


def _small_allreduce_adamw(part, w, m, v):
    def body(part_ref, w_ref, m_ref, v_ref, g_out, d_out, m_out, v_out, loss_out, buf, send_sems, recv_sems):
        x, y, c = _place()
        me = 4 * x + 2 * y + c
        buf[me] = part_ref[...]

        def flip(k):
            fx, fy, fc = (k >> 2) & 1, (k >> 1) & 1, k & 1
            px, py, pc = (1 - x if fx else x), (1 - y if fy else y), (1 - c if fc else c)
            return (px, py, pc), 4 * px + 2 * py + pc

        def copy(k, slot):
            return pltpu.make_async_remote_copy(
                src_ref=part_ref, dst_ref=buf.at[slot], send_sem=send_sems.at[k - 1], recv_sem=recv_sems.at[k - 1],
                device_id=flip(k)[0], device_id_type=MESH)

        sent = [copy(k, me) for k in range(1, 8)]
        for cp in sent:
            cp.start()
        for k in range(1, 8):
            copy(k, flip(k)[1]).wait_recv()
        for cp in sent:
            cp.wait_send()
        tot = buf[0]
        for j in range(1, 8):
            tot = tot + buf[j]
        delta, m_new, v_new = _adamw(w_ref[...], tot, m_ref[...], v_ref[...])
        g_out[...] = tot
        d_out[...] = delta
        m_out[...] = m_new
        v_out[...] = v_new
        loss_out[...] = jnp.sum(tot[LOSS_ROW0:LOSS_ROW0 + 8, :]).reshape(1, 1)

    vm = pl.BlockSpec(memory_space=pltpu.VMEM)
    shp = jax.ShapeDtypeStruct((SMALL_ROWS, 128), F32)
    return pl.pallas_call(
        body, out_shape=[shp, shp, shp, shp, jax.ShapeDtypeStruct((1, 1), F32)],
        in_specs=[vm, vm, vm, vm], out_specs=[vm, vm, vm, vm, vm],
        scratch_shapes=[pltpu.VMEM((8, SMALL_ROWS, 128), F32), pltpu.SemaphoreType.DMA((7,)), pltpu.SemaphoreType.DMA((7,))],
        name="small_allreduce_adamw")(part, w, m, v)


def _row_tile(rows, cap):
    return max(t for t in range(16, min(rows, cap) + 1, 16) if rows % t == 0)


def _reduce_pair(name, parts):
    _, rows, w = parts.shape
    tr = _row_tile(rows, 576)
    nt = rows // tr

    def body(p_ref, o_ref, mine, theirs, send_sems, recv_sems):
        i = pl.program_id(0)
        x, y, c = _place()

        def copy(t):
            rows_t = pl.ds(pl.multiple_of(t * tr, tr), tr)
            return pltpu.make_async_remote_copy(src_ref=mine.at[rows_t], dst_ref=theirs.at[rows_t], send_sem=send_sems.at[t],
                                                recv_sem=recv_sems.at[t], device_id=(x, y, 1 - c), device_id_type=MESH)

        @pl.when(i < nt)
        def _():
            mine[pl.ds(pl.multiple_of(i * tr, tr), tr), :] = (
                (p_ref[0].astype(F32) + p_ref[1].astype(F32)) + p_ref[2].astype(F32)) + p_ref[3].astype(F32)
            copy(i).start()

        @pl.when(i >= nt)
        def _():
            t = i - nt
            copy(t).wait()
            rows_t = pl.ds(pl.multiple_of(t * tr, tr), tr)
            o_ref[...] = mine[rows_t, :] + theirs[rows_t, :]

    return pl.pallas_call(
        body, out_shape=jax.ShapeDtypeStruct((rows, w), F32), grid=(2 * nt,),
        in_specs=[pl.BlockSpec((4, tr, w), lambda i: (0, jnp.minimum(i, nt - 1), 0))],
        out_specs=pl.BlockSpec((tr, w), lambda i: (jnp.maximum(i - nt, 0), 0)),
        scratch_shapes=[pltpu.VMEM((rows, w), F32), pltpu.VMEM((rows, w), F32), pltpu.SemaphoreType.DMA((nt,)),
                        pltpu.SemaphoreType.DMA((nt,))],
        compiler_params=_params(("arbitrary",)), name=name)(parts)


def _adamw_shard(name, g, w, m, v):
    rows, cols = w.shape
    tr = _row_tile(rows, 256)

    def body(g_ref, w_ref, m_ref, v_ref, d_out, m_out, v_out):
        delta, m_new, v_new = _adamw(w_ref[...], g_ref[...], m_ref[...], v_ref[...])
        d_out[...] = delta
        m_out[...] = m_new
        v_out[...] = v_new

    t_spec = pl.BlockSpec((tr, cols), lambda i: (i, 0))
    shp = jax.ShapeDtypeStruct((rows, cols), F32)
    return pl.pallas_call(body, out_shape=[shp] * 3, grid=(rows // tr,), in_specs=[t_spec] * 4, out_specs=[t_spec] * 3,
                          compiler_params=_params(("parallel",)), name=name)(g, w, m, v)


def _shard_shape(name):
    _, r, c, ax = BIG_BY_NAME[name]
    return (r, c // 4) if ax == 1 else (r // 4, c)


def _pad_rows(a, axis):
    pad = [(0, 0)] * a.ndim
    pad[axis] = (0, -a.shape[axis] % PACK_ALIGN)
    return jnp.pad(a, pad)


def _pack_shards(names, shards, dtype):
    return _pad_rows(jnp.concatenate([s.astype(dtype).reshape(-1, PACK_W) for s in shards], axis=0), 0)


def _unpack_shards(names, slab):
    out, off = [], 0
    for name in names:
        rs, cs = _shard_shape(name)
        n = rs * cs // PACK_W
        out.append(slab[off:off + n].reshape(rs, cs))
        off += n
    return out


def _unpack_full(names, slabs):
    out, off = [], 0
    for name in names:
        _, r, c, ax = BIG_BY_NAME[name]
        n = r * c // 4 // PACK_W
        seg = slabs[:, off:off + n]
        out.append(seg.reshape(4, r, c // 4).transpose(1, 0, 2).reshape(r, c) if ax == 1 else seg.reshape(r, c))
        off += n
    return out


def _pack_full(names, mats, dtype):
    segs = []
    for name, a in zip(names, mats):
        _, r, c, ax = BIG_BY_NAME[name]
        a = a.astype(dtype)
        a = a.reshape(r, 4, c // 4).transpose(1, 0, 2) if ax == 1 else a
        segs.append(a.reshape(4, -1, PACK_W))
    return _pad_rows(jnp.concatenate(segs, axis=1), 1)


def _pad_heads_cols(wm, heads, d):
    k = wm.shape[0]
    return jnp.pad(wm.reshape(k, heads, d), ((0, 0), (0, 0), (0, HEAD_W - d))).reshape(k, heads * HEAD_W)


def _unpad_heads_cols(wm, heads, d):
    k = wm.shape[0]
    return wm.reshape(k, heads, HEAD_W)[:, :, :d].reshape(k, heads * d)


def _win_ext(w_in):
    o = np.cumsum([0, Q_LORA, KV_LORA, QK_ROPE, H_B * HD_B, KV_B * HD_B, KV_B * HD_B, D_MODEL, D_MODEL])
    pc = lambda a, n: jnp.pad(a, ((0, 0), (0, n - a.shape[1])))
    return jnp.concatenate([
        w_in[:, o[0]:o[1]], w_in[:, o[1]:o[2]], pc(w_in[:, o[2]:o[3]], HEAD_W),
        _pad_heads_cols(w_in[:, o[3]:o[4]], H_B, HD_B), _pad_heads_cols(w_in[:, o[4]:o[5]], KV_B, HD_B),
        _pad_heads_cols(w_in[:, o[5]:o[6]], KV_B, HD_B), w_in[:, o[6]:o[7]], w_in[:, o[7]:o[8]]], axis=1)


def _win_unext(we):
    c = HEAD_W
    return jnp.concatenate([
        we[:, :ZC_CKV * c], we[:, ZC_CKV * c:ZC_KPE * c], we[:, ZC_KPE * c:ZC_KPE * c + QK_ROPE],
        _unpad_heads_cols(we[:, ZC_QB * c:ZC_KB * c], H_B, HD_B), _unpad_heads_cols(we[:, ZC_KB * c:ZC_VB * c], KV_B, HD_B),
        _unpad_heads_cols(we[:, ZC_VB * c:ZC_GA * c], KV_B, HD_B), we[:, ZC_GA * c:]], axis=1)


def _wkv_ext(w_kvb):
    wk = w_kvb.reshape(KV_LORA, H_A, QK_NOPE + V_DIM_A)
    k_cols = jnp.pad(wk[:, :, :QK_NOPE], ((0, 0), (0, 0), (0, HEAD_W - QK_NOPE))).reshape(KV_LORA, H_A * HEAD_W)
    v_cols = jnp.pad(wk[:, :, QK_NOPE:], ((0, 0), (0, 0), (0, HEAD_W - V_DIM_A))).reshape(KV_LORA, H_A * HEAD_W)
    eye = jnp.pad(jnp.eye(QK_ROPE, dtype=w_kvb.dtype), ((0, 0), (QK_NOPE, HEAD_W - QK_NOPE - QK_ROPE)))
    pe_rows = jnp.concatenate([jnp.tile(eye, (1, H_A)), jnp.zeros((QK_ROPE, H_A * HEAD_W), w_kvb.dtype)], axis=1)
    top = jnp.concatenate([k_cols, v_cols], axis=1)
    return jnp.concatenate([top, pe_rows, jnp.zeros((2 * HEAD_W - KV_LORA - QK_ROPE, 2 * H_A * HEAD_W), w_kvb.dtype)], axis=0)


def _wkv_unext(we):
    k_cols = we[:KV_LORA, :H_A * HEAD_W].reshape(KV_LORA, H_A, HEAD_W)[:, :, :QK_NOPE]
    v_cols = we[:KV_LORA, H_A * HEAD_W:].reshape(KV_LORA, H_A, HEAD_W)[:, :, :V_DIM_A]
    return jnp.concatenate([k_cols, v_cols], axis=2).reshape(KV_LORA, H_A * (QK_NOPE + V_DIM_A))


def _pad_heads_rows(wm, heads, d):
    n = wm.shape[1]
    return jnp.pad(wm.reshape(heads, d, n), ((0, 0), (0, HEAD_W - d), (0, 0))).reshape(heads * HEAD_W, n)


def _unpad_heads_rows(wm, heads, d):
    n = wm.shape[1]
    return wm.reshape(heads, HEAD_W, n)[:, :d].reshape(heads * d, n)


def _rope_tables(seq):
    def ang(pos, dim):
        inv = ROPE_THETA ** (-jnp.arange(0, dim, 2, dtype=F32) / dim)
        return pos.astype(F32)[:, None] * inv[None, :]

    def rot(dim):
        r = np.zeros((dim, dim), np.float32)
        half = dim // 2
        r[np.arange(half) + half, np.arange(half)] = -1.0
        r[np.arange(half), np.arange(half) + half] = 1.0
        return r

    def table(blocks):
        first, width = blocks[0][0], sum(2 * a.shape[1] for _, a in blocks)
        ones = lambda n: jnp.ones((seq, n), F32)
        cos = jnp.concatenate([ones(first)] + [jnp.cos(a) for _, a in blocks for _ in (0, 1)] + [ones(HEAD_W - first - width)], axis=1)
        sin = jnp.concatenate([0.0 * ones(first)] + [jnp.sin(a) for _, a in blocks for _ in (0, 1)]
                              + [0.0 * ones(HEAD_W - first - width)], axis=1)
        pm = np.zeros((HEAD_W, HEAD_W), np.float32)
        for c0, a in blocks:
            d = 2 * a.shape[1]
            pm[c0:c0 + d, c0:c0 + d] = rot(d)
        return cos, sin, jnp.asarray(pm, BF16), jnp.asarray(pm.T, BF16)

    tok = jnp.arange(seq)
    a1 = ang(tok, QK_ROPE)
    arow, acol = ang(tok // GRID_W, HD_B // 2), ang(tok % GRID_W, HD_B // 2)
    return table([(QK_NOPE, a1)]), table([(0, a1)]), table([(0, arow), (HD_B // 2, acol)])


def _local_step(x, p, tgt, gains, wts, ride=None):
    nb, seq, _ = x.shape
    t = nb * seq
    x0 = x.reshape(t, D_MODEL)
    p2 = p.reshape(t, PLE_DIM)
    tg = tgt.reshape(t, D_MODEL)
    (cq_t, sq_t, pq, pq_t), (ck_t, sk_t, pk, pk_t), (cb_t, sb_t, pb, pb_t) = _rope_tables(seq)
    padg = lambda g: jnp.pad(g, ((0, 0), (0, HEAD_W - g.shape[1])))
    g_qn, g_kn = padg(gains["g_qn"]), padg(gains["g_kn"])

    win = _win_ext(wts["w_in"])
    wqb = _pad_heads_cols(wts["w_qb"], H_A, QK_NOPE + QK_ROPE)
    wkv = _wkv_ext(wts["w_kvb"])

    norm = lambda n: (lambda v, g: (_rms(v, g, n),))
    full = lambda a: (a, a.shape[1], 0, False)

    h = _rowwise("norm_mix", norm(D_MODEL), [full(x0)], [(D_MODEL, BF16, D_MODEL, False)], consts=[gains["g_mix"]])
    z = _mm("in_proj", h, win, tn=2048)
    cq = _rowwise("norm_qa", norm(Q_LORA), [(z, Q_LORA, 0, False)], [(Q_LORA, BF16, Q_LORA, False)], consts=[gains["g_qa"]])
    qa = _mm("q_up", cq, wqb)

    def rope_fwd(scale):
        return lambda v, cos, sin, pm: ((v * cos + _perm(v, pm) * sin) * scale,)

    q_a = _rowwise("rope_qa", rope_fwd(SCALE_A), [(qa, HEAD_W, 0, True)], [(H_A * HEAD_W, BF16, HEAD_W, True)],
                   pos=[cq_t, sq_t], consts=[pq], heads=H_A, seq=seq)
    ckv = _rowwise("norm_kva", norm(KV_LORA), [(z, HEAD_W, ZC_CKV, False)], [(HEAD_W, BF16, HEAD_W, False)], consts=[gains["g_kva"]])
    kpe = _rowwise("rope_kpe", rope_fwd(1.0), [(z, HEAD_W, ZC_KPE, False)], [(HEAD_W, BF16, HEAD_W, False)],
                   pos=[ck_t, sk_t], consts=[pk], seq=seq)
    kin = jnp.concatenate([ckv, kpe], axis=1)
    kv_a = _mm("kv_up", kin, wkv, out_dtypes=(BF16,))
    o_a = _attn_fwd("attn_a_fwd", q_a, kv_a, 0, kv_a, H_A, heads=H_A, group=1, nseq=nb, seq=seq,
                    rider=ride and ride["gather_a"])

    def prep_fwd(scale):
        def fn(v, cos, sin, g, pm):
            yv = _rms(v, g, HD_B)
            return ((yv * cos + _perm(yv, pm) * sin) * scale,)
        return fn

    q_b = _rowwise("prep_qb", prep_fwd(SCALE_B), [(z, HEAD_W, ZC_QB, True)], [(H_B * HEAD_W, BF16, HEAD_W, True)],
                   pos=[cb_t, sb_t], consts=[g_qn, pb], heads=H_B, seq=seq)
    k_b = _rowwise("prep_kb", prep_fwd(1.0), [(z, HEAD_W, ZC_KB, True)], [(KV_B * HEAD_W, BF16, HEAD_W, True)],
                   pos=[cb_t, sb_t], consts=[g_kn, pb], heads=KV_B, seq=seq)
    o_b = _attn_fwd("attn_b_fwd", q_b, k_b, 0, z, ZC_VB, heads=H_B, group=H_B // KV_B, nseq=nb, seq=seq,
                    rider=ride and ride["gather_b"])
    if ride is not None:
        (o_a, got_a), (o_b, got_b) = o_a, o_b
        wts = {**wts, **ride["late_weights"](got_a, got_b)}
    woa = _pad_heads_rows(wts["w_oa"], H_A, V_DIM_A)
    wob = _pad_heads_rows(wts["w_ob"], H_B, HD_B)
    wo, wup, wdown, wpg, wple = wts["w_o"], wts["w_up"], wts["w_down"], wts["w_ple_gate"], wts["w_ple"]
    ya = _mm("out_a", o_a, woa)
    yb = _mm("out_b", o_b, wob)

    z_ga, z_gb = (z, D_MODEL, ZC_GA // 8, False), (z, D_MODEL, ZC_GB // 8, False)
    merged = _rowwise("merge", lambda ga, gb, a, b: (_sigmoid(ga) * a + _sigmoid(gb) * b,),
                      [z_ga, z_gb, full(ya), full(yb)], [(D_MODEL, BF16, D_MODEL, False)])
    x1 = _mm("out_proj", merged, wo, epi=lambda acc, r: (r + acc,), extras=(x0,))
    h2 = _rowwise("norm_mlp", norm(D_MODEL), [full(x1)], [(D_MODEL, BF16, D_MODEL, False)], consts=[gains["g_mlp"]])

    def relu2(acc):
        u = jnp.maximum(acc, 0.0)
        return u, u * u

    u, usq = _mm("mlp_up", h2, wup, out_dtypes=(BF16, BF16), epi=relu2, tn=2048)
    x2 = _mm("mlp_down", usq, wdown, epi=lambda acc, r: (r + acc,), extras=(x1,))
    h3 = _rowwise("norm_ple", norm(D_MODEL), [full(x2)], [(D_MODEL, BF16, D_MODEL, False)], consts=[gains["g_ple"]])
    gpre = _mm("ple_gate", h3, wpg)
    pe = _mm("ple_proj", p2, wple)

    def tail(x2v, gp, pev, tv, gf):
        sg = _sigmoid(gp)
        x3 = x2v + sg * pev
        rs = lax.rsqrt(jnp.sum(x3 * x3, axis=-1, keepdims=True) * (1.0 / D_MODEL) + EPS)
        xh = x3 * rs
        err = xh * gf - tv
        dy = err * (1.0 / D_MODEL)
        dyg = dy * gf
        dx3 = rs * (dyg - xh * (jnp.sum(dyg * xh, axis=-1, keepdims=True) * (1.0 / D_MODEL)))
        return (dx3, dx3 * pev * sg * (1.0 - sg), dx3 * sg,
                jnp.sum(err * err, axis=0, keepdims=True) * (0.5 / D_MODEL), jnp.sum(dy * xh, axis=0, keepdims=True))

    dx3, dgpre, dpe, loss_part, dg_final = _rowwise(
        "tail", tail, [full(x2), full(gpre), full(pe), full(tg)],
        [(D_MODEL, F32, D_MODEL, False), (D_MODEL, BF16, D_MODEL, False), (D_MODEL, BF16, D_MODEL, False)],
        consts=[gains["g_final"].reshape(1, D_MODEL)], accs=[(1, D_MODEL), (1, D_MODEL)], tm=256)

    def norm_bwd(n, with_res):
        if with_res:
            def fn(dh, v, res, g):
                dx, dg = _rms_bwd(dh, v, g, n)
                return dx + res, dg
        else:
            def fn(dh, v, g):
                return _rms_bwd(dh, v, g, n)
        return fn

    dw = {}
    dw["w_ple"] = _mm_tn("dw_ple", p2, dpe)
    dw["w_ple_gate"] = _mm_tn("dw_ple_gate", h3, dgpre)
    dh3 = _mm("d_ple_gate", dgpre, wpg, trans_b=True)
    dx2, dg_ple = _rowwise("norm_ple_bwd", norm_bwd(D_MODEL, True), [full(dh3), full(x2), full(dx3)],
                           [(D_MODEL, F32, D_MODEL, False)], consts=[gains["g_ple"]], accs=[(1, D_MODEL)], tm=256)
    dw["w_down"] = _mm_tn("dw_down", usq, dx2)
    dupre = _mm("d_mlp_down", dx2, wdown, trans_b=True, out_dtypes=(BF16,), epi=lambda acc, uv: (acc * (2.0 * uv.astype(F32)),),
                extras=(u,), tn=2048)
    dw["w_up"] = _mm_tn("dw_up", h2, dupre)
    dh2 = _mm("d_mlp_up", dupre, wup, trans_b=True)
    dx1, dg_mlp = _rowwise("norm_mlp_bwd", norm_bwd(D_MODEL, True), [full(dh2), full(x1), full(dx2)],
                           [(D_MODEL, F32, D_MODEL, False)], consts=[gains["g_mlp"]], accs=[(1, D_MODEL)], tm=256)
    dw["w_o"] = _mm_tn("dw_o", merged, dx1)
    dmerged = _mm("d_out_proj", dx1, wo, trans_b=True)

    def merge_bwd(dm, ga, gb, a, b):
        sa, sb = _sigmoid(ga), _sigmoid(gb)
        return dm * sa, dm * sb, dm * a * sa * (1.0 - sa), dm * b * sb * (1.0 - sb)

    dya, dyb, dga, dgb = _rowwise("merge_bwd", merge_bwd, [full(dmerged), z_ga, z_gb, full(ya), full(yb)],
                                  [(D_MODEL, BF16, D_MODEL, False)] * 4, tm=256)
    dw["w_oa"] = _unpad_heads_rows(_mm_tn("dw_oa", o_a, dya), H_A, V_DIM_A)
    dw["w_ob"] = _unpad_heads_rows(_mm_tn("dw_ob", o_b, dyb), H_B, HD_B)
    do_a = _mm("d_out_a", dya, woa, trans_b=True, out_dtypes=(BF16,))
    do_b = _mm("d_out_b", dyb, wob, trans_b=True, out_dtypes=(BF16,))
    res_a = _attn_bwd("attn_a_bwd", q_a, kv_a, 0, kv_a, H_A, o_a, do_a, heads=H_A, group=1, nseq=nb, seq=seq,
                      rider=ride and ride["scatter_late"](dw))
    dq_a, dk_a, dv_a = res_a[:3]
    dq_b, dk_b, dv_b = _attn_bwd("attn_b_bwd", q_b, k_b, 0, z, ZC_VB, o_b, do_b, heads=H_B, group=H_B // KV_B, nseq=nb, seq=seq)
    if ride is not None:
        ride["out"]["parts_late"] = res_a[3]

    def rope_bwd(scale):
        return lambda d, cos, sin, pm_t: ((d * cos + _perm(d * sin, pm_t)) * scale,)

    dqa = _rowwise("rope_qa_bwd", rope_bwd(SCALE_A), [(dq_a, HEAD_W, 0, True)], [(H_A * HEAD_W, BF16, HEAD_W, True)],
                   pos=[cq_t, sq_t], consts=[pq_t], heads=H_A, seq=seq)
    dw["w_qb"] = _unpad_heads_cols(_mm_tn("dw_qb", cq, dqa), H_A, QK_NOPE + QK_ROPE)
    dcq = _mm("d_q_up", dqa, wqb, trans_b=True)
    dq_lat, dg_qa = _rowwise("norm_qa_bwd", norm_bwd(Q_LORA, False), [full(dcq), (z, Q_LORA, 0, False)],
                             [(Q_LORA, BF16, Q_LORA, False)], consts=[gains["g_qa"]], accs=[(1, Q_LORA)])
    dkv_a = jnp.concatenate([dk_a, dv_a], axis=1)
    dw["w_kvb"] = _wkv_unext(_mm_tn("dw_kv", kin, dkv_a))
    dkin = _mm("d_kv_up", dkv_a, wkv, trans_b=True)
    dckv, dg_kva = _rowwise("norm_kva_bwd", norm_bwd(KV_LORA, False), [(dkin, HEAD_W, 0, False), (z, HEAD_W, ZC_CKV, False)],
                            [(HEAD_W, BF16, HEAD_W, False)], consts=[gains["g_kva"]], accs=[(1, KV_LORA)])
    dkpe = _rowwise("rope_kpe_bwd", rope_bwd(1.0), [(dkin, HEAD_W, 1, False)], [(HEAD_W, BF16, HEAD_W, False)],
                    pos=[ck_t, sk_t], consts=[pk_t], seq=seq)

    def prep_bwd(scale):
        def fn(d, v, cos, sin, g, pm_t):
            dyv = (d * cos + _perm(d * sin, pm_t)) * scale
            return _rms_bwd(dyv, v, g, HD_B)
        return fn

    dqb, dg_qn = _rowwise("prep_qb_bwd", prep_bwd(SCALE_B), [(dq_b, HEAD_W, 0, True), (z, HEAD_W, ZC_QB, True)],
                          [(H_B * HEAD_W, BF16, HEAD_W, True)], pos=[cb_t, sb_t], consts=[g_qn, pb_t], accs=[(1, HEAD_W)],
                          heads=H_B, seq=seq)
    dkb, dg_kn = _rowwise("prep_kb_bwd", prep_bwd(1.0), [(dk_b, HEAD_W, 0, True), (z, HEAD_W, ZC_KB, True)],
                          [(KV_B * HEAD_W, BF16, HEAD_W, True)], pos=[cb_t, sb_t], consts=[g_kn, pb_t], accs=[(1, HEAD_W)],
                          heads=KV_B, seq=seq)

    dz = jnp.concatenate([dq_lat, dckv, dkpe, dqb, dkb, dv_b, dga, dgb], axis=1)
    dw["w_in"] = _win_unext(_mm_tn("dw_in", h, dz))
    dh = _mm("d_in_proj", dz, win, trans_b=True, rider=ride and ride["scatter_early"](dw))
    if ride is not None:
        dh, ride["out"]["parts_early"] = dh
    dx0, dg_mix = _rowwise("norm_mix_bwd", norm_bwd(D_MODEL, True), [full(dh), full(x0), full(dx1)],
                           [(D_MODEL, F32, D_MODEL, False)], consts=[gains["g_mix"]], accs=[(1, D_MODEL)], tm=256)

    dg = {"g_mix": dg_mix, "g_qa": dg_qa, "g_kva": dg_kva, "g_qn": dg_qn[:, :HD_B], "g_kn": dg_kn[:, :HD_B],
          "g_mlp": dg_mlp, "g_ple": dg_ple, "g_final": dg_final}
    return loss_part, dx0.reshape(nb, seq, D_MODEL), dg, dw


def _pack_small(vals, loss_part=None):
    flat = jnp.concatenate([vals[n].reshape(1, -1) for n, _ in SMALL], axis=1)
    loss = jnp.zeros((1, 8 * 128), F32) if loss_part is None else loss_part
    gap = jnp.zeros((1, LOSS_ROW0 * 128 - SMALL_N), F32)
    return jnp.concatenate([flat, gap, loss], axis=1).reshape(SMALL_ROWS, 128)


def _unpack_small(slab, like):
    flat, out, off = slab.reshape(-1), {}, 0
    for n, k in SMALL:
        out[n] = flat[off:off + k].reshape(like[n].shape)
        off += k
    return out


def kernel(x, p, g_mix, w_in, g_qa, w_qb, g_kva, w_kvb, g_qn, g_kn, w_oa, w_ob, w_o, g_mlp, w_up, w_down, g_ple, w_ple_gate, w_ple, g_final, loss_target, m_g_mix, m_w_in, m_g_qa, m_w_qb, m_g_kva, m_w_kvb, m_g_qn, m_g_kn, m_w_oa, m_w_ob, m_w_o, m_g_mlp, m_w_up, m_w_down, m_g_ple, m_w_ple_gate, m_w_ple, m_g_final, v_g_mix, v_w_in, v_g_qa, v_w_qb, v_g_kva, v_w_kvb, v_g_qn, v_g_kn, v_w_oa, v_w_ob, v_w_o, v_g_mlp, v_w_up, v_w_down, v_g_ple, v_w_ple_gate, v_w_ple, v_g_final):
    given = dict(locals())
    order = ["g_mix", "w_in", "g_qa", "w_qb", "g_kva", "w_kvb", "g_qn", "g_kn", "w_oa", "w_ob", "w_o", "g_mlp", "w_up",
             "w_down", "g_ple", "w_ple_gate", "w_ple", "g_final"]
    big_names = [n for n, _, _, _ in BIG]
    local = lambda prefix, names: [given[prefix + n][0] for n in names]
    late = LATE_A + LATE_B

    early_w = _exchange("weight_gather_early", "gather", _pack_shards(EARLY, local("", EARLY), BF16))
    wts = dict(zip(EARLY, _unpack_full(EARLY, early_w)))
    gains = {n: given[n].reshape(1, -1) for n, _ in SMALL}
    ride = {
        "gather_a": _Exchange("gather", _pack_shards(LATE_A, local("", LATE_A), BF16)),
        "gather_b": _Exchange("gather", _pack_shards(LATE_B, local("", LATE_B), BF16)),
        "late_weights": lambda ga, gb: {**dict(zip(LATE_A, _unpack_full(LATE_A, ga))), **dict(zip(LATE_B, _unpack_full(LATE_B, gb)))},
        "scatter_late": lambda dw: _Exchange("scatter", _pack_full(late, [dw[n] for n in late], BF16)),
        "scatter_early": lambda dw: _Exchange("scatter", _pack_full(EARLY, [dw[n] for n in EARLY], BF16)),
        "out": {},
    }
    loss_part, grad_x, dg, dw = _local_step(x, p[0], loss_target, gains, wts, ride)

    small = lambda prefix: _pack_small({n: given[prefix + n] for n, _ in SMALL})
    g_s, d_s, m_s, v_s, loss = _small_allreduce_adamw(_pack_small(dg, loss_part), small(""), small("m_"), small("v_"))

    grads = dict(zip(late, _unpack_shards(late, _reduce_pair("grad_reduce_late", ride["out"]["parts_late"]))))
    grads.update(zip(EARLY, _unpack_shards(EARLY, _reduce_pair("grad_reduce_early", ride["out"]["parts_early"]))))

    res = {}
    for key, slab in (("grad_", g_s), ("delta_", d_s), ("new_m_", m_s), ("new_v_", v_s)):
        for n, val in _unpack_small(slab, given).items():
            res[key + n] = val
    for n in big_names:
        d_w, m_w, v_w = _adamw_shard("adamw_" + n, grads[n], given[n][0], given["m_" + n][0], given["v_" + n][0])
        res["grad_" + n], res["delta_" + n], res["new_m_" + n], res["new_v_" + n] = grads[n][None], d_w[None], m_w[None], v_w[None]
    outs = [loss.reshape(()), grad_x]
    for key in ("grad_", "delta_", "new_m_", "new_v_"):
        outs += [res[key + n] for n in order]
    return tuple(outs)
```

```python
import functools

import numpy as np
import jax
import jax.numpy as jnp
from jax import lax
from jax.experimental import pallas as pl
from jax.experimental.pallas import tpu as pltpu

F32 = jnp.float32
BF16 = jnp.bfloat16
MESH = pl.DeviceIdType.MESH

D_MODEL = 1024
GRID_W = 64
ROPE_THETA = 10000.0
EPS = 1e-6
H_A, QK_NOPE, QK_ROPE, V_DIM_A, Q_LORA, KV_LORA = 8, 64, 32, 64, 256, 128
H_B, KV_B, HD_B = 8, 2, 64
D_FF = 4096
PLE_DIM = 256
HEAD_W = 128
SCALE_A = (QK_NOPE + QK_ROPE) ** -0.5
SCALE_B = HD_B ** -0.5

ADAM_LR, ADAM_B1, ADAM_B2, ADAM_EPS, ADAM_WD, ADAM_STEP = 0.001, 0.9, 0.999, 1e-08, 0.01, 10
M_HAT_DIV = 1.0 - ADAM_B1 ** ADAM_STEP
V_HAT_DIV = 1.0 - ADAM_B2 ** ADAM_STEP

VMEM_LIMIT_BYTES = 56 * 1024 * 1024

ZC_QLAT, ZC_CKV, ZC_KPE, ZC_QB, ZC_KB, ZC_VB, ZC_GA, ZC_GB = 0, 2, 3, 4, 12, 14, 16, 24
Z_WIDTH = 32 * HEAD_W

BIG = [
    ("w_in", 1024, 3232, 1), ("w_qb", 256, 768, 1), ("w_kvb", 128, 1024, 1), ("w_oa", 512, 1024, 1),
    ("w_ob", 512, 1024, 1), ("w_o", 1024, 1024, 0), ("w_up", 1024, 4096, 1), ("w_down", 4096, 1024, 0),
    ("w_ple_gate", 1024, 1024, 0), ("w_ple", 256, 1024, 1),
]
BIG_BY_NAME = {e[0]: e for e in BIG}
PACK_W = 1024
PACK_ALIGN = 64
EARLY = ["w_in", "w_qb", "w_kvb"]
LATE_A = ["w_oa", "w_ob", "w_o", "w_up"]
LATE_B = ["w_down", "w_ple_gate", "w_ple"]

SMALL = [("g_mix", 1024), ("g_qa", 256), ("g_kva", 128), ("g_qn", 64), ("g_kn", 64), ("g_mlp", 1024),
         ("g_ple", 1024), ("g_final", 1024)]
SMALL_N = sum(n for _, n in SMALL)
LOSS_ROW0 = 40
SMALL_ROWS = 48


def _params(sem):
    return pltpu.CompilerParams(dimension_semantics=sem, vmem_limit_bytes=VMEM_LIMIT_BYTES)


def _sigmoid(v):
    return 1.0 / (1.0 + jnp.exp(-v))


def _perm(v, p_ref):
    pm = p_ref[...]
    hi = v.astype(BF16)
    lo = (v - hi.astype(F32)).astype(BF16)
    return (jnp.dot(hi, pm, preferred_element_type=F32) + jnp.dot(lo, pm, preferred_element_type=F32))


def _rms(v, g, n):
    rs = lax.rsqrt(jnp.sum(v * v, axis=-1, keepdims=True) * (1.0 / n) + EPS)
    return v * rs * g


def _rms_bwd(dy, v, g, n):
    rs = lax.rsqrt(jnp.sum(v * v, axis=-1, keepdims=True) * (1.0 / n) + EPS)
    vh = v * rs
    dyg = dy * g
    dx = rs * (dyg - vh * (jnp.sum(dyg * vh, axis=-1, keepdims=True) * (1.0 / n)))
    return dx, jnp.sum(dy * vh, axis=0, keepdims=True)


def _ride(body, grid, rider):
    if rider is None:
        return body, [], [], [], []
    n_sem = len(rider.scratch)

    def wrapped(*refs):
        ids = [pl.program_id(a) for a in range(len(grid))]
        n_in = len(refs) - n_sem - 2 - rider.n_core_out - rider.n_core_scratch
        core_in, src = refs[:n_in], refs[n_in]
        core_out = refs[n_in + 1:n_in + 1 + rider.n_core_out]
        dst = refs[n_in + 1 + rider.n_core_out]
        core_scr = refs[n_in + 2 + rider.n_core_out:len(refs) - n_sem]
        sems = refs[len(refs) - n_sem:]

        @pl.when(functools.reduce(jnp.logical_and, [a == 0 for a in ids]))
        def _():
            rider.start(src, dst, *sems)

        body(*core_in, *core_out, *core_scr)

        @pl.when(functools.reduce(jnp.logical_and, [a == n - 1 for a, n in zip(ids, grid)]))
        def _():
            rider.finish(src, dst, *sems)

    hbm = pl.BlockSpec(memory_space=pl.ANY)
    return wrapped, [rider.src], [hbm], [rider.out_shape], list(rider.scratch)


def _mm(name, a, b, *, trans_b=False, out_dtypes=(F32,), epi=None, extras=(), tm=512, tn=None, rider=None):
    m, k = a.shape
    n = b.shape[0] if trans_b else b.shape[1]
    tn = n if tn is None else min(tn, n)
    tm = min(tm, m)
    assert m % tm == 0 and n % tn == 0 and (b.shape[1] if trans_b else b.shape[0]) == k
    n_ex = len(extras)
    dims = (((1,), (1,)), ((), ())) if trans_b else (((1,), (0,)), ((), ()))

    def body(a_ref, b_ref, *rest):
        acc = lax.dot_general(a_ref[...].astype(BF16), b_ref[...].astype(BF16), dims, preferred_element_type=F32)
        res = (acc,) if epi is None else epi(acc, *[e[...] for e in rest[:n_ex]])
        for o_ref, r in zip(rest[n_ex:], res):
            o_ref[...] = r.astype(o_ref.dtype)

    grid = (n // tn, m // tm)
    if rider is not None:
        rider.n_core_out, rider.n_core_scratch = len(out_dtypes), 0
    body, x_in, x_spec, x_out, x_scr = _ride(body, grid, rider)
    a_spec = pl.BlockSpec((tm, k), lambda j, i: (i, 0))
    b_spec = pl.BlockSpec((tn, k), lambda j, i: (j, 0)) if trans_b else pl.BlockSpec((k, tn), lambda j, i: (0, j))
    t_spec = pl.BlockSpec((tm, tn), lambda j, i: (i, j))
    outs = pl.pallas_call(
        body, out_shape=[jax.ShapeDtypeStruct((m, n), d) for d in out_dtypes] + x_out, grid=grid,
        in_specs=[a_spec, b_spec] + [t_spec] * n_ex + x_spec, out_specs=[t_spec] * len(out_dtypes) + x_spec,
        scratch_shapes=x_scr, compiler_params=_params(("parallel", "parallel") if rider is None else ("arbitrary", "arbitrary")),
        name=name)(a, b, *extras, *x_in)
    return outs[0] if len(outs) == 1 else outs


def _mm_tn(name, a, b, *, tk=1024, tn=1024, tt=1024):
    t, k = a.shape
    n = b.shape[1]
    tk, tn, tt = min(tk, k), min(tn, n), min(tt, t)
    assert b.shape[0] == t and k % tk == 0 and n % tn == 0 and t % tt == 0

    def body(a_ref, b_ref, o_ref):
        part = lax.dot_general(a_ref[...].astype(BF16), b_ref[...].astype(BF16), (((0,), (0,)), ((), ())),
                               preferred_element_type=F32)

        @pl.when(pl.program_id(2) == 0)
        def _():
            o_ref[...] = part

        @pl.when(pl.program_id(2) != 0)
        def _():
            o_ref[...] += part

    return pl.pallas_call(
        body, out_shape=jax.ShapeDtypeStruct((k, n), F32), grid=(k // tk, n // tn, t // tt),
        in_specs=[pl.BlockSpec((tt, tk), lambda ki, ni, ti: (ti, ki)), pl.BlockSpec((tt, tn), lambda ki, ni, ti: (ti, ni))],
        out_specs=pl.BlockSpec((tk, tn), lambda ki, ni, ti: (ki, ni)),
        compiler_params=_params(("parallel", "parallel", "arbitrary")), name=name)(a, b)


def _rowwise(name, fn, ins, outs, *, consts=(), pos=(), accs=(), heads=1, tm=512, seq=None):
    t = ins[0][0].shape[0]
    tm = min(tm, t if seq is None else seq)
    assert t % tm == 0 and (seq is None or seq % tm == 0)
    n_in, n_pos, n_c, n_out, n_acc = len(ins), len(pos), len(consts), len(outs), len(accs)

    def body(*refs):
        vals = [r[...] for r in refs[:n_in + n_pos + n_c]]
        res = fn(*vals)
        o_refs = refs[n_in + n_pos + n_c:]
        for o_ref, r in zip(o_refs[:n_out], res[:n_out]):
            o_ref[...] = r.astype(o_ref.dtype)
        if n_acc:
            first = jnp.logical_and(pl.program_id(0) == 0, pl.program_id(1) == 0)

            @pl.when(first)
            def _():
                for o_ref, r in zip(o_refs[n_out:], res[n_out:]):
                    o_ref[...] = r

            @pl.when(jnp.logical_not(first))
            def _():
                for o_ref, r in zip(o_refs[n_out:], res[n_out:]):
                    o_ref[...] += r

    def tiled(width, c0, per_head):
        return pl.BlockSpec((tm, width), (lambda h, i: (i, c0 + h)) if per_head else (lambda h, i: (i, c0)))

    in_specs = [tiled(w, c0, ph) for _, w, c0, ph in ins]
    if n_pos:
        nblk = seq // tm
        in_specs += [pl.BlockSpec((tm, a.shape[1]), lambda h, i: (i % nblk, 0)) for a in pos]
    in_specs += [pl.BlockSpec(a.shape, lambda h, i: (0, 0)) for a in consts]
    out_specs = [tiled(w, 0, ph) for _, _, w, ph in outs] + [pl.BlockSpec(s, lambda h, i: (0, 0)) for s in accs]
    out_shape = [jax.ShapeDtypeStruct((t, c), d) for c, d, _, _ in outs] + [jax.ShapeDtypeStruct(s, F32) for s in accs]
    sem = ("arbitrary", "arbitrary") if n_acc else ("parallel", "parallel")
    res = pl.pallas_call(body, out_shape=out_shape, grid=(heads, t // tm), in_specs=in_specs, out_specs=out_specs,
                         compiler_params=_params(sem), name=name)(*[a for a, _, _, _ in ins], *pos, *consts)
    return res[0] if len(res) == 1 else res


ATTN_HEADS_PER_STEP = 2


def _attn_fwd(name, q, k, kc0, v, vc0, *, heads, group, nseq, seq, tq=512, rider=None):
    tq = min(tq, seq)
    nq = seq // tq
    hp = ATTN_HEADS_PER_STEP
    grid = (heads // hp, nseq, nq)
    shared = group > 1
    assert group % hp == 0 if shared else (kc0 % hp == 0 and vc0 % hp == 0)

    def body(q_ref, k_ref, v_ref, o_ref, lse_ref):
        for j in range(hp):
            cols = slice(j * HEAD_W, (j + 1) * HEAD_W)
            kj = (k_ref[...] if shared else k_ref[:, cols]).astype(BF16)
            vj = (v_ref[...] if shared else v_ref[:, cols]).astype(BF16)
            s = lax.dot_general(q_ref[:, cols], kj, (((1,), (1,)), ((), ())), preferred_element_type=F32)
            m = jnp.max(s, axis=-1, keepdims=True)
            p = jnp.exp(s - m)
            l = jnp.sum(p, axis=-1, keepdims=True)
            o = jnp.dot(p.astype(BF16), vj, preferred_element_type=F32)
            o_ref[:, cols] = (o * (1.0 / l)).astype(o_ref.dtype)
            lse_ref[j] = m + jnp.log(l)

    if rider is not None:
        rider.n_core_out, rider.n_core_scratch = 2, 0
    body, x_in, x_spec, x_out, x_scr = _ride(body, grid, rider)
    q_spec = pl.BlockSpec((tq, hp * HEAD_W), lambda h, b, i: (b * nq + i, h))
    if shared:
        k_spec = pl.BlockSpec((seq, HEAD_W), lambda h, b, i: (b, kc0 + (h * hp) // group))
        v_spec = pl.BlockSpec((seq, HEAD_W), lambda h, b, i: (b, vc0 + (h * hp) // group))
    else:
        k_spec = pl.BlockSpec((seq, hp * HEAD_W), lambda h, b, i: (b, kc0 // hp + h))
        v_spec = pl.BlockSpec((seq, hp * HEAD_W), lambda h, b, i: (b, vc0 // hp + h))
    lse_spec = pl.BlockSpec((hp, tq, 1), lambda h, b, i: (h, b * nq + i, 0))
    sem = ("parallel",) * 3 if rider is None else ("arbitrary",) * 3
    return pl.pallas_call(
        body, out_shape=[jax.ShapeDtypeStruct(q.shape, BF16), jax.ShapeDtypeStruct((heads, q.shape[0], 1), F32)] + x_out,
        grid=grid, in_specs=[q_spec, k_spec, v_spec] + x_spec, out_specs=[q_spec, lse_spec] + x_spec, scratch_shapes=x_scr,
        compiler_params=_params(sem), name=name)(q, k, v, *x_in)


def _attn_bwd(name, q, k, kc0, v, vc0, o, do, lse, *, heads, group, nseq, seq, tq=256, rider=None):
    tq = min(tq, seq)
    nq = seq // tq
    hk = heads // group
    t = q.shape[0]
    grid = (hk, nseq, group, nq)

    def body(q_ref, k_ref, v_ref, o_ref, do_ref, lse_ref, dq_ref, dk_ref, dv_ref, dk_acc, dv_acc):
        g, i = pl.program_id(2), pl.program_id(3)
        qv, kv, vv, dov = q_ref[...], k_ref[...].astype(BF16), v_ref[...].astype(BF16), do_ref[...]
        s = lax.dot_general(qv, kv, (((1,), (1,)), ((), ())), preferred_element_type=F32)
        pn = jnp.exp(s - lse_ref[...])
        dp = lax.dot_general(dov, vv, (((1,), (1,)), ((), ())), preferred_element_type=F32)
        delta = jnp.sum(dov.astype(F32) * o_ref[...].astype(F32), axis=-1, keepdims=True)
        ds = (pn * (dp - delta)).astype(BF16)
        dq_ref[...] = jnp.dot(ds, kv, preferred_element_type=F32)
        dk_part = lax.dot_general(ds, qv, (((0,), (0,)), ((), ())), preferred_element_type=F32)
        dv_part = lax.dot_general(pn.astype(BF16), dov, (((0,), (0,)), ((), ())), preferred_element_type=F32)
        first = jnp.logical_and(g == 0, i == 0)

        @pl.when(first)
        def _():
            dk_acc[...] = dk_part
            dv_acc[...] = dv_part

        @pl.when(jnp.logical_not(first))
        def _():
            dk_acc[...] += dk_part
            dv_acc[...] += dv_part

        @pl.when(jnp.logical_and(g == group - 1, i == nq - 1))
        def _():
            dk_ref[...] = dk_acc[...].astype(dk_ref.dtype)
            dv_ref[...] = dv_acc[...].astype(dv_ref.dtype)

    if rider is not None:
        rider.n_core_out, rider.n_core_scratch = 3, 2
    body, x_in, x_spec, x_out, x_scr = _ride(body, grid, rider)
    q_spec = pl.BlockSpec((tq, HEAD_W), lambda kh, b, g, i: (b * nq + i, kh * group + g))
    kv_out = pl.BlockSpec((seq, HEAD_W), lambda kh, b, g, i: (b, kh))
    lse_spec = pl.BlockSpec((None, tq, 1), lambda kh, b, g, i: (kh * group + g, b * nq + i, 0))
    sem = ("parallel", "parallel", "arbitrary", "arbitrary") if rider is None else ("arbitrary",) * 4
    return pl.pallas_call(
        body,
        out_shape=[jax.ShapeDtypeStruct(q.shape, F32), jax.ShapeDtypeStruct((t, hk * HEAD_W), BF16),
                   jax.ShapeDtypeStruct((t, hk * HEAD_W), BF16)] + x_out,
        grid=grid,
        in_specs=[q_spec, pl.BlockSpec((seq, HEAD_W), lambda kh, b, g, i: (b, kc0 + kh)),
                  pl.BlockSpec((seq, HEAD_W), lambda kh, b, g, i: (b, vc0 + kh)), q_spec, q_spec, lse_spec] + x_spec,
        out_specs=[q_spec, kv_out, kv_out] + x_spec,
        scratch_shapes=[pltpu.VMEM((seq, HEAD_W), F32), pltpu.VMEM((seq, HEAD_W), F32)] + x_scr,
        compiler_params=_params(sem), name=name)(q, k, v, o, do, lse, *x_in)


def _place():
    return lax.axis_index("x"), lax.axis_index("y"), lax.axis_index("c")


def _other_chips(x, y):
    return [(1 - x, y), (x, 1 - y), (1 - x, 1 - y)]


class _Exchange:
    def __init__(self, kind, src):
        assert kind in ("gather", "scatter")
        self.kind, self.src = kind, src
        rows, w = src.shape[-2:]
        self.out_shape = jax.ShapeDtypeStruct((4, rows, w), src.dtype)
        self.scratch = [pltpu.SemaphoreType.DMA((3,)), pltpu.SemaphoreType.DMA((3,)), pltpu.SemaphoreType.DMA(())]
        self.n_core_out = self.n_core_scratch = 0

    def _copies(self, src_ref, out_ref, send_sems, recv_sems, landing):
        x, y, c = _place()

        def remote(k, s, d, to):
            return pltpu.make_async_remote_copy(src_ref=s, dst_ref=d, send_sem=send_sems.at[k], recv_sem=recv_sems.at[k],
                                                device_id=to, device_id_type=MESH)

        me = 2 * x + y
        part = (lambda j: src_ref) if self.kind == "gather" else (lambda j: src_ref.at[j])
        if landing:
            return [remote(k, part(me), out_ref.at[2 * px + py], (px, py, c)) for k, (px, py) in enumerate(_other_chips(x, y))]
        return [remote(k, part(2 * px + py), out_ref.at[me], (px, py, c)) for k, (px, py) in enumerate(_other_chips(x, y))]

    def _local(self, src_ref, out_ref, local_sem):
        x, y, _ = _place()
        me = 2 * x + y
        return pltpu.make_async_copy(src_ref if self.kind == "gather" else src_ref.at[me], out_ref.at[me], local_sem)

    def start(self, src_ref, out_ref, send_sems, recv_sems, local_sem):
        self._local(src_ref, out_ref, local_sem).start()
        for mine in self._copies(src_ref, out_ref, send_sems, recv_sems, False):
            mine.start()

    def finish(self, src_ref, out_ref, send_sems, recv_sems, local_sem):
        for landed in self._copies(src_ref, out_ref, send_sems, recv_sems, True):
            landed.wait_recv()
        for mine in self._copies(src_ref, out_ref, send_sems, recv_sems, False):
            mine.wait_send()
        self._local(src_ref, out_ref, local_sem).wait()


def _exchange(name, kind, src):
    ex = _Exchange(kind, src)

    def body(src_ref, out_ref, *sems):
        ex.start(src_ref, out_ref, *sems)
        ex.finish(src_ref, out_ref, *sems)

    hbm = pl.BlockSpec(memory_space=pl.ANY)
    return pl.pallas_call(body, out_shape=ex.out_shape, in_specs=[hbm], out_specs=hbm, scratch_shapes=ex.scratch, name=name)(src)


def _adamw(w, g, m, v):
    m = ADAM_B1 * m + (1.0 - ADAM_B1) * g
    v = ADAM_B2 * v + (1.0 - ADAM_B2) * (g * g)
    delta = -ADAM_LR * ((m / M_HAT_DIV) / (jnp.sqrt(v / V_HAT_DIV) + ADAM_EPS) + ADAM_WD * w)
    return delta, m, v


def _small_allreduce_adamw(part, w, m, v):
    def body(part_ref, w_ref, m_ref, v_ref, g_out, d_out, m_out, v_out, loss_out, buf, send_sems, recv_sems):
        x, y, c = _place()
        me = 4 * x + 2 * y + c
        buf[me] = part_ref[...]

        def flip(k):
            fx, fy, fc = (k >> 2) & 1, (k >> 1) & 1, k & 1
            px, py, pc = (1 - x if fx else x), (1 - y if fy else y), (1 - c if fc else c)
            return (px, py, pc), 4 * px + 2 * py + pc

        def copy(k, slot):
            return pltpu.make_async_remote_copy(
                src_ref=part_ref, dst_ref=buf.at[slot], send_sem=send_sems.at[k - 1], recv_sem=recv_sems.at[k - 1],
                device_id=flip(k)[0], device_id_type=MESH)

        sent = [copy(k, me) for k in range(1, 8)]
        for cp in sent:
            cp.start()
        for k in range(1, 8):
            copy(k, flip(k)[1]).wait_recv()
        for cp in sent:
            cp.wait_send()
        tot = buf[0]
        for j in range(1, 8):
            tot = tot + buf[j]
        delta, m_new, v_new = _adamw(w_ref[...], tot, m_ref[...], v_ref[...])
        g_out[...] = tot
        d_out[...] = delta
        m_out[...] = m_new
        v_out[...] = v_new
        loss_out[...] = jnp.sum(tot[LOSS_ROW0:LOSS_ROW0 + 8, :]).reshape(1, 1)

    vm = pl.BlockSpec(memory_space=pltpu.VMEM)
    shp = jax.ShapeDtypeStruct((SMALL_ROWS, 128), F32)
    return pl.pallas_call(
        body, out_shape=[shp, shp, shp, shp, jax.ShapeDtypeStruct((1, 1), F32)],
        in_specs=[vm, vm, vm, vm], out_specs=[vm, vm, vm, vm, vm],
        scratch_shapes=[pltpu.VMEM((8, SMALL_ROWS, 128), F32), pltpu.SemaphoreType.DMA((7,)), pltpu.SemaphoreType.DMA((7,))],
        name="small_allreduce_adamw")(part, w, m, v)


def _row_tile(rows, cap):
    return max(t for t in range(16, min(rows, cap) + 1, 16) if rows % t == 0)


def _reduce_pair(name, parts):
    _, rows, w = parts.shape
    tr = _row_tile(rows, 576)
    nt = rows // tr

    def body(p_ref, o_ref, mine, theirs, send_sems, recv_sems):
        i = pl.program_id(0)
        x, y, c = _place()

        def copy(t):
            rows_t = pl.ds(pl.multiple_of(t * tr, tr), tr)
            return pltpu.make_async_remote_copy(src_ref=mine.at[rows_t], dst_ref=theirs.at[rows_t], send_sem=send_sems.at[t],
                                                recv_sem=recv_sems.at[t], device_id=(x, y, 1 - c), device_id_type=MESH)

        @pl.when(i < nt)
        def _():
            mine[pl.ds(pl.multiple_of(i * tr, tr), tr), :] = (
                (p_ref[0].astype(F32) + p_ref[1].astype(F32)) + p_ref[2].astype(F32)) + p_ref[3].astype(F32)
            copy(i).start()

        @pl.when(i >= nt)
        def _():
            t = i - nt
            copy(t).wait()
            rows_t = pl.ds(pl.multiple_of(t * tr, tr), tr)
            o_ref[...] = mine[rows_t, :] + theirs[rows_t, :]

    return pl.pallas_call(
        body, out_shape=jax.ShapeDtypeStruct((rows, w), F32), grid=(2 * nt,),
        in_specs=[pl.BlockSpec((4, tr, w), lambda i: (0, jnp.minimum(i, nt - 1), 0))],
        out_specs=pl.BlockSpec((tr, w), lambda i: (jnp.maximum(i - nt, 0), 0)),
        scratch_shapes=[pltpu.VMEM((rows, w), F32), pltpu.VMEM((rows, w), F32), pltpu.SemaphoreType.DMA((nt,)),
                        pltpu.SemaphoreType.DMA((nt,))],
        compiler_params=_params(("arbitrary",)), name=name)(parts)


def _adamw_shard(name, g, w, m, v):
    rows, cols = w.shape
    tr = _row_tile(rows, 256)

    def body(g_ref, w_ref, m_ref, v_ref, d_out, m_out, v_out):
        delta, m_new, v_new = _adamw(w_ref[...], g_ref[...], m_ref[...], v_ref[...])
        d_out[...] = delta
        m_out[...] = m_new
        v_out[...] = v_new

    t_spec = pl.BlockSpec((tr, cols), lambda i: (i, 0))
    shp = jax.ShapeDtypeStruct((rows, cols), F32)
    return pl.pallas_call(body, out_shape=[shp] * 3, grid=(rows // tr,), in_specs=[t_spec] * 4, out_specs=[t_spec] * 3,
                          compiler_params=_params(("parallel",)), name=name)(g, w, m, v)


def _shard_shape(name):
    _, r, c, ax = BIG_BY_NAME[name]
    return (r, c // 4) if ax == 1 else (r // 4, c)


def _pad_rows(a, axis):
    pad = [(0, 0)] * a.ndim
    pad[axis] = (0, -a.shape[axis] % PACK_ALIGN)
    return jnp.pad(a, pad)


def _pack_shards(names, shards, dtype):
    return _pad_rows(jnp.concatenate([s.astype(dtype).reshape(-1, PACK_W) for s in shards], axis=0), 0)


def _unpack_shards(names, slab):
    out, off = [], 0
    for name in names:
        rs, cs = _shard_shape(name)
        n = rs * cs // PACK_W
        out.append(slab[off:off + n].reshape(rs, cs))
        off += n
    return out


def _unpack_full(names, slabs):
    out, off = [], 0
    for name in names:
        _, r, c, ax = BIG_BY_NAME[name]
        n = r * c // 4 // PACK_W
        seg = slabs[:, off:off + n]
        out.append(seg.reshape(4, r, c // 4).transpose(1, 0, 2).reshape(r, c) if ax == 1 else seg.reshape(r, c))
        off += n
    return out


def _pack_full(names, mats, dtype):
    segs = []
    for name, a in zip(names, mats):
        _, r, c, ax = BIG_BY_NAME[name]
        a = a.astype(dtype)
        a = a.reshape(r, 4, c // 4).transpose(1, 0, 2) if ax == 1 else a
        segs.append(a.reshape(4, -1, PACK_W))
    return _pad_rows(jnp.concatenate(segs, axis=1), 1)


def _pad_heads_cols(wm, heads, d):
    k = wm.shape[0]
    return jnp.pad(wm.reshape(k, heads, d), ((0, 0), (0, 0), (0, HEAD_W - d))).reshape(k, heads * HEAD_W)


def _unpad_heads_cols(wm, heads, d):
    k = wm.shape[0]
    return wm.reshape(k, heads, HEAD_W)[:, :, :d].reshape(k, heads * d)


def _win_ext(w_in):
    o = np.cumsum([0, Q_LORA, KV_LORA, QK_ROPE, H_B * HD_B, KV_B * HD_B, KV_B * HD_B, D_MODEL, D_MODEL])
    pc = lambda a, n: jnp.pad(a, ((0, 0), (0, n - a.shape[1])))
    return jnp.concatenate([
        w_in[:, o[0]:o[1]], w_in[:, o[1]:o[2]], pc(w_in[:, o[2]:o[3]], HEAD_W),
        _pad_heads_cols(w_in[:, o[3]:o[4]], H_B, HD_B), _pad_heads_cols(w_in[:, o[4]:o[5]], KV_B, HD_B),
        _pad_heads_cols(w_in[:, o[5]:o[6]], KV_B, HD_B), w_in[:, o[6]:o[7]], w_in[:, o[7]:o[8]]], axis=1)


def _win_unext(we):
    c = HEAD_W
    return jnp.concatenate([
        we[:, :ZC_CKV * c], we[:, ZC_CKV * c:ZC_KPE * c], we[:, ZC_KPE * c:ZC_KPE * c + QK_ROPE],
        _unpad_heads_cols(we[:, ZC_QB * c:ZC_KB * c], H_B, HD_B), _unpad_heads_cols(we[:, ZC_KB * c:ZC_VB * c], KV_B, HD_B),
        _unpad_heads_cols(we[:, ZC_VB * c:ZC_GA * c], KV_B, HD_B), we[:, ZC_GA * c:]], axis=1)


def _wkv_ext(w_kvb):
    wk = w_kvb.reshape(KV_LORA, H_A, QK_NOPE + V_DIM_A)
    k_cols = jnp.pad(wk[:, :, :QK_NOPE], ((0, 0), (0, 0), (0, HEAD_W - QK_NOPE))).reshape(KV_LORA, H_A * HEAD_W)
    v_cols = jnp.pad(wk[:, :, QK_NOPE:], ((0, 0), (0, 0), (0, HEAD_W - V_DIM_A))).reshape(KV_LORA, H_A * HEAD_W)
    eye = jnp.pad(jnp.eye(QK_ROPE, dtype=w_kvb.dtype), ((0, 0), (QK_NOPE, HEAD_W - QK_NOPE - QK_ROPE)))
    pe_rows = jnp.concatenate([jnp.tile(eye, (1, H_A)), jnp.zeros((QK_ROPE, H_A * HEAD_W), w_kvb.dtype)], axis=1)
    top = jnp.concatenate([k_cols, v_cols], axis=1)
    return jnp.concatenate([top, pe_rows, jnp.zeros((2 * HEAD_W - KV_LORA - QK_ROPE, 2 * H_A * HEAD_W), w_kvb.dtype)], axis=0)


def _wkv_unext(we):
    k_cols = we[:KV_LORA, :H_A * HEAD_W].reshape(KV_LORA, H_A, HEAD_W)[:, :, :QK_NOPE]
    v_cols = we[:KV_LORA, H_A * HEAD_W:].reshape(KV_LORA, H_A, HEAD_W)[:, :, :V_DIM_A]
    return jnp.concatenate([k_cols, v_cols], axis=2).reshape(KV_LORA, H_A * (QK_NOPE + V_DIM_A))


def _pad_heads_rows(wm, heads, d):
    n = wm.shape[1]
    return jnp.pad(wm.reshape(heads, d, n), ((0, 0), (0, HEAD_W - d), (0, 0))).reshape(heads * HEAD_W, n)


def _unpad_heads_rows(wm, heads, d):
    n = wm.shape[1]
    return wm.reshape(heads, HEAD_W, n)[:, :d].reshape(heads * d, n)


def _rope_tables(seq):
    def ang(pos, dim):
        inv = ROPE_THETA ** (-jnp.arange(0, dim, 2, dtype=F32) / dim)
        return pos.astype(F32)[:, None] * inv[None, :]

    def rot(dim):
        r = np.zeros((dim, dim), np.float32)
        half = dim // 2
        r[np.arange(half) + half, np.arange(half)] = -1.0
        r[np.arange(half), np.arange(half) + half] = 1.0
        return r

    def table(blocks):
        first, width = blocks[0][0], sum(2 * a.shape[1] for _, a in blocks)
        ones = lambda n: jnp.ones((seq, n), F32)
        cos = jnp.concatenate([ones(first)] + [jnp.cos(a) for _, a in blocks for _ in (0, 1)] + [ones(HEAD_W - first - width)], axis=1)
        sin = jnp.concatenate([0.0 * ones(first)] + [jnp.sin(a) for _, a in blocks for _ in (0, 1)]
                              + [0.0 * ones(HEAD_W - first - width)], axis=1)
        pm = np.zeros((HEAD_W, HEAD_W), np.float32)
        for c0, a in blocks:
            d = 2 * a.shape[1]
            pm[c0:c0 + d, c0:c0 + d] = rot(d)
        return cos, sin, jnp.asarray(pm, BF16), jnp.asarray(pm.T, BF16)

    tok = jnp.arange(seq)
    a1 = ang(tok, QK_ROPE)
    arow, acol = ang(tok // GRID_W, HD_B // 2), ang(tok % GRID_W, HD_B // 2)
    return table([(QK_NOPE, a1)]), table([(0, a1)]), table([(0, arow), (HD_B // 2, acol)])


def _local_step(x, p, tgt, gains, wts, ride=None):
    nb, seq, _ = x.shape
    t = nb * seq
    x0 = x.reshape(t, D_MODEL)
    p2 = p.reshape(t, PLE_DIM)
    tg = tgt.reshape(t, D_MODEL)
    (cq_t, sq_t, pq, pq_t), (ck_t, sk_t, pk, pk_t), (cb_t, sb_t, pb, pb_t) = _rope_tables(seq)
    padg = lambda g: jnp.pad(g, ((0, 0), (0, HEAD_W - g.shape[1])))
    g_qn, g_kn = padg(gains["g_qn"]), padg(gains["g_kn"])

    win = _win_ext(wts["w_in"])
    wqb = _pad_heads_cols(wts["w_qb"], H_A, QK_NOPE + QK_ROPE)
    wkv = _wkv_ext(wts["w_kvb"])

    norm = lambda n: (lambda v, g: (_rms(v, g, n),))
    full = lambda a: (a, a.shape[1], 0, False)

    h = _rowwise("norm_mix", norm(D_MODEL), [full(x0)], [(D_MODEL, BF16, D_MODEL, False)], consts=[gains["g_mix"]])
    z = _mm("in_proj", h, win, tn=2048)
    cq = _rowwise("norm_qa", norm(Q_LORA), [(z, Q_LORA, 0, False)], [(Q_LORA, BF16, Q_LORA, False)], consts=[gains["g_qa"]])
    qa = _mm("q_up", cq, wqb)

    def rope_fwd(scale):
        return lambda v, cos, sin, pm: ((v * cos + _perm(v, pm) * sin) * scale,)

    q_a = _rowwise("rope_qa", rope_fwd(SCALE_A), [(qa, HEAD_W, 0, True)], [(H_A * HEAD_W, BF16, HEAD_W, True)],
                   pos=[cq_t, sq_t], consts=[pq], heads=H_A, seq=seq)
    ckv = _rowwise("norm_kva", norm(KV_LORA), [(z, HEAD_W, ZC_CKV, False)], [(HEAD_W, BF16, HEAD_W, False)], consts=[gains["g_kva"]])
    kpe = _rowwise("rope_kpe", rope_fwd(1.0), [(z, HEAD_W, ZC_KPE, False)], [(HEAD_W, BF16, HEAD_W, False)],
                   pos=[ck_t, sk_t], consts=[pk], seq=seq)
    kin = jnp.concatenate([ckv, kpe], axis=1)
    kv_a = _mm("kv_up", kin, wkv, out_dtypes=(BF16,))
    o_a, lse_a, *got_a = _attn_fwd("attn_a_fwd", q_a, kv_a, 0, kv_a, H_A, heads=H_A, group=1, nseq=nb, seq=seq,
                                   rider=ride and ride["gather_a"])

    def prep_fwd(scale):
        def fn(v, cos, sin, g, pm):
            yv = _rms(v, g, HD_B)
            return ((yv * cos + _perm(yv, pm) * sin) * scale,)
        return fn

    q_b = _rowwise("prep_qb", prep_fwd(SCALE_B), [(z, HEAD_W, ZC_QB, True)], [(H_B * HEAD_W, BF16, HEAD_W, True)],
                   pos=[cb_t, sb_t], consts=[g_qn, pb], heads=H_B, seq=seq)
    k_b = _rowwise("prep_kb", prep_fwd(1.0), [(z, HEAD_W, ZC_KB, True)], [(KV_B * HEAD_W, BF16, HEAD_W, True)],
                   pos=[cb_t, sb_t], consts=[g_kn, pb], heads=KV_B, seq=seq)
    o_b, lse_b, *got_b = _attn_fwd("attn_b_fwd", q_b, k_b, 0, z, ZC_VB, heads=H_B, group=H_B // KV_B, nseq=nb, seq=seq,
                                   rider=ride and ride["gather_b"])
    if ride is not None:
        wts = {**wts, **ride["late_weights"](got_a[0], got_b[0])}
    woa = _pad_heads_rows(wts["w_oa"], H_A, V_DIM_A)
    wob = _pad_heads_rows(wts["w_ob"], H_B, HD_B)
    wo, wup, wdown, wpg, wple = wts["w_o"], wts["w_up"], wts["w_down"], wts["w_ple_gate"], wts["w_ple"]
    ya = _mm("out_a", o_a, woa)
    yb = _mm("out_b", o_b, wob)

    z_ga, z_gb = (z, D_MODEL, ZC_GA // 8, False), (z, D_MODEL, ZC_GB // 8, False)
    merged = _rowwise("merge", lambda ga, gb, a, b: (_sigmoid(ga) * a + _sigmoid(gb) * b,),
                      [z_ga, z_gb, full(ya), full(yb)], [(D_MODEL, BF16, D_MODEL, False)])
    x1 = _mm("out_proj", merged, wo, epi=lambda acc, r: (r + acc,), extras=(x0,))
    h2 = _rowwise("norm_mlp", norm(D_MODEL), [full(x1)], [(D_MODEL, BF16, D_MODEL, False)], consts=[gains["g_mlp"]])

    def relu2(acc):
        u = jnp.maximum(acc, 0.0)
        return u, u * u

    u, usq = _mm("mlp_up", h2, wup, out_dtypes=(BF16, BF16), epi=relu2, tn=2048)
    x2 = _mm("mlp_down", usq, wdown, epi=lambda acc, r: (r + acc,), extras=(x1,))
    h3 = _rowwise("norm_ple", norm(D_MODEL), [full(x2)], [(D_MODEL, BF16, D_MODEL, False)], consts=[gains["g_ple"]])
    gpre = _mm("ple_gate", h3, wpg)
    pe = _mm("ple_proj", p2, wple)

    def tail(x2v, gp, pev, tv, gf):
        sg = _sigmoid(gp)
        x3 = x2v + sg * pev
        rs = lax.rsqrt(jnp.sum(x3 * x3, axis=-1, keepdims=True) * (1.0 / D_MODEL) + EPS)
        xh = x3 * rs
        err = xh * gf - tv
        dy = err * (1.0 / D_MODEL)
        dyg = dy * gf
        dx3 = rs * (dyg - xh * (jnp.sum(dyg * xh, axis=-1, keepdims=True) * (1.0 / D_MODEL)))
        return (dx3, dx3 * pev * sg * (1.0 - sg), dx3 * sg,
                jnp.sum(err * err, axis=0, keepdims=True) * (0.5 / D_MODEL), jnp.sum(dy * xh, axis=0, keepdims=True))

    dx3, dgpre, dpe, loss_part, dg_final = _rowwise(
        "tail", tail, [full(x2), full(gpre), full(pe), full(tg)],
        [(D_MODEL, F32, D_MODEL, False), (D_MODEL, BF16, D_MODEL, False), (D_MODEL, BF16, D_MODEL, False)],
        consts=[gains["g_final"].reshape(1, D_MODEL)], accs=[(1, D_MODEL), (1, D_MODEL)], tm=256)

    def norm_bwd(n, with_res):
        if with_res:
            def fn(dh, v, res, g):
                dx, dg = _rms_bwd(dh, v, g, n)
                return dx + res, dg
        else:
            def fn(dh, v, g):
                return _rms_bwd(dh, v, g, n)
        return fn

    dw = {}
    dw["w_ple"] = _mm_tn("dw_ple", p2, dpe)
    dw["w_ple_gate"] = _mm_tn("dw_ple_gate", h3, dgpre)
    dh3 = _mm("d_ple_gate", dgpre, wpg, trans_b=True)
    dx2, dg_ple = _rowwise("norm_ple_bwd", norm_bwd(D_MODEL, True), [full(dh3), full(x2), full(dx3)],
                           [(D_MODEL, F32, D_MODEL, False)], consts=[gains["g_ple"]], accs=[(1, D_MODEL)], tm=256)
    dw["w_down"] = _mm_tn("dw_down", usq, dx2)
    dupre = _mm("d_mlp_down", dx2, wdown, trans_b=True, out_dtypes=(BF16,), epi=lambda acc, uv: (acc * (2.0 * uv.astype(F32)),),
                extras=(u,), tn=2048)
    dw["w_up"] = _mm_tn("dw_up", h2, dupre)
    dh2 = _mm("d_mlp_up", dupre, wup, trans_b=True)
    dx1, dg_mlp = _rowwise("norm_mlp_bwd", norm_bwd(D_MODEL, True), [full(dh2), full(x1), full(dx2)],
                           [(D_MODEL, F32, D_MODEL, False)], consts=[gains["g_mlp"]], accs=[(1, D_MODEL)], tm=256)
    dw["w_o"] = _mm_tn("dw_o", merged, dx1)
    dmerged = _mm("d_out_proj", dx1, wo, trans_b=True)

    def merge_bwd(dm, ga, gb, a, b):
        sa, sb = _sigmoid(ga), _sigmoid(gb)
        return dm * sa, dm * sb, dm * a * sa * (1.0 - sa), dm * b * sb * (1.0 - sb)

    dya, dyb, dga, dgb = _rowwise("merge_bwd", merge_bwd, [full(dmerged), z_ga, z_gb, full(ya), full(yb)],
                                  [(D_MODEL, BF16, D_MODEL, False)] * 4, tm=256)
    dw["w_oa"] = _unpad_heads_rows(_mm_tn("dw_oa", o_a, dya), H_A, V_DIM_A)
    dw["w_ob"] = _unpad_heads_rows(_mm_tn("dw_ob", o_b, dyb), H_B, HD_B)
    do_a = _mm("d_out_a", dya, woa, trans_b=True, out_dtypes=(BF16,))
    do_b = _mm("d_out_b", dyb, wob, trans_b=True, out_dtypes=(BF16,))
    res_a = _attn_bwd("attn_a_bwd", q_a, kv_a, 0, kv_a, H_A, o_a, do_a, lse_a, heads=H_A, group=1, nseq=nb, seq=seq,
                      rider=ride and ride["scatter_late"](dw))
    dq_a, dk_a, dv_a = res_a[:3]
    dq_b, dk_b, dv_b = _attn_bwd("attn_b_bwd", q_b, k_b, 0, z, ZC_VB, o_b, do_b, lse_b, heads=H_B, group=H_B // KV_B, nseq=nb,
                                 seq=seq)
    if ride is not None:
        ride["out"]["parts_late"] = res_a[3]

    def rope_bwd(scale):
        return lambda d, cos, sin, pm_t: ((d * cos + _perm(d * sin, pm_t)) * scale,)

    dqa = _rowwise("rope_qa_bwd", rope_bwd(SCALE_A), [(dq_a, HEAD_W, 0, True)], [(H_A * HEAD_W, BF16, HEAD_W, True)],
                   pos=[cq_t, sq_t], consts=[pq_t], heads=H_A, seq=seq)
    dw["w_qb"] = _unpad_heads_cols(_mm_tn("dw_qb", cq, dqa), H_A, QK_NOPE + QK_ROPE)
    dcq = _mm("d_q_up", dqa, wqb, trans_b=True)
    dq_lat, dg_qa = _rowwise("norm_qa_bwd", norm_bwd(Q_LORA, False), [full(dcq), (z, Q_LORA, 0, False)],
                             [(Q_LORA, BF16, Q_LORA, False)], consts=[gains["g_qa"]], accs=[(1, Q_LORA)])
    dkv_a = jnp.concatenate([dk_a, dv_a], axis=1)
    dw["w_kvb"] = _wkv_unext(_mm_tn("dw_kv", kin, dkv_a))
    dkin = _mm("d_kv_up", dkv_a, wkv, trans_b=True)
    dckv, dg_kva = _rowwise("norm_kva_bwd", norm_bwd(KV_LORA, False), [(dkin, HEAD_W, 0, False), (z, HEAD_W, ZC_CKV, False)],
                            [(HEAD_W, BF16, HEAD_W, False)], consts=[gains["g_kva"]], accs=[(1, KV_LORA)])
    dkpe = _rowwise("rope_kpe_bwd", rope_bwd(1.0), [(dkin, HEAD_W, 1, False)], [(HEAD_W, BF16, HEAD_W, False)],
                    pos=[ck_t, sk_t], consts=[pk_t], seq=seq)

    def prep_bwd(scale):
        def fn(d, v, cos, sin, g, pm_t):
            dyv = (d * cos + _perm(d * sin, pm_t)) * scale
            return _rms_bwd(dyv, v, g, HD_B)
        return fn

    dqb, dg_qn = _rowwise("prep_qb_bwd", prep_bwd(SCALE_B), [(dq_b, HEAD_W, 0, True), (z, HEAD_W, ZC_QB, True)],
                          [(H_B * HEAD_W, BF16, HEAD_W, True)], pos=[cb_t, sb_t], consts=[g_qn, pb_t], accs=[(1, HEAD_W)],
                          heads=H_B, seq=seq)
    dkb, dg_kn = _rowwise("prep_kb_bwd", prep_bwd(1.0), [(dk_b, HEAD_W, 0, True), (z, HEAD_W, ZC_KB, True)],
                          [(KV_B * HEAD_W, BF16, HEAD_W, True)], pos=[cb_t, sb_t], consts=[g_kn, pb_t], accs=[(1, HEAD_W)],
                          heads=KV_B, seq=seq)

    dz = jnp.concatenate([dq_lat, dckv, dkpe, dqb, dkb, dv_b, dga, dgb], axis=1)
    dw["w_in"] = _win_unext(_mm_tn("dw_in", h, dz))
    dh = _mm("d_in_proj", dz, win, trans_b=True, rider=ride and ride["scatter_early"](dw))
    if ride is not None:
        dh, ride["out"]["parts_early"] = dh
    dx0, dg_mix = _rowwise("norm_mix_bwd", norm_bwd(D_MODEL, True), [full(dh), full(x0), full(dx1)],
                           [(D_MODEL, F32, D_MODEL, False)], consts=[gains["g_mix"]], accs=[(1, D_MODEL)], tm=256)

    dg = {"g_mix": dg_mix, "g_qa": dg_qa, "g_kva": dg_kva, "g_qn": dg_qn[:, :HD_B], "g_kn": dg_kn[:, :HD_B],
          "g_mlp": dg_mlp, "g_ple": dg_ple, "g_final": dg_final}
    return loss_part, dx0.reshape(nb, seq, D_MODEL), dg, dw


def _pack_small(vals, loss_part=None):
    flat = jnp.concatenate([vals[n].reshape(1, -1) for n, _ in SMALL], axis=1)
    loss = jnp.zeros((1, 8 * 128), F32) if loss_part is None else loss_part
    gap = jnp.zeros((1, LOSS_ROW0 * 128 - SMALL_N), F32)
    return jnp.concatenate([flat, gap, loss], axis=1).reshape(SMALL_ROWS, 128)


def _unpack_small(slab, like):
    flat, out, off = slab.reshape(-1), {}, 0
    for n, k in SMALL:
        out[n] = flat[off:off + k].reshape(like[n].shape)
        off += k
    return out


def kernel(x, p, g_mix, w_in, g_qa, w_qb, g_kva, w_kvb, g_qn, g_kn, w_oa, w_ob, w_o, g_mlp, w_up, w_down, g_ple, w_ple_gate, w_ple, g_final, loss_target, m_g_mix, m_w_in, m_g_qa, m_w_qb, m_g_kva, m_w_kvb, m_g_qn, m_g_kn, m_w_oa, m_w_ob, m_w_o, m_g_mlp, m_w_up, m_w_down, m_g_ple, m_w_ple_gate, m_w_ple, m_g_final, v_g_mix, v_w_in, v_g_qa, v_w_qb, v_g_kva, v_w_kvb, v_g_qn, v_g_kn, v_w_oa, v_w_ob, v_w_o, v_g_mlp, v_w_up, v_w_down, v_g_ple, v_w_ple_gate, v_w_ple, v_g_final):
    given = dict(locals())
    order = ["g_mix", "w_in", "g_qa", "w_qb", "g_kva", "w_kvb", "g_qn", "g_kn", "w_oa", "w_ob", "w_o", "g_mlp", "w_up",
             "w_down", "g_ple", "w_ple_gate", "w_ple", "g_final"]
    big_names = [n for n, _, _, _ in BIG]
    local = lambda prefix, names: [given[prefix + n][0] for n in names]
    late = LATE_A + LATE_B

    early_w = _exchange("weight_gather_early", "gather", _pack_shards(EARLY, local("", EARLY), BF16))
    wts = dict(zip(EARLY, _unpack_full(EARLY, early_w)))
    gains = {n: given[n].reshape(1, -1) for n, _ in SMALL}
    ride = {
        "gather_a": _Exchange("gather", _pack_shards(LATE_A, local("", LATE_A), BF16)),
        "gather_b": _Exchange("gather", _pack_shards(LATE_B, local("", LATE_B), BF16)),
        "late_weights": lambda ga, gb: {**dict(zip(LATE_A, _unpack_full(LATE_A, ga))), **dict(zip(LATE_B, _unpack_full(LATE_B, gb)))},
        "scatter_late": lambda dw: _Exchange("scatter", _pack_full(late, [dw[n] for n in late], BF16)),
        "scatter_early": lambda dw: _Exchange("scatter", _pack_full(EARLY, [dw[n] for n in EARLY], BF16)),
        "out": {},
    }
    loss_part, grad_x, dg, dw = _local_step(x, p[0], loss_target, gains, wts, ride)

    small = lambda prefix: _pack_small({n: given[prefix + n] for n, _ in SMALL})
    g_s, d_s, m_s, v_s, loss = _small_allreduce_adamw(_pack_small(dg, loss_part), small(""), small("m_"), small("v_"))

    grads = dict(zip(late, _unpack_shards(late, _reduce_pair("grad_reduce_late", ride["out"]["parts_late"]))))
    grads.update(zip(EARLY, _unpack_shards(EARLY, _reduce_pair("grad_reduce_early", ride["out"]["parts_early"]))))

    res = {}
    for key, slab in (("grad_", g_s), ("delta_", d_s), ("new_m_", m_s), ("new_v_", v_s)):
        for n, val in _unpack_small(slab, given).items():
            res[key + n] = val
    for n in big_names:
        d_w, m_w, v_w = _adamw_shard("adamw_" + n, grads[n], given[n][0], given["m_" + n][0], given["v_" + n][0])
        res["grad_" + n], res["delta_" + n], res["new_m_" + n], res["new_v_" + n] = grads[n][None], d_w[None], m_w[None], v_w[None]
    outs = [loss.reshape(()), grad_x]
    for key in ("grad_", "delta_", "new_m_", "new_v_"):
        outs += [res[key + n] for n in order]
    return tuple(outs)
```

```python
import functools

import numpy as np
import jax
import jax.numpy as jnp
from jax import lax
from jax.experimental import pallas as pl
from jax.experimental.pallas import tpu as pltpu

F32 = jnp.float32
BF16 = jnp.bfloat16
MESH = pl.DeviceIdType.MESH

D_MODEL = 1024
GRID_W = 64
ROPE_THETA = 10000.0
EPS = 1e-6
H_A, QK_NOPE, QK_ROPE, V_DIM_A, Q_LORA, KV_LORA = 8, 64, 32, 64, 256, 128
H_B, KV_B, HD_B = 8, 2, 64
D_FF = 4096
PLE_DIM = 256
HEAD_W = 128
SCALE_A = (QK_NOPE + QK_ROPE) ** -0.5
SCALE_B = HD_B ** -0.5

ADAM_LR, ADAM_B1, ADAM_B2, ADAM_EPS, ADAM_WD, ADAM_STEP = 0.001, 0.9, 0.999, 1e-08, 0.01, 10
M_HAT_DIV = 1.0 - ADAM_B1 ** ADAM_STEP
V_HAT_DIV = 1.0 - ADAM_B2 ** ADAM_STEP

VMEM_LIMIT_BYTES = 56 * 1024 * 1024

ZC_QB, ZC_QLAT, ZC_CKV, ZC_KPE, ZC_KB, ZC_VB, ZC_GA, ZC_GB = 0, 8, 10, 11, 12, 14, 16, 24
Z_WIDTH = 32 * HEAD_W

BIG = [
    ("w_in", 1024, 3232, 1), ("w_qb", 256, 768, 1), ("w_kvb", 128, 1024, 1), ("w_oa", 512, 1024, 1),
    ("w_ob", 512, 1024, 1), ("w_o", 1024, 1024, 0), ("w_up", 1024, 4096, 1), ("w_down", 4096, 1024, 0),
    ("w_ple_gate", 1024, 1024, 0), ("w_ple", 256, 1024, 1),
]
BIG_BY_NAME = {e[0]: e for e in BIG}
PACK_W = 1024
PACK_ALIGN = 64
EARLY_SMALL = ["w_qb", "w_kvb"]
LATE_A = ["w_oa", "w_ob", "w_o", "w_up"]
LATE_B = ["w_down", "w_ple_gate", "w_ple"]

SMALL = [("g_mix", 1024), ("g_qa", 256), ("g_kva", 128), ("g_qn", 64), ("g_kn", 64), ("g_mlp", 1024),
         ("g_ple", 1024), ("g_final", 1024)]
SMALL_N = sum(n for _, n in SMALL)
LOSS_ROW0 = 40
SMALL_ROWS = 48


def _params(sem):
    return pltpu.CompilerParams(dimension_semantics=sem, vmem_limit_bytes=VMEM_LIMIT_BYTES)


def _sigmoid(v):
    return 1.0 / (1.0 + jnp.exp(-v))


def _perm(v, p_ref):
    pm = p_ref[...]
    hi = v.astype(BF16)
    lo = (v - hi.astype(F32)).astype(BF16)
    return (jnp.dot(hi, pm, preferred_element_type=F32) + jnp.dot(lo, pm, preferred_element_type=F32))


def _rms(v, g, n):
    rs = lax.rsqrt(jnp.sum(v * v, axis=-1, keepdims=True) * (1.0 / n) + EPS)
    return v * rs * g


def _rms_bwd(dy, v, g, n):
    rs = lax.rsqrt(jnp.sum(v * v, axis=-1, keepdims=True) * (1.0 / n) + EPS)
    vh = v * rs
    dyg = dy * g
    dx = rs * (dyg - vh * (jnp.sum(dyg * vh, axis=-1, keepdims=True) * (1.0 / n)))
    return dx, jnp.sum(dy * vh, axis=0, keepdims=True)


def _ride(body, grid, rider):
    if rider is None:
        return body, [], [], [], []
    n_sem = len(rider.scratch)

    def wrapped(*refs):
        ids = [pl.program_id(a) for a in range(len(grid))]
        n_in = len(refs) - n_sem - 2 - rider.n_core_out - rider.n_core_scratch
        core_in, src = refs[:n_in], refs[n_in]
        core_out = refs[n_in + 1:n_in + 1 + rider.n_core_out]
        dst = refs[n_in + 1 + rider.n_core_out]
        core_scr = refs[n_in + 2 + rider.n_core_out:len(refs) - n_sem]
        sems = refs[len(refs) - n_sem:]

        @pl.when(functools.reduce(jnp.logical_and, [a == 0 for a in ids]))
        def _():
            rider.start(src, dst, *sems)

        body(*core_in, *core_out, *core_scr)

        @pl.when(functools.reduce(jnp.logical_and, [a == n - 1 for a, n in zip(ids, grid)]))
        def _():
            rider.finish(src, dst, *sems)

    hbm = pl.BlockSpec(memory_space=pl.ANY)
    return wrapped, [rider.src], [hbm], [rider.out_shape], list(rider.scratch)


def _mm(name, a, b, *, trans_b=False, out_dtypes=(F32,), epi=None, extras=(), tm=512, tn=None, rider=None):
    m, k = a.shape
    n = b.shape[0] if trans_b else b.shape[1]
    tn = n if tn is None else min(tn, n)
    tm = min(tm, m)
    assert m % tm == 0 and n % tn == 0 and (b.shape[1] if trans_b else b.shape[0]) == k
    n_ex = len(extras)
    dims = (((1,), (1,)), ((), ())) if trans_b else (((1,), (0,)), ((), ()))

    def body(a_ref, b_ref, *rest):
        acc = lax.dot_general(a_ref[...].astype(BF16), b_ref[...].astype(BF16), dims, preferred_element_type=F32)
        res = (acc,) if epi is None else epi(acc, *[e[...] for e in rest[:n_ex]])
        for o_ref, r in zip(rest[n_ex:], res):
            o_ref[...] = r.astype(o_ref.dtype)

    grid = (n // tn, m // tm)
    if rider is not None:
        rider.n_core_out, rider.n_core_scratch = len(out_dtypes), 0
    body, x_in, x_spec, x_out, x_scr = _ride(body, grid, rider)
    a_spec = pl.BlockSpec((tm, k), lambda j, i: (i, 0))
    b_spec = pl.BlockSpec((tn, k), lambda j, i: (j, 0)) if trans_b else pl.BlockSpec((k, tn), lambda j, i: (0, j))
    t_spec = pl.BlockSpec((tm, tn), lambda j, i: (i, j))
    outs = pl.pallas_call(
        body, out_shape=[jax.ShapeDtypeStruct((m, n), d) for d in out_dtypes] + x_out, grid=grid,
        in_specs=[a_spec, b_spec] + [t_spec] * n_ex + x_spec, out_specs=[t_spec] * len(out_dtypes) + x_spec,
        scratch_shapes=x_scr, compiler_params=_params(("parallel", "parallel") if rider is None else ("arbitrary", "arbitrary")),
        name=name)(a, b, *extras, *x_in)
    return outs[0] if len(outs) == 1 else outs


def _per_head(fn, heads, n_tiled, n_out):
    def run(*args):
        res = [fn(*[a[:, hd * HEAD_W:(hd + 1) * HEAD_W] for a in args[:n_tiled]], *args[n_tiled:]) for hd in range(heads)]
        tiles = [jnp.concatenate([r[k] for r in res], axis=1) for k in range(n_out)]
        sums = [functools.reduce(lambda u, v: u + v, [r[k] for r in res]) for k in range(n_out, len(res[0]))]
        return (*tiles, *sums)
    return run


def _mm_tn(name, a, b, *, tk=1024, tn=1024, tt=1024):
    t, k = a.shape
    n = b.shape[1]
    tk, tn, tt = min(tk, k), min(tn, n), min(tt, t)
    assert b.shape[0] == t and k % tk == 0 and n % tn == 0 and t % tt == 0

    def body(a_ref, b_ref, o_ref):
        part = lax.dot_general(a_ref[...].astype(BF16), b_ref[...].astype(BF16), (((0,), (0,)), ((), ())),
                               preferred_element_type=F32)

        @pl.when(pl.program_id(2) == 0)
        def _():
            o_ref[...] = part

        @pl.when(pl.program_id(2) != 0)
        def _():
            o_ref[...] += part

    return pl.pallas_call(
        body, out_shape=jax.ShapeDtypeStruct((k, n), F32), grid=(k // tk, n // tn, t // tt),
        in_specs=[pl.BlockSpec((tt, tk), lambda ki, ni, ti: (ti, ki)), pl.BlockSpec((tt, tn), lambda ki, ni, ti: (ti, ni))],
        out_specs=pl.BlockSpec((tk, tn), lambda ki, ni, ti: (ki, ni)),
        compiler_params=_params(("parallel", "parallel", "arbitrary")), name=name)(a, b)


def _rowwise(name, fn, ins, outs, *, consts=(), pos=(), accs=(), heads=1, tm=512, seq=None):
    t = ins[0][0].shape[0]
    tm = min(tm, t if seq is None else seq)
    assert t % tm == 0 and (seq is None or seq % tm == 0)
    n_in, n_pos, n_c, n_out, n_acc = len(ins), len(pos), len(consts), len(outs), len(accs)

    def body(*refs):
        vals = [r[...] for r in refs[:n_in + n_pos + n_c]]
        res = fn(*vals)
        o_refs = refs[n_in + n_pos + n_c:]
        for o_ref, r in zip(o_refs[:n_out], res[:n_out]):
            o_ref[...] = r.astype(o_ref.dtype)
        if n_acc:
            first = jnp.logical_and(pl.program_id(0) == 0, pl.program_id(1) == 0)

            @pl.when(first)
            def _():
                for o_ref, r in zip(o_refs[n_out:], res[n_out:]):
                    o_ref[...] = r

            @pl.when(jnp.logical_not(first))
            def _():
                for o_ref, r in zip(o_refs[n_out:], res[n_out:]):
                    o_ref[...] += r

    def tiled(width, c0, per_head):
        return pl.BlockSpec((tm, width), (lambda h, i: (i, c0 + h)) if per_head else (lambda h, i: (i, c0)))

    in_specs = [tiled(w, c0, ph) for _, w, c0, ph in ins]
    if n_pos:
        nblk = seq // tm
        in_specs += [pl.BlockSpec((tm, a.shape[1]), lambda h, i: (i % nblk, 0)) for a in pos]
    in_specs += [pl.BlockSpec(a.shape, lambda h, i: (0, 0)) for a in consts]
    out_specs = [tiled(w, 0, ph) for _, _, w, ph in outs] + [pl.BlockSpec(s, lambda h, i: (0, 0)) for s in accs]
    out_shape = [jax.ShapeDtypeStruct((t, c), d) for c, d, _, _ in outs] + [jax.ShapeDtypeStruct(s, F32) for s in accs]
    sem = ("arbitrary", "arbitrary") if n_acc else ("parallel", "parallel")
    res = pl.pallas_call(body, out_shape=out_shape, grid=(heads, t // tm), in_specs=in_specs, out_specs=out_specs,
                         compiler_params=_params(sem), name=name)(*[a for a, _, _, _ in ins], *pos, *consts)
    return res[0] if len(res) == 1 else res


ATTN_HEADS_PER_STEP = 2


def _attn_fwd(name, q, k, kc0, v, vc0, *, heads, group, nseq, seq, tq=512, rider=None):
    tq = min(tq, seq)
    nq = seq // tq
    hp = ATTN_HEADS_PER_STEP
    grid = (heads // hp, nseq, nq)
    shared = group > 1
    assert group % hp == 0 if shared else (kc0 % hp == 0 and vc0 % hp == 0)

    def body(q_ref, k_ref, v_ref, o_ref, lse_ref):
        for j in range(hp):
            cols = slice(j * HEAD_W, (j + 1) * HEAD_W)
            kj = (k_ref[...] if shared else k_ref[:, cols]).astype(BF16)
            vj = (v_ref[...] if shared else v_ref[:, cols]).astype(BF16)
            s = lax.dot_general(q_ref[:, cols], kj, (((1,), (1,)), ((), ())), preferred_element_type=F32)
            m = jnp.max(s, axis=-1, keepdims=True)
            p = jnp.exp(s - m)
            l = jnp.sum(p, axis=-1, keepdims=True)
            o = jnp.dot(p.astype(BF16), vj, preferred_element_type=F32)
            o_ref[:, cols] = (o * (1.0 / l)).astype(o_ref.dtype)
            lse_ref[j] = m + jnp.log(l)

    if rider is not None:
        rider.n_core_out, rider.n_core_scratch = 2, 0
    body, x_in, x_spec, x_out, x_scr = _ride(body, grid, rider)
    q_spec = pl.BlockSpec((tq, hp * HEAD_W), lambda h, b, i: (b * nq + i, h))
    if shared:
        k_spec = pl.BlockSpec((seq, HEAD_W), lambda h, b, i: (b, kc0 + (h * hp) // group))
        v_spec = pl.BlockSpec((seq, HEAD_W), lambda h, b, i: (b, vc0 + (h * hp) // group))
    else:
        k_spec = pl.BlockSpec((seq, hp * HEAD_W), lambda h, b, i: (b, kc0 // hp + h))
        v_spec = pl.BlockSpec((seq, hp * HEAD_W), lambda h, b, i: (b, vc0 // hp + h))
    lse_spec = pl.BlockSpec((hp, tq, 1), lambda h, b, i: (h, b * nq + i, 0))
    sem = ("parallel",) * 3 if rider is None else ("arbitrary",) * 3
    return pl.pallas_call(
        body, out_shape=[jax.ShapeDtypeStruct(q.shape, BF16), jax.ShapeDtypeStruct((heads, q.shape[0], 1), F32)] + x_out,
        grid=grid, in_specs=[q_spec, k_spec, v_spec] + x_spec, out_specs=[q_spec, lse_spec] + x_spec, scratch_shapes=x_scr,
        compiler_params=_params(sem), name=name)(q, k, v, *x_in)


def _attn_bwd(name, q, k, kc0, v, vc0, o, do, lse, *, heads, group, nseq, seq, tq=256, rider=None):
    tq = min(tq, seq)
    nq = seq // tq
    hk = heads // group
    t = q.shape[0]
    grid = (hk, nseq, group, nq)

    def body(q_ref, k_ref, v_ref, o_ref, do_ref, lse_ref, dq_ref, dk_ref, dv_ref, dk_acc, dv_acc):
        g, i = pl.program_id(2), pl.program_id(3)
        qv, kv, vv, dov = q_ref[...], k_ref[...].astype(BF16), v_ref[...].astype(BF16), do_ref[...]
        s = lax.dot_general(qv, kv, (((1,), (1,)), ((), ())), preferred_element_type=F32)
        pn = jnp.exp(s - lse_ref[...])
        dp = lax.dot_general(dov, vv, (((1,), (1,)), ((), ())), preferred_element_type=F32)
        delta = jnp.sum(dov.astype(F32) * o_ref[...].astype(F32), axis=-1, keepdims=True)
        ds = (pn * (dp - delta)).astype(BF16)
        dq_ref[...] = jnp.dot(ds, kv, preferred_element_type=F32)
        dk_part = lax.dot_general(ds, qv, (((0,), (0,)), ((), ())), preferred_element_type=F32)
        dv_part = lax.dot_general(pn.astype(BF16), dov, (((0,), (0,)), ((), ())), preferred_element_type=F32)
        first = jnp.logical_and(g == 0, i == 0)

        @pl.when(first)
        def _():
            dk_acc[...] = dk_part
            dv_acc[...] = dv_part

        @pl.when(jnp.logical_not(first))
        def _():
            dk_acc[...] += dk_part
            dv_acc[...] += dv_part

        @pl.when(jnp.logical_and(g == group - 1, i == nq - 1))
        def _():
            dk_ref[...] = dk_acc[...].astype(dk_ref.dtype)
            dv_ref[...] = dv_acc[...].astype(dv_ref.dtype)

    if rider is not None:
        rider.n_core_out, rider.n_core_scratch = 3, 2
    body, x_in, x_spec, x_out, x_scr = _ride(body, grid, rider)
    q_spec = pl.BlockSpec((tq, HEAD_W), lambda kh, b, g, i: (b * nq + i, kh * group + g))
    kv_out = pl.BlockSpec((seq, HEAD_W), lambda kh, b, g, i: (b, kh))
    lse_spec = pl.BlockSpec((None, tq, 1), lambda kh, b, g, i: (kh * group + g, b * nq + i, 0))
    sem = ("parallel", "parallel", "arbitrary", "arbitrary") if rider is None else ("arbitrary",) * 4
    return pl.pallas_call(
        body,
        out_shape=[jax.ShapeDtypeStruct(q.shape, F32), jax.ShapeDtypeStruct((t, hk * HEAD_W), BF16),
                   jax.ShapeDtypeStruct((t, hk * HEAD_W), BF16)] + x_out,
        grid=grid,
        in_specs=[q_spec, pl.BlockSpec((seq, HEAD_W), lambda kh, b, g, i: (b, kc0 + kh)),
                  pl.BlockSpec((seq, HEAD_W), lambda kh, b, g, i: (b, vc0 + kh)), q_spec, q_spec, lse_spec] + x_spec,
        out_specs=[q_spec, kv_out, kv_out] + x_spec,
        scratch_shapes=[pltpu.VMEM((seq, HEAD_W), F32), pltpu.VMEM((seq, HEAD_W), F32)] + x_scr,
        compiler_params=_params(sem), name=name)(q, k, v, o, do, lse, *x_in)


def _place():
    return lax.axis_index("x"), lax.axis_index("y"), lax.axis_index("c")


def _other_chips(x, y):
    return [(1 - x, y), (x, 1 - y), (1 - x, 1 - y)]


class _Exchange:
    def __init__(self, kind, src):
        assert kind in ("gather", "scatter")
        self.kind, self.src = kind, src
        rows, w = src.shape[-2:]
        self.out_shape = jax.ShapeDtypeStruct((4, rows, w), src.dtype)
        self.scratch = [pltpu.SemaphoreType.DMA((3,)), pltpu.SemaphoreType.DMA((3,)), pltpu.SemaphoreType.DMA(())]
        self.n_core_out = self.n_core_scratch = 0

    def _copies(self, src_ref, out_ref, send_sems, recv_sems, landing):
        x, y, c = _place()

        def remote(k, s, d, to):
            return pltpu.make_async_remote_copy(src_ref=s, dst_ref=d, send_sem=send_sems.at[k], recv_sem=recv_sems.at[k],
                                                device_id=to, device_id_type=MESH)

        me = 2 * x + y
        part = (lambda j: src_ref) if self.kind == "gather" else (lambda j: src_ref.at[j])
        if landing:
            return [remote(k, part(me), out_ref.at[2 * px + py], (px, py, c)) for k, (px, py) in enumerate(_other_chips(x, y))]
        return [remote(k, part(2 * px + py), out_ref.at[me], (px, py, c)) for k, (px, py) in enumerate(_other_chips(x, y))]

    def _local(self, src_ref, out_ref, local_sem):
        x, y, _ = _place()
        me = 2 * x + y
        return pltpu.make_async_copy(src_ref if self.kind == "gather" else src_ref.at[me], out_ref.at[me], local_sem)

    def start(self, src_ref, out_ref, send_sems, recv_sems, local_sem):
        self._local(src_ref, out_ref, local_sem).start()
        for mine in self._copies(src_ref, out_ref, send_sems, recv_sems, False):
            mine.start()

    def finish(self, src_ref, out_ref, send_sems, recv_sems, local_sem):
        for landed in self._copies(src_ref, out_ref, send_sems, recv_sems, True):
            landed.wait_recv()
        for mine in self._copies(src_ref, out_ref, send_sems, recv_sems, False):
            mine.wait_send()
        self._local(src_ref, out_ref, local_sem).wait()


def _exchange_alone(name, exchanges):
    n = len(exchanges)

    def body(*refs):
        args = [(refs[j], refs[n + j], *refs[2 * n + 3 * j:2 * n + 3 * j + 3]) for j in range(n)]
        for ex, a in zip(exchanges, args):
            ex.start(*a)
        for ex, a in zip(exchanges, args):
            ex.finish(*a)

    hbm = pl.BlockSpec(memory_space=pl.ANY)
    return pl.pallas_call(body, out_shape=[ex.out_shape for ex in exchanges], in_specs=[hbm] * n, out_specs=[hbm] * n,
                          scratch_shapes=[s for ex in exchanges for s in ex.scratch], name=name)(*[ex.src for ex in exchanges])


def _adamw(w, g, m, v):
    m = ADAM_B1 * m + (1.0 - ADAM_B1) * g
    v = ADAM_B2 * v + (1.0 - ADAM_B2) * (g * g)
    delta = -ADAM_LR * ((m / M_HAT_DIV) / (jnp.sqrt(v / V_HAT_DIV) + ADAM_EPS) + ADAM_WD * w)
    return delta, m, v


def _small_allreduce_adamw(part, w, m, v):
    def body(part_ref, w_ref, m_ref, v_ref, g_out, d_out, m_out, v_out, loss_out, buf, send_sems, recv_sems):
        x, y, c = _place()
        me = 4 * x + 2 * y + c
        buf[me] = part_ref[...]

        def flip(k):
            fx, fy, fc = (k >> 2) & 1, (k >> 1) & 1, k & 1
            px, py, pc = (1 - x if fx else x), (1 - y if fy else y), (1 - c if fc else c)
            return (px, py, pc), 4 * px + 2 * py + pc

        def copy(k, slot):
            return pltpu.make_async_remote_copy(
                src_ref=part_ref, dst_ref=buf.at[slot], send_sem=send_sems.at[k - 1], recv_sem=recv_sems.at[k - 1],
                device_id=flip(k)[0], device_id_type=MESH)

        sent = [copy(k, me) for k in range(1, 8)]
        for cp in sent:
            cp.start()
        for k in range(1, 8):
            copy(k, flip(k)[1]).wait_recv()
        for cp in sent:
            cp.wait_send()
        tot = buf[0]
        for j in range(1, 8):
            tot = tot + buf[j]
        delta, m_new, v_new = _adamw(w_ref[...], tot, m_ref[...], v_ref[...])
        g_out[...] = tot
        d_out[...] = delta
        m_out[...] = m_new
        v_out[...] = v_new
        loss_out[...] = jnp.sum(tot[LOSS_ROW0:LOSS_ROW0 + 8, :]).reshape(1, 1)

    vm = pl.BlockSpec(memory_space=pltpu.VMEM)
    shp = jax.ShapeDtypeStruct((SMALL_ROWS, 128), F32)
    return pl.pallas_call(
        body, out_shape=[shp, shp, shp, shp, jax.ShapeDtypeStruct((1, 1), F32)],
        in_specs=[vm, vm, vm, vm], out_specs=[vm, vm, vm, vm, vm],
        scratch_shapes=[pltpu.VMEM((8, SMALL_ROWS, 128), F32), pltpu.SemaphoreType.DMA((7,)), pltpu.SemaphoreType.DMA((7,))],
        name="small_allreduce_adamw")(part, w, m, v)


def _row_tile(rows, cap):
    return max(t for t in range(16, min(rows, cap) + 1, 16) if rows % t == 0)


def _reduce_pair(name, parts):
    _, rows, w = parts.shape
    tr = _row_tile(rows, 576)
    nt = rows // tr

    def body(p_ref, o_ref, mine, theirs, send_sems, recv_sems):
        i = pl.program_id(0)
        x, y, c = _place()

        def copy(t):
            rows_t = pl.ds(pl.multiple_of(t * tr, tr), tr)
            return pltpu.make_async_remote_copy(src_ref=mine.at[rows_t], dst_ref=theirs.at[rows_t], send_sem=send_sems.at[t],
                                                recv_sem=recv_sems.at[t], device_id=(x, y, 1 - c), device_id_type=MESH)

        @pl.when(i < nt)
        def _():
            mine[pl.ds(pl.multiple_of(i * tr, tr), tr), :] = (
                (p_ref[0].astype(F32) + p_ref[1].astype(F32)) + p_ref[2].astype(F32)) + p_ref[3].astype(F32)
            copy(i).start()

        @pl.when(i >= nt)
        def _():
            t = i - nt
            copy(t).wait()
            rows_t = pl.ds(pl.multiple_of(t * tr, tr), tr)
            o_ref[...] = mine[rows_t, :] + theirs[rows_t, :]

    return pl.pallas_call(
        body, out_shape=jax.ShapeDtypeStruct((rows, w), F32), grid=(2 * nt,),
        in_specs=[pl.BlockSpec((4, tr, w), lambda i: (0, jnp.minimum(i, nt - 1), 0))],
        out_specs=pl.BlockSpec((tr, w), lambda i: (jnp.maximum(i - nt, 0), 0)),
        scratch_shapes=[pltpu.VMEM((rows, w), F32), pltpu.VMEM((rows, w), F32), pltpu.SemaphoreType.DMA((nt,)),
                        pltpu.SemaphoreType.DMA((nt,))],
        compiler_params=_params(("arbitrary",)), name=name)(parts)


def _adamw_shard(name, g, w, m, v):
    rows, cols = w.shape
    tr = _row_tile(rows, 256)

    def body(g_ref, w_ref, m_ref, v_ref, d_out, m_out, v_out):
        delta, m_new, v_new = _adamw(w_ref[...], g_ref[...], m_ref[...], v_ref[...])
        d_out[...] = delta
        m_out[...] = m_new
        v_out[...] = v_new

    t_spec = pl.BlockSpec((tr, cols), lambda i: (i, 0))
    shp = jax.ShapeDtypeStruct((rows, cols), F32)
    return pl.pallas_call(body, out_shape=[shp] * 3, grid=(rows // tr,), in_specs=[t_spec] * 4, out_specs=[t_spec] * 3,
                          compiler_params=_params(("parallel",)), name=name)(g, w, m, v)


def _shard_shape(name):
    _, r, c, ax = BIG_BY_NAME[name]
    return (r, c // 4) if ax == 1 else (r // 4, c)


def _pad_rows(a, axis):
    pad = [(0, 0)] * a.ndim
    pad[axis] = (0, -a.shape[axis] % PACK_ALIGN)
    return jnp.pad(a, pad)


def _pack_shards(names, shards, dtype):
    return _pad_rows(jnp.concatenate([s.astype(dtype).reshape(-1, PACK_W) for s in shards], axis=0), 0)


def _unpack_shards(names, slab):
    out, off = [], 0
    for name in names:
        rs, cs = _shard_shape(name)
        n = rs * cs // PACK_W
        out.append(slab[off:off + n].reshape(rs, cs))
        off += n
    return out


def _unpack_full(names, slabs):
    out, off = [], 0
    for name in names:
        _, r, c, ax = BIG_BY_NAME[name]
        n = r * c // 4 // PACK_W
        seg = slabs[:, off:off + n]
        out.append(seg.reshape(4, r, c // 4).transpose(1, 0, 2).reshape(r, c) if ax == 1 else seg.reshape(r, c))
        off += n
    return out


def _pack_full(names, mats, dtype):
    segs = []
    for name, a in zip(names, mats):
        _, r, c, ax = BIG_BY_NAME[name]
        a = a.astype(dtype)
        a = a.reshape(r, 4, c // 4).transpose(1, 0, 2) if ax == 1 else a
        segs.append(a.reshape(4, -1, PACK_W))
    return _pad_rows(jnp.concatenate(segs, axis=1), 1)


def _pad_heads_cols(wm, heads, d):
    k = wm.shape[0]
    return jnp.pad(wm.reshape(k, heads, d), ((0, 0), (0, 0), (0, HEAD_W - d))).reshape(k, heads * HEAD_W)


def _unpad_heads_cols(wm, heads, d):
    k = wm.shape[0]
    return wm.reshape(k, heads, HEAD_W)[:, :, :d].reshape(k, heads * d)


def _win_ext(w_in):
    o = np.cumsum([0, Q_LORA, KV_LORA, QK_ROPE, H_B * HD_B, KV_B * HD_B, KV_B * HD_B, D_MODEL, D_MODEL])
    pc = lambda a, n: jnp.pad(a, ((0, 0), (0, n - a.shape[1])))
    return jnp.concatenate([
        _pad_heads_cols(w_in[:, o[3]:o[4]], H_B, HD_B), w_in[:, o[0]:o[1]], w_in[:, o[1]:o[2]], pc(w_in[:, o[2]:o[3]], HEAD_W),
        _pad_heads_cols(w_in[:, o[4]:o[5]], KV_B, HD_B), _pad_heads_cols(w_in[:, o[5]:o[6]], KV_B, HD_B),
        w_in[:, o[6]:o[7]], w_in[:, o[7]:o[8]]], axis=1)


def _win_unext(we):
    c = HEAD_W
    return jnp.concatenate([
        we[:, ZC_QLAT * c:ZC_CKV * c], we[:, ZC_CKV * c:ZC_KPE * c], we[:, ZC_KPE * c:ZC_KPE * c + QK_ROPE],
        _unpad_heads_cols(we[:, ZC_QB * c:ZC_QLAT * c], H_B, HD_B), _unpad_heads_cols(we[:, ZC_KB * c:ZC_VB * c], KV_B, HD_B),
        _unpad_heads_cols(we[:, ZC_VB * c:ZC_GA * c], KV_B, HD_B), we[:, ZC_GA * c:]], axis=1)


def _wkv_ext(w_kvb):
    wk = w_kvb.reshape(KV_LORA, H_A, QK_NOPE + V_DIM_A)
    k_cols = jnp.pad(wk[:, :, :QK_NOPE], ((0, 0), (0, 0), (0, HEAD_W - QK_NOPE))).reshape(KV_LORA, H_A * HEAD_W)
    v_cols = jnp.pad(wk[:, :, QK_NOPE:], ((0, 0), (0, 0), (0, HEAD_W - V_DIM_A))).reshape(KV_LORA, H_A * HEAD_W)
    eye = jnp.pad(jnp.eye(QK_ROPE, dtype=w_kvb.dtype), ((0, 0), (QK_NOPE, HEAD_W - QK_NOPE - QK_ROPE)))
    pe_rows = jnp.concatenate([jnp.tile(eye, (1, H_A)), jnp.zeros((QK_ROPE, H_A * HEAD_W), w_kvb.dtype)], axis=1)
    top = jnp.concatenate([k_cols, v_cols], axis=1)
    return jnp.concatenate([top, pe_rows, jnp.zeros((2 * HEAD_W - KV_LORA - QK_ROPE, 2 * H_A * HEAD_W), w_kvb.dtype)], axis=0)


def _wkv_unext(we):
    k_cols = we[:KV_LORA, :H_A * HEAD_W].reshape(KV_LORA, H_A, HEAD_W)[:, :, :QK_NOPE]
    v_cols = we[:KV_LORA, H_A * HEAD_W:].reshape(KV_LORA, H_A, HEAD_W)[:, :, :V_DIM_A]
    return jnp.concatenate([k_cols, v_cols], axis=2).reshape(KV_LORA, H_A * (QK_NOPE + V_DIM_A))


def _pad_heads_rows(wm, heads, d):
    n = wm.shape[1]
    return jnp.pad(wm.reshape(heads, d, n), ((0, 0), (0, HEAD_W - d), (0, 0))).reshape(heads * HEAD_W, n)


def _unpad_heads_rows(wm, heads, d):
    n = wm.shape[1]
    return wm.reshape(heads, HEAD_W, n)[:, :d].reshape(heads * d, n)


def _rope_tables(seq):
    def ang(pos, dim):
        inv = np.float32(ROPE_THETA) ** (-np.arange(0, dim, 2, dtype=np.float32) / np.float32(dim))
        return pos.astype(np.float32)[:, None] * inv[None, :]

    def rot(dim):
        r = np.zeros((dim, dim), np.float32)
        half = dim // 2
        r[np.arange(half) + half, np.arange(half)] = -1.0
        r[np.arange(half), np.arange(half) + half] = 1.0
        return r

    def table(blocks):
        cos, sin = np.ones((seq, HEAD_W), np.float32), np.zeros((seq, HEAD_W), np.float32)
        pm = np.zeros((HEAD_W, HEAD_W), np.float32)
        for c0, a in blocks:
            d = 2 * a.shape[1]
            cos[:, c0:c0 + d] = np.concatenate([np.cos(a), np.cos(a)], axis=1)
            sin[:, c0:c0 + d] = np.concatenate([np.sin(a), np.sin(a)], axis=1)
            pm[c0:c0 + d, c0:c0 + d] = rot(d)
        return jnp.asarray(cos), jnp.asarray(sin), jnp.asarray(pm, BF16), jnp.asarray(pm.T, BF16)

    tok = np.arange(seq)
    a1 = ang(tok, QK_ROPE)
    arow, acol = ang(tok // GRID_W, HD_B // 2), ang(tok % GRID_W, HD_B // 2)
    return table([(QK_NOPE, a1)]), table([(0, a1)]), table([(0, arow), (HD_B // 2, acol)])


def _local_step(x, p, tgt, gains, wts, ride=None):
    nb, seq, _ = x.shape
    t = nb * seq
    x0 = x.reshape(t, D_MODEL)
    p2 = p.reshape(t, PLE_DIM)
    tg = tgt.reshape(t, D_MODEL)
    (cq_t, sq_t, pq, pq_t), (ck_t, sk_t, pk, pk_t), (cb_t, sb_t, pb, pb_t) = _rope_tables(seq)
    padg = lambda g: jnp.pad(g, ((0, 0), (0, HEAD_W - g.shape[1])))
    g_qn, g_kn = padg(gains["g_qn"]), padg(gains["g_kn"])

    win = _win_ext(wts["w_in"])
    wqb = _pad_heads_cols(wts["w_qb"], H_A, QK_NOPE + QK_ROPE)
    wkv = _wkv_ext(wts["w_kvb"])

    norm = lambda n: (lambda v, g: (_rms(v, g, n),))
    full = lambda a: (a, a.shape[1], 0, False)

    h = _rowwise("norm_mix", norm(D_MODEL), [full(x0)], [(D_MODEL, BF16, D_MODEL, False)], consts=[gains["g_mix"]])
    z = _mm("in_proj", h, win, tn=2048)
    cq = _rowwise("norm_qa", norm(Q_LORA), [(z, Q_LORA, ZC_QLAT // 2, False)], [(Q_LORA, BF16, Q_LORA, False)],
                  consts=[gains["g_qa"]])
    qa = _mm("q_up", cq, wqb)

    def rope_fwd(scale):
        return lambda v, cos, sin, pm: ((v * cos + _perm(v, pm) * sin) * scale,)

    heads_tile = lambda n: (n * HEAD_W, BF16, n * HEAD_W, False)
    q_a = _rowwise("rope_qa", _per_head(rope_fwd(SCALE_A), H_A, 1, 1), [full(qa)], [heads_tile(H_A)],
                   pos=[cq_t, sq_t], consts=[pq], seq=seq)
    ckv = _rowwise("norm_kva", norm(KV_LORA), [(z, HEAD_W, ZC_CKV, False)], [(HEAD_W, BF16, HEAD_W, False)], consts=[gains["g_kva"]])
    kpe = _rowwise("rope_kpe", rope_fwd(1.0), [(z, HEAD_W, ZC_KPE, False)], [(HEAD_W, BF16, HEAD_W, False)],
                   pos=[ck_t, sk_t], consts=[pk], seq=seq)
    kin = jnp.concatenate([ckv, kpe], axis=1)
    kv_a = _mm("kv_up", kin, wkv, out_dtypes=(BF16,))
    o_a, lse_a, *got_a = _attn_fwd("attn_a_fwd", q_a, kv_a, 0, kv_a, H_A, heads=H_A, group=1, nseq=nb, seq=seq,
                                   rider=ride and ride["gather_a"])

    def prep_fwd(scale):
        def fn(v, cos, sin, g, pm):
            yv = _rms(v, g, HD_B)
            return ((yv * cos + _perm(yv, pm) * sin) * scale,)
        return fn

    z_qb, z_kb = (z, H_B * HEAD_W, ZC_QB // H_B, False), (z, KV_B * HEAD_W, ZC_KB // KV_B, False)
    q_b = _rowwise("prep_qb", _per_head(prep_fwd(SCALE_B), H_B, 1, 1), [z_qb], [heads_tile(H_B)],
                   pos=[cb_t, sb_t], consts=[g_qn, pb], seq=seq)
    k_b = _rowwise("prep_kb", _per_head(prep_fwd(1.0), KV_B, 1, 1), [z_kb], [heads_tile(KV_B)],
                   pos=[cb_t, sb_t], consts=[g_kn, pb], seq=seq)
    o_b, lse_b, *got_b = _attn_fwd("attn_b_fwd", q_b, k_b, 0, z, ZC_VB, heads=H_B, group=H_B // KV_B, nseq=nb, seq=seq,
                                   rider=ride and ride["gather_b"])
    if ride is not None:
        wts = {**wts, **ride["late_weights"](got_a[0], got_b[0])}
    woa = _pad_heads_rows(wts["w_oa"], H_A, V_DIM_A)
    wob = _pad_heads_rows(wts["w_ob"], H_B, HD_B)
    wo, wup, wdown, wpg, wple = wts["w_o"], wts["w_up"], wts["w_down"], wts["w_ple_gate"], wts["w_ple"]
    ya = _mm("out_a", o_a, woa)
    yb = _mm("out_b", o_b, wob)

    z_ga, z_gb = (z, D_MODEL, ZC_GA // 8, False), (z, D_MODEL, ZC_GB // 8, False)
    merged = _rowwise("merge", lambda ga, gb, a, b: (_sigmoid(ga) * a + _sigmoid(gb) * b,),
                      [z_ga, z_gb, full(ya), full(yb)], [(D_MODEL, BF16, D_MODEL, False)])
    x1 = _mm("out_proj", merged, wo, epi=lambda acc, r: (r + acc,), extras=(x0,))
    h2 = _rowwise("norm_mlp", norm(D_MODEL), [full(x1)], [(D_MODEL, BF16, D_MODEL, False)], consts=[gains["g_mlp"]])

    def relu2(acc):
        u = jnp.maximum(acc, 0.0)
        return u, u * u

    u, usq = _mm("mlp_up", h2, wup, out_dtypes=(BF16, BF16), epi=relu2, tn=2048)
    x2 = _mm("mlp_down", usq, wdown, epi=lambda acc, r: (r + acc,), extras=(x1,))
    h3 = _rowwise("norm_ple", norm(D_MODEL), [full(x2)], [(D_MODEL, BF16, D_MODEL, False)], consts=[gains["g_ple"]])
    gpre = _mm("ple_gate", h3, wpg)
    pe = _mm("ple_proj", p2, wple)

    def tail(x2v, gp, pev, tv, gf):
        sg = _sigmoid(gp)
        x3 = x2v + sg * pev
        rs = lax.rsqrt(jnp.sum(x3 * x3, axis=-1, keepdims=True) * (1.0 / D_MODEL) + EPS)
        xh = x3 * rs
        err = xh * gf - tv
        dy = err * (1.0 / D_MODEL)
        dyg = dy * gf
        dx3 = rs * (dyg - xh * (jnp.sum(dyg * xh, axis=-1, keepdims=True) * (1.0 / D_MODEL)))
        return (dx3, dx3 * pev * sg * (1.0 - sg), dx3 * sg,
                jnp.sum(err * err, axis=0, keepdims=True) * (0.5 / D_MODEL), jnp.sum(dy * xh, axis=0, keepdims=True))

    dx3, dgpre, dpe, loss_part, dg_final = _rowwise(
        "tail", tail, [full(x2), full(gpre), full(pe), full(tg)],
        [(D_MODEL, F32, D_MODEL, False), (D_MODEL, BF16, D_MODEL, False), (D_MODEL, BF16, D_MODEL, False)],
        consts=[gains["g_final"].reshape(1, D_MODEL)], accs=[(1, D_MODEL), (1, D_MODEL)], tm=256)

    def norm_bwd(n, with_res):
        if with_res:
            def fn(dh, v, res, g):
                dx, dg = _rms_bwd(dh, v, g, n)
                return dx + res, dg
        else:
            def fn(dh, v, g):
                return _rms_bwd(dh, v, g, n)
        return fn

    dw = {}
    dw["w_ple"] = _mm_tn("dw_ple", p2, dpe)
    dw["w_ple_gate"] = _mm_tn("dw_ple_gate", h3, dgpre)
    dh3 = _mm("d_ple_gate", dgpre, wpg, trans_b=True)
    dx2, dg_ple = _rowwise("norm_ple_bwd", norm_bwd(D_MODEL, True), [full(dh3), full(x2), full(dx3)],
                           [(D_MODEL, F32, D_MODEL, False)], consts=[gains["g_ple"]], accs=[(1, D_MODEL)], tm=256)
    dw["w_down"] = _mm_tn("dw_down", usq, dx2)
    dupre = _mm("d_mlp_down", dx2, wdown, trans_b=True, out_dtypes=(BF16,), epi=lambda acc, uv: (acc * (2.0 * uv.astype(F32)),),
                extras=(u,), tn=2048)
    dw["w_up"] = _mm_tn("dw_up", h2, dupre)
    dh2 = _mm("d_mlp_up", dupre, wup, trans_b=True)
    dx1, dg_mlp = _rowwise("norm_mlp_bwd", norm_bwd(D_MODEL, True), [full(dh2), full(x1), full(dx2)],
                           [(D_MODEL, F32, D_MODEL, False)], consts=[gains["g_mlp"]], accs=[(1, D_MODEL)], tm=256)
    dw["w_o"] = _mm_tn("dw_o", merged, dx1)
    dmerged = _mm("d_out_proj", dx1, wo, trans_b=True)

    def merge_bwd(dm, ga, gb, a, b):
        sa, sb = _sigmoid(ga), _sigmoid(gb)
        return dm * sa, dm * sb, dm * a * sa * (1.0 - sa), dm * b * sb * (1.0 - sb)

    dya, dyb, dga, dgb = _rowwise("merge_bwd", merge_bwd, [full(dmerged), z_ga, z_gb, full(ya), full(yb)],
                                  [(D_MODEL, BF16, D_MODEL, False)] * 4, tm=256)
    dw["w_oa"] = _unpad_heads_rows(_mm_tn("dw_oa", o_a, dya), H_A, V_DIM_A)
    dw["w_ob"] = _unpad_heads_rows(_mm_tn("dw_ob", o_b, dyb), H_B, HD_B)
    do_a = _mm("d_out_a", dya, woa, trans_b=True, out_dtypes=(BF16,))
    do_b = _mm("d_out_b", dyb, wob, trans_b=True, out_dtypes=(BF16,))
    res_a = _attn_bwd("attn_a_bwd", q_a, kv_a, 0, kv_a, H_A, o_a, do_a, lse_a, heads=H_A, group=1, nseq=nb, seq=seq,
                      rider=ride and ride["scatter_late"](dw))
    dq_a, dk_a, dv_a = res_a[:3]
    if ride is not None:
        ride["out"]["parts_late"] = res_a[3]

    def rope_bwd(scale):
        return lambda d, cos, sin, pm_t: ((d * cos + _perm(d * sin, pm_t)) * scale,)

    dqa = _rowwise("rope_qa_bwd", _per_head(rope_bwd(SCALE_A), H_A, 1, 1), [full(dq_a)], [heads_tile(H_A)],
                   pos=[cq_t, sq_t], consts=[pq_t], seq=seq)
    dw["w_qb"] = _unpad_heads_cols(_mm_tn("dw_qb", cq, dqa), H_A, QK_NOPE + QK_ROPE)
    dcq = _mm("d_q_up", dqa, wqb, trans_b=True)
    dq_lat, dg_qa = _rowwise("norm_qa_bwd", norm_bwd(Q_LORA, False), [full(dcq), (z, Q_LORA, ZC_QLAT // 2, False)],
                             [(Q_LORA, BF16, Q_LORA, False)], consts=[gains["g_qa"]], accs=[(1, Q_LORA)])
    dkv_a = jnp.concatenate([dk_a, dv_a], axis=1)
    dw["w_kvb"] = _wkv_unext(_mm_tn("dw_kv", kin, dkv_a))
    dq_b, dk_b, dv_b, *parts_small = _attn_bwd("attn_b_bwd", q_b, k_b, 0, z, ZC_VB, o_b, do_b, lse_b, heads=H_B, group=H_B // KV_B,
                                               nseq=nb, seq=seq, rider=ride and ride["scatter_small"](dw))
    if ride is not None:
        ride["out"]["parts_small"] = parts_small[0]
    dkin = _mm("d_kv_up", dkv_a, wkv, trans_b=True)
    dckv, dg_kva = _rowwise("norm_kva_bwd", norm_bwd(KV_LORA, False), [(dkin, HEAD_W, 0, False), (z, HEAD_W, ZC_CKV, False)],
                            [(HEAD_W, BF16, HEAD_W, False)], consts=[gains["g_kva"]], accs=[(1, KV_LORA)])
    dkpe = _rowwise("rope_kpe_bwd", rope_bwd(1.0), [(dkin, HEAD_W, 1, False)], [(HEAD_W, BF16, HEAD_W, False)],
                    pos=[ck_t, sk_t], consts=[pk_t], seq=seq)

    def prep_bwd(scale):
        def fn(d, v, cos, sin, g, pm_t):
            dyv = (d * cos + _perm(d * sin, pm_t)) * scale
            return _rms_bwd(dyv, v, g, HD_B)
        return fn

    dqb, dg_qn = _rowwise("prep_qb_bwd", _per_head(prep_bwd(SCALE_B), H_B, 2, 1), [full(dq_b), z_qb], [heads_tile(H_B)],
                          pos=[cb_t, sb_t], consts=[g_qn, pb_t], accs=[(1, HEAD_W)], seq=seq)
    dkb, dg_kn = _rowwise("prep_kb_bwd", _per_head(prep_bwd(1.0), KV_B, 2, 1), [full(dk_b), z_kb], [heads_tile(KV_B)],
                          pos=[cb_t, sb_t], consts=[g_kn, pb_t], accs=[(1, HEAD_W)], seq=seq)

    dz = jnp.concatenate([dqb, dq_lat, dckv, dkpe, dkb, dv_b, dga, dgb], axis=1)
    dw["w_in"] = _win_unext(_mm_tn("dw_in", h, dz))
    dh = _mm("d_in_proj", dz, win, trans_b=True, rider=ride and ride["scatter_in"](dw))
    if ride is not None:
        dh, ride["out"]["parts_in"] = dh
    dx0, dg_mix = _rowwise("norm_mix_bwd", norm_bwd(D_MODEL, True), [full(dh), full(x0), full(dx1)],
                           [(D_MODEL, F32, D_MODEL, False)], consts=[gains["g_mix"]], accs=[(1, D_MODEL)], tm=256)

    dg = {"g_mix": dg_mix, "g_qa": dg_qa, "g_kva": dg_kva, "g_qn": dg_qn[:, :HD_B], "g_kn": dg_kn[:, :HD_B],
          "g_mlp": dg_mlp, "g_ple": dg_ple, "g_final": dg_final}
    return loss_part, dx0.reshape(nb, seq, D_MODEL), dg, dw


def _pack_small(vals, loss_part=None):
    flat = jnp.concatenate([vals[n].reshape(1, -1) for n, _ in SMALL], axis=1)
    loss = jnp.zeros((1, 8 * 128), F32) if loss_part is None else loss_part
    gap = jnp.zeros((1, LOSS_ROW0 * 128 - SMALL_N), F32)
    return jnp.concatenate([flat, gap, loss], axis=1).reshape(SMALL_ROWS, 128)


def _unpack_small(slab, like):
    flat, out, off = slab.reshape(-1), {}, 0
    for n, k in SMALL:
        out[n] = flat[off:off + k].reshape(like[n].shape)
        off += k
    return out


def kernel(x, p, g_mix, w_in, g_qa, w_qb, g_kva, w_kvb, g_qn, g_kn, w_oa, w_ob, w_o, g_mlp, w_up, w_down, g_ple, w_ple_gate, w_ple, g_final, loss_target, m_g_mix, m_w_in, m_g_qa, m_w_qb, m_g_kva, m_w_kvb, m_g_qn, m_g_kn, m_w_oa, m_w_ob, m_w_o, m_g_mlp, m_w_up, m_w_down, m_g_ple, m_w_ple_gate, m_w_ple, m_g_final, v_g_mix, v_w_in, v_g_qa, v_w_qb, v_g_kva, v_w_kvb, v_g_qn, v_g_kn, v_w_oa, v_w_ob, v_w_o, v_g_mlp, v_w_up, v_w_down, v_g_ple, v_w_ple_gate, v_w_ple, v_g_final):
    given = dict(locals())
    order = ["g_mix", "w_in", "g_qa", "w_qb", "g_kva", "w_kvb", "g_qn", "g_kn", "w_oa", "w_ob", "w_o", "g_mlp", "w_up",
             "w_down", "g_ple", "w_ple_gate", "w_ple", "g_final"]
    big_names = [n for n, _, _, _ in BIG]
    local = lambda prefix, names: [given[prefix + n][0] for n in names]
    late = LATE_A + LATE_B

    shards_cols = lambda a: a.reshape(a.shape[0], 4, a.shape[1] // 4).transpose(1, 0, 2)
    got_in, got_small = _exchange_alone("weight_gather_early", [
        _Exchange("gather", w_in[0].astype(BF16)), _Exchange("gather", _pack_shards(EARLY_SMALL, local("", EARLY_SMALL), BF16))])
    wts = {"w_in": got_in.transpose(1, 0, 2).reshape(got_in.shape[1], -1), **dict(zip(EARLY_SMALL, _unpack_full(EARLY_SMALL, got_small)))}
    gains = {n: given[n].reshape(1, -1) for n, _ in SMALL}
    ride = {
        "gather_a": _Exchange("gather", _pack_shards(LATE_A, local("", LATE_A), BF16)),
        "gather_b": _Exchange("gather", _pack_shards(LATE_B, local("", LATE_B), BF16)),
        "late_weights": lambda ga, gb: {**dict(zip(LATE_A, _unpack_full(LATE_A, ga))), **dict(zip(LATE_B, _unpack_full(LATE_B, gb)))},
        "scatter_late": lambda dw: _Exchange("scatter", _pack_full(late, [dw[n] for n in late], BF16)),
        "scatter_small": lambda dw: _Exchange("scatter", _pack_full(EARLY_SMALL, [dw[n] for n in EARLY_SMALL], BF16)),
        "scatter_in": lambda dw: _Exchange("scatter", shards_cols(dw["w_in"].astype(BF16))),
        "out": {},
    }
    loss_part, grad_x, dg, dw = _local_step(x, p[0], loss_target, gains, wts, ride)

    small = lambda prefix: _pack_small({n: given[prefix + n] for n, _ in SMALL})
    g_s, d_s, m_s, v_s, loss = _small_allreduce_adamw(_pack_small(dg, loss_part), small(""), small("m_"), small("v_"))

    grads = dict(zip(late, _unpack_shards(late, _reduce_pair("grad_reduce_late", ride["out"]["parts_late"]))))
    grads.update(zip(EARLY_SMALL, _unpack_shards(EARLY_SMALL, _reduce_pair("grad_reduce_small", ride["out"]["parts_small"]))))
    grads["w_in"] = _reduce_pair("grad_reduce_in", ride["out"]["parts_in"])

    res = {}
    for key, slab in (("grad_", g_s), ("delta_", d_s), ("new_m_", m_s), ("new_v_", v_s)):
        for n, val in _unpack_small(slab, given).items():
            res[key + n] = val
    for n in big_names:
        d_w, m_w, v_w = _adamw_shard("adamw_" + n, grads[n], given[n][0], given["m_" + n][0], given["v_" + n][0])
        res["grad_" + n], res["delta_" + n], res["new_m_" + n], res["new_v_" + n] = grads[n][None], d_w[None], m_w[None], v_w[None]
    outs = [loss.reshape(()), grad_x]
    for key in ("grad_", "delta_", "new_m_", "new_v_"):
        outs += [res[key + n] for n in order]
    return tuple(outs)
```

```python
import functools

import numpy as np
import jax
import jax.numpy as jnp
from jax import lax
from jax.experimental import pallas as pl
from jax.experimental.pallas import tpu as pltpu

F32 = jnp.float32
BF16 = jnp.bfloat16
MESH = pl.DeviceIdType.MESH

D_MODEL = 1024
GRID_W = 64
ROPE_THETA = 10000.0
EPS = 1e-6
H_A, QK_NOPE, QK_ROPE, V_DIM_A, Q_LORA, KV_LORA = 8, 64, 32, 64, 256, 128
H_B, KV_B, HD_B = 8, 2, 64
D_FF = 4096
PLE_DIM = 256
HEAD_W = 128
SCALE_A = (QK_NOPE + QK_ROPE) ** -0.5
SCALE_B = HD_B ** -0.5

ADAM_LR, ADAM_B1, ADAM_B2, ADAM_EPS, ADAM_WD, ADAM_STEP = 0.001, 0.9, 0.999, 1e-08, 0.01, 10
M_HAT_DIV = 1.0 - ADAM_B1 ** ADAM_STEP
V_HAT_DIV = 1.0 - ADAM_B2 ** ADAM_STEP

VMEM_LIMIT_BYTES = 56 * 1024 * 1024

ZC_QB, ZC_QLAT, ZC_CKV, ZC_KPE, ZC_KB, ZC_VB, ZC_GA, ZC_GB = 0, 8, 10, 11, 12, 14, 16, 24
Z_WIDTH = 32 * HEAD_W

BIG = [
    ("w_in", 1024, 3232, 1), ("w_qb", 256, 768, 1), ("w_kvb", 128, 1024, 1), ("w_oa", 512, 1024, 1),
    ("w_ob", 512, 1024, 1), ("w_o", 1024, 1024, 0), ("w_up", 1024, 4096, 1), ("w_down", 4096, 1024, 0),
    ("w_ple_gate", 1024, 1024, 0), ("w_ple", 256, 1024, 1),
]
BIG_BY_NAME = {e[0]: e for e in BIG}
PACK_W = 1024
PACK_ALIGN = 64
EARLY_SMALL = ["w_qb", "w_kvb"]
LATE_A = ["w_oa", "w_ob", "w_o", "w_up"]
LATE_B = ["w_down", "w_ple_gate", "w_ple"]

SMALL = [("g_mix", 1024), ("g_qa", 256), ("g_kva", 128), ("g_qn", 64), ("g_kn", 64), ("g_mlp", 1024),
         ("g_ple", 1024), ("g_final", 1024)]
SMALL_N = sum(n for _, n in SMALL)
LOSS_ROW0 = 40
SMALL_ROWS = 48


def _params(sem):
    return pltpu.CompilerParams(dimension_semantics=sem, vmem_limit_bytes=VMEM_LIMIT_BYTES)


def _sigmoid(v):
    return 1.0 / (1.0 + jnp.exp(-v))


def _perm(v, p_ref):
    pm = p_ref[...]
    hi = v.astype(BF16)
    lo = (v - hi.astype(F32)).astype(BF16)
    return (jnp.dot(hi, pm, preferred_element_type=F32) + jnp.dot(lo, pm, preferred_element_type=F32))


def _rms(v, g, n):
    rs = lax.rsqrt(jnp.sum(v * v, axis=-1, keepdims=True) * (1.0 / n) + EPS)
    return v * rs * g


def _rms_bwd(dy, v, g, n):
    rs = lax.rsqrt(jnp.sum(v * v, axis=-1, keepdims=True) * (1.0 / n) + EPS)
    vh = v * rs
    dyg = dy * g
    dx = rs * (dyg - vh * (jnp.sum(dyg * vh, axis=-1, keepdims=True) * (1.0 / n)))
    return dx, jnp.sum(dy * vh, axis=0, keepdims=True)


def _ride(body, grid, rider):
    if rider is None:
        return body, [], [], [], []
    n_sem = len(rider.scratch)

    def wrapped(*refs):
        ids = [pl.program_id(a) for a in range(len(grid))]
        n_in = len(refs) - n_sem - 2 - rider.n_core_out - rider.n_core_scratch
        core_in, src = refs[:n_in], refs[n_in]
        core_out = refs[n_in + 1:n_in + 1 + rider.n_core_out]
        dst = refs[n_in + 1 + rider.n_core_out]
        core_scr = refs[n_in + 2 + rider.n_core_out:len(refs) - n_sem]
        sems = refs[len(refs) - n_sem:]

        @pl.when(functools.reduce(jnp.logical_and, [a == 0 for a in ids]))
        def _():
            rider.start(src, dst, *sems)

        body(*core_in, *core_out, *core_scr)

        @pl.when(functools.reduce(jnp.logical_and, [a == n - 1 for a, n in zip(ids, grid)]))
        def _():
            rider.finish(src, dst, *sems)

    hbm = pl.BlockSpec(memory_space=pl.ANY)
    return wrapped, [rider.src], [hbm], [rider.out_shape], list(rider.scratch)


def _mm(name, a, b, *, trans_b=False, out_dtypes=(F32,), epi=None, extras=(), consts=(), accs=(), tm=512, tn=None, rider=None):
    m, k = a.shape
    n = b.shape[0] if trans_b else b.shape[1]
    tn = n if tn is None else min(tn, n)
    tm = min(tm, m)
    assert m % tm == 0 and n % tn == 0 and (b.shape[1] if trans_b else b.shape[0]) == k
    extras = [e if isinstance(e, tuple) else (e, 0) for e in extras]
    n_ex, n_c, n_out, n_acc = len(extras), len(consts), len(out_dtypes), len(accs)
    dims = (((1,), (1,)), ((), ())) if trans_b else (((1,), (0,)), ((), ()))

    def body(a_ref, b_ref, *rest):
        acc = lax.dot_general(a_ref[...].astype(BF16), b_ref[...].astype(BF16), dims, preferred_element_type=F32)
        res = (acc,) if epi is None else epi(acc, *[e[...] for e in rest[:n_ex + n_c]])
        o_refs = rest[n_ex + n_c:]
        for o_ref, r in zip(o_refs[:n_out], res[:n_out]):
            o_ref[...] = r.astype(o_ref.dtype)
        if n_acc:
            first = jnp.logical_and(pl.program_id(0) == 0, pl.program_id(1) == 0)

            @pl.when(first)
            def _():
                for o_ref, r in zip(o_refs[n_out:], res[n_out:]):
                    o_ref[...] = r

            @pl.when(jnp.logical_not(first))
            def _():
                for o_ref, r in zip(o_refs[n_out:], res[n_out:]):
                    o_ref[...] += r

    grid = (n // tn, m // tm)
    if rider is not None:
        rider.n_core_out, rider.n_core_scratch = n_out + n_acc, 0
    body, x_in, x_spec, x_out, x_scr = _ride(body, grid, rider)
    a_spec = pl.BlockSpec((tm, k), lambda j, i: (i, 0))
    b_spec = pl.BlockSpec((tn, k), lambda j, i: (j, 0)) if trans_b else pl.BlockSpec((k, tn), lambda j, i: (0, j))
    t_spec = pl.BlockSpec((tm, tn), lambda j, i: (i, j))
    e_specs = [pl.BlockSpec((tm, tn), lambda j, i, off=off: (i, j + off)) for _, off in extras]
    c_specs = [pl.BlockSpec(c.shape, lambda j, i: (0, 0)) for c in consts]
    acc_specs = [pl.BlockSpec(sh, lambda j, i: (0, 0)) for sh in accs]
    sem = ("parallel", "parallel") if rider is None and not n_acc else ("arbitrary", "arbitrary")
    outs = pl.pallas_call(
        body, out_shape=[jax.ShapeDtypeStruct((m, n), d) for d in out_dtypes] + [jax.ShapeDtypeStruct(sh, F32) for sh in accs] + x_out,
        grid=grid, in_specs=[a_spec, b_spec] + e_specs + c_specs + x_spec, out_specs=[t_spec] * n_out + acc_specs + x_spec,
        scratch_shapes=x_scr, compiler_params=_params(sem), name=name)(a, b, *[e for e, _ in extras], *consts, *x_in)
    return outs[0] if len(outs) == 1 else outs


def _per_head(fn, heads, n_tiled, n_out):
    def run(*args):
        res = [fn(*[a[:, hd * HEAD_W:(hd + 1) * HEAD_W] for a in args[:n_tiled]], *args[n_tiled:]) for hd in range(heads)]
        tiles = [jnp.concatenate([r[k] for r in res], axis=1) for k in range(n_out)]
        sums = [functools.reduce(lambda u, v: u + v, [r[k] for r in res]) for k in range(n_out, len(res[0]))]
        return (*tiles, *sums)
    return run


def _mm_tn(name, a, b, *, out_dtype=BF16, tk=1024, tn=1024, tt=1024):
    t, k = a.shape
    n = b.shape[1]
    tk, tn, tt = min(tk, k), min(tn, n), min(tt, t)
    assert b.shape[0] == t and k % tk == 0 and n % tn == 0 and t % tt == 0
    nt = t // tt

    def body(a_ref, b_ref, o_ref, acc):
        part = lax.dot_general(a_ref[...].astype(BF16), b_ref[...].astype(BF16), (((0,), (0,)), ((), ())),
                               preferred_element_type=F32)

        @pl.when(pl.program_id(2) == 0)
        def _():
            acc[...] = part

        @pl.when(pl.program_id(2) != 0)
        def _():
            acc[...] += part

        @pl.when(pl.program_id(2) == nt - 1)
        def _():
            o_ref[...] = acc[...].astype(o_ref.dtype)

    return pl.pallas_call(
        body, out_shape=jax.ShapeDtypeStruct((k, n), out_dtype), grid=(k // tk, n // tn, nt),
        in_specs=[pl.BlockSpec((tt, tk), lambda ki, ni, ti: (ti, ki)), pl.BlockSpec((tt, tn), lambda ki, ni, ti: (ti, ni))],
        out_specs=pl.BlockSpec((tk, tn), lambda ki, ni, ti: (ki, ni)), scratch_shapes=[pltpu.VMEM((tk, tn), F32)],
        compiler_params=_params(("parallel", "parallel", "arbitrary")), name=name)(a, b)


def _rowwise(name, fn, ins, outs, *, consts=(), pos=(), accs=(), heads=1, tm=512, seq=None):
    t = ins[0][0].shape[0]
    tm = min(tm, t if seq is None else seq)
    assert t % tm == 0 and (seq is None or seq % tm == 0)
    n_in, n_pos, n_c, n_out, n_acc = len(ins), len(pos), len(consts), len(outs), len(accs)

    def body(*refs):
        vals = [r[...] for r in refs[:n_in + n_pos + n_c]]
        res = fn(*vals)
        o_refs = refs[n_in + n_pos + n_c:]
        for o_ref, r in zip(o_refs[:n_out], res[:n_out]):
            o_ref[...] = r.astype(o_ref.dtype)
        if n_acc:
            first = jnp.logical_and(pl.program_id(0) == 0, pl.program_id(1) == 0)

            @pl.when(first)
            def _():
                for o_ref, r in zip(o_refs[n_out:], res[n_out:]):
                    o_ref[...] = r

            @pl.when(jnp.logical_not(first))
            def _():
                for o_ref, r in zip(o_refs[n_out:], res[n_out:]):
                    o_ref[...] += r

    def tiled(width, c0, per_head):
        return pl.BlockSpec((tm, width), (lambda h, i: (i, c0 + h)) if per_head else (lambda h, i: (i, c0)))

    in_specs = [tiled(w, c0, ph) for _, w, c0, ph in ins]
    if n_pos:
        nblk = seq // tm
        in_specs += [pl.BlockSpec((tm, a.shape[1]), lambda h, i: (i % nblk, 0)) for a in pos]
    in_specs += [pl.BlockSpec(a.shape, lambda h, i: (0, 0)) for a in consts]
    out_specs = [tiled(w, 0, ph) for _, _, w, ph in outs] + [pl.BlockSpec(s, lambda h, i: (0, 0)) for s in accs]
    out_shape = [jax.ShapeDtypeStruct((t, c), d) for c, d, _, _ in outs] + [jax.ShapeDtypeStruct(s, F32) for s in accs]
    sem = ("arbitrary", "arbitrary") if n_acc else ("parallel", "parallel")
    res = pl.pallas_call(body, out_shape=out_shape, grid=(heads, t // tm), in_specs=in_specs, out_specs=out_specs,
                         compiler_params=_params(sem), name=name)(*[a for a, _, _, _ in ins], *pos, *consts)
    return res[0] if len(res) == 1 else res


ATTN_HEADS_PER_STEP = 2


def _attn_fwd(name, q, k, kc0, v, vc0, *, heads, group, nseq, seq, tq=512, rider=None):
    tq = min(tq, seq)
    nq = seq // tq
    hp = ATTN_HEADS_PER_STEP
    grid = (heads // hp, nseq, nq)
    shared = group > 1
    assert group % hp == 0 if shared else (kc0 % hp == 0 and vc0 % hp == 0)

    def body(q_ref, k_ref, v_ref, o_ref, lse_ref):
        for j in range(hp):
            cols = slice(j * HEAD_W, (j + 1) * HEAD_W)
            kj = (k_ref[...] if shared else k_ref[:, cols]).astype(BF16)
            vj = (v_ref[...] if shared else v_ref[:, cols]).astype(BF16)
            s = lax.dot_general(q_ref[:, cols], kj, (((1,), (1,)), ((), ())), preferred_element_type=F32)
            m = jnp.max(s, axis=-1, keepdims=True)
            p = jnp.exp(s - m)
            l = jnp.sum(p, axis=-1, keepdims=True)
            o = jnp.dot(p.astype(BF16), vj, preferred_element_type=F32)
            o_ref[:, cols] = (o * (1.0 / l)).astype(o_ref.dtype)
            lse_ref[j] = m + jnp.log(l)

    if rider is not None:
        rider.n_core_out, rider.n_core_scratch = 2, 0
    body, x_in, x_spec, x_out, x_scr = _ride(body, grid, rider)
    q_spec = pl.BlockSpec((tq, hp * HEAD_W), lambda h, b, i: (b * nq + i, h))
    if shared:
        k_spec = pl.BlockSpec((seq, HEAD_W), lambda h, b, i: (b, kc0 + (h * hp) // group))
        v_spec = pl.BlockSpec((seq, HEAD_W), lambda h, b, i: (b, vc0 + (h * hp) // group))
    else:
        k_spec = pl.BlockSpec((seq, hp * HEAD_W), lambda h, b, i: (b, kc0 // hp + h))
        v_spec = pl.BlockSpec((seq, hp * HEAD_W), lambda h, b, i: (b, vc0 // hp + h))
    lse_spec = pl.BlockSpec((hp, tq, 1), lambda h, b, i: (h, b * nq + i, 0))
    sem = ("parallel",) * 3 if rider is None else ("arbitrary",) * 3
    return pl.pallas_call(
        body, out_shape=[jax.ShapeDtypeStruct(q.shape, BF16), jax.ShapeDtypeStruct((heads, q.shape[0], 1), F32)] + x_out,
        grid=grid, in_specs=[q_spec, k_spec, v_spec] + x_spec, out_specs=[q_spec, lse_spec] + x_spec, scratch_shapes=x_scr,
        compiler_params=_params(sem), name=name)(q, k, v, *x_in)


def _attn_bwd(name, q, k, kc0, v, vc0, o, do, lse, *, heads, group, nseq, seq, tq=256, rider=None):
    tq = min(tq, seq)
    nq = seq // tq
    hk = heads // group
    t = q.shape[0]
    grid = (hk, nseq, group, nq)

    def body(q_ref, k_ref, v_ref, o_ref, do_ref, lse_ref, dq_ref, dk_ref, dv_ref, dk_acc, dv_acc):
        g, i = pl.program_id(2), pl.program_id(3)
        qv, kv, vv, dov = q_ref[...], k_ref[...].astype(BF16), v_ref[...].astype(BF16), do_ref[...]
        s = lax.dot_general(qv, kv, (((1,), (1,)), ((), ())), preferred_element_type=F32)
        pn = jnp.exp(s - lse_ref[...])
        dp = lax.dot_general(dov, vv, (((1,), (1,)), ((), ())), preferred_element_type=F32)
        delta = jnp.sum(dov.astype(F32) * o_ref[...].astype(F32), axis=-1, keepdims=True)
        ds = (pn * (dp - delta)).astype(BF16)
        dq_ref[...] = jnp.dot(ds, kv, preferred_element_type=F32)
        dk_part = lax.dot_general(ds, qv, (((0,), (0,)), ((), ())), preferred_element_type=F32)
        dv_part = lax.dot_general(pn.astype(BF16), dov, (((0,), (0,)), ((), ())), preferred_element_type=F32)
        first = jnp.logical_and(g == 0, i == 0)

        @pl.when(first)
        def _():
            dk_acc[...] = dk_part
            dv_acc[...] = dv_part

        @pl.when(jnp.logical_not(first))
        def _():
            dk_acc[...] += dk_part
            dv_acc[...] += dv_part

        @pl.when(jnp.logical_and(g == group - 1, i == nq - 1))
        def _():
            dk_ref[...] = dk_acc[...].astype(dk_ref.dtype)
            dv_ref[...] = dv_acc[...].astype(dv_ref.dtype)

    if rider is not None:
        rider.n_core_out, rider.n_core_scratch = 3, 2
    body, x_in, x_spec, x_out, x_scr = _ride(body, grid, rider)
    q_spec = pl.BlockSpec((tq, HEAD_W), lambda kh, b, g, i: (b * nq + i, kh * group + g))
    kv_out = pl.BlockSpec((seq, HEAD_W), lambda kh, b, g, i: (b, kh))
    lse_spec = pl.BlockSpec((None, tq, 1), lambda kh, b, g, i: (kh * group + g, b * nq + i, 0))
    sem = ("parallel", "parallel", "arbitrary", "arbitrary") if rider is None else ("arbitrary",) * 4
    return pl.pallas_call(
        body,
        out_shape=[jax.ShapeDtypeStruct(q.shape, F32), jax.ShapeDtypeStruct((t, hk * HEAD_W), BF16),
                   jax.ShapeDtypeStruct((t, hk * HEAD_W), BF16)] + x_out,
        grid=grid,
        in_specs=[q_spec, pl.BlockSpec((seq, HEAD_W), lambda kh, b, g, i: (b, kc0 + kh)),
                  pl.BlockSpec((seq, HEAD_W), lambda kh, b, g, i: (b, vc0 + kh)), q_spec, q_spec, lse_spec] + x_spec,
        out_specs=[q_spec, kv_out, kv_out] + x_spec,
        scratch_shapes=[pltpu.VMEM((seq, HEAD_W), F32), pltpu.VMEM((seq, HEAD_W), F32)] + x_scr,
        compiler_params=_params(sem), name=name)(q, k, v, o, do, lse, *x_in)


def _place():
    return lax.axis_index("x"), lax.axis_index("y"), lax.axis_index("c")


def _other_chips(x, y):
    return [(1 - x, y), (x, 1 - y), (1 - x, 1 - y)]


class _Exchange:
    def __init__(self, kind, src):
        assert kind in ("gather", "scatter")
        self.kind, self.src = kind, src
        rows, w = src.shape[-2:]
        self.out_shape = jax.ShapeDtypeStruct((4, rows, w), src.dtype)
        self.scratch = [pltpu.SemaphoreType.DMA((3,)), pltpu.SemaphoreType.DMA((3,)), pltpu.SemaphoreType.DMA(())]
        self.n_core_out = self.n_core_scratch = 0

    def _copies(self, src_ref, out_ref, send_sems, recv_sems, landing):
        x, y, c = _place()

        def remote(k, s, d, to):
            return pltpu.make_async_remote_copy(src_ref=s, dst_ref=d, send_sem=send_sems.at[k], recv_sem=recv_sems.at[k],
                                                device_id=to, device_id_type=MESH)

        me = 2 * x + y
        part = (lambda j: src_ref) if self.kind == "gather" else (lambda j: src_ref.at[j])
        if landing:
            return [remote(k, part(me), out_ref.at[2 * px + py], (px, py, c)) for k, (px, py) in enumerate(_other_chips(x, y))]
        return [remote(k, part(2 * px + py), out_ref.at[me], (px, py, c)) for k, (px, py) in enumerate(_other_chips(x, y))]

    def _local(self, src_ref, out_ref, local_sem):
        x, y, _ = _place()
        me = 2 * x + y
        return pltpu.make_async_copy(src_ref if self.kind == "gather" else src_ref.at[me], out_ref.at[me], local_sem)

    def start(self, src_ref, out_ref, send_sems, recv_sems, local_sem):
        self._local(src_ref, out_ref, local_sem).start()
        for mine in self._copies(src_ref, out_ref, send_sems, recv_sems, False):
            mine.start()

    def finish(self, src_ref, out_ref, send_sems, recv_sems, local_sem):
        for landed in self._copies(src_ref, out_ref, send_sems, recv_sems, True):
            landed.wait_recv()
        for mine in self._copies(src_ref, out_ref, send_sems, recv_sems, False):
            mine.wait_send()
        self._local(src_ref, out_ref, local_sem).wait()


def _exchange_alone(name, exchanges):
    n = len(exchanges)

    def body(*refs):
        args = [(refs[j], refs[n + j], *refs[2 * n + 3 * j:2 * n + 3 * j + 3]) for j in range(n)]
        for ex, a in zip(exchanges, args):
            ex.start(*a)
        for ex, a in zip(exchanges, args):
            ex.finish(*a)

    hbm = pl.BlockSpec(memory_space=pl.ANY)
    return pl.pallas_call(body, out_shape=[ex.out_shape for ex in exchanges], in_specs=[hbm] * n, out_specs=[hbm] * n,
                          scratch_shapes=[s for ex in exchanges for s in ex.scratch], name=name)(*[ex.src for ex in exchanges])


def _adamw(w, g, m, v):
    m = ADAM_B1 * m + (1.0 - ADAM_B1) * g
    v = ADAM_B2 * v + (1.0 - ADAM_B2) * (g * g)
    delta = -ADAM_LR * ((m / M_HAT_DIV) / (jnp.sqrt(v / V_HAT_DIV) + ADAM_EPS) + ADAM_WD * w)
    return delta, m, v


def _small_allreduce_adamw(part, w, m, v):
    def body(part_ref, w_ref, m_ref, v_ref, g_out, d_out, m_out, v_out, loss_out, buf, send_sems, recv_sems):
        x, y, c = _place()
        me = 4 * x + 2 * y + c
        buf[me] = part_ref[...]

        def flip(k):
            fx, fy, fc = (k >> 2) & 1, (k >> 1) & 1, k & 1
            px, py, pc = (1 - x if fx else x), (1 - y if fy else y), (1 - c if fc else c)
            return (px, py, pc), 4 * px + 2 * py + pc

        def copy(k, slot):
            return pltpu.make_async_remote_copy(
                src_ref=part_ref, dst_ref=buf.at[slot], send_sem=send_sems.at[k - 1], recv_sem=recv_sems.at[k - 1],
                device_id=flip(k)[0], device_id_type=MESH)

        sent = [copy(k, me) for k in range(1, 8)]
        for cp in sent:
            cp.start()
        for k in range(1, 8):
            copy(k, flip(k)[1]).wait_recv()
        for cp in sent:
            cp.wait_send()
        tot = buf[0]
        for j in range(1, 8):
            tot = tot + buf[j]
        delta, m_new, v_new = _adamw(w_ref[...], tot, m_ref[...], v_ref[...])
        g_out[...] = tot
        d_out[...] = delta
        m_out[...] = m_new
        v_out[...] = v_new
        loss_out[...] = jnp.sum(tot[LOSS_ROW0:LOSS_ROW0 + 8, :]).reshape(1, 1)

    vm = pl.BlockSpec(memory_space=pltpu.VMEM)
    shp = jax.ShapeDtypeStruct((SMALL_ROWS, 128), F32)
    return pl.pallas_call(
        body, out_shape=[shp, shp, shp, shp, jax.ShapeDtypeStruct((1, 1), F32)],
        in_specs=[vm, vm, vm, vm], out_specs=[vm, vm, vm, vm, vm],
        scratch_shapes=[pltpu.VMEM((8, SMALL_ROWS, 128), F32), pltpu.SemaphoreType.DMA((7,)), pltpu.SemaphoreType.DMA((7,))],
        name="small_allreduce_adamw")(part, w, m, v)


def _row_tile(rows, cap):
    return max(t for t in range(16, min(rows, cap) + 1, 16) if rows % t == 0)


def _reduce_pair(name, parts):
    _, rows, w = parts.shape
    tr = _row_tile(rows, 576)
    nt = rows // tr

    def body(p_ref, o_ref, mine, theirs, send_sems, recv_sems):
        i = pl.program_id(0)
        x, y, c = _place()

        def copy(t):
            rows_t = pl.ds(pl.multiple_of(t * tr, tr), tr)
            return pltpu.make_async_remote_copy(src_ref=mine.at[rows_t], dst_ref=theirs.at[rows_t], send_sem=send_sems.at[t],
                                                recv_sem=recv_sems.at[t], device_id=(x, y, 1 - c), device_id_type=MESH)

        @pl.when(i < nt)
        def _():
            mine[pl.ds(pl.multiple_of(i * tr, tr), tr), :] = (
                (p_ref[0].astype(F32) + p_ref[1].astype(F32)) + p_ref[2].astype(F32)) + p_ref[3].astype(F32)
            copy(i).start()

        @pl.when(i >= nt)
        def _():
            t = i - nt
            copy(t).wait()
            rows_t = pl.ds(pl.multiple_of(t * tr, tr), tr)
            o_ref[...] = mine[rows_t, :] + theirs[rows_t, :]

    return pl.pallas_call(
        body, out_shape=jax.ShapeDtypeStruct((rows, w), F32), grid=(2 * nt,),
        in_specs=[pl.BlockSpec((4, tr, w), lambda i: (0, jnp.minimum(i, nt - 1), 0))],
        out_specs=pl.BlockSpec((tr, w), lambda i: (jnp.maximum(i - nt, 0), 0)),
        scratch_shapes=[pltpu.VMEM((rows, w), F32), pltpu.VMEM((rows, w), F32), pltpu.SemaphoreType.DMA((nt,)),
                        pltpu.SemaphoreType.DMA((nt,))],
        compiler_params=_params(("arbitrary",)), name=name)(parts)


def _adamw_shard(name, g, w, m, v):
    rows, cols = w.shape
    tr = _row_tile(rows, 256)

    def body(g_ref, w_ref, m_ref, v_ref, d_out, m_out, v_out):
        delta, m_new, v_new = _adamw(w_ref[...], g_ref[...], m_ref[...], v_ref[...])
        d_out[...] = delta
        m_out[...] = m_new
        v_out[...] = v_new

    t_spec = pl.BlockSpec((tr, cols), lambda i: (i, 0))
    shp = jax.ShapeDtypeStruct((rows, cols), F32)
    return pl.pallas_call(body, out_shape=[shp] * 3, grid=(rows // tr,), in_specs=[t_spec] * 4, out_specs=[t_spec] * 3,
                          compiler_params=_params(("parallel",)), name=name)(g, w, m, v)


def _shard_shape(name):
    _, r, c, ax = BIG_BY_NAME[name]
    return (r, c // 4) if ax == 1 else (r // 4, c)


def _pad_rows(a, axis):
    pad = [(0, 0)] * a.ndim
    pad[axis] = (0, -a.shape[axis] % PACK_ALIGN)
    return jnp.pad(a, pad)


def _pack_shards(names, shards, dtype):
    return _pad_rows(jnp.concatenate([s.astype(dtype).reshape(-1, PACK_W) for s in shards], axis=0), 0)


def _unpack_shards(names, slab):
    out, off = [], 0
    for name in names:
        rs, cs = _shard_shape(name)
        n = rs * cs // PACK_W
        out.append(slab[off:off + n].reshape(rs, cs))
        off += n
    return out


def _unpack_full(names, slabs):
    out, off = [], 0
    for name in names:
        _, r, c, ax = BIG_BY_NAME[name]
        n = r * c // 4 // PACK_W
        seg = slabs[:, off:off + n]
        out.append(seg.reshape(4, r, c // 4).transpose(1, 0, 2).reshape(r, c) if ax == 1 else seg.reshape(r, c))
        off += n
    return out


def _pack_full(names, mats, dtype):
    segs = []
    for name, a in zip(names, mats):
        _, r, c, ax = BIG_BY_NAME[name]
        a = a.astype(dtype)
        a = a.reshape(r, 4, c // 4).transpose(1, 0, 2) if ax == 1 else a
        segs.append(a.reshape(4, -1, PACK_W))
    return _pad_rows(jnp.concatenate(segs, axis=1), 1)


def _pad_heads_cols(wm, heads, d):
    k = wm.shape[0]
    return jnp.pad(wm.reshape(k, heads, d), ((0, 0), (0, 0), (0, HEAD_W - d))).reshape(k, heads * HEAD_W)


def _unpad_heads_cols(wm, heads, d):
    k = wm.shape[0]
    return wm.reshape(k, heads, HEAD_W)[:, :, :d].reshape(k, heads * d)


def _win_ext(w_in):
    o = np.cumsum([0, Q_LORA, KV_LORA, QK_ROPE, H_B * HD_B, KV_B * HD_B, KV_B * HD_B, D_MODEL, D_MODEL])
    pc = lambda a, n: jnp.pad(a, ((0, 0), (0, n - a.shape[1])))
    return jnp.concatenate([
        _pad_heads_cols(w_in[:, o[3]:o[4]], H_B, HD_B), w_in[:, o[0]:o[1]], w_in[:, o[1]:o[2]], pc(w_in[:, o[2]:o[3]], HEAD_W),
        _pad_heads_cols(w_in[:, o[4]:o[5]], KV_B, HD_B), _pad_heads_cols(w_in[:, o[5]:o[6]], KV_B, HD_B),
        w_in[:, o[6]:o[7]], w_in[:, o[7]:o[8]]], axis=1)


def _win_unext(we):
    c = HEAD_W
    return jnp.concatenate([
        we[:, ZC_QLAT * c:ZC_CKV * c], we[:, ZC_CKV * c:ZC_KPE * c], we[:, ZC_KPE * c:ZC_KPE * c + QK_ROPE],
        _unpad_heads_cols(we[:, ZC_QB * c:ZC_QLAT * c], H_B, HD_B), _unpad_heads_cols(we[:, ZC_KB * c:ZC_VB * c], KV_B, HD_B),
        _unpad_heads_cols(we[:, ZC_VB * c:ZC_GA * c], KV_B, HD_B), we[:, ZC_GA * c:]], axis=1)


def _wkv_ext(w_kvb):
    wk = w_kvb.reshape(KV_LORA, H_A, QK_NOPE + V_DIM_A)
    k_cols = jnp.pad(wk[:, :, :QK_NOPE], ((0, 0), (0, 0), (0, HEAD_W - QK_NOPE))).reshape(KV_LORA, H_A * HEAD_W)
    v_cols = jnp.pad(wk[:, :, QK_NOPE:], ((0, 0), (0, 0), (0, HEAD_W - V_DIM_A))).reshape(KV_LORA, H_A * HEAD_W)
    eye = jnp.pad(jnp.eye(QK_ROPE, dtype=w_kvb.dtype), ((0, 0), (QK_NOPE, HEAD_W - QK_NOPE - QK_ROPE)))
    pe_rows = jnp.concatenate([jnp.tile(eye, (1, H_A)), jnp.zeros((QK_ROPE, H_A * HEAD_W), w_kvb.dtype)], axis=1)
    top = jnp.concatenate([k_cols, v_cols], axis=1)
    return jnp.concatenate([top, pe_rows, jnp.zeros((2 * HEAD_W - KV_LORA - QK_ROPE, 2 * H_A * HEAD_W), w_kvb.dtype)], axis=0)


def _wkv_unext(we):
    k_cols = we[:KV_LORA, :H_A * HEAD_W].reshape(KV_LORA, H_A, HEAD_W)[:, :, :QK_NOPE]
    v_cols = we[:KV_LORA, H_A * HEAD_W:].reshape(KV_LORA, H_A, HEAD_W)[:, :, :V_DIM_A]
    return jnp.concatenate([k_cols, v_cols], axis=2).reshape(KV_LORA, H_A * (QK_NOPE + V_DIM_A))


def _pad_heads_rows(wm, heads, d):
    n = wm.shape[1]
    return jnp.pad(wm.reshape(heads, d, n), ((0, 0), (0, HEAD_W - d), (0, 0))).reshape(heads * HEAD_W, n)


def _unpad_heads_rows(wm, heads, d):
    n = wm.shape[1]
    return wm.reshape(heads, HEAD_W, n)[:, :d].reshape(heads * d, n)


def _rope_tables(seq):
    def ang(pos, dim):
        inv = np.float32(ROPE_THETA) ** (-np.arange(0, dim, 2, dtype=np.float32) / np.float32(dim))
        return pos.astype(np.float32)[:, None] * inv[None, :]

    def rot(dim):
        r = np.zeros((dim, dim), np.float32)
        half = dim // 2
        r[np.arange(half) + half, np.arange(half)] = -1.0
        r[np.arange(half), np.arange(half) + half] = 1.0
        return r

    def table(blocks):
        cos, sin = np.ones((seq, HEAD_W), np.float32), np.zeros((seq, HEAD_W), np.float32)
        pm = np.zeros((HEAD_W, HEAD_W), np.float32)
        for c0, a in blocks:
            d = 2 * a.shape[1]
            cos[:, c0:c0 + d] = np.concatenate([np.cos(a), np.cos(a)], axis=1)
            sin[:, c0:c0 + d] = np.concatenate([np.sin(a), np.sin(a)], axis=1)
            pm[c0:c0 + d, c0:c0 + d] = rot(d)
        return jnp.asarray(cos), jnp.asarray(sin), jnp.asarray(pm, BF16), jnp.asarray(pm.T, BF16)

    tok = np.arange(seq)
    a1 = ang(tok, QK_ROPE)
    arow, acol = ang(tok // GRID_W, HD_B // 2), ang(tok % GRID_W, HD_B // 2)
    return table([(QK_NOPE, a1)]), table([(0, a1)]), table([(0, arow), (HD_B // 2, acol)])


def _local_step(x, p, tgt, gains, wts, ride=None):
    nb, seq, _ = x.shape
    t = nb * seq
    x0 = x.reshape(t, D_MODEL)
    p2 = p.reshape(t, PLE_DIM)
    tg = tgt.reshape(t, D_MODEL)
    (cq_t, sq_t, pq, pq_t), (ck_t, sk_t, pk, pk_t), (cb_t, sb_t, pb, pb_t) = _rope_tables(seq)
    padg = lambda g: jnp.pad(g, ((0, 0), (0, HEAD_W - g.shape[1])))
    g_qn, g_kn = padg(gains["g_qn"]), padg(gains["g_kn"])

    win = _win_ext(wts["w_in"])
    wqb = _pad_heads_cols(wts["w_qb"], H_A, QK_NOPE + QK_ROPE)
    wkv = _wkv_ext(wts["w_kvb"])

    norm = lambda n: (lambda v, g: (_rms(v, g, n),))
    full = lambda a: (a, a.shape[1], 0, False)

    h = _rowwise("norm_mix", norm(D_MODEL), [full(x0)], [(D_MODEL, BF16, D_MODEL, False)], consts=[gains["g_mix"]])
    z = _mm("in_proj", h, win, tn=2048)
    cq = _rowwise("norm_qa", norm(Q_LORA), [(z, Q_LORA, ZC_QLAT // 2, False)], [(Q_LORA, BF16, Q_LORA, False)],
                  consts=[gains["g_qa"]])
    qa = _mm("q_up", cq, wqb)

    def rope_fwd(scale):
        return lambda v, cos, sin, pm: ((v * cos + _perm(v, pm) * sin) * scale,)

    heads_tile = lambda n: (n * HEAD_W, BF16, n * HEAD_W, False)
    q_a = _rowwise("rope_qa", _per_head(rope_fwd(SCALE_A), H_A, 1, 1), [full(qa)], [heads_tile(H_A)],
                   pos=[cq_t, sq_t], consts=[pq], seq=seq)
    ckv = _rowwise("norm_kva", norm(KV_LORA), [(z, HEAD_W, ZC_CKV, False)], [(HEAD_W, BF16, HEAD_W, False)], consts=[gains["g_kva"]])
    kpe = _rowwise("rope_kpe", rope_fwd(1.0), [(z, HEAD_W, ZC_KPE, False)], [(HEAD_W, BF16, HEAD_W, False)],
                   pos=[ck_t, sk_t], consts=[pk], seq=seq)
    kin = jnp.concatenate([ckv, kpe], axis=1)
    kv_a = _mm("kv_up", kin, wkv, out_dtypes=(BF16,))
    o_a, lse_a, *got_a = _attn_fwd("attn_a_fwd", q_a, kv_a, 0, kv_a, H_A, heads=H_A, group=1, nseq=nb, seq=seq,
                                   rider=ride and ride["gather_a"])

    def prep_fwd(scale):
        def fn(v, cos, sin, g, pm):
            yv = _rms(v, g, HD_B)
            return ((yv * cos + _perm(yv, pm) * sin) * scale,)
        return fn

    z_qb, z_kb = (z, H_B * HEAD_W, ZC_QB // H_B, False), (z, KV_B * HEAD_W, ZC_KB // KV_B, False)
    q_b = _rowwise("prep_qb", _per_head(prep_fwd(SCALE_B), H_B, 1, 1), [z_qb], [heads_tile(H_B)],
                   pos=[cb_t, sb_t], consts=[g_qn, pb], seq=seq)
    k_b = _rowwise("prep_kb", _per_head(prep_fwd(1.0), KV_B, 1, 1), [z_kb], [heads_tile(KV_B)],
                   pos=[cb_t, sb_t], consts=[g_kn, pb], seq=seq)
    o_b, lse_b, *got_b = _attn_fwd("attn_b_fwd", q_b, k_b, 0, z, ZC_VB, heads=H_B, group=H_B // KV_B, nseq=nb, seq=seq,
                                   rider=ride and ride["gather_b"])
    if ride is not None:
        wts = {**wts, **ride["late_weights"](got_a[0], got_b[0])}
    woa = _pad_heads_rows(wts["w_oa"], H_A, V_DIM_A)
    wob = _pad_heads_rows(wts["w_ob"], H_B, HD_B)
    wo, wup, wdown, wpg, wple = wts["w_o"], wts["w_up"], wts["w_down"], wts["w_ple_gate"], wts["w_ple"]
    ya = _mm("out_a", o_a, woa)
    yb = _mm("out_b", o_b, wob)

    z_ga, z_gb = (z, D_MODEL, ZC_GA // 8, False), (z, D_MODEL, ZC_GB // 8, False)
    merged = _rowwise("merge", lambda ga, gb, a, b: (_sigmoid(ga) * a + _sigmoid(gb) * b,),
                      [z_ga, z_gb, full(ya), full(yb)], [(D_MODEL, BF16, D_MODEL, False)])
    def residual_norm(acc, r, g):
        xv = r + acc
        return xv, _rms(xv, g, D_MODEL)

    x1, h2 = _mm("out_proj", merged, wo, out_dtypes=(F32, BF16), epi=residual_norm, extras=(x0,), consts=[gains["g_mlp"]])

    def relu2(acc):
        u = jnp.maximum(acc, 0.0)
        return u, u * u

    u, usq = _mm("mlp_up", h2, wup, out_dtypes=(BF16, BF16), epi=relu2, tn=2048)
    x2, h3 = _mm("mlp_down", usq, wdown, out_dtypes=(F32, BF16), epi=residual_norm, extras=(x1,), consts=[gains["g_ple"]])
    gpre = _mm("ple_gate", h3, wpg)
    pe = _mm("ple_proj", p2, wple)

    def tail(x2v, gp, pev, tv, gf):
        sg = _sigmoid(gp)
        x3 = x2v + sg * pev
        rs = lax.rsqrt(jnp.sum(x3 * x3, axis=-1, keepdims=True) * (1.0 / D_MODEL) + EPS)
        xh = x3 * rs
        err = xh * gf - tv
        dy = err * (1.0 / D_MODEL)
        dyg = dy * gf
        dx3 = rs * (dyg - xh * (jnp.sum(dyg * xh, axis=-1, keepdims=True) * (1.0 / D_MODEL)))
        return (dx3, dx3 * pev * sg * (1.0 - sg), dx3 * sg,
                jnp.sum(err * err, axis=0, keepdims=True) * (0.5 / D_MODEL), jnp.sum(dy * xh, axis=0, keepdims=True))

    dx3, dgpre, dpe, loss_part, dg_final = _rowwise(
        "tail", tail, [full(x2), full(gpre), full(pe), full(tg)],
        [(D_MODEL, F32, D_MODEL, False), (D_MODEL, BF16, D_MODEL, False), (D_MODEL, BF16, D_MODEL, False)],
        consts=[gains["g_final"].reshape(1, D_MODEL)], accs=[(1, D_MODEL), (1, D_MODEL)], tm=256)

    def norm_bwd(n, with_res):
        if with_res:
            def fn(dh, v, res, g):
                dx, dg = _rms_bwd(dh, v, g, n)
                return dx + res, dg
        else:
            def fn(dh, v, g):
                return _rms_bwd(dh, v, g, n)
        return fn

    dw = {}
    dw["w_ple"] = _mm_tn("dw_ple", p2, dpe)
    dw["w_ple_gate"] = _mm_tn("dw_ple_gate", h3, dgpre)
    norm_res_bwd = norm_bwd(D_MODEL, True)
    dx2, dg_ple = _mm("d_ple_gate", dgpre, wpg, trans_b=True, epi=norm_res_bwd, extras=(x2, dx3), consts=[gains["g_ple"]],
                      accs=[(1, D_MODEL)], tm=256)
    dw["w_down"] = _mm_tn("dw_down", usq, dx2)
    dupre = _mm("d_mlp_down", dx2, wdown, trans_b=True, out_dtypes=(BF16,), epi=lambda acc, uv: (acc * (2.0 * uv.astype(F32)),),
                extras=(u,), tn=2048)
    dw["w_up"] = _mm_tn("dw_up", h2, dupre)
    dx1, dg_mlp = _mm("d_mlp_up", dupre, wup, trans_b=True, epi=norm_res_bwd, extras=(x1, dx2), consts=[gains["g_mlp"]],
                      accs=[(1, D_MODEL)], tm=256)
    dw["w_o"] = _mm_tn("dw_o", merged, dx1)

    def merge_bwd(dm, ga, gb, a, b):
        sa, sb = _sigmoid(ga), _sigmoid(gb)
        return dm * sa, dm * sb, dm * a * sa * (1.0 - sa), dm * b * sb * (1.0 - sb)

    dya, dyb, dga, dgb = _mm("d_out_proj", dx1, wo, trans_b=True, out_dtypes=(BF16,) * 4, epi=merge_bwd,
                             extras=((z, ZC_GA // 8), (z, ZC_GB // 8), ya, yb), tm=256)
    dw["w_oa"] = _unpad_heads_rows(_mm_tn("dw_oa", o_a, dya), H_A, V_DIM_A)
    dw["w_ob"] = _unpad_heads_rows(_mm_tn("dw_ob", o_b, dyb), H_B, HD_B)
    do_a = _mm("d_out_a", dya, woa, trans_b=True, out_dtypes=(BF16,))
    do_b = _mm("d_out_b", dyb, wob, trans_b=True, out_dtypes=(BF16,))
    res_a = _attn_bwd("attn_a_bwd", q_a, kv_a, 0, kv_a, H_A, o_a, do_a, lse_a, heads=H_A, group=1, nseq=nb, seq=seq,
                      rider=ride and ride["scatter_late"](dw))
    dq_a, dk_a, dv_a = res_a[:3]
    if ride is not None:
        ride["out"]["parts_late"] = res_a[3]

    def rope_bwd(scale):
        return lambda d, cos, sin, pm_t: ((d * cos + _perm(d * sin, pm_t)) * scale,)

    dqa = _rowwise("rope_qa_bwd", _per_head(rope_bwd(SCALE_A), H_A, 1, 1), [full(dq_a)], [heads_tile(H_A)],
                   pos=[cq_t, sq_t], consts=[pq_t], seq=seq)
    dw["w_qb"] = _unpad_heads_cols(_mm_tn("dw_qb", cq, dqa), H_A, QK_NOPE + QK_ROPE)
    dcq = _mm("d_q_up", dqa, wqb, trans_b=True)
    dq_lat, dg_qa = _rowwise("norm_qa_bwd", norm_bwd(Q_LORA, False), [full(dcq), (z, Q_LORA, ZC_QLAT // 2, False)],
                             [(Q_LORA, BF16, Q_LORA, False)], consts=[gains["g_qa"]], accs=[(1, Q_LORA)])
    dkv_a = jnp.concatenate([dk_a, dv_a], axis=1)
    dw["w_kvb"] = _wkv_unext(_mm_tn("dw_kv", kin, dkv_a))
    dq_b, dk_b, dv_b, *parts_small = _attn_bwd("attn_b_bwd", q_b, k_b, 0, z, ZC_VB, o_b, do_b, lse_b, heads=H_B, group=H_B // KV_B,
                                               nseq=nb, seq=seq, rider=ride and ride["scatter_small"](dw))
    if ride is not None:
        ride["out"]["parts_small"] = parts_small[0]
    dkin = _mm("d_kv_up", dkv_a, wkv, trans_b=True)
    dckv, dg_kva = _rowwise("norm_kva_bwd", norm_bwd(KV_LORA, False), [(dkin, HEAD_W, 0, False), (z, HEAD_W, ZC_CKV, False)],
                            [(HEAD_W, BF16, HEAD_W, False)], consts=[gains["g_kva"]], accs=[(1, KV_LORA)])
    dkpe = _rowwise("rope_kpe_bwd", rope_bwd(1.0), [(dkin, HEAD_W, 1, False)], [(HEAD_W, BF16, HEAD_W, False)],
                    pos=[ck_t, sk_t], consts=[pk_t], seq=seq)

    def prep_bwd(scale):
        def fn(d, v, cos, sin, g, pm_t):
            dyv = (d * cos + _perm(d * sin, pm_t)) * scale
            return _rms_bwd(dyv, v, g, HD_B)
        return fn

    dqb, dg_qn = _rowwise("prep_qb_bwd", _per_head(prep_bwd(SCALE_B), H_B, 2, 1), [full(dq_b), z_qb], [heads_tile(H_B)],
                          pos=[cb_t, sb_t], consts=[g_qn, pb_t], accs=[(1, HEAD_W)], seq=seq)
    dkb, dg_kn = _rowwise("prep_kb_bwd", _per_head(prep_bwd(1.0), KV_B, 2, 1), [full(dk_b), z_kb], [heads_tile(KV_B)],
                          pos=[cb_t, sb_t], consts=[g_kn, pb_t], accs=[(1, HEAD_W)], seq=seq)

    dz = jnp.concatenate([dqb, dq_lat, dckv, dkpe, dkb, dv_b, dga, dgb], axis=1)
    dw["w_in"] = _win_unext(_mm_tn("dw_in", h, dz))
    dx0, dg_mix, *parts_in = _mm("d_in_proj", dz, win, trans_b=True, epi=norm_res_bwd, extras=(x0, dx1), consts=[gains["g_mix"]],
                                 accs=[(1, D_MODEL)], tm=256, rider=ride and ride["scatter_in"](dw))
    if ride is not None:
        ride["out"]["parts_in"] = parts_in[0]

    dg = {"g_mix": dg_mix, "g_qa": dg_qa, "g_kva": dg_kva, "g_qn": dg_qn[:, :HD_B], "g_kn": dg_kn[:, :HD_B],
          "g_mlp": dg_mlp, "g_ple": dg_ple, "g_final": dg_final}
    return loss_part, dx0.reshape(nb, seq, D_MODEL), dg, dw


def _pack_small(vals, loss_part=None):
    flat = jnp.concatenate([vals[n].reshape(1, -1) for n, _ in SMALL], axis=1)
    loss = jnp.zeros((1, 8 * 128), F32) if loss_part is None else loss_part
    gap = jnp.zeros((1, LOSS_ROW0 * 128 - SMALL_N), F32)
    return jnp.concatenate([flat, gap, loss], axis=1).reshape(SMALL_ROWS, 128)


def _unpack_small(slab, like):
    flat, out, off = slab.reshape(-1), {}, 0
    for n, k in SMALL:
        out[n] = flat[off:off + k].reshape(like[n].shape)
        off += k
    return out


def kernel(x, p, g_mix, w_in, g_qa, w_qb, g_kva, w_kvb, g_qn, g_kn, w_oa, w_ob, w_o, g_mlp, w_up, w_down, g_ple, w_ple_gate, w_ple, g_final, loss_target, m_g_mix, m_w_in, m_g_qa, m_w_qb, m_g_kva, m_w_kvb, m_g_qn, m_g_kn, m_w_oa, m_w_ob, m_w_o, m_g_mlp, m_w_up, m_w_down, m_g_ple, m_w_ple_gate, m_w_ple, m_g_final, v_g_mix, v_w_in, v_g_qa, v_w_qb, v_g_kva, v_w_kvb, v_g_qn, v_g_kn, v_w_oa, v_w_ob, v_w_o, v_g_mlp, v_w_up, v_w_down, v_g_ple, v_w_ple_gate, v_w_ple, v_g_final):
    given = dict(locals())
    order = ["g_mix", "w_in", "g_qa", "w_qb", "g_kva", "w_kvb", "g_qn", "g_kn", "w_oa", "w_ob", "w_o", "g_mlp", "w_up",
             "w_down", "g_ple", "w_ple_gate", "w_ple", "g_final"]
    big_names = [n for n, _, _, _ in BIG]
    local = lambda prefix, names: [given[prefix + n][0] for n in names]
    late = LATE_A + LATE_B

    shards_cols = lambda a: a.reshape(a.shape[0], 4, a.shape[1] // 4).transpose(1, 0, 2)
    got_in, got_small = _exchange_alone("weight_gather_early", [
        _Exchange("gather", w_in[0].astype(BF16)), _Exchange("gather", _pack_shards(EARLY_SMALL, local("", EARLY_SMALL), BF16))])
    wts = {"w_in": got_in.transpose(1, 0, 2).reshape(got_in.shape[1], -1), **dict(zip(EARLY_SMALL, _unpack_full(EARLY_SMALL, got_small)))}
    gains = {n: given[n].reshape(1, -1) for n, _ in SMALL}
    ride = {
        "gather_a": _Exchange("gather", _pack_shards(LATE_A, local("", LATE_A), BF16)),
        "gather_b": _Exchange("gather", _pack_shards(LATE_B, local("", LATE_B), BF16)),
        "late_weights": lambda ga, gb: {**dict(zip(LATE_A, _unpack_full(LATE_A, ga))), **dict(zip(LATE_B, _unpack_full(LATE_B, gb)))},
        "scatter_late": lambda dw: _Exchange("scatter", _pack_full(late, [dw[n] for n in late], BF16)),
        "scatter_small": lambda dw: _Exchange("scatter", _pack_full(EARLY_SMALL, [dw[n] for n in EARLY_SMALL], BF16)),
        "scatter_in": lambda dw: _Exchange("scatter", shards_cols(dw["w_in"].astype(BF16))),
        "out": {},
    }
    loss_part, grad_x, dg, dw = _local_step(x, p[0], loss_target, gains, wts, ride)

    small = lambda prefix: _pack_small({n: given[prefix + n] for n, _ in SMALL})
    g_s, d_s, m_s, v_s, loss = _small_allreduce_adamw(_pack_small(dg, loss_part), small(""), small("m_"), small("v_"))

    grads = dict(zip(late, _unpack_shards(late, _reduce_pair("grad_reduce_late", ride["out"]["parts_late"]))))
    grads.update(zip(EARLY_SMALL, _unpack_shards(EARLY_SMALL, _reduce_pair("grad_reduce_small", ride["out"]["parts_small"]))))
    grads["w_in"] = _reduce_pair("grad_reduce_in", ride["out"]["parts_in"])

    res = {}
    for key, slab in (("grad_", g_s), ("delta_", d_s), ("new_m_", m_s), ("new_v_", v_s)):
        for n, val in _unpack_small(slab, given).items():
            res[key + n] = val
    for n in big_names:
        d_w, m_w, v_w = _adamw_shard("adamw_" + n, grads[n], given[n][0], given["m_" + n][0], given["v_" + n][0])
        res["grad_" + n], res["delta_" + n], res["new_m_" + n], res["new_v_" + n] = grads[n][None], d_w[None], m_w[None], v_w[None]
    outs = [loss.reshape(()), grad_x]
    for key in ("grad_", "delta_", "new_m_", "new_v_"):
        outs += [res[key + n] for n in order]
    return tuple(outs)
```

```python
import functools

import numpy as np
import jax
import jax.numpy as jnp
from jax import lax
from jax.experimental import pallas as pl
from jax.experimental.pallas import tpu as pltpu

F32 = jnp.float32
BF16 = jnp.bfloat16
MESH = pl.DeviceIdType.MESH

D_MODEL = 1024
GRID_W = 64
ROPE_THETA = 10000.0
EPS = 1e-6
H_A, QK_NOPE, QK_ROPE, V_DIM_A, Q_LORA, KV_LORA = 8, 64, 32, 64, 256, 128
H_B, KV_B, HD_B = 8, 2, 64
D_FF = 4096
PLE_DIM = 256
HEAD_W = 128
SCALE_A = (QK_NOPE + QK_ROPE) ** -0.5
SCALE_B = HD_B ** -0.5

ADAM_LR, ADAM_B1, ADAM_B2, ADAM_EPS, ADAM_WD, ADAM_STEP = 0.001, 0.9, 0.999, 1e-08, 0.01, 10
M_HAT_DIV = 1.0 - ADAM_B1 ** ADAM_STEP
V_HAT_DIV = 1.0 - ADAM_B2 ** ADAM_STEP

VMEM_LIMIT_BYTES = 56 * 1024 * 1024

ZC_QB, ZC_QLAT, ZC_CKV, ZC_KPE, ZC_KB, ZC_VB, ZC_GA, ZC_GB = 0, 8, 10, 11, 12, 14, 16, 24
Z_WIDTH = 32 * HEAD_W

BIG = [
    ("w_in", 1024, 3232, 1), ("w_qb", 256, 768, 1), ("w_kvb", 128, 1024, 1), ("w_oa", 512, 1024, 1),
    ("w_ob", 512, 1024, 1), ("w_o", 1024, 1024, 0), ("w_up", 1024, 4096, 1), ("w_down", 4096, 1024, 0),
    ("w_ple_gate", 1024, 1024, 0), ("w_ple", 256, 1024, 1),
]
BIG_BY_NAME = {e[0]: e for e in BIG}
PACK_W = 1024
PACK_ALIGN = 64
EARLY_SMALL = ["w_qb", "w_kvb"]
LATE_A = ["w_oa", "w_ob", "w_o", "w_up"]
LATE_B = ["w_down", "w_ple_gate", "w_ple"]

SMALL = [("g_mix", 1024), ("g_qa", 256), ("g_kva", 128), ("g_qn", 64), ("g_kn", 64), ("g_mlp", 1024),
         ("g_ple", 1024), ("g_final", 1024)]
SMALL_N = sum(n for _, n in SMALL)
LOSS_ROW0 = 40
SMALL_ROWS = 48


def _params(sem):
    return pltpu.CompilerParams(dimension_semantics=sem, vmem_limit_bytes=VMEM_LIMIT_BYTES)


def _sigmoid(v):
    return 1.0 / (1.0 + jnp.exp(-v))


def _perm(v, p_ref):
    pm = p_ref[...]
    hi = v.astype(BF16)
    lo = (v - hi.astype(F32)).astype(BF16)
    return (jnp.dot(hi, pm, preferred_element_type=F32) + jnp.dot(lo, pm, preferred_element_type=F32))


def _rms(v, g, n):
    rs = lax.rsqrt(jnp.sum(v * v, axis=-1, keepdims=True) * (1.0 / n) + EPS)
    return v * rs * g


def _rms_bwd(dy, v, g, n):
    rs = lax.rsqrt(jnp.sum(v * v, axis=-1, keepdims=True) * (1.0 / n) + EPS)
    vh = v * rs
    dyg = dy * g
    dx = rs * (dyg - vh * (jnp.sum(dyg * vh, axis=-1, keepdims=True) * (1.0 / n)))
    return dx, jnp.sum(dy * vh, axis=0, keepdims=True)


def _ride(body, grid, rider):
    if rider is None:
        return body, [], [], [], []
    n_sem = len(rider.scratch)

    def wrapped(*refs):
        ids = [pl.program_id(a) for a in range(len(grid))]
        n_in = len(refs) - n_sem - 2 - rider.n_core_out - rider.n_core_scratch
        core_in, src = refs[:n_in], refs[n_in]
        core_out = refs[n_in + 1:n_in + 1 + rider.n_core_out]
        dst = refs[n_in + 1 + rider.n_core_out]
        core_scr = refs[n_in + 2 + rider.n_core_out:len(refs) - n_sem]
        sems = refs[len(refs) - n_sem:]

        @pl.when(functools.reduce(jnp.logical_and, [a == 0 for a in ids]))
        def _():
            rider.start(src, dst, *sems)

        body(*core_in, *core_out, *core_scr)

        @pl.when(functools.reduce(jnp.logical_and, [a == n - 1 for a, n in zip(ids, grid)]))
        def _():
            rider.finish(src, dst, *sems)

    hbm = pl.BlockSpec(memory_space=pl.ANY)
    return wrapped, [rider.src], [hbm], [rider.out_shape], list(rider.scratch)


def _mm(name, a, b, *, trans_b=False, out_dtypes=(F32,), epi=None, extras=(), consts=(), accs=(), tm=512, tn=None, rider=None):
    m, k = a.shape
    n = b.shape[0] if trans_b else b.shape[1]
    tn = n if tn is None else min(tn, n)
    tm = min(tm, m)
    assert m % tm == 0 and n % tn == 0 and (b.shape[1] if trans_b else b.shape[0]) == k
    extras = [e if isinstance(e, tuple) else (e, 0) for e in extras]
    n_ex, n_c, n_out, n_acc = len(extras), len(consts), len(out_dtypes), len(accs)
    dims = (((1,), (1,)), ((), ())) if trans_b else (((1,), (0,)), ((), ()))

    def body(a_ref, b_ref, *rest):
        acc = lax.dot_general(a_ref[...].astype(BF16), b_ref[...].astype(BF16), dims, preferred_element_type=F32)
        res = (acc,) if epi is None else epi(acc, *[e[...] for e in rest[:n_ex + n_c]])
        o_refs = rest[n_ex + n_c:]
        for o_ref, r in zip(o_refs[:n_out], res[:n_out]):
            o_ref[...] = r.astype(o_ref.dtype)
        if n_acc:
            first = jnp.logical_and(pl.program_id(0) == 0, pl.program_id(1) == 0)

            @pl.when(first)
            def _():
                for o_ref, r in zip(o_refs[n_out:], res[n_out:]):
                    o_ref[...] = r

            @pl.when(jnp.logical_not(first))
            def _():
                for o_ref, r in zip(o_refs[n_out:], res[n_out:]):
                    o_ref[...] += r

    grid = (n // tn, m // tm)
    if rider is not None:
        rider.n_core_out, rider.n_core_scratch = n_out + n_acc, 0
    body, x_in, x_spec, x_out, x_scr = _ride(body, grid, rider)
    a_spec = pl.BlockSpec((tm, k), lambda j, i: (i, 0))
    b_spec = pl.BlockSpec((tn, k), lambda j, i: (j, 0)) if trans_b else pl.BlockSpec((k, tn), lambda j, i: (0, j))
    t_spec = pl.BlockSpec((tm, tn), lambda j, i: (i, j))
    e_specs = [pl.BlockSpec((tm, tn), lambda j, i, off=off: (i, j + off)) for _, off in extras]
    c_specs = [pl.BlockSpec(c.shape, lambda j, i: (0, 0)) for c in consts]
    acc_specs = [pl.BlockSpec(sh, lambda j, i: (0, 0)) for sh in accs]
    sem = ("parallel", "parallel") if rider is None and not n_acc else ("arbitrary", "arbitrary")
    outs = pl.pallas_call(
        body, out_shape=[jax.ShapeDtypeStruct((m, n), d) for d in out_dtypes] + [jax.ShapeDtypeStruct(sh, F32) for sh in accs] + x_out,
        grid=grid, in_specs=[a_spec, b_spec] + e_specs + c_specs + x_spec, out_specs=[t_spec] * n_out + acc_specs + x_spec,
        scratch_shapes=x_scr, compiler_params=_params(sem), name=name)(a, b, *[e for e, _ in extras], *consts, *x_in)
    return outs[0] if len(outs) == 1 else outs


def _per_head(fn, heads, n_tiled, n_out):
    def run(*args):
        res = [fn(*[a[:, hd * HEAD_W:(hd + 1) * HEAD_W] for a in args[:n_tiled]], *args[n_tiled:]) for hd in range(heads)]
        tiles = [jnp.concatenate([r[k] for r in res], axis=1) for k in range(n_out)]
        sums = [functools.reduce(lambda u, v: u + v, [r[k] for r in res]) for k in range(n_out, len(res[0]))]
        return (*tiles, *sums)
    return run


def _mm_tn(name, a, b, *, out_dtype=BF16, tk=1024, tn=1024, tt=1024):
    t, k = a.shape
    n = b.shape[1]
    tk, tn, tt = min(tk, k), min(tn, n), min(tt, t)
    assert b.shape[0] == t and k % tk == 0 and n % tn == 0 and t % tt == 0
    nt = t // tt

    def body(a_ref, b_ref, o_ref, acc):
        part = lax.dot_general(a_ref[...].astype(BF16), b_ref[...].astype(BF16), (((0,), (0,)), ((), ())),
                               preferred_element_type=F32)

        @pl.when(pl.program_id(2) == 0)
        def _():
            acc[...] = part

        @pl.when(pl.program_id(2) != 0)
        def _():
            acc[...] += part

        @pl.when(pl.program_id(2) == nt - 1)
        def _():
            o_ref[...] = acc[...].astype(o_ref.dtype)

    return pl.pallas_call(
        body, out_shape=jax.ShapeDtypeStruct((k, n), out_dtype), grid=(k // tk, n // tn, nt),
        in_specs=[pl.BlockSpec((tt, tk), lambda ki, ni, ti: (ti, ki)), pl.BlockSpec((tt, tn), lambda ki, ni, ti: (ti, ni))],
        out_specs=pl.BlockSpec((tk, tn), lambda ki, ni, ti: (ki, ni)), scratch_shapes=[pltpu.VMEM((tk, tn), F32)],
        compiler_params=_params(("parallel", "parallel", "arbitrary")), name=name)(a, b)


def _rowwise(name, fn, ins, outs, *, consts=(), pos=(), accs=(), heads=1, tm=512, seq=None):
    t = ins[0][0].shape[0]
    tm = min(tm, t if seq is None else seq)
    assert t % tm == 0 and (seq is None or seq % tm == 0)
    n_in, n_pos, n_c, n_out, n_acc = len(ins), len(pos), len(consts), len(outs), len(accs)

    def body(*refs):
        vals = [r[...] for r in refs[:n_in + n_pos + n_c]]
        res = fn(*vals)
        o_refs = refs[n_in + n_pos + n_c:]
        for o_ref, r in zip(o_refs[:n_out], res[:n_out]):
            o_ref[...] = r.astype(o_ref.dtype)
        if n_acc:
            first = jnp.logical_and(pl.program_id(0) == 0, pl.program_id(1) == 0)

            @pl.when(first)
            def _():
                for o_ref, r in zip(o_refs[n_out:], res[n_out:]):
                    o_ref[...] = r

            @pl.when(jnp.logical_not(first))
            def _():
                for o_ref, r in zip(o_refs[n_out:], res[n_out:]):
                    o_ref[...] += r

    def tiled(width, c0, per_head):
        return pl.BlockSpec((tm, width), (lambda h, i: (i, c0 + h)) if per_head else (lambda h, i: (i, c0)))

    in_specs = [tiled(w, c0, ph) for _, w, c0, ph in ins]
    if n_pos:
        nblk = seq // tm
        in_specs += [pl.BlockSpec((tm, a.shape[1]), lambda h, i: (i % nblk, 0)) for a in pos]
    in_specs += [pl.BlockSpec(a.shape, lambda h, i: (0, 0)) for a in consts]
    out_specs = [tiled(w, 0, ph) for _, _, w, ph in outs] + [pl.BlockSpec(s, lambda h, i: (0, 0)) for s in accs]
    out_shape = [jax.ShapeDtypeStruct((t, c), d) for c, d, _, _ in outs] + [jax.ShapeDtypeStruct(s, F32) for s in accs]
    sem = ("arbitrary", "arbitrary") if n_acc else ("parallel", "parallel")
    res = pl.pallas_call(body, out_shape=out_shape, grid=(heads, t // tm), in_specs=in_specs, out_specs=out_specs,
                         compiler_params=_params(sem), name=name)(*[a for a, _, _, _ in ins], *pos, *consts)
    return res[0] if len(res) == 1 else res


ATTN_HEADS_PER_STEP = 2


def _attn_fwd(name, q, k, kc0, v, vc0, *, heads, group, nseq, seq, tq=512, rider=None):
    tq = min(tq, seq)
    nq = seq // tq
    hp = ATTN_HEADS_PER_STEP
    grid = (heads // hp, nseq, nq)
    shared = group > 1
    assert group % hp == 0 if shared else (kc0 % hp == 0 and vc0 % hp == 0)

    def body(q_ref, k_ref, v_ref, o_ref, lse_ref):
        for j in range(hp):
            cols = slice(j * HEAD_W, (j + 1) * HEAD_W)
            kj = (k_ref[...] if shared else k_ref[:, cols]).astype(BF16)
            vj = (v_ref[...] if shared else v_ref[:, cols]).astype(BF16)
            s = lax.dot_general(q_ref[:, cols], kj, (((1,), (1,)), ((), ())), preferred_element_type=F32)
            m = jnp.max(s, axis=-1, keepdims=True)
            p = jnp.exp(s - m)
            l = jnp.sum(p, axis=-1, keepdims=True)
            o = jnp.dot(p.astype(BF16), vj, preferred_element_type=F32)
            o_ref[:, cols] = (o * (1.0 / l)).astype(o_ref.dtype)
            lse_ref[j] = m + jnp.log(l)

    if rider is not None:
        rider.n_core_out, rider.n_core_scratch = 2, 0
    body, x_in, x_spec, x_out, x_scr = _ride(body, grid, rider)
    q_spec = pl.BlockSpec((tq, hp * HEAD_W), lambda h, b, i: (b * nq + i, h))
    if shared:
        k_spec = pl.BlockSpec((seq, HEAD_W), lambda h, b, i: (b, kc0 + (h * hp) // group))
        v_spec = pl.BlockSpec((seq, HEAD_W), lambda h, b, i: (b, vc0 + (h * hp) // group))
    else:
        k_spec = pl.BlockSpec((seq, hp * HEAD_W), lambda h, b, i: (b, kc0 // hp + h))
        v_spec = pl.BlockSpec((seq, hp * HEAD_W), lambda h, b, i: (b, vc0 // hp + h))
    lse_spec = pl.BlockSpec((hp, tq, 1), lambda h, b, i: (h, b * nq + i, 0))
    sem = ("parallel",) * 3 if rider is None else ("arbitrary",) * 3
    return pl.pallas_call(
        body, out_shape=[jax.ShapeDtypeStruct(q.shape, BF16), jax.ShapeDtypeStruct((heads, q.shape[0], 1), F32)] + x_out,
        grid=grid, in_specs=[q_spec, k_spec, v_spec] + x_spec, out_specs=[q_spec, lse_spec] + x_spec, scratch_shapes=x_scr,
        compiler_params=_params(sem), name=name)(q, k, v, *x_in)


def _attn_bwd(name, q, k, kc0, v, vc0, o, do, lse, *, heads, group, nseq, seq, tq=512, rider=None):
    tq = min(tq, seq)
    nq = seq // tq
    hk = heads // group
    t = q.shape[0]
    grid = (hk, nseq, group, nq)

    def body(q_ref, k_ref, v_ref, o_ref, do_ref, lse_ref, dq_ref, dk_ref, dv_ref, dk_acc, dv_acc):
        g, i = pl.program_id(2), pl.program_id(3)
        qv, kv, vv, dov = q_ref[...], k_ref[...].astype(BF16), v_ref[...].astype(BF16), do_ref[...]
        s = lax.dot_general(qv, kv, (((1,), (1,)), ((), ())), preferred_element_type=F32)
        pn = jnp.exp(s - lse_ref[...])
        dp = lax.dot_general(dov, vv, (((1,), (1,)), ((), ())), preferred_element_type=F32)
        delta = jnp.sum(dov.astype(F32) * o_ref[...].astype(F32), axis=-1, keepdims=True)
        ds = (pn * (dp - delta)).astype(BF16)
        dq_ref[...] = jnp.dot(ds, kv, preferred_element_type=F32)
        dk_part = lax.dot_general(ds, qv, (((0,), (0,)), ((), ())), preferred_element_type=F32)
        dv_part = lax.dot_general(pn.astype(BF16), dov, (((0,), (0,)), ((), ())), preferred_element_type=F32)
        first = jnp.logical_and(g == 0, i == 0)

        @pl.when(first)
        def _():
            dk_acc[...] = dk_part
            dv_acc[...] = dv_part

        @pl.when(jnp.logical_not(first))
        def _():
            dk_acc[...] += dk_part
            dv_acc[...] += dv_part

        @pl.when(jnp.logical_and(g == group - 1, i == nq - 1))
        def _():
            dk_ref[...] = dk_acc[...].astype(dk_ref.dtype)
            dv_ref[...] = dv_acc[...].astype(dv_ref.dtype)

    if rider is not None:
        rider.n_core_out, rider.n_core_scratch = 3, 2
    body, x_in, x_spec, x_out, x_scr = _ride(body, grid, rider)
    q_spec = pl.BlockSpec((tq, HEAD_W), lambda kh, b, g, i: (b * nq + i, kh * group + g))
    kv_out = pl.BlockSpec((seq, HEAD_W), lambda kh, b, g, i: (b, kh))
    lse_spec = pl.BlockSpec((None, tq, 1), lambda kh, b, g, i: (kh * group + g, b * nq + i, 0))
    sem = ("parallel", "parallel", "arbitrary", "arbitrary") if rider is None else ("arbitrary",) * 4
    return pl.pallas_call(
        body,
        out_shape=[jax.ShapeDtypeStruct(q.shape, F32), jax.ShapeDtypeStruct((t, hk * HEAD_W), BF16),
                   jax.ShapeDtypeStruct((t, hk * HEAD_W), BF16)] + x_out,
        grid=grid,
        in_specs=[q_spec, pl.BlockSpec((seq, HEAD_W), lambda kh, b, g, i: (b, kc0 + kh)),
                  pl.BlockSpec((seq, HEAD_W), lambda kh, b, g, i: (b, vc0 + kh)), q_spec, q_spec, lse_spec] + x_spec,
        out_specs=[q_spec, kv_out, kv_out] + x_spec,
        scratch_shapes=[pltpu.VMEM((seq, HEAD_W), F32), pltpu.VMEM((seq, HEAD_W), F32)] + x_scr,
        compiler_params=_params(sem), name=name)(q, k, v, o, do, lse, *x_in)


def _place():
    return lax.axis_index("x"), lax.axis_index("y"), lax.axis_index("c")


def _other_chips(x, y):
    return [(1 - x, y), (x, 1 - y), (1 - x, 1 - y)]


class _Exchange:
    def __init__(self, kind, src):
        assert kind in ("gather", "scatter")
        self.kind, self.src = kind, src
        rows, w = src.shape[-2:]
        self.out_shape = jax.ShapeDtypeStruct((4, rows, w), src.dtype)
        self.scratch = [pltpu.SemaphoreType.DMA((3,)), pltpu.SemaphoreType.DMA((3,)), pltpu.SemaphoreType.DMA(())]
        self.n_core_out = self.n_core_scratch = 0

    def _copies(self, src_ref, out_ref, send_sems, recv_sems, landing):
        x, y, c = _place()

        def remote(k, s, d, to):
            return pltpu.make_async_remote_copy(src_ref=s, dst_ref=d, send_sem=send_sems.at[k], recv_sem=recv_sems.at[k],
                                                device_id=to, device_id_type=MESH)

        me = 2 * x + y
        part = (lambda j: src_ref) if self.kind == "gather" else (lambda j: src_ref.at[j])
        if landing:
            return [remote(k, part(me), out_ref.at[2 * px + py], (px, py, c)) for k, (px, py) in enumerate(_other_chips(x, y))]
        return [remote(k, part(2 * px + py), out_ref.at[me], (px, py, c)) for k, (px, py) in enumerate(_other_chips(x, y))]

    def _local(self, src_ref, out_ref, local_sem):
        x, y, _ = _place()
        me = 2 * x + y
        return pltpu.make_async_copy(src_ref if self.kind == "gather" else src_ref.at[me], out_ref.at[me], local_sem)

    def start(self, src_ref, out_ref, send_sems, recv_sems, local_sem):
        self._local(src_ref, out_ref, local_sem).start()
        for mine in self._copies(src_ref, out_ref, send_sems, recv_sems, False):
            mine.start()

    def finish(self, src_ref, out_ref, send_sems, recv_sems, local_sem):
        for landed in self._copies(src_ref, out_ref, send_sems, recv_sems, True):
            landed.wait_recv()
        for mine in self._copies(src_ref, out_ref, send_sems, recv_sems, False):
            mine.wait_send()
        self._local(src_ref, out_ref, local_sem).wait()


def _exchange_alone(name, exchanges):
    n = len(exchanges)

    def body(*refs):
        args = [(refs[j], refs[n + j], *refs[2 * n + 3 * j:2 * n + 3 * j + 3]) for j in range(n)]
        for ex, a in zip(exchanges, args):
            ex.start(*a)
        for ex, a in zip(exchanges, args):
            ex.finish(*a)

    hbm = pl.BlockSpec(memory_space=pl.ANY)
    return pl.pallas_call(body, out_shape=[ex.out_shape for ex in exchanges], in_specs=[hbm] * n, out_specs=[hbm] * n,
                          scratch_shapes=[s for ex in exchanges for s in ex.scratch], name=name)(*[ex.src for ex in exchanges])


def _adamw(w, g, m, v):
    m = ADAM_B1 * m + (1.0 - ADAM_B1) * g
    v = ADAM_B2 * v + (1.0 - ADAM_B2) * (g * g)
    delta = -ADAM_LR * ((m / M_HAT_DIV) / (jnp.sqrt(v / V_HAT_DIV) + ADAM_EPS) + ADAM_WD * w)
    return delta, m, v


def _small_allreduce_adamw(part, w, m, v):
    def body(part_ref, w_ref, m_ref, v_ref, g_out, d_out, m_out, v_out, loss_out, buf, send_sems, recv_sems):
        x, y, c = _place()
        me = 4 * x + 2 * y + c
        buf[me] = part_ref[...]

        def flip(k):
            fx, fy, fc = (k >> 2) & 1, (k >> 1) & 1, k & 1
            px, py, pc = (1 - x if fx else x), (1 - y if fy else y), (1 - c if fc else c)
            return (px, py, pc), 4 * px + 2 * py + pc

        def copy(k, slot):
            return pltpu.make_async_remote_copy(
                src_ref=part_ref, dst_ref=buf.at[slot], send_sem=send_sems.at[k - 1], recv_sem=recv_sems.at[k - 1],
                device_id=flip(k)[0], device_id_type=MESH)

        sent = [copy(k, me) for k in range(1, 8)]
        for cp in sent:
            cp.start()
        for k in range(1, 8):
            copy(k, flip(k)[1]).wait_recv()
        for cp in sent:
            cp.wait_send()
        tot = buf[0]
        for j in range(1, 8):
            tot = tot + buf[j]
        delta, m_new, v_new = _adamw(w_ref[...], tot, m_ref[...], v_ref[...])
        g_out[...] = tot
        d_out[...] = delta
        m_out[...] = m_new
        v_out[...] = v_new
        loss_out[...] = jnp.sum(tot[LOSS_ROW0:LOSS_ROW0 + 8, :]).reshape(1, 1)

    vm = pl.BlockSpec(memory_space=pltpu.VMEM)
    shp = jax.ShapeDtypeStruct((SMALL_ROWS, 128), F32)
    return pl.pallas_call(
        body, out_shape=[shp, shp, shp, shp, jax.ShapeDtypeStruct((1, 1), F32)],
        in_specs=[vm, vm, vm, vm], out_specs=[vm, vm, vm, vm, vm],
        scratch_shapes=[pltpu.VMEM((8, SMALL_ROWS, 128), F32), pltpu.SemaphoreType.DMA((7,)), pltpu.SemaphoreType.DMA((7,))],
        name="small_allreduce_adamw")(part, w, m, v)


def _row_tile(rows, cap):
    return max(t for t in range(16, min(rows, cap) + 1, 16) if rows % t == 0)


def _reduce_pair(name, parts):
    _, rows, w = parts.shape
    tr = _row_tile(rows, 576)
    nt = rows // tr

    def body(p_ref, o_ref, mine, theirs, send_sems, recv_sems):
        i = pl.program_id(0)
        x, y, c = _place()

        def copy(t):
            rows_t = pl.ds(pl.multiple_of(t * tr, tr), tr)
            return pltpu.make_async_remote_copy(src_ref=mine.at[rows_t], dst_ref=theirs.at[rows_t], send_sem=send_sems.at[t],
                                                recv_sem=recv_sems.at[t], device_id=(x, y, 1 - c), device_id_type=MESH)

        @pl.when(i < nt)
        def _():
            mine[pl.ds(pl.multiple_of(i * tr, tr), tr), :] = (
                (p_ref[0].astype(F32) + p_ref[1].astype(F32)) + p_ref[2].astype(F32)) + p_ref[3].astype(F32)
            copy(i).start()

        @pl.when(i >= nt)
        def _():
            t = i - nt
            copy(t).wait()
            rows_t = pl.ds(pl.multiple_of(t * tr, tr), tr)
            o_ref[...] = mine[rows_t, :] + theirs[rows_t, :]

    return pl.pallas_call(
        body, out_shape=jax.ShapeDtypeStruct((rows, w), F32), grid=(2 * nt,),
        in_specs=[pl.BlockSpec((4, tr, w), lambda i: (0, jnp.minimum(i, nt - 1), 0))],
        out_specs=pl.BlockSpec((tr, w), lambda i: (jnp.maximum(i - nt, 0), 0)),
        scratch_shapes=[pltpu.VMEM((rows, w), F32), pltpu.VMEM((rows, w), F32), pltpu.SemaphoreType.DMA((nt,)),
                        pltpu.SemaphoreType.DMA((nt,))],
        compiler_params=_params(("arbitrary",)), name=name)(parts)


def _adamw_shard(name, g, w, m, v):
    rows, cols = w.shape
    tr = _row_tile(rows, 256)

    def body(g_ref, w_ref, m_ref, v_ref, d_out, m_out, v_out):
        delta, m_new, v_new = _adamw(w_ref[...], g_ref[...], m_ref[...], v_ref[...])
        d_out[...] = delta
        m_out[...] = m_new
        v_out[...] = v_new

    t_spec = pl.BlockSpec((tr, cols), lambda i: (i, 0))
    shp = jax.ShapeDtypeStruct((rows, cols), F32)
    return pl.pallas_call(body, out_shape=[shp] * 3, grid=(rows // tr,), in_specs=[t_spec] * 4, out_specs=[t_spec] * 3,
                          compiler_params=_params(("parallel",)), name=name)(g, w, m, v)


def _shard_shape(name):
    _, r, c, ax = BIG_BY_NAME[name]
    return (r, c // 4) if ax == 1 else (r // 4, c)


def _pad_rows(a, axis):
    pad = [(0, 0)] * a.ndim
    pad[axis] = (0, -a.shape[axis] % PACK_ALIGN)
    return jnp.pad(a, pad)


def _pack_shards(names, shards, dtype):
    return _pad_rows(jnp.concatenate([s.astype(dtype).reshape(-1, PACK_W) for s in shards], axis=0), 0)


def _unpack_shards(names, slab):
    out, off = [], 0
    for name in names:
        rs, cs = _shard_shape(name)
        n = rs * cs // PACK_W
        out.append(slab[off:off + n].reshape(rs, cs))
        off += n
    return out


def _unpack_full(names, slabs):
    out, off = [], 0
    for name in names:
        _, r, c, ax = BIG_BY_NAME[name]
        n = r * c // 4 // PACK_W
        seg = slabs[:, off:off + n]
        out.append(seg.reshape(4, r, c // 4).transpose(1, 0, 2).reshape(r, c) if ax == 1 else seg.reshape(r, c))
        off += n
    return out


def _pack_full(names, mats, dtype):
    segs = []
    for name, a in zip(names, mats):
        _, r, c, ax = BIG_BY_NAME[name]
        a = a.astype(dtype)
        a = a.reshape(r, 4, c // 4).transpose(1, 0, 2) if ax == 1 else a
        segs.append(a.reshape(4, -1, PACK_W))
    return _pad_rows(jnp.concatenate(segs, axis=1), 1)


def _pad_heads_cols(wm, heads, d):
    k = wm.shape[0]
    return jnp.pad(wm.reshape(k, heads, d), ((0, 0), (0, 0), (0, HEAD_W - d))).reshape(k, heads * HEAD_W)


def _unpad_heads_cols(wm, heads, d):
    k = wm.shape[0]
    return wm.reshape(k, heads, HEAD_W)[:, :, :d].reshape(k, heads * d)


def _win_ext(w_in):
    o = np.cumsum([0, Q_LORA, KV_LORA, QK_ROPE, H_B * HD_B, KV_B * HD_B, KV_B * HD_B, D_MODEL, D_MODEL])
    pc = lambda a, n: jnp.pad(a, ((0, 0), (0, n - a.shape[1])))
    return jnp.concatenate([
        _pad_heads_cols(w_in[:, o[3]:o[4]], H_B, HD_B), w_in[:, o[0]:o[1]], w_in[:, o[1]:o[2]], pc(w_in[:, o[2]:o[3]], HEAD_W),
        _pad_heads_cols(w_in[:, o[4]:o[5]], KV_B, HD_B), _pad_heads_cols(w_in[:, o[5]:o[6]], KV_B, HD_B),
        w_in[:, o[6]:o[7]], w_in[:, o[7]:o[8]]], axis=1)


def _win_unext(we):
    c = HEAD_W
    return jnp.concatenate([
        we[:, ZC_QLAT * c:ZC_CKV * c], we[:, ZC_CKV * c:ZC_KPE * c], we[:, ZC_KPE * c:ZC_KPE * c + QK_ROPE],
        _unpad_heads_cols(we[:, ZC_QB * c:ZC_QLAT * c], H_B, HD_B), _unpad_heads_cols(we[:, ZC_KB * c:ZC_VB * c], KV_B, HD_B),
        _unpad_heads_cols(we[:, ZC_VB * c:ZC_GA * c], KV_B, HD_B), we[:, ZC_GA * c:]], axis=1)


def _wkv_ext(w_kvb):
    wk = w_kvb.reshape(KV_LORA, H_A, QK_NOPE + V_DIM_A)
    k_cols = jnp.pad(wk[:, :, :QK_NOPE], ((0, 0), (0, 0), (0, HEAD_W - QK_NOPE))).reshape(KV_LORA, H_A * HEAD_W)
    v_cols = jnp.pad(wk[:, :, QK_NOPE:], ((0, 0), (0, 0), (0, HEAD_W - V_DIM_A))).reshape(KV_LORA, H_A * HEAD_W)
    eye = jnp.pad(jnp.eye(QK_ROPE, dtype=w_kvb.dtype), ((0, 0), (QK_NOPE, HEAD_W - QK_NOPE - QK_ROPE)))
    pe_rows = jnp.concatenate([jnp.tile(eye, (1, H_A)), jnp.zeros((QK_ROPE, H_A * HEAD_W), w_kvb.dtype)], axis=1)
    top = jnp.concatenate([k_cols, v_cols], axis=1)
    return jnp.concatenate([top, pe_rows, jnp.zeros((2 * HEAD_W - KV_LORA - QK_ROPE, 2 * H_A * HEAD_W), w_kvb.dtype)], axis=0)


def _wkv_unext(we):
    k_cols = we[:KV_LORA, :H_A * HEAD_W].reshape(KV_LORA, H_A, HEAD_W)[:, :, :QK_NOPE]
    v_cols = we[:KV_LORA, H_A * HEAD_W:].reshape(KV_LORA, H_A, HEAD_W)[:, :, :V_DIM_A]
    return jnp.concatenate([k_cols, v_cols], axis=2).reshape(KV_LORA, H_A * (QK_NOPE + V_DIM_A))


def _pad_heads_rows(wm, heads, d):
    n = wm.shape[1]
    return jnp.pad(wm.reshape(heads, d, n), ((0, 0), (0, HEAD_W - d), (0, 0))).reshape(heads * HEAD_W, n)


def _unpad_heads_rows(wm, heads, d):
    n = wm.shape[1]
    return wm.reshape(heads, HEAD_W, n)[:, :d].reshape(heads * d, n)


def _rope_tables(seq):
    def ang(pos, dim):
        inv = np.float32(ROPE_THETA) ** (-np.arange(0, dim, 2, dtype=np.float32) / np.float32(dim))
        return pos.astype(np.float32)[:, None] * inv[None, :]

    def rot(dim):
        r = np.zeros((dim, dim), np.float32)
        half = dim // 2
        r[np.arange(half) + half, np.arange(half)] = -1.0
        r[np.arange(half), np.arange(half) + half] = 1.0
        return r

    def table(blocks):
        cos, sin = np.ones((seq, HEAD_W), np.float32), np.zeros((seq, HEAD_W), np.float32)
        pm = np.zeros((HEAD_W, HEAD_W), np.float32)
        for c0, a in blocks:
            d = 2 * a.shape[1]
            cos[:, c0:c0 + d] = np.concatenate([np.cos(a), np.cos(a)], axis=1)
            sin[:, c0:c0 + d] = np.concatenate([np.sin(a), np.sin(a)], axis=1)
            pm[c0:c0 + d, c0:c0 + d] = rot(d)
        return jnp.asarray(cos), jnp.asarray(sin), jnp.asarray(pm, BF16), jnp.asarray(pm.T, BF16)

    tok = np.arange(seq)
    a1 = ang(tok, QK_ROPE)
    arow, acol = ang(tok // GRID_W, HD_B // 2), ang(tok % GRID_W, HD_B // 2)
    return table([(QK_NOPE, a1)]), table([(0, a1)]), table([(0, arow), (HD_B // 2, acol)])


def _local_step(x, p, tgt, gains, wts, ride=None):
    nb, seq, _ = x.shape
    t = nb * seq
    x0 = x.reshape(t, D_MODEL)
    p2 = p.reshape(t, PLE_DIM)
    tg = tgt.reshape(t, D_MODEL)
    (cq_t, sq_t, pq, pq_t), (ck_t, sk_t, pk, pk_t), (cb_t, sb_t, pb, pb_t) = _rope_tables(seq)
    padg = lambda g: jnp.pad(g, ((0, 0), (0, HEAD_W - g.shape[1])))
    g_qn, g_kn = padg(gains["g_qn"]), padg(gains["g_kn"])

    win = _win_ext(wts["w_in"])
    wqb = _pad_heads_cols(wts["w_qb"], H_A, QK_NOPE + QK_ROPE)
    wkv = _wkv_ext(wts["w_kvb"])

    norm = lambda n: (lambda v, g: (_rms(v, g, n),))
    full = lambda a: (a, a.shape[1], 0, False)

    h = _rowwise("norm_mix", norm(D_MODEL), [full(x0)], [(D_MODEL, BF16, D_MODEL, False)], consts=[gains["g_mix"]])
    z = _mm("in_proj", h, win, tn=2048)
    cq = _rowwise("norm_qa", norm(Q_LORA), [(z, Q_LORA, ZC_QLAT // 2, False)], [(Q_LORA, BF16, Q_LORA, False)],
                  consts=[gains["g_qa"]])
    qa = _mm("q_up", cq, wqb)

    def rope_fwd(scale):
        return lambda v, cos, sin, pm: ((v * cos + _perm(v, pm) * sin) * scale,)

    heads_tile = lambda n: (n * HEAD_W, BF16, n * HEAD_W, False)
    q_a = _rowwise("rope_qa", _per_head(rope_fwd(SCALE_A), H_A, 1, 1), [full(qa)], [heads_tile(H_A)],
                   pos=[cq_t, sq_t], consts=[pq], seq=seq)
    ckv = _rowwise("norm_kva", norm(KV_LORA), [(z, HEAD_W, ZC_CKV, False)], [(HEAD_W, BF16, HEAD_W, False)], consts=[gains["g_kva"]])
    kpe = _rowwise("rope_kpe", rope_fwd(1.0), [(z, HEAD_W, ZC_KPE, False)], [(HEAD_W, BF16, HEAD_W, False)],
                   pos=[ck_t, sk_t], consts=[pk], seq=seq)
    kin = jnp.concatenate([ckv, kpe], axis=1)
    kv_a = _mm("kv_up", kin, wkv, out_dtypes=(BF16,))
    o_a, lse_a, *got_a = _attn_fwd("attn_a_fwd", q_a, kv_a, 0, kv_a, H_A, heads=H_A, group=1, nseq=nb, seq=seq,
                                   rider=ride and ride["gather_a"])

    def prep_fwd(scale):
        def fn(v, cos, sin, g, pm):
            yv = _rms(v, g, HD_B)
            return ((yv * cos + _perm(yv, pm) * sin) * scale,)
        return fn

    z_qb, z_kb = (z, H_B * HEAD_W, ZC_QB // H_B, False), (z, KV_B * HEAD_W, ZC_KB // KV_B, False)
    q_b = _rowwise("prep_qb", _per_head(prep_fwd(SCALE_B), H_B, 1, 1), [z_qb], [heads_tile(H_B)],
                   pos=[cb_t, sb_t], consts=[g_qn, pb], seq=seq)
    k_b = _rowwise("prep_kb", _per_head(prep_fwd(1.0), KV_B, 1, 1), [z_kb], [heads_tile(KV_B)],
                   pos=[cb_t, sb_t], consts=[g_kn, pb], seq=seq)
    o_b, lse_b, *got_b = _attn_fwd("attn_b_fwd", q_b, k_b, 0, z, ZC_VB, heads=H_B, group=H_B // KV_B, nseq=nb, seq=seq,
                                   rider=ride and ride["gather_b"])
    if ride is not None:
        wts = {**wts, **ride["late_weights"](got_a[0], got_b[0])}
    woa = _pad_heads_rows(wts["w_oa"], H_A, V_DIM_A)
    wob = _pad_heads_rows(wts["w_ob"], H_B, HD_B)
    wo, wup, wdown, wpg, wple = wts["w_o"], wts["w_up"], wts["w_down"], wts["w_ple_gate"], wts["w_ple"]

    def residual_norm(acc, r, g):
        xv = r + acc
        return xv, _rms(xv, g, D_MODEL)

    def mix_out(oa, ob, ga, gb, r, g, w_a, w_b, w_out):
        a = jnp.dot(oa, w_a[...], preferred_element_type=F32)
        b = jnp.dot(ob, w_b[...], preferred_element_type=F32)
        mg = (_sigmoid(ga) * a + _sigmoid(gb) * b).astype(BF16)
        return (a, b, mg, *residual_norm(jnp.dot(mg, w_out[...], preferred_element_type=F32), r, g))

    z_ga, z_gb = (z, D_MODEL, ZC_GA // 8, False), (z, D_MODEL, ZC_GB // 8, False)
    wide = lambda d: (D_MODEL, d, D_MODEL, False)
    ya, yb, merged, x1, h2 = _rowwise("mix_out", mix_out, [full(o_a), full(o_b), z_ga, z_gb, full(x0)],
                                      [wide(F32), wide(F32), wide(BF16), wide(F32), wide(BF16)],
                                      consts=[gains["g_mlp"], woa, wob, wo], tm=256)

    def relu2(acc):
        u = jnp.maximum(acc, 0.0)
        return u, u * u

    u, usq = _mm("mlp_up", h2, wup, out_dtypes=(BF16, BF16), epi=relu2, tn=2048)
    x2, h3 = _mm("mlp_down", usq, wdown, out_dtypes=(F32, BF16), epi=residual_norm, extras=(x1,), consts=[gains["g_ple"]])

    def tail(x2v, h3v, pv, tv, gf, w_gate, w_emb):
        sg = _sigmoid(jnp.dot(h3v, w_gate[...], preferred_element_type=F32))
        pev = jnp.dot(pv.astype(BF16), w_emb[...], preferred_element_type=F32)
        x3 = x2v + sg * pev
        rs = lax.rsqrt(jnp.sum(x3 * x3, axis=-1, keepdims=True) * (1.0 / D_MODEL) + EPS)
        xh = x3 * rs
        err = xh * gf - tv
        dy = err * (1.0 / D_MODEL)
        dyg = dy * gf
        dx3 = rs * (dyg - xh * (jnp.sum(dyg * xh, axis=-1, keepdims=True) * (1.0 / D_MODEL)))
        return (dx3, dx3 * pev * sg * (1.0 - sg), dx3 * sg,
                jnp.sum(err * err, axis=0, keepdims=True) * (0.5 / D_MODEL), jnp.sum(dy * xh, axis=0, keepdims=True))

    dx3, dgpre, dpe, loss_part, dg_final = _rowwise(
        "tail", tail, [full(x2), full(h3), full(p2), full(tg)], [wide(F32), wide(BF16), wide(BF16)],
        consts=[gains["g_final"].reshape(1, D_MODEL), wpg, wple], accs=[(1, D_MODEL), (1, D_MODEL)], tm=256)

    def norm_bwd(n, with_res):
        if with_res:
            def fn(dh, v, res, g):
                dx, dg = _rms_bwd(dh, v, g, n)
                return dx + res, dg
        else:
            def fn(dh, v, g):
                return _rms_bwd(dh, v, g, n)
        return fn

    dw = {}
    dw["w_ple"] = _mm_tn("dw_ple", p2, dpe)
    dw["w_ple_gate"] = _mm_tn("dw_ple_gate", h3, dgpre)
    norm_res_bwd = norm_bwd(D_MODEL, True)
    dx2, dg_ple = _mm("d_ple_gate", dgpre, wpg, trans_b=True, epi=norm_res_bwd, extras=(x2, dx3), consts=[gains["g_ple"]],
                      accs=[(1, D_MODEL)], tm=256)
    dw["w_down"] = _mm_tn("dw_down", usq, dx2)
    dupre = _mm("d_mlp_down", dx2, wdown, trans_b=True, out_dtypes=(BF16,), epi=lambda acc, uv: (acc * (2.0 * uv.astype(F32)),),
                extras=(u,), tn=2048)
    dw["w_up"] = _mm_tn("dw_up", h2, dupre)
    dx1, dg_mlp = _mm("d_mlp_up", dupre, wup, trans_b=True, epi=norm_res_bwd, extras=(x1, dx2), consts=[gains["g_mlp"]],
                      accs=[(1, D_MODEL)], tm=256)
    dw["w_o"] = _mm_tn("dw_o", merged, dx1)

    def merge_bwd(dm, ga, gb, a, b):
        sa, sb = _sigmoid(ga), _sigmoid(gb)
        return dm * sa, dm * sb, dm * a * sa * (1.0 - sa), dm * b * sb * (1.0 - sb)

    dya, dyb, dga, dgb = _mm("d_out_proj", dx1, wo, trans_b=True, out_dtypes=(BF16,) * 4, epi=merge_bwd,
                             extras=((z, ZC_GA // 8), (z, ZC_GB // 8), ya, yb), tm=256)
    dw["w_oa"] = _unpad_heads_rows(_mm_tn("dw_oa", o_a, dya), H_A, V_DIM_A)
    dw["w_ob"] = _unpad_heads_rows(_mm_tn("dw_ob", o_b, dyb), H_B, HD_B)
    do_a = _mm("d_out_a", dya, woa, trans_b=True, out_dtypes=(BF16,))
    do_b = _mm("d_out_b", dyb, wob, trans_b=True, out_dtypes=(BF16,))
    res_a = _attn_bwd("attn_a_bwd", q_a, kv_a, 0, kv_a, H_A, o_a, do_a, lse_a, heads=H_A, group=1, nseq=nb, seq=seq,
                      rider=ride and ride["scatter_late"](dw))
    dq_a, dk_a, dv_a = res_a[:3]
    if ride is not None:
        ride["out"]["parts_late"] = res_a[3]

    def rope_bwd(scale):
        return lambda d, cos, sin, pm_t: ((d * cos + _perm(d * sin, pm_t)) * scale,)

    dqa = _rowwise("rope_qa_bwd", _per_head(rope_bwd(SCALE_A), H_A, 1, 1), [full(dq_a)], [heads_tile(H_A)],
                   pos=[cq_t, sq_t], consts=[pq_t], seq=seq)
    dw["w_qb"] = _unpad_heads_cols(_mm_tn("dw_qb", cq, dqa), H_A, QK_NOPE + QK_ROPE)
    dcq = _mm("d_q_up", dqa, wqb, trans_b=True)
    dq_lat, dg_qa = _rowwise("norm_qa_bwd", norm_bwd(Q_LORA, False), [full(dcq), (z, Q_LORA, ZC_QLAT // 2, False)],
                             [(Q_LORA, BF16, Q_LORA, False)], consts=[gains["g_qa"]], accs=[(1, Q_LORA)])
    dkv_a = jnp.concatenate([dk_a, dv_a], axis=1)
    dw["w_kvb"] = _wkv_unext(_mm_tn("dw_kv", kin, dkv_a))
    dq_b, dk_b, dv_b, *parts_small = _attn_bwd("attn_b_bwd", q_b, k_b, 0, z, ZC_VB, o_b, do_b, lse_b, heads=H_B, group=H_B // KV_B,
                                               nseq=nb, seq=seq, rider=ride and ride["scatter_small"](dw))
    if ride is not None:
        ride["out"]["parts_small"] = parts_small[0]
    dkin = _mm("d_kv_up", dkv_a, wkv, trans_b=True)
    dckv, dg_kva = _rowwise("norm_kva_bwd", norm_bwd(KV_LORA, False), [(dkin, HEAD_W, 0, False), (z, HEAD_W, ZC_CKV, False)],
                            [(HEAD_W, BF16, HEAD_W, False)], consts=[gains["g_kva"]], accs=[(1, KV_LORA)])
    dkpe = _rowwise("rope_kpe_bwd", rope_bwd(1.0), [(dkin, HEAD_W, 1, False)], [(HEAD_W, BF16, HEAD_W, False)],
                    pos=[ck_t, sk_t], consts=[pk_t], seq=seq)

    def prep_bwd(scale):
        def fn(d, v, cos, sin, g, pm_t):
            dyv = (d * cos + _perm(d * sin, pm_t)) * scale
            return _rms_bwd(dyv, v, g, HD_B)
        return fn

    dqb, dg_qn = _rowwise("prep_qb_bwd", _per_head(prep_bwd(SCALE_B), H_B, 2, 1), [full(dq_b), z_qb], [heads_tile(H_B)],
                          pos=[cb_t, sb_t], consts=[g_qn, pb_t], accs=[(1, HEAD_W)], seq=seq)
    dkb, dg_kn = _rowwise("prep_kb_bwd", _per_head(prep_bwd(1.0), KV_B, 2, 1), [full(dk_b), z_kb], [heads_tile(KV_B)],
                          pos=[cb_t, sb_t], consts=[g_kn, pb_t], accs=[(1, HEAD_W)], seq=seq)

    dz = jnp.concatenate([dqb, dq_lat, dckv, dkpe, dkb, dv_b, dga, dgb], axis=1)
    dw["w_in"] = _win_unext(_mm_tn("dw_in", h, dz))
    dx0, dg_mix, *parts_in = _mm("d_in_proj", dz, win, trans_b=True, epi=norm_res_bwd, extras=(x0, dx1), consts=[gains["g_mix"]],
                                 accs=[(1, D_MODEL)], tm=256, rider=ride and ride["scatter_in"](dw))
    if ride is not None:
        ride["out"]["parts_in"] = parts_in[0]

    dg = {"g_mix": dg_mix, "g_qa": dg_qa, "g_kva": dg_kva, "g_qn": dg_qn[:, :HD_B], "g_kn": dg_kn[:, :HD_B],
          "g_mlp": dg_mlp, "g_ple": dg_ple, "g_final": dg_final}
    return loss_part, dx0.reshape(nb, seq, D_MODEL), dg, dw


def _pack_small(vals, loss_part=None):
    flat = jnp.concatenate([vals[n].reshape(1, -1) for n, _ in SMALL], axis=1)
    loss = jnp.zeros((1, 8 * 128), F32) if loss_part is None else loss_part
    gap = jnp.zeros((1, LOSS_ROW0 * 128 - SMALL_N), F32)
    return jnp.concatenate([flat, gap, loss], axis=1).reshape(SMALL_ROWS, 128)


def _unpack_small(slab, like):
    flat, out, off = slab.reshape(-1), {}, 0
    for n, k in SMALL:
        out[n] = flat[off:off + k].reshape(like[n].shape)
        off += k
    return out


def kernel(x, p, g_mix, w_in, g_qa, w_qb, g_kva, w_kvb, g_qn, g_kn, w_oa, w_ob, w_o, g_mlp, w_up, w_down, g_ple, w_ple_gate, w_ple, g_final, loss_target, m_g_mix, m_w_in, m_g_qa, m_w_qb, m_g_kva, m_w_kvb, m_g_qn, m_g_kn, m_w_oa, m_w_ob, m_w_o, m_g_mlp, m_w_up, m_w_down, m_g_ple, m_w_ple_gate, m_w_ple, m_g_final, v_g_mix, v_w_in, v_g_qa, v_w_qb, v_g_kva, v_w_kvb, v_g_qn, v_g_kn, v_w_oa, v_w_ob, v_w_o, v_g_mlp, v_w_up, v_w_down, v_g_ple, v_w_ple_gate, v_w_ple, v_g_final):
    given = dict(locals())
    order = ["g_mix", "w_in", "g_qa", "w_qb", "g_kva", "w_kvb", "g_qn", "g_kn", "w_oa", "w_ob", "w_o", "g_mlp", "w_up",
             "w_down", "g_ple", "w_ple_gate", "w_ple", "g_final"]
    big_names = [n for n, _, _, _ in BIG]
    local = lambda prefix, names: [given[prefix + n][0] for n in names]
    late = LATE_A + LATE_B

    shards_cols = lambda a: a.reshape(a.shape[0], 4, a.shape[1] // 4).transpose(1, 0, 2)
    got_in, got_small = _exchange_alone("weight_gather_early", [
        _Exchange("gather", w_in[0].astype(BF16)), _Exchange("gather", _pack_shards(EARLY_SMALL, local("", EARLY_SMALL), BF16))])
    wts = {"w_in": got_in.transpose(1, 0, 2).reshape(got_in.shape[1], -1), **dict(zip(EARLY_SMALL, _unpack_full(EARLY_SMALL, got_small)))}
    gains = {n: given[n].reshape(1, -1) for n, _ in SMALL}
    ride = {
        "gather_a": _Exchange("gather", _pack_shards(LATE_A, local("", LATE_A), BF16)),
        "gather_b": _Exchange("gather", _pack_shards(LATE_B, local("", LATE_B), BF16)),
        "late_weights": lambda ga, gb: {**dict(zip(LATE_A, _unpack_full(LATE_A, ga))), **dict(zip(LATE_B, _unpack_full(LATE_B, gb)))},
        "scatter_late": lambda dw: _Exchange("scatter", _pack_full(late, [dw[n] for n in late], BF16)),
        "scatter_small": lambda dw: _Exchange("scatter", _pack_full(EARLY_SMALL, [dw[n] for n in EARLY_SMALL], BF16)),
        "scatter_in": lambda dw: _Exchange("scatter", shards_cols(dw["w_in"].astype(BF16))),
        "out": {},
    }
    loss_part, grad_x, dg, dw = _local_step(x, p[0], loss_target, gains, wts, ride)

    small = lambda prefix: _pack_small({n: given[prefix + n] for n, _ in SMALL})
    g_s, d_s, m_s, v_s, loss = _small_allreduce_adamw(_pack_small(dg, loss_part), small(""), small("m_"), small("v_"))

    grads = dict(zip(late, _unpack_shards(late, _reduce_pair("grad_reduce_late", ride["out"]["parts_late"]))))
    grads.update(zip(EARLY_SMALL, _unpack_shards(EARLY_SMALL, _reduce_pair("grad_reduce_small", ride["out"]["parts_small"]))))
    grads["w_in"] = _reduce_pair("grad_reduce_in", ride["out"]["parts_in"])

    res = {}
    for key, slab in (("grad_", g_s), ("delta_", d_s), ("new_m_", m_s), ("new_v_", v_s)):
        for n, val in _unpack_small(slab, given).items():
            res[key + n] = val
    for n in big_names:
        d_w, m_w, v_w = _adamw_shard("adamw_" + n, grads[n], given[n][0], given["m_" + n][0], given["v_" + n][0])
        res["grad_" + n], res["delta_" + n], res["new_m_" + n], res["new_v_" + n] = grads[n][None], d_w[None], m_w[None], v_w[None]
    outs = [loss.reshape(()), grad_x]
    for key in ("grad_", "delta_", "new_m_", "new_v_"):
        outs += [res[key + n] for n in order]
    return tuple(outs)
```

```python
import functools

import numpy as np
import jax
import jax.numpy as jnp
from jax import lax
from jax.experimental import pallas as pl
from jax.experimental.pallas import tpu as pltpu

F32 = jnp.float32
BF16 = jnp.bfloat16
MESH = pl.DeviceIdType.MESH

D_MODEL = 1024
GRID_W = 64
ROPE_THETA = 10000.0
EPS = 1e-6
H_A, QK_NOPE, QK_ROPE, V_DIM_A, Q_LORA, KV_LORA = 8, 64, 32, 64, 256, 128
H_B, KV_B, HD_B = 8, 2, 64
D_FF = 4096
PLE_DIM = 256
HEAD_W = 128
SCALE_A = (QK_NOPE + QK_ROPE) ** -0.5
SCALE_B = HD_B ** -0.5

ADAM_LR, ADAM_B1, ADAM_B2, ADAM_EPS, ADAM_WD, ADAM_STEP = 0.001, 0.9, 0.999, 1e-08, 0.01, 10
M_HAT_DIV = 1.0 - ADAM_B1 ** ADAM_STEP
V_HAT_DIV = 1.0 - ADAM_B2 ** ADAM_STEP

VMEM_LIMIT_BYTES = 56 * 1024 * 1024

ZC_QB, ZC_QLAT, ZC_CKV, ZC_KPE, ZC_KB, ZC_VB, ZC_GA, ZC_GB = 0, 8, 10, 11, 12, 14, 16, 24
Z_WIDTH = 32 * HEAD_W

BIG = [
    ("w_in", 1024, 3232, 1), ("w_qb", 256, 768, 1), ("w_kvb", 128, 1024, 1), ("w_oa", 512, 1024, 1),
    ("w_ob", 512, 1024, 1), ("w_o", 1024, 1024, 0), ("w_up", 1024, 4096, 1), ("w_down", 4096, 1024, 0),
    ("w_ple_gate", 1024, 1024, 0), ("w_ple", 256, 1024, 1),
]
BIG_BY_NAME = {e[0]: e for e in BIG}
PACK_W = 1024
PACK_ALIGN = 64
EARLY_SMALL = ["w_qb", "w_kvb"]
LATE_A = ["w_oa", "w_ob", "w_o", "w_up"]
LATE_B = ["w_down", "w_ple_gate", "w_ple"]

SMALL = [("g_mix", 1024), ("g_qa", 256), ("g_kva", 128), ("g_qn", 64), ("g_kn", 64), ("g_mlp", 1024),
         ("g_ple", 1024), ("g_final", 1024)]
SMALL_N = sum(n for _, n in SMALL)
LOSS_ROW0 = 40
SMALL_ROWS = 48


def _params(sem):
    return pltpu.CompilerParams(dimension_semantics=sem, vmem_limit_bytes=VMEM_LIMIT_BYTES)


def _sigmoid(v):
    return 1.0 / (1.0 + jnp.exp(-v))


def _perm(v, p_ref):
    pm = p_ref[...]
    hi = v.astype(BF16)
    lo = (v - hi.astype(F32)).astype(BF16)
    return (jnp.dot(hi, pm, preferred_element_type=F32) + jnp.dot(lo, pm, preferred_element_type=F32))


def _rms(v, g, n):
    rs = lax.rsqrt(jnp.sum(v * v, axis=-1, keepdims=True) * (1.0 / n) + EPS)
    return v * rs * g


def _rms_bwd(dy, v, g, n):
    rs = lax.rsqrt(jnp.sum(v * v, axis=-1, keepdims=True) * (1.0 / n) + EPS)
    vh = v * rs
    dyg = dy * g
    dx = rs * (dyg - vh * (jnp.sum(dyg * vh, axis=-1, keepdims=True) * (1.0 / n)))
    return dx, jnp.sum(dy * vh, axis=0, keepdims=True)


def _ride(body, grid, rider):
    if rider is None:
        return body, [], [], [], []
    n_sem = len(rider.scratch)

    def wrapped(*refs):
        ids = [pl.program_id(a) for a in range(len(grid))]
        n_in = len(refs) - n_sem - 2 - rider.n_core_out - rider.n_core_scratch
        core_in, src = refs[:n_in], refs[n_in]
        core_out = refs[n_in + 1:n_in + 1 + rider.n_core_out]
        dst = refs[n_in + 1 + rider.n_core_out]
        core_scr = refs[n_in + 2 + rider.n_core_out:len(refs) - n_sem]
        sems = refs[len(refs) - n_sem:]

        @pl.when(functools.reduce(jnp.logical_and, [a == 0 for a in ids]))
        def _():
            rider.start(src, dst, *sems)

        body(*core_in, *core_out, *core_scr)

        @pl.when(functools.reduce(jnp.logical_and, [a == n - 1 for a, n in zip(ids, grid)]))
        def _():
            rider.finish(src, dst, *sems)

    hbm = pl.BlockSpec(memory_space=pl.ANY)
    return wrapped, [rider.src], [hbm], [rider.out_shape], list(rider.scratch)


def _mm(name, a, b, *, trans_b=False, out_dtypes=(F32,), epi=None, extras=(), consts=(), accs=(), tm=512, tn=None, rider=None):
    m, k = a.shape
    n = b.shape[0] if trans_b else b.shape[1]
    tn = n if tn is None else min(tn, n)
    tm = min(tm, m)
    assert m % tm == 0 and n % tn == 0 and (b.shape[1] if trans_b else b.shape[0]) == k
    extras = [e if isinstance(e, tuple) else (e, 0) for e in extras]
    n_ex, n_c, n_out, n_acc = len(extras), len(consts), len(out_dtypes), len(accs)
    dims = (((1,), (1,)), ((), ())) if trans_b else (((1,), (0,)), ((), ()))

    def body(a_ref, b_ref, *rest):
        acc = lax.dot_general(a_ref[...].astype(BF16), b_ref[...].astype(BF16), dims, preferred_element_type=F32)
        res = (acc,) if epi is None else epi(acc, *[e[...] for e in rest[:n_ex + n_c]])
        o_refs = rest[n_ex + n_c:]
        for o_ref, r in zip(o_refs[:n_out], res[:n_out]):
            o_ref[...] = r.astype(o_ref.dtype)
        if n_acc:
            first = jnp.logical_and(pl.program_id(0) == 0, pl.program_id(1) == 0)

            @pl.when(first)
            def _():
                for o_ref, r in zip(o_refs[n_out:], res[n_out:]):
                    o_ref[...] = r

            @pl.when(jnp.logical_not(first))
            def _():
                for o_ref, r in zip(o_refs[n_out:], res[n_out:]):
                    o_ref[...] += r

    grid = (n // tn, m // tm)
    if rider is not None:
        rider.n_core_out, rider.n_core_scratch = n_out + n_acc, 0
    body, x_in, x_spec, x_out, x_scr = _ride(body, grid, rider)
    a_spec = pl.BlockSpec((tm, k), lambda j, i: (i, 0))
    b_spec = pl.BlockSpec((tn, k), lambda j, i: (j, 0)) if trans_b else pl.BlockSpec((k, tn), lambda j, i: (0, j))
    t_spec = pl.BlockSpec((tm, tn), lambda j, i: (i, j))
    e_specs = [pl.BlockSpec((tm, tn), lambda j, i, off=off: (i, j + off)) for _, off in extras]
    c_specs = [pl.BlockSpec(c.shape, lambda j, i: (0, 0)) for c in consts]
    acc_specs = [pl.BlockSpec(sh, lambda j, i: (0, 0)) for sh in accs]
    sem = ("parallel", "parallel") if rider is None and not n_acc else ("arbitrary", "arbitrary")
    outs = pl.pallas_call(
        body, out_shape=[jax.ShapeDtypeStruct((m, n), d) for d in out_dtypes] + [jax.ShapeDtypeStruct(sh, F32) for sh in accs] + x_out,
        grid=grid, in_specs=[a_spec, b_spec] + e_specs + c_specs + x_spec, out_specs=[t_spec] * n_out + acc_specs + x_spec,
        scratch_shapes=x_scr, compiler_params=_params(sem), name=name)(a, b, *[e for e, _ in extras], *consts, *x_in)
    return outs[0] if len(outs) == 1 else outs


def _per_head(fn, heads, n_tiled, n_out):
    def run(*args):
        res = [fn(*[a[:, hd * HEAD_W:(hd + 1) * HEAD_W] for a in args[:n_tiled]], *args[n_tiled:]) for hd in range(heads)]
        tiles = [jnp.concatenate([r[k] for r in res], axis=1) for k in range(n_out)]
        sums = [functools.reduce(lambda u, v: u + v, [r[k] for r in res]) for k in range(n_out, len(res[0]))]
        return (*tiles, *sums)
    return run


def _mm_tn(name, a, b, *, out_dtype=BF16, tk=1024, tn=1024, tt=1024):
    t, k = a.shape
    n = b.shape[1]
    tk, tn, tt = min(tk, k), min(tn, n), min(tt, t)
    assert b.shape[0] == t and k % tk == 0 and n % tn == 0 and t % tt == 0
    nt = t // tt

    def body(a_ref, b_ref, o_ref, acc):
        part = lax.dot_general(a_ref[...].astype(BF16), b_ref[...].astype(BF16), (((0,), (0,)), ((), ())),
                               preferred_element_type=F32)

        @pl.when(pl.program_id(2) == 0)
        def _():
            acc[...] = part

        @pl.when(pl.program_id(2) != 0)
        def _():
            acc[...] += part

        @pl.when(pl.program_id(2) == nt - 1)
        def _():
            o_ref[...] = acc[...].astype(o_ref.dtype)

    return pl.pallas_call(
        body, out_shape=jax.ShapeDtypeStruct((k, n), out_dtype), grid=(k // tk, n // tn, nt),
        in_specs=[pl.BlockSpec((tt, tk), lambda ki, ni, ti: (ti, ki)), pl.BlockSpec((tt, tn), lambda ki, ni, ti: (ti, ni))],
        out_specs=pl.BlockSpec((tk, tn), lambda ki, ni, ti: (ki, ni)), scratch_shapes=[pltpu.VMEM((tk, tn), F32)],
        compiler_params=_params(("parallel", "parallel", "arbitrary")), name=name)(a, b)


def _rowwise(name, fn, ins, outs, *, consts=(), pos=(), accs=(), heads=1, tm=512, seq=None):
    t = ins[0][0].shape[0]
    tm = min(tm, t if seq is None else seq)
    assert t % tm == 0 and (seq is None or seq % tm == 0)
    n_in, n_pos, n_c, n_out, n_acc = len(ins), len(pos), len(consts), len(outs), len(accs)

    def body(*refs):
        vals = [r[...] for r in refs[:n_in + n_pos + n_c]]
        res = fn(*vals)
        o_refs = refs[n_in + n_pos + n_c:]
        for o_ref, r in zip(o_refs[:n_out], res[:n_out]):
            o_ref[...] = r.astype(o_ref.dtype)
        if n_acc:
            first = jnp.logical_and(pl.program_id(0) == 0, pl.program_id(1) == 0)

            @pl.when(first)
            def _():
                for o_ref, r in zip(o_refs[n_out:], res[n_out:]):
                    o_ref[...] = r

            @pl.when(jnp.logical_not(first))
            def _():
                for o_ref, r in zip(o_refs[n_out:], res[n_out:]):
                    o_ref[...] += r

    def tiled(width, c0, per_head):
        return pl.BlockSpec((tm, width), (lambda h, i: (i, c0 + h)) if per_head else (lambda h, i: (i, c0)))

    in_specs = [tiled(w, c0, ph) for _, w, c0, ph in ins]
    if n_pos:
        nblk = seq // tm
        in_specs += [pl.BlockSpec((tm, a.shape[1]), lambda h, i: (i % nblk, 0)) for a in pos]
    in_specs += [pl.BlockSpec(a.shape, lambda h, i: (0, 0)) for a in consts]
    out_specs = [tiled(w, 0, ph) for _, _, w, ph in outs] + [pl.BlockSpec(s, lambda h, i: (0, 0)) for s in accs]
    out_shape = [jax.ShapeDtypeStruct((t, c), d) for c, d, _, _ in outs] + [jax.ShapeDtypeStruct(s, F32) for s in accs]
    sem = ("arbitrary", "arbitrary") if n_acc else ("parallel", "parallel")
    res = pl.pallas_call(body, out_shape=out_shape, grid=(heads, t // tm), in_specs=in_specs, out_specs=out_specs,
                         compiler_params=_params(sem), name=name)(*[a for a, _, _, _ in ins], *pos, *consts)
    return res[0] if len(res) == 1 else res


ATTN_HEADS_PER_STEP = 2


def _attn_fwd(name, q, k, kc0, v, vc0, *, heads, group, nseq, seq, tq=512, rider=None):
    tq = min(tq, seq)
    nq = seq // tq
    hp = ATTN_HEADS_PER_STEP
    grid = (heads // hp, nseq, nq)
    shared = group > 1
    assert group % hp == 0 if shared else (kc0 % hp == 0 and vc0 % hp == 0)

    def body(q_ref, k_ref, v_ref, o_ref, lse_ref):
        for j in range(hp):
            cols = slice(j * HEAD_W, (j + 1) * HEAD_W)
            kj = (k_ref[...] if shared else k_ref[:, cols]).astype(BF16)
            vj = (v_ref[...] if shared else v_ref[:, cols]).astype(BF16)
            s = lax.dot_general(q_ref[:, cols], kj, (((1,), (1,)), ((), ())), preferred_element_type=F32)
            m = jnp.max(s, axis=-1, keepdims=True)
            p = jnp.exp(s - m)
            l = jnp.sum(p, axis=-1, keepdims=True)
            o = jnp.dot(p.astype(BF16), vj, preferred_element_type=F32)
            o_ref[:, cols] = (o * (1.0 / l)).astype(o_ref.dtype)
            lse_ref[j] = m + jnp.log(l)

    if rider is not None:
        rider.n_core_out, rider.n_core_scratch = 2, 0
    body, x_in, x_spec, x_out, x_scr = _ride(body, grid, rider)
    q_spec = pl.BlockSpec((tq, hp * HEAD_W), lambda h, b, i: (b * nq + i, h))
    if shared:
        k_spec = pl.BlockSpec((seq, HEAD_W), lambda h, b, i: (b, kc0 + (h * hp) // group))
        v_spec = pl.BlockSpec((seq, HEAD_W), lambda h, b, i: (b, vc0 + (h * hp) // group))
    else:
        k_spec = pl.BlockSpec((seq, hp * HEAD_W), lambda h, b, i: (b, kc0 // hp + h))
        v_spec = pl.BlockSpec((seq, hp * HEAD_W), lambda h, b, i: (b, vc0 // hp + h))
    lse_spec = pl.BlockSpec((hp, tq, 1), lambda h, b, i: (h, b * nq + i, 0))
    sem = ("parallel",) * 3 if rider is None else ("arbitrary",) * 3
    return pl.pallas_call(
        body, out_shape=[jax.ShapeDtypeStruct(q.shape, BF16), jax.ShapeDtypeStruct((heads, q.shape[0], 1), F32)] + x_out,
        grid=grid, in_specs=[q_spec, k_spec, v_spec] + x_spec, out_specs=[q_spec, lse_spec] + x_spec, scratch_shapes=x_scr,
        compiler_params=_params(sem), name=name)(q, k, v, *x_in)


def _attn_bwd(name, q, k, kc0, v, vc0, o, do, lse, *, heads, group, nseq, seq, tq=1024, rider=None):
    tq = min(tq, seq)
    nq = seq // tq
    hk = heads // group
    t = q.shape[0]
    grid = (hk, nseq, group, nq)

    def body(q_ref, k_ref, v_ref, o_ref, do_ref, lse_ref, dq_ref, dk_ref, dv_ref, dk_acc, dv_acc):
        g, i = pl.program_id(2), pl.program_id(3)
        qv, kv, vv, dov = q_ref[...], k_ref[...].astype(BF16), v_ref[...].astype(BF16), do_ref[...]
        s = lax.dot_general(qv, kv, (((1,), (1,)), ((), ())), preferred_element_type=F32)
        pn = jnp.exp(s - lse_ref[...])
        dp = lax.dot_general(dov, vv, (((1,), (1,)), ((), ())), preferred_element_type=F32)
        delta = jnp.sum(dov.astype(F32) * o_ref[...].astype(F32), axis=-1, keepdims=True)
        ds = (pn * (dp - delta)).astype(BF16)
        dq_ref[...] = jnp.dot(ds, kv, preferred_element_type=F32)
        dk_part = lax.dot_general(ds, qv, (((0,), (0,)), ((), ())), preferred_element_type=F32)
        dv_part = lax.dot_general(pn.astype(BF16), dov, (((0,), (0,)), ((), ())), preferred_element_type=F32)
        first = jnp.logical_and(g == 0, i == 0)

        @pl.when(first)
        def _():
            dk_acc[...] = dk_part
            dv_acc[...] = dv_part

        @pl.when(jnp.logical_not(first))
        def _():
            dk_acc[...] += dk_part
            dv_acc[...] += dv_part

        @pl.when(jnp.logical_and(g == group - 1, i == nq - 1))
        def _():
            dk_ref[...] = dk_acc[...].astype(dk_ref.dtype)
            dv_ref[...] = dv_acc[...].astype(dv_ref.dtype)

    if rider is not None:
        rider.n_core_out, rider.n_core_scratch = 3, 2
    body, x_in, x_spec, x_out, x_scr = _ride(body, grid, rider)
    q_spec = pl.BlockSpec((tq, HEAD_W), lambda kh, b, g, i: (b * nq + i, kh * group + g))
    kv_out = pl.BlockSpec((seq, HEAD_W), lambda kh, b, g, i: (b, kh))
    lse_spec = pl.BlockSpec((None, tq, 1), lambda kh, b, g, i: (kh * group + g, b * nq + i, 0))
    sem = ("parallel", "parallel", "arbitrary", "arbitrary") if rider is None else ("arbitrary",) * 4
    return pl.pallas_call(
        body,
        out_shape=[jax.ShapeDtypeStruct(q.shape, F32), jax.ShapeDtypeStruct((t, hk * HEAD_W), BF16),
                   jax.ShapeDtypeStruct((t, hk * HEAD_W), BF16)] + x_out,
        grid=grid,
        in_specs=[q_spec, pl.BlockSpec((seq, HEAD_W), lambda kh, b, g, i: (b, kc0 + kh)),
                  pl.BlockSpec((seq, HEAD_W), lambda kh, b, g, i: (b, vc0 + kh)), q_spec, q_spec, lse_spec] + x_spec,
        out_specs=[q_spec, kv_out, kv_out] + x_spec,
        scratch_shapes=[pltpu.VMEM((seq, HEAD_W), F32), pltpu.VMEM((seq, HEAD_W), F32)] + x_scr,
        compiler_params=_params(sem), name=name)(q, k, v, o, do, lse, *x_in)


def _place():
    return lax.axis_index("x"), lax.axis_index("y"), lax.axis_index("c")


def _other_chips(x, y):
    return [(1 - x, y), (x, 1 - y), (1 - x, 1 - y)]


class _Exchange:
    def __init__(self, kind, src):
        assert kind in ("gather", "scatter")
        self.kind, self.src = kind, src
        rows, w = src.shape[-2:]
        self.out_shape = jax.ShapeDtypeStruct((4, rows, w), src.dtype)
        self.scratch = [pltpu.SemaphoreType.DMA((3,)), pltpu.SemaphoreType.DMA((3,)), pltpu.SemaphoreType.DMA(())]
        self.n_core_out = self.n_core_scratch = 0

    def _copies(self, src_ref, out_ref, send_sems, recv_sems, landing):
        x, y, c = _place()

        def remote(k, s, d, to):
            return pltpu.make_async_remote_copy(src_ref=s, dst_ref=d, send_sem=send_sems.at[k], recv_sem=recv_sems.at[k],
                                                device_id=to, device_id_type=MESH)

        me = 2 * x + y
        part = (lambda j: src_ref) if self.kind == "gather" else (lambda j: src_ref.at[j])
        if landing:
            return [remote(k, part(me), out_ref.at[2 * px + py], (px, py, c)) for k, (px, py) in enumerate(_other_chips(x, y))]
        return [remote(k, part(2 * px + py), out_ref.at[me], (px, py, c)) for k, (px, py) in enumerate(_other_chips(x, y))]

    def _local(self, src_ref, out_ref, local_sem):
        x, y, _ = _place()
        me = 2 * x + y
        return pltpu.make_async_copy(src_ref if self.kind == "gather" else src_ref.at[me], out_ref.at[me], local_sem)

    def start(self, src_ref, out_ref, send_sems, recv_sems, local_sem):
        self._local(src_ref, out_ref, local_sem).start()
        for mine in self._copies(src_ref, out_ref, send_sems, recv_sems, False):
            mine.start()

    def finish(self, src_ref, out_ref, send_sems, recv_sems, local_sem):
        for landed in self._copies(src_ref, out_ref, send_sems, recv_sems, True):
            landed.wait_recv()
        for mine in self._copies(src_ref, out_ref, send_sems, recv_sems, False):
            mine.wait_send()
        self._local(src_ref, out_ref, local_sem).wait()


def _gather_by_halves(name, srcs):
    n = len(srcs)

    def body(*refs):
        x, y, c = _place()
        me = 2 * x + y
        local_sems = refs[-1]
        copies = []
        for j in range(n):
            src_ref, out_ref, send_sems, recv_sems = refs[j], refs[n + j], refs[2 * n + 2 * j], refs[2 * n + 2 * j + 1]
            half = srcs[j].shape[0] // 2
            rows_c = pl.ds(pl.multiple_of(c * half, half), half)
            rows_s = pl.ds(pl.multiple_of((1 - c) * half, half), half)

            def remote(k, s_ref, d_ref, to, send_sems=send_sems, recv_sems=recv_sems):
                return pltpu.make_async_remote_copy(src_ref=s_ref, dst_ref=d_ref, send_sem=send_sems.at[k], recv_sem=recv_sems.at[k],
                                                    device_id=to, device_id_type=MESH)

            local = pltpu.make_async_copy(src_ref, out_ref.at[me], local_sems.at[j])
            local.start()
            chips = _other_chips(x, y)
            sent = [remote(k, src_ref.at[rows_c], out_ref.at[me, rows_c], (px, py, c)) for k, (px, py) in enumerate(chips)]
            landing = [remote(k, src_ref.at[rows_c], out_ref.at[2 * px + py, rows_c], (px, py, c)) for k, (px, py) in enumerate(chips)]
            passed = [remote(3 + k, out_ref.at[2 * px + py, rows_c], out_ref.at[2 * px + py, rows_c], (x, y, 1 - c))
                      for k, (px, py) in enumerate(chips)]
            from_sibling = [remote(3 + k, out_ref.at[2 * px + py, rows_s], out_ref.at[2 * px + py, rows_s], (x, y, 1 - c))
                            for k, (px, py) in enumerate(chips)]
            for cp in sent:
                cp.start()
            copies.append((local, sent, landing, passed, from_sibling))
        for local, sent, landing, passed, from_sibling in copies:
            for k in range(3):
                landing[k].wait_recv()
                passed[k].start()
        for local, sent, landing, passed, from_sibling in copies:
            for k in range(3):
                from_sibling[k].wait_recv()
            for cp in sent + passed:
                cp.wait_send()
            local.wait()

    sems = [pltpu.SemaphoreType.DMA((6,)) for _ in range(2 * n)] + [pltpu.SemaphoreType.DMA((n,))]
    return pl.pallas_call(
        body, out_shape=[jax.ShapeDtypeStruct((4, *a.shape), a.dtype) for a in srcs],
        in_specs=[pl.BlockSpec(memory_space=pl.ANY)] * n, out_specs=[pl.BlockSpec(memory_space=pltpu.VMEM)] * n,
        scratch_shapes=sems, compiler_params=pltpu.CompilerParams(vmem_limit_bytes=VMEM_LIMIT_BYTES), name=name)(*srcs)


def _adamw(w, g, m, v):
    m = ADAM_B1 * m + (1.0 - ADAM_B1) * g
    v = ADAM_B2 * v + (1.0 - ADAM_B2) * (g * g)
    delta = -ADAM_LR * ((m / M_HAT_DIV) / (jnp.sqrt(v / V_HAT_DIV) + ADAM_EPS) + ADAM_WD * w)
    return delta, m, v


def _small_allreduce_adamw(part, w, m, v):
    def body(part_ref, w_ref, m_ref, v_ref, g_out, d_out, m_out, v_out, loss_out, buf, send_sems, recv_sems):
        x, y, c = _place()
        me = 4 * x + 2 * y + c
        buf[me] = part_ref[...]

        def flip(k):
            fx, fy, fc = (k >> 2) & 1, (k >> 1) & 1, k & 1
            px, py, pc = (1 - x if fx else x), (1 - y if fy else y), (1 - c if fc else c)
            return (px, py, pc), 4 * px + 2 * py + pc

        def copy(k, slot):
            return pltpu.make_async_remote_copy(
                src_ref=part_ref, dst_ref=buf.at[slot], send_sem=send_sems.at[k - 1], recv_sem=recv_sems.at[k - 1],
                device_id=flip(k)[0], device_id_type=MESH)

        sent = [copy(k, me) for k in range(1, 8)]
        for cp in sent:
            cp.start()
        for k in range(1, 8):
            copy(k, flip(k)[1]).wait_recv()
        for cp in sent:
            cp.wait_send()
        tot = buf[0]
        for j in range(1, 8):
            tot = tot + buf[j]
        delta, m_new, v_new = _adamw(w_ref[...], tot, m_ref[...], v_ref[...])
        g_out[...] = tot
        d_out[...] = delta
        m_out[...] = m_new
        v_out[...] = v_new
        loss_out[...] = jnp.sum(tot[LOSS_ROW0:LOSS_ROW0 + 8, :]).reshape(1, 1)

    vm = pl.BlockSpec(memory_space=pltpu.VMEM)
    shp = jax.ShapeDtypeStruct((SMALL_ROWS, 128), F32)
    return pl.pallas_call(
        body, out_shape=[shp, shp, shp, shp, jax.ShapeDtypeStruct((1, 1), F32)],
        in_specs=[vm, vm, vm, vm], out_specs=[vm, vm, vm, vm, vm],
        scratch_shapes=[pltpu.VMEM((8, SMALL_ROWS, 128), F32), pltpu.SemaphoreType.DMA((7,)), pltpu.SemaphoreType.DMA((7,))],
        name="small_allreduce_adamw")(part, w, m, v)


def _row_tile(rows, cap):
    return max(t for t in range(16, min(rows, cap) + 1, 16) if rows % t == 0)


def _reduce_pair(name, parts):
    _, rows, w = parts.shape
    tr = _row_tile(rows, 576)
    nt = rows // tr

    def body(p_ref, o_ref, mine, theirs, send_sems, recv_sems):
        i = pl.program_id(0)
        x, y, c = _place()

        def copy(t):
            rows_t = pl.ds(pl.multiple_of(t * tr, tr), tr)
            return pltpu.make_async_remote_copy(src_ref=mine.at[rows_t], dst_ref=theirs.at[rows_t], send_sem=send_sems.at[t],
                                                recv_sem=recv_sems.at[t], device_id=(x, y, 1 - c), device_id_type=MESH)

        @pl.when(i < nt)
        def _():
            mine[pl.ds(pl.multiple_of(i * tr, tr), tr), :] = (
                (p_ref[0].astype(F32) + p_ref[1].astype(F32)) + p_ref[2].astype(F32)) + p_ref[3].astype(F32)
            copy(i).start()

        @pl.when(i >= nt)
        def _():
            t = i - nt
            copy(t).wait()
            rows_t = pl.ds(pl.multiple_of(t * tr, tr), tr)
            o_ref[...] = mine[rows_t, :] + theirs[rows_t, :]

    return pl.pallas_call(
        body, out_shape=jax.ShapeDtypeStruct((rows, w), F32), grid=(2 * nt,),
        in_specs=[pl.BlockSpec((4, tr, w), lambda i: (0, jnp.minimum(i, nt - 1), 0))],
        out_specs=pl.BlockSpec((tr, w), lambda i: (jnp.maximum(i - nt, 0), 0)),
        scratch_shapes=[pltpu.VMEM((rows, w), F32), pltpu.VMEM((rows, w), F32), pltpu.SemaphoreType.DMA((nt,)),
                        pltpu.SemaphoreType.DMA((nt,))],
        compiler_params=_params(("arbitrary",)), name=name)(parts)


def _adamw_shard(name, g, w, m, v):
    rows, cols = w.shape
    tr = _row_tile(rows, 256)

    def body(g_ref, w_ref, m_ref, v_ref, d_out, m_out, v_out):
        delta, m_new, v_new = _adamw(w_ref[...], g_ref[...], m_ref[...], v_ref[...])
        d_out[...] = delta
        m_out[...] = m_new
        v_out[...] = v_new

    t_spec = pl.BlockSpec((tr, cols), lambda i: (i, 0))
    shp = jax.ShapeDtypeStruct((rows, cols), F32)
    return pl.pallas_call(body, out_shape=[shp] * 3, grid=(rows // tr,), in_specs=[t_spec] * 4, out_specs=[t_spec] * 3,
                          compiler_params=_params(("parallel",)), name=name)(g, w, m, v)


def _shard_shape(name):
    _, r, c, ax = BIG_BY_NAME[name]
    return (r, c // 4) if ax == 1 else (r // 4, c)


def _pad_rows(a, axis):
    pad = [(0, 0)] * a.ndim
    pad[axis] = (0, -a.shape[axis] % PACK_ALIGN)
    return jnp.pad(a, pad)


def _pack_shards(names, shards, dtype):
    return _pad_rows(jnp.concatenate([s.astype(dtype).reshape(-1, PACK_W) for s in shards], axis=0), 0)


def _unpack_shards(names, slab):
    out, off = [], 0
    for name in names:
        rs, cs = _shard_shape(name)
        n = rs * cs // PACK_W
        out.append(slab[off:off + n].reshape(rs, cs))
        off += n
    return out


def _unpack_full(names, slabs):
    out, off = [], 0
    for name in names:
        _, r, c, ax = BIG_BY_NAME[name]
        n = r * c // 4 // PACK_W
        seg = slabs[:, off:off + n]
        out.append(seg.reshape(4, r, c // 4).transpose(1, 0, 2).reshape(r, c) if ax == 1 else seg.reshape(r, c))
        off += n
    return out


def _pack_full(names, mats, dtype):
    segs = []
    for name, a in zip(names, mats):
        _, r, c, ax = BIG_BY_NAME[name]
        a = a.astype(dtype)
        a = a.reshape(r, 4, c // 4).transpose(1, 0, 2) if ax == 1 else a
        segs.append(a.reshape(4, -1, PACK_W))
    return _pad_rows(jnp.concatenate(segs, axis=1), 1)


def _pad_heads_cols(wm, heads, d):
    k = wm.shape[0]
    return jnp.pad(wm.reshape(k, heads, d), ((0, 0), (0, 0), (0, HEAD_W - d))).reshape(k, heads * HEAD_W)


def _unpad_heads_cols(wm, heads, d):
    k = wm.shape[0]
    return wm.reshape(k, heads, HEAD_W)[:, :, :d].reshape(k, heads * d)


def _win_ext(w_in):
    o = np.cumsum([0, Q_LORA, KV_LORA, QK_ROPE, H_B * HD_B, KV_B * HD_B, KV_B * HD_B, D_MODEL, D_MODEL])
    pc = lambda a, n: jnp.pad(a, ((0, 0), (0, n - a.shape[1])))
    return jnp.concatenate([
        _pad_heads_cols(w_in[:, o[3]:o[4]], H_B, HD_B), w_in[:, o[0]:o[1]], w_in[:, o[1]:o[2]], pc(w_in[:, o[2]:o[3]], HEAD_W),
        _pad_heads_cols(w_in[:, o[4]:o[5]], KV_B, HD_B), _pad_heads_cols(w_in[:, o[5]:o[6]], KV_B, HD_B),
        w_in[:, o[6]:o[7]], w_in[:, o[7]:o[8]]], axis=1)


def _win_unext(we):
    c = HEAD_W
    return jnp.concatenate([
        we[:, ZC_QLAT * c:ZC_CKV * c], we[:, ZC_CKV * c:ZC_KPE * c], we[:, ZC_KPE * c:ZC_KPE * c + QK_ROPE],
        _unpad_heads_cols(we[:, ZC_QB * c:ZC_QLAT * c], H_B, HD_B), _unpad_heads_cols(we[:, ZC_KB * c:ZC_VB * c], KV_B, HD_B),
        _unpad_heads_cols(we[:, ZC_VB * c:ZC_GA * c], KV_B, HD_B), we[:, ZC_GA * c:]], axis=1)


def _wkv_ext(w_kvb):
    wk = w_kvb.reshape(KV_LORA, H_A, QK_NOPE + V_DIM_A)
    k_cols = jnp.pad(wk[:, :, :QK_NOPE], ((0, 0), (0, 0), (0, HEAD_W - QK_NOPE))).reshape(KV_LORA, H_A * HEAD_W)
    v_cols = jnp.pad(wk[:, :, QK_NOPE:], ((0, 0), (0, 0), (0, HEAD_W - V_DIM_A))).reshape(KV_LORA, H_A * HEAD_W)
    eye = jnp.pad(jnp.eye(QK_ROPE, dtype=w_kvb.dtype), ((0, 0), (QK_NOPE, HEAD_W - QK_NOPE - QK_ROPE)))
    pe_rows = jnp.concatenate([jnp.tile(eye, (1, H_A)), jnp.zeros((QK_ROPE, H_A * HEAD_W), w_kvb.dtype)], axis=1)
    top = jnp.concatenate([k_cols, v_cols], axis=1)
    return jnp.concatenate([top, pe_rows, jnp.zeros((2 * HEAD_W - KV_LORA - QK_ROPE, 2 * H_A * HEAD_W), w_kvb.dtype)], axis=0)


def _wkv_unext(we):
    k_cols = we[:KV_LORA, :H_A * HEAD_W].reshape(KV_LORA, H_A, HEAD_W)[:, :, :QK_NOPE]
    v_cols = we[:KV_LORA, H_A * HEAD_W:].reshape(KV_LORA, H_A, HEAD_W)[:, :, :V_DIM_A]
    return jnp.concatenate([k_cols, v_cols], axis=2).reshape(KV_LORA, H_A * (QK_NOPE + V_DIM_A))


def _pad_heads_rows(wm, heads, d):
    n = wm.shape[1]
    return jnp.pad(wm.reshape(heads, d, n), ((0, 0), (0, HEAD_W - d), (0, 0))).reshape(heads * HEAD_W, n)


def _unpad_heads_rows(wm, heads, d):
    n = wm.shape[1]
    return wm.reshape(heads, HEAD_W, n)[:, :d].reshape(heads * d, n)


def _rope_tables(seq):
    def ang(pos, dim):
        inv = np.float32(ROPE_THETA) ** (-np.arange(0, dim, 2, dtype=np.float32) / np.float32(dim))
        return pos.astype(np.float32)[:, None] * inv[None, :]

    def rot(dim):
        r = np.zeros((dim, dim), np.float32)
        half = dim // 2
        r[np.arange(half) + half, np.arange(half)] = -1.0
        r[np.arange(half), np.arange(half) + half] = 1.0
        return r

    def table(blocks):
        cos, sin = np.ones((seq, HEAD_W), np.float32), np.zeros((seq, HEAD_W), np.float32)
        pm = np.zeros((HEAD_W, HEAD_W), np.float32)
        for c0, a in blocks:
            d = 2 * a.shape[1]
            cos[:, c0:c0 + d] = np.concatenate([np.cos(a), np.cos(a)], axis=1)
            sin[:, c0:c0 + d] = np.concatenate([np.sin(a), np.sin(a)], axis=1)
            pm[c0:c0 + d, c0:c0 + d] = rot(d)
        return jnp.asarray(cos), jnp.asarray(sin), jnp.asarray(pm, BF16), jnp.asarray(pm.T, BF16)

    tok = np.arange(seq)
    a1 = ang(tok, QK_ROPE)
    arow, acol = ang(tok // GRID_W, HD_B // 2), ang(tok % GRID_W, HD_B // 2)
    return table([(QK_NOPE, a1)]), table([(0, a1)]), table([(0, arow), (HD_B // 2, acol)])


def _local_step(x, p, tgt, gains, wts, ride=None):
    nb, seq, _ = x.shape
    t = nb * seq
    x0 = x.reshape(t, D_MODEL)
    p2 = p.reshape(t, PLE_DIM)
    tg = tgt.reshape(t, D_MODEL)
    (cq_t, sq_t, pq, pq_t), (ck_t, sk_t, pk, pk_t), (cb_t, sb_t, pb, pb_t) = _rope_tables(seq)
    padg = lambda g: jnp.pad(g, ((0, 0), (0, HEAD_W - g.shape[1])))
    g_qn, g_kn = padg(gains["g_qn"]), padg(gains["g_kn"])

    win = _win_ext(wts["w_in"])
    wqb = _pad_heads_cols(wts["w_qb"], H_A, QK_NOPE + QK_ROPE)
    wkv = _wkv_ext(wts["w_kvb"])

    norm = lambda n: (lambda v, g: (_rms(v, g, n),))
    full = lambda a: (a, a.shape[1], 0, False)

    h = _rowwise("norm_mix", norm(D_MODEL), [full(x0)], [(D_MODEL, BF16, D_MODEL, False)], consts=[gains["g_mix"]])
    z = _mm("in_proj", h, win, tn=2048)
    cq = _rowwise("norm_qa", norm(Q_LORA), [(z, Q_LORA, ZC_QLAT // 2, False)], [(Q_LORA, BF16, Q_LORA, False)],
                  consts=[gains["g_qa"]])
    qa = _mm("q_up", cq, wqb)

    def rope_fwd(scale):
        return lambda v, cos, sin, pm: ((v * cos + _perm(v, pm) * sin) * scale,)

    heads_tile = lambda n: (n * HEAD_W, BF16, n * HEAD_W, False)
    q_a = _rowwise("rope_qa", _per_head(rope_fwd(SCALE_A), H_A, 1, 1), [full(qa)], [heads_tile(H_A)],
                   pos=[cq_t, sq_t], consts=[pq], seq=seq)
    ckv = _rowwise("norm_kva", norm(KV_LORA), [(z, HEAD_W, ZC_CKV, False)], [(HEAD_W, BF16, HEAD_W, False)], consts=[gains["g_kva"]])
    kpe = _rowwise("rope_kpe", rope_fwd(1.0), [(z, HEAD_W, ZC_KPE, False)], [(HEAD_W, BF16, HEAD_W, False)],
                   pos=[ck_t, sk_t], consts=[pk], seq=seq)
    kin = jnp.concatenate([ckv, kpe], axis=1)
    kv_a = _mm("kv_up", kin, wkv, out_dtypes=(BF16,))
    o_a, lse_a, *got_a = _attn_fwd("attn_a_fwd", q_a, kv_a, 0, kv_a, H_A, heads=H_A, group=1, nseq=nb, seq=seq,
                                   rider=ride and ride["gather_a"])

    def prep_fwd(scale):
        def fn(v, cos, sin, g, pm):
            yv = _rms(v, g, HD_B)
            return ((yv * cos + _perm(yv, pm) * sin) * scale,)
        return fn

    z_qb, z_kb = (z, H_B * HEAD_W, ZC_QB // H_B, False), (z, KV_B * HEAD_W, ZC_KB // KV_B, False)
    q_b = _rowwise("prep_qb", _per_head(prep_fwd(SCALE_B), H_B, 1, 1), [z_qb], [heads_tile(H_B)],
                   pos=[cb_t, sb_t], consts=[g_qn, pb], seq=seq)
    k_b = _rowwise("prep_kb", _per_head(prep_fwd(1.0), KV_B, 1, 1), [z_kb], [heads_tile(KV_B)],
                   pos=[cb_t, sb_t], consts=[g_kn, pb], seq=seq)
    o_b, lse_b, *got_b = _attn_fwd("attn_b_fwd", q_b, k_b, 0, z, ZC_VB, heads=H_B, group=H_B // KV_B, nseq=nb, seq=seq,
                                   rider=ride and ride["gather_b"])
    if ride is not None:
        wts = {**wts, **ride["late_weights"](got_a[0], got_b[0])}
    woa = _pad_heads_rows(wts["w_oa"], H_A, V_DIM_A)
    wob = _pad_heads_rows(wts["w_ob"], H_B, HD_B)
    wo, wup, wdown, wpg, wple = wts["w_o"], wts["w_up"], wts["w_down"], wts["w_ple_gate"], wts["w_ple"]

    def residual_norm(acc, r, g):
        xv = r + acc
        return xv, _rms(xv, g, D_MODEL)

    def mix_out(oa, ob, ga, gb, r, g, w_a, w_b, w_out):
        a = jnp.dot(oa, w_a[...], preferred_element_type=F32)
        b = jnp.dot(ob, w_b[...], preferred_element_type=F32)
        mg = (_sigmoid(ga) * a + _sigmoid(gb) * b).astype(BF16)
        return (a, b, mg, *residual_norm(jnp.dot(mg, w_out[...], preferred_element_type=F32), r, g))

    z_ga, z_gb = (z, D_MODEL, ZC_GA // 8, False), (z, D_MODEL, ZC_GB // 8, False)
    wide = lambda d: (D_MODEL, d, D_MODEL, False)
    ya, yb, merged, x1, h2 = _rowwise("mix_out", mix_out, [full(o_a), full(o_b), z_ga, z_gb, full(x0)],
                                      [wide(F32), wide(F32), wide(BF16), wide(F32), wide(BF16)],
                                      consts=[gains["g_mlp"], woa, wob, wo], tm=256)

    def relu2(acc):
        u = jnp.maximum(acc, 0.0)
        return u, u * u

    u, usq = _mm("mlp_up", h2, wup, out_dtypes=(BF16, BF16), epi=relu2, tn=2048)
    x2, h3 = _mm("mlp_down", usq, wdown, out_dtypes=(F32, BF16), epi=residual_norm, extras=(x1,), consts=[gains["g_ple"]])

    def tail(x2v, h3v, pv, tv, gf, w_gate, w_emb):
        sg = _sigmoid(jnp.dot(h3v, w_gate[...], preferred_element_type=F32))
        pev = jnp.dot(pv.astype(BF16), w_emb[...], preferred_element_type=F32)
        x3 = x2v + sg * pev
        rs = lax.rsqrt(jnp.sum(x3 * x3, axis=-1, keepdims=True) * (1.0 / D_MODEL) + EPS)
        xh = x3 * rs
        err = xh * gf - tv
        dy = err * (1.0 / D_MODEL)
        dyg = dy * gf
        dx3 = rs * (dyg - xh * (jnp.sum(dyg * xh, axis=-1, keepdims=True) * (1.0 / D_MODEL)))
        return (dx3, dx3 * pev * sg * (1.0 - sg), dx3 * sg,
                jnp.sum(err * err, axis=0, keepdims=True) * (0.5 / D_MODEL), jnp.sum(dy * xh, axis=0, keepdims=True))

    dx3, dgpre, dpe, loss_part, dg_final = _rowwise(
        "tail", tail, [full(x2), full(h3), full(p2), full(tg)], [wide(F32), wide(BF16), wide(BF16)],
        consts=[gains["g_final"].reshape(1, D_MODEL), wpg, wple], accs=[(1, D_MODEL), (1, D_MODEL)], tm=256)

    def norm_bwd(n, with_res):
        if with_res:
            def fn(dh, v, res, g):
                dx, dg = _rms_bwd(dh, v, g, n)
                return dx + res, dg
        else:
            def fn(dh, v, g):
                return _rms_bwd(dh, v, g, n)
        return fn

    dw = {}
    dw["w_ple"] = _mm_tn("dw_ple", p2, dpe)
    dw["w_ple_gate"] = _mm_tn("dw_ple_gate", h3, dgpre)
    norm_res_bwd = norm_bwd(D_MODEL, True)
    dx2, dg_ple = _mm("d_ple_gate", dgpre, wpg, trans_b=True, epi=norm_res_bwd, extras=(x2, dx3), consts=[gains["g_ple"]],
                      accs=[(1, D_MODEL)], tm=256)
    dw["w_down"] = _mm_tn("dw_down", usq, dx2)
    dupre = _mm("d_mlp_down", dx2, wdown, trans_b=True, out_dtypes=(BF16,), epi=lambda acc, uv: (acc * (2.0 * uv.astype(F32)),),
                extras=(u,), tn=2048)
    dw["w_up"] = _mm_tn("dw_up", h2, dupre)
    dx1, dg_mlp = _mm("d_mlp_up", dupre, wup, trans_b=True, epi=norm_res_bwd, extras=(x1, dx2), consts=[gains["g_mlp"]],
                      accs=[(1, D_MODEL)], tm=256)
    dw["w_o"] = _mm_tn("dw_o", merged, dx1)

    def merge_bwd(dm, ga, gb, a, b, w_a, w_b):
        sa, sb = _sigmoid(ga), _sigmoid(gb)
        da, db = (dm * sa).astype(BF16), (dm * sb).astype(BF16)
        nt = (((1,), (1,)), ((), ()))
        return (da, db, dm * a * sa * (1.0 - sa), dm * b * sb * (1.0 - sb),
                lax.dot_general(da, w_a, nt, preferred_element_type=F32), lax.dot_general(db, w_b, nt, preferred_element_type=F32))

    dya, dyb, dga, dgb, do_a, do_b = _mm("d_out_proj", dx1, wo, trans_b=True, out_dtypes=(BF16,) * 6, epi=merge_bwd,
                                         extras=((z, ZC_GA // 8), (z, ZC_GB // 8), ya, yb), consts=[woa, wob], tm=256)
    dw["w_oa"] = _unpad_heads_rows(_mm_tn("dw_oa", o_a, dya), H_A, V_DIM_A)
    dw["w_ob"] = _unpad_heads_rows(_mm_tn("dw_ob", o_b, dyb), H_B, HD_B)
    res_a = _attn_bwd("attn_a_bwd", q_a, kv_a, 0, kv_a, H_A, o_a, do_a, lse_a, heads=H_A, group=1, nseq=nb, seq=seq,
                      rider=ride and ride["scatter_a"](dw))
    dq_a, dk_a, dv_a = res_a[:3]
    if ride is not None:
        ride["out"]["parts_a"] = res_a[3]

    def rope_bwd(scale):
        return lambda d, cos, sin, pm_t: ((d * cos + _perm(d * sin, pm_t)) * scale,)

    dqa = _rowwise("rope_qa_bwd", _per_head(rope_bwd(SCALE_A), H_A, 1, 1), [full(dq_a)], [heads_tile(H_A)],
                   pos=[cq_t, sq_t], consts=[pq_t], seq=seq)
    dw["w_qb"] = _unpad_heads_cols(_mm_tn("dw_qb", cq, dqa), H_A, QK_NOPE + QK_ROPE)
    dcq = _mm("d_q_up", dqa, wqb, trans_b=True)
    dq_lat, dg_qa = _rowwise("norm_qa_bwd", norm_bwd(Q_LORA, False), [full(dcq), (z, Q_LORA, ZC_QLAT // 2, False)],
                             [(Q_LORA, BF16, Q_LORA, False)], consts=[gains["g_qa"]], accs=[(1, Q_LORA)])
    dkv_a = jnp.concatenate([dk_a, dv_a], axis=1)
    dw["w_kvb"] = _wkv_unext(_mm_tn("dw_kv", kin, dkv_a))
    dq_b, dk_b, dv_b, *parts_b = _attn_bwd("attn_b_bwd", q_b, k_b, 0, z, ZC_VB, o_b, do_b, lse_b, heads=H_B, group=H_B // KV_B,
                                               nseq=nb, seq=seq, rider=ride and ride["scatter_b"](dw))
    if ride is not None:
        ride["out"]["parts_b"] = parts_b[0]
    dkin = _mm("d_kv_up", dkv_a, wkv, trans_b=True)
    dckv, dg_kva = _rowwise("norm_kva_bwd", norm_bwd(KV_LORA, False), [(dkin, HEAD_W, 0, False), (z, HEAD_W, ZC_CKV, False)],
                            [(HEAD_W, BF16, HEAD_W, False)], consts=[gains["g_kva"]], accs=[(1, KV_LORA)])
    dkpe = _rowwise("rope_kpe_bwd", rope_bwd(1.0), [(dkin, HEAD_W, 1, False)], [(HEAD_W, BF16, HEAD_W, False)],
                    pos=[ck_t, sk_t], consts=[pk_t], seq=seq)

    def prep_bwd(scale):
        def fn(d, v, cos, sin, g, pm_t):
            dyv = (d * cos + _perm(d * sin, pm_t)) * scale
            return _rms_bwd(dyv, v, g, HD_B)
        return fn

    dqb, dg_qn = _rowwise("prep_qb_bwd", _per_head(prep_bwd(SCALE_B), H_B, 2, 1), [full(dq_b), z_qb], [heads_tile(H_B)],
                          pos=[cb_t, sb_t], consts=[g_qn, pb_t], accs=[(1, HEAD_W)], seq=seq)
    dkb, dg_kn = _rowwise("prep_kb_bwd", _per_head(prep_bwd(1.0), KV_B, 2, 1), [full(dk_b), z_kb], [heads_tile(KV_B)],
                          pos=[cb_t, sb_t], consts=[g_kn, pb_t], accs=[(1, HEAD_W)], seq=seq)

    dz = jnp.concatenate([dqb, dq_lat, dckv, dkpe, dkb, dv_b, dga, dgb], axis=1)
    dw["w_in"] = _win_unext(_mm_tn("dw_in", h, dz))
    dx0, dg_mix, *parts_in = _mm("d_in_proj", dz, win, trans_b=True, epi=norm_res_bwd, extras=(x0, dx1), consts=[gains["g_mix"]],
                                 accs=[(1, D_MODEL)], tm=256, rider=ride and ride["scatter_in"](dw))
    if ride is not None:
        ride["out"]["parts_in"] = parts_in[0]

    dg = {"g_mix": dg_mix, "g_qa": dg_qa, "g_kva": dg_kva, "g_qn": dg_qn[:, :HD_B], "g_kn": dg_kn[:, :HD_B],
          "g_mlp": dg_mlp, "g_ple": dg_ple, "g_final": dg_final}
    return loss_part, dx0.reshape(nb, seq, D_MODEL), dg, dw


def _pack_small(vals, loss_part=None):
    flat = jnp.concatenate([vals[n].reshape(1, -1) for n, _ in SMALL], axis=1)
    loss = jnp.zeros((1, 8 * 128), F32) if loss_part is None else loss_part
    gap = jnp.zeros((1, LOSS_ROW0 * 128 - SMALL_N), F32)
    return jnp.concatenate([flat, gap, loss], axis=1).reshape(SMALL_ROWS, 128)


def _unpack_small(slab, like):
    flat, out, off = slab.reshape(-1), {}, 0
    for n, k in SMALL:
        out[n] = flat[off:off + k].reshape(like[n].shape)
        off += k
    return out


def kernel(x, p, g_mix, w_in, g_qa, w_qb, g_kva, w_kvb, g_qn, g_kn, w_oa, w_ob, w_o, g_mlp, w_up, w_down, g_ple, w_ple_gate, w_ple, g_final, loss_target, m_g_mix, m_w_in, m_g_qa, m_w_qb, m_g_kva, m_w_kvb, m_g_qn, m_g_kn, m_w_oa, m_w_ob, m_w_o, m_g_mlp, m_w_up, m_w_down, m_g_ple, m_w_ple_gate, m_w_ple, m_g_final, v_g_mix, v_w_in, v_g_qa, v_w_qb, v_g_kva, v_w_kvb, v_g_qn, v_g_kn, v_w_oa, v_w_ob, v_w_o, v_g_mlp, v_w_up, v_w_down, v_g_ple, v_w_ple_gate, v_w_ple, v_g_final):
    given = dict(locals())
    order = ["g_mix", "w_in", "g_qa", "w_qb", "g_kva", "w_kvb", "g_qn", "g_kn", "w_oa", "w_ob", "w_o", "g_mlp", "w_up",
             "w_down", "g_ple", "w_ple_gate", "w_ple", "g_final"]
    big_names = [n for n, _, _, _ in BIG]
    local = lambda prefix, names: [given[prefix + n][0] for n in names]
    group_b = LATE_B + EARLY_SMALL

    shards_cols = lambda a: a.reshape(a.shape[0], 4, a.shape[1] // 4).transpose(1, 0, 2)
    got_in, got_small = _gather_by_halves("weight_gather_early",
                                          [w_in[0].astype(BF16), _pack_shards(EARLY_SMALL, local("", EARLY_SMALL), BF16)])
    wts = {"w_in": got_in.transpose(1, 0, 2).reshape(got_in.shape[1], -1), **dict(zip(EARLY_SMALL, _unpack_full(EARLY_SMALL, got_small)))}
    gains = {n: given[n].reshape(1, -1) for n, _ in SMALL}
    ride = {
        "gather_a": _Exchange("gather", _pack_shards(LATE_A, local("", LATE_A), BF16)),
        "gather_b": _Exchange("gather", _pack_shards(LATE_B, local("", LATE_B), BF16)),
        "late_weights": lambda ga, gb: {**dict(zip(LATE_A, _unpack_full(LATE_A, ga))), **dict(zip(LATE_B, _unpack_full(LATE_B, gb)))},
        "scatter_a": lambda dw: _Exchange("scatter", _pack_full(LATE_A, [dw[n] for n in LATE_A], BF16)),
        "scatter_b": lambda dw: _Exchange("scatter", _pack_full(group_b, [dw[n] for n in group_b], BF16)),
        "scatter_in": lambda dw: _Exchange("scatter", shards_cols(dw["w_in"].astype(BF16))),
        "out": {},
    }
    loss_part, grad_x, dg, dw = _local_step(x, p[0], loss_target, gains, wts, ride)

    small = lambda prefix: _pack_small({n: given[prefix + n] for n, _ in SMALL})
    g_s, d_s, m_s, v_s, loss = _small_allreduce_adamw(_pack_small(dg, loss_part), small(""), small("m_"), small("v_"))

    grads = dict(zip(LATE_A, _unpack_shards(LATE_A, _reduce_pair("grad_reduce_a", ride["out"]["parts_a"]))))
    grads.update(zip(group_b, _unpack_shards(group_b, _reduce_pair("grad_reduce_b", ride["out"]["parts_b"]))))
    grads["w_in"] = _reduce_pair("grad_reduce_in", ride["out"]["parts_in"])

    res = {}
    for key, slab in (("grad_", g_s), ("delta_", d_s), ("new_m_", m_s), ("new_v_", v_s)):
        for n, val in _unpack_small(slab, given).items():
            res[key + n] = val
    for n in big_names:
        d_w, m_w, v_w = _adamw_shard("adamw_" + n, grads[n], given[n][0], given["m_" + n][0], given["v_" + n][0])
        res["grad_" + n], res["delta_" + n], res["new_m_" + n], res["new_v_" + n] = grads[n][None], d_w[None], m_w[None], v_w[None]
    outs = [loss.reshape(()), grad_x]
    for key in ("grad_", "delta_", "new_m_", "new_v_"):
        outs += [res[key + n] for n in order]
    return tuple(outs)
```

```python
import functools

import numpy as np
import jax
import jax.numpy as jnp
from jax import lax
from jax.experimental import pallas as pl
from jax.experimental.pallas import tpu as pltpu

F32 = jnp.float32
BF16 = jnp.bfloat16
MESH = pl.DeviceIdType.MESH

D_MODEL = 1024
GRID_W = 64
ROPE_THETA = 10000.0
EPS = 1e-6
H_A, QK_NOPE, QK_ROPE, V_DIM_A, Q_LORA, KV_LORA = 8, 64, 32, 64, 256, 128
H_B, KV_B, HD_B = 8, 2, 64
D_FF = 4096
PLE_DIM = 256
HEAD_W = 128
SCALE_A = (QK_NOPE + QK_ROPE) ** -0.5
SCALE_B = HD_B ** -0.5

ADAM_LR, ADAM_B1, ADAM_B2, ADAM_EPS, ADAM_WD, ADAM_STEP = 0.001, 0.9, 0.999, 1e-08, 0.01, 10
M_HAT_DIV = 1.0 - ADAM_B1 ** ADAM_STEP
V_HAT_DIV = 1.0 - ADAM_B2 ** ADAM_STEP

VMEM_LIMIT_BYTES = 56 * 1024 * 1024

ZC_QB, ZC_QLAT, ZC_CKV, ZC_KPE, ZC_KB, ZC_VB, ZC_GA, ZC_GB = 0, 8, 10, 11, 12, 14, 16, 24
Z_WIDTH = 32 * HEAD_W

BIG = [
    ("w_in", 1024, 3232, 1), ("w_qb", 256, 768, 1), ("w_kvb", 128, 1024, 1), ("w_oa", 512, 1024, 1),
    ("w_ob", 512, 1024, 1), ("w_o", 1024, 1024, 0), ("w_up", 1024, 4096, 1), ("w_down", 4096, 1024, 0),
    ("w_ple_gate", 1024, 1024, 0), ("w_ple", 256, 1024, 1),
]
BIG_BY_NAME = {e[0]: e for e in BIG}
PACK_W = 1024
PACK_ALIGN = 64
EARLY_SMALL = ["w_qb", "w_kvb"]
LATE_A = ["w_oa", "w_ob", "w_o", "w_up"]
LATE_B = ["w_down", "w_ple_gate", "w_ple"]

SMALL = [("g_mix", 1024), ("g_qa", 256), ("g_kva", 128), ("g_qn", 64), ("g_kn", 64), ("g_mlp", 1024),
         ("g_ple", 1024), ("g_final", 1024)]
SMALL_N = sum(n for _, n in SMALL)
LOSS_ROW0 = 40
SMALL_ROWS = 48


def _params(sem):
    return pltpu.CompilerParams(dimension_semantics=sem, vmem_limit_bytes=VMEM_LIMIT_BYTES)


def _sigmoid(v):
    return 1.0 / (1.0 + jnp.exp(-v))


def _perm(v, p_ref):
    pm = p_ref[...]
    hi = v.astype(BF16)
    lo = (v - hi.astype(F32)).astype(BF16)
    return (jnp.dot(hi, pm, preferred_element_type=F32) + jnp.dot(lo, pm, preferred_element_type=F32))


def _rms(v, g, n):
    rs = lax.rsqrt(jnp.sum(v * v, axis=-1, keepdims=True) * (1.0 / n) + EPS)
    return v * rs * g


def _rms_bwd(dy, v, g, n):
    rs = lax.rsqrt(jnp.sum(v * v, axis=-1, keepdims=True) * (1.0 / n) + EPS)
    vh = v * rs
    dyg = dy * g
    dx = rs * (dyg - vh * (jnp.sum(dyg * vh, axis=-1, keepdims=True) * (1.0 / n)))
    return dx, jnp.sum(dy * vh, axis=0, keepdims=True)


def _ride(body, grid, rider):
    if rider is None:
        return body, [], [], [], []
    n_sem = len(rider.scratch)

    def wrapped(*refs):
        ids = [pl.program_id(a) for a in range(len(grid))]
        n_in = len(refs) - n_sem - 2 - rider.n_core_out - rider.n_core_scratch
        core_in, src = refs[:n_in], refs[n_in]
        core_out = refs[n_in + 1:n_in + 1 + rider.n_core_out]
        dst = refs[n_in + 1 + rider.n_core_out]
        core_scr = refs[n_in + 2 + rider.n_core_out:len(refs) - n_sem]
        sems = refs[len(refs) - n_sem:]

        @pl.when(functools.reduce(jnp.logical_and, [a == 0 for a in ids]))
        def _():
            rider.start(src, dst, *sems)

        body(*core_in, *core_out, *core_scr)

        @pl.when(functools.reduce(jnp.logical_and, [a == n - 1 for a, n in zip(ids, grid)]))
        def _():
            rider.finish(src, dst, *sems)

    hbm = pl.BlockSpec(memory_space=pl.ANY)
    return wrapped, [rider.src], [hbm], [rider.out_shape], list(rider.scratch)


def _mm(name, a, b, *, trans_b=False, out_dtypes=(F32,), epi=None, extras=(), consts=(), accs=(), tm=512, tn=None, rider=None):
    m, k = a.shape
    n = b.shape[0] if trans_b else b.shape[1]
    tn = n if tn is None else min(tn, n)
    tm = min(tm, m)
    assert m % tm == 0 and n % tn == 0 and (b.shape[1] if trans_b else b.shape[0]) == k
    extras = [e if isinstance(e, tuple) else (e, 0) for e in extras]
    n_ex, n_c, n_out, n_acc = len(extras), len(consts), len(out_dtypes), len(accs)
    dims = (((1,), (1,)), ((), ())) if trans_b else (((1,), (0,)), ((), ()))

    def body(a_ref, b_ref, *rest):
        acc = lax.dot_general(a_ref[...].astype(BF16), b_ref[...].astype(BF16), dims, preferred_element_type=F32)
        res = (acc,) if epi is None else epi(acc, *[e[...] for e in rest[:n_ex + n_c]])
        o_refs = rest[n_ex + n_c:]
        for o_ref, r in zip(o_refs[:n_out], res[:n_out]):
            o_ref[...] = r.astype(o_ref.dtype)
        if n_acc:
            first = jnp.logical_and(pl.program_id(0) == 0, pl.program_id(1) == 0)

            @pl.when(first)
            def _():
                for o_ref, r in zip(o_refs[n_out:], res[n_out:]):
                    o_ref[...] = r

            @pl.when(jnp.logical_not(first))
            def _():
                for o_ref, r in zip(o_refs[n_out:], res[n_out:]):
                    o_ref[...] += r

    grid = (n // tn, m // tm)
    if rider is not None:
        rider.n_core_out, rider.n_core_scratch = n_out + n_acc, 0
    body, x_in, x_spec, x_out, x_scr = _ride(body, grid, rider)
    a_spec = pl.BlockSpec((tm, k), lambda j, i: (i, 0))
    b_spec = pl.BlockSpec((tn, k), lambda j, i: (j, 0)) if trans_b else pl.BlockSpec((k, tn), lambda j, i: (0, j))
    t_spec = pl.BlockSpec((tm, tn), lambda j, i: (i, j))
    e_specs = [pl.BlockSpec((tm, tn), lambda j, i, off=off: (i, j + off)) for _, off in extras]
    c_specs = [pl.BlockSpec(c.shape, lambda j, i: (0, 0)) for c in consts]
    acc_specs = [pl.BlockSpec(sh, lambda j, i: (0, 0)) for sh in accs]
    sem = ("parallel", "parallel") if rider is None and not n_acc else ("arbitrary", "arbitrary")
    outs = pl.pallas_call(
        body, out_shape=[jax.ShapeDtypeStruct((m, n), d) for d in out_dtypes] + [jax.ShapeDtypeStruct(sh, F32) for sh in accs] + x_out,
        grid=grid, in_specs=[a_spec, b_spec] + e_specs + c_specs + x_spec, out_specs=[t_spec] * n_out + acc_specs + x_spec,
        scratch_shapes=x_scr, compiler_params=_params(sem), name=name)(a, b, *[e for e, _ in extras], *consts, *x_in)
    return outs[0] if len(outs) == 1 else outs


def _per_head(fn, heads, n_tiled, n_out):
    def run(*args):
        res = [fn(*[a[:, hd * HEAD_W:(hd + 1) * HEAD_W] for a in args[:n_tiled]], *args[n_tiled:]) for hd in range(heads)]
        tiles = [jnp.concatenate([r[k] for r in res], axis=1) for k in range(n_out)]
        sums = [functools.reduce(lambda u, v: u + v, [r[k] for r in res]) for k in range(n_out, len(res[0]))]
        return (*tiles, *sums)
    return run


def _mm_tn(name, a, b, *, out_dtype=BF16, tk=1024, tn=1024, tt=1024):
    t, k = a.shape
    n = b.shape[1]
    tk, tn, tt = min(tk, k), min(tn, n), min(tt, t)
    assert b.shape[0] == t and k % tk == 0 and n % tn == 0 and t % tt == 0
    nt = t // tt

    def body(a_ref, b_ref, o_ref, acc):
        part = lax.dot_general(a_ref[...].astype(BF16), b_ref[...].astype(BF16), (((0,), (0,)), ((), ())),
                               preferred_element_type=F32)

        @pl.when(pl.program_id(2) == 0)
        def _():
            acc[...] = part

        @pl.when(pl.program_id(2) != 0)
        def _():
            acc[...] += part

        @pl.when(pl.program_id(2) == nt - 1)
        def _():
            o_ref[...] = acc[...].astype(o_ref.dtype)

    return pl.pallas_call(
        body, out_shape=jax.ShapeDtypeStruct((k, n), out_dtype), grid=(k // tk, n // tn, nt),
        in_specs=[pl.BlockSpec((tt, tk), lambda ki, ni, ti: (ti, ki)), pl.BlockSpec((tt, tn), lambda ki, ni, ti: (ti, ni))],
        out_specs=pl.BlockSpec((tk, tn), lambda ki, ni, ti: (ki, ni)), scratch_shapes=[pltpu.VMEM((tk, tn), F32)],
        compiler_params=_params(("parallel", "parallel", "arbitrary")), name=name)(a, b)


def _rowwise(name, fn, ins, outs, *, consts=(), pos=(), accs=(), heads=1, tm=512, seq=None):
    t = ins[0][0].shape[0]
    tm = min(tm, t if seq is None else seq)
    assert t % tm == 0 and (seq is None or seq % tm == 0)
    n_in, n_pos, n_c, n_out, n_acc = len(ins), len(pos), len(consts), len(outs), len(accs)

    def body(*refs):
        vals = [r[...] for r in refs[:n_in + n_pos + n_c]]
        res = fn(*vals)
        o_refs = refs[n_in + n_pos + n_c:]
        for o_ref, r in zip(o_refs[:n_out], res[:n_out]):
            o_ref[...] = r.astype(o_ref.dtype)
        if n_acc:
            first = jnp.logical_and(pl.program_id(0) == 0, pl.program_id(1) == 0)

            @pl.when(first)
            def _():
                for o_ref, r in zip(o_refs[n_out:], res[n_out:]):
                    o_ref[...] = r

            @pl.when(jnp.logical_not(first))
            def _():
                for o_ref, r in zip(o_refs[n_out:], res[n_out:]):
                    o_ref[...] += r

    def tiled(width, c0, per_head):
        return pl.BlockSpec((tm, width), (lambda h, i: (i, c0 + h)) if per_head else (lambda h, i: (i, c0)))

    in_specs = [tiled(w, c0, ph) for _, w, c0, ph in ins]
    if n_pos:
        nblk = seq // tm
        in_specs += [pl.BlockSpec((tm, a.shape[1]), lambda h, i: (i % nblk, 0)) for a in pos]
    in_specs += [pl.BlockSpec(a.shape, lambda h, i: (0, 0)) for a in consts]
    out_specs = [tiled(w, 0, ph) for _, _, w, ph in outs] + [pl.BlockSpec(s, lambda h, i: (0, 0)) for s in accs]
    out_shape = [jax.ShapeDtypeStruct((t, c), d) for c, d, _, _ in outs] + [jax.ShapeDtypeStruct(s, F32) for s in accs]
    sem = ("arbitrary", "arbitrary") if n_acc else ("parallel", "parallel")
    res = pl.pallas_call(body, out_shape=out_shape, grid=(heads, t // tm), in_specs=in_specs, out_specs=out_specs,
                         compiler_params=_params(sem), name=name)(*[a for a, _, _, _ in ins], *pos, *consts)
    return res[0] if len(res) == 1 else res


ATTN_HEADS_PER_STEP = 4


def _attn_fwd(name, q, k, kc0, v, vc0, *, heads, group, nseq, seq, tq=512, rider=None):
    tq = min(tq, seq)
    nq = seq // tq
    hp = ATTN_HEADS_PER_STEP
    grid = (heads // hp, nseq, nq)
    shared = group > 1
    assert group % hp == 0 if shared else (kc0 % hp == 0 and vc0 % hp == 0)

    def body(q_ref, k_ref, v_ref, o_ref, lse_ref):
        for j in range(hp):
            cols = slice(j * HEAD_W, (j + 1) * HEAD_W)
            kj = (k_ref[...] if shared else k_ref[:, cols]).astype(BF16)
            vj = (v_ref[...] if shared else v_ref[:, cols]).astype(BF16)
            s = lax.dot_general(q_ref[:, cols], kj, (((1,), (1,)), ((), ())), preferred_element_type=F32)
            m = jnp.max(s, axis=-1, keepdims=True)
            p = jnp.exp(s - m)
            l = jnp.sum(p, axis=-1, keepdims=True)
            o = jnp.dot(p.astype(BF16), vj, preferred_element_type=F32)
            o_ref[:, cols] = (o * (1.0 / l)).astype(o_ref.dtype)
            lse_ref[j] = m + jnp.log(l)

    if rider is not None:
        rider.n_core_out, rider.n_core_scratch = 2, 0
    body, x_in, x_spec, x_out, x_scr = _ride(body, grid, rider)
    q_spec = pl.BlockSpec((tq, hp * HEAD_W), lambda h, b, i: (b * nq + i, h))
    if shared:
        k_spec = pl.BlockSpec((seq, HEAD_W), lambda h, b, i: (b, kc0 + (h * hp) // group))
        v_spec = pl.BlockSpec((seq, HEAD_W), lambda h, b, i: (b, vc0 + (h * hp) // group))
    else:
        k_spec = pl.BlockSpec((seq, hp * HEAD_W), lambda h, b, i: (b, kc0 // hp + h))
        v_spec = pl.BlockSpec((seq, hp * HEAD_W), lambda h, b, i: (b, vc0 // hp + h))
    lse_spec = pl.BlockSpec((hp, tq, 1), lambda h, b, i: (h, b * nq + i, 0))
    sem = ("parallel",) * 3 if rider is None else ("arbitrary",) * 3
    return pl.pallas_call(
        body, out_shape=[jax.ShapeDtypeStruct(q.shape, BF16), jax.ShapeDtypeStruct((heads, q.shape[0], 1), F32)] + x_out,
        grid=grid, in_specs=[q_spec, k_spec, v_spec] + x_spec, out_specs=[q_spec, lse_spec] + x_spec, scratch_shapes=x_scr,
        compiler_params=_params(sem), name=name)(q, k, v, *x_in)


def _attn_bwd(name, q, k, kc0, v, vc0, o, do, lse, *, heads, group, nseq, seq, tq=1024, rider=None):
    tq = min(tq, seq)
    nq = seq // tq
    hk = heads // group
    t = q.shape[0]
    grid = (hk, nseq, group, nq)

    def body(q_ref, k_ref, v_ref, o_ref, do_ref, lse_ref, dq_ref, dk_ref, dv_ref, dk_acc, dv_acc):
        g, i = pl.program_id(2), pl.program_id(3)
        qv, kv, vv, dov = q_ref[...], k_ref[...].astype(BF16), v_ref[...].astype(BF16), do_ref[...]
        s = lax.dot_general(qv, kv, (((1,), (1,)), ((), ())), preferred_element_type=F32)
        pn = jnp.exp(s - lse_ref[...])
        dp = lax.dot_general(dov, vv, (((1,), (1,)), ((), ())), preferred_element_type=F32)
        delta = jnp.sum(dov.astype(F32) * o_ref[...].astype(F32), axis=-1, keepdims=True)
        ds = (pn * (dp - delta)).astype(BF16)
        dq_ref[...] = jnp.dot(ds, kv, preferred_element_type=F32)
        dk_part = lax.dot_general(ds, qv, (((0,), (0,)), ((), ())), preferred_element_type=F32)
        dv_part = lax.dot_general(pn.astype(BF16), dov, (((0,), (0,)), ((), ())), preferred_element_type=F32)
        first = jnp.logical_and(g == 0, i == 0)

        @pl.when(first)
        def _():
            dk_acc[...] = dk_part
            dv_acc[...] = dv_part

        @pl.when(jnp.logical_not(first))
        def _():
            dk_acc[...] += dk_part
            dv_acc[...] += dv_part

        @pl.when(jnp.logical_and(g == group - 1, i == nq - 1))
        def _():
            dk_ref[...] = dk_acc[...].astype(dk_ref.dtype)
            dv_ref[...] = dv_acc[...].astype(dv_ref.dtype)

    if rider is not None:
        rider.n_core_out, rider.n_core_scratch = 3, 2
    body, x_in, x_spec, x_out, x_scr = _ride(body, grid, rider)
    q_spec = pl.BlockSpec((tq, HEAD_W), lambda kh, b, g, i: (b * nq + i, kh * group + g))
    kv_out = pl.BlockSpec((seq, HEAD_W), lambda kh, b, g, i: (b, kh))
    lse_spec = pl.BlockSpec((None, tq, 1), lambda kh, b, g, i: (kh * group + g, b * nq + i, 0))
    sem = ("parallel", "parallel", "arbitrary", "arbitrary") if rider is None else ("arbitrary",) * 4
    return pl.pallas_call(
        body,
        out_shape=[jax.ShapeDtypeStruct(q.shape, F32), jax.ShapeDtypeStruct((t, hk * HEAD_W), BF16),
                   jax.ShapeDtypeStruct((t, hk * HEAD_W), BF16)] + x_out,
        grid=grid,
        in_specs=[q_spec, pl.BlockSpec((seq, HEAD_W), lambda kh, b, g, i: (b, kc0 + kh)),
                  pl.BlockSpec((seq, HEAD_W), lambda kh, b, g, i: (b, vc0 + kh)), q_spec, q_spec, lse_spec] + x_spec,
        out_specs=[q_spec, kv_out, kv_out] + x_spec,
        scratch_shapes=[pltpu.VMEM((seq, HEAD_W), F32), pltpu.VMEM((seq, HEAD_W), F32)] + x_scr,
        compiler_params=_params(sem), name=name)(q, k, v, o, do, lse, *x_in)


def _place():
    return lax.axis_index("x"), lax.axis_index("y"), lax.axis_index("c")


def _other_chips(x, y):
    return [(1 - x, y), (x, 1 - y), (1 - x, 1 - y)]


class _Exchange:
    def __init__(self, kind, src):
        assert kind in ("gather", "scatter")
        self.kind, self.src = kind, src
        rows, w = src.shape[-2:]
        self.out_shape = jax.ShapeDtypeStruct((4, rows, w), src.dtype)
        self.scratch = [pltpu.SemaphoreType.DMA((3,)), pltpu.SemaphoreType.DMA((3,)), pltpu.SemaphoreType.DMA(())]
        self.n_core_out = self.n_core_scratch = 0

    def _copies(self, src_ref, out_ref, send_sems, recv_sems, landing):
        x, y, c = _place()

        def remote(k, s, d, to):
            return pltpu.make_async_remote_copy(src_ref=s, dst_ref=d, send_sem=send_sems.at[k], recv_sem=recv_sems.at[k],
                                                device_id=to, device_id_type=MESH)

        me = 2 * x + y
        part = (lambda j: src_ref) if self.kind == "gather" else (lambda j: src_ref.at[j])
        if landing:
            return [remote(k, part(me), out_ref.at[2 * px + py], (px, py, c)) for k, (px, py) in enumerate(_other_chips(x, y))]
        return [remote(k, part(2 * px + py), out_ref.at[me], (px, py, c)) for k, (px, py) in enumerate(_other_chips(x, y))]

    def _local(self, src_ref, out_ref, local_sem):
        x, y, _ = _place()
        me = 2 * x + y
        return pltpu.make_async_copy(src_ref if self.kind == "gather" else src_ref.at[me], out_ref.at[me], local_sem)

    def start(self, src_ref, out_ref, send_sems, recv_sems, local_sem):
        self._local(src_ref, out_ref, local_sem).start()
        for mine in self._copies(src_ref, out_ref, send_sems, recv_sems, False):
            mine.start()

    def finish(self, src_ref, out_ref, send_sems, recv_sems, local_sem):
        for landed in self._copies(src_ref, out_ref, send_sems, recv_sems, True):
            landed.wait_recv()
        for mine in self._copies(src_ref, out_ref, send_sems, recv_sems, False):
            mine.wait_send()
        self._local(src_ref, out_ref, local_sem).wait()


def _gather_by_halves(name, srcs):
    n = len(srcs)

    def body(*refs):
        x, y, c = _place()
        me = 2 * x + y
        local_sems = refs[-1]
        copies = []
        for j in range(n):
            src_ref, out_ref, send_sems, recv_sems = refs[j], refs[n + j], refs[2 * n + 2 * j], refs[2 * n + 2 * j + 1]
            half = srcs[j].shape[0] // 2
            rows_c = pl.ds(pl.multiple_of(c * half, half), half)
            rows_s = pl.ds(pl.multiple_of((1 - c) * half, half), half)

            def remote(k, s_ref, d_ref, to, send_sems=send_sems, recv_sems=recv_sems):
                return pltpu.make_async_remote_copy(src_ref=s_ref, dst_ref=d_ref, send_sem=send_sems.at[k], recv_sem=recv_sems.at[k],
                                                    device_id=to, device_id_type=MESH)

            local = pltpu.make_async_copy(src_ref, out_ref.at[me], local_sems.at[j])
            local.start()
            chips = _other_chips(x, y)
            sent = [remote(k, src_ref.at[rows_c], out_ref.at[me, rows_c], (px, py, c)) for k, (px, py) in enumerate(chips)]
            landing = [remote(k, src_ref.at[rows_c], out_ref.at[2 * px + py, rows_c], (px, py, c)) for k, (px, py) in enumerate(chips)]
            passed = [remote(3 + k, out_ref.at[2 * px + py, rows_c], out_ref.at[2 * px + py, rows_c], (x, y, 1 - c))
                      for k, (px, py) in enumerate(chips)]
            from_sibling = [remote(3 + k, out_ref.at[2 * px + py, rows_s], out_ref.at[2 * px + py, rows_s], (x, y, 1 - c))
                            for k, (px, py) in enumerate(chips)]
            for cp in sent:
                cp.start()
            copies.append((local, sent, landing, passed, from_sibling))
        for local, sent, landing, passed, from_sibling in copies:
            for k in range(3):
                landing[k].wait_recv()
                passed[k].start()
        for local, sent, landing, passed, from_sibling in copies:
            for k in range(3):
                from_sibling[k].wait_recv()
            for cp in sent + passed:
                cp.wait_send()
            local.wait()

    sems = [pltpu.SemaphoreType.DMA((6,)) for _ in range(2 * n)] + [pltpu.SemaphoreType.DMA((n,))]
    return pl.pallas_call(
        body, out_shape=[jax.ShapeDtypeStruct((4, *a.shape), a.dtype) for a in srcs],
        in_specs=[pl.BlockSpec(memory_space=pl.ANY)] * n, out_specs=[pl.BlockSpec(memory_space=pltpu.VMEM)] * n,
        scratch_shapes=sems, compiler_params=pltpu.CompilerParams(vmem_limit_bytes=VMEM_LIMIT_BYTES), name=name)(*srcs)


def _adamw(w, g, m, v):
    m = ADAM_B1 * m + (1.0 - ADAM_B1) * g
    v = ADAM_B2 * v + (1.0 - ADAM_B2) * (g * g)
    delta = -ADAM_LR * ((m / M_HAT_DIV) / (jnp.sqrt(v / V_HAT_DIV) + ADAM_EPS) + ADAM_WD * w)
    return delta, m, v


def _small_allreduce_adamw(part, w, m, v):
    def body(part_ref, w_ref, m_ref, v_ref, g_out, d_out, m_out, v_out, loss_out, buf, send_sems, recv_sems):
        x, y, c = _place()
        me = 4 * x + 2 * y + c
        buf[me] = part_ref[...]

        def flip(k):
            fx, fy, fc = (k >> 2) & 1, (k >> 1) & 1, k & 1
            px, py, pc = (1 - x if fx else x), (1 - y if fy else y), (1 - c if fc else c)
            return (px, py, pc), 4 * px + 2 * py + pc

        def copy(k, slot):
            return pltpu.make_async_remote_copy(
                src_ref=part_ref, dst_ref=buf.at[slot], send_sem=send_sems.at[k - 1], recv_sem=recv_sems.at[k - 1],
                device_id=flip(k)[0], device_id_type=MESH)

        sent = [copy(k, me) for k in range(1, 8)]
        for cp in sent:
            cp.start()
        for k in range(1, 8):
            copy(k, flip(k)[1]).wait_recv()
        for cp in sent:
            cp.wait_send()
        tot = buf[0]
        for j in range(1, 8):
            tot = tot + buf[j]
        delta, m_new, v_new = _adamw(w_ref[...], tot, m_ref[...], v_ref[...])
        g_out[...] = tot
        d_out[...] = delta
        m_out[...] = m_new
        v_out[...] = v_new
        loss_out[...] = jnp.sum(tot[LOSS_ROW0:LOSS_ROW0 + 8, :]).reshape(1, 1)

    vm = pl.BlockSpec(memory_space=pltpu.VMEM)
    shp = jax.ShapeDtypeStruct((SMALL_ROWS, 128), F32)
    return pl.pallas_call(
        body, out_shape=[shp, shp, shp, shp, jax.ShapeDtypeStruct((1, 1), F32)],
        in_specs=[vm, vm, vm, vm], out_specs=[vm, vm, vm, vm, vm],
        scratch_shapes=[pltpu.VMEM((8, SMALL_ROWS, 128), F32), pltpu.SemaphoreType.DMA((7,)), pltpu.SemaphoreType.DMA((7,))],
        name="small_allreduce_adamw")(part, w, m, v)


def _row_tile(rows, cap):
    return max(t for t in range(16, min(rows, cap) + 1, 16) if rows % t == 0)


def _reduce_pair(name, parts):
    _, rows, w = parts.shape
    tr = _row_tile(rows, 576)
    nt = rows // tr

    def body(p_ref, o_ref, mine, theirs, send_sems, recv_sems):
        i = pl.program_id(0)
        x, y, c = _place()

        def copy(t):
            rows_t = pl.ds(pl.multiple_of(t * tr, tr), tr)
            return pltpu.make_async_remote_copy(src_ref=mine.at[rows_t], dst_ref=theirs.at[rows_t], send_sem=send_sems.at[t],
                                                recv_sem=recv_sems.at[t], device_id=(x, y, 1 - c), device_id_type=MESH)

        @pl.when(i < nt)
        def _():
            mine[pl.ds(pl.multiple_of(i * tr, tr), tr), :] = (
                (p_ref[0].astype(F32) + p_ref[1].astype(F32)) + p_ref[2].astype(F32)) + p_ref[3].astype(F32)
            copy(i).start()

        @pl.when(i >= nt)
        def _():
            t = i - nt
            copy(t).wait()
            rows_t = pl.ds(pl.multiple_of(t * tr, tr), tr)
            o_ref[...] = mine[rows_t, :] + theirs[rows_t, :]

    return pl.pallas_call(
        body, out_shape=jax.ShapeDtypeStruct((rows, w), F32), grid=(2 * nt,),
        in_specs=[pl.BlockSpec((4, tr, w), lambda i: (0, jnp.minimum(i, nt - 1), 0))],
        out_specs=pl.BlockSpec((tr, w), lambda i: (jnp.maximum(i - nt, 0), 0)),
        scratch_shapes=[pltpu.VMEM((rows, w), F32), pltpu.VMEM((rows, w), F32), pltpu.SemaphoreType.DMA((nt,)),
                        pltpu.SemaphoreType.DMA((nt,))],
        compiler_params=_params(("arbitrary",)), name=name)(parts)


def _adamw_shard(name, g, w, m, v):
    rows, cols = w.shape
    tr = _row_tile(rows, 256)

    def body(g_ref, w_ref, m_ref, v_ref, d_out, m_out, v_out):
        delta, m_new, v_new = _adamw(w_ref[...], g_ref[...], m_ref[...], v_ref[...])
        d_out[...] = delta
        m_out[...] = m_new
        v_out[...] = v_new

    t_spec = pl.BlockSpec((tr, cols), lambda i: (i, 0))
    shp = jax.ShapeDtypeStruct((rows, cols), F32)
    return pl.pallas_call(body, out_shape=[shp] * 3, grid=(rows // tr,), in_specs=[t_spec] * 4, out_specs=[t_spec] * 3,
                          compiler_params=_params(("parallel",)), name=name)(g, w, m, v)


def _shard_shape(name):
    _, r, c, ax = BIG_BY_NAME[name]
    return (r, c // 4) if ax == 1 else (r // 4, c)


def _pad_rows(a, axis):
    pad = [(0, 0)] * a.ndim
    pad[axis] = (0, -a.shape[axis] % PACK_ALIGN)
    return jnp.pad(a, pad)


def _pack_shards(names, shards, dtype):
    return _pad_rows(jnp.concatenate([s.astype(dtype).reshape(-1, PACK_W) for s in shards], axis=0), 0)


def _unpack_shards(names, slab):
    out, off = [], 0
    for name in names:
        rs, cs = _shard_shape(name)
        n = rs * cs // PACK_W
        out.append(slab[off:off + n].reshape(rs, cs))
        off += n
    return out


def _unpack_full(names, slabs):
    out, off = [], 0
    for name in names:
        _, r, c, ax = BIG_BY_NAME[name]
        n = r * c // 4 // PACK_W
        seg = slabs[:, off:off + n]
        out.append(seg.reshape(4, r, c // 4).transpose(1, 0, 2).reshape(r, c) if ax == 1 else seg.reshape(r, c))
        off += n
    return out


def _pack_full(names, mats, dtype):
    segs = []
    for name, a in zip(names, mats):
        _, r, c, ax = BIG_BY_NAME[name]
        a = a.astype(dtype)
        a = a.reshape(r, 4, c // 4).transpose(1, 0, 2) if ax == 1 else a
        segs.append(a.reshape(4, -1, PACK_W))
    return _pad_rows(jnp.concatenate(segs, axis=1), 1)


def _pad_heads_cols(wm, heads, d):
    k = wm.shape[0]
    return jnp.pad(wm.reshape(k, heads, d), ((0, 0), (0, 0), (0, HEAD_W - d))).reshape(k, heads * HEAD_W)


def _unpad_heads_cols(wm, heads, d):
    k = wm.shape[0]
    return wm.reshape(k, heads, HEAD_W)[:, :, :d].reshape(k, heads * d)


def _win_ext(w_in):
    o = np.cumsum([0, Q_LORA, KV_LORA, QK_ROPE, H_B * HD_B, KV_B * HD_B, KV_B * HD_B, D_MODEL, D_MODEL])
    pc = lambda a, n: jnp.pad(a, ((0, 0), (0, n - a.shape[1])))
    return jnp.concatenate([
        _pad_heads_cols(w_in[:, o[3]:o[4]], H_B, HD_B), w_in[:, o[0]:o[1]], w_in[:, o[1]:o[2]], pc(w_in[:, o[2]:o[3]], HEAD_W),
        _pad_heads_cols(w_in[:, o[4]:o[5]], KV_B, HD_B), _pad_heads_cols(w_in[:, o[5]:o[6]], KV_B, HD_B),
        w_in[:, o[6]:o[7]], w_in[:, o[7]:o[8]]], axis=1)


def _win_unext(we):
    c = HEAD_W
    return jnp.concatenate([
        we[:, ZC_QLAT * c:ZC_CKV * c], we[:, ZC_CKV * c:ZC_KPE * c], we[:, ZC_KPE * c:ZC_KPE * c + QK_ROPE],
        _unpad_heads_cols(we[:, ZC_QB * c:ZC_QLAT * c], H_B, HD_B), _unpad_heads_cols(we[:, ZC_KB * c:ZC_VB * c], KV_B, HD_B),
        _unpad_heads_cols(we[:, ZC_VB * c:ZC_GA * c], KV_B, HD_B), we[:, ZC_GA * c:]], axis=1)


def _wkv_ext(w_kvb):
    wk = w_kvb.reshape(KV_LORA, H_A, QK_NOPE + V_DIM_A)
    k_cols = jnp.pad(wk[:, :, :QK_NOPE], ((0, 0), (0, 0), (0, HEAD_W - QK_NOPE))).reshape(KV_LORA, H_A * HEAD_W)
    v_cols = jnp.pad(wk[:, :, QK_NOPE:], ((0, 0), (0, 0), (0, HEAD_W - V_DIM_A))).reshape(KV_LORA, H_A * HEAD_W)
    eye = jnp.pad(jnp.eye(QK_ROPE, dtype=w_kvb.dtype), ((0, 0), (QK_NOPE, HEAD_W - QK_NOPE - QK_ROPE)))
    pe_rows = jnp.concatenate([jnp.tile(eye, (1, H_A)), jnp.zeros((QK_ROPE, H_A * HEAD_W), w_kvb.dtype)], axis=1)
    top = jnp.concatenate([k_cols, v_cols], axis=1)
    return jnp.concatenate([top, pe_rows, jnp.zeros((2 * HEAD_W - KV_LORA - QK_ROPE, 2 * H_A * HEAD_W), w_kvb.dtype)], axis=0)


def _wkv_unext(we):
    k_cols = we[:KV_LORA, :H_A * HEAD_W].reshape(KV_LORA, H_A, HEAD_W)[:, :, :QK_NOPE]
    v_cols = we[:KV_LORA, H_A * HEAD_W:].reshape(KV_LORA, H_A, HEAD_W)[:, :, :V_DIM_A]
    return jnp.concatenate([k_cols, v_cols], axis=2).reshape(KV_LORA, H_A * (QK_NOPE + V_DIM_A))


def _pad_heads_rows(wm, heads, d):
    n = wm.shape[1]
    return jnp.pad(wm.reshape(heads, d, n), ((0, 0), (0, HEAD_W - d), (0, 0))).reshape(heads * HEAD_W, n)


def _unpad_heads_rows(wm, heads, d):
    n = wm.shape[1]
    return wm.reshape(heads, HEAD_W, n)[:, :d].reshape(heads * d, n)


def _rope_tables(seq):
    def ang(pos, dim):
        inv = np.float32(ROPE_THETA) ** (-np.arange(0, dim, 2, dtype=np.float32) / np.float32(dim))
        return pos.astype(np.float32)[:, None] * inv[None, :]

    def rot(dim):
        r = np.zeros((dim, dim), np.float32)
        half = dim // 2
        r[np.arange(half) + half, np.arange(half)] = -1.0
        r[np.arange(half), np.arange(half) + half] = 1.0
        return r

    def table(blocks):
        cos, sin = np.ones((seq, HEAD_W), np.float32), np.zeros((seq, HEAD_W), np.float32)
        pm = np.zeros((HEAD_W, HEAD_W), np.float32)
        for c0, a in blocks:
            d = 2 * a.shape[1]
            cos[:, c0:c0 + d] = np.concatenate([np.cos(a), np.cos(a)], axis=1)
            sin[:, c0:c0 + d] = np.concatenate([np.sin(a), np.sin(a)], axis=1)
            pm[c0:c0 + d, c0:c0 + d] = rot(d)
        return jnp.asarray(cos), jnp.asarray(sin), jnp.asarray(pm, BF16), jnp.asarray(pm.T, BF16)

    tok = np.arange(seq)
    a1 = ang(tok, QK_ROPE)
    arow, acol = ang(tok // GRID_W, HD_B // 2), ang(tok % GRID_W, HD_B // 2)
    return table([(QK_NOPE, a1)]), table([(0, a1)]), table([(0, arow), (HD_B // 2, acol)])


def _local_step(x, p, tgt, gains, wts, ride=None):
    nb, seq, _ = x.shape
    t = nb * seq
    x0 = x.reshape(t, D_MODEL)
    p2 = p.reshape(t, PLE_DIM)
    tg = tgt.reshape(t, D_MODEL)
    (cq_t, sq_t, pq, pq_t), (ck_t, sk_t, pk, pk_t), (cb_t, sb_t, pb, pb_t) = _rope_tables(seq)
    padg = lambda g: jnp.pad(g, ((0, 0), (0, HEAD_W - g.shape[1])))
    g_qn, g_kn = padg(gains["g_qn"]), padg(gains["g_kn"])

    win = _win_ext(wts["w_in"])
    wqb = _pad_heads_cols(wts["w_qb"], H_A, QK_NOPE + QK_ROPE)
    wkv = _wkv_ext(wts["w_kvb"])

    norm = lambda n: (lambda v, g: (_rms(v, g, n),))
    full = lambda a: (a, a.shape[1], 0, False)

    h = _rowwise("norm_mix", norm(D_MODEL), [full(x0)], [(D_MODEL, BF16, D_MODEL, False)], consts=[gains["g_mix"]])
    z = _mm("in_proj", h, win, tn=2048)
    cq = _rowwise("norm_qa", norm(Q_LORA), [(z, Q_LORA, ZC_QLAT // 2, False)], [(Q_LORA, BF16, Q_LORA, False)],
                  consts=[gains["g_qa"]])
    qa = _mm("q_up", cq, wqb)

    def rope_fwd(scale):
        return lambda v, cos, sin, pm: ((v * cos + _perm(v, pm) * sin) * scale,)

    heads_tile = lambda n: (n * HEAD_W, BF16, n * HEAD_W, False)
    q_a = _rowwise("rope_qa", _per_head(rope_fwd(SCALE_A), H_A, 1, 1), [full(qa)], [heads_tile(H_A)],
                   pos=[cq_t, sq_t], consts=[pq], seq=seq)
    ckv = _rowwise("norm_kva", norm(KV_LORA), [(z, HEAD_W, ZC_CKV, False)], [(HEAD_W, BF16, HEAD_W, False)], consts=[gains["g_kva"]])
    kpe = _rowwise("rope_kpe", rope_fwd(1.0), [(z, HEAD_W, ZC_KPE, False)], [(HEAD_W, BF16, HEAD_W, False)],
                   pos=[ck_t, sk_t], consts=[pk], seq=seq)
    kin = jnp.concatenate([ckv, kpe], axis=1)
    kv_a = _mm("kv_up", kin, wkv, out_dtypes=(BF16,))
    o_a, lse_a, *got_a = _attn_fwd("attn_a_fwd", q_a, kv_a, 0, kv_a, H_A, heads=H_A, group=1, nseq=nb, seq=seq,
                                   rider=ride and ride["gather_a"])

    def prep_fwd(scale):
        def fn(v, cos, sin, g, pm):
            yv = _rms(v, g, HD_B)
            return ((yv * cos + _perm(yv, pm) * sin) * scale,)
        return fn

    z_qb, z_kb = (z, H_B * HEAD_W, ZC_QB // H_B, False), (z, KV_B * HEAD_W, ZC_KB // KV_B, False)
    q_b = _rowwise("prep_qb", _per_head(prep_fwd(SCALE_B), H_B, 1, 1), [z_qb], [heads_tile(H_B)],
                   pos=[cb_t, sb_t], consts=[g_qn, pb], seq=seq)
    k_b = _rowwise("prep_kb", _per_head(prep_fwd(1.0), KV_B, 1, 1), [z_kb], [heads_tile(KV_B)],
                   pos=[cb_t, sb_t], consts=[g_kn, pb], seq=seq)
    o_b, lse_b, *got_b = _attn_fwd("attn_b_fwd", q_b, k_b, 0, z, ZC_VB, heads=H_B, group=H_B // KV_B, nseq=nb, seq=seq,
                                   rider=ride and ride["gather_b"])
    if ride is not None:
        wts = {**wts, **ride["late_weights"](got_a[0], got_b[0])}
    woa = _pad_heads_rows(wts["w_oa"], H_A, V_DIM_A)
    wob = _pad_heads_rows(wts["w_ob"], H_B, HD_B)
    wo, wup, wdown, wpg, wple = wts["w_o"], wts["w_up"], wts["w_down"], wts["w_ple_gate"], wts["w_ple"]

    def residual_norm(acc, r, g):
        xv = r + acc
        return xv, _rms(xv, g, D_MODEL)

    def mix_out(oa, ob, ga, gb, r, g, w_a, w_b, w_out):
        a = jnp.dot(oa, w_a[...], preferred_element_type=F32)
        b = jnp.dot(ob, w_b[...], preferred_element_type=F32)
        mg = (_sigmoid(ga) * a + _sigmoid(gb) * b).astype(BF16)
        return (a, b, mg, *residual_norm(jnp.dot(mg, w_out[...], preferred_element_type=F32), r, g))

    z_ga, z_gb = (z, D_MODEL, ZC_GA // 8, False), (z, D_MODEL, ZC_GB // 8, False)
    wide = lambda d: (D_MODEL, d, D_MODEL, False)
    ya, yb, merged, x1, h2 = _rowwise("mix_out", mix_out, [full(o_a), full(o_b), z_ga, z_gb, full(x0)],
                                      [wide(F32), wide(F32), wide(BF16), wide(F32), wide(BF16)],
                                      consts=[gains["g_mlp"], woa, wob, wo], tm=256)

    def relu2(acc):
        u = jnp.maximum(acc, 0.0)
        return u, u * u

    u, usq = _mm("mlp_up", h2, wup, out_dtypes=(BF16, BF16), epi=relu2, tn=2048)
    x2, h3 = _mm("mlp_down", usq, wdown, out_dtypes=(F32, BF16), epi=residual_norm, extras=(x1,), consts=[gains["g_ple"]])

    def tail(x2v, h3v, pv, tv, gf, w_gate, w_emb):
        sg = _sigmoid(jnp.dot(h3v, w_gate[...], preferred_element_type=F32))
        pev = jnp.dot(pv.astype(BF16), w_emb[...], preferred_element_type=F32)
        x3 = x2v + sg * pev
        rs = lax.rsqrt(jnp.sum(x3 * x3, axis=-1, keepdims=True) * (1.0 / D_MODEL) + EPS)
        xh = x3 * rs
        err = xh * gf - tv
        dy = err * (1.0 / D_MODEL)
        dyg = dy * gf
        dx3 = rs * (dyg - xh * (jnp.sum(dyg * xh, axis=-1, keepdims=True) * (1.0 / D_MODEL)))
        return (dx3, dx3 * pev * sg * (1.0 - sg), dx3 * sg,
                jnp.sum(err * err, axis=0, keepdims=True) * (0.5 / D_MODEL), jnp.sum(dy * xh, axis=0, keepdims=True))

    dx3, dgpre, dpe, loss_part, dg_final = _rowwise(
        "tail", tail, [full(x2), full(h3), full(p2), full(tg)], [wide(F32), wide(BF16), wide(BF16)],
        consts=[gains["g_final"].reshape(1, D_MODEL), wpg, wple], accs=[(1, D_MODEL), (1, D_MODEL)], tm=256)

    def norm_bwd(n, with_res):
        if with_res:
            def fn(dh, v, res, g):
                dx, dg = _rms_bwd(dh, v, g, n)
                return dx + res, dg
        else:
            def fn(dh, v, g):
                return _rms_bwd(dh, v, g, n)
        return fn

    dw = {}
    dw["w_ple"] = _mm_tn("dw_ple", p2, dpe)
    dw["w_ple_gate"] = _mm_tn("dw_ple_gate", h3, dgpre)
    norm_res_bwd = norm_bwd(D_MODEL, True)
    dx2, dg_ple = _mm("d_ple_gate", dgpre, wpg, trans_b=True, epi=norm_res_bwd, extras=(x2, dx3), consts=[gains["g_ple"]],
                      accs=[(1, D_MODEL)], tm=256)
    dw["w_down"] = _mm_tn("dw_down", usq, dx2)
    dupre = _mm("d_mlp_down", dx2, wdown, trans_b=True, out_dtypes=(BF16,), epi=lambda acc, uv: (acc * (2.0 * uv.astype(F32)),),
                extras=(u,), tn=2048)
    dw["w_up"] = _mm_tn("dw_up", h2, dupre)
    dx1, dg_mlp = _mm("d_mlp_up", dupre, wup, trans_b=True, epi=norm_res_bwd, extras=(x1, dx2), consts=[gains["g_mlp"]],
                      accs=[(1, D_MODEL)], tm=256)
    dw["w_o"] = _mm_tn("dw_o", merged, dx1)

    def merge_bwd(dm, ga, gb, a, b, w_a, w_b):
        sa, sb = _sigmoid(ga), _sigmoid(gb)
        da, db = (dm * sa).astype(BF16), (dm * sb).astype(BF16)
        nt = (((1,), (1,)), ((), ()))
        return (da, db, dm * a * sa * (1.0 - sa), dm * b * sb * (1.0 - sb),
                lax.dot_general(da, w_a, nt, preferred_element_type=F32), lax.dot_general(db, w_b, nt, preferred_element_type=F32))

    dya, dyb, dga, dgb, do_a, do_b = _mm("d_out_proj", dx1, wo, trans_b=True, out_dtypes=(BF16,) * 6, epi=merge_bwd,
                                         extras=((z, ZC_GA // 8), (z, ZC_GB // 8), ya, yb), consts=[woa, wob], tm=256)
    dw["w_oa"] = _unpad_heads_rows(_mm_tn("dw_oa", o_a, dya), H_A, V_DIM_A)
    dw["w_ob"] = _unpad_heads_rows(_mm_tn("dw_ob", o_b, dyb), H_B, HD_B)
    res_a = _attn_bwd("attn_a_bwd", q_a, kv_a, 0, kv_a, H_A, o_a, do_a, lse_a, heads=H_A, group=1, nseq=nb, seq=seq,
                      rider=ride and ride["scatter_a"](dw))
    dq_a, dk_a, dv_a = res_a[:3]
    if ride is not None:
        ride["out"]["parts_a"] = res_a[3]

    def rope_bwd(scale):
        return lambda d, cos, sin, pm_t: ((d * cos + _perm(d * sin, pm_t)) * scale,)

    dqa = _rowwise("rope_qa_bwd", _per_head(rope_bwd(SCALE_A), H_A, 1, 1), [full(dq_a)], [heads_tile(H_A)],
                   pos=[cq_t, sq_t], consts=[pq_t], seq=seq)
    dw["w_qb"] = _unpad_heads_cols(_mm_tn("dw_qb", cq, dqa), H_A, QK_NOPE + QK_ROPE)
    dcq = _mm("d_q_up", dqa, wqb, trans_b=True)
    dq_lat, dg_qa = _rowwise("norm_qa_bwd", norm_bwd(Q_LORA, False), [full(dcq), (z, Q_LORA, ZC_QLAT // 2, False)],
                             [(Q_LORA, BF16, Q_LORA, False)], consts=[gains["g_qa"]], accs=[(1, Q_LORA)])
    dkv_a = jnp.concatenate([dk_a, dv_a], axis=1)
    dw["w_kvb"] = _wkv_unext(_mm_tn("dw_kv", kin, dkv_a))
    dq_b, dk_b, dv_b, *parts_b = _attn_bwd("attn_b_bwd", q_b, k_b, 0, z, ZC_VB, o_b, do_b, lse_b, heads=H_B, group=H_B // KV_B,
                                               nseq=nb, seq=seq, rider=ride and ride["scatter_b"](dw))
    if ride is not None:
        ride["out"]["parts_b"] = parts_b[0]
    dkin = _mm("d_kv_up", dkv_a, wkv, trans_b=True)
    dckv, dg_kva = _rowwise("norm_kva_bwd", norm_bwd(KV_LORA, False), [(dkin, HEAD_W, 0, False), (z, HEAD_W, ZC_CKV, False)],
                            [(HEAD_W, BF16, HEAD_W, False)], consts=[gains["g_kva"]], accs=[(1, KV_LORA)])
    dkpe = _rowwise("rope_kpe_bwd", rope_bwd(1.0), [(dkin, HEAD_W, 1, False)], [(HEAD_W, BF16, HEAD_W, False)],
                    pos=[ck_t, sk_t], consts=[pk_t], seq=seq)

    def prep_bwd(scale):
        def fn(d, v, cos, sin, g, pm_t):
            dyv = (d * cos + _perm(d * sin, pm_t)) * scale
            return _rms_bwd(dyv, v, g, HD_B)
        return fn

    dqb, dg_qn = _rowwise("prep_qb_bwd", _per_head(prep_bwd(SCALE_B), H_B, 2, 1), [full(dq_b), z_qb], [heads_tile(H_B)],
                          pos=[cb_t, sb_t], consts=[g_qn, pb_t], accs=[(1, HEAD_W)], seq=seq)
    dkb, dg_kn = _rowwise("prep_kb_bwd", _per_head(prep_bwd(1.0), KV_B, 2, 1), [full(dk_b), z_kb], [heads_tile(KV_B)],
                          pos=[cb_t, sb_t], consts=[g_kn, pb_t], accs=[(1, HEAD_W)], seq=seq)

    dz = jnp.concatenate([dqb, dq_lat, dckv, dkpe, dkb, dv_b, dga, dgb], axis=1)
    dw["w_in"] = _win_unext(_mm_tn("dw_in", h, dz))
    dx0, dg_mix, *parts_in = _mm("d_in_proj", dz, win, trans_b=True, epi=norm_res_bwd, extras=(x0, dx1), consts=[gains["g_mix"]],
                                 accs=[(1, D_MODEL)], tm=256, rider=ride and ride["scatter_in"](dw))
    if ride is not None:
        ride["out"]["parts_in"] = parts_in[0]

    dg = {"g_mix": dg_mix, "g_qa": dg_qa, "g_kva": dg_kva, "g_qn": dg_qn[:, :HD_B], "g_kn": dg_kn[:, :HD_B],
          "g_mlp": dg_mlp, "g_ple": dg_ple, "g_final": dg_final}
    return loss_part, dx0.reshape(nb, seq, D_MODEL), dg, dw


def _pack_small(vals, loss_part=None):
    flat = jnp.concatenate([vals[n].reshape(1, -1) for n, _ in SMALL], axis=1)
    loss = jnp.zeros((1, 8 * 128), F32) if loss_part is None else loss_part
    gap = jnp.zeros((1, LOSS_ROW0 * 128 - SMALL_N), F32)
    return jnp.concatenate([flat, gap, loss], axis=1).reshape(SMALL_ROWS, 128)


def _unpack_small(slab, like):
    flat, out, off = slab.reshape(-1), {}, 0
    for n, k in SMALL:
        out[n] = flat[off:off + k].reshape(like[n].shape)
        off += k
    return out


def kernel(x, p, g_mix, w_in, g_qa, w_qb, g_kva, w_kvb, g_qn, g_kn, w_oa, w_ob, w_o, g_mlp, w_up, w_down, g_ple, w_ple_gate, w_ple, g_final, loss_target, m_g_mix, m_w_in, m_g_qa, m_w_qb, m_g_kva, m_w_kvb, m_g_qn, m_g_kn, m_w_oa, m_w_ob, m_w_o, m_g_mlp, m_w_up, m_w_down, m_g_ple, m_w_ple_gate, m_w_ple, m_g_final, v_g_mix, v_w_in, v_g_qa, v_w_qb, v_g_kva, v_w_kvb, v_g_qn, v_g_kn, v_w_oa, v_w_ob, v_w_o, v_g_mlp, v_w_up, v_w_down, v_g_ple, v_w_ple_gate, v_w_ple, v_g_final):
    given = dict(locals())
    order = ["g_mix", "w_in", "g_qa", "w_qb", "g_kva", "w_kvb", "g_qn", "g_kn", "w_oa", "w_ob", "w_o", "g_mlp", "w_up",
             "w_down", "g_ple", "w_ple_gate", "w_ple", "g_final"]
    big_names = [n for n, _, _, _ in BIG]
    local = lambda prefix, names: [given[prefix + n][0] for n in names]
    group_b = LATE_B + EARLY_SMALL

    shards_cols = lambda a: a.reshape(a.shape[0], 4, a.shape[1] // 4).transpose(1, 0, 2)
    got_in, got_small = _gather_by_halves("weight_gather_early",
                                          [w_in[0].astype(BF16), _pack_shards(EARLY_SMALL, local("", EARLY_SMALL), BF16)])
    wts = {"w_in": got_in.transpose(1, 0, 2).reshape(got_in.shape[1], -1), **dict(zip(EARLY_SMALL, _unpack_full(EARLY_SMALL, got_small)))}
    gains = {n: given[n].reshape(1, -1) for n, _ in SMALL}
    ride = {
        "gather_a": _Exchange("gather", _pack_shards(LATE_A, local("", LATE_A), BF16)),
        "gather_b": _Exchange("gather", _pack_shards(LATE_B, local("", LATE_B), BF16)),
        "late_weights": lambda ga, gb: {**dict(zip(LATE_A, _unpack_full(LATE_A, ga))), **dict(zip(LATE_B, _unpack_full(LATE_B, gb)))},
        "scatter_a": lambda dw: _Exchange("scatter", _pack_full(LATE_A, [dw[n] for n in LATE_A], BF16)),
        "scatter_b": lambda dw: _Exchange("scatter", _pack_full(group_b, [dw[n] for n in group_b], BF16)),
        "scatter_in": lambda dw: _Exchange("scatter", shards_cols(dw["w_in"].astype(BF16))),
        "out": {},
    }
    loss_part, grad_x, dg, dw = _local_step(x, p[0], loss_target, gains, wts, ride)

    small = lambda prefix: _pack_small({n: given[prefix + n] for n, _ in SMALL})
    g_s, d_s, m_s, v_s, loss = _small_allreduce_adamw(_pack_small(dg, loss_part), small(""), small("m_"), small("v_"))

    grads = dict(zip(LATE_A, _unpack_shards(LATE_A, _reduce_pair("grad_reduce_a", ride["out"]["parts_a"]))))
    grads.update(zip(group_b, _unpack_shards(group_b, _reduce_pair("grad_reduce_b", ride["out"]["parts_b"]))))
    grads["w_in"] = _reduce_pair("grad_reduce_in", ride["out"]["parts_in"])

    res = {}
    for key, slab in (("grad_", g_s), ("delta_", d_s), ("new_m_", m_s), ("new_v_", v_s)):
        for n, val in _unpack_small(slab, given).items():
            res[key + n] = val
    for n in big_names:
        d_w, m_w, v_w = _adamw_shard("adamw_" + n, grads[n], given[n][0], given["m_" + n][0], given["v_" + n][0])
        res["grad_" + n], res["delta_" + n], res["new_m_" + n], res["new_v_" + n] = grads[n][None], d_w[None], m_w[None], v_w[None]
    outs = [loss.reshape(()), grad_x]
    for key in ("grad_", "delta_", "new_m_", "new_v_"):
        outs += [res[key + n] for n in order]
    return tuple(outs)
```

```python
import functools

import numpy as np
import jax
import jax.numpy as jnp
from jax import lax
from jax.experimental import pallas as pl
from jax.experimental.pallas import tpu as pltpu

F32 = jnp.float32
BF16 = jnp.bfloat16
MESH = pl.DeviceIdType.MESH

D_MODEL = 1024
GRID_W = 64
ROPE_THETA = 10000.0
EPS = 1e-6
H_A, QK_NOPE, QK_ROPE, V_DIM_A, Q_LORA, KV_LORA = 8, 64, 32, 64, 256, 128
H_B, KV_B, HD_B = 8, 2, 64
D_FF = 4096
PLE_DIM = 256
HEAD_W = 128
SCALE_A = (QK_NOPE + QK_ROPE) ** -0.5
SCALE_B = HD_B ** -0.5

ADAM_LR, ADAM_B1, ADAM_B2, ADAM_EPS, ADAM_WD, ADAM_STEP = 0.001, 0.9, 0.999, 1e-08, 0.01, 10
M_HAT_DIV = 1.0 - ADAM_B1 ** ADAM_STEP
V_HAT_DIV = 1.0 - ADAM_B2 ** ADAM_STEP

VMEM_LIMIT_BYTES = 56 * 1024 * 1024

ZC_QB, ZC_QLAT, ZC_CKV, ZC_KPE, ZC_KB, ZC_VB, ZC_GA, ZC_GB = 0, 8, 10, 11, 12, 14, 16, 24
Z_WIDTH = 32 * HEAD_W

BIG = [
    ("w_in", 1024, 3232, 1), ("w_qb", 256, 768, 1), ("w_kvb", 128, 1024, 1), ("w_oa", 512, 1024, 1),
    ("w_ob", 512, 1024, 1), ("w_o", 1024, 1024, 0), ("w_up", 1024, 4096, 1), ("w_down", 4096, 1024, 0),
    ("w_ple_gate", 1024, 1024, 0), ("w_ple", 256, 1024, 1),
]
BIG_BY_NAME = {e[0]: e for e in BIG}
PACK_W = 1024
PACK_ALIGN = 64
SLAB_EARLY = ["w_qb", "w_kvb"]
SLAB_LATE = ["w_oa", "w_ob", "w_ple"]

SMALL = [("g_mix", 1024), ("g_qa", 256), ("g_kva", 128), ("g_qn", 64), ("g_kn", 64), ("g_mlp", 1024),
         ("g_ple", 1024), ("g_final", 1024)]
SMALL_N = sum(n for _, n in SMALL)
LOSS_ROW0 = 40
SMALL_ROWS = 48


def _params(sem):
    return pltpu.CompilerParams(dimension_semantics=sem, vmem_limit_bytes=VMEM_LIMIT_BYTES)


def _sigmoid(v):
    return 1.0 / (1.0 + jnp.exp(-v))


def _perm(v, p_ref):
    pm = p_ref[...]
    hi = v.astype(BF16)
    lo = (v - hi.astype(F32)).astype(BF16)
    return (jnp.dot(hi, pm, preferred_element_type=F32) + jnp.dot(lo, pm, preferred_element_type=F32))


def _rms(v, g, n):
    rs = lax.rsqrt(jnp.sum(v * v, axis=-1, keepdims=True) * (1.0 / n) + EPS)
    return v * rs * g


def _rms_bwd(dy, v, g, n):
    rs = lax.rsqrt(jnp.sum(v * v, axis=-1, keepdims=True) * (1.0 / n) + EPS)
    vh = v * rs
    dyg = dy * g
    dx = rs * (dyg - vh * (jnp.sum(dyg * vh, axis=-1, keepdims=True) * (1.0 / n)))
    return dx, jnp.sum(dy * vh, axis=0, keepdims=True)


def _ride(body, grid, rider):
    if rider is None:
        return body, [], [], [], []
    n_x, n_sem = len(rider.srcs), len(rider.scratch)

    def wrapped(*refs):
        ids = [pl.program_id(a) for a in range(len(grid))]
        n_in = len(refs) - n_sem - 2 * n_x - rider.n_core_out - rider.n_core_scratch
        core_in, srcs = refs[:n_in], refs[n_in:n_in + n_x]
        core_out = refs[n_in + n_x:n_in + n_x + rider.n_core_out]
        dsts = refs[n_in + n_x + rider.n_core_out:n_in + 2 * n_x + rider.n_core_out]
        core_scr = refs[n_in + 2 * n_x + rider.n_core_out:len(refs) - n_sem]
        sems = refs[len(refs) - n_sem:]

        @pl.when(functools.reduce(jnp.logical_and, [a == 0 for a in ids]))
        def _():
            rider.start(srcs, dsts, *sems)

        body(*core_in, *core_out, *core_scr)

        @pl.when(functools.reduce(jnp.logical_and, [a == n - 1 for a, n in zip(ids, grid)]))
        def _():
            rider.finish(srcs, dsts, *sems)

    hbm = pl.BlockSpec(memory_space=pl.ANY)
    return wrapped, list(rider.srcs), [hbm] * n_x, list(rider.out_shapes), list(rider.scratch)


def _mm(name, a, b, *, trans_b=False, out_dtypes=(F32,), epi=None, extras=(), consts=(), accs=(), tm=512, tn=None, rider=None):
    m, k = a.shape
    n = b.shape[0] if trans_b else b.shape[1]
    tn = n if tn is None else min(tn, n)
    tm = min(tm, m)
    assert m % tm == 0 and n % tn == 0 and (b.shape[1] if trans_b else b.shape[0]) == k
    extras = [e if isinstance(e, tuple) else (e, 0) for e in extras]
    n_ex, n_c, n_out, n_acc = len(extras), len(consts), len(out_dtypes), len(accs)
    dims = (((1,), (1,)), ((), ())) if trans_b else (((1,), (0,)), ((), ()))

    def body(a_ref, b_ref, *rest):
        acc = lax.dot_general(a_ref[...].astype(BF16), b_ref[...].astype(BF16), dims, preferred_element_type=F32)
        res = (acc,) if epi is None else epi(acc, *[e[...] for e in rest[:n_ex + n_c]])
        o_refs = rest[n_ex + n_c:]
        for o_ref, r in zip(o_refs[:n_out], res[:n_out]):
            o_ref[...] = r.astype(o_ref.dtype)
        if n_acc:
            first = jnp.logical_and(pl.program_id(0) == 0, pl.program_id(1) == 0)

            @pl.when(first)
            def _():
                for o_ref, r in zip(o_refs[n_out:], res[n_out:]):
                    o_ref[...] = r

            @pl.when(jnp.logical_not(first))
            def _():
                for o_ref, r in zip(o_refs[n_out:], res[n_out:]):
                    o_ref[...] += r

    grid = (n // tn, m // tm)
    if rider is not None:
        rider.n_core_out, rider.n_core_scratch = n_out + n_acc, 0
    body, x_in, x_spec, x_out, x_scr = _ride(body, grid, rider)
    a_spec = pl.BlockSpec((tm, k), lambda j, i: (i, 0))
    b_spec = pl.BlockSpec((tn, k), lambda j, i: (j, 0)) if trans_b else pl.BlockSpec((k, tn), lambda j, i: (0, j))
    t_spec = pl.BlockSpec((tm, tn), lambda j, i: (i, j))
    e_specs = [pl.BlockSpec((tm, tn), lambda j, i, off=off: (i, j + off)) for _, off in extras]
    c_specs = [pl.BlockSpec(c.shape, lambda j, i: (0, 0)) for c in consts]
    acc_specs = [pl.BlockSpec(sh, lambda j, i: (0, 0)) for sh in accs]
    sem = ("parallel", "parallel") if rider is None and not n_acc else ("arbitrary", "arbitrary")
    outs = pl.pallas_call(
        body, out_shape=[jax.ShapeDtypeStruct((m, n), d) for d in out_dtypes] + [jax.ShapeDtypeStruct(sh, F32) for sh in accs] + x_out,
        grid=grid, in_specs=[a_spec, b_spec] + e_specs + c_specs + x_spec, out_specs=[t_spec] * n_out + acc_specs + x_spec,
        scratch_shapes=x_scr, compiler_params=_params(sem), name=name)(a, b, *[e for e, _ in extras], *consts, *x_in)
    return outs[0] if len(outs) == 1 else outs


def _per_head(fn, heads, n_tiled, n_out):
    def run(*args):
        res = [fn(*[a[:, hd * HEAD_W:(hd + 1) * HEAD_W] for a in args[:n_tiled]], *args[n_tiled:]) for hd in range(heads)]
        tiles = [jnp.concatenate([r[k] for r in res], axis=1) for k in range(n_out)]
        sums = [functools.reduce(lambda u, v: u + v, [r[k] for r in res]) for k in range(n_out, len(res[0]))]
        return (*tiles, *sums)
    return run


def _mm_tn(name, a, b, *, out_dtype=BF16, tk=1024, tn=1024, tt=1024):
    t, k = a.shape
    n = b.shape[1]
    tk, tn, tt = min(tk, k), min(tn, n), min(tt, t)
    assert b.shape[0] == t and k % tk == 0 and n % tn == 0 and t % tt == 0
    nt = t // tt

    def body(a_ref, b_ref, o_ref, acc):
        part = lax.dot_general(a_ref[...].astype(BF16), b_ref[...].astype(BF16), (((0,), (0,)), ((), ())),
                               preferred_element_type=F32)

        @pl.when(pl.program_id(2) == 0)
        def _():
            acc[...] = part

        @pl.when(pl.program_id(2) != 0)
        def _():
            acc[...] += part

        @pl.when(pl.program_id(2) == nt - 1)
        def _():
            o_ref[...] = acc[...].astype(o_ref.dtype)

    return pl.pallas_call(
        body, out_shape=jax.ShapeDtypeStruct((k, n), out_dtype), grid=(k // tk, n // tn, nt),
        in_specs=[pl.BlockSpec((tt, tk), lambda ki, ni, ti: (ti, ki)), pl.BlockSpec((tt, tn), lambda ki, ni, ti: (ti, ni))],
        out_specs=pl.BlockSpec((tk, tn), lambda ki, ni, ti: (ki, ni)), scratch_shapes=[pltpu.VMEM((tk, tn), F32)],
        compiler_params=_params(("parallel", "parallel", "arbitrary")), name=name)(a, b)


def _rowwise(name, fn, ins, outs, *, consts=(), pos=(), accs=(), heads=1, tm=512, seq=None, rider=None):
    t = ins[0][0].shape[0]
    tm = min(tm, t if seq is None else seq)
    assert t % tm == 0 and (seq is None or seq % tm == 0)
    n_in, n_pos, n_c, n_out, n_acc = len(ins), len(pos), len(consts), len(outs), len(accs)

    def body(*refs):
        vals = [r[...] for r in refs[:n_in + n_pos + n_c]]
        res = fn(*vals)
        o_refs = refs[n_in + n_pos + n_c:]
        for o_ref, r in zip(o_refs[:n_out], res[:n_out]):
            o_ref[...] = r.astype(o_ref.dtype)
        if n_acc:
            first = jnp.logical_and(pl.program_id(0) == 0, pl.program_id(1) == 0)

            @pl.when(first)
            def _():
                for o_ref, r in zip(o_refs[n_out:], res[n_out:]):
                    o_ref[...] = r

            @pl.when(jnp.logical_not(first))
            def _():
                for o_ref, r in zip(o_refs[n_out:], res[n_out:]):
                    o_ref[...] += r

    def tiled(width, c0, per_head):
        return pl.BlockSpec((tm, width), (lambda h, i: (i, c0 + h)) if per_head else (lambda h, i: (i, c0)))

    in_specs = [tiled(w, c0, ph) for _, w, c0, ph in ins]
    if n_pos:
        nblk = seq // tm
        in_specs += [pl.BlockSpec((tm, a.shape[1]), lambda h, i: (i % nblk, 0)) for a in pos]
    in_specs += [pl.BlockSpec(a.shape, lambda h, i: (0, 0)) for a in consts]
    out_specs = [tiled(w, 0, ph) for _, _, w, ph in outs] + [pl.BlockSpec(s, lambda h, i: (0, 0)) for s in accs]
    out_shape = [jax.ShapeDtypeStruct((t, c), d) for c, d, _, _ in outs] + [jax.ShapeDtypeStruct(s, F32) for s in accs]
    sem = ("arbitrary", "arbitrary") if n_acc or rider is not None else ("parallel", "parallel")
    grid = (heads, t // tm)
    if rider is not None:
        rider.n_core_out, rider.n_core_scratch = n_out + n_acc, 0
    body, x_in, x_spec, x_out, x_scr = _ride(body, grid, rider)
    res = pl.pallas_call(body, out_shape=out_shape + x_out, grid=grid, in_specs=in_specs + x_spec, out_specs=out_specs + x_spec,
                         scratch_shapes=x_scr, compiler_params=_params(sem), name=name)(*[a for a, _, _, _ in ins], *pos, *consts, *x_in)
    return res[0] if len(res) == 1 else res


ATTN_HEADS_PER_STEP = 4


def _attn_fwd(name, q, k, kc0, v, vc0, *, heads, group, nseq, seq, tq=512, rider=None):
    tq = min(tq, seq)
    nq = seq // tq
    hp = ATTN_HEADS_PER_STEP
    grid = (heads // hp, nseq, nq)
    shared = group > 1
    assert group % hp == 0 if shared else (kc0 % hp == 0 and vc0 % hp == 0)

    def body(q_ref, k_ref, v_ref, o_ref, lse_ref):
        for j in range(hp):
            cols = slice(j * HEAD_W, (j + 1) * HEAD_W)
            kj = (k_ref[...] if shared else k_ref[:, cols]).astype(BF16)
            vj = (v_ref[...] if shared else v_ref[:, cols]).astype(BF16)
            s = lax.dot_general(q_ref[:, cols], kj, (((1,), (1,)), ((), ())), preferred_element_type=F32)
            m = jnp.max(s, axis=-1, keepdims=True)
            p = jnp.exp(s - m)
            l = jnp.sum(p, axis=-1, keepdims=True)
            o = jnp.dot(p.astype(BF16), vj, preferred_element_type=F32)
            o_ref[:, cols] = (o * (1.0 / l)).astype(o_ref.dtype)
            lse_ref[j] = m + jnp.log(l)

    if rider is not None:
        rider.n_core_out, rider.n_core_scratch = 2, 0
    body, x_in, x_spec, x_out, x_scr = _ride(body, grid, rider)
    q_spec = pl.BlockSpec((tq, hp * HEAD_W), lambda h, b, i: (b * nq + i, h))
    if shared:
        k_spec = pl.BlockSpec((seq, HEAD_W), lambda h, b, i: (b, kc0 + (h * hp) // group))
        v_spec = pl.BlockSpec((seq, HEAD_W), lambda h, b, i: (b, vc0 + (h * hp) // group))
    else:
        k_spec = pl.BlockSpec((seq, hp * HEAD_W), lambda h, b, i: (b, kc0 // hp + h))
        v_spec = pl.BlockSpec((seq, hp * HEAD_W), lambda h, b, i: (b, vc0 // hp + h))
    lse_spec = pl.BlockSpec((hp, tq, 1), lambda h, b, i: (h, b * nq + i, 0))
    sem = ("parallel",) * 3 if rider is None else ("arbitrary",) * 3
    return pl.pallas_call(
        body, out_shape=[jax.ShapeDtypeStruct(q.shape, BF16), jax.ShapeDtypeStruct((heads, q.shape[0], 1), F32)] + x_out,
        grid=grid, in_specs=[q_spec, k_spec, v_spec] + x_spec, out_specs=[q_spec, lse_spec] + x_spec, scratch_shapes=x_scr,
        compiler_params=_params(sem), name=name)(q, k, v, *x_in)


def _attn_bwd(name, q, k, kc0, v, vc0, o, do, lse, *, heads, group, nseq, seq, tq=1024, rider=None):
    tq = min(tq, seq)
    nq = seq // tq
    hk = heads // group
    t = q.shape[0]
    grid = (hk, nseq, group, nq)

    def body(q_ref, k_ref, v_ref, o_ref, do_ref, lse_ref, dq_ref, dk_ref, dv_ref, dk_acc, dv_acc):
        g, i = pl.program_id(2), pl.program_id(3)
        qv, kv, vv, dov = q_ref[...], k_ref[...].astype(BF16), v_ref[...].astype(BF16), do_ref[...]
        s = lax.dot_general(qv, kv, (((1,), (1,)), ((), ())), preferred_element_type=F32)
        pn = jnp.exp(s - lse_ref[...])
        dp = lax.dot_general(dov, vv, (((1,), (1,)), ((), ())), preferred_element_type=F32)
        delta = jnp.sum(dov.astype(F32) * o_ref[...].astype(F32), axis=-1, keepdims=True)
        ds = (pn * (dp - delta)).astype(BF16)
        dq_ref[...] = jnp.dot(ds, kv, preferred_element_type=F32)
        dk_part = lax.dot_general(ds, qv, (((0,), (0,)), ((), ())), preferred_element_type=F32)
        dv_part = lax.dot_general(pn.astype(BF16), dov, (((0,), (0,)), ((), ())), preferred_element_type=F32)
        first = jnp.logical_and(g == 0, i == 0)

        @pl.when(first)
        def _():
            dk_acc[...] = dk_part
            dv_acc[...] = dv_part

        @pl.when(jnp.logical_not(first))
        def _():
            dk_acc[...] += dk_part
            dv_acc[...] += dv_part

        @pl.when(jnp.logical_and(g == group - 1, i == nq - 1))
        def _():
            dk_ref[...] = dk_acc[...].astype(dk_ref.dtype)
            dv_ref[...] = dv_acc[...].astype(dv_ref.dtype)

    if rider is not None:
        rider.n_core_out, rider.n_core_scratch = 3, 2
    body, x_in, x_spec, x_out, x_scr = _ride(body, grid, rider)
    q_spec = pl.BlockSpec((tq, HEAD_W), lambda kh, b, g, i: (b * nq + i, kh * group + g))
    kv_out = pl.BlockSpec((seq, HEAD_W), lambda kh, b, g, i: (b, kh))
    lse_spec = pl.BlockSpec((None, tq, 1), lambda kh, b, g, i: (kh * group + g, b * nq + i, 0))
    sem = ("parallel", "parallel", "arbitrary", "arbitrary") if rider is None else ("arbitrary",) * 4
    return pl.pallas_call(
        body,
        out_shape=[jax.ShapeDtypeStruct(q.shape, F32), jax.ShapeDtypeStruct((t, hk * HEAD_W), BF16),
                   jax.ShapeDtypeStruct((t, hk * HEAD_W), BF16)] + x_out,
        grid=grid,
        in_specs=[q_spec, pl.BlockSpec((seq, HEAD_W), lambda kh, b, g, i: (b, kc0 + kh)),
                  pl.BlockSpec((seq, HEAD_W), lambda kh, b, g, i: (b, vc0 + kh)), q_spec, q_spec, lse_spec] + x_spec,
        out_specs=[q_spec, kv_out, kv_out] + x_spec,
        scratch_shapes=[pltpu.VMEM((seq, HEAD_W), F32), pltpu.VMEM((seq, HEAD_W), F32)] + x_scr,
        compiler_params=_params(sem), name=name)(q, k, v, o, do, lse, *x_in)


def _place():
    return lax.axis_index("x"), lax.axis_index("y"), lax.axis_index("c")


def _other_chips(x, y):
    return [(1 - x, y), (x, 1 - y), (1 - x, 1 - y)]


class _Exchange:
    def __init__(self, kind, srcs):
        assert kind in ("gather", "scatter")
        self.kind, self.srcs = kind, list(srcs)
        n = len(self.srcs)
        self.out_shapes = [jax.ShapeDtypeStruct((4, *a.shape[-2:]), a.dtype) for a in self.srcs]
        self.scratch = [pltpu.SemaphoreType.DMA((3 * n,)), pltpu.SemaphoreType.DMA((3 * n,)), pltpu.SemaphoreType.DMA((n,))]
        self.n_core_out = self.n_core_scratch = 0

    def _copies(self, j, src_ref, out_ref, send_sems, recv_sems, landing):
        x, y, c = _place()

        def remote(k, s, d, to):
            return pltpu.make_async_remote_copy(src_ref=s, dst_ref=d, send_sem=send_sems.at[3 * j + k], recv_sem=recv_sems.at[3 * j + k],
                                                device_id=to, device_id_type=MESH)

        me = 2 * x + y
        part = (lambda i: src_ref) if self.kind == "gather" else (lambda i: src_ref.at[i])
        if landing:
            return [remote(k, part(me), out_ref.at[2 * px + py], (px, py, c)) for k, (px, py) in enumerate(_other_chips(x, y))]
        return [remote(k, part(2 * px + py), out_ref.at[me], (px, py, c)) for k, (px, py) in enumerate(_other_chips(x, y))]

    def _local(self, j, src_ref, out_ref, local_sems):
        x, y, _ = _place()
        me = 2 * x + y
        return pltpu.make_async_copy(src_ref if self.kind == "gather" else src_ref.at[me], out_ref.at[me], local_sems.at[j])

    def start(self, src_refs, out_refs, send_sems, recv_sems, local_sems):
        for j, (src_ref, out_ref) in enumerate(zip(src_refs, out_refs)):
            self._local(j, src_ref, out_ref, local_sems).start()
            for mine in self._copies(j, src_ref, out_ref, send_sems, recv_sems, False):
                mine.start()

    def finish(self, src_refs, out_refs, send_sems, recv_sems, local_sems):
        for j, (src_ref, out_ref) in enumerate(zip(src_refs, out_refs)):
            for landed in self._copies(j, src_ref, out_ref, send_sems, recv_sems, True):
                landed.wait_recv()
        for j, (src_ref, out_ref) in enumerate(zip(src_refs, out_refs)):
            for mine in self._copies(j, src_ref, out_ref, send_sems, recv_sems, False):
                mine.wait_send()
            self._local(j, src_ref, out_ref, local_sems).wait()


def _gather_by_halves(name, srcs):
    n = len(srcs)

    def body(*refs):
        x, y, c = _place()
        me = 2 * x + y
        local_sems = refs[-1]
        copies = []
        for j in range(n):
            src_ref, out_ref, send_sems, recv_sems = refs[j], refs[n + j], refs[2 * n + 2 * j], refs[2 * n + 2 * j + 1]
            half = srcs[j].shape[0] // 2
            rows_c = pl.ds(pl.multiple_of(c * half, half), half)
            rows_s = pl.ds(pl.multiple_of((1 - c) * half, half), half)

            def remote(k, s_ref, d_ref, to, send_sems=send_sems, recv_sems=recv_sems):
                return pltpu.make_async_remote_copy(src_ref=s_ref, dst_ref=d_ref, send_sem=send_sems.at[k], recv_sem=recv_sems.at[k],
                                                    device_id=to, device_id_type=MESH)

            local = pltpu.make_async_copy(src_ref, out_ref.at[me], local_sems.at[j])
            local.start()
            chips = _other_chips(x, y)
            sent = [remote(k, src_ref.at[rows_c], out_ref.at[me, rows_c], (px, py, c)) for k, (px, py) in enumerate(chips)]
            landing = [remote(k, src_ref.at[rows_c], out_ref.at[2 * px + py, rows_c], (px, py, c)) for k, (px, py) in enumerate(chips)]
            passed = [remote(3 + k, out_ref.at[2 * px + py, rows_c], out_ref.at[2 * px + py, rows_c], (x, y, 1 - c))
                      for k, (px, py) in enumerate(chips)]
            from_sibling = [remote(3 + k, out_ref.at[2 * px + py, rows_s], out_ref.at[2 * px + py, rows_s], (x, y, 1 - c))
                            for k, (px, py) in enumerate(chips)]
            for cp in sent:
                cp.start()
            copies.append((local, sent, landing, passed, from_sibling))
        for local, sent, landing, passed, from_sibling in copies:
            for k in range(3):
                landing[k].wait_recv()
                passed[k].start()
        for local, sent, landing, passed, from_sibling in copies:
            for k in range(3):
                from_sibling[k].wait_recv()
            for cp in sent + passed:
                cp.wait_send()
            local.wait()

    sems = [pltpu.SemaphoreType.DMA((6,)) for _ in range(2 * n)] + [pltpu.SemaphoreType.DMA((n,))]
    return pl.pallas_call(
        body, out_shape=[jax.ShapeDtypeStruct((4, *a.shape), a.dtype) for a in srcs],
        in_specs=[pl.BlockSpec(memory_space=pl.ANY)] * n, out_specs=[pl.BlockSpec(memory_space=pltpu.VMEM)] * n,
        scratch_shapes=sems, compiler_params=pltpu.CompilerParams(vmem_limit_bytes=VMEM_LIMIT_BYTES), name=name)(*srcs)


def _adamw(w, g, m, v):
    m = ADAM_B1 * m + (1.0 - ADAM_B1) * g
    v = ADAM_B2 * v + (1.0 - ADAM_B2) * (g * g)
    delta = -ADAM_LR * ((m / M_HAT_DIV) / (jnp.sqrt(v / V_HAT_DIV) + ADAM_EPS) + ADAM_WD * w)
    return delta, m, v


def _small_allreduce_adamw(part, w, m, v):
    def body(part_ref, w_ref, m_ref, v_ref, g_out, d_out, m_out, v_out, loss_out, buf, send_sems, recv_sems):
        x, y, c = _place()
        me = 4 * x + 2 * y + c
        buf[me] = part_ref[...]

        def flip(k):
            fx, fy, fc = (k >> 2) & 1, (k >> 1) & 1, k & 1
            px, py, pc = (1 - x if fx else x), (1 - y if fy else y), (1 - c if fc else c)
            return (px, py, pc), 4 * px + 2 * py + pc

        def copy(k, slot):
            return pltpu.make_async_remote_copy(
                src_ref=part_ref, dst_ref=buf.at[slot], send_sem=send_sems.at[k - 1], recv_sem=recv_sems.at[k - 1],
                device_id=flip(k)[0], device_id_type=MESH)

        sent = [copy(k, me) for k in range(1, 8)]
        for cp in sent:
            cp.start()
        for k in range(1, 8):
            copy(k, flip(k)[1]).wait_recv()
        for cp in sent:
            cp.wait_send()
        tot = buf[0]
        for j in range(1, 8):
            tot = tot + buf[j]
        delta, m_new, v_new = _adamw(w_ref[...], tot, m_ref[...], v_ref[...])
        g_out[...] = tot
        d_out[...] = delta
        m_out[...] = m_new
        v_out[...] = v_new
        loss_out[...] = jnp.sum(tot[LOSS_ROW0:LOSS_ROW0 + 8, :]).reshape(1, 1)

    vm = pl.BlockSpec(memory_space=pltpu.VMEM)
    shp = jax.ShapeDtypeStruct((SMALL_ROWS, 128), F32)
    return pl.pallas_call(
        body, out_shape=[shp, shp, shp, shp, jax.ShapeDtypeStruct((1, 1), F32)],
        in_specs=[vm, vm, vm, vm], out_specs=[vm, vm, vm, vm, vm],
        scratch_shapes=[pltpu.VMEM((8, SMALL_ROWS, 128), F32), pltpu.SemaphoreType.DMA((7,)), pltpu.SemaphoreType.DMA((7,))],
        name="small_allreduce_adamw")(part, w, m, v)


def _row_tile(rows, cap):
    return max(t for t in range(16, min(rows, cap) + 1, 16) if rows % t == 0)


def _reduce_pair(name, parts):
    _, rows, w = parts.shape
    tr = _row_tile(rows, 576)
    nt = rows // tr

    def body(p_ref, o_ref, mine, theirs, send_sems, recv_sems):
        i = pl.program_id(0)
        x, y, c = _place()

        def copy(t):
            rows_t = pl.ds(pl.multiple_of(t * tr, tr), tr)
            return pltpu.make_async_remote_copy(src_ref=mine.at[rows_t], dst_ref=theirs.at[rows_t], send_sem=send_sems.at[t],
                                                recv_sem=recv_sems.at[t], device_id=(x, y, 1 - c), device_id_type=MESH)

        @pl.when(i < nt)
        def _():
            mine[pl.ds(pl.multiple_of(i * tr, tr), tr), :] = (
                (p_ref[0].astype(F32) + p_ref[1].astype(F32)) + p_ref[2].astype(F32)) + p_ref[3].astype(F32)
            copy(i).start()

        @pl.when(i >= nt)
        def _():
            t = i - nt
            copy(t).wait()
            rows_t = pl.ds(pl.multiple_of(t * tr, tr), tr)
            o_ref[...] = mine[rows_t, :] + theirs[rows_t, :]

    return pl.pallas_call(
        body, out_shape=jax.ShapeDtypeStruct((rows, w), F32), grid=(2 * nt,),
        in_specs=[pl.BlockSpec((4, tr, w), lambda i: (0, jnp.minimum(i, nt - 1), 0))],
        out_specs=pl.BlockSpec((tr, w), lambda i: (jnp.maximum(i - nt, 0), 0)),
        scratch_shapes=[pltpu.VMEM((rows, w), F32), pltpu.VMEM((rows, w), F32), pltpu.SemaphoreType.DMA((nt,)),
                        pltpu.SemaphoreType.DMA((nt,))],
        compiler_params=_params(("arbitrary",)), name=name)(parts)


def _adamw_shard(name, g, w, m, v):
    rows, cols = w.shape
    tr = _row_tile(rows, 256)

    def body(g_ref, w_ref, m_ref, v_ref, d_out, m_out, v_out):
        delta, m_new, v_new = _adamw(w_ref[...], g_ref[...], m_ref[...], v_ref[...])
        d_out[...] = delta
        m_out[...] = m_new
        v_out[...] = v_new

    t_spec = pl.BlockSpec((tr, cols), lambda i: (i, 0))
    shp = jax.ShapeDtypeStruct((rows, cols), F32)
    return pl.pallas_call(body, out_shape=[shp] * 3, grid=(rows // tr,), in_specs=[t_spec] * 4, out_specs=[t_spec] * 3,
                          compiler_params=_params(("parallel",)), name=name)(g, w, m, v)


def _shard_shape(name):
    _, r, c, ax = BIG_BY_NAME[name]
    return (r, c // 4) if ax == 1 else (r // 4, c)


def _pad_rows(a, axis):
    pad = [(0, 0)] * a.ndim
    pad[axis] = (0, -a.shape[axis] % PACK_ALIGN)
    return jnp.pad(a, pad)


def _pack_shards(names, shards, dtype):
    return _pad_rows(jnp.concatenate([s.astype(dtype).reshape(-1, PACK_W) for s in shards], axis=0), 0)


def _unpack_shards(names, slab):
    out, off = [], 0
    for name in names:
        rs, cs = _shard_shape(name)
        n = rs * cs // PACK_W
        out.append(slab[off:off + n].reshape(rs, cs))
        off += n
    return out


def _unpack_full(names, slabs):
    out, off = [], 0
    for name in names:
        _, r, c, ax = BIG_BY_NAME[name]
        n = r * c // 4 // PACK_W
        seg = slabs[:, off:off + n]
        out.append(seg.reshape(4, r, c // 4).transpose(1, 0, 2).reshape(r, c) if ax == 1 else seg.reshape(r, c))
        off += n
    return out


def _pack_full(names, mats, dtype):
    segs = []
    for name, a in zip(names, mats):
        _, r, c, ax = BIG_BY_NAME[name]
        a = a.astype(dtype)
        a = a.reshape(r, 4, c // 4).transpose(1, 0, 2) if ax == 1 else a
        segs.append(a.reshape(4, -1, PACK_W))
    return _pad_rows(jnp.concatenate(segs, axis=1), 1)


def _pad_heads_cols(wm, heads, d):
    k = wm.shape[0]
    return jnp.pad(wm.reshape(k, heads, d), ((0, 0), (0, 0), (0, HEAD_W - d))).reshape(k, heads * HEAD_W)


def _unpad_heads_cols(wm, heads, d):
    k = wm.shape[0]
    return wm.reshape(k, heads, HEAD_W)[:, :, :d].reshape(k, heads * d)


def _win_ext(w_in):
    o = np.cumsum([0, Q_LORA, KV_LORA, QK_ROPE, H_B * HD_B, KV_B * HD_B, KV_B * HD_B, D_MODEL, D_MODEL])
    pc = lambda a, n: jnp.pad(a, ((0, 0), (0, n - a.shape[1])))
    return jnp.concatenate([
        _pad_heads_cols(w_in[:, o[3]:o[4]], H_B, HD_B), w_in[:, o[0]:o[1]], w_in[:, o[1]:o[2]], pc(w_in[:, o[2]:o[3]], HEAD_W),
        _pad_heads_cols(w_in[:, o[4]:o[5]], KV_B, HD_B), _pad_heads_cols(w_in[:, o[5]:o[6]], KV_B, HD_B),
        w_in[:, o[6]:o[7]], w_in[:, o[7]:o[8]]], axis=1)


def _win_unext(we):
    c = HEAD_W
    return jnp.concatenate([
        we[:, ZC_QLAT * c:ZC_CKV * c], we[:, ZC_CKV * c:ZC_KPE * c], we[:, ZC_KPE * c:ZC_KPE * c + QK_ROPE],
        _unpad_heads_cols(we[:, ZC_QB * c:ZC_QLAT * c], H_B, HD_B), _unpad_heads_cols(we[:, ZC_KB * c:ZC_VB * c], KV_B, HD_B),
        _unpad_heads_cols(we[:, ZC_VB * c:ZC_GA * c], KV_B, HD_B), we[:, ZC_GA * c:]], axis=1)


def _wkv_ext(w_kvb):
    wk = w_kvb.reshape(KV_LORA, H_A, QK_NOPE + V_DIM_A)
    k_cols = jnp.pad(wk[:, :, :QK_NOPE], ((0, 0), (0, 0), (0, HEAD_W - QK_NOPE))).reshape(KV_LORA, H_A * HEAD_W)
    v_cols = jnp.pad(wk[:, :, QK_NOPE:], ((0, 0), (0, 0), (0, HEAD_W - V_DIM_A))).reshape(KV_LORA, H_A * HEAD_W)
    eye = jnp.pad(jnp.eye(QK_ROPE, dtype=w_kvb.dtype), ((0, 0), (QK_NOPE, HEAD_W - QK_NOPE - QK_ROPE)))
    pe_rows = jnp.concatenate([jnp.tile(eye, (1, H_A)), jnp.zeros((QK_ROPE, H_A * HEAD_W), w_kvb.dtype)], axis=1)
    top = jnp.concatenate([k_cols, v_cols], axis=1)
    return jnp.concatenate([top, pe_rows, jnp.zeros((2 * HEAD_W - KV_LORA - QK_ROPE, 2 * H_A * HEAD_W), w_kvb.dtype)], axis=0)


def _wkv_unext(we):
    k_cols = we[:KV_LORA, :H_A * HEAD_W].reshape(KV_LORA, H_A, HEAD_W)[:, :, :QK_NOPE]
    v_cols = we[:KV_LORA, H_A * HEAD_W:].reshape(KV_LORA, H_A, HEAD_W)[:, :, :V_DIM_A]
    return jnp.concatenate([k_cols, v_cols], axis=2).reshape(KV_LORA, H_A * (QK_NOPE + V_DIM_A))


def _pad_heads_rows(wm, heads, d):
    n = wm.shape[1]
    return jnp.pad(wm.reshape(heads, d, n), ((0, 0), (0, HEAD_W - d), (0, 0))).reshape(heads * HEAD_W, n)


def _unpad_heads_rows(wm, heads, d):
    n = wm.shape[1]
    return wm.reshape(heads, HEAD_W, n)[:, :d].reshape(heads * d, n)


def _rope_tables(seq):
    def ang(pos, dim):
        inv = np.float32(ROPE_THETA) ** (-np.arange(0, dim, 2, dtype=np.float32) / np.float32(dim))
        return pos.astype(np.float32)[:, None] * inv[None, :]

    def rot(dim):
        r = np.zeros((dim, dim), np.float32)
        half = dim // 2
        r[np.arange(half) + half, np.arange(half)] = -1.0
        r[np.arange(half), np.arange(half) + half] = 1.0
        return r

    def table(blocks):
        cos, sin = np.ones((seq, HEAD_W), np.float32), np.zeros((seq, HEAD_W), np.float32)
        pm = np.zeros((HEAD_W, HEAD_W), np.float32)
        for c0, a in blocks:
            d = 2 * a.shape[1]
            cos[:, c0:c0 + d] = np.concatenate([np.cos(a), np.cos(a)], axis=1)
            sin[:, c0:c0 + d] = np.concatenate([np.sin(a), np.sin(a)], axis=1)
            pm[c0:c0 + d, c0:c0 + d] = rot(d)
        return jnp.asarray(cos), jnp.asarray(sin), jnp.asarray(pm, BF16), jnp.asarray(pm.T, BF16)

    tok = np.arange(seq)
    a1 = ang(tok, QK_ROPE)
    arow, acol = ang(tok // GRID_W, HD_B // 2), ang(tok % GRID_W, HD_B // 2)
    return table([(QK_NOPE, a1)]), table([(0, a1)]), table([(0, arow), (HD_B // 2, acol)])


def _local_step(x, p, tgt, gains, wts, ride=None):
    nb, seq, _ = x.shape
    t = nb * seq
    x0 = x.reshape(t, D_MODEL)
    p2 = p.reshape(t, PLE_DIM)
    tg = tgt.reshape(t, D_MODEL)
    (cq_t, sq_t, pq, pq_t), (ck_t, sk_t, pk, pk_t), (cb_t, sb_t, pb, pb_t) = _rope_tables(seq)
    padg = lambda g: jnp.pad(g, ((0, 0), (0, HEAD_W - g.shape[1])))
    g_qn, g_kn = padg(gains["g_qn"]), padg(gains["g_kn"])

    win = _win_ext(wts["w_in"])
    wqb = _pad_heads_cols(wts["w_qb"], H_A, QK_NOPE + QK_ROPE)
    wkv = _wkv_ext(wts["w_kvb"])

    norm = lambda n: (lambda v, g: (_rms(v, g, n),))
    full = lambda a: (a, a.shape[1], 0, False)

    h = _rowwise("norm_mix", norm(D_MODEL), [full(x0)], [(D_MODEL, BF16, D_MODEL, False)], consts=[gains["g_mix"]])
    z = _mm("in_proj", h, win, tn=2048)
    cq = _rowwise("norm_qa", norm(Q_LORA), [(z, Q_LORA, ZC_QLAT // 2, False)], [(Q_LORA, BF16, Q_LORA, False)],
                  consts=[gains["g_qa"]])
    qa = _mm("q_up", cq, wqb)

    def rope_fwd(scale):
        return lambda v, cos, sin, pm: ((v * cos + _perm(v, pm) * sin) * scale,)

    heads_tile = lambda n: (n * HEAD_W, BF16, n * HEAD_W, False)
    q_a = _rowwise("rope_qa", _per_head(rope_fwd(SCALE_A), H_A, 1, 1), [full(qa)], [heads_tile(H_A)],
                   pos=[cq_t, sq_t], consts=[pq], seq=seq)
    ckv = _rowwise("norm_kva", norm(KV_LORA), [(z, HEAD_W, ZC_CKV, False)], [(HEAD_W, BF16, HEAD_W, False)], consts=[gains["g_kva"]])
    kpe = _rowwise("rope_kpe", rope_fwd(1.0), [(z, HEAD_W, ZC_KPE, False)], [(HEAD_W, BF16, HEAD_W, False)],
                   pos=[ck_t, sk_t], consts=[pk], seq=seq)
    kin = jnp.concatenate([ckv, kpe], axis=1)
    kv_a = _mm("kv_up", kin, wkv, out_dtypes=(BF16,))
    o_a, lse_a, *got_a = _attn_fwd("attn_a_fwd", q_a, kv_a, 0, kv_a, H_A, heads=H_A, group=1, nseq=nb, seq=seq,
                                   rider=ride and ride["gather_a"])

    def prep_fwd(scale):
        def fn(v, cos, sin, g, pm):
            yv = _rms(v, g, HD_B)
            return ((yv * cos + _perm(yv, pm) * sin) * scale,)
        return fn

    z_qb, z_kb = (z, H_B * HEAD_W, ZC_QB // H_B, False), (z, KV_B * HEAD_W, ZC_KB // KV_B, False)
    q_b = _rowwise("prep_qb", _per_head(prep_fwd(SCALE_B), H_B, 1, 1), [z_qb], [heads_tile(H_B)],
                   pos=[cb_t, sb_t], consts=[g_qn, pb], seq=seq)
    k_b = _rowwise("prep_kb", _per_head(prep_fwd(1.0), KV_B, 1, 1), [z_kb], [heads_tile(KV_B)],
                   pos=[cb_t, sb_t], consts=[g_kn, pb], seq=seq)
    o_b, lse_b, *got_b = _attn_fwd("attn_b_fwd", q_b, k_b, 0, z, ZC_VB, heads=H_B, group=H_B // KV_B, nseq=nb, seq=seq,
                                   rider=ride and ride["gather_b"])
    if ride is not None:
        wts = {**wts, **ride["weights_a"](got_a), **ride["weights_b"](got_b)}
    woa = _pad_heads_rows(wts["w_oa"], H_A, V_DIM_A)
    wob = _pad_heads_rows(wts["w_ob"], H_B, HD_B)
    wo, wup, wpg, wple = wts["w_o"], wts["w_up"], wts["w_ple_gate"], wts["w_ple"]

    def residual_norm(acc, r, g):
        xv = r + acc
        return xv, _rms(xv, g, D_MODEL)

    def mix_out(oa, ob, ga, gb, r, g, w_a, w_b, w_out):
        a = jnp.dot(oa, w_a[...], preferred_element_type=F32)
        b = jnp.dot(ob, w_b[...], preferred_element_type=F32)
        mg = (_sigmoid(ga) * a + _sigmoid(gb) * b).astype(BF16)
        return (a, b, mg, *residual_norm(jnp.dot(mg, w_out[...], preferred_element_type=F32), r, g))

    z_ga, z_gb = (z, D_MODEL, ZC_GA // 8, False), (z, D_MODEL, ZC_GB // 8, False)
    wide = lambda d: (D_MODEL, d, D_MODEL, False)
    ya, yb, merged, x1, h2, *got_m = _rowwise("mix_out", mix_out, [full(o_a), full(o_b), z_ga, z_gb, full(x0)],
                                              [wide(F32), wide(F32), wide(BF16), wide(F32), wide(BF16)],
                                              consts=[gains["g_mlp"], woa, wob, wo], tm=256, rider=ride and ride["gather_m"])
    wdown = wts["w_down"] if ride is None else ride["weights_m"](got_m)["w_down"]

    def relu2(acc):
        u = jnp.maximum(acc, 0.0)
        return u, u * u

    u, usq = _mm("mlp_up", h2, wup, out_dtypes=(BF16, BF16), epi=relu2, tn=2048)
    x2, h3 = _mm("mlp_down", usq, wdown, out_dtypes=(F32, BF16), epi=residual_norm, extras=(x1,), consts=[gains["g_ple"]])

    def tail(x2v, h3v, pv, tv, gf, w_gate, w_emb):
        sg = _sigmoid(jnp.dot(h3v, w_gate[...], preferred_element_type=F32))
        pev = jnp.dot(pv.astype(BF16), w_emb[...], preferred_element_type=F32)
        x3 = x2v + sg * pev
        rs = lax.rsqrt(jnp.sum(x3 * x3, axis=-1, keepdims=True) * (1.0 / D_MODEL) + EPS)
        xh = x3 * rs
        err = xh * gf - tv
        dy = err * (1.0 / D_MODEL)
        dyg = dy * gf
        dx3 = rs * (dyg - xh * (jnp.sum(dyg * xh, axis=-1, keepdims=True) * (1.0 / D_MODEL)))
        return (dx3, dx3 * pev * sg * (1.0 - sg), dx3 * sg,
                jnp.sum(err * err, axis=0, keepdims=True) * (0.5 / D_MODEL), jnp.sum(dy * xh, axis=0, keepdims=True))

    dx3, dgpre, dpe, loss_part, dg_final = _rowwise(
        "tail", tail, [full(x2), full(h3), full(p2), full(tg)], [wide(F32), wide(BF16), wide(BF16)],
        consts=[gains["g_final"].reshape(1, D_MODEL), wpg, wple], accs=[(1, D_MODEL), (1, D_MODEL)], tm=256)

    def norm_bwd(n, with_res):
        if with_res:
            def fn(dh, v, res, g):
                dx, dg = _rms_bwd(dh, v, g, n)
                return dx + res, dg
        else:
            def fn(dh, v, g):
                return _rms_bwd(dh, v, g, n)
        return fn

    dw = {}
    dw["w_ple"] = _mm_tn("dw_ple", p2, dpe)
    dw["w_ple_gate"] = _mm_tn("dw_ple_gate", h3, dgpre)
    norm_res_bwd = norm_bwd(D_MODEL, True)
    dx2, dg_ple = _mm("d_ple_gate", dgpre, wpg, trans_b=True, epi=norm_res_bwd, extras=(x2, dx3), consts=[gains["g_ple"]],
                      accs=[(1, D_MODEL)], tm=256)
    dw["w_down"] = _mm_tn("dw_down", usq, dx2)
    dupre = _mm("d_mlp_down", dx2, wdown, trans_b=True, out_dtypes=(BF16,), epi=lambda acc, uv: (acc * (2.0 * uv.astype(F32)),),
                extras=(u,), tn=2048)
    dw["w_up"] = _mm_tn("dw_up", h2, dupre)
    dx1, dg_mlp = _mm("d_mlp_up", dupre, wup, trans_b=True, epi=norm_res_bwd, extras=(x1, dx2), consts=[gains["g_mlp"]],
                      accs=[(1, D_MODEL)], tm=256)
    dw["w_o"] = _mm_tn("dw_o", merged, dx1)

    def merge_bwd(dm, ga, gb, a, b, w_a, w_b):
        sa, sb = _sigmoid(ga), _sigmoid(gb)
        da, db = (dm * sa).astype(BF16), (dm * sb).astype(BF16)
        nt = (((1,), (1,)), ((), ()))
        return (da, db, dm * a * sa * (1.0 - sa), dm * b * sb * (1.0 - sb),
                lax.dot_general(da, w_a, nt, preferred_element_type=F32), lax.dot_general(db, w_b, nt, preferred_element_type=F32))

    dya, dyb, dga, dgb, do_a, do_b = _mm("d_out_proj", dx1, wo, trans_b=True, out_dtypes=(BF16,) * 6, epi=merge_bwd,
                                         extras=((z, ZC_GA // 8), (z, ZC_GB // 8), ya, yb), consts=[woa, wob], tm=256)
    dw["w_oa"] = _unpad_heads_rows(_mm_tn("dw_oa", o_a, dya), H_A, V_DIM_A)
    dw["w_ob"] = _unpad_heads_rows(_mm_tn("dw_ob", o_b, dyb), H_B, HD_B)
    res_a = _attn_bwd("attn_a_bwd", q_a, kv_a, 0, kv_a, H_A, o_a, do_a, lse_a, heads=H_A, group=1, nseq=nb, seq=seq,
                      rider=ride and ride["scatter_a"](dw))
    dq_a, dk_a, dv_a = res_a[:3]
    if ride is not None:
        ride["out"]["parts_a"] = res_a[3:]

    def rope_bwd(scale):
        return lambda d, cos, sin, pm_t: ((d * cos + _perm(d * sin, pm_t)) * scale,)

    dqa = _rowwise("rope_qa_bwd", _per_head(rope_bwd(SCALE_A), H_A, 1, 1), [full(dq_a)], [heads_tile(H_A)],
                   pos=[cq_t, sq_t], consts=[pq_t], seq=seq)
    dw["w_qb"] = _unpad_heads_cols(_mm_tn("dw_qb", cq, dqa), H_A, QK_NOPE + QK_ROPE)
    dcq = _mm("d_q_up", dqa, wqb, trans_b=True)
    dq_lat, dg_qa = _rowwise("norm_qa_bwd", norm_bwd(Q_LORA, False), [full(dcq), (z, Q_LORA, ZC_QLAT // 2, False)],
                             [(Q_LORA, BF16, Q_LORA, False)], consts=[gains["g_qa"]], accs=[(1, Q_LORA)])
    dkv_a = jnp.concatenate([dk_a, dv_a], axis=1)
    dw["w_kvb"] = _wkv_unext(_mm_tn("dw_kv", kin, dkv_a))
    dq_b, dk_b, dv_b, *parts_b = _attn_bwd("attn_b_bwd", q_b, k_b, 0, z, ZC_VB, o_b, do_b, lse_b, heads=H_B, group=H_B // KV_B,
                                               nseq=nb, seq=seq, rider=ride and ride["scatter_b"](dw))
    if ride is not None:
        ride["out"]["parts_b"] = parts_b
    dkin = _mm("d_kv_up", dkv_a, wkv, trans_b=True)
    dckv, dg_kva = _rowwise("norm_kva_bwd", norm_bwd(KV_LORA, False), [(dkin, HEAD_W, 0, False), (z, HEAD_W, ZC_CKV, False)],
                            [(HEAD_W, BF16, HEAD_W, False)], consts=[gains["g_kva"]], accs=[(1, KV_LORA)])
    dkpe = _rowwise("rope_kpe_bwd", rope_bwd(1.0), [(dkin, HEAD_W, 1, False)], [(HEAD_W, BF16, HEAD_W, False)],
                    pos=[ck_t, sk_t], consts=[pk_t], seq=seq)

    def prep_bwd(scale):
        def fn(d, v, cos, sin, g, pm_t):
            dyv = (d * cos + _perm(d * sin, pm_t)) * scale
            return _rms_bwd(dyv, v, g, HD_B)
        return fn

    dqb, dg_qn = _rowwise("prep_qb_bwd", _per_head(prep_bwd(SCALE_B), H_B, 2, 1), [full(dq_b), z_qb], [heads_tile(H_B)],
                          pos=[cb_t, sb_t], consts=[g_qn, pb_t], accs=[(1, HEAD_W)], seq=seq)
    dkb, dg_kn = _rowwise("prep_kb_bwd", _per_head(prep_bwd(1.0), KV_B, 2, 1), [full(dk_b), z_kb], [heads_tile(KV_B)],
                          pos=[cb_t, sb_t], consts=[g_kn, pb_t], accs=[(1, HEAD_W)], seq=seq)

    dz = jnp.concatenate([dqb, dq_lat, dckv, dkpe, dkb, dv_b, dga, dgb], axis=1)
    dw["w_in"] = _win_unext(_mm_tn("dw_in", h, dz))
    dx0, dg_mix, *parts_in = _mm("d_in_proj", dz, win, trans_b=True, epi=norm_res_bwd, extras=(x0, dx1), consts=[gains["g_mix"]],
                                 accs=[(1, D_MODEL)], tm=256, rider=ride and ride["scatter_in"](dw))
    if ride is not None:
        ride["out"]["parts_in"] = parts_in

    dg = {"g_mix": dg_mix, "g_qa": dg_qa, "g_kva": dg_kva, "g_qn": dg_qn[:, :HD_B], "g_kn": dg_kn[:, :HD_B],
          "g_mlp": dg_mlp, "g_ple": dg_ple, "g_final": dg_final}
    return loss_part, dx0.reshape(nb, seq, D_MODEL), dg, dw


def _pack_small(vals, loss_part=None):
    flat = jnp.concatenate([vals[n].reshape(1, -1) for n, _ in SMALL], axis=1)
    loss = jnp.zeros((1, 8 * 128), F32) if loss_part is None else loss_part
    gap = jnp.zeros((1, LOSS_ROW0 * 128 - SMALL_N), F32)
    return jnp.concatenate([flat, gap, loss], axis=1).reshape(SMALL_ROWS, 128)


def _unpack_small(slab, like):
    flat, out, off = slab.reshape(-1), {}, 0
    for n, k in SMALL:
        out[n] = flat[off:off + k].reshape(like[n].shape)
        off += k
    return out


def kernel(x, p, g_mix, w_in, g_qa, w_qb, g_kva, w_kvb, g_qn, g_kn, w_oa, w_ob, w_o, g_mlp, w_up, w_down, g_ple, w_ple_gate, w_ple, g_final, loss_target, m_g_mix, m_w_in, m_g_qa, m_w_qb, m_g_kva, m_w_kvb, m_g_qn, m_g_kn, m_w_oa, m_w_ob, m_w_o, m_g_mlp, m_w_up, m_w_down, m_g_ple, m_w_ple_gate, m_w_ple, m_g_final, v_g_mix, v_w_in, v_g_qa, v_w_qb, v_g_kva, v_w_kvb, v_g_qn, v_g_kn, v_w_oa, v_w_ob, v_w_o, v_g_mlp, v_w_up, v_w_down, v_g_ple, v_w_ple_gate, v_w_ple, v_g_final):
    given = dict(locals())
    order = ["g_mix", "w_in", "g_qa", "w_qb", "g_kva", "w_kvb", "g_qn", "g_kn", "w_oa", "w_ob", "w_o", "g_mlp", "w_up",
             "w_down", "g_ple", "w_ple_gate", "w_ple", "g_final"]
    big_names = [n for n, _, _, _ in BIG]
    local = lambda prefix, names: [given[prefix + n][0] for n in names]
    slab = lambda names: _pack_shards(names, local("", names), BF16)
    bf = lambda n: given[n][0].astype(BF16)
    cols_full = lambda g: g.transpose(1, 0, 2).reshape(g.shape[1], -1)
    rows_full = lambda g: g.reshape(-1, g.shape[2])
    shards_cols = lambda a: a.reshape(a.shape[0], 4, a.shape[1] // 4).transpose(1, 0, 2)
    shards_rows = lambda a: a.reshape(4, a.shape[0] // 4, a.shape[1])
    packed = lambda names, dw: _pack_full(names, [dw[n] for n in names], BF16)

    got_in, got_early = _gather_by_halves("weight_gather_early", [bf("w_in"), slab(SLAB_EARLY)])
    wts = {"w_in": cols_full(got_in), **dict(zip(SLAB_EARLY, _unpack_full(SLAB_EARLY, got_early)))}
    gains = {n: given[n].reshape(1, -1) for n, _ in SMALL}
    ride = {
        "gather_a": _Exchange("gather", [slab(SLAB_LATE), bf("w_o"), bf("w_ple_gate")]),
        "weights_a": lambda got: {**dict(zip(SLAB_LATE, _unpack_full(SLAB_LATE, got[0]))), "w_o": rows_full(got[1]),
                                  "w_ple_gate": rows_full(got[2])},
        "gather_b": _Exchange("gather", [bf("w_up")]),
        "weights_b": lambda got: {"w_up": cols_full(got[0])},
        "gather_m": _Exchange("gather", [bf("w_down")]),
        "weights_m": lambda got: {"w_down": rows_full(got[0])},
        "scatter_a": lambda dw: _Exchange("scatter", [shards_cols(dw["w_up"]), shards_rows(dw["w_o"]), packed(SLAB_LATE, dw)]),
        "scatter_b": lambda dw: _Exchange("scatter", [shards_rows(dw["w_down"]), shards_rows(dw["w_ple_gate"]), packed(SLAB_EARLY, dw)]),
        "scatter_in": lambda dw: _Exchange("scatter", [shards_cols(dw["w_in"])]),
        "out": {},
    }
    loss_part, grad_x, dg, dw = _local_step(x, p[0], loss_target, gains, wts, ride)

    small = lambda prefix: _pack_small({n: given[prefix + n] for n, _ in SMALL})
    g_s, d_s, m_s, v_s, loss = _small_allreduce_adamw(_pack_small(dg, loss_part), small(""), small("m_"), small("v_"))

    parts = ride["out"]
    grads = {"w_up": _reduce_pair("grad_reduce_up", parts["parts_a"][0]), "w_o": _reduce_pair("grad_reduce_o", parts["parts_a"][1]),
             "w_down": _reduce_pair("grad_reduce_down", parts["parts_b"][0]),
             "w_ple_gate": _reduce_pair("grad_reduce_ple_gate", parts["parts_b"][1]),
             "w_in": _reduce_pair("grad_reduce_in", parts["parts_in"][0])}
    grads.update(zip(SLAB_LATE, _unpack_shards(SLAB_LATE, _reduce_pair("grad_reduce_slab_late", parts["parts_a"][2]))))
    grads.update(zip(SLAB_EARLY, _unpack_shards(SLAB_EARLY, _reduce_pair("grad_reduce_slab_early", parts["parts_b"][2]))))

    res = {}
    for key, slab in (("grad_", g_s), ("delta_", d_s), ("new_m_", m_s), ("new_v_", v_s)):
        for n, val in _unpack_small(slab, given).items():
            res[key + n] = val
    for n in big_names:
        d_w, m_w, v_w = _adamw_shard("adamw_" + n, grads[n], given[n][0], given["m_" + n][0], given["v_" + n][0])
        res["grad_" + n], res["delta_" + n], res["new_m_" + n], res["new_v_" + n] = grads[n][None], d_w[None], m_w[None], v_w[None]
    outs = [loss.reshape(()), grad_x]
    for key in ("grad_", "delta_", "new_m_", "new_v_"):
        outs += [res[key + n] for n in order]
    return tuple(outs)
```

```python
import functools

import numpy as np
import jax
import jax.numpy as jnp
from jax import lax
from jax.experimental import pallas as pl
from jax.experimental.pallas import tpu as pltpu

F32 = jnp.float32
BF16 = jnp.bfloat16
MESH = pl.DeviceIdType.MESH

D_MODEL = 1024
GRID_W = 64
ROPE_THETA = 10000.0
EPS = 1e-6
H_A, QK_NOPE, QK_ROPE, V_DIM_A, Q_LORA, KV_LORA = 8, 64, 32, 64, 256, 128
H_B, KV_B, HD_B = 8, 2, 64
D_FF = 4096
PLE_DIM = 256
HEAD_W = 128
SCALE_A = (QK_NOPE + QK_ROPE) ** -0.5
SCALE_B = HD_B ** -0.5

ADAM_LR, ADAM_B1, ADAM_B2, ADAM_EPS, ADAM_WD, ADAM_STEP = 0.001, 0.9, 0.999, 1e-08, 0.01, 10
M_HAT_DIV = 1.0 - ADAM_B1 ** ADAM_STEP
V_HAT_DIV = 1.0 - ADAM_B2 ** ADAM_STEP

VMEM_LIMIT_BYTES = 56 * 1024 * 1024

ZC_QB, ZC_QLAT, ZC_CKV, ZC_KPE, ZC_KB, ZC_VB, ZC_GA, ZC_GB = 0, 8, 10, 11, 12, 14, 16, 24
Z_WIDTH = 32 * HEAD_W

BIG = [
    ("w_in", 1024, 3232, 1), ("w_qb", 256, 768, 1), ("w_kvb", 128, 1024, 1), ("w_oa", 512, 1024, 1),
    ("w_ob", 512, 1024, 1), ("w_o", 1024, 1024, 0), ("w_up", 1024, 4096, 1), ("w_down", 4096, 1024, 0),
    ("w_ple_gate", 1024, 1024, 0), ("w_ple", 256, 1024, 1),
]
BIG_BY_NAME = {e[0]: e for e in BIG}
PACK_W = 1024
PACK_ALIGN = 64
SLAB_EARLY = ["w_qb", "w_kvb"]
SLAB_LATE = ["w_oa", "w_ob", "w_ple"]

SMALL = [("g_mix", 1024), ("g_qa", 256), ("g_kva", 128), ("g_qn", 64), ("g_kn", 64), ("g_mlp", 1024),
         ("g_ple", 1024), ("g_final", 1024)]
SMALL_N = sum(n for _, n in SMALL)
LOSS_ROW0 = 40
SMALL_ROWS = 48


def _params(sem):
    return pltpu.CompilerParams(dimension_semantics=sem, vmem_limit_bytes=VMEM_LIMIT_BYTES)


def _sigmoid(v):
    return 1.0 / (1.0 + jnp.exp(-v))


def _perm(v, p_ref):
    pm = p_ref[...]
    hi = v.astype(BF16)
    lo = (v - hi.astype(F32)).astype(BF16)
    return (jnp.dot(hi, pm, preferred_element_type=F32) + jnp.dot(lo, pm, preferred_element_type=F32))


def _rms(v, g, n):
    rs = lax.rsqrt(jnp.sum(v * v, axis=-1, keepdims=True) * (1.0 / n) + EPS)
    return v * rs * g


def _rms_bwd(dy, v, g, n):
    rs = lax.rsqrt(jnp.sum(v * v, axis=-1, keepdims=True) * (1.0 / n) + EPS)
    vh = v * rs
    dyg = dy * g
    dx = rs * (dyg - vh * (jnp.sum(dyg * vh, axis=-1, keepdims=True) * (1.0 / n)))
    return dx, jnp.sum(dy * vh, axis=0, keepdims=True)


def _ride(body, grid, rider):
    if rider is None:
        return body, [], [], [], []
    n_x, n_sem = len(rider.srcs), len(rider.scratch)

    def wrapped(*refs):
        ids = [pl.program_id(a) for a in range(len(grid))]
        n_in = len(refs) - n_sem - 2 * n_x - rider.n_core_out - rider.n_core_scratch
        core_in, srcs = refs[:n_in], refs[n_in:n_in + n_x]
        core_out = refs[n_in + n_x:n_in + n_x + rider.n_core_out]
        dsts = refs[n_in + n_x + rider.n_core_out:n_in + 2 * n_x + rider.n_core_out]
        core_scr = refs[n_in + 2 * n_x + rider.n_core_out:len(refs) - n_sem]
        sems = refs[len(refs) - n_sem:]

        @pl.when(functools.reduce(jnp.logical_and, [a == 0 for a in ids]))
        def _():
            rider.start(srcs, dsts, *sems)

        body(*core_in, *core_out, *core_scr)

        @pl.when(functools.reduce(jnp.logical_and, [a == n - 1 for a, n in zip(ids, grid)]))
        def _():
            rider.finish(srcs, dsts, *sems)

    hbm = pl.BlockSpec(memory_space=pl.ANY)
    return wrapped, list(rider.srcs), [hbm] * n_x, list(rider.out_shapes), list(rider.scratch)


def _mm(name, a, b, *, trans_b=False, out_dtypes=(F32,), epi=None, extras=(), consts=(), accs=(), tm=512, tn=None, rider=None):
    m, k = a.shape
    n = b.shape[0] if trans_b else b.shape[1]
    tn = n if tn is None else min(tn, n)
    tm = min(tm, m)
    assert m % tm == 0 and n % tn == 0 and (b.shape[1] if trans_b else b.shape[0]) == k
    extras = [e if isinstance(e, tuple) else (e, 0) for e in extras]
    n_ex, n_c, n_out, n_acc = len(extras), len(consts), len(out_dtypes), len(accs)
    dims = (((1,), (1,)), ((), ())) if trans_b else (((1,), (0,)), ((), ()))

    def body(a_ref, b_ref, *rest):
        acc = lax.dot_general(a_ref[...].astype(BF16), b_ref[...].astype(BF16), dims, preferred_element_type=F32)
        res = (acc,) if epi is None else epi(acc, *[e[...] for e in rest[:n_ex + n_c]])
        o_refs = rest[n_ex + n_c:]
        for o_ref, r in zip(o_refs[:n_out], res[:n_out]):
            o_ref[...] = r.astype(o_ref.dtype)
        if n_acc:
            first = jnp.logical_and(pl.program_id(0) == 0, pl.program_id(1) == 0)

            @pl.when(first)
            def _():
                for o_ref, r in zip(o_refs[n_out:], res[n_out:]):
                    o_ref[...] = r

            @pl.when(jnp.logical_not(first))
            def _():
                for o_ref, r in zip(o_refs[n_out:], res[n_out:]):
                    o_ref[...] += r

    grid = (n // tn, m // tm)
    if rider is not None:
        rider.n_core_out, rider.n_core_scratch = n_out + n_acc, 0
    body, x_in, x_spec, x_out, x_scr = _ride(body, grid, rider)
    a_spec = pl.BlockSpec((tm, k), lambda j, i: (i, 0))
    b_spec = pl.BlockSpec((tn, k), lambda j, i: (j, 0)) if trans_b else pl.BlockSpec((k, tn), lambda j, i: (0, j))
    t_spec = pl.BlockSpec((tm, tn), lambda j, i: (i, j))
    e_specs = [pl.BlockSpec((tm, tn), lambda j, i, off=off: (i, j + off)) for _, off in extras]
    c_specs = [pl.BlockSpec(c.shape, lambda j, i: (0, 0)) for c in consts]
    acc_specs = [pl.BlockSpec(sh, lambda j, i: (0, 0)) for sh in accs]
    sem = ("parallel", "parallel") if rider is None and not n_acc else ("arbitrary", "arbitrary")
    outs = pl.pallas_call(
        body, out_shape=[jax.ShapeDtypeStruct((m, n), d) for d in out_dtypes] + [jax.ShapeDtypeStruct(sh, F32) for sh in accs] + x_out,
        grid=grid, in_specs=[a_spec, b_spec] + e_specs + c_specs + x_spec, out_specs=[t_spec] * n_out + acc_specs + x_spec,
        scratch_shapes=x_scr, compiler_params=_params(sem), name=name)(a, b, *[e for e, _ in extras], *consts, *x_in)
    return outs[0] if len(outs) == 1 else outs


def _per_head(fn, heads, n_tiled, n_out):
    def run(*args):
        res = [fn(*[a[:, hd * HEAD_W:(hd + 1) * HEAD_W] for a in args[:n_tiled]], *args[n_tiled:]) for hd in range(heads)]
        tiles = [jnp.concatenate([r[k] for r in res], axis=1) for k in range(n_out)]
        sums = [functools.reduce(lambda u, v: u + v, [r[k] for r in res]) for k in range(n_out, len(res[0]))]
        return (*tiles, *sums)
    return run


def _mm_tn(name, a, b, *, out_dtype=BF16, tk=1024, tn=1024, tt=1024):
    t, k = a.shape
    n = b.shape[1]
    tk, tn, tt = min(tk, k), min(tn, n), min(tt, t)
    assert b.shape[0] == t and k % tk == 0 and n % tn == 0 and t % tt == 0
    nt = t // tt

    def body(a_ref, b_ref, o_ref, acc):
        part = lax.dot_general(a_ref[...].astype(BF16), b_ref[...].astype(BF16), (((0,), (0,)), ((), ())),
                               preferred_element_type=F32)

        @pl.when(pl.program_id(2) == 0)
        def _():
            acc[...] = part

        @pl.when(pl.program_id(2) != 0)
        def _():
            acc[...] += part

        @pl.when(pl.program_id(2) == nt - 1)
        def _():
            o_ref[...] = acc[...].astype(o_ref.dtype)

    return pl.pallas_call(
        body, out_shape=jax.ShapeDtypeStruct((k, n), out_dtype), grid=(k // tk, n // tn, nt),
        in_specs=[pl.BlockSpec((tt, tk), lambda ki, ni, ti: (ti, ki)), pl.BlockSpec((tt, tn), lambda ki, ni, ti: (ti, ni))],
        out_specs=pl.BlockSpec((tk, tn), lambda ki, ni, ti: (ki, ni)), scratch_shapes=[pltpu.VMEM((tk, tn), F32)],
        compiler_params=_params(("parallel", "parallel", "arbitrary")), name=name)(a, b)


def _rowwise(name, fn, ins, outs, *, consts=(), pos=(), accs=(), heads=1, tm=512, seq=None, rider=None):
    t = ins[0][0].shape[0]
    tm = min(tm, t if seq is None else seq)
    assert t % tm == 0 and (seq is None or seq % tm == 0)
    n_in, n_pos, n_c, n_out, n_acc = len(ins), len(pos), len(consts), len(outs), len(accs)

    def body(*refs):
        vals = [r[...] for r in refs[:n_in + n_pos + n_c]]
        res = fn(*vals)
        o_refs = refs[n_in + n_pos + n_c:]
        for o_ref, r in zip(o_refs[:n_out], res[:n_out]):
            o_ref[...] = r.astype(o_ref.dtype)
        if n_acc:
            first = jnp.logical_and(pl.program_id(0) == 0, pl.program_id(1) == 0)

            @pl.when(first)
            def _():
                for o_ref, r in zip(o_refs[n_out:], res[n_out:]):
                    o_ref[...] = r

            @pl.when(jnp.logical_not(first))
            def _():
                for o_ref, r in zip(o_refs[n_out:], res[n_out:]):
                    o_ref[...] += r

    def tiled(width, c0, per_head):
        return pl.BlockSpec((tm, width), (lambda h, i: (i, c0 + h)) if per_head else (lambda h, i: (i, c0)))

    in_specs = [tiled(w, c0, ph) for _, w, c0, ph in ins]
    if n_pos:
        nblk = seq // tm
        in_specs += [pl.BlockSpec((tm, a.shape[1]), lambda h, i: (i % nblk, 0)) for a in pos]
    in_specs += [pl.BlockSpec(a.shape, lambda h, i: (0, 0)) for a in consts]
    out_specs = [tiled(w, 0, ph) for _, _, w, ph in outs] + [pl.BlockSpec(s, lambda h, i: (0, 0)) for s in accs]
    out_shape = [jax.ShapeDtypeStruct((t, c), d) for c, d, _, _ in outs] + [jax.ShapeDtypeStruct(s, F32) for s in accs]
    sem = ("arbitrary", "arbitrary") if n_acc or rider is not None else ("parallel", "parallel")
    grid = (heads, t // tm)
    if rider is not None:
        rider.n_core_out, rider.n_core_scratch = n_out + n_acc, 0
    body, x_in, x_spec, x_out, x_scr = _ride(body, grid, rider)
    res = pl.pallas_call(body, out_shape=out_shape + x_out, grid=grid, in_specs=in_specs + x_spec, out_specs=out_specs + x_spec,
                         scratch_shapes=x_scr, compiler_params=_params(sem), name=name)(*[a for a, _, _, _ in ins], *pos, *consts, *x_in)
    return res[0] if len(res) == 1 else res


ATTN_HEADS_PER_STEP = 4


def _attn_fwd(name, q, k, kc0, v, vc0, *, heads, group, nseq, seq, tq=512, rider=None):
    tq = min(tq, seq)
    nq = seq // tq
    hp = ATTN_HEADS_PER_STEP
    grid = (heads // hp, nseq, nq)
    shared = group > 1
    assert group % hp == 0 if shared else (kc0 % hp == 0 and vc0 % hp == 0)

    def body(q_ref, k_ref, v_ref, o_ref, lse_ref):
        for j in range(hp):
            cols = slice(j * HEAD_W, (j + 1) * HEAD_W)
            kj = (k_ref[...] if shared else k_ref[:, cols]).astype(BF16)
            vj = (v_ref[...] if shared else v_ref[:, cols]).astype(BF16)
            s = lax.dot_general(q_ref[:, cols], kj, (((1,), (1,)), ((), ())), preferred_element_type=F32)
            m = jnp.max(s, axis=-1, keepdims=True)
            p = jnp.exp(s - m)
            l = jnp.sum(p, axis=-1, keepdims=True)
            o = jnp.dot(p.astype(BF16), vj, preferred_element_type=F32)
            o_ref[:, cols] = (o * (1.0 / l)).astype(o_ref.dtype)
            lse_ref[j] = m + jnp.log(l)

    if rider is not None:
        rider.n_core_out, rider.n_core_scratch = 2, 0
    body, x_in, x_spec, x_out, x_scr = _ride(body, grid, rider)
    q_spec = pl.BlockSpec((tq, hp * HEAD_W), lambda h, b, i: (b * nq + i, h))
    if shared:
        k_spec = pl.BlockSpec((seq, HEAD_W), lambda h, b, i: (b, kc0 + (h * hp) // group))
        v_spec = pl.BlockSpec((seq, HEAD_W), lambda h, b, i: (b, vc0 + (h * hp) // group))
    else:
        k_spec = pl.BlockSpec((seq, hp * HEAD_W), lambda h, b, i: (b, kc0 // hp + h))
        v_spec = pl.BlockSpec((seq, hp * HEAD_W), lambda h, b, i: (b, vc0 // hp + h))
    lse_spec = pl.BlockSpec((hp, tq, 1), lambda h, b, i: (h, b * nq + i, 0))
    sem = ("parallel",) * 3 if rider is None else ("arbitrary",) * 3
    return pl.pallas_call(
        body, out_shape=[jax.ShapeDtypeStruct(q.shape, BF16), jax.ShapeDtypeStruct((heads, q.shape[0], 1), F32)] + x_out,
        grid=grid, in_specs=[q_spec, k_spec, v_spec] + x_spec, out_specs=[q_spec, lse_spec] + x_spec, scratch_shapes=x_scr,
        compiler_params=_params(sem), name=name)(q, k, v, *x_in)


def _attn_bwd(name, q, k, kc0, v, vc0, o, do, lse, *, heads, group, nseq, seq, tq=1024, rider=None):
    tq = min(tq, seq)
    nq = seq // tq
    hk = heads // group
    t = q.shape[0]
    grid = (hk, nseq, group, nq)

    def body(q_ref, k_ref, v_ref, o_ref, do_ref, lse_ref, dq_ref, dk_ref, dv_ref, dk_acc, dv_acc):
        g, i = pl.program_id(2), pl.program_id(3)
        qv, kv, vv, dov = q_ref[...], k_ref[...].astype(BF16), v_ref[...].astype(BF16), do_ref[...]
        s = lax.dot_general(qv, kv, (((1,), (1,)), ((), ())), preferred_element_type=F32)
        pn = jnp.exp(s - lse_ref[...])
        dp = lax.dot_general(dov, vv, (((1,), (1,)), ((), ())), preferred_element_type=F32)
        delta = jnp.sum(dov.astype(F32) * o_ref[...].astype(F32), axis=-1, keepdims=True)
        ds = (pn * (dp - delta)).astype(BF16)
        dq_ref[...] = jnp.dot(ds, kv, preferred_element_type=F32)
        dk_part = lax.dot_general(ds, qv, (((0,), (0,)), ((), ())), preferred_element_type=F32)
        dv_part = lax.dot_general(pn.astype(BF16), dov, (((0,), (0,)), ((), ())), preferred_element_type=F32)
        first = jnp.logical_and(g == 0, i == 0)

        @pl.when(first)
        def _():
            dk_acc[...] = dk_part
            dv_acc[...] = dv_part

        @pl.when(jnp.logical_not(first))
        def _():
            dk_acc[...] += dk_part
            dv_acc[...] += dv_part

        @pl.when(jnp.logical_and(g == group - 1, i == nq - 1))
        def _():
            dk_ref[...] = dk_acc[...].astype(dk_ref.dtype)
            dv_ref[...] = dv_acc[...].astype(dv_ref.dtype)

    if rider is not None:
        rider.n_core_out, rider.n_core_scratch = 3, 2
    body, x_in, x_spec, x_out, x_scr = _ride(body, grid, rider)
    q_spec = pl.BlockSpec((tq, HEAD_W), lambda kh, b, g, i: (b * nq + i, kh * group + g))
    kv_out = pl.BlockSpec((seq, HEAD_W), lambda kh, b, g, i: (b, kh))
    lse_spec = pl.BlockSpec((None, tq, 1), lambda kh, b, g, i: (kh * group + g, b * nq + i, 0))
    sem = ("parallel", "parallel", "arbitrary", "arbitrary") if rider is None else ("arbitrary",) * 4
    return pl.pallas_call(
        body,
        out_shape=[jax.ShapeDtypeStruct(q.shape, F32), jax.ShapeDtypeStruct((t, hk * HEAD_W), BF16),
                   jax.ShapeDtypeStruct((t, hk * HEAD_W), BF16)] + x_out,
        grid=grid,
        in_specs=[q_spec, pl.BlockSpec((seq, HEAD_W), lambda kh, b, g, i: (b, kc0 + kh)),
                  pl.BlockSpec((seq, HEAD_W), lambda kh, b, g, i: (b, vc0 + kh)), q_spec, q_spec, lse_spec] + x_spec,
        out_specs=[q_spec, kv_out, kv_out] + x_spec,
        scratch_shapes=[pltpu.VMEM((seq, HEAD_W), F32), pltpu.VMEM((seq, HEAD_W), F32)] + x_scr,
        compiler_params=_params(sem), name=name)(q, k, v, o, do, lse, *x_in)


def _place():
    return lax.axis_index("x"), lax.axis_index("y"), lax.axis_index("c")


def _other_chips(x, y):
    return [(1 - x, y), (x, 1 - y), (1 - x, 1 - y)]


class _Exchange:
    def __init__(self, kind, srcs):
        assert kind in ("gather", "scatter")
        self.kind, self.srcs = kind, list(srcs)
        n = len(self.srcs)
        self.out_shapes = [jax.ShapeDtypeStruct((4, *a.shape[-2:]), a.dtype) for a in self.srcs]
        self.scratch = [pltpu.SemaphoreType.DMA((3 * n,)), pltpu.SemaphoreType.DMA((3 * n,)), pltpu.SemaphoreType.DMA((n,))]
        self.n_core_out = self.n_core_scratch = 0

    def _copies(self, j, src_ref, out_ref, send_sems, recv_sems, landing):
        x, y, c = _place()

        def remote(k, s, d, to):
            return pltpu.make_async_remote_copy(src_ref=s, dst_ref=d, send_sem=send_sems.at[3 * j + k], recv_sem=recv_sems.at[3 * j + k],
                                                device_id=to, device_id_type=MESH)

        me = 2 * x + y
        part = (lambda i: src_ref) if self.kind == "gather" else (lambda i: src_ref.at[i])
        if landing:
            return [remote(k, part(me), out_ref.at[2 * px + py], (px, py, c)) for k, (px, py) in enumerate(_other_chips(x, y))]
        return [remote(k, part(2 * px + py), out_ref.at[me], (px, py, c)) for k, (px, py) in enumerate(_other_chips(x, y))]

    def _local(self, j, src_ref, out_ref, local_sems):
        x, y, _ = _place()
        me = 2 * x + y
        return pltpu.make_async_copy(src_ref if self.kind == "gather" else src_ref.at[me], out_ref.at[me], local_sems.at[j])

    def start(self, src_refs, out_refs, send_sems, recv_sems, local_sems):
        for j, (src_ref, out_ref) in enumerate(zip(src_refs, out_refs)):
            self._local(j, src_ref, out_ref, local_sems).start()
            for mine in self._copies(j, src_ref, out_ref, send_sems, recv_sems, False):
                mine.start()

    def finish(self, src_refs, out_refs, send_sems, recv_sems, local_sems):
        for j, (src_ref, out_ref) in enumerate(zip(src_refs, out_refs)):
            for landed in self._copies(j, src_ref, out_ref, send_sems, recv_sems, True):
                landed.wait_recv()
        for j, (src_ref, out_ref) in enumerate(zip(src_refs, out_refs)):
            for mine in self._copies(j, src_ref, out_ref, send_sems, recv_sems, False):
                mine.wait_send()
            self._local(j, src_ref, out_ref, local_sems).wait()


def _gather_by_halves(name, srcs):
    n = len(srcs)

    def body(*refs):
        x, y, c = _place()
        me = 2 * x + y
        local_sems = refs[-1]
        copies = []
        for j in range(n):
            src_ref, out_ref, send_sems, recv_sems = refs[j], refs[n + j], refs[2 * n + 2 * j], refs[2 * n + 2 * j + 1]
            half = srcs[j].shape[0] // 2
            rows_c = pl.ds(pl.multiple_of(c * half, half), half)
            rows_s = pl.ds(pl.multiple_of((1 - c) * half, half), half)

            def remote(k, s_ref, d_ref, to, send_sems=send_sems, recv_sems=recv_sems):
                return pltpu.make_async_remote_copy(src_ref=s_ref, dst_ref=d_ref, send_sem=send_sems.at[k], recv_sem=recv_sems.at[k],
                                                    device_id=to, device_id_type=MESH)

            local = pltpu.make_async_copy(src_ref, out_ref.at[me], local_sems.at[j])
            local.start()
            chips = _other_chips(x, y)
            sent = [remote(k, src_ref.at[rows_c], out_ref.at[me, rows_c], (px, py, c)) for k, (px, py) in enumerate(chips)]
            landing = [remote(k, src_ref.at[rows_c], out_ref.at[2 * px + py, rows_c], (px, py, c)) for k, (px, py) in enumerate(chips)]
            passed = [remote(3 + k, out_ref.at[2 * px + py, rows_c], out_ref.at[2 * px + py, rows_c], (x, y, 1 - c))
                      for k, (px, py) in enumerate(chips)]
            from_sibling = [remote(3 + k, out_ref.at[2 * px + py, rows_s], out_ref.at[2 * px + py, rows_s], (x, y, 1 - c))
                            for k, (px, py) in enumerate(chips)]
            for cp in sent:
                cp.start()
            copies.append((local, sent, landing, passed, from_sibling))
        for local, sent, landing, passed, from_sibling in copies:
            for k in range(3):
                landing[k].wait_recv()
                passed[k].start()
        for local, sent, landing, passed, from_sibling in copies:
            for k in range(3):
                from_sibling[k].wait_recv()
            for cp in sent + passed:
                cp.wait_send()
            local.wait()

    sems = [pltpu.SemaphoreType.DMA((6,)) for _ in range(2 * n)] + [pltpu.SemaphoreType.DMA((n,))]
    return pl.pallas_call(
        body, out_shape=[jax.ShapeDtypeStruct((4, *a.shape), a.dtype) for a in srcs],
        in_specs=[pl.BlockSpec(memory_space=pl.ANY)] * n, out_specs=[pl.BlockSpec(memory_space=pltpu.VMEM)] * n,
        scratch_shapes=sems, compiler_params=pltpu.CompilerParams(vmem_limit_bytes=VMEM_LIMIT_BYTES), name=name)(*srcs)


def _adamw(w, g, m, v):
    m = ADAM_B1 * m + (1.0 - ADAM_B1) * g
    v = ADAM_B2 * v + (1.0 - ADAM_B2) * (g * g)
    delta = -ADAM_LR * ((m / M_HAT_DIV) / (jnp.sqrt(v / V_HAT_DIV) + ADAM_EPS) + ADAM_WD * w)
    return delta, m, v


def _small_allreduce_adamw(part, w, m, v):
    def body(part_ref, w_ref, m_ref, v_ref, g_out, d_out, m_out, v_out, loss_out, buf, send_sems, recv_sems):
        x, y, c = _place()
        me = 4 * x + 2 * y + c
        buf[me] = part_ref[...]

        def flip(k):
            fx, fy, fc = (k >> 2) & 1, (k >> 1) & 1, k & 1
            px, py, pc = (1 - x if fx else x), (1 - y if fy else y), (1 - c if fc else c)
            return (px, py, pc), 4 * px + 2 * py + pc

        def copy(k, slot):
            return pltpu.make_async_remote_copy(
                src_ref=part_ref, dst_ref=buf.at[slot], send_sem=send_sems.at[k - 1], recv_sem=recv_sems.at[k - 1],
                device_id=flip(k)[0], device_id_type=MESH)

        sent = [copy(k, me) for k in range(1, 8)]
        for cp in sent:
            cp.start()
        for k in range(1, 8):
            copy(k, flip(k)[1]).wait_recv()
        for cp in sent:
            cp.wait_send()
        tot = buf[0]
        for j in range(1, 8):
            tot = tot + buf[j]
        delta, m_new, v_new = _adamw(w_ref[...], tot, m_ref[...], v_ref[...])
        g_out[...] = tot
        d_out[...] = delta
        m_out[...] = m_new
        v_out[...] = v_new
        loss_out[...] = jnp.sum(tot[LOSS_ROW0:LOSS_ROW0 + 8, :]).reshape(1, 1)

    vm = pl.BlockSpec(memory_space=pltpu.VMEM)
    shp = jax.ShapeDtypeStruct((SMALL_ROWS, 128), F32)
    return pl.pallas_call(
        body, out_shape=[shp, shp, shp, shp, jax.ShapeDtypeStruct((1, 1), F32)],
        in_specs=[vm, vm, vm, vm], out_specs=[vm, vm, vm, vm, vm],
        scratch_shapes=[pltpu.VMEM((8, SMALL_ROWS, 128), F32), pltpu.SemaphoreType.DMA((7,)), pltpu.SemaphoreType.DMA((7,))],
        name="small_allreduce_adamw")(part, w, m, v)


def _row_tile(rows, cap):
    return max(t for t in range(16, min(rows, cap) + 1, 16) if rows % t == 0)


def _reduce_pair(name, parts):
    _, rows, w = parts.shape
    tr = _row_tile(rows, 576)
    nt = rows // tr

    def body(p_ref, o_ref, mine, theirs, send_sems, recv_sems):
        i = pl.program_id(0)
        x, y, c = _place()

        def copy(t):
            rows_t = pl.ds(pl.multiple_of(t * tr, tr), tr)
            return pltpu.make_async_remote_copy(src_ref=mine.at[rows_t], dst_ref=theirs.at[rows_t], send_sem=send_sems.at[t],
                                                recv_sem=recv_sems.at[t], device_id=(x, y, 1 - c), device_id_type=MESH)

        @pl.when(i < nt)
        def _():
            mine[pl.ds(pl.multiple_of(i * tr, tr), tr), :] = (
                (p_ref[0].astype(F32) + p_ref[1].astype(F32)) + p_ref[2].astype(F32)) + p_ref[3].astype(F32)
            copy(i).start()

        @pl.when(i >= nt)
        def _():
            t = i - nt
            copy(t).wait()
            rows_t = pl.ds(pl.multiple_of(t * tr, tr), tr)
            o_ref[...] = mine[rows_t, :] + theirs[rows_t, :]

    return pl.pallas_call(
        body, out_shape=jax.ShapeDtypeStruct((rows, w), F32), grid=(2 * nt,),
        in_specs=[pl.BlockSpec((4, tr, w), lambda i: (0, jnp.minimum(i, nt - 1), 0))],
        out_specs=pl.BlockSpec((tr, w), lambda i: (jnp.maximum(i - nt, 0), 0)),
        scratch_shapes=[pltpu.VMEM((rows, w), F32), pltpu.VMEM((rows, w), F32), pltpu.SemaphoreType.DMA((nt,)),
                        pltpu.SemaphoreType.DMA((nt,))],
        compiler_params=_params(("arbitrary",)), name=name)(parts)


def _adamw_shard(name, g, w, m, v):
    rows, cols = w.shape
    tr = _row_tile(rows, 256)

    def body(g_ref, w_ref, m_ref, v_ref, d_out, m_out, v_out):
        delta, m_new, v_new = _adamw(w_ref[...], g_ref[...], m_ref[...], v_ref[...])
        d_out[...] = delta
        m_out[...] = m_new
        v_out[...] = v_new

    t_spec = pl.BlockSpec((tr, cols), lambda i: (i, 0))
    shp = jax.ShapeDtypeStruct((rows, cols), F32)
    return pl.pallas_call(body, out_shape=[shp] * 3, grid=(rows // tr,), in_specs=[t_spec] * 4, out_specs=[t_spec] * 3,
                          compiler_params=_params(("parallel",)), name=name)(g, w, m, v)


def _shard_shape(name):
    _, r, c, ax = BIG_BY_NAME[name]
    return (r, c // 4) if ax == 1 else (r // 4, c)


def _pad_rows(a, axis):
    pad = [(0, 0)] * a.ndim
    pad[axis] = (0, -a.shape[axis] % PACK_ALIGN)
    return jnp.pad(a, pad)


def _pack_shards(names, shards, dtype):
    return _pad_rows(jnp.concatenate([s.astype(dtype).reshape(-1, PACK_W) for s in shards], axis=0), 0)


def _unpack_shards(names, slab):
    out, off = [], 0
    for name in names:
        rs, cs = _shard_shape(name)
        n = rs * cs // PACK_W
        out.append(slab[off:off + n].reshape(rs, cs))
        off += n
    return out


def _unpack_full(names, slabs):
    out, off = [], 0
    for name in names:
        _, r, c, ax = BIG_BY_NAME[name]
        n = r * c // 4 // PACK_W
        seg = slabs[:, off:off + n]
        out.append(seg.reshape(4, r, c // 4).transpose(1, 0, 2).reshape(r, c) if ax == 1 else seg.reshape(r, c))
        off += n
    return out


def _pack_full(names, mats, dtype):
    segs = []
    for name, a in zip(names, mats):
        _, r, c, ax = BIG_BY_NAME[name]
        a = a.astype(dtype)
        a = a.reshape(r, 4, c // 4).transpose(1, 0, 2) if ax == 1 else a
        segs.append(a.reshape(4, -1, PACK_W))
    return _pad_rows(jnp.concatenate(segs, axis=1), 1)


def _pad_heads_cols(wm, heads, d):
    k = wm.shape[0]
    return jnp.pad(wm.reshape(k, heads, d), ((0, 0), (0, 0), (0, HEAD_W - d))).reshape(k, heads * HEAD_W)


def _unpad_heads_cols(wm, heads, d):
    k = wm.shape[0]
    return wm.reshape(k, heads, HEAD_W)[:, :, :d].reshape(k, heads * d)


def _win_ext(w_in):
    o = np.cumsum([0, Q_LORA, KV_LORA, QK_ROPE, H_B * HD_B, KV_B * HD_B, KV_B * HD_B, D_MODEL, D_MODEL])
    pc = lambda a, n: jnp.pad(a, ((0, 0), (0, n - a.shape[1])))
    return jnp.concatenate([
        _pad_heads_cols(w_in[:, o[3]:o[4]], H_B, HD_B), w_in[:, o[0]:o[1]], w_in[:, o[1]:o[2]], pc(w_in[:, o[2]:o[3]], HEAD_W),
        _pad_heads_cols(w_in[:, o[4]:o[5]], KV_B, HD_B), _pad_heads_cols(w_in[:, o[5]:o[6]], KV_B, HD_B),
        w_in[:, o[6]:o[7]], w_in[:, o[7]:o[8]]], axis=1)


def _win_unext(we):
    c = HEAD_W
    return jnp.concatenate([
        we[:, ZC_QLAT * c:ZC_CKV * c], we[:, ZC_CKV * c:ZC_KPE * c], we[:, ZC_KPE * c:ZC_KPE * c + QK_ROPE],
        _unpad_heads_cols(we[:, ZC_QB * c:ZC_QLAT * c], H_B, HD_B), _unpad_heads_cols(we[:, ZC_KB * c:ZC_VB * c], KV_B, HD_B),
        _unpad_heads_cols(we[:, ZC_VB * c:ZC_GA * c], KV_B, HD_B), we[:, ZC_GA * c:]], axis=1)


def _wkv_ext(w_kvb):
    wk = w_kvb.reshape(KV_LORA, H_A, QK_NOPE + V_DIM_A)
    k_cols = jnp.pad(wk[:, :, :QK_NOPE], ((0, 0), (0, 0), (0, HEAD_W - QK_NOPE))).reshape(KV_LORA, H_A * HEAD_W)
    v_cols = jnp.pad(wk[:, :, QK_NOPE:], ((0, 0), (0, 0), (0, HEAD_W - V_DIM_A))).reshape(KV_LORA, H_A * HEAD_W)
    eye = jnp.pad(jnp.eye(QK_ROPE, dtype=w_kvb.dtype), ((0, 0), (QK_NOPE, HEAD_W - QK_NOPE - QK_ROPE)))
    pe_rows = jnp.concatenate([jnp.tile(eye, (1, H_A)), jnp.zeros((QK_ROPE, H_A * HEAD_W), w_kvb.dtype)], axis=1)
    top = jnp.concatenate([k_cols, v_cols], axis=1)
    return jnp.concatenate([top, pe_rows, jnp.zeros((2 * HEAD_W - KV_LORA - QK_ROPE, 2 * H_A * HEAD_W), w_kvb.dtype)], axis=0)


def _wkv_unext(we):
    k_cols = we[:KV_LORA, :H_A * HEAD_W].reshape(KV_LORA, H_A, HEAD_W)[:, :, :QK_NOPE]
    v_cols = we[:KV_LORA, H_A * HEAD_W:].reshape(KV_LORA, H_A, HEAD_W)[:, :, :V_DIM_A]
    return jnp.concatenate([k_cols, v_cols], axis=2).reshape(KV_LORA, H_A * (QK_NOPE + V_DIM_A))


def _pad_heads_rows(wm, heads, d):
    n = wm.shape[1]
    return jnp.pad(wm.reshape(heads, d, n), ((0, 0), (0, HEAD_W - d), (0, 0))).reshape(heads * HEAD_W, n)


def _unpad_heads_rows(wm, heads, d):
    n = wm.shape[1]
    return wm.reshape(heads, HEAD_W, n)[:, :d].reshape(heads * d, n)


def _rope_tables(seq):
    def ang(pos, dim):
        inv = np.float32(ROPE_THETA) ** (-np.arange(0, dim, 2, dtype=np.float32) / np.float32(dim))
        return pos.astype(np.float32)[:, None] * inv[None, :]

    def rot(dim):
        r = np.zeros((dim, dim), np.float32)
        half = dim // 2
        r[np.arange(half) + half, np.arange(half)] = -1.0
        r[np.arange(half), np.arange(half) + half] = 1.0
        return r

    def table(blocks):
        cos, sin = np.ones((seq, HEAD_W), np.float32), np.zeros((seq, HEAD_W), np.float32)
        pm = np.zeros((HEAD_W, HEAD_W), np.float32)
        for c0, a in blocks:
            d = 2 * a.shape[1]
            cos[:, c0:c0 + d] = np.concatenate([np.cos(a), np.cos(a)], axis=1)
            sin[:, c0:c0 + d] = np.concatenate([np.sin(a), np.sin(a)], axis=1)
            pm[c0:c0 + d, c0:c0 + d] = rot(d)
        return jnp.asarray(cos), jnp.asarray(sin), jnp.asarray(pm, BF16), jnp.asarray(pm.T, BF16)

    tok = np.arange(seq)
    a1 = ang(tok, QK_ROPE)
    arow, acol = ang(tok // GRID_W, HD_B // 2), ang(tok % GRID_W, HD_B // 2)
    return table([(QK_NOPE, a1)]), table([(0, a1)]), table([(0, arow), (HD_B // 2, acol)])


def _local_step(x, p, tgt, gains, wts, ride=None):
    nb, seq, _ = x.shape
    t = nb * seq
    x0 = x.reshape(t, D_MODEL)
    p2 = p.reshape(t, PLE_DIM)
    tg = tgt.reshape(t, D_MODEL)
    (cq_t, sq_t, pq, pq_t), (ck_t, sk_t, pk, pk_t), (cb_t, sb_t, pb, pb_t) = _rope_tables(seq)
    padg = lambda g: jnp.pad(g, ((0, 0), (0, HEAD_W - g.shape[1])))
    g_qn, g_kn = padg(gains["g_qn"]), padg(gains["g_kn"])

    win = _win_ext(wts["w_in"])
    wqb = _pad_heads_cols(wts["w_qb"], H_A, QK_NOPE + QK_ROPE)
    wkv = _wkv_ext(wts["w_kvb"])

    norm = lambda n: (lambda v, g: (_rms(v, g, n),))
    full = lambda a: (a, a.shape[1], 0, False)
    wts = dict(wts)
    rider_of = lambda kernel_name: None if ride is None else ride["gather"][kernel_name][0]

    def landed(kernel_name, got):
        if ride is not None:
            wts.update(ride["gather"][kernel_name][1](got))

    h = _rowwise("norm_mix", norm(D_MODEL), [full(x0)], [(D_MODEL, BF16, D_MODEL, False)], consts=[gains["g_mix"]])
    res = _mm("in_proj", h, win, tn=2048, rider=rider_of("in_proj"))
    z, got = (res, []) if ride is None else (res[0], res[1:])
    landed("in_proj", got)
    cq = _rowwise("norm_qa", norm(Q_LORA), [(z, Q_LORA, ZC_QLAT // 2, False)], [(Q_LORA, BF16, Q_LORA, False)],
                  consts=[gains["g_qa"]])
    qa = _mm("q_up", cq, wqb)

    def rope_fwd(scale):
        return lambda v, cos, sin, pm: ((v * cos + _perm(v, pm) * sin) * scale,)

    heads_tile = lambda n: (n * HEAD_W, BF16, n * HEAD_W, False)
    q_a = _rowwise("rope_qa", _per_head(rope_fwd(SCALE_A), H_A, 1, 1), [full(qa)], [heads_tile(H_A)],
                   pos=[cq_t, sq_t], consts=[pq], seq=seq)
    ckv = _rowwise("norm_kva", norm(KV_LORA), [(z, HEAD_W, ZC_CKV, False)], [(HEAD_W, BF16, HEAD_W, False)], consts=[gains["g_kva"]])
    kpe = _rowwise("rope_kpe", rope_fwd(1.0), [(z, HEAD_W, ZC_KPE, False)], [(HEAD_W, BF16, HEAD_W, False)],
                   pos=[ck_t, sk_t], consts=[pk], seq=seq)
    kin = jnp.concatenate([ckv, kpe], axis=1)
    kv_a = _mm("kv_up", kin, wkv, out_dtypes=(BF16,))
    o_a, lse_a, *got = _attn_fwd("attn_a_fwd", q_a, kv_a, 0, kv_a, H_A, heads=H_A, group=1, nseq=nb, seq=seq,
                                 rider=rider_of("attn_a_fwd"))
    landed("attn_a_fwd", got)

    def prep_fwd(scale):
        def fn(v, cos, sin, g, pm):
            yv = _rms(v, g, HD_B)
            return ((yv * cos + _perm(yv, pm) * sin) * scale,)
        return fn

    z_qb, z_kb = (z, H_B * HEAD_W, ZC_QB // H_B, False), (z, KV_B * HEAD_W, ZC_KB // KV_B, False)
    q_b = _rowwise("prep_qb", _per_head(prep_fwd(SCALE_B), H_B, 1, 1), [z_qb], [heads_tile(H_B)],
                   pos=[cb_t, sb_t], consts=[g_qn, pb], seq=seq)
    k_b = _rowwise("prep_kb", _per_head(prep_fwd(1.0), KV_B, 1, 1), [z_kb], [heads_tile(KV_B)],
                   pos=[cb_t, sb_t], consts=[g_kn, pb], seq=seq)
    o_b, lse_b, *got = _attn_fwd("attn_b_fwd", q_b, k_b, 0, z, ZC_VB, heads=H_B, group=H_B // KV_B, nseq=nb, seq=seq,
                                 rider=rider_of("attn_b_fwd"))
    landed("attn_b_fwd", got)
    woa = _pad_heads_rows(wts["w_oa"], H_A, V_DIM_A)
    wob = _pad_heads_rows(wts["w_ob"], H_B, HD_B)
    wo, wup, wdown, wple = wts["w_o"], wts["w_up"], wts["w_down"], wts["w_ple"]

    def residual_norm(acc, r, g):
        xv = r + acc
        return xv, _rms(xv, g, D_MODEL)

    def mix_out(oa, ob, ga, gb, r, g, w_a, w_b, w_out):
        a = jnp.dot(oa, w_a[...], preferred_element_type=F32)
        b = jnp.dot(ob, w_b[...], preferred_element_type=F32)
        mg = (_sigmoid(ga) * a + _sigmoid(gb) * b).astype(BF16)
        return (a, b, mg, *residual_norm(jnp.dot(mg, w_out[...], preferred_element_type=F32), r, g))

    z_ga, z_gb = (z, D_MODEL, ZC_GA // 8, False), (z, D_MODEL, ZC_GB // 8, False)
    wide = lambda d: (D_MODEL, d, D_MODEL, False)
    ya, yb, merged, x1, h2, *got = _rowwise("mix_out", mix_out, [full(o_a), full(o_b), z_ga, z_gb, full(x0)],
                                            [wide(F32), wide(F32), wide(BF16), wide(F32), wide(BF16)],
                                            consts=[gains["g_mlp"], woa, wob, wo], tm=256, rider=rider_of("mix_out"))
    landed("mix_out", got)
    wpg = wts["w_ple_gate"]

    def relu2(acc):
        u = jnp.maximum(acc, 0.0)
        return u, u * u

    u, usq = _mm("mlp_up", h2, wup, out_dtypes=(BF16, BF16), epi=relu2, tn=2048)
    x2, h3 = _mm("mlp_down", usq, wdown, out_dtypes=(F32, BF16), epi=residual_norm, extras=(x1,), consts=[gains["g_ple"]])

    def tail(x2v, h3v, pv, tv, gf, w_gate, w_emb):
        sg = _sigmoid(jnp.dot(h3v, w_gate[...], preferred_element_type=F32))
        pev = jnp.dot(pv.astype(BF16), w_emb[...], preferred_element_type=F32)
        x3 = x2v + sg * pev
        rs = lax.rsqrt(jnp.sum(x3 * x3, axis=-1, keepdims=True) * (1.0 / D_MODEL) + EPS)
        xh = x3 * rs
        err = xh * gf - tv
        dy = err * (1.0 / D_MODEL)
        dyg = dy * gf
        dx3 = rs * (dyg - xh * (jnp.sum(dyg * xh, axis=-1, keepdims=True) * (1.0 / D_MODEL)))
        return (dx3, dx3 * pev * sg * (1.0 - sg), dx3 * sg,
                jnp.sum(err * err, axis=0, keepdims=True) * (0.5 / D_MODEL), jnp.sum(dy * xh, axis=0, keepdims=True))

    dx3, dgpre, dpe, loss_part, dg_final = _rowwise(
        "tail", tail, [full(x2), full(h3), full(p2), full(tg)], [wide(F32), wide(BF16), wide(BF16)],
        consts=[gains["g_final"].reshape(1, D_MODEL), wpg, wple], accs=[(1, D_MODEL), (1, D_MODEL)], tm=256)

    def norm_bwd(n, with_res):
        if with_res:
            def fn(dh, v, res, g):
                dx, dg = _rms_bwd(dh, v, g, n)
                return dx + res, dg
        else:
            def fn(dh, v, g):
                return _rms_bwd(dh, v, g, n)
        return fn

    dw = {}
    dw["w_ple"] = _mm_tn("dw_ple", p2, dpe)
    dw["w_ple_gate"] = _mm_tn("dw_ple_gate", h3, dgpre)
    norm_res_bwd = norm_bwd(D_MODEL, True)
    dx2, dg_ple = _mm("d_ple_gate", dgpre, wpg, trans_b=True, epi=norm_res_bwd, extras=(x2, dx3), consts=[gains["g_ple"]],
                      accs=[(1, D_MODEL)], tm=256)
    dw["w_down"] = _mm_tn("dw_down", usq, dx2)
    dupre = _mm("d_mlp_down", dx2, wdown, trans_b=True, out_dtypes=(BF16,), epi=lambda acc, uv: (acc * (2.0 * uv.astype(F32)),),
                extras=(u,), tn=2048)
    dw["w_up"] = _mm_tn("dw_up", h2, dupre)
    dx1, dg_mlp = _mm("d_mlp_up", dupre, wup, trans_b=True, epi=norm_res_bwd, extras=(x1, dx2), consts=[gains["g_mlp"]],
                      accs=[(1, D_MODEL)], tm=256)
    dw["w_o"] = _mm_tn("dw_o", merged, dx1)

    def merge_bwd(dm, ga, gb, a, b, w_a, w_b):
        sa, sb = _sigmoid(ga), _sigmoid(gb)
        da, db = (dm * sa).astype(BF16), (dm * sb).astype(BF16)
        nt = (((1,), (1,)), ((), ()))
        return (da, db, dm * a * sa * (1.0 - sa), dm * b * sb * (1.0 - sb),
                lax.dot_general(da, w_a, nt, preferred_element_type=F32), lax.dot_general(db, w_b, nt, preferred_element_type=F32))

    dya, dyb, dga, dgb, do_a, do_b = _mm("d_out_proj", dx1, wo, trans_b=True, out_dtypes=(BF16,) * 6, epi=merge_bwd,
                                         extras=((z, ZC_GA // 8), (z, ZC_GB // 8), ya, yb), consts=[woa, wob], tm=256)
    dw["w_oa"] = _unpad_heads_rows(_mm_tn("dw_oa", o_a, dya), H_A, V_DIM_A)
    dw["w_ob"] = _unpad_heads_rows(_mm_tn("dw_ob", o_b, dyb), H_B, HD_B)
    res_a = _attn_bwd("attn_a_bwd", q_a, kv_a, 0, kv_a, H_A, o_a, do_a, lse_a, heads=H_A, group=1, nseq=nb, seq=seq,
                      rider=ride and ride["scatter_a"](dw))
    dq_a, dk_a, dv_a = res_a[:3]
    if ride is not None:
        ride["out"]["parts_a"] = res_a[3:]

    def rope_bwd(scale):
        return lambda d, cos, sin, pm_t: ((d * cos + _perm(d * sin, pm_t)) * scale,)

    dqa = _rowwise("rope_qa_bwd", _per_head(rope_bwd(SCALE_A), H_A, 1, 1), [full(dq_a)], [heads_tile(H_A)],
                   pos=[cq_t, sq_t], consts=[pq_t], seq=seq)
    dw["w_qb"] = _unpad_heads_cols(_mm_tn("dw_qb", cq, dqa), H_A, QK_NOPE + QK_ROPE)
    dcq = _mm("d_q_up", dqa, wqb, trans_b=True)
    dq_lat, dg_qa = _rowwise("norm_qa_bwd", norm_bwd(Q_LORA, False), [full(dcq), (z, Q_LORA, ZC_QLAT // 2, False)],
                             [(Q_LORA, BF16, Q_LORA, False)], consts=[gains["g_qa"]], accs=[(1, Q_LORA)])
    dkv_a = jnp.concatenate([dk_a, dv_a], axis=1)
    dw["w_kvb"] = _wkv_unext(_mm_tn("dw_kv", kin, dkv_a))
    dq_b, dk_b, dv_b, *parts_b = _attn_bwd("attn_b_bwd", q_b, k_b, 0, z, ZC_VB, o_b, do_b, lse_b, heads=H_B, group=H_B // KV_B,
                                               nseq=nb, seq=seq, rider=ride and ride["scatter_b"](dw))
    if ride is not None:
        ride["out"]["parts_b"] = parts_b
    dkin = _mm("d_kv_up", dkv_a, wkv, trans_b=True)
    dckv, dg_kva = _rowwise("norm_kva_bwd", norm_bwd(KV_LORA, False), [(dkin, HEAD_W, 0, False), (z, HEAD_W, ZC_CKV, False)],
                            [(HEAD_W, BF16, HEAD_W, False)], consts=[gains["g_kva"]], accs=[(1, KV_LORA)])
    dkpe = _rowwise("rope_kpe_bwd", rope_bwd(1.0), [(dkin, HEAD_W, 1, False)], [(HEAD_W, BF16, HEAD_W, False)],
                    pos=[ck_t, sk_t], consts=[pk_t], seq=seq)

    def prep_bwd(scale):
        def fn(d, v, cos, sin, g, pm_t):
            dyv = (d * cos + _perm(d * sin, pm_t)) * scale
            return _rms_bwd(dyv, v, g, HD_B)
        return fn

    dqb, dg_qn = _rowwise("prep_qb_bwd", _per_head(prep_bwd(SCALE_B), H_B, 2, 1), [full(dq_b), z_qb], [heads_tile(H_B)],
                          pos=[cb_t, sb_t], consts=[g_qn, pb_t], accs=[(1, HEAD_W)], seq=seq)
    dkb, dg_kn = _rowwise("prep_kb_bwd", _per_head(prep_bwd(1.0), KV_B, 2, 1), [full(dk_b), z_kb], [heads_tile(KV_B)],
                          pos=[cb_t, sb_t], consts=[g_kn, pb_t], accs=[(1, HEAD_W)], seq=seq)

    dz = jnp.concatenate([dqb, dq_lat, dckv, dkpe, dkb, dv_b, dga, dgb], axis=1)
    dw["w_in"] = _win_unext(_mm_tn("dw_in", h, dz))
    dx0, dg_mix, *parts_in = _mm("d_in_proj", dz, win, trans_b=True, epi=norm_res_bwd, extras=(x0, dx1), consts=[gains["g_mix"]],
                                 accs=[(1, D_MODEL)], tm=256, rider=ride and ride["scatter_in"](dw))
    if ride is not None:
        ride["out"]["parts_in"] = parts_in

    dg = {"g_mix": dg_mix, "g_qa": dg_qa, "g_kva": dg_kva, "g_qn": dg_qn[:, :HD_B], "g_kn": dg_kn[:, :HD_B],
          "g_mlp": dg_mlp, "g_ple": dg_ple, "g_final": dg_final}
    return loss_part, dx0.reshape(nb, seq, D_MODEL), dg, dw


def _pack_small(vals, loss_part=None):
    flat = jnp.concatenate([vals[n].reshape(1, -1) for n, _ in SMALL], axis=1)
    loss = jnp.zeros((1, 8 * 128), F32) if loss_part is None else loss_part
    gap = jnp.zeros((1, LOSS_ROW0 * 128 - SMALL_N), F32)
    return jnp.concatenate([flat, gap, loss], axis=1).reshape(SMALL_ROWS, 128)


def _unpack_small(slab, like):
    flat, out, off = slab.reshape(-1), {}, 0
    for n, k in SMALL:
        out[n] = flat[off:off + k].reshape(like[n].shape)
        off += k
    return out


def kernel(x, p, g_mix, w_in, g_qa, w_qb, g_kva, w_kvb, g_qn, g_kn, w_oa, w_ob, w_o, g_mlp, w_up, w_down, g_ple, w_ple_gate, w_ple, g_final, loss_target, m_g_mix, m_w_in, m_g_qa, m_w_qb, m_g_kva, m_w_kvb, m_g_qn, m_g_kn, m_w_oa, m_w_ob, m_w_o, m_g_mlp, m_w_up, m_w_down, m_g_ple, m_w_ple_gate, m_w_ple, m_g_final, v_g_mix, v_w_in, v_g_qa, v_w_qb, v_g_kva, v_w_kvb, v_g_qn, v_g_kn, v_w_oa, v_w_ob, v_w_o, v_g_mlp, v_w_up, v_w_down, v_g_ple, v_w_ple_gate, v_w_ple, v_g_final):
    given = dict(locals())
    order = ["g_mix", "w_in", "g_qa", "w_qb", "g_kva", "w_kvb", "g_qn", "g_kn", "w_oa", "w_ob", "w_o", "g_mlp", "w_up",
             "w_down", "g_ple", "w_ple_gate", "w_ple", "g_final"]
    big_names = [n for n, _, _, _ in BIG]
    local = lambda prefix, names: [given[prefix + n][0] for n in names]
    slab = lambda names: _pack_shards(names, local("", names), BF16)
    bf = lambda n: given[n][0].astype(BF16)
    cols_full = lambda g: g.transpose(1, 0, 2).reshape(g.shape[1], -1)
    rows_full = lambda g: g.reshape(-1, g.shape[2])
    shards_cols = lambda a: a.reshape(a.shape[0], 4, a.shape[1] // 4).transpose(1, 0, 2)
    shards_rows = lambda a: a.reshape(4, a.shape[0] // 4, a.shape[1])
    packed = lambda names, dw: _pack_full(names, [dw[n] for n in names], BF16)

    got_in, got_early = _gather_by_halves("weight_gather_early", [bf("w_in"), slab(SLAB_EARLY)])
    wts = {"w_in": cols_full(got_in), **dict(zip(SLAB_EARLY, _unpack_full(SLAB_EARLY, got_early)))}
    gains = {n: given[n].reshape(1, -1) for n, _ in SMALL}
    ride = {
        "gather": {
            "in_proj": (_Exchange("gather", [slab(SLAB_LATE), bf("w_o")]),
                        lambda got: {**dict(zip(SLAB_LATE, _unpack_full(SLAB_LATE, got[0]))), "w_o": rows_full(got[1])}),
            "attn_a_fwd": (_Exchange("gather", [bf("w_up")]), lambda got: {"w_up": cols_full(got[0])}),
            "attn_b_fwd": (_Exchange("gather", [bf("w_down")]), lambda got: {"w_down": rows_full(got[0])}),
            "mix_out": (_Exchange("gather", [bf("w_ple_gate")]), lambda got: {"w_ple_gate": rows_full(got[0])}),
        },
        "scatter_a": lambda dw: _Exchange("scatter", [shards_cols(dw["w_up"]), shards_rows(dw["w_o"]), packed(SLAB_LATE, dw)]),
        "scatter_b": lambda dw: _Exchange("scatter", [shards_rows(dw["w_down"]), shards_rows(dw["w_ple_gate"]), packed(SLAB_EARLY, dw)]),
        "scatter_in": lambda dw: _Exchange("scatter", [shards_cols(dw["w_in"])]),
        "out": {},
    }
    loss_part, grad_x, dg, dw = _local_step(x, p[0], loss_target, gains, wts, ride)

    small = lambda prefix: _pack_small({n: given[prefix + n] for n, _ in SMALL})
    g_s, d_s, m_s, v_s, loss = _small_allreduce_adamw(_pack_small(dg, loss_part), small(""), small("m_"), small("v_"))

    parts = ride["out"]
    grads = {"w_up": _reduce_pair("grad_reduce_up", parts["parts_a"][0]), "w_o": _reduce_pair("grad_reduce_o", parts["parts_a"][1]),
             "w_down": _reduce_pair("grad_reduce_down", parts["parts_b"][0]),
             "w_ple_gate": _reduce_pair("grad_reduce_ple_gate", parts["parts_b"][1]),
             "w_in": _reduce_pair("grad_reduce_in", parts["parts_in"][0])}
    grads.update(zip(SLAB_LATE, _unpack_shards(SLAB_LATE, _reduce_pair("grad_reduce_slab_late", parts["parts_a"][2]))))
    grads.update(zip(SLAB_EARLY, _unpack_shards(SLAB_EARLY, _reduce_pair("grad_reduce_slab_early", parts["parts_b"][2]))))

    res = {}
    for key, slab in (("grad_", g_s), ("delta_", d_s), ("new_m_", m_s), ("new_v_", v_s)):
        for n, val in _unpack_small(slab, given).items():
            res[key + n] = val
    for n in big_names:
        d_w, m_w, v_w = _adamw_shard("adamw_" + n, grads[n], given[n][0], given["m_" + n][0], given["v_" + n][0])
        res["grad_" + n], res["delta_" + n], res["new_m_" + n], res["new_v_" + n] = grads[n][None], d_w[None], m_w[None], v_w[None]
    outs = [loss.reshape(()), grad_x]
    for key in ("grad_", "delta_", "new_m_", "new_v_"):
        outs += [res[key + n] for n in order]
    return tuple(outs)
```

```python
import functools

import numpy as np
import jax
import jax.numpy as jnp
from jax import lax
from jax.experimental import pallas as pl
from jax.experimental.pallas import tpu as pltpu

F32 = jnp.float32
BF16 = jnp.bfloat16
MESH = pl.DeviceIdType.MESH

D_MODEL = 1024
GRID_W = 64
ROPE_THETA = 10000.0
EPS = 1e-6
H_A, QK_NOPE, QK_ROPE, V_DIM_A, Q_LORA, KV_LORA = 8, 64, 32, 64, 256, 128
H_B, KV_B, HD_B = 8, 2, 64
D_FF = 4096
PLE_DIM = 256
HEAD_W = 128
SCALE_A = (QK_NOPE + QK_ROPE) ** -0.5
SCALE_B = HD_B ** -0.5

ADAM_LR, ADAM_B1, ADAM_B2, ADAM_EPS, ADAM_WD, ADAM_STEP = 0.001, 0.9, 0.999, 1e-08, 0.01, 10
M_HAT_DIV = 1.0 - ADAM_B1 ** ADAM_STEP
V_HAT_DIV = 1.0 - ADAM_B2 ** ADAM_STEP

VMEM_LIMIT_BYTES = 56 * 1024 * 1024

ZC_QB, ZC_QLAT, ZC_CKV, ZC_KPE, ZC_KB, ZC_VB, ZC_GA, ZC_GB = 0, 8, 10, 11, 12, 14, 16, 24
Z_WIDTH = 32 * HEAD_W

BIG = [
    ("w_in", 1024, 3232, 1), ("w_qb", 256, 768, 1), ("w_kvb", 128, 1024, 1), ("w_oa", 512, 1024, 1),
    ("w_ob", 512, 1024, 1), ("w_o", 1024, 1024, 0), ("w_up", 1024, 4096, 1), ("w_down", 4096, 1024, 0),
    ("w_ple_gate", 1024, 1024, 0), ("w_ple", 256, 1024, 1),
]
BIG_BY_NAME = {e[0]: e for e in BIG}
PACK_W = 1024
PACK_ALIGN = 64
SLAB_EARLY = ["w_qb", "w_kvb"]
SLAB_LATE = ["w_oa", "w_ob", "w_ple"]

SMALL = [("g_mix", 1024), ("g_qa", 256), ("g_kva", 128), ("g_qn", 64), ("g_kn", 64), ("g_mlp", 1024),
         ("g_ple", 1024), ("g_final", 1024)]
SMALL_N = sum(n for _, n in SMALL)
LOSS_ROW0 = 40
SMALL_ROWS = 48


def _params(sem):
    return pltpu.CompilerParams(dimension_semantics=sem, vmem_limit_bytes=VMEM_LIMIT_BYTES)


def _sigmoid(v):
    return 1.0 / (1.0 + jnp.exp(-v))


def _perm(v, p_ref):
    pm = p_ref[...]
    hi = v.astype(BF16)
    lo = (v - hi.astype(F32)).astype(BF16)
    return (jnp.dot(hi, pm, preferred_element_type=F32) + jnp.dot(lo, pm, preferred_element_type=F32))


def _rms(v, g, n):
    rs = lax.rsqrt(jnp.sum(v * v, axis=-1, keepdims=True) * (1.0 / n) + EPS)
    return v * rs * g


def _rms_bwd(dy, v, g, n):
    rs = lax.rsqrt(jnp.sum(v * v, axis=-1, keepdims=True) * (1.0 / n) + EPS)
    vh = v * rs
    dyg = dy * g
    dx = rs * (dyg - vh * (jnp.sum(dyg * vh, axis=-1, keepdims=True) * (1.0 / n)))
    return dx, jnp.sum(dy * vh, axis=0, keepdims=True)


def _ride(body, grid, rider):
    if rider is None:
        return body, [], [], [], []
    n_x, n_sem = len(rider.srcs), len(rider.scratch)

    def wrapped(*refs):
        ids = [pl.program_id(a) for a in range(len(grid))]
        n_in = len(refs) - n_sem - 2 * n_x - rider.n_core_out - rider.n_core_scratch
        core_in, srcs = refs[:n_in], refs[n_in:n_in + n_x]
        core_out = refs[n_in + n_x:n_in + n_x + rider.n_core_out]
        dsts = refs[n_in + n_x + rider.n_core_out:n_in + 2 * n_x + rider.n_core_out]
        core_scr = refs[n_in + 2 * n_x + rider.n_core_out:len(refs) - n_sem]
        sems = refs[len(refs) - n_sem:]

        @pl.when(functools.reduce(jnp.logical_and, [a == 0 for a in ids]))
        def _():
            rider.start(srcs, dsts, *sems)

        body(*core_in, *core_out, *core_scr)

        @pl.when(functools.reduce(jnp.logical_and, [a == n - 1 for a, n in zip(ids, grid)]))
        def _():
            rider.finish(srcs, dsts, *sems)

    hbm = pl.BlockSpec(memory_space=pl.ANY)
    return wrapped, list(rider.srcs), [hbm] * n_x, list(rider.out_shapes), list(rider.scratch)


def _mm(name, a, b, *, trans_b=False, b_slots=False, out_dtypes=(F32,), epi=None, extras=(), consts=(), accs=(), tm=512, tn=None,
        rider=None):
    a_ops = [o if isinstance(o, tuple) else (o, 0, o.shape[1]) for o in (a if isinstance(a, list) else [a])]
    b_ops = b if isinstance(b, list) else [b]
    assert len(a_ops) == len(b_ops) and not (b_slots and (trans_b or len(b_ops) > 1))
    m = a_ops[0][0].shape[0]
    if b_slots:
        n, tn = b.shape[0] * b.shape[2], b.shape[2]
    else:
        first = b_ops[0][0] if isinstance(b_ops[0], tuple) else b_ops[0]
        n = first.shape[-2] if trans_b else first.shape[1]
        tn = n if tn is None else min(tn, n)
    tm = min(tm, m)
    assert m % tm == 0 and n % tn == 0
    extras = [e if isinstance(e, tuple) else (e, 0) for e in extras]
    n_p, n_ex, n_c, n_out, n_acc = len(a_ops), len(extras), len(consts), len(out_dtypes), len(accs)
    dims = (((1,), (1,)), ((), ())) if trans_b else (((1,), (0,)), ((), ()))

    def body(*refs):
        acc = None
        for a_ref, b_ref in zip(refs[:n_p], refs[n_p:2 * n_p]):
            part = lax.dot_general(a_ref[...].astype(BF16), b_ref[...].astype(BF16), dims, preferred_element_type=F32)
            acc = part if acc is None else acc + part
        rest = refs[2 * n_p:]
        res = (acc,) if epi is None else epi(acc, *[e[...] for e in rest[:n_ex + n_c]])
        o_refs = rest[n_ex + n_c:]
        for o_ref, r in zip(o_refs[:n_out], res[:n_out]):
            o_ref[...] = r.astype(o_ref.dtype)
        if n_acc:
            first_step = jnp.logical_and(pl.program_id(0) == 0, pl.program_id(1) == 0)

            @pl.when(first_step)
            def _():
                for o_ref, r in zip(o_refs[n_out:], res[n_out:]):
                    o_ref[...] = r

            @pl.when(jnp.logical_not(first_step))
            def _():
                for o_ref, r in zip(o_refs[n_out:], res[n_out:]):
                    o_ref[...] += r

    def b_spec(op, k_i):
        if b_slots:
            return pl.BlockSpec((None, k_i, tn), lambda j, i: (j, 0, 0))
        if not isinstance(op, tuple):
            return pl.BlockSpec((tn, k_i), lambda j, i: (j, 0)) if trans_b else pl.BlockSpec((k_i, tn), lambda j, i: (0, j))
        assert trans_b
        if len(op) == 2:
            return pl.BlockSpec((None, tn, k_i), lambda j, i, slot=op[1]: (slot, j, 0))
        return pl.BlockSpec((tn, k_i), lambda j, i, blk=op[1]: (j, blk))

    grid = (n // tn, m // tm)
    if rider is not None:
        rider.n_core_out, rider.n_core_scratch = n_out + n_acc, 0
    body, x_in, x_spec, x_out, x_scr = _ride(body, grid, rider)
    a_specs = [pl.BlockSpec((tm, k_i), lambda j, i, blk=blk: (i, blk)) for _, blk, k_i in a_ops]
    b_specs = [b_spec(op, k_i) for op, (_, _, k_i) in zip(b_ops, a_ops)]
    t_spec = pl.BlockSpec((tm, tn), lambda j, i: (i, j))
    e_specs = [pl.BlockSpec((tm, tn), lambda j, i, off=off: (i, j + off)) for _, off in extras]
    c_specs = [pl.BlockSpec(c.shape, lambda j, i: (0, 0)) for c in consts]
    acc_specs = [pl.BlockSpec(sh, lambda j, i: (0, 0)) for sh in accs]
    sem = ("parallel", "parallel") if rider is None and not n_acc else ("arbitrary", "arbitrary")
    outs = pl.pallas_call(
        body, out_shape=[jax.ShapeDtypeStruct((m, n), d) for d in out_dtypes] + [jax.ShapeDtypeStruct(sh, F32) for sh in accs] + x_out,
        grid=grid, in_specs=a_specs + b_specs + e_specs + c_specs + x_spec, out_specs=[t_spec] * n_out + acc_specs + x_spec,
        scratch_shapes=x_scr, compiler_params=_params(sem),
        name=name)(*[o[0] for o in a_ops], *[o[0] if isinstance(o, tuple) else o for o in b_ops], *[e for e, _ in extras], *consts, *x_in)
    return outs[0] if len(outs) == 1 else outs


def _per_head(fn, heads, n_tiled, n_out):
    def run(*args):
        res = [fn(*[a[:, hd * HEAD_W:(hd + 1) * HEAD_W] for a in args[:n_tiled]], *args[n_tiled:]) for hd in range(heads)]
        tiles = [jnp.concatenate([r[k] for r in res], axis=1) for k in range(n_out)]
        sums = [functools.reduce(lambda u, v: u + v, [r[k] for r in res]) for k in range(n_out, len(res[0]))]
        return (*tiles, *sums)
    return run


def _mm_tn(name, a, b, *, out_dtype=BF16, out_slots=False, tk=1024, tn=1024, tt=1024):
    t, k = a.shape
    n = b.shape[1]
    tk, tn, tt = min(tk, k), min(tn, n), min(tt, t)
    assert b.shape[0] == t and k % tk == 0 and n % tn == 0 and t % tt == 0
    nt = t // tt

    def body(a_ref, b_ref, o_ref, acc):
        part = lax.dot_general(a_ref[...].astype(BF16), b_ref[...].astype(BF16), (((0,), (0,)), ((), ())),
                               preferred_element_type=F32)

        @pl.when(pl.program_id(2) == 0)
        def _():
            acc[...] = part

        @pl.when(pl.program_id(2) != 0)
        def _():
            acc[...] += part

        @pl.when(pl.program_id(2) == nt - 1)
        def _():
            o_ref[...] = acc[...].astype(o_ref.dtype)

    if out_slots:
        out_shape, out_spec = (n // tn, k, tn), pl.BlockSpec((None, tk, tn), lambda ki, ni, ti: (ni, ki, 0))
    else:
        out_shape, out_spec = (k, n), pl.BlockSpec((tk, tn), lambda ki, ni, ti: (ki, ni))
    return pl.pallas_call(
        body, out_shape=jax.ShapeDtypeStruct(out_shape, out_dtype), grid=(k // tk, n // tn, nt),
        in_specs=[pl.BlockSpec((tt, tk), lambda ki, ni, ti: (ti, ki)), pl.BlockSpec((tt, tn), lambda ki, ni, ti: (ti, ni))],
        out_specs=out_spec, scratch_shapes=[pltpu.VMEM((tk, tn), F32)],
        compiler_params=_params(("parallel", "parallel", "arbitrary")), name=name)(a, b)


def _rowwise(name, fn, ins, outs, *, consts=(), pos=(), accs=(), heads=1, tm=512, seq=None, rider=None):
    t = ins[0][0].shape[0]
    tm = min(tm, t if seq is None else seq)
    assert t % tm == 0 and (seq is None or seq % tm == 0)
    n_in, n_pos, n_c, n_out, n_acc = len(ins), len(pos), len(consts), len(outs), len(accs)

    def body(*refs):
        vals = [r[...] for r in refs[:n_in + n_pos + n_c]]
        res = fn(*vals)
        o_refs = refs[n_in + n_pos + n_c:]
        for o_ref, r in zip(o_refs[:n_out], res[:n_out]):
            o_ref[...] = r.astype(o_ref.dtype)
        if n_acc:
            first = jnp.logical_and(pl.program_id(0) == 0, pl.program_id(1) == 0)

            @pl.when(first)
            def _():
                for o_ref, r in zip(o_refs[n_out:], res[n_out:]):
                    o_ref[...] = r

            @pl.when(jnp.logical_not(first))
            def _():
                for o_ref, r in zip(o_refs[n_out:], res[n_out:]):
                    o_ref[...] += r

    def tiled(width, c0, per_head):
        return pl.BlockSpec((tm, width), (lambda h, i: (i, c0 + h)) if per_head else (lambda h, i: (i, c0)))

    in_specs = [tiled(w, c0, ph) for _, w, c0, ph in ins]
    if n_pos:
        nblk = seq // tm
        in_specs += [pl.BlockSpec((tm, a.shape[1]), lambda h, i: (i % nblk, 0)) for a in pos]
    in_specs += [pl.BlockSpec(a.shape, lambda h, i: (0, 0)) for a in consts]
    out_specs = [tiled(w, 0, ph) for _, _, w, ph in outs] + [pl.BlockSpec(s, lambda h, i: (0, 0)) for s in accs]
    out_shape = [jax.ShapeDtypeStruct((t, c), d) for c, d, _, _ in outs] + [jax.ShapeDtypeStruct(s, F32) for s in accs]
    sem = ("arbitrary", "arbitrary") if n_acc or rider is not None else ("parallel", "parallel")
    grid = (heads, t // tm)
    if rider is not None:
        rider.n_core_out, rider.n_core_scratch = n_out + n_acc, 0
    body, x_in, x_spec, x_out, x_scr = _ride(body, grid, rider)
    res = pl.pallas_call(body, out_shape=out_shape + x_out, grid=grid, in_specs=in_specs + x_spec, out_specs=out_specs + x_spec,
                         scratch_shapes=x_scr, compiler_params=_params(sem), name=name)(*[a for a, _, _, _ in ins], *pos, *consts, *x_in)
    return res[0] if len(res) == 1 else res


ATTN_HEADS_PER_STEP = 4


def _attn_fwd(name, q, k, kc0, v, vc0, *, heads, group, nseq, seq, tq=512, rider=None):
    tq = min(tq, seq)
    nq = seq // tq
    hp = ATTN_HEADS_PER_STEP
    grid = (heads // hp, nseq, nq)
    shared = group > 1
    assert group % hp == 0 if shared else (kc0 % hp == 0 and vc0 % hp == 0)

    def body(q_ref, k_ref, v_ref, o_ref, lse_ref):
        for j in range(hp):
            cols = slice(j * HEAD_W, (j + 1) * HEAD_W)
            kj = (k_ref[...] if shared else k_ref[:, cols]).astype(BF16)
            vj = (v_ref[...] if shared else v_ref[:, cols]).astype(BF16)
            s = lax.dot_general(q_ref[:, cols], kj, (((1,), (1,)), ((), ())), preferred_element_type=F32)
            m = jnp.max(s, axis=-1, keepdims=True)
            p = jnp.exp(s - m)
            l = jnp.sum(p, axis=-1, keepdims=True)
            o = jnp.dot(p.astype(BF16), vj, preferred_element_type=F32)
            o_ref[:, cols] = (o * (1.0 / l)).astype(o_ref.dtype)
            lse_ref[j] = m + jnp.log(l)

    if rider is not None:
        rider.n_core_out, rider.n_core_scratch = 2, 0
    body, x_in, x_spec, x_out, x_scr = _ride(body, grid, rider)
    q_spec = pl.BlockSpec((tq, hp * HEAD_W), lambda h, b, i: (b * nq + i, h))
    if shared:
        k_spec = pl.BlockSpec((seq, HEAD_W), lambda h, b, i: (b, kc0 + (h * hp) // group))
        v_spec = pl.BlockSpec((seq, HEAD_W), lambda h, b, i: (b, vc0 + (h * hp) // group))
    else:
        k_spec = pl.BlockSpec((seq, hp * HEAD_W), lambda h, b, i: (b, kc0 // hp + h))
        v_spec = pl.BlockSpec((seq, hp * HEAD_W), lambda h, b, i: (b, vc0 // hp + h))
    lse_spec = pl.BlockSpec((hp, tq, 1), lambda h, b, i: (h, b * nq + i, 0))
    sem = ("parallel",) * 3 if rider is None else ("arbitrary",) * 3
    return pl.pallas_call(
        body, out_shape=[jax.ShapeDtypeStruct(q.shape, BF16), jax.ShapeDtypeStruct((heads, q.shape[0], 1), F32)] + x_out,
        grid=grid, in_specs=[q_spec, k_spec, v_spec] + x_spec, out_specs=[q_spec, lse_spec] + x_spec, scratch_shapes=x_scr,
        compiler_params=_params(sem), name=name)(q, k, v, *x_in)


def _attn_bwd(name, q, k, kc0, v, vc0, o, do, lse, *, heads, group, nseq, seq, tq=1024, rider=None):
    tq = min(tq, seq)
    nq = seq // tq
    hk = heads // group
    t = q.shape[0]
    grid = (hk, nseq, group, nq)

    def body(q_ref, k_ref, v_ref, o_ref, do_ref, lse_ref, dq_ref, dk_ref, dv_ref, dk_acc, dv_acc):
        g, i = pl.program_id(2), pl.program_id(3)
        qv, kv, vv, dov = q_ref[...], k_ref[...].astype(BF16), v_ref[...].astype(BF16), do_ref[...]
        s = lax.dot_general(qv, kv, (((1,), (1,)), ((), ())), preferred_element_type=F32)
        pn = jnp.exp(s - lse_ref[...])
        dp = lax.dot_general(dov, vv, (((1,), (1,)), ((), ())), preferred_element_type=F32)
        delta = jnp.sum(dov.astype(F32) * o_ref[...].astype(F32), axis=-1, keepdims=True)
        ds = (pn * (dp - delta)).astype(BF16)
        dq_ref[...] = jnp.dot(ds, kv, preferred_element_type=F32)
        dk_part = lax.dot_general(ds, qv, (((0,), (0,)), ((), ())), preferred_element_type=F32)
        dv_part = lax.dot_general(pn.astype(BF16), dov, (((0,), (0,)), ((), ())), preferred_element_type=F32)
        first = jnp.logical_and(g == 0, i == 0)

        @pl.when(first)
        def _():
            dk_acc[...] = dk_part
            dv_acc[...] = dv_part

        @pl.when(jnp.logical_not(first))
        def _():
            dk_acc[...] += dk_part
            dv_acc[...] += dv_part

        @pl.when(jnp.logical_and(g == group - 1, i == nq - 1))
        def _():
            dk_ref[...] = dk_acc[...].astype(dk_ref.dtype)
            dv_ref[...] = dv_acc[...].astype(dv_ref.dtype)

    if rider is not None:
        rider.n_core_out, rider.n_core_scratch = 3, 2
    body, x_in, x_spec, x_out, x_scr = _ride(body, grid, rider)
    q_spec = pl.BlockSpec((tq, HEAD_W), lambda kh, b, g, i: (b * nq + i, kh * group + g))
    kv_out = pl.BlockSpec((seq, HEAD_W), lambda kh, b, g, i: (b, kh))
    lse_spec = pl.BlockSpec((None, tq, 1), lambda kh, b, g, i: (kh * group + g, b * nq + i, 0))
    sem = ("parallel", "parallel", "arbitrary", "arbitrary") if rider is None else ("arbitrary",) * 4
    return pl.pallas_call(
        body,
        out_shape=[jax.ShapeDtypeStruct(q.shape, F32), jax.ShapeDtypeStruct((t, hk * HEAD_W), BF16),
                   jax.ShapeDtypeStruct((t, hk * HEAD_W), BF16)] + x_out,
        grid=grid,
        in_specs=[q_spec, pl.BlockSpec((seq, HEAD_W), lambda kh, b, g, i: (b, kc0 + kh)),
                  pl.BlockSpec((seq, HEAD_W), lambda kh, b, g, i: (b, vc0 + kh)), q_spec, q_spec, lse_spec] + x_spec,
        out_specs=[q_spec, kv_out, kv_out] + x_spec,
        scratch_shapes=[pltpu.VMEM((seq, HEAD_W), F32), pltpu.VMEM((seq, HEAD_W), F32)] + x_scr,
        compiler_params=_params(sem), name=name)(q, k, v, o, do, lse, *x_in)


def _place():
    return lax.axis_index("x"), lax.axis_index("y"), lax.axis_index("c")


def _other_chips(x, y):
    return [(1 - x, y), (x, 1 - y), (1 - x, 1 - y)]


class _Exchange:
    def __init__(self, kind, srcs):
        assert kind in ("gather", "scatter")
        self.kind, self.srcs = kind, list(srcs)
        n = len(self.srcs)
        self.out_shapes = [jax.ShapeDtypeStruct((4, *a.shape[-2:]), a.dtype) for a in self.srcs]
        self.scratch = [pltpu.SemaphoreType.DMA((3 * n,)), pltpu.SemaphoreType.DMA((3 * n,)), pltpu.SemaphoreType.DMA((n,))]
        self.n_core_out = self.n_core_scratch = 0

    def _copies(self, j, src_ref, out_ref, send_sems, recv_sems, landing):
        x, y, c = _place()

        def remote(k, s, d, to):
            return pltpu.make_async_remote_copy(src_ref=s, dst_ref=d, send_sem=send_sems.at[3 * j + k], recv_sem=recv_sems.at[3 * j + k],
                                                device_id=to, device_id_type=MESH)

        me = 2 * x + y
        part = (lambda i: src_ref) if self.kind == "gather" else (lambda i: src_ref.at[i])
        if landing:
            return [remote(k, part(me), out_ref.at[2 * px + py], (px, py, c)) for k, (px, py) in enumerate(_other_chips(x, y))]
        return [remote(k, part(2 * px + py), out_ref.at[me], (px, py, c)) for k, (px, py) in enumerate(_other_chips(x, y))]

    def _local(self, j, src_ref, out_ref, local_sems):
        x, y, _ = _place()
        me = 2 * x + y
        return pltpu.make_async_copy(src_ref if self.kind == "gather" else src_ref.at[me], out_ref.at[me], local_sems.at[j])

    def start(self, src_refs, out_refs, send_sems, recv_sems, local_sems):
        for j, (src_ref, out_ref) in enumerate(zip(src_refs, out_refs)):
            self._local(j, src_ref, out_ref, local_sems).start()
            for mine in self._copies(j, src_ref, out_ref, send_sems, recv_sems, False):
                mine.start()

    def finish(self, src_refs, out_refs, send_sems, recv_sems, local_sems):
        for j, (src_ref, out_ref) in enumerate(zip(src_refs, out_refs)):
            for landed in self._copies(j, src_ref, out_ref, send_sems, recv_sems, True):
                landed.wait_recv()
        for j, (src_ref, out_ref) in enumerate(zip(src_refs, out_refs)):
            for mine in self._copies(j, src_ref, out_ref, send_sems, recv_sems, False):
                mine.wait_send()
            self._local(j, src_ref, out_ref, local_sems).wait()


def _gather_by_halves(name, srcs):
    n = len(srcs)

    def body(*refs):
        x, y, c = _place()
        me = 2 * x + y
        local_sems = refs[-1]
        copies = []
        for j in range(n):
            src_ref, out_ref, send_sems, recv_sems = refs[j], refs[n + j], refs[2 * n + 2 * j], refs[2 * n + 2 * j + 1]
            half = srcs[j].shape[0] // 2
            rows_c = pl.ds(pl.multiple_of(c * half, half), half)
            rows_s = pl.ds(pl.multiple_of((1 - c) * half, half), half)

            def remote(k, s_ref, d_ref, to, send_sems=send_sems, recv_sems=recv_sems):
                return pltpu.make_async_remote_copy(src_ref=s_ref, dst_ref=d_ref, send_sem=send_sems.at[k], recv_sem=recv_sems.at[k],
                                                    device_id=to, device_id_type=MESH)

            local = pltpu.make_async_copy(src_ref, out_ref.at[me], local_sems.at[j])
            local.start()
            chips = _other_chips(x, y)
            sent = [remote(k, src_ref.at[rows_c], out_ref.at[me, rows_c], (px, py, c)) for k, (px, py) in enumerate(chips)]
            landing = [remote(k, src_ref.at[rows_c], out_ref.at[2 * px + py, rows_c], (px, py, c)) for k, (px, py) in enumerate(chips)]
            passed = [remote(3 + k, out_ref.at[2 * px + py, rows_c], out_ref.at[2 * px + py, rows_c], (x, y, 1 - c))
                      for k, (px, py) in enumerate(chips)]
            from_sibling = [remote(3 + k, out_ref.at[2 * px + py, rows_s], out_ref.at[2 * px + py, rows_s], (x, y, 1 - c))
                            for k, (px, py) in enumerate(chips)]
            for cp in sent:
                cp.start()
            copies.append((local, sent, landing, passed, from_sibling))
        for local, sent, landing, passed, from_sibling in copies:
            for k in range(3):
                landing[k].wait_recv()
                passed[k].start()
        for local, sent, landing, passed, from_sibling in copies:
            for k in range(3):
                from_sibling[k].wait_recv()
            for cp in sent + passed:
                cp.wait_send()
            local.wait()

    sems = [pltpu.SemaphoreType.DMA((6,)) for _ in range(2 * n)] + [pltpu.SemaphoreType.DMA((n,))]
    return pl.pallas_call(
        body, out_shape=[jax.ShapeDtypeStruct((4, *a.shape), a.dtype) for a in srcs],
        in_specs=[pl.BlockSpec(memory_space=pl.ANY)] * n, out_specs=[pl.BlockSpec(memory_space=pltpu.VMEM)] * n,
        scratch_shapes=sems, compiler_params=pltpu.CompilerParams(vmem_limit_bytes=VMEM_LIMIT_BYTES), name=name)(*srcs)


def _adamw(w, g, m, v):
    m = ADAM_B1 * m + (1.0 - ADAM_B1) * g
    v = ADAM_B2 * v + (1.0 - ADAM_B2) * (g * g)
    delta = -ADAM_LR * ((m / M_HAT_DIV) / (jnp.sqrt(v / V_HAT_DIV) + ADAM_EPS) + ADAM_WD * w)
    return delta, m, v


def _small_allreduce_adamw(part, w, m, v):
    def body(part_ref, w_ref, m_ref, v_ref, g_out, d_out, m_out, v_out, loss_out, buf, send_sems, recv_sems):
        x, y, c = _place()
        me = 4 * x + 2 * y + c
        buf[me] = part_ref[...]

        def flip(k):
            fx, fy, fc = (k >> 2) & 1, (k >> 1) & 1, k & 1
            px, py, pc = (1 - x if fx else x), (1 - y if fy else y), (1 - c if fc else c)
            return (px, py, pc), 4 * px + 2 * py + pc

        def copy(k, slot):
            return pltpu.make_async_remote_copy(
                src_ref=part_ref, dst_ref=buf.at[slot], send_sem=send_sems.at[k - 1], recv_sem=recv_sems.at[k - 1],
                device_id=flip(k)[0], device_id_type=MESH)

        sent = [copy(k, me) for k in range(1, 8)]
        for cp in sent:
            cp.start()
        for k in range(1, 8):
            copy(k, flip(k)[1]).wait_recv()
        for cp in sent:
            cp.wait_send()
        tot = buf[0]
        for j in range(1, 8):
            tot = tot + buf[j]
        delta, m_new, v_new = _adamw(w_ref[...], tot, m_ref[...], v_ref[...])
        g_out[...] = tot
        d_out[...] = delta
        m_out[...] = m_new
        v_out[...] = v_new
        loss_out[...] = jnp.sum(tot[LOSS_ROW0:LOSS_ROW0 + 8, :]).reshape(1, 1)

    vm = pl.BlockSpec(memory_space=pltpu.VMEM)
    shp = jax.ShapeDtypeStruct((SMALL_ROWS, 128), F32)
    return pl.pallas_call(
        body, out_shape=[shp, shp, shp, shp, jax.ShapeDtypeStruct((1, 1), F32)],
        in_specs=[vm, vm, vm, vm], out_specs=[vm, vm, vm, vm, vm],
        scratch_shapes=[pltpu.VMEM((8, SMALL_ROWS, 128), F32), pltpu.SemaphoreType.DMA((7,)), pltpu.SemaphoreType.DMA((7,))],
        name="small_allreduce_adamw")(part, w, m, v)


def _row_tile(rows, cap):
    return max(t for t in range(16, min(rows, cap) + 1, 16) if rows % t == 0)


def _reduce_pair(name, parts):
    _, rows, w = parts.shape
    tr = _row_tile(rows, 576)
    nt = rows // tr

    def body(p_ref, o_ref, mine, theirs, send_sems, recv_sems):
        i = pl.program_id(0)
        x, y, c = _place()

        def copy(t):
            rows_t = pl.ds(pl.multiple_of(t * tr, tr), tr)
            return pltpu.make_async_remote_copy(src_ref=mine.at[rows_t], dst_ref=theirs.at[rows_t], send_sem=send_sems.at[t],
                                                recv_sem=recv_sems.at[t], device_id=(x, y, 1 - c), device_id_type=MESH)

        @pl.when(i < nt)
        def _():
            mine[pl.ds(pl.multiple_of(i * tr, tr), tr), :] = (
                (p_ref[0].astype(F32) + p_ref[1].astype(F32)) + p_ref[2].astype(F32)) + p_ref[3].astype(F32)
            copy(i).start()

        @pl.when(i >= nt)
        def _():
            t = i - nt
            copy(t).wait()
            rows_t = pl.ds(pl.multiple_of(t * tr, tr), tr)
            o_ref[...] = mine[rows_t, :] + theirs[rows_t, :]

    return pl.pallas_call(
        body, out_shape=jax.ShapeDtypeStruct((rows, w), F32), grid=(2 * nt,),
        in_specs=[pl.BlockSpec((4, tr, w), lambda i: (0, jnp.minimum(i, nt - 1), 0))],
        out_specs=pl.BlockSpec((tr, w), lambda i: (jnp.maximum(i - nt, 0), 0)),
        scratch_shapes=[pltpu.VMEM((rows, w), F32), pltpu.VMEM((rows, w), F32), pltpu.SemaphoreType.DMA((nt,)),
                        pltpu.SemaphoreType.DMA((nt,))],
        compiler_params=_params(("arbitrary",)), name=name)(parts)


def _adamw_shard(name, g, w, m, v):
    rows, cols = w.shape
    tr = _row_tile(rows, 256)

    def body(g_ref, w_ref, m_ref, v_ref, d_out, m_out, v_out):
        delta, m_new, v_new = _adamw(w_ref[...], g_ref[...], m_ref[...], v_ref[...])
        d_out[...] = delta
        m_out[...] = m_new
        v_out[...] = v_new

    t_spec = pl.BlockSpec((tr, cols), lambda i: (i, 0))
    shp = jax.ShapeDtypeStruct((rows, cols), F32)
    return pl.pallas_call(body, out_shape=[shp] * 3, grid=(rows // tr,), in_specs=[t_spec] * 4, out_specs=[t_spec] * 3,
                          compiler_params=_params(("parallel",)), name=name)(g, w, m, v)


def _shard_shape(name):
    _, r, c, ax = BIG_BY_NAME[name]
    return (r, c // 4) if ax == 1 else (r // 4, c)


def _pad_rows(a, axis):
    pad = [(0, 0)] * a.ndim
    pad[axis] = (0, -a.shape[axis] % PACK_ALIGN)
    return jnp.pad(a, pad)


def _pack_shards(names, shards, dtype):
    return _pad_rows(jnp.concatenate([s.astype(dtype).reshape(-1, PACK_W) for s in shards], axis=0), 0)


def _unpack_shards(names, slab):
    out, off = [], 0
    for name in names:
        rs, cs = _shard_shape(name)
        n = rs * cs // PACK_W
        out.append(slab[off:off + n].reshape(rs, cs))
        off += n
    return out


def _unpack_full(names, slabs):
    out, off = [], 0
    for name in names:
        _, r, c, ax = BIG_BY_NAME[name]
        n = r * c // 4 // PACK_W
        seg = slabs[:, off:off + n]
        out.append(seg.reshape(4, r, c // 4).transpose(1, 0, 2).reshape(r, c) if ax == 1 else seg.reshape(r, c))
        off += n
    return out


def _pack_full(names, mats, dtype):
    segs = []
    for name, a in zip(names, mats):
        _, r, c, ax = BIG_BY_NAME[name]
        a = a.astype(dtype)
        a = a.reshape(r, 4, c // 4).transpose(1, 0, 2) if ax == 1 else a
        segs.append(a.reshape(4, -1, PACK_W))
    return _pad_rows(jnp.concatenate(segs, axis=1), 1)


def _pad_heads_cols(wm, heads, d):
    k = wm.shape[0]
    return jnp.pad(wm.reshape(k, heads, d), ((0, 0), (0, 0), (0, HEAD_W - d))).reshape(k, heads * HEAD_W)


def _unpad_heads_cols(wm, heads, d):
    k = wm.shape[0]
    return wm.reshape(k, heads, HEAD_W)[:, :, :d].reshape(k, heads * d)


def _win_ext(w_in):
    o = np.cumsum([0, Q_LORA, KV_LORA, QK_ROPE, H_B * HD_B, KV_B * HD_B, KV_B * HD_B, D_MODEL, D_MODEL])
    pc = lambda a, n: jnp.pad(a, ((0, 0), (0, n - a.shape[1])))
    return jnp.concatenate([
        _pad_heads_cols(w_in[:, o[3]:o[4]], H_B, HD_B), w_in[:, o[0]:o[1]], w_in[:, o[1]:o[2]], pc(w_in[:, o[2]:o[3]], HEAD_W),
        _pad_heads_cols(w_in[:, o[4]:o[5]], KV_B, HD_B), _pad_heads_cols(w_in[:, o[5]:o[6]], KV_B, HD_B),
        w_in[:, o[6]:o[7]], w_in[:, o[7]:o[8]]], axis=1)


def _win_unext(blocks):
    c = HEAD_W
    qb, mid, ga, gb = blocks
    at = lambda zc: (zc - ZC_QLAT) * c
    return jnp.concatenate([
        mid[:, at(ZC_QLAT):at(ZC_CKV)], mid[:, at(ZC_CKV):at(ZC_KPE)], mid[:, at(ZC_KPE):at(ZC_KPE) + QK_ROPE],
        _unpad_heads_cols(qb, H_B, HD_B), _unpad_heads_cols(mid[:, at(ZC_KB):at(ZC_VB)], KV_B, HD_B),
        _unpad_heads_cols(mid[:, at(ZC_VB):at(ZC_GA)], KV_B, HD_B), ga, gb], axis=1)


def _wkv_ext(w_kvb):
    wk = w_kvb.reshape(KV_LORA, H_A, QK_NOPE + V_DIM_A)
    k_cols = jnp.pad(wk[:, :, :QK_NOPE], ((0, 0), (0, 0), (0, HEAD_W - QK_NOPE))).reshape(KV_LORA, H_A * HEAD_W)
    v_cols = jnp.pad(wk[:, :, QK_NOPE:], ((0, 0), (0, 0), (0, HEAD_W - V_DIM_A))).reshape(KV_LORA, H_A * HEAD_W)
    eye = jnp.pad(jnp.eye(QK_ROPE, dtype=w_kvb.dtype), ((0, 0), (QK_NOPE, HEAD_W - QK_NOPE - QK_ROPE)))
    pe_rows = jnp.concatenate([jnp.tile(eye, (1, H_A)), jnp.zeros((QK_ROPE, H_A * HEAD_W), w_kvb.dtype)], axis=1)
    top = jnp.concatenate([k_cols, v_cols], axis=1)
    return jnp.concatenate([top, pe_rows, jnp.zeros((2 * HEAD_W - KV_LORA - QK_ROPE, 2 * H_A * HEAD_W), w_kvb.dtype)], axis=0)


def _wkv_unext(k_block, v_block):
    k_cols = k_block[:KV_LORA].reshape(KV_LORA, H_A, HEAD_W)[:, :, :QK_NOPE]
    v_cols = v_block[:KV_LORA].reshape(KV_LORA, H_A, HEAD_W)[:, :, :V_DIM_A]
    return jnp.concatenate([k_cols, v_cols], axis=2).reshape(KV_LORA, H_A * (QK_NOPE + V_DIM_A))


def _pad_heads_rows(wm, heads, d):
    n = wm.shape[1]
    return jnp.pad(wm.reshape(heads, d, n), ((0, 0), (0, HEAD_W - d), (0, 0))).reshape(heads * HEAD_W, n)


def _unpad_heads_rows(wm, heads, d):
    n = wm.shape[1]
    return wm.reshape(heads, HEAD_W, n)[:, :d].reshape(heads * d, n)


def _rope_tables(seq):
    def ang(pos, dim):
        inv = np.float32(ROPE_THETA) ** (-np.arange(0, dim, 2, dtype=np.float32) / np.float32(dim))
        return pos.astype(np.float32)[:, None] * inv[None, :]

    def rot(dim):
        r = np.zeros((dim, dim), np.float32)
        half = dim // 2
        r[np.arange(half) + half, np.arange(half)] = -1.0
        r[np.arange(half), np.arange(half) + half] = 1.0
        return r

    def table(blocks):
        cos, sin = np.ones((seq, HEAD_W), np.float32), np.zeros((seq, HEAD_W), np.float32)
        pm = np.zeros((HEAD_W, HEAD_W), np.float32)
        for c0, a in blocks:
            d = 2 * a.shape[1]
            cos[:, c0:c0 + d] = np.concatenate([np.cos(a), np.cos(a)], axis=1)
            sin[:, c0:c0 + d] = np.concatenate([np.sin(a), np.sin(a)], axis=1)
            pm[c0:c0 + d, c0:c0 + d] = rot(d)
        return jnp.asarray(cos), jnp.asarray(sin), jnp.asarray(pm, BF16), jnp.asarray(pm.T, BF16)

    tok = np.arange(seq)
    a1 = ang(tok, QK_ROPE)
    arow, acol = ang(tok // GRID_W, HD_B // 2), ang(tok % GRID_W, HD_B // 2)
    return table([(QK_NOPE, a1)]), table([(0, a1)]), table([(0, arow), (HD_B // 2, acol)])


def _local_step(x, p, tgt, gains, wts, ride=None):
    nb, seq, _ = x.shape
    t = nb * seq
    x0 = x.reshape(t, D_MODEL)
    p2 = p.reshape(t, PLE_DIM)
    tg = tgt.reshape(t, D_MODEL)
    (cq_t, sq_t, pq, pq_t), (ck_t, sk_t, pk, pk_t), (cb_t, sb_t, pb, pb_t) = _rope_tables(seq)
    padg = lambda g: jnp.pad(g, ((0, 0), (0, HEAD_W - g.shape[1])))
    g_qn, g_kn = padg(gains["g_qn"]), padg(gains["g_kn"])

    win = _win_ext(wts["w_in"])
    wqb = _pad_heads_cols(wts["w_qb"], H_A, QK_NOPE + QK_ROPE)
    wkv = _wkv_ext(wts["w_kvb"])

    norm = lambda n: (lambda v, g: (_rms(v, g, n),))
    full = lambda a: (a, a.shape[1], 0, False)
    wts = dict(wts)
    rider_of = lambda kernel_name: None if ride is None else ride["gather"][kernel_name][0]

    def landed(kernel_name, got):
        if ride is not None:
            wts.update(ride["gather"][kernel_name][1](got))

    h = _rowwise("norm_mix", norm(D_MODEL), [full(x0)], [(D_MODEL, BF16, D_MODEL, False)], consts=[gains["g_mix"]])
    res = _mm("in_proj", h, win, tn=2048, rider=rider_of("in_proj"))
    z, got = (res, []) if ride is None else (res[0], res[1:])
    landed("in_proj", got)
    cq = _rowwise("norm_qa", norm(Q_LORA), [(z, Q_LORA, ZC_QLAT // 2, False)], [(Q_LORA, BF16, Q_LORA, False)],
                  consts=[gains["g_qa"]])
    qa = _mm("q_up", cq, wqb)

    def rope_fwd(scale):
        return lambda v, cos, sin, pm: ((v * cos + _perm(v, pm) * sin) * scale,)

    heads_tile = lambda n: (n * HEAD_W, BF16, n * HEAD_W, False)
    q_a = _rowwise("rope_qa", _per_head(rope_fwd(SCALE_A), H_A, 1, 1), [full(qa)], [heads_tile(H_A)],
                   pos=[cq_t, sq_t], consts=[pq], seq=seq)
    ckv = _rowwise("norm_kva", norm(KV_LORA), [(z, HEAD_W, ZC_CKV, False)], [(HEAD_W, BF16, HEAD_W, False)], consts=[gains["g_kva"]])
    kpe = _rowwise("rope_kpe", rope_fwd(1.0), [(z, HEAD_W, ZC_KPE, False)], [(HEAD_W, BF16, HEAD_W, False)],
                   pos=[ck_t, sk_t], consts=[pk], seq=seq)
    kin = jnp.concatenate([ckv, kpe], axis=1)
    kv_a = _mm("kv_up", kin, wkv, out_dtypes=(BF16,))
    o_a, lse_a, *got = _attn_fwd("attn_a_fwd", q_a, kv_a, 0, kv_a, H_A, heads=H_A, group=1, nseq=nb, seq=seq,
                                 rider=rider_of("attn_a_fwd"))
    landed("attn_a_fwd", got)

    def prep_fwd(scale):
        def fn(v, cos, sin, g, pm):
            yv = _rms(v, g, HD_B)
            return ((yv * cos + _perm(yv, pm) * sin) * scale,)
        return fn

    z_qb, z_kb = (z, H_B * HEAD_W, ZC_QB // H_B, False), (z, KV_B * HEAD_W, ZC_KB // KV_B, False)
    q_b = _rowwise("prep_qb", _per_head(prep_fwd(SCALE_B), H_B, 1, 1), [z_qb], [heads_tile(H_B)],
                   pos=[cb_t, sb_t], consts=[g_qn, pb], seq=seq)
    k_b = _rowwise("prep_kb", _per_head(prep_fwd(1.0), KV_B, 1, 1), [z_kb], [heads_tile(KV_B)],
                   pos=[cb_t, sb_t], consts=[g_kn, pb], seq=seq)
    o_b, lse_b, *got = _attn_fwd("attn_b_fwd", q_b, k_b, 0, z, ZC_VB, heads=H_B, group=H_B // KV_B, nseq=nb, seq=seq,
                                 rider=rider_of("attn_b_fwd"))
    landed("attn_b_fwd", got)
    woa = _pad_heads_rows(wts["w_oa"], H_A, V_DIM_A)
    wob = _pad_heads_rows(wts["w_ob"], H_B, HD_B)
    wo, wup, wdown, wple = wts["w_o"], wts["w_up"], wts["w_down"], wts["w_ple"]

    def residual_norm(acc, r, g):
        xv = r + acc
        return xv, _rms(xv, g, D_MODEL)

    def mix_out(oa, ob, ga, gb, r, g, w_a, w_b, w_out):
        a = jnp.dot(oa, w_a[...], preferred_element_type=F32)
        b = jnp.dot(ob, w_b[...], preferred_element_type=F32)
        mg = (_sigmoid(ga) * a + _sigmoid(gb) * b).astype(BF16)
        return (a, b, mg, *residual_norm(jnp.dot(mg, w_out[...], preferred_element_type=F32), r, g))

    z_ga, z_gb = (z, D_MODEL, ZC_GA // 8, False), (z, D_MODEL, ZC_GB // 8, False)
    wide = lambda d: (D_MODEL, d, D_MODEL, False)
    ya, yb, merged, x1, h2, *got = _rowwise("mix_out", mix_out, [full(o_a), full(o_b), z_ga, z_gb, full(x0)],
                                            [wide(F32), wide(F32), wide(BF16), wide(F32), wide(BF16)],
                                            consts=[gains["g_mlp"], woa, wob, wo], tm=256, rider=rider_of("mix_out"))
    landed("mix_out", got)
    wpg = wts["w_ple_gate"]

    def relu2(acc):
        u = jnp.maximum(acc, 0.0)
        return u, u * u

    u, usq = _mm("mlp_up", h2, wup, b_slots=True, out_dtypes=(BF16, BF16), epi=relu2)
    x2, h3 = _mm("mlp_down", usq, wdown, out_dtypes=(F32, BF16), epi=residual_norm, extras=(x1,), consts=[gains["g_ple"]])

    def tail(x2v, h3v, pv, tv, gf, w_gate, w_emb):
        sg = _sigmoid(jnp.dot(h3v, w_gate[...], preferred_element_type=F32))
        pev = jnp.dot(pv.astype(BF16), w_emb[...], preferred_element_type=F32)
        x3 = x2v + sg * pev
        rs = lax.rsqrt(jnp.sum(x3 * x3, axis=-1, keepdims=True) * (1.0 / D_MODEL) + EPS)
        xh = x3 * rs
        err = xh * gf - tv
        dy = err * (1.0 / D_MODEL)
        dyg = dy * gf
        dx3 = rs * (dyg - xh * (jnp.sum(dyg * xh, axis=-1, keepdims=True) * (1.0 / D_MODEL)))
        return (dx3, dx3 * pev * sg * (1.0 - sg), dx3 * sg,
                jnp.sum(err * err, axis=0, keepdims=True) * (0.5 / D_MODEL), jnp.sum(dy * xh, axis=0, keepdims=True))

    dx3, dgpre, dpe, loss_part, dg_final = _rowwise(
        "tail", tail, [full(x2), full(h3), full(p2), full(tg)], [wide(F32), wide(BF16), wide(BF16)],
        consts=[gains["g_final"].reshape(1, D_MODEL), wpg, wple], accs=[(1, D_MODEL), (1, D_MODEL)], tm=256)

    def norm_bwd(n, with_res):
        if with_res:
            def fn(dh, v, res, g):
                dx, dg = _rms_bwd(dh, v, g, n)
                return dx + res, dg
        else:
            def fn(dh, v, g):
                return _rms_bwd(dh, v, g, n)
        return fn

    dw = {}
    dw["w_ple"] = _mm_tn("dw_ple", p2, dpe)
    dw["w_ple_gate"] = _mm_tn("dw_ple_gate", h3, dgpre)
    norm_res_bwd = norm_bwd(D_MODEL, True)
    dx2, dg_ple = _mm("d_ple_gate", dgpre, wpg, trans_b=True, epi=norm_res_bwd, extras=(x2, dx3), consts=[gains["g_ple"]],
                      accs=[(1, D_MODEL)], tm=256)
    dw["w_down"] = _mm_tn("dw_down", usq, dx2)
    dupre = _mm("d_mlp_down", dx2, wdown, trans_b=True, out_dtypes=(BF16,), epi=lambda acc, uv: (acc * (2.0 * uv.astype(F32)),),
                extras=(u,), tn=2048)
    dw["w_up"] = _mm_tn("dw_up", h2, dupre, out_slots=True)
    n_up = wup.shape[0]
    dx1, dg_mlp = _mm("d_mlp_up", [(dupre, j, wup.shape[2]) for j in range(n_up)], [(wup, j) for j in range(n_up)], trans_b=True,
                      epi=norm_res_bwd, extras=(x1, dx2), consts=[gains["g_mlp"]],
                      accs=[(1, D_MODEL)], tm=256)
    dw["w_o"] = _mm_tn("dw_o", merged, dx1)

    def merge_bwd(dm, ga, gb, a, b, w_a, w_b):
        sa, sb = _sigmoid(ga), _sigmoid(gb)
        da, db = (dm * sa).astype(BF16), (dm * sb).astype(BF16)
        nt = (((1,), (1,)), ((), ()))
        return (da, db, dm * a * sa * (1.0 - sa), dm * b * sb * (1.0 - sb),
                lax.dot_general(da, w_a, nt, preferred_element_type=F32), lax.dot_general(db, w_b, nt, preferred_element_type=F32))

    dya, dyb, dga, dgb, do_a, do_b = _mm("d_out_proj", dx1, wo, trans_b=True, out_dtypes=(BF16,) * 6, epi=merge_bwd,
                                         extras=((z, ZC_GA // 8), (z, ZC_GB // 8), ya, yb), consts=[woa, wob], tm=256)
    dw["w_oa"] = _unpad_heads_rows(_mm_tn("dw_oa", o_a, dya), H_A, V_DIM_A)
    dw["w_ob"] = _unpad_heads_rows(_mm_tn("dw_ob", o_b, dyb), H_B, HD_B)
    res_a = _attn_bwd("attn_a_bwd", q_a, kv_a, 0, kv_a, H_A, o_a, do_a, lse_a, heads=H_A, group=1, nseq=nb, seq=seq,
                      rider=ride and ride["scatter_a"](dw))
    dq_a, dk_a, dv_a = res_a[:3]
    if ride is not None:
        ride["out"]["parts_a"] = res_a[3:]

    def rope_bwd(scale):
        return lambda d, cos, sin, pm_t: ((d * cos + _perm(d * sin, pm_t)) * scale,)

    dqa = _rowwise("rope_qa_bwd", _per_head(rope_bwd(SCALE_A), H_A, 1, 1), [full(dq_a)], [heads_tile(H_A)],
                   pos=[cq_t, sq_t], consts=[pq_t], seq=seq)
    dw["w_qb"] = _unpad_heads_cols(_mm_tn("dw_qb", cq, dqa), H_A, QK_NOPE + QK_ROPE)
    dcq = _mm("d_q_up", dqa, wqb, trans_b=True)
    dq_lat, dg_qa = _rowwise("norm_qa_bwd", norm_bwd(Q_LORA, False), [full(dcq), (z, Q_LORA, ZC_QLAT // 2, False)],
                             [(Q_LORA, BF16, Q_LORA, False)], consts=[gains["g_qa"]], accs=[(1, Q_LORA)])
    dw["w_kvb"] = _wkv_unext(_mm_tn("dw_kv_k", kin, dk_a), _mm_tn("dw_kv_v", kin, dv_a))
    dq_b, dk_b, dv_b, *parts_b = _attn_bwd("attn_b_bwd", q_b, k_b, 0, z, ZC_VB, o_b, do_b, lse_b, heads=H_B, group=H_B // KV_B,
                                               nseq=nb, seq=seq, rider=ride and ride["scatter_b"](dw))
    if ride is not None:
        ride["out"]["parts_b"] = parts_b
    kv_w = H_A * HEAD_W
    dkin = _mm("d_kv_up", [dk_a, dv_a], [(wkv, 0, kv_w), (wkv, 1, kv_w)], trans_b=True)
    dckv, dg_kva = _rowwise("norm_kva_bwd", norm_bwd(KV_LORA, False), [(dkin, HEAD_W, 0, False), (z, HEAD_W, ZC_CKV, False)],
                            [(HEAD_W, BF16, HEAD_W, False)], consts=[gains["g_kva"]], accs=[(1, KV_LORA)])
    dkpe = _rowwise("rope_kpe_bwd", rope_bwd(1.0), [(dkin, HEAD_W, 1, False)], [(HEAD_W, BF16, HEAD_W, False)],
                    pos=[ck_t, sk_t], consts=[pk_t], seq=seq)

    def prep_bwd(scale):
        def fn(d, v, cos, sin, g, pm_t):
            dyv = (d * cos + _perm(d * sin, pm_t)) * scale
            return _rms_bwd(dyv, v, g, HD_B)
        return fn

    dqb, dg_qn = _rowwise("prep_qb_bwd", _per_head(prep_bwd(SCALE_B), H_B, 2, 1), [full(dq_b), z_qb], [heads_tile(H_B)],
                          pos=[cb_t, sb_t], consts=[g_qn, pb_t], accs=[(1, HEAD_W)], seq=seq)
    dkb, dg_kn = _rowwise("prep_kb_bwd", _per_head(prep_bwd(1.0), KV_B, 2, 1), [full(dk_b), z_kb], [heads_tile(KV_B)],
                          pos=[cb_t, sb_t], consts=[g_kn, pb_t], accs=[(1, HEAD_W)], seq=seq)

    dz = [dqb, jnp.concatenate([dq_lat, dckv, dkpe, dkb, dv_b], axis=1), dga, dgb]
    dw["w_in"] = _win_unext([_mm_tn("dw_in_%d" % j, h, blk) for j, blk in enumerate(dz)])
    dx0, dg_mix, *parts_in = _mm("d_in_proj", dz, [(win, j, D_MODEL) for j in range(4)], trans_b=True, epi=norm_res_bwd, extras=(x0, dx1), consts=[gains["g_mix"]],
                                 accs=[(1, D_MODEL)], tm=256, rider=ride and ride["scatter_in"](dw))
    if ride is not None:
        ride["out"]["parts_in"] = parts_in

    dg = {"g_mix": dg_mix, "g_qa": dg_qa, "g_kva": dg_kva, "g_qn": dg_qn[:, :HD_B], "g_kn": dg_kn[:, :HD_B],
          "g_mlp": dg_mlp, "g_ple": dg_ple, "g_final": dg_final}
    return loss_part, dx0.reshape(nb, seq, D_MODEL), dg, dw


def _pack_small(vals, loss_part=None):
    flat = jnp.concatenate([vals[n].reshape(1, -1) for n, _ in SMALL], axis=1)
    loss = jnp.zeros((1, 8 * 128), F32) if loss_part is None else loss_part
    gap = jnp.zeros((1, LOSS_ROW0 * 128 - SMALL_N), F32)
    return jnp.concatenate([flat, gap, loss], axis=1).reshape(SMALL_ROWS, 128)


def _unpack_small(slab, like):
    flat, out, off = slab.reshape(-1), {}, 0
    for n, k in SMALL:
        out[n] = flat[off:off + k].reshape(like[n].shape)
        off += k
    return out


def kernel(x, p, g_mix, w_in, g_qa, w_qb, g_kva, w_kvb, g_qn, g_kn, w_oa, w_ob, w_o, g_mlp, w_up, w_down, g_ple, w_ple_gate, w_ple, g_final, loss_target, m_g_mix, m_w_in, m_g_qa, m_w_qb, m_g_kva, m_w_kvb, m_g_qn, m_g_kn, m_w_oa, m_w_ob, m_w_o, m_g_mlp, m_w_up, m_w_down, m_g_ple, m_w_ple_gate, m_w_ple, m_g_final, v_g_mix, v_w_in, v_g_qa, v_w_qb, v_g_kva, v_w_kvb, v_g_qn, v_g_kn, v_w_oa, v_w_ob, v_w_o, v_g_mlp, v_w_up, v_w_down, v_g_ple, v_w_ple_gate, v_w_ple, v_g_final):
    given = dict(locals())
    order = ["g_mix", "w_in", "g_qa", "w_qb", "g_kva", "w_kvb", "g_qn", "g_kn", "w_oa", "w_ob", "w_o", "g_mlp", "w_up",
             "w_down", "g_ple", "w_ple_gate", "w_ple", "g_final"]
    big_names = [n for n, _, _, _ in BIG]
    local = lambda prefix, names: [given[prefix + n][0] for n in names]
    slab = lambda names: _pack_shards(names, local("", names), BF16)
    bf = lambda n: given[n][0].astype(BF16)
    cols_full = lambda g: g.transpose(1, 0, 2).reshape(g.shape[1], -1)
    rows_full = lambda g: g.reshape(-1, g.shape[2])
    shards_cols = lambda a: a.reshape(a.shape[0], 4, a.shape[1] // 4).transpose(1, 0, 2)
    shards_rows = lambda a: a.reshape(4, a.shape[0] // 4, a.shape[1])
    packed = lambda names, dw: _pack_full(names, [dw[n] for n in names], BF16)

    got_in, got_early = _gather_by_halves("weight_gather_early", [bf("w_in"), slab(SLAB_EARLY)])
    wts = {"w_in": cols_full(got_in), **dict(zip(SLAB_EARLY, _unpack_full(SLAB_EARLY, got_early)))}
    gains = {n: given[n].reshape(1, -1) for n, _ in SMALL}
    ride = {
        "gather": {
            "in_proj": (_Exchange("gather", [slab(SLAB_LATE)]), lambda got: dict(zip(SLAB_LATE, _unpack_full(SLAB_LATE, got[0])))),
            "attn_a_fwd": (_Exchange("gather", [bf("w_up"), bf("w_o")]),
                           lambda got: {"w_up": got[0], "w_o": rows_full(got[1])}),
            "attn_b_fwd": (_Exchange("gather", [bf("w_down")]), lambda got: {"w_down": rows_full(got[0])}),
            "mix_out": (_Exchange("gather", [bf("w_ple_gate")]), lambda got: {"w_ple_gate": rows_full(got[0])}),
        },
        "scatter_a": lambda dw: _Exchange("scatter", [dw["w_up"], shards_rows(dw["w_o"]), packed(SLAB_LATE, dw)]),
        "scatter_b": lambda dw: _Exchange("scatter", [shards_rows(dw["w_down"]), shards_rows(dw["w_ple_gate"]), packed(SLAB_EARLY, dw)]),
        "scatter_in": lambda dw: _Exchange("scatter", [shards_cols(dw["w_in"])]),
        "out": {},
    }
    loss_part, grad_x, dg, dw = _local_step(x, p[0], loss_target, gains, wts, ride)

    small = lambda prefix: _pack_small({n: given[prefix + n] for n, _ in SMALL})
    g_s, d_s, m_s, v_s, loss = _small_allreduce_adamw(_pack_small(dg, loss_part), small(""), small("m_"), small("v_"))

    parts = ride["out"]
    grads = {"w_up": _reduce_pair("grad_reduce_up", parts["parts_a"][0]), "w_o": _reduce_pair("grad_reduce_o", parts["parts_a"][1]),
             "w_down": _reduce_pair("grad_reduce_down", parts["parts_b"][0]),
             "w_ple_gate": _reduce_pair("grad_reduce_ple_gate", parts["parts_b"][1]),
             "w_in": _reduce_pair("grad_reduce_in", parts["parts_in"][0])}
    grads.update(zip(SLAB_LATE, _unpack_shards(SLAB_LATE, _reduce_pair("grad_reduce_slab_late", parts["parts_a"][2]))))
    grads.update(zip(SLAB_EARLY, _unpack_shards(SLAB_EARLY, _reduce_pair("grad_reduce_slab_early", parts["parts_b"][2]))))

    res = {}
    for key, slab in (("grad_", g_s), ("delta_", d_s), ("new_m_", m_s), ("new_v_", v_s)):
        for n, val in _unpack_small(slab, given).items():
            res[key + n] = val
    for n in big_names:
        d_w, m_w, v_w = _adamw_shard("adamw_" + n, grads[n], given[n][0], given["m_" + n][0], given["v_" + n][0])
        res["grad_" + n], res["delta_" + n], res["new_m_" + n], res["new_v_" + n] = grads[n][None], d_w[None], m_w[None], v_w[None]
    outs = [loss.reshape(()), grad_x]
    for key in ("grad_", "delta_", "new_m_", "new_v_"):
        outs += [res[key + n] for n in order]
    return tuple(outs)
```

```python
import functools

import numpy as np
import jax
import jax.numpy as jnp
from jax import lax
from jax.experimental import pallas as pl
from jax.experimental.pallas import tpu as pltpu

F32 = jnp.float32
BF16 = jnp.bfloat16
MESH = pl.DeviceIdType.MESH

D_MODEL = 1024
GRID_W = 64
ROPE_THETA = 10000.0
EPS = 1e-6
H_A, QK_NOPE, QK_ROPE, V_DIM_A, Q_LORA, KV_LORA = 8, 64, 32, 64, 256, 128
H_B, KV_B, HD_B = 8, 2, 64
D_FF = 4096
PLE_DIM = 256
HEAD_W = 128
SCALE_A = (QK_NOPE + QK_ROPE) ** -0.5
SCALE_B = HD_B ** -0.5

ADAM_LR, ADAM_B1, ADAM_B2, ADAM_EPS, ADAM_WD, ADAM_STEP = 0.001, 0.9, 0.999, 1e-08, 0.01, 10
M_HAT_DIV = 1.0 - ADAM_B1 ** ADAM_STEP
V_HAT_DIV = 1.0 - ADAM_B2 ** ADAM_STEP

VMEM_LIMIT_BYTES = 56 * 1024 * 1024

ZC_QB, ZC_QLAT, ZC_CKV, ZC_KPE, ZC_KB, ZC_VB, ZC_GA, ZC_GB = 0, 8, 10, 11, 12, 14, 16, 24
Z_WIDTH = 32 * HEAD_W

BIG = [
    ("w_in", 1024, 3232, 1), ("w_qb", 256, 768, 1), ("w_kvb", 128, 1024, 1), ("w_oa", 512, 1024, 1),
    ("w_ob", 512, 1024, 1), ("w_o", 1024, 1024, 0), ("w_up", 1024, 4096, 1), ("w_down", 4096, 1024, 0),
    ("w_ple_gate", 1024, 1024, 0), ("w_ple", 256, 1024, 1),
]
BIG_BY_NAME = {e[0]: e for e in BIG}
PACK_W = 1024
PACK_ALIGN = 64
SLAB_EARLY = ["w_qb", "w_kvb"]
SLAB_LATE = ["w_oa", "w_ob", "w_ple"]

SMALL = [("g_mix", 1024), ("g_qa", 256), ("g_kva", 128), ("g_qn", 64), ("g_kn", 64), ("g_mlp", 1024),
         ("g_ple", 1024), ("g_final", 1024)]
SMALL_N = sum(n for _, n in SMALL)
LOSS_ROW0 = 40
SMALL_ROWS = 48


def _params(sem):
    return pltpu.CompilerParams(dimension_semantics=sem, vmem_limit_bytes=VMEM_LIMIT_BYTES)


def _sigmoid(v):
    return 1.0 / (1.0 + jnp.exp(-v))


def _perm(v, p_ref):
    pm = p_ref[...]
    hi = v.astype(BF16)
    lo = (v - hi.astype(F32)).astype(BF16)
    return (jnp.dot(hi, pm, preferred_element_type=F32) + jnp.dot(lo, pm, preferred_element_type=F32))


def _rms(v, g, n):
    rs = lax.rsqrt(jnp.sum(v * v, axis=-1, keepdims=True) * (1.0 / n) + EPS)
    return v * rs * g


def _rms_bwd(dy, v, g, n):
    rs = lax.rsqrt(jnp.sum(v * v, axis=-1, keepdims=True) * (1.0 / n) + EPS)
    vh = v * rs
    dyg = dy * g
    dx = rs * (dyg - vh * (jnp.sum(dyg * vh, axis=-1, keepdims=True) * (1.0 / n)))
    return dx, jnp.sum(dy * vh, axis=0, keepdims=True)


def _ride(body, grid, rider):
    if rider is None:
        return body, [], [], [], []
    n_x, n_sem = len(rider.srcs), len(rider.scratch)

    def wrapped(*refs):
        ids = [pl.program_id(a) for a in range(len(grid))]
        n_in = len(refs) - n_sem - 2 * n_x - rider.n_core_out - rider.n_core_scratch
        core_in, srcs = refs[:n_in], refs[n_in:n_in + n_x]
        core_out = refs[n_in + n_x:n_in + n_x + rider.n_core_out]
        dsts = refs[n_in + n_x + rider.n_core_out:n_in + 2 * n_x + rider.n_core_out]
        core_scr = refs[n_in + 2 * n_x + rider.n_core_out:len(refs) - n_sem]
        sems = refs[len(refs) - n_sem:]

        @pl.when(functools.reduce(jnp.logical_and, [a == 0 for a in ids]))
        def _():
            rider.start(srcs, dsts, *sems)

        body(*core_in, *core_out, *core_scr)

        @pl.when(functools.reduce(jnp.logical_and, [a == n - 1 for a, n in zip(ids, grid)]))
        def _():
            rider.finish(srcs, dsts, *sems)

    hbm = pl.BlockSpec(memory_space=pl.ANY)
    return wrapped, list(rider.srcs), [hbm] * n_x, list(rider.out_shapes), list(rider.scratch)


def _mm(name, a, b, *, trans_b=False, b_slots=False, out_dtypes=(F32,), epi=None, extras=(), consts=(), accs=(), tm=512, tn=None,
        rider=None):
    a_ops = [o if isinstance(o, tuple) else (o, 0, o.shape[1]) for o in (a if isinstance(a, list) else [a])]
    b_ops = b if isinstance(b, list) else [b]
    assert len(a_ops) == len(b_ops) and not (b_slots and (trans_b or len(b_ops) > 1))
    m = a_ops[0][0].shape[0]
    if b_slots:
        n, tn = b.shape[0] * b.shape[2], b.shape[2]
    else:
        first = b_ops[0][0] if isinstance(b_ops[0], tuple) else b_ops[0]
        n = first.shape[-2] if trans_b else first.shape[1]
        tn = n if tn is None else min(tn, n)
    tm = min(tm, m)
    assert m % tm == 0 and n % tn == 0
    extras = [e if isinstance(e, tuple) else (e, 0) for e in extras]
    n_p, n_ex, n_c, n_out, n_acc = len(a_ops), len(extras), len(consts), len(out_dtypes), len(accs)
    dims = (((1,), (1,)), ((), ())) if trans_b else (((1,), (0,)), ((), ()))

    def body(*refs):
        acc = None
        for a_ref, b_ref in zip(refs[:n_p], refs[n_p:2 * n_p]):
            part = lax.dot_general(a_ref[...].astype(BF16), b_ref[...].astype(BF16), dims, preferred_element_type=F32)
            acc = part if acc is None else acc + part
        rest = refs[2 * n_p:]
        res = (acc,) if epi is None else epi(acc, *[e[...] for e in rest[:n_ex + n_c]])
        o_refs = rest[n_ex + n_c:]
        for o_ref, r in zip(o_refs[:n_out], res[:n_out]):
            o_ref[...] = r.astype(o_ref.dtype)
        if n_acc:
            first_step = jnp.logical_and(pl.program_id(0) == 0, pl.program_id(1) == 0)

            @pl.when(first_step)
            def _():
                for o_ref, r in zip(o_refs[n_out:], res[n_out:]):
                    o_ref[...] = r

            @pl.when(jnp.logical_not(first_step))
            def _():
                for o_ref, r in zip(o_refs[n_out:], res[n_out:]):
                    o_ref[...] += r

    def b_spec(op, k_i):
        if b_slots:
            return pl.BlockSpec((None, k_i, tn), lambda j, i: (j, 0, 0))
        if not isinstance(op, tuple):
            return pl.BlockSpec((tn, k_i), lambda j, i: (j, 0)) if trans_b else pl.BlockSpec((k_i, tn), lambda j, i: (0, j))
        assert trans_b
        if len(op) == 2:
            return pl.BlockSpec((None, tn, k_i), lambda j, i, slot=op[1]: (slot, j, 0))
        return pl.BlockSpec((tn, k_i), lambda j, i, blk=op[1]: (j, blk))

    grid = (n // tn, m // tm)
    if rider is not None:
        rider.n_core_out, rider.n_core_scratch = n_out + n_acc, 0
    body, x_in, x_spec, x_out, x_scr = _ride(body, grid, rider)
    a_specs = [pl.BlockSpec((tm, k_i), lambda j, i, blk=blk: (i, blk)) for _, blk, k_i in a_ops]
    b_specs = [b_spec(op, k_i) for op, (_, _, k_i) in zip(b_ops, a_ops)]
    t_spec = pl.BlockSpec((tm, tn), lambda j, i: (i, j))
    e_specs = [pl.BlockSpec((tm, tn), lambda j, i, off=off: (i, j + off)) for _, off in extras]
    c_specs = [pl.BlockSpec(c.shape, lambda j, i: (0, 0)) for c in consts]
    acc_specs = [pl.BlockSpec(sh, lambda j, i: (0, 0)) for sh in accs]
    sem = ("parallel", "parallel") if rider is None and not n_acc else ("arbitrary", "arbitrary")
    outs = pl.pallas_call(
        body, out_shape=[jax.ShapeDtypeStruct((m, n), d) for d in out_dtypes] + [jax.ShapeDtypeStruct(sh, F32) for sh in accs] + x_out,
        grid=grid, in_specs=a_specs + b_specs + e_specs + c_specs + x_spec, out_specs=[t_spec] * n_out + acc_specs + x_spec,
        scratch_shapes=x_scr, compiler_params=_params(sem),
        name=name)(*[o[0] for o in a_ops], *[o[0] if isinstance(o, tuple) else o for o in b_ops], *[e for e, _ in extras], *consts, *x_in)
    return outs[0] if len(outs) == 1 else outs


def _per_head(fn, heads, n_tiled, n_out):
    def run(*args):
        res = [fn(*[a[:, hd * HEAD_W:(hd + 1) * HEAD_W] for a in args[:n_tiled]], *args[n_tiled:]) for hd in range(heads)]
        tiles = [jnp.concatenate([r[k] for r in res], axis=1) for k in range(n_out)]
        sums = [functools.reduce(lambda u, v: u + v, [r[k] for r in res]) for k in range(n_out, len(res[0]))]
        return (*tiles, *sums)
    return run


def _mm_tn(name, a, b, *, out_dtype=BF16, out_slots=False, tk=1024, tn=1024, tt=4096):
    t, k = a.shape
    n = b.shape[1]
    tk, tn = min(tk, k), min(tn, n)
    if a.dtype == F32 or b.dtype == F32:
        tt = tt // 2
    if k == tk and n == tn:
        tt = tt // 2
    tt = min(tt, t)
    assert b.shape[0] == t and k % tk == 0 and n % tn == 0 and t % tt == 0
    nt = t // tt

    def body(a_ref, b_ref, o_ref, acc):
        part = lax.dot_general(a_ref[...].astype(BF16), b_ref[...].astype(BF16), (((0,), (0,)), ((), ())),
                               preferred_element_type=F32)

        @pl.when(pl.program_id(2) == 0)
        def _():
            acc[...] = part

        @pl.when(pl.program_id(2) != 0)
        def _():
            acc[...] += part

        @pl.when(pl.program_id(2) == nt - 1)
        def _():
            o_ref[...] = acc[...].astype(o_ref.dtype)

    if out_slots:
        out_shape, out_spec = (n // tn, k, tn), pl.BlockSpec((None, tk, tn), lambda ki, ni, ti: (ni, ki, 0))
    else:
        out_shape, out_spec = (k, n), pl.BlockSpec((tk, tn), lambda ki, ni, ti: (ki, ni))
    return pl.pallas_call(
        body, out_shape=jax.ShapeDtypeStruct(out_shape, out_dtype), grid=(k // tk, n // tn, nt),
        in_specs=[pl.BlockSpec((tt, tk), lambda ki, ni, ti: (ti, ki)), pl.BlockSpec((tt, tn), lambda ki, ni, ti: (ti, ni))],
        out_specs=out_spec, scratch_shapes=[pltpu.VMEM((tk, tn), F32)],
        compiler_params=_params(("parallel", "parallel", "arbitrary")), name=name)(a, b)


def _rowwise(name, fn, ins, outs, *, consts=(), pos=(), accs=(), heads=1, tm=512, seq=None, rider=None):
    t = ins[0][0].shape[0]
    tm = min(tm, t if seq is None else seq)
    assert t % tm == 0 and (seq is None or seq % tm == 0)
    n_in, n_pos, n_c, n_out, n_acc = len(ins), len(pos), len(consts), len(outs), len(accs)

    def body(*refs):
        vals = [r[...] for r in refs[:n_in + n_pos + n_c]]
        res = fn(*vals)
        o_refs = refs[n_in + n_pos + n_c:]
        for o_ref, r in zip(o_refs[:n_out], res[:n_out]):
            o_ref[...] = r.astype(o_ref.dtype)
        if n_acc:
            first = jnp.logical_and(pl.program_id(0) == 0, pl.program_id(1) == 0)

            @pl.when(first)
            def _():
                for o_ref, r in zip(o_refs[n_out:], res[n_out:]):
                    o_ref[...] = r

            @pl.when(jnp.logical_not(first))
            def _():
                for o_ref, r in zip(o_refs[n_out:], res[n_out:]):
                    o_ref[...] += r

    def tiled(width, c0, per_head):
        return pl.BlockSpec((tm, width), (lambda h, i: (i, c0 + h)) if per_head else (lambda h, i: (i, c0)))

    in_specs = [tiled(w, c0, ph) for _, w, c0, ph in ins]
    if n_pos:
        nblk = seq // tm
        in_specs += [pl.BlockSpec((tm, a.shape[1]), lambda h, i: (i % nblk, 0)) for a in pos]
    in_specs += [pl.BlockSpec(a.shape, lambda h, i: (0, 0)) for a in consts]
    out_specs = [tiled(w, 0, ph) for _, _, w, ph in outs] + [pl.BlockSpec(s, lambda h, i: (0, 0)) for s in accs]
    out_shape = [jax.ShapeDtypeStruct((t, c), d) for c, d, _, _ in outs] + [jax.ShapeDtypeStruct(s, F32) for s in accs]
    sem = ("arbitrary", "arbitrary") if n_acc or rider is not None else ("parallel", "parallel")
    grid = (heads, t // tm)
    if rider is not None:
        rider.n_core_out, rider.n_core_scratch = n_out + n_acc, 0
    body, x_in, x_spec, x_out, x_scr = _ride(body, grid, rider)
    res = pl.pallas_call(body, out_shape=out_shape + x_out, grid=grid, in_specs=in_specs + x_spec, out_specs=out_specs + x_spec,
                         scratch_shapes=x_scr, compiler_params=_params(sem), name=name)(*[a for a, _, _, _ in ins], *pos, *consts, *x_in)
    return res[0] if len(res) == 1 else res


ATTN_HEADS_PER_STEP = 4


def _attn_fwd(name, q, k, kc0, v, vc0, *, heads, group, nseq, seq, tq=512, rider=None):
    tq = min(tq, seq)
    nq = seq // tq
    hp = ATTN_HEADS_PER_STEP
    grid = (heads // hp, nseq, nq)
    shared = group > 1
    assert group % hp == 0 if shared else (kc0 % hp == 0 and vc0 % hp == 0)

    def body(q_ref, k_ref, v_ref, o_ref, lse_ref):
        for j in range(hp):
            cols = slice(j * HEAD_W, (j + 1) * HEAD_W)
            kj = (k_ref[...] if shared else k_ref[:, cols]).astype(BF16)
            vj = (v_ref[...] if shared else v_ref[:, cols]).astype(BF16)
            s = lax.dot_general(q_ref[:, cols], kj, (((1,), (1,)), ((), ())), preferred_element_type=F32)
            m = jnp.max(s, axis=-1, keepdims=True)
            p = jnp.exp(s - m)
            l = jnp.sum(p, axis=-1, keepdims=True)
            o = jnp.dot(p.astype(BF16), vj, preferred_element_type=F32)
            o_ref[:, cols] = (o * (1.0 / l)).astype(o_ref.dtype)
            lse_ref[j] = m + jnp.log(l)

    if rider is not None:
        rider.n_core_out, rider.n_core_scratch = 2, 0
    body, x_in, x_spec, x_out, x_scr = _ride(body, grid, rider)
    q_spec = pl.BlockSpec((tq, hp * HEAD_W), lambda h, b, i: (b * nq + i, h))
    if shared:
        k_spec = pl.BlockSpec((seq, HEAD_W), lambda h, b, i: (b, kc0 + (h * hp) // group))
        v_spec = pl.BlockSpec((seq, HEAD_W), lambda h, b, i: (b, vc0 + (h * hp) // group))
    else:
        k_spec = pl.BlockSpec((seq, hp * HEAD_W), lambda h, b, i: (b, kc0 // hp + h))
        v_spec = pl.BlockSpec((seq, hp * HEAD_W), lambda h, b, i: (b, vc0 // hp + h))
    lse_spec = pl.BlockSpec((hp, tq, 1), lambda h, b, i: (h, b * nq + i, 0))
    sem = ("parallel",) * 3 if rider is None else ("arbitrary",) * 3
    return pl.pallas_call(
        body, out_shape=[jax.ShapeDtypeStruct(q.shape, BF16), jax.ShapeDtypeStruct((heads, q.shape[0], 1), F32)] + x_out,
        grid=grid, in_specs=[q_spec, k_spec, v_spec] + x_spec, out_specs=[q_spec, lse_spec] + x_spec, scratch_shapes=x_scr,
        compiler_params=_params(sem), name=name)(q, k, v, *x_in)


def _attn_bwd(name, q, k, kc0, v, vc0, o, do, lse, *, heads, group, nseq, seq, tq=1024, rider=None):
    tq = min(tq, seq)
    nq = seq // tq
    hk = heads // group
    t = q.shape[0]
    grid = (hk, nseq, group, nq)

    def body(q_ref, k_ref, v_ref, o_ref, do_ref, lse_ref, dq_ref, dk_ref, dv_ref, dk_acc, dv_acc):
        g, i = pl.program_id(2), pl.program_id(3)
        qv, kv, vv, dov = q_ref[...], k_ref[...].astype(BF16), v_ref[...].astype(BF16), do_ref[...]
        s = lax.dot_general(qv, kv, (((1,), (1,)), ((), ())), preferred_element_type=F32)
        pn = jnp.exp(s - lse_ref[...])
        dp = lax.dot_general(dov, vv, (((1,), (1,)), ((), ())), preferred_element_type=F32)
        delta = jnp.sum(dov.astype(F32) * o_ref[...].astype(F32), axis=-1, keepdims=True)
        ds = (pn * (dp - delta)).astype(BF16)
        dq_ref[...] = jnp.dot(ds, kv, preferred_element_type=F32)
        dk_part = lax.dot_general(ds, qv, (((0,), (0,)), ((), ())), preferred_element_type=F32)
        dv_part = lax.dot_general(pn.astype(BF16), dov, (((0,), (0,)), ((), ())), preferred_element_type=F32)
        first = jnp.logical_and(g == 0, i == 0)

        @pl.when(first)
        def _():
            dk_acc[...] = dk_part
            dv_acc[...] = dv_part

        @pl.when(jnp.logical_not(first))
        def _():
            dk_acc[...] += dk_part
            dv_acc[...] += dv_part

        @pl.when(jnp.logical_and(g == group - 1, i == nq - 1))
        def _():
            dk_ref[...] = dk_acc[...].astype(dk_ref.dtype)
            dv_ref[...] = dv_acc[...].astype(dv_ref.dtype)

    if rider is not None:
        rider.n_core_out, rider.n_core_scratch = 3, 2
    body, x_in, x_spec, x_out, x_scr = _ride(body, grid, rider)
    q_spec = pl.BlockSpec((tq, HEAD_W), lambda kh, b, g, i: (b * nq + i, kh * group + g))
    kv_out = pl.BlockSpec((seq, HEAD_W), lambda kh, b, g, i: (b, kh))
    lse_spec = pl.BlockSpec((None, tq, 1), lambda kh, b, g, i: (kh * group + g, b * nq + i, 0))
    sem = ("parallel", "parallel", "arbitrary", "arbitrary") if rider is None else ("arbitrary",) * 4
    return pl.pallas_call(
        body,
        out_shape=[jax.ShapeDtypeStruct(q.shape, F32), jax.ShapeDtypeStruct((t, hk * HEAD_W), BF16),
                   jax.ShapeDtypeStruct((t, hk * HEAD_W), BF16)] + x_out,
        grid=grid,
        in_specs=[q_spec, pl.BlockSpec((seq, HEAD_W), lambda kh, b, g, i: (b, kc0 + kh)),
                  pl.BlockSpec((seq, HEAD_W), lambda kh, b, g, i: (b, vc0 + kh)), q_spec, q_spec, lse_spec] + x_spec,
        out_specs=[q_spec, kv_out, kv_out] + x_spec,
        scratch_shapes=[pltpu.VMEM((seq, HEAD_W), F32), pltpu.VMEM((seq, HEAD_W), F32)] + x_scr,
        compiler_params=_params(sem), name=name)(q, k, v, o, do, lse, *x_in)


def _place():
    return lax.axis_index("x"), lax.axis_index("y"), lax.axis_index("c")


def _other_chips(x, y):
    return [(1 - x, y), (x, 1 - y), (1 - x, 1 - y)]


class _Exchange:
    def __init__(self, kind, srcs):
        assert kind in ("gather", "scatter")
        self.kind, self.srcs = kind, list(srcs)
        n = len(self.srcs)
        self.out_shapes = [jax.ShapeDtypeStruct((4, *a.shape[-2:]), a.dtype) for a in self.srcs]
        self.scratch = [pltpu.SemaphoreType.DMA((3 * n,)), pltpu.SemaphoreType.DMA((3 * n,)), pltpu.SemaphoreType.DMA((n,))]
        self.n_core_out = self.n_core_scratch = 0

    def _copies(self, j, src_ref, out_ref, send_sems, recv_sems, landing):
        x, y, c = _place()

        def remote(k, s, d, to):
            return pltpu.make_async_remote_copy(src_ref=s, dst_ref=d, send_sem=send_sems.at[3 * j + k], recv_sem=recv_sems.at[3 * j + k],
                                                device_id=to, device_id_type=MESH)

        me = 2 * x + y
        part = (lambda i: src_ref) if self.kind == "gather" else (lambda i: src_ref.at[i])
        if landing:
            return [remote(k, part(me), out_ref.at[2 * px + py], (px, py, c)) for k, (px, py) in enumerate(_other_chips(x, y))]
        return [remote(k, part(2 * px + py), out_ref.at[me], (px, py, c)) for k, (px, py) in enumerate(_other_chips(x, y))]

    def _local(self, j, src_ref, out_ref, local_sems):
        x, y, _ = _place()
        me = 2 * x + y
        return pltpu.make_async_copy(src_ref if self.kind == "gather" else src_ref.at[me], out_ref.at[me], local_sems.at[j])

    def start(self, src_refs, out_refs, send_sems, recv_sems, local_sems):
        for j, (src_ref, out_ref) in enumerate(zip(src_refs, out_refs)):
            self._local(j, src_ref, out_ref, local_sems).start()
            for mine in self._copies(j, src_ref, out_ref, send_sems, recv_sems, False):
                mine.start()

    def finish(self, src_refs, out_refs, send_sems, recv_sems, local_sems):
        for j, (src_ref, out_ref) in enumerate(zip(src_refs, out_refs)):
            for landed in self._copies(j, src_ref, out_ref, send_sems, recv_sems, True):
                landed.wait_recv()
        for j, (src_ref, out_ref) in enumerate(zip(src_refs, out_refs)):
            for mine in self._copies(j, src_ref, out_ref, send_sems, recv_sems, False):
                mine.wait_send()
            self._local(j, src_ref, out_ref, local_sems).wait()


def _gather_by_halves(name, srcs):
    n = len(srcs)

    def body(*refs):
        x, y, c = _place()
        me = 2 * x + y
        local_sems = refs[-1]
        copies = []
        for j in range(n):
            src_ref, out_ref, send_sems, recv_sems = refs[j], refs[n + j], refs[2 * n + 2 * j], refs[2 * n + 2 * j + 1]
            half = srcs[j].shape[0] // 2
            rows_c = pl.ds(pl.multiple_of(c * half, half), half)
            rows_s = pl.ds(pl.multiple_of((1 - c) * half, half), half)

            def remote(k, s_ref, d_ref, to, send_sems=send_sems, recv_sems=recv_sems):
                return pltpu.make_async_remote_copy(src_ref=s_ref, dst_ref=d_ref, send_sem=send_sems.at[k], recv_sem=recv_sems.at[k],
                                                    device_id=to, device_id_type=MESH)

            local = pltpu.make_async_copy(src_ref, out_ref.at[me], local_sems.at[j])
            local.start()
            chips = _other_chips(x, y)
            sent = [remote(k, src_ref.at[rows_c], out_ref.at[me, rows_c], (px, py, c)) for k, (px, py) in enumerate(chips)]
            landing = [remote(k, src_ref.at[rows_c], out_ref.at[2 * px + py, rows_c], (px, py, c)) for k, (px, py) in enumerate(chips)]
            passed = [remote(3 + k, out_ref.at[2 * px + py, rows_c], out_ref.at[2 * px + py, rows_c], (x, y, 1 - c))
                      for k, (px, py) in enumerate(chips)]
            from_sibling = [remote(3 + k, out_ref.at[2 * px + py, rows_s], out_ref.at[2 * px + py, rows_s], (x, y, 1 - c))
                            for k, (px, py) in enumerate(chips)]
            for cp in sent:
                cp.start()
            copies.append((local, sent, landing, passed, from_sibling))
        for local, sent, landing, passed, from_sibling in copies:
            for k in range(3):
                landing[k].wait_recv()
                passed[k].start()
        for local, sent, landing, passed, from_sibling in copies:
            for k in range(3):
                from_sibling[k].wait_recv()
            for cp in sent + passed:
                cp.wait_send()
            local.wait()

    sems = [pltpu.SemaphoreType.DMA((6,)) for _ in range(2 * n)] + [pltpu.SemaphoreType.DMA((n,))]
    return pl.pallas_call(
        body, out_shape=[jax.ShapeDtypeStruct((4, *a.shape), a.dtype) for a in srcs],
        in_specs=[pl.BlockSpec(memory_space=pl.ANY)] * n, out_specs=[pl.BlockSpec(memory_space=pltpu.VMEM)] * n,
        scratch_shapes=sems, compiler_params=pltpu.CompilerParams(vmem_limit_bytes=VMEM_LIMIT_BYTES), name=name)(*srcs)


def _adamw(w, g, m, v):
    m = ADAM_B1 * m + (1.0 - ADAM_B1) * g
    v = ADAM_B2 * v + (1.0 - ADAM_B2) * (g * g)
    delta = -ADAM_LR * ((m / M_HAT_DIV) / (jnp.sqrt(v / V_HAT_DIV) + ADAM_EPS) + ADAM_WD * w)
    return delta, m, v


def _small_allreduce_adamw(part, w, m, v):
    def body(part_ref, w_ref, m_ref, v_ref, g_out, d_out, m_out, v_out, loss_out, buf, send_sems, recv_sems):
        x, y, c = _place()
        me = 4 * x + 2 * y + c
        buf[me] = part_ref[...]

        def flip(k):
            fx, fy, fc = (k >> 2) & 1, (k >> 1) & 1, k & 1
            px, py, pc = (1 - x if fx else x), (1 - y if fy else y), (1 - c if fc else c)
            return (px, py, pc), 4 * px + 2 * py + pc

        def copy(k, slot):
            return pltpu.make_async_remote_copy(
                src_ref=part_ref, dst_ref=buf.at[slot], send_sem=send_sems.at[k - 1], recv_sem=recv_sems.at[k - 1],
                device_id=flip(k)[0], device_id_type=MESH)

        sent = [copy(k, me) for k in range(1, 8)]
        for cp in sent:
            cp.start()
        for k in range(1, 8):
            copy(k, flip(k)[1]).wait_recv()
        for cp in sent:
            cp.wait_send()
        tot = buf[0]
        for j in range(1, 8):
            tot = tot + buf[j]
        delta, m_new, v_new = _adamw(w_ref[...], tot, m_ref[...], v_ref[...])
        g_out[...] = tot
        d_out[...] = delta
        m_out[...] = m_new
        v_out[...] = v_new
        loss_out[...] = jnp.sum(tot[LOSS_ROW0:LOSS_ROW0 + 8, :]).reshape(1, 1)

    vm = pl.BlockSpec(memory_space=pltpu.VMEM)
    shp = jax.ShapeDtypeStruct((SMALL_ROWS, 128), F32)
    return pl.pallas_call(
        body, out_shape=[shp, shp, shp, shp, jax.ShapeDtypeStruct((1, 1), F32)],
        in_specs=[vm, vm, vm, vm], out_specs=[vm, vm, vm, vm, vm],
        scratch_shapes=[pltpu.VMEM((8, SMALL_ROWS, 128), F32), pltpu.SemaphoreType.DMA((7,)), pltpu.SemaphoreType.DMA((7,))],
        name="small_allreduce_adamw")(part, w, m, v)


def _row_tile(rows, cap):
    return max(t for t in range(16, min(rows, cap) + 1, 16) if rows % t == 0)


def _reduce_pair(name, parts):
    _, rows, w = parts.shape
    tr = _row_tile(rows, 576)
    nt = rows // tr

    def body(p_ref, o_ref, mine, theirs, send_sems, recv_sems):
        i = pl.program_id(0)
        x, y, c = _place()

        def copy(t):
            rows_t = pl.ds(pl.multiple_of(t * tr, tr), tr)
            return pltpu.make_async_remote_copy(src_ref=mine.at[rows_t], dst_ref=theirs.at[rows_t], send_sem=send_sems.at[t],
                                                recv_sem=recv_sems.at[t], device_id=(x, y, 1 - c), device_id_type=MESH)

        @pl.when(i < nt)
        def _():
            mine[pl.ds(pl.multiple_of(i * tr, tr), tr), :] = (
                (p_ref[0].astype(F32) + p_ref[1].astype(F32)) + p_ref[2].astype(F32)) + p_ref[3].astype(F32)
            copy(i).start()

        @pl.when(i >= nt)
        def _():
            t = i - nt
            copy(t).wait()
            rows_t = pl.ds(pl.multiple_of(t * tr, tr), tr)
            o_ref[...] = mine[rows_t, :] + theirs[rows_t, :]

    return pl.pallas_call(
        body, out_shape=jax.ShapeDtypeStruct((rows, w), F32), grid=(2 * nt,),
        in_specs=[pl.BlockSpec((4, tr, w), lambda i: (0, jnp.minimum(i, nt - 1), 0))],
        out_specs=pl.BlockSpec((tr, w), lambda i: (jnp.maximum(i - nt, 0), 0)),
        scratch_shapes=[pltpu.VMEM((rows, w), F32), pltpu.VMEM((rows, w), F32), pltpu.SemaphoreType.DMA((nt,)),
                        pltpu.SemaphoreType.DMA((nt,))],
        compiler_params=_params(("arbitrary",)), name=name)(parts)


def _adamw_shard(name, g, w, m, v):
    rows, cols = w.shape
    tr = _row_tile(rows, 256)

    def body(g_ref, w_ref, m_ref, v_ref, d_out, m_out, v_out):
        delta, m_new, v_new = _adamw(w_ref[...], g_ref[...], m_ref[...], v_ref[...])
        d_out[...] = delta
        m_out[...] = m_new
        v_out[...] = v_new

    t_spec = pl.BlockSpec((tr, cols), lambda i: (i, 0))
    shp = jax.ShapeDtypeStruct((rows, cols), F32)
    return pl.pallas_call(body, out_shape=[shp] * 3, grid=(rows // tr,), in_specs=[t_spec] * 4, out_specs=[t_spec] * 3,
                          compiler_params=_params(("parallel",)), name=name)(g, w, m, v)


def _shard_shape(name):
    _, r, c, ax = BIG_BY_NAME[name]
    return (r, c // 4) if ax == 1 else (r // 4, c)


def _pad_rows(a, axis):
    pad = [(0, 0)] * a.ndim
    pad[axis] = (0, -a.shape[axis] % PACK_ALIGN)
    return jnp.pad(a, pad)


def _pack_shards(names, shards, dtype):
    return _pad_rows(jnp.concatenate([s.astype(dtype).reshape(-1, PACK_W) for s in shards], axis=0), 0)


def _unpack_shards(names, slab):
    out, off = [], 0
    for name in names:
        rs, cs = _shard_shape(name)
        n = rs * cs // PACK_W
        out.append(slab[off:off + n].reshape(rs, cs))
        off += n
    return out


def _unpack_full(names, slabs):
    out, off = [], 0
    for name in names:
        _, r, c, ax = BIG_BY_NAME[name]
        n = r * c // 4 // PACK_W
        seg = slabs[:, off:off + n]
        out.append(seg.reshape(4, r, c // 4).transpose(1, 0, 2).reshape(r, c) if ax == 1 else seg.reshape(r, c))
        off += n
    return out


def _pack_full(names, mats, dtype):
    segs = []
    for name, a in zip(names, mats):
        _, r, c, ax = BIG_BY_NAME[name]
        a = a.astype(dtype)
        a = a.reshape(r, 4, c // 4).transpose(1, 0, 2) if ax == 1 else a
        segs.append(a.reshape(4, -1, PACK_W))
    return _pad_rows(jnp.concatenate(segs, axis=1), 1)


def _pad_heads_cols(wm, heads, d):
    k = wm.shape[0]
    return jnp.pad(wm.reshape(k, heads, d), ((0, 0), (0, 0), (0, HEAD_W - d))).reshape(k, heads * HEAD_W)


def _unpad_heads_cols(wm, heads, d):
    k = wm.shape[0]
    return wm.reshape(k, heads, HEAD_W)[:, :, :d].reshape(k, heads * d)


def _win_ext(w_in):
    o = np.cumsum([0, Q_LORA, KV_LORA, QK_ROPE, H_B * HD_B, KV_B * HD_B, KV_B * HD_B, D_MODEL, D_MODEL])
    pc = lambda a, n: jnp.pad(a, ((0, 0), (0, n - a.shape[1])))
    return jnp.concatenate([
        _pad_heads_cols(w_in[:, o[3]:o[4]], H_B, HD_B), w_in[:, o[0]:o[1]], w_in[:, o[1]:o[2]], pc(w_in[:, o[2]:o[3]], HEAD_W),
        _pad_heads_cols(w_in[:, o[4]:o[5]], KV_B, HD_B), _pad_heads_cols(w_in[:, o[5]:o[6]], KV_B, HD_B),
        w_in[:, o[6]:o[7]], w_in[:, o[7]:o[8]]], axis=1)


def _win_unext(blocks):
    c = HEAD_W
    qb, mid, ga, gb = blocks
    at = lambda zc: (zc - ZC_QLAT) * c
    return jnp.concatenate([
        mid[:, at(ZC_QLAT):at(ZC_CKV)], mid[:, at(ZC_CKV):at(ZC_KPE)], mid[:, at(ZC_KPE):at(ZC_KPE) + QK_ROPE],
        _unpad_heads_cols(qb, H_B, HD_B), _unpad_heads_cols(mid[:, at(ZC_KB):at(ZC_VB)], KV_B, HD_B),
        _unpad_heads_cols(mid[:, at(ZC_VB):at(ZC_GA)], KV_B, HD_B), ga, gb], axis=1)


def _wkv_ext(w_kvb):
    wk = w_kvb.reshape(KV_LORA, H_A, QK_NOPE + V_DIM_A)
    k_cols = jnp.pad(wk[:, :, :QK_NOPE], ((0, 0), (0, 0), (0, HEAD_W - QK_NOPE))).reshape(KV_LORA, H_A * HEAD_W)
    v_cols = jnp.pad(wk[:, :, QK_NOPE:], ((0, 0), (0, 0), (0, HEAD_W - V_DIM_A))).reshape(KV_LORA, H_A * HEAD_W)
    eye = jnp.pad(jnp.eye(QK_ROPE, dtype=w_kvb.dtype), ((0, 0), (QK_NOPE, HEAD_W - QK_NOPE - QK_ROPE)))
    pe_rows = jnp.concatenate([jnp.tile(eye, (1, H_A)), jnp.zeros((QK_ROPE, H_A * HEAD_W), w_kvb.dtype)], axis=1)
    top = jnp.concatenate([k_cols, v_cols], axis=1)
    return jnp.concatenate([top, pe_rows, jnp.zeros((2 * HEAD_W - KV_LORA - QK_ROPE, 2 * H_A * HEAD_W), w_kvb.dtype)], axis=0)


def _wkv_unext(k_block, v_block):
    k_cols = k_block[:KV_LORA].reshape(KV_LORA, H_A, HEAD_W)[:, :, :QK_NOPE]
    v_cols = v_block[:KV_LORA].reshape(KV_LORA, H_A, HEAD_W)[:, :, :V_DIM_A]
    return jnp.concatenate([k_cols, v_cols], axis=2).reshape(KV_LORA, H_A * (QK_NOPE + V_DIM_A))


def _pad_heads_rows(wm, heads, d):
    n = wm.shape[1]
    return jnp.pad(wm.reshape(heads, d, n), ((0, 0), (0, HEAD_W - d), (0, 0))).reshape(heads * HEAD_W, n)


def _unpad_heads_rows(wm, heads, d):
    n = wm.shape[1]
    return wm.reshape(heads, HEAD_W, n)[:, :d].reshape(heads * d, n)


def _rope_tables(seq):
    def ang(pos, dim):
        inv = np.float32(ROPE_THETA) ** (-np.arange(0, dim, 2, dtype=np.float32) / np.float32(dim))
        return pos.astype(np.float32)[:, None] * inv[None, :]

    def rot(dim):
        r = np.zeros((dim, dim), np.float32)
        half = dim // 2
        r[np.arange(half) + half, np.arange(half)] = -1.0
        r[np.arange(half), np.arange(half) + half] = 1.0
        return r

    def table(blocks):
        cos, sin = np.ones((seq, HEAD_W), np.float32), np.zeros((seq, HEAD_W), np.float32)
        pm = np.zeros((HEAD_W, HEAD_W), np.float32)
        for c0, a in blocks:
            d = 2 * a.shape[1]
            cos[:, c0:c0 + d] = np.concatenate([np.cos(a), np.cos(a)], axis=1)
            sin[:, c0:c0 + d] = np.concatenate([np.sin(a), np.sin(a)], axis=1)
            pm[c0:c0 + d, c0:c0 + d] = rot(d)
        return jnp.asarray(cos), jnp.asarray(sin), jnp.asarray(pm, BF16), jnp.asarray(pm.T, BF16)

    tok = np.arange(seq)
    a1 = ang(tok, QK_ROPE)
    arow, acol = ang(tok // GRID_W, HD_B // 2), ang(tok % GRID_W, HD_B // 2)
    return table([(QK_NOPE, a1)]), table([(0, a1)]), table([(0, arow), (HD_B // 2, acol)])


def _local_step(x, p, tgt, gains, wts, ride=None):
    nb, seq, _ = x.shape
    t = nb * seq
    x0 = x.reshape(t, D_MODEL)
    p2 = p.reshape(t, PLE_DIM)
    tg = tgt.reshape(t, D_MODEL)
    (cq_t, sq_t, pq, pq_t), (ck_t, sk_t, pk, pk_t), (cb_t, sb_t, pb, pb_t) = _rope_tables(seq)
    padg = lambda g: jnp.pad(g, ((0, 0), (0, HEAD_W - g.shape[1])))
    g_qn, g_kn = padg(gains["g_qn"]), padg(gains["g_kn"])

    win = _win_ext(wts["w_in"])
    wqb = _pad_heads_cols(wts["w_qb"], H_A, QK_NOPE + QK_ROPE)
    wkv = _wkv_ext(wts["w_kvb"])

    norm = lambda n: (lambda v, g: (_rms(v, g, n),))
    full = lambda a: (a, a.shape[1], 0, False)
    wts = dict(wts)
    rider_of = lambda kernel_name: None if ride is None else ride["gather"][kernel_name][0]

    def landed(kernel_name, got):
        if ride is not None:
            wts.update(ride["gather"][kernel_name][1](got))

    h = _rowwise("norm_mix", norm(D_MODEL), [full(x0)], [(D_MODEL, BF16, D_MODEL, False)], consts=[gains["g_mix"]])
    res = _mm("in_proj", h, win, tn=2048, rider=rider_of("in_proj"))
    z, got = (res, []) if ride is None else (res[0], res[1:])
    landed("in_proj", got)
    cq = _rowwise("norm_qa", norm(Q_LORA), [(z, Q_LORA, ZC_QLAT // 2, False)], [(Q_LORA, BF16, Q_LORA, False)],
                  consts=[gains["g_qa"]])
    qa = _mm("q_up", cq, wqb)

    def rope_fwd(scale):
        return lambda v, cos, sin, pm: ((v * cos + _perm(v, pm) * sin) * scale,)

    heads_tile = lambda n: (n * HEAD_W, BF16, n * HEAD_W, False)
    q_a = _rowwise("rope_qa", _per_head(rope_fwd(SCALE_A), H_A, 1, 1), [full(qa)], [heads_tile(H_A)],
                   pos=[cq_t, sq_t], consts=[pq], seq=seq)
    ckv = _rowwise("norm_kva", norm(KV_LORA), [(z, HEAD_W, ZC_CKV, False)], [(HEAD_W, BF16, HEAD_W, False)], consts=[gains["g_kva"]])
    kpe = _rowwise("rope_kpe", rope_fwd(1.0), [(z, HEAD_W, ZC_KPE, False)], [(HEAD_W, BF16, HEAD_W, False)],
                   pos=[ck_t, sk_t], consts=[pk], seq=seq)
    kin = jnp.concatenate([ckv, kpe], axis=1)
    kv_a = _mm("kv_up", kin, wkv, out_dtypes=(BF16,))
    o_a, lse_a, *got = _attn_fwd("attn_a_fwd", q_a, kv_a, 0, kv_a, H_A, heads=H_A, group=1, nseq=nb, seq=seq,
                                 rider=rider_of("attn_a_fwd"))
    landed("attn_a_fwd", got)

    def prep_fwd(scale):
        def fn(v, cos, sin, g, pm):
            yv = _rms(v, g, HD_B)
            return ((yv * cos + _perm(yv, pm) * sin) * scale,)
        return fn

    z_qb, z_kb = (z, H_B * HEAD_W, ZC_QB // H_B, False), (z, KV_B * HEAD_W, ZC_KB // KV_B, False)
    res = _rowwise("prep_qb", _per_head(prep_fwd(SCALE_B), H_B, 1, 1), [z_qb], [heads_tile(H_B)],
                   pos=[cb_t, sb_t], consts=[g_qn, pb], seq=seq, rider=rider_of("prep_qb"))
    q_b, got = (res, []) if ride is None else (res[0], res[1:])
    landed("prep_qb", got)
    k_b = _rowwise("prep_kb", _per_head(prep_fwd(1.0), KV_B, 1, 1), [z_kb], [heads_tile(KV_B)],
                   pos=[cb_t, sb_t], consts=[g_kn, pb], seq=seq)
    o_b, lse_b, *got = _attn_fwd("attn_b_fwd", q_b, k_b, 0, z, ZC_VB, heads=H_B, group=H_B // KV_B, nseq=nb, seq=seq,
                                 rider=rider_of("attn_b_fwd"))
    landed("attn_b_fwd", got)
    woa = _pad_heads_rows(wts["w_oa"], H_A, V_DIM_A)
    wob = _pad_heads_rows(wts["w_ob"], H_B, HD_B)
    wo, wup, wdown, wple = wts["w_o"], wts["w_up"], wts["w_down"], wts["w_ple"]

    def residual_norm(acc, r, g):
        xv = r + acc
        return xv, _rms(xv, g, D_MODEL)

    def mix_out(oa, ob, ga, gb, r, g, w_a, w_b, w_out):
        a = jnp.dot(oa, w_a[...], preferred_element_type=F32)
        b = jnp.dot(ob, w_b[...], preferred_element_type=F32)
        mg = (_sigmoid(ga) * a + _sigmoid(gb) * b).astype(BF16)
        return (a, b, mg, *residual_norm(jnp.dot(mg, w_out[...], preferred_element_type=F32), r, g))

    z_ga, z_gb = (z, D_MODEL, ZC_GA // 8, False), (z, D_MODEL, ZC_GB // 8, False)
    wide = lambda d: (D_MODEL, d, D_MODEL, False)
    ya, yb, merged, x1, h2, *got = _rowwise("mix_out", mix_out, [full(o_a), full(o_b), z_ga, z_gb, full(x0)],
                                            [wide(F32), wide(F32), wide(BF16), wide(F32), wide(BF16)],
                                            consts=[gains["g_mlp"], woa, wob, wo], tm=256, rider=rider_of("mix_out"))
    landed("mix_out", got)
    wpg = wts["w_ple_gate"]

    def relu2(acc):
        u = jnp.maximum(acc, 0.0)
        return u, u * u

    u, usq = _mm("mlp_up", h2, wup, b_slots=True, out_dtypes=(BF16, BF16), epi=relu2, tm=1024)
    x2, h3 = _mm("mlp_down", usq, wdown, out_dtypes=(F32, BF16), epi=residual_norm, extras=(x1,), consts=[gains["g_ple"]])

    def tail(x2v, h3v, pv, tv, gf, w_gate, w_emb):
        sg = _sigmoid(jnp.dot(h3v, w_gate[...], preferred_element_type=F32))
        pev = jnp.dot(pv.astype(BF16), w_emb[...], preferred_element_type=F32)
        x3 = x2v + sg * pev
        rs = lax.rsqrt(jnp.sum(x3 * x3, axis=-1, keepdims=True) * (1.0 / D_MODEL) + EPS)
        xh = x3 * rs
        err = xh * gf - tv
        dy = err * (1.0 / D_MODEL)
        dyg = dy * gf
        dx3 = rs * (dyg - xh * (jnp.sum(dyg * xh, axis=-1, keepdims=True) * (1.0 / D_MODEL)))
        return (dx3, dx3 * pev * sg * (1.0 - sg), dx3 * sg,
                jnp.sum(err * err, axis=0, keepdims=True) * (0.5 / D_MODEL), jnp.sum(dy * xh, axis=0, keepdims=True))

    dx3, dgpre, dpe, loss_part, dg_final = _rowwise(
        "tail", tail, [full(x2), full(h3), full(p2), full(tg)], [wide(F32), wide(BF16), wide(BF16)],
        consts=[gains["g_final"].reshape(1, D_MODEL), wpg, wple], accs=[(1, D_MODEL), (1, D_MODEL)], tm=256)

    def norm_bwd(n, with_res):
        if with_res:
            def fn(dh, v, res, g):
                dx, dg = _rms_bwd(dh, v, g, n)
                return dx + res, dg
        else:
            def fn(dh, v, g):
                return _rms_bwd(dh, v, g, n)
        return fn

    dw = {}
    dw["w_ple"] = _mm_tn("dw_ple", p2, dpe)
    dw["w_ple_gate"] = _mm_tn("dw_ple_gate", h3, dgpre)
    norm_res_bwd = norm_bwd(D_MODEL, True)
    dx2, dg_ple = _mm("d_ple_gate", dgpre, wpg, trans_b=True, epi=norm_res_bwd, extras=(x2, dx3), consts=[gains["g_ple"]],
                      accs=[(1, D_MODEL)], tm=256)
    dw["w_down"] = _mm_tn("dw_down", usq, dx2)
    dupre = _mm("d_mlp_down", dx2, wdown, trans_b=True, out_dtypes=(BF16,), epi=lambda acc, uv: (acc * (2.0 * uv.astype(F32)),),
                extras=(u,), tn=2048)
    dw["w_up"] = _mm_tn("dw_up", h2, dupre, out_slots=True)
    n_up = wup.shape[0]
    dx1, dg_mlp = _mm("d_mlp_up", [(dupre, j, wup.shape[2]) for j in range(n_up)], [(wup, j) for j in range(n_up)], trans_b=True,
                      epi=norm_res_bwd, extras=(x1, dx2), consts=[gains["g_mlp"]],
                      accs=[(1, D_MODEL)], tm=256)
    dw["w_o"] = _mm_tn("dw_o", merged, dx1)

    def merge_bwd(dm, ga, gb, a, b, w_a, w_b):
        sa, sb = _sigmoid(ga), _sigmoid(gb)
        da, db = (dm * sa).astype(BF16), (dm * sb).astype(BF16)
        nt = (((1,), (1,)), ((), ()))
        return (da, db, dm * a * sa * (1.0 - sa), dm * b * sb * (1.0 - sb),
                lax.dot_general(da, w_a, nt, preferred_element_type=F32), lax.dot_general(db, w_b, nt, preferred_element_type=F32))

    dya, dyb, dga, dgb, do_a, do_b = _mm("d_out_proj", dx1, wo, trans_b=True, out_dtypes=(BF16,) * 6, epi=merge_bwd,
                                         extras=((z, ZC_GA // 8), (z, ZC_GB // 8), ya, yb), consts=[woa, wob], tm=256)
    dw["w_oa"] = _unpad_heads_rows(_mm_tn("dw_oa", o_a, dya), H_A, V_DIM_A)
    dw["w_ob"] = _unpad_heads_rows(_mm_tn("dw_ob", o_b, dyb), H_B, HD_B)
    res_a = _attn_bwd("attn_a_bwd", q_a, kv_a, 0, kv_a, H_A, o_a, do_a, lse_a, heads=H_A, group=1, nseq=nb, seq=seq,
                      rider=ride and ride["scatter_a"](dw))
    dq_a, dk_a, dv_a = res_a[:3]
    if ride is not None:
        ride["out"]["parts_a"] = res_a[3:]

    def rope_bwd(scale):
        return lambda d, cos, sin, pm_t: ((d * cos + _perm(d * sin, pm_t)) * scale,)

    dqa = _rowwise("rope_qa_bwd", _per_head(rope_bwd(SCALE_A), H_A, 1, 1), [full(dq_a)], [heads_tile(H_A)],
                   pos=[cq_t, sq_t], consts=[pq_t], seq=seq)
    dw["w_qb"] = _unpad_heads_cols(_mm_tn("dw_qb", cq, dqa), H_A, QK_NOPE + QK_ROPE)
    dcq = _mm("d_q_up", dqa, wqb, trans_b=True)
    dq_lat, dg_qa = _rowwise("norm_qa_bwd", norm_bwd(Q_LORA, False), [full(dcq), (z, Q_LORA, ZC_QLAT // 2, False)],
                             [(Q_LORA, BF16, Q_LORA, False)], consts=[gains["g_qa"]], accs=[(1, Q_LORA)])
    dw["w_kvb"] = _wkv_unext(_mm_tn("dw_kv_k", kin, dk_a), _mm_tn("dw_kv_v", kin, dv_a))
    dq_b, dk_b, dv_b, *parts_b = _attn_bwd("attn_b_bwd", q_b, k_b, 0, z, ZC_VB, o_b, do_b, lse_b, heads=H_B, group=H_B // KV_B,
                                               nseq=nb, seq=seq, rider=ride and ride["scatter_b"](dw))
    if ride is not None:
        ride["out"]["parts_b"] = parts_b
    kv_w = H_A * HEAD_W
    dkin = _mm("d_kv_up", [dk_a, dv_a], [(wkv, 0, kv_w), (wkv, 1, kv_w)], trans_b=True)
    dckv, dg_kva = _rowwise("norm_kva_bwd", norm_bwd(KV_LORA, False), [(dkin, HEAD_W, 0, False), (z, HEAD_W, ZC_CKV, False)],
                            [(HEAD_W, BF16, HEAD_W, False)], consts=[gains["g_kva"]], accs=[(1, KV_LORA)])
    dkpe = _rowwise("rope_kpe_bwd", rope_bwd(1.0), [(dkin, HEAD_W, 1, False)], [(HEAD_W, BF16, HEAD_W, False)],
                    pos=[ck_t, sk_t], consts=[pk_t], seq=seq)

    def prep_bwd(scale):
        def fn(d, v, cos, sin, g, pm_t):
            dyv = (d * cos + _perm(d * sin, pm_t)) * scale
            return _rms_bwd(dyv, v, g, HD_B)
        return fn

    dqb, dg_qn = _rowwise("prep_qb_bwd", _per_head(prep_bwd(SCALE_B), H_B, 2, 1), [full(dq_b), z_qb], [heads_tile(H_B)],
                          pos=[cb_t, sb_t], consts=[g_qn, pb_t], accs=[(1, HEAD_W)], seq=seq)
    dkb, dg_kn = _rowwise("prep_kb_bwd", _per_head(prep_bwd(1.0), KV_B, 2, 1), [full(dk_b), z_kb], [heads_tile(KV_B)],
                          pos=[cb_t, sb_t], consts=[g_kn, pb_t], accs=[(1, HEAD_W)], seq=seq)

    dz = [dqb, jnp.concatenate([dq_lat, dckv, dkpe, dkb, dv_b], axis=1), dga, dgb]
    dw["w_in"] = _win_unext([_mm_tn("dw_in_%d" % j, h, blk) for j, blk in enumerate(dz)])
    dx0, dg_mix, *parts_in = _mm("d_in_proj", dz, [(win, j, D_MODEL) for j in range(4)], trans_b=True, epi=norm_res_bwd, extras=(x0, dx1), consts=[gains["g_mix"]],
                                 accs=[(1, D_MODEL)], tm=256, rider=ride and ride["scatter_in"](dw))
    if ride is not None:
        ride["out"]["parts_in"] = parts_in

    dg = {"g_mix": dg_mix, "g_qa": dg_qa, "g_kva": dg_kva, "g_qn": dg_qn[:, :HD_B], "g_kn": dg_kn[:, :HD_B],
          "g_mlp": dg_mlp, "g_ple": dg_ple, "g_final": dg_final}
    return loss_part, dx0.reshape(nb, seq, D_MODEL), dg, dw


def _pack_small(vals, loss_part=None):
    flat = jnp.concatenate([vals[n].reshape(1, -1) for n, _ in SMALL], axis=1)
    loss = jnp.zeros((1, 8 * 128), F32) if loss_part is None else loss_part
    gap = jnp.zeros((1, LOSS_ROW0 * 128 - SMALL_N), F32)
    return jnp.concatenate([flat, gap, loss], axis=1).reshape(SMALL_ROWS, 128)


def _unpack_small(slab, like):
    flat, out, off = slab.reshape(-1), {}, 0
    for n, k in SMALL:
        out[n] = flat[off:off + k].reshape(like[n].shape)
        off += k
    return out


def kernel(x, p, g_mix, w_in, g_qa, w_qb, g_kva, w_kvb, g_qn, g_kn, w_oa, w_ob, w_o, g_mlp, w_up, w_down, g_ple, w_ple_gate, w_ple, g_final, loss_target, m_g_mix, m_w_in, m_g_qa, m_w_qb, m_g_kva, m_w_kvb, m_g_qn, m_g_kn, m_w_oa, m_w_ob, m_w_o, m_g_mlp, m_w_up, m_w_down, m_g_ple, m_w_ple_gate, m_w_ple, m_g_final, v_g_mix, v_w_in, v_g_qa, v_w_qb, v_g_kva, v_w_kvb, v_g_qn, v_g_kn, v_w_oa, v_w_ob, v_w_o, v_g_mlp, v_w_up, v_w_down, v_g_ple, v_w_ple_gate, v_w_ple, v_g_final):
    given = dict(locals())
    order = ["g_mix", "w_in", "g_qa", "w_qb", "g_kva", "w_kvb", "g_qn", "g_kn", "w_oa", "w_ob", "w_o", "g_mlp", "w_up",
             "w_down", "g_ple", "w_ple_gate", "w_ple", "g_final"]
    big_names = [n for n, _, _, _ in BIG]
    local = lambda prefix, names: [given[prefix + n][0] for n in names]
    slab = lambda names: _pack_shards(names, local("", names), BF16)
    bf = lambda n: given[n][0].astype(BF16)
    cols_full = lambda g: g.transpose(1, 0, 2).reshape(g.shape[1], -1)
    rows_full = lambda g: g.reshape(-1, g.shape[2])
    shards_cols = lambda a: a.reshape(a.shape[0], 4, a.shape[1] // 4).transpose(1, 0, 2)
    shards_rows = lambda a: a.reshape(4, a.shape[0] // 4, a.shape[1])
    packed = lambda names, dw: _pack_full(names, [dw[n] for n in names], BF16)

    got_in, got_early = _gather_by_halves("weight_gather_early", [bf("w_in"), slab(SLAB_EARLY)])
    wts = {"w_in": cols_full(got_in), **dict(zip(SLAB_EARLY, _unpack_full(SLAB_EARLY, got_early)))}
    gains = {n: given[n].reshape(1, -1) for n, _ in SMALL}
    ride = {
        "gather": {
            "in_proj": (_Exchange("gather", [slab(SLAB_LATE)]), lambda got: dict(zip(SLAB_LATE, _unpack_full(SLAB_LATE, got[0])))),
            "prep_qb": (_Exchange("gather", [bf("w_o")]), lambda got: {"w_o": rows_full(got[0])}),
            "attn_a_fwd": (_Exchange("gather", [bf("w_up")]), lambda got: {"w_up": got[0]}),
            "attn_b_fwd": (_Exchange("gather", [bf("w_down")]), lambda got: {"w_down": rows_full(got[0])}),
            "mix_out": (_Exchange("gather", [bf("w_ple_gate")]), lambda got: {"w_ple_gate": rows_full(got[0])}),
        },
        "scatter_a": lambda dw: _Exchange("scatter", [dw["w_up"], shards_rows(dw["w_o"]), packed(SLAB_LATE, dw)]),
        "scatter_b": lambda dw: _Exchange("scatter", [shards_rows(dw["w_down"]), shards_rows(dw["w_ple_gate"]), packed(SLAB_EARLY, dw)]),
        "scatter_in": lambda dw: _Exchange("scatter", [shards_cols(dw["w_in"])]),
        "out": {},
    }
    loss_part, grad_x, dg, dw = _local_step(x, p[0], loss_target, gains, wts, ride)

    small = lambda prefix: _pack_small({n: given[prefix + n] for n, _ in SMALL})
    g_s, d_s, m_s, v_s, loss = _small_allreduce_adamw(_pack_small(dg, loss_part), small(""), small("m_"), small("v_"))

    parts = ride["out"]
    grads = {"w_up": _reduce_pair("grad_reduce_up", parts["parts_a"][0]), "w_o": _reduce_pair("grad_reduce_o", parts["parts_a"][1]),
             "w_down": _reduce_pair("grad_reduce_down", parts["parts_b"][0]),
             "w_ple_gate": _reduce_pair("grad_reduce_ple_gate", parts["parts_b"][1]),
             "w_in": _reduce_pair("grad_reduce_in", parts["parts_in"][0])}
    grads.update(zip(SLAB_LATE, _unpack_shards(SLAB_LATE, _reduce_pair("grad_reduce_slab_late", parts["parts_a"][2]))))
    grads.update(zip(SLAB_EARLY, _unpack_shards(SLAB_EARLY, _reduce_pair("grad_reduce_slab_early", parts["parts_b"][2]))))

    res = {}
    for key, slab in (("grad_", g_s), ("delta_", d_s), ("new_m_", m_s), ("new_v_", v_s)):
        for n, val in _unpack_small(slab, given).items():
            res[key + n] = val
    for n in big_names:
        d_w, m_w, v_w = _adamw_shard("adamw_" + n, grads[n], given[n][0], given["m_" + n][0], given["v_" + n][0])
        res["grad_" + n], res["delta_" + n], res["new_m_" + n], res["new_v_" + n] = grads[n][None], d_w[None], m_w[None], v_w[None]
    outs = [loss.reshape(()), grad_x]
    for key in ("grad_", "delta_", "new_m_", "new_v_"):
        outs += [res[key + n] for n in order]
    return tuple(outs)
```

```python
import functools

import numpy as np
import jax
import jax.numpy as jnp
from jax import lax
from jax.experimental import pallas as pl
from jax.experimental.pallas import tpu as pltpu

F32 = jnp.float32
BF16 = jnp.bfloat16
MESH = pl.DeviceIdType.MESH

D_MODEL = 1024
GRID_W = 64
ROPE_THETA = 10000.0
EPS = 1e-6
H_A, QK_NOPE, QK_ROPE, V_DIM_A, Q_LORA, KV_LORA = 8, 64, 32, 64, 256, 128
H_B, KV_B, HD_B = 8, 2, 64
D_FF = 4096
PLE_DIM = 256
HEAD_W = 128
SCALE_A = (QK_NOPE + QK_ROPE) ** -0.5
SCALE_B = HD_B ** -0.5

ADAM_LR, ADAM_B1, ADAM_B2, ADAM_EPS, ADAM_WD, ADAM_STEP = 0.001, 0.9, 0.999, 1e-08, 0.01, 10
M_HAT_DIV = 1.0 - ADAM_B1 ** ADAM_STEP
V_HAT_DIV = 1.0 - ADAM_B2 ** ADAM_STEP

VMEM_LIMIT_BYTES = 56 * 1024 * 1024

ZC_QB, ZC_QLAT, ZC_CKV, ZC_KPE, ZC_KB, ZC_VB, ZC_GA, ZC_GB = 0, 8, 10, 11, 12, 14, 16, 24
Z_WIDTH = 32 * HEAD_W

BIG = [
    ("w_in", 1024, 3232, 1), ("w_qb", 256, 768, 1), ("w_kvb", 128, 1024, 1), ("w_oa", 512, 1024, 1),
    ("w_ob", 512, 1024, 1), ("w_o", 1024, 1024, 0), ("w_up", 1024, 4096, 1), ("w_down", 4096, 1024, 0),
    ("w_ple_gate", 1024, 1024, 0), ("w_ple", 256, 1024, 1),
]
BIG_BY_NAME = {e[0]: e for e in BIG}
PACK_W = 1024
PACK_ALIGN = 64
SLAB_EARLY = ["w_qb", "w_kvb"]
SLAB_LATE = ["w_oa", "w_ob", "w_ple"]

SMALL = [("g_mix", 1024), ("g_qa", 256), ("g_kva", 128), ("g_qn", 64), ("g_kn", 64), ("g_mlp", 1024),
         ("g_ple", 1024), ("g_final", 1024)]
SMALL_N = sum(n for _, n in SMALL)
LOSS_ROW0 = 40
SMALL_ROWS = 48


def _params(sem):
    return pltpu.CompilerParams(dimension_semantics=sem, vmem_limit_bytes=VMEM_LIMIT_BYTES)


def _sigmoid(v):
    return 1.0 / (1.0 + jnp.exp(-v))


def _perm(v, p_ref):
    pm = p_ref[...]
    hi = v.astype(BF16)
    lo = (v - hi.astype(F32)).astype(BF16)
    return (jnp.dot(hi, pm, preferred_element_type=F32) + jnp.dot(lo, pm, preferred_element_type=F32))


def _rms(v, g, n):
    rs = lax.rsqrt(jnp.sum(v * v, axis=-1, keepdims=True) * (1.0 / n) + EPS)
    return v * rs * g


def _rms_bwd(dy, v, g, n):
    rs = lax.rsqrt(jnp.sum(v * v, axis=-1, keepdims=True) * (1.0 / n) + EPS)
    vh = v * rs
    dyg = dy * g
    dx = rs * (dyg - vh * (jnp.sum(dyg * vh, axis=-1, keepdims=True) * (1.0 / n)))
    return dx, jnp.sum(dy * vh, axis=0, keepdims=True)


def _ride(body, grid, rider):
    if rider is None:
        return body, [], [], [], []
    n_x, n_sem = len(rider.srcs), len(rider.scratch)

    def wrapped(*refs):
        ids = [pl.program_id(a) for a in range(len(grid))]
        n_in = len(refs) - n_sem - 2 * n_x - rider.n_core_out - rider.n_core_scratch
        core_in, srcs = refs[:n_in], refs[n_in:n_in + n_x]
        core_out = refs[n_in + n_x:n_in + n_x + rider.n_core_out]
        dsts = refs[n_in + n_x + rider.n_core_out:n_in + 2 * n_x + rider.n_core_out]
        core_scr = refs[n_in + 2 * n_x + rider.n_core_out:len(refs) - n_sem]
        sems = refs[len(refs) - n_sem:]

        @pl.when(functools.reduce(jnp.logical_and, [a == 0 for a in ids]))
        def _():
            rider.start(srcs, dsts, *sems)

        body(*core_in, *core_out, *core_scr)

        @pl.when(functools.reduce(jnp.logical_and, [a == n - 1 for a, n in zip(ids, grid)]))
        def _():
            rider.finish(srcs, dsts, *sems)

    hbm = pl.BlockSpec(memory_space=pl.ANY)
    return wrapped, list(rider.srcs), [hbm] * n_x, list(rider.out_shapes), list(rider.scratch)


def _mm(name, a, b, *, trans_b=False, b_slots=False, out_dtypes=(F32,), epi=None, extras=(), consts=(), accs=(), tm=512, tn=None,
        rider=None):
    a_ops = [o if isinstance(o, tuple) else (o, 0, o.shape[1]) for o in (a if isinstance(a, list) else [a])]
    b_ops = b if isinstance(b, list) else [b]
    assert len(a_ops) == len(b_ops) and not (b_slots and (trans_b or len(b_ops) > 1))
    m = a_ops[0][0].shape[0]
    if b_slots:
        n, tn = b.shape[0] * b.shape[2], b.shape[2]
    else:
        first = b_ops[0][0] if isinstance(b_ops[0], tuple) else b_ops[0]
        n = first.shape[-2] if trans_b else first.shape[1]
        tn = n if tn is None else min(tn, n)
    tm = min(tm, m)
    assert m % tm == 0 and n % tn == 0
    extras = [e if isinstance(e, tuple) else (e, 0) for e in extras]
    n_p, n_ex, n_c, n_out, n_acc = len(a_ops), len(extras), len(consts), len(out_dtypes), len(accs)
    dims = (((1,), (1,)), ((), ())) if trans_b else (((1,), (0,)), ((), ()))

    def body(*refs):
        acc = None
        for a_ref, b_ref in zip(refs[:n_p], refs[n_p:2 * n_p]):
            part = lax.dot_general(a_ref[...].astype(BF16), b_ref[...].astype(BF16), dims, preferred_element_type=F32)
            acc = part if acc is None else acc + part
        rest = refs[2 * n_p:]
        res = (acc,) if epi is None else epi(acc, *[e[...] for e in rest[:n_ex + n_c]])
        o_refs = rest[n_ex + n_c:]
        for o_ref, r in zip(o_refs[:n_out], res[:n_out]):
            o_ref[...] = r.astype(o_ref.dtype)
        if n_acc:
            first_step = jnp.logical_and(pl.program_id(0) == 0, pl.program_id(1) == 0)

            @pl.when(first_step)
            def _():
                for o_ref, r in zip(o_refs[n_out:], res[n_out:]):
                    o_ref[...] = r

            @pl.when(jnp.logical_not(first_step))
            def _():
                for o_ref, r in zip(o_refs[n_out:], res[n_out:]):
                    o_ref[...] += r

    def b_spec(op, k_i):
        if b_slots:
            return pl.BlockSpec((None, k_i, tn), lambda j, i: (j, 0, 0))
        if not isinstance(op, tuple):
            return pl.BlockSpec((tn, k_i), lambda j, i: (j, 0)) if trans_b else pl.BlockSpec((k_i, tn), lambda j, i: (0, j))
        assert trans_b
        if len(op) == 2:
            return pl.BlockSpec((None, tn, k_i), lambda j, i, slot=op[1]: (slot, j, 0))
        return pl.BlockSpec((tn, k_i), lambda j, i, blk=op[1]: (j, blk))

    grid = (n // tn, m // tm)
    if rider is not None:
        rider.n_core_out, rider.n_core_scratch = n_out + n_acc, 0
    body, x_in, x_spec, x_out, x_scr = _ride(body, grid, rider)
    a_specs = [pl.BlockSpec((tm, k_i), lambda j, i, blk=blk: (i, blk)) for _, blk, k_i in a_ops]
    b_specs = [b_spec(op, k_i) for op, (_, _, k_i) in zip(b_ops, a_ops)]
    t_spec = pl.BlockSpec((tm, tn), lambda j, i: (i, j))
    e_specs = [pl.BlockSpec((tm, tn), lambda j, i, off=off: (i, j + off)) for _, off in extras]
    c_specs = [pl.BlockSpec(c.shape, lambda j, i: (0, 0)) for c in consts]
    acc_specs = [pl.BlockSpec(sh, lambda j, i: (0, 0)) for sh in accs]
    sem = ("parallel", "parallel") if rider is None and not n_acc else ("arbitrary", "arbitrary")
    outs = pl.pallas_call(
        body, out_shape=[jax.ShapeDtypeStruct((m, n), d) for d in out_dtypes] + [jax.ShapeDtypeStruct(sh, F32) for sh in accs] + x_out,
        grid=grid, in_specs=a_specs + b_specs + e_specs + c_specs + x_spec, out_specs=[t_spec] * n_out + acc_specs + x_spec,
        scratch_shapes=x_scr, compiler_params=_params(sem),
        name=name)(*[o[0] for o in a_ops], *[o[0] if isinstance(o, tuple) else o for o in b_ops], *[e for e, _ in extras], *consts, *x_in)
    return outs[0] if len(outs) == 1 else outs


def _per_head(fn, heads, n_tiled, n_out):
    def run(*args):
        res = [fn(*[a[:, hd * HEAD_W:(hd + 1) * HEAD_W] for a in args[:n_tiled]], *args[n_tiled:]) for hd in range(heads)]
        tiles = [jnp.concatenate([r[k] for r in res], axis=1) for k in range(n_out)]
        sums = [functools.reduce(lambda u, v: u + v, [r[k] for r in res]) for k in range(n_out, len(res[0]))]
        return (*tiles, *sums)
    return run


def _mm_tn(name, a, b, *, out_dtype=BF16, out_slots=False, tk=1024, tn=1024, tt=4096):
    t, k = a.shape
    n = b.shape[1]
    tk, tn = min(tk, k), min(tn, n)
    if a.dtype == F32 or b.dtype == F32:
        tt = tt // 2
    if k == tk and n == tn:
        tt = tt // 2
    tt = min(tt, t)
    assert b.shape[0] == t and k % tk == 0 and n % tn == 0 and t % tt == 0
    nt = t // tt

    def body(a_ref, b_ref, o_ref, acc):
        part = lax.dot_general(a_ref[...].astype(BF16), b_ref[...].astype(BF16), (((0,), (0,)), ((), ())),
                               preferred_element_type=F32)

        @pl.when(pl.program_id(2) == 0)
        def _():
            acc[...] = part

        @pl.when(pl.program_id(2) != 0)
        def _():
            acc[...] += part

        @pl.when(pl.program_id(2) == nt - 1)
        def _():
            o_ref[...] = acc[...].astype(o_ref.dtype)

    if out_slots:
        out_shape, out_spec = (n // tn, k, tn), pl.BlockSpec((None, tk, tn), lambda ki, ni, ti: (ni, ki, 0))
    else:
        out_shape, out_spec = (k, n), pl.BlockSpec((tk, tn), lambda ki, ni, ti: (ki, ni))
    return pl.pallas_call(
        body, out_shape=jax.ShapeDtypeStruct(out_shape, out_dtype), grid=(k // tk, n // tn, nt),
        in_specs=[pl.BlockSpec((tt, tk), lambda ki, ni, ti: (ti, ki)), pl.BlockSpec((tt, tn), lambda ki, ni, ti: (ti, ni))],
        out_specs=out_spec, scratch_shapes=[pltpu.VMEM((tk, tn), F32)],
        compiler_params=_params(("parallel", "parallel", "arbitrary")), name=name)(a, b)


def _rowwise(name, fn, ins, outs, *, consts=(), pos=(), accs=(), heads=1, tm=512, seq=None, rider=None):
    t = ins[0][0].shape[0]
    tm = min(tm, t if seq is None else seq)
    assert t % tm == 0 and (seq is None or seq % tm == 0)
    n_in, n_pos, n_c, n_out, n_acc = len(ins), len(pos), len(consts), len(outs), len(accs)

    def body(*refs):
        vals = [r[...] for r in refs[:n_in + n_pos + n_c]]
        res = fn(*vals)
        o_refs = refs[n_in + n_pos + n_c:]
        for o_ref, r in zip(o_refs[:n_out], res[:n_out]):
            o_ref[...] = r.astype(o_ref.dtype)
        if n_acc:
            first = jnp.logical_and(pl.program_id(0) == 0, pl.program_id(1) == 0)

            @pl.when(first)
            def _():
                for o_ref, r in zip(o_refs[n_out:], res[n_out:]):
                    o_ref[...] = r

            @pl.when(jnp.logical_not(first))
            def _():
                for o_ref, r in zip(o_refs[n_out:], res[n_out:]):
                    o_ref[...] += r

    def tiled(width, c0, per_head):
        return pl.BlockSpec((tm, width), (lambda h, i: (i, c0 + h)) if per_head else (lambda h, i: (i, c0)))

    in_specs = [tiled(w, c0, ph) for _, w, c0, ph in ins]
    if n_pos:
        nblk = seq // tm
        in_specs += [pl.BlockSpec((tm, a.shape[1]), lambda h, i: (i % nblk, 0)) for a in pos]
    in_specs += [pl.BlockSpec(a.shape, lambda h, i: (0, 0)) for a in consts]
    out_specs = [tiled(w, 0, ph) for _, _, w, ph in outs] + [pl.BlockSpec(s, lambda h, i: (0, 0)) for s in accs]
    out_shape = [jax.ShapeDtypeStruct((t, c), d) for c, d, _, _ in outs] + [jax.ShapeDtypeStruct(s, F32) for s in accs]
    sem = ("arbitrary", "arbitrary") if n_acc or rider is not None else ("parallel", "parallel")
    grid = (heads, t // tm)
    if rider is not None:
        rider.n_core_out, rider.n_core_scratch = n_out + n_acc, 0
    body, x_in, x_spec, x_out, x_scr = _ride(body, grid, rider)
    res = pl.pallas_call(body, out_shape=out_shape + x_out, grid=grid, in_specs=in_specs + x_spec, out_specs=out_specs + x_spec,
                         scratch_shapes=x_scr, compiler_params=_params(sem), name=name)(*[a for a, _, _, _ in ins], *pos, *consts, *x_in)
    return res[0] if len(res) == 1 else res


ATTN_HEADS_PER_STEP = 4


def _attn_fwd(name, q, k, kc0, v, vc0, *, heads, group, nseq, seq, tq=512, rider=None):
    tq = min(tq, seq)
    nq = seq // tq
    hp = ATTN_HEADS_PER_STEP
    grid = (heads // hp, nseq, nq)
    shared = group > 1
    assert group % hp == 0 if shared else (kc0 % hp == 0 and vc0 % hp == 0)

    def body(q_ref, k_ref, v_ref, o_ref, lse_ref):
        for j in range(hp):
            cols = slice(j * HEAD_W, (j + 1) * HEAD_W)
            kj = (k_ref[...] if shared else k_ref[:, cols]).astype(BF16)
            vj = (v_ref[...] if shared else v_ref[:, cols]).astype(BF16)
            s = lax.dot_general(q_ref[:, cols], kj, (((1,), (1,)), ((), ())), preferred_element_type=F32)
            m = jnp.max(s, axis=-1, keepdims=True)
            p = jnp.exp(s - m)
            l = jnp.sum(p, axis=-1, keepdims=True)
            o = jnp.dot(p.astype(BF16), vj, preferred_element_type=F32)
            o_ref[:, cols] = (o * (1.0 / l)).astype(o_ref.dtype)
            lse_ref[j] = m + jnp.log(l)

    if rider is not None:
        rider.n_core_out, rider.n_core_scratch = 2, 0
    body, x_in, x_spec, x_out, x_scr = _ride(body, grid, rider)
    q_spec = pl.BlockSpec((tq, hp * HEAD_W), lambda h, b, i: (b * nq + i, h))
    if shared:
        k_spec = pl.BlockSpec((seq, HEAD_W), lambda h, b, i: (b, kc0 + (h * hp) // group))
        v_spec = pl.BlockSpec((seq, HEAD_W), lambda h, b, i: (b, vc0 + (h * hp) // group))
    else:
        k_spec = pl.BlockSpec((seq, hp * HEAD_W), lambda h, b, i: (b, kc0 // hp + h))
        v_spec = pl.BlockSpec((seq, hp * HEAD_W), lambda h, b, i: (b, vc0 // hp + h))
    lse_spec = pl.BlockSpec((hp, tq, 1), lambda h, b, i: (h, b * nq + i, 0))
    sem = ("parallel",) * 3 if rider is None else ("arbitrary",) * 3
    return pl.pallas_call(
        body, out_shape=[jax.ShapeDtypeStruct(q.shape, BF16), jax.ShapeDtypeStruct((heads, q.shape[0], 1), F32)] + x_out,
        grid=grid, in_specs=[q_spec, k_spec, v_spec] + x_spec, out_specs=[q_spec, lse_spec] + x_spec, scratch_shapes=x_scr,
        compiler_params=_params(sem), name=name)(q, k, v, *x_in)


def _attn_bwd(name, q, k, kc0, v, vc0, o, do, lse, *, heads, group, nseq, seq, tq=1024, rider=None):
    tq = min(tq, seq)
    nq = seq // tq
    hk = heads // group
    t = q.shape[0]
    grid = (hk, nseq, group, nq)

    def body(q_ref, k_ref, v_ref, o_ref, do_ref, lse_ref, dq_ref, dk_ref, dv_ref, dk_acc, dv_acc):
        g, i = pl.program_id(2), pl.program_id(3)
        qv, kv, vv, dov = q_ref[...], k_ref[...].astype(BF16), v_ref[...].astype(BF16), do_ref[...]
        s = lax.dot_general(qv, kv, (((1,), (1,)), ((), ())), preferred_element_type=F32)
        pn = jnp.exp(s - lse_ref[...])
        dp = lax.dot_general(dov, vv, (((1,), (1,)), ((), ())), preferred_element_type=F32)
        delta = jnp.sum(dov.astype(F32) * o_ref[...].astype(F32), axis=-1, keepdims=True)
        ds = (pn * (dp - delta)).astype(BF16)
        dq_ref[...] = jnp.dot(ds, kv, preferred_element_type=F32)
        dk_part = lax.dot_general(ds, qv, (((0,), (0,)), ((), ())), preferred_element_type=F32)
        dv_part = lax.dot_general(pn.astype(BF16), dov, (((0,), (0,)), ((), ())), preferred_element_type=F32)
        first = jnp.logical_and(g == 0, i == 0)

        @pl.when(first)
        def _():
            dk_acc[...] = dk_part
            dv_acc[...] = dv_part

        @pl.when(jnp.logical_not(first))
        def _():
            dk_acc[...] += dk_part
            dv_acc[...] += dv_part

        @pl.when(jnp.logical_and(g == group - 1, i == nq - 1))
        def _():
            dk_ref[...] = dk_acc[...].astype(dk_ref.dtype)
            dv_ref[...] = dv_acc[...].astype(dv_ref.dtype)

    if rider is not None:
        rider.n_core_out, rider.n_core_scratch = 3, 2
    body, x_in, x_spec, x_out, x_scr = _ride(body, grid, rider)
    q_spec = pl.BlockSpec((tq, HEAD_W), lambda kh, b, g, i: (b * nq + i, kh * group + g))
    kv_out = pl.BlockSpec((seq, HEAD_W), lambda kh, b, g, i: (b, kh))
    lse_spec = pl.BlockSpec((None, tq, 1), lambda kh, b, g, i: (kh * group + g, b * nq + i, 0))
    sem = ("parallel", "parallel", "arbitrary", "arbitrary") if rider is None else ("arbitrary",) * 4
    return pl.pallas_call(
        body,
        out_shape=[jax.ShapeDtypeStruct(q.shape, F32), jax.ShapeDtypeStruct((t, hk * HEAD_W), BF16),
                   jax.ShapeDtypeStruct((t, hk * HEAD_W), BF16)] + x_out,
        grid=grid,
        in_specs=[q_spec, pl.BlockSpec((seq, HEAD_W), lambda kh, b, g, i: (b, kc0 + kh)),
                  pl.BlockSpec((seq, HEAD_W), lambda kh, b, g, i: (b, vc0 + kh)), q_spec, q_spec, lse_spec] + x_spec,
        out_specs=[q_spec, kv_out, kv_out] + x_spec,
        scratch_shapes=[pltpu.VMEM((seq, HEAD_W), F32), pltpu.VMEM((seq, HEAD_W), F32)] + x_scr,
        compiler_params=_params(sem), name=name)(q, k, v, o, do, lse, *x_in)


def _place():
    return lax.axis_index("x"), lax.axis_index("y"), lax.axis_index("c")


def _other_chips(x, y):
    return [(1 - x, y), (x, 1 - y), (1 - x, 1 - y)]


class _Exchange:
    def __init__(self, kind, srcs):
        assert kind in ("gather", "scatter")
        self.kind, self.srcs = kind, list(srcs)
        n = len(self.srcs)
        self.out_shapes = [jax.ShapeDtypeStruct((4, *a.shape[-2:]), a.dtype) for a in self.srcs]
        self.scratch = [pltpu.SemaphoreType.DMA((3 * n,)), pltpu.SemaphoreType.DMA((3 * n,)), pltpu.SemaphoreType.DMA((n,))]
        self.n_core_out = self.n_core_scratch = 0

    def _copies(self, j, src_ref, out_ref, send_sems, recv_sems, landing):
        x, y, c = _place()

        def remote(k, s, d, to):
            return pltpu.make_async_remote_copy(src_ref=s, dst_ref=d, send_sem=send_sems.at[3 * j + k], recv_sem=recv_sems.at[3 * j + k],
                                                device_id=to, device_id_type=MESH)

        me = 2 * x + y
        part = (lambda i: src_ref) if self.kind == "gather" else (lambda i: src_ref.at[i])
        if landing:
            return [remote(k, part(me), out_ref.at[2 * px + py], (px, py, c)) for k, (px, py) in enumerate(_other_chips(x, y))]
        return [remote(k, part(2 * px + py), out_ref.at[me], (px, py, c)) for k, (px, py) in enumerate(_other_chips(x, y))]

    def _local(self, j, src_ref, out_ref, local_sems):
        x, y, _ = _place()
        me = 2 * x + y
        return pltpu.make_async_copy(src_ref if self.kind == "gather" else src_ref.at[me], out_ref.at[me], local_sems.at[j])

    def start(self, src_refs, out_refs, send_sems, recv_sems, local_sems):
        for j, (src_ref, out_ref) in enumerate(zip(src_refs, out_refs)):
            self._local(j, src_ref, out_ref, local_sems).start()
            for mine in self._copies(j, src_ref, out_ref, send_sems, recv_sems, False):
                mine.start()

    def finish(self, src_refs, out_refs, send_sems, recv_sems, local_sems):
        for j, (src_ref, out_ref) in enumerate(zip(src_refs, out_refs)):
            for landed in self._copies(j, src_ref, out_ref, send_sems, recv_sems, True):
                landed.wait_recv()
        for j, (src_ref, out_ref) in enumerate(zip(src_refs, out_refs)):
            for mine in self._copies(j, src_ref, out_ref, send_sems, recv_sems, False):
                mine.wait_send()
            self._local(j, src_ref, out_ref, local_sems).wait()


def _gather_by_halves(name, srcs):
    n = len(srcs)

    def body(*refs):
        x, y, c = _place()
        me = 2 * x + y
        local_sems = refs[-1]
        copies = []
        for j in range(n):
            src_ref, out_ref, send_sems, recv_sems = refs[j], refs[n + j], refs[2 * n + 2 * j], refs[2 * n + 2 * j + 1]
            half = srcs[j].shape[0] // 2
            rows_c = pl.ds(pl.multiple_of(c * half, half), half)
            rows_s = pl.ds(pl.multiple_of((1 - c) * half, half), half)

            def remote(k, s_ref, d_ref, to, send_sems=send_sems, recv_sems=recv_sems):
                return pltpu.make_async_remote_copy(src_ref=s_ref, dst_ref=d_ref, send_sem=send_sems.at[k], recv_sem=recv_sems.at[k],
                                                    device_id=to, device_id_type=MESH)

            local = pltpu.make_async_copy(src_ref, out_ref.at[me], local_sems.at[j])
            local.start()
            chips = _other_chips(x, y)
            sent = [remote(k, src_ref.at[rows_c], out_ref.at[me, rows_c], (px, py, c)) for k, (px, py) in enumerate(chips)]
            landing = [remote(k, src_ref.at[rows_c], out_ref.at[2 * px + py, rows_c], (px, py, c)) for k, (px, py) in enumerate(chips)]
            passed = [remote(3 + k, out_ref.at[2 * px + py, rows_c], out_ref.at[2 * px + py, rows_c], (x, y, 1 - c))
                      for k, (px, py) in enumerate(chips)]
            from_sibling = [remote(3 + k, out_ref.at[2 * px + py, rows_s], out_ref.at[2 * px + py, rows_s], (x, y, 1 - c))
                            for k, (px, py) in enumerate(chips)]
            for cp in sent:
                cp.start()
            copies.append((local, sent, landing, passed, from_sibling))
        for local, sent, landing, passed, from_sibling in copies:
            for k in range(3):
                landing[k].wait_recv()
                passed[k].start()
        for local, sent, landing, passed, from_sibling in copies:
            for k in range(3):
                from_sibling[k].wait_recv()
            for cp in sent + passed:
                cp.wait_send()
            local.wait()

    sems = [pltpu.SemaphoreType.DMA((6,)) for _ in range(2 * n)] + [pltpu.SemaphoreType.DMA((n,))]
    return pl.pallas_call(
        body, out_shape=[jax.ShapeDtypeStruct((4, *a.shape), a.dtype) for a in srcs],
        in_specs=[pl.BlockSpec(memory_space=pl.ANY)] * n, out_specs=[pl.BlockSpec(memory_space=pltpu.VMEM)] * n,
        scratch_shapes=sems, compiler_params=pltpu.CompilerParams(vmem_limit_bytes=VMEM_LIMIT_BYTES), name=name)(*srcs)


def _adamw(w, g, m, v):
    m = ADAM_B1 * m + (1.0 - ADAM_B1) * g
    v = ADAM_B2 * v + (1.0 - ADAM_B2) * (g * g)
    delta = -ADAM_LR * ((m / M_HAT_DIV) / (jnp.sqrt(v / V_HAT_DIV) + ADAM_EPS) + ADAM_WD * w)
    return delta, m, v


def _small_allreduce_adamw(part, w, m, v):
    def body(part_ref, w_ref, m_ref, v_ref, g_out, d_out, m_out, v_out, loss_out, buf, send_sems, recv_sems):
        x, y, c = _place()
        me = 4 * x + 2 * y + c
        buf[me] = part_ref[...]

        def flip(k):
            fx, fy, fc = (k >> 2) & 1, (k >> 1) & 1, k & 1
            px, py, pc = (1 - x if fx else x), (1 - y if fy else y), (1 - c if fc else c)
            return (px, py, pc), 4 * px + 2 * py + pc

        def copy(k, slot):
            return pltpu.make_async_remote_copy(
                src_ref=part_ref, dst_ref=buf.at[slot], send_sem=send_sems.at[k - 1], recv_sem=recv_sems.at[k - 1],
                device_id=flip(k)[0], device_id_type=MESH)

        sent = [copy(k, me) for k in range(1, 8)]
        for cp in sent:
            cp.start()
        for k in range(1, 8):
            copy(k, flip(k)[1]).wait_recv()
        for cp in sent:
            cp.wait_send()
        tot = buf[0]
        for j in range(1, 8):
            tot = tot + buf[j]
        delta, m_new, v_new = _adamw(w_ref[...], tot, m_ref[...], v_ref[...])
        g_out[...] = tot
        d_out[...] = delta
        m_out[...] = m_new
        v_out[...] = v_new
        loss_out[...] = jnp.sum(tot[LOSS_ROW0:LOSS_ROW0 + 8, :]).reshape(1, 1)

    vm = pl.BlockSpec(memory_space=pltpu.VMEM)
    shp = jax.ShapeDtypeStruct((SMALL_ROWS, 128), F32)
    return pl.pallas_call(
        body, out_shape=[shp, shp, shp, shp, jax.ShapeDtypeStruct((1, 1), F32)],
        in_specs=[vm, vm, vm, vm], out_specs=[vm, vm, vm, vm, vm],
        scratch_shapes=[pltpu.VMEM((8, SMALL_ROWS, 128), F32), pltpu.SemaphoreType.DMA((7,)), pltpu.SemaphoreType.DMA((7,))],
        name="small_allreduce_adamw")(part, w, m, v)


def _row_tile(rows, cap):
    return max(t for t in range(16, min(rows, cap) + 1, 16) if rows % t == 0)


def _reduce_pair(name, parts):
    _, rows, w = parts.shape
    tr = _row_tile(rows, 576)
    nt = rows // tr

    def body(p_ref, o_ref, mine, theirs, send_sems, recv_sems):
        i = pl.program_id(0)
        x, y, c = _place()

        def copy(t):
            rows_t = pl.ds(pl.multiple_of(t * tr, tr), tr)
            return pltpu.make_async_remote_copy(src_ref=mine.at[rows_t], dst_ref=theirs.at[rows_t], send_sem=send_sems.at[t],
                                                recv_sem=recv_sems.at[t], device_id=(x, y, 1 - c), device_id_type=MESH)

        @pl.when(i < nt)
        def _():
            mine[pl.ds(pl.multiple_of(i * tr, tr), tr), :] = (
                (p_ref[0].astype(F32) + p_ref[1].astype(F32)) + p_ref[2].astype(F32)) + p_ref[3].astype(F32)
            copy(i).start()

        @pl.when(i >= nt)
        def _():
            t = i - nt
            copy(t).wait()
            rows_t = pl.ds(pl.multiple_of(t * tr, tr), tr)
            o_ref[...] = mine[rows_t, :] + theirs[rows_t, :]

    return pl.pallas_call(
        body, out_shape=jax.ShapeDtypeStruct((rows, w), F32), grid=(2 * nt,),
        in_specs=[pl.BlockSpec((4, tr, w), lambda i: (0, jnp.minimum(i, nt - 1), 0))],
        out_specs=pl.BlockSpec((tr, w), lambda i: (jnp.maximum(i - nt, 0), 0)),
        scratch_shapes=[pltpu.VMEM((rows, w), F32), pltpu.VMEM((rows, w), F32), pltpu.SemaphoreType.DMA((nt,)),
                        pltpu.SemaphoreType.DMA((nt,))],
        compiler_params=_params(("arbitrary",)), name=name)(parts)


def _presum_halves(name, shards):
    _, rows, w = shards.shape
    half = rows // 2

    def body(s_ref, o_ref, theirs, send_sems, recv_sems):
        x, y, c = _place()
        rows_c = pl.ds(pl.multiple_of(c * half, half), half)
        rows_s = pl.ds(pl.multiple_of((1 - c) * half, half), half)
        sent = [pltpu.make_async_remote_copy(src_ref=s_ref.at[j, rows_s], dst_ref=theirs.at[j], send_sem=send_sems.at[j],
                                             recv_sem=recv_sems.at[j], device_id=(x, y, 1 - c), device_id_type=MESH) for j in range(4)]
        for cp in sent:
            cp.start()
        for j, cp in enumerate(sent):
            cp.wait_recv()
            o_ref[j] = (s_ref[j, rows_c, :].astype(F32) + theirs[j].astype(F32)).astype(o_ref.dtype)
        for cp in sent:
            cp.wait_send()

    vm = pl.BlockSpec(memory_space=pltpu.VMEM)
    return pl.pallas_call(
        body, out_shape=jax.ShapeDtypeStruct((4, half, w), shards.dtype), in_specs=[vm], out_specs=vm,
        scratch_shapes=[pltpu.VMEM((4, half, w), shards.dtype), pltpu.SemaphoreType.DMA((4,)), pltpu.SemaphoreType.DMA((4,))],
        compiler_params=pltpu.CompilerParams(vmem_limit_bytes=VMEM_LIMIT_BYTES), name=name)(shards)


def _reduce_halves(name, parts):
    _, half, w = parts.shape

    def body(p_ref, o_ref, mine, send_sem, recv_sem):
        x, y, c = _place()
        rows_c = pl.ds(pl.multiple_of(c * half, half), half)
        rows_s = pl.ds(pl.multiple_of((1 - c) * half, half), half)
        mine[...] = ((p_ref[0].astype(F32) + p_ref[1].astype(F32)) + p_ref[2].astype(F32)) + p_ref[3].astype(F32)
        send = pltpu.make_async_remote_copy(src_ref=mine, dst_ref=o_ref.at[rows_c], send_sem=send_sem, recv_sem=recv_sem,
                                            device_id=(x, y, 1 - c), device_id_type=MESH)
        send.start()
        o_ref[rows_c, :] = mine[...]
        pltpu.make_async_remote_copy(src_ref=mine, dst_ref=o_ref.at[rows_s], send_sem=send_sem, recv_sem=recv_sem,
                                     device_id=(x, y, 1 - c), device_id_type=MESH).wait_recv()
        send.wait_send()

    vm = pl.BlockSpec(memory_space=pltpu.VMEM)
    return pl.pallas_call(
        body, out_shape=jax.ShapeDtypeStruct((2 * half, w), F32), in_specs=[vm], out_specs=vm,
        scratch_shapes=[pltpu.VMEM((half, w), F32), pltpu.SemaphoreType.DMA(()), pltpu.SemaphoreType.DMA(())],
        compiler_params=pltpu.CompilerParams(vmem_limit_bytes=VMEM_LIMIT_BYTES), name=name)(parts)


def _adamw_shard(name, g, w, m, v):
    _, rows, cols = w.shape
    tr = _row_tile(rows, 256)

    def body(g_ref, w_ref, m_ref, v_ref, g_out, d_out, m_out, v_out):
        gv = g_ref[...]
        delta, m_new, v_new = _adamw(w_ref[...], gv, m_ref[...], v_ref[...])
        g_out[...] = gv
        d_out[...] = delta
        m_out[...] = m_new
        v_out[...] = v_new

    t_spec = pl.BlockSpec((None, tr, cols), lambda i: (0, i, 0))
    shp = jax.ShapeDtypeStruct((1, rows, cols), F32)
    return pl.pallas_call(body, out_shape=[shp] * 4, grid=(rows // tr,), in_specs=[pl.BlockSpec((tr, cols), lambda i: (i, 0))] + [t_spec] * 3,
                          out_specs=[t_spec] * 4, compiler_params=_params(("parallel",)), name=name)(g, w, m, v)


def _shard_shape(name):
    _, r, c, ax = BIG_BY_NAME[name]
    return (r, c // 4) if ax == 1 else (r // 4, c)


def _pad_rows(a, axis):
    pad = [(0, 0)] * a.ndim
    pad[axis] = (0, -a.shape[axis] % PACK_ALIGN)
    return jnp.pad(a, pad)


def _pack_shards(names, shards, dtype):
    return _pad_rows(jnp.concatenate([s.astype(dtype).reshape(-1, PACK_W) for s in shards], axis=0), 0)


def _unpack_shards(names, slab):
    out, off = [], 0
    for name in names:
        rs, cs = _shard_shape(name)
        n = rs * cs // PACK_W
        out.append(slab[off:off + n].reshape(rs, cs))
        off += n
    return out


def _unpack_full(names, slabs):
    out, off = [], 0
    for name in names:
        _, r, c, ax = BIG_BY_NAME[name]
        n = r * c // 4 // PACK_W
        seg = slabs[:, off:off + n]
        out.append(seg.reshape(4, r, c // 4).transpose(1, 0, 2).reshape(r, c) if ax == 1 else seg.reshape(r, c))
        off += n
    return out


def _pack_full(names, mats, dtype):
    segs = []
    for name, a in zip(names, mats):
        _, r, c, ax = BIG_BY_NAME[name]
        a = a.astype(dtype)
        a = a.reshape(r, 4, c // 4).transpose(1, 0, 2) if ax == 1 else a
        segs.append(a.reshape(4, -1, PACK_W))
    return _pad_rows(jnp.concatenate(segs, axis=1), 1)


def _pad_heads_cols(wm, heads, d):
    k = wm.shape[0]
    return jnp.pad(wm.reshape(k, heads, d), ((0, 0), (0, 0), (0, HEAD_W - d))).reshape(k, heads * HEAD_W)


def _unpad_heads_cols(wm, heads, d):
    k = wm.shape[0]
    return wm.reshape(k, heads, HEAD_W)[:, :, :d].reshape(k, heads * d)


def _win_ext(w_in):
    o = np.cumsum([0, Q_LORA, KV_LORA, QK_ROPE, H_B * HD_B, KV_B * HD_B, KV_B * HD_B, D_MODEL, D_MODEL])
    pc = lambda a, n: jnp.pad(a, ((0, 0), (0, n - a.shape[1])))
    return jnp.concatenate([
        _pad_heads_cols(w_in[:, o[3]:o[4]], H_B, HD_B), w_in[:, o[0]:o[1]], w_in[:, o[1]:o[2]], pc(w_in[:, o[2]:o[3]], HEAD_W),
        _pad_heads_cols(w_in[:, o[4]:o[5]], KV_B, HD_B), _pad_heads_cols(w_in[:, o[5]:o[6]], KV_B, HD_B),
        w_in[:, o[6]:o[7]], w_in[:, o[7]:o[8]]], axis=1)


def _win_unext(blocks):
    c = HEAD_W
    qb, mid, ga, gb = blocks
    at = lambda zc: (zc - ZC_QLAT) * c
    return jnp.concatenate([
        mid[:, at(ZC_QLAT):at(ZC_CKV)], mid[:, at(ZC_CKV):at(ZC_KPE)], mid[:, at(ZC_KPE):at(ZC_KPE) + QK_ROPE],
        _unpad_heads_cols(qb, H_B, HD_B), _unpad_heads_cols(mid[:, at(ZC_KB):at(ZC_VB)], KV_B, HD_B),
        _unpad_heads_cols(mid[:, at(ZC_VB):at(ZC_GA)], KV_B, HD_B), ga, gb], axis=1)


def _wkv_ext(w_kvb):
    wk = w_kvb.reshape(KV_LORA, H_A, QK_NOPE + V_DIM_A)
    k_cols = jnp.pad(wk[:, :, :QK_NOPE], ((0, 0), (0, 0), (0, HEAD_W - QK_NOPE))).reshape(KV_LORA, H_A * HEAD_W)
    v_cols = jnp.pad(wk[:, :, QK_NOPE:], ((0, 0), (0, 0), (0, HEAD_W - V_DIM_A))).reshape(KV_LORA, H_A * HEAD_W)
    eye = jnp.pad(jnp.eye(QK_ROPE, dtype=w_kvb.dtype), ((0, 0), (QK_NOPE, HEAD_W - QK_NOPE - QK_ROPE)))
    pe_rows = jnp.concatenate([jnp.tile(eye, (1, H_A)), jnp.zeros((QK_ROPE, H_A * HEAD_W), w_kvb.dtype)], axis=1)
    top = jnp.concatenate([k_cols, v_cols], axis=1)
    return jnp.concatenate([top, pe_rows, jnp.zeros((2 * HEAD_W - KV_LORA - QK_ROPE, 2 * H_A * HEAD_W), w_kvb.dtype)], axis=0)


def _wkv_unext(k_block, v_block):
    k_cols = k_block[:KV_LORA].reshape(KV_LORA, H_A, HEAD_W)[:, :, :QK_NOPE]
    v_cols = v_block[:KV_LORA].reshape(KV_LORA, H_A, HEAD_W)[:, :, :V_DIM_A]
    return jnp.concatenate([k_cols, v_cols], axis=2).reshape(KV_LORA, H_A * (QK_NOPE + V_DIM_A))


def _pad_heads_rows(wm, heads, d):
    n = wm.shape[1]
    return jnp.pad(wm.reshape(heads, d, n), ((0, 0), (0, HEAD_W - d), (0, 0))).reshape(heads * HEAD_W, n)


def _unpad_heads_rows(wm, heads, d):
    n = wm.shape[1]
    return wm.reshape(heads, HEAD_W, n)[:, :d].reshape(heads * d, n)


def _rope_tables(seq):
    def ang(pos, dim):
        inv = np.float32(ROPE_THETA) ** (-np.arange(0, dim, 2, dtype=np.float32) / np.float32(dim))
        return pos.astype(np.float32)[:, None] * inv[None, :]

    def rot(dim):
        r = np.zeros((dim, dim), np.float32)
        half = dim // 2
        r[np.arange(half) + half, np.arange(half)] = -1.0
        r[np.arange(half), np.arange(half) + half] = 1.0
        return r

    def table(blocks):
        cos, sin = np.ones((seq, HEAD_W), np.float32), np.zeros((seq, HEAD_W), np.float32)
        pm = np.zeros((HEAD_W, HEAD_W), np.float32)
        for c0, a in blocks:
            d = 2 * a.shape[1]
            cos[:, c0:c0 + d] = np.concatenate([np.cos(a), np.cos(a)], axis=1)
            sin[:, c0:c0 + d] = np.concatenate([np.sin(a), np.sin(a)], axis=1)
            pm[c0:c0 + d, c0:c0 + d] = rot(d)
        return jnp.asarray(cos), jnp.asarray(sin), jnp.asarray(pm, BF16), jnp.asarray(pm.T, BF16)

    tok = np.arange(seq)
    a1 = ang(tok, QK_ROPE)
    arow, acol = ang(tok // GRID_W, HD_B // 2), ang(tok % GRID_W, HD_B // 2)
    return table([(QK_NOPE, a1)]), table([(0, a1)]), table([(0, arow), (HD_B // 2, acol)])


def _local_step(x, p, tgt, gains, wts, ride=None):
    nb, seq, _ = x.shape
    t = nb * seq
    x0 = x.reshape(t, D_MODEL)
    p2 = p.reshape(t, PLE_DIM)
    tg = tgt.reshape(t, D_MODEL)
    (cq_t, sq_t, pq, pq_t), (ck_t, sk_t, pk, pk_t), (cb_t, sb_t, pb, pb_t) = _rope_tables(seq)
    padg = lambda g: jnp.pad(g, ((0, 0), (0, HEAD_W - g.shape[1])))
    g_qn, g_kn = padg(gains["g_qn"]), padg(gains["g_kn"])

    win = _win_ext(wts["w_in"])
    wqb = _pad_heads_cols(wts["w_qb"], H_A, QK_NOPE + QK_ROPE)
    wkv = _wkv_ext(wts["w_kvb"])

    norm = lambda n: (lambda v, g: (_rms(v, g, n),))
    full = lambda a: (a, a.shape[1], 0, False)
    wts = dict(wts)
    rider_of = lambda kernel_name: None if ride is None else ride["gather"][kernel_name][0]

    def landed(kernel_name, got):
        if ride is not None:
            wts.update(ride["gather"][kernel_name][1](got))

    h = _rowwise("norm_mix", norm(D_MODEL), [full(x0)], [(D_MODEL, BF16, D_MODEL, False)], consts=[gains["g_mix"]])
    res = _mm("in_proj", h, win, tn=2048, rider=rider_of("in_proj"))
    z, got = (res, []) if ride is None else (res[0], res[1:])
    landed("in_proj", got)
    cq = _rowwise("norm_qa", norm(Q_LORA), [(z, Q_LORA, ZC_QLAT // 2, False)], [(Q_LORA, BF16, Q_LORA, False)],
                  consts=[gains["g_qa"]])
    qa = _mm("q_up", cq, wqb)

    def rope_fwd(scale):
        return lambda v, cos, sin, pm: ((v * cos + _perm(v, pm) * sin) * scale,)

    heads_tile = lambda n: (n * HEAD_W, BF16, n * HEAD_W, False)
    q_a = _rowwise("rope_qa", _per_head(rope_fwd(SCALE_A), H_A, 1, 1), [full(qa)], [heads_tile(H_A)],
                   pos=[cq_t, sq_t], consts=[pq], seq=seq)
    ckv = _rowwise("norm_kva", norm(KV_LORA), [(z, HEAD_W, ZC_CKV, False)], [(HEAD_W, BF16, HEAD_W, False)], consts=[gains["g_kva"]])
    kpe = _rowwise("rope_kpe", rope_fwd(1.0), [(z, HEAD_W, ZC_KPE, False)], [(HEAD_W, BF16, HEAD_W, False)],
                   pos=[ck_t, sk_t], consts=[pk], seq=seq)
    kin = jnp.concatenate([ckv, kpe], axis=1)
    kv_a = _mm("kv_up", kin, wkv, out_dtypes=(BF16,))
    o_a, lse_a, *got = _attn_fwd("attn_a_fwd", q_a, kv_a, 0, kv_a, H_A, heads=H_A, group=1, nseq=nb, seq=seq,
                                 rider=rider_of("attn_a_fwd"))
    landed("attn_a_fwd", got)

    def prep_fwd(scale):
        def fn(v, cos, sin, g, pm):
            yv = _rms(v, g, HD_B)
            return ((yv * cos + _perm(yv, pm) * sin) * scale,)
        return fn

    z_qb, z_kb = (z, H_B * HEAD_W, ZC_QB // H_B, False), (z, KV_B * HEAD_W, ZC_KB // KV_B, False)
    res = _rowwise("prep_qb", _per_head(prep_fwd(SCALE_B), H_B, 1, 1), [z_qb], [heads_tile(H_B)],
                   pos=[cb_t, sb_t], consts=[g_qn, pb], seq=seq, rider=rider_of("prep_qb"))
    q_b, got = (res, []) if ride is None else (res[0], res[1:])
    landed("prep_qb", got)
    k_b = _rowwise("prep_kb", _per_head(prep_fwd(1.0), KV_B, 1, 1), [z_kb], [heads_tile(KV_B)],
                   pos=[cb_t, sb_t], consts=[g_kn, pb], seq=seq)
    o_b, lse_b, *got = _attn_fwd("attn_b_fwd", q_b, k_b, 0, z, ZC_VB, heads=H_B, group=H_B // KV_B, nseq=nb, seq=seq,
                                 rider=rider_of("attn_b_fwd"))
    landed("attn_b_fwd", got)
    woa = _pad_heads_rows(wts["w_oa"], H_A, V_DIM_A)
    wob = _pad_heads_rows(wts["w_ob"], H_B, HD_B)
    wo, wup, wdown, wple = wts["w_o"], wts["w_up"], wts["w_down"], wts["w_ple"]

    def residual_norm(acc, r, g):
        xv = r + acc
        return xv, _rms(xv, g, D_MODEL)

    def mix_out(oa, ob, ga, gb, r, g, w_a, w_b, w_out):
        a = jnp.dot(oa, w_a[...], preferred_element_type=F32)
        b = jnp.dot(ob, w_b[...], preferred_element_type=F32)
        mg = (_sigmoid(ga) * a + _sigmoid(gb) * b).astype(BF16)
        return (a, b, mg, *residual_norm(jnp.dot(mg, w_out[...], preferred_element_type=F32), r, g))

    z_ga, z_gb = (z, D_MODEL, ZC_GA // 8, False), (z, D_MODEL, ZC_GB // 8, False)
    wide = lambda d: (D_MODEL, d, D_MODEL, False)
    ya, yb, merged, x1, h2, *got = _rowwise("mix_out", mix_out, [full(o_a), full(o_b), z_ga, z_gb, full(x0)],
                                            [wide(F32), wide(F32), wide(BF16), wide(F32), wide(BF16)],
                                            consts=[gains["g_mlp"], woa, wob, wo], tm=256, rider=rider_of("mix_out"))
    landed("mix_out", got)
    wpg = wts["w_ple_gate"]

    def relu2(acc):
        u = jnp.maximum(acc, 0.0)
        return u, u * u

    u, usq = _mm("mlp_up", h2, wup, b_slots=True, out_dtypes=(BF16, BF16), epi=relu2, tm=1024)
    x2, h3 = _mm("mlp_down", usq, wdown, out_dtypes=(F32, BF16), epi=residual_norm, extras=(x1,), consts=[gains["g_ple"]])

    def tail(x2v, h3v, pv, tv, gf, w_gate, w_emb):
        sg = _sigmoid(jnp.dot(h3v, w_gate[...], preferred_element_type=F32))
        pev = jnp.dot(pv.astype(BF16), w_emb[...], preferred_element_type=F32)
        x3 = x2v + sg * pev
        rs = lax.rsqrt(jnp.sum(x3 * x3, axis=-1, keepdims=True) * (1.0 / D_MODEL) + EPS)
        xh = x3 * rs
        err = xh * gf - tv
        dy = err * (1.0 / D_MODEL)
        dyg = dy * gf
        dx3 = rs * (dyg - xh * (jnp.sum(dyg * xh, axis=-1, keepdims=True) * (1.0 / D_MODEL)))
        return (dx3, dx3 * pev * sg * (1.0 - sg), dx3 * sg,
                jnp.sum(err * err, axis=0, keepdims=True) * (0.5 / D_MODEL), jnp.sum(dy * xh, axis=0, keepdims=True))

    dx3, dgpre, dpe, loss_part, dg_final = _rowwise(
        "tail", tail, [full(x2), full(h3), full(p2), full(tg)], [wide(F32), wide(BF16), wide(BF16)],
        consts=[gains["g_final"].reshape(1, D_MODEL), wpg, wple], accs=[(1, D_MODEL), (1, D_MODEL)], tm=256)

    def norm_bwd(n, with_res):
        if with_res:
            def fn(dh, v, res, g):
                dx, dg = _rms_bwd(dh, v, g, n)
                return dx + res, dg
        else:
            def fn(dh, v, g):
                return _rms_bwd(dh, v, g, n)
        return fn

    dw = {}
    dw["w_ple"] = _mm_tn("dw_ple", p2, dpe)
    dw["w_ple_gate"] = _mm_tn("dw_ple_gate", h3, dgpre)
    norm_res_bwd = norm_bwd(D_MODEL, True)
    dx2, dg_ple = _mm("d_ple_gate", dgpre, wpg, trans_b=True, epi=norm_res_bwd, extras=(x2, dx3), consts=[gains["g_ple"]],
                      accs=[(1, D_MODEL)], tm=256)
    dw["w_down"] = _mm_tn("dw_down", usq, dx2)
    dupre = _mm("d_mlp_down", dx2, wdown, trans_b=True, out_dtypes=(BF16,), epi=lambda acc, uv: (acc * (2.0 * uv.astype(F32)),),
                extras=(u,), tn=2048)
    dw["w_up"] = _mm_tn("dw_up", h2, dupre, out_slots=True)
    n_up = wup.shape[0]
    dx1, dg_mlp = _mm("d_mlp_up", [(dupre, j, wup.shape[2]) for j in range(n_up)], [(wup, j) for j in range(n_up)], trans_b=True,
                      epi=norm_res_bwd, extras=(x1, dx2), consts=[gains["g_mlp"]],
                      accs=[(1, D_MODEL)], tm=256)
    dw["w_o"] = _mm_tn("dw_o", merged, dx1)

    def merge_bwd(dm, ga, gb, a, b, w_a, w_b):
        sa, sb = _sigmoid(ga), _sigmoid(gb)
        da, db = (dm * sa).astype(BF16), (dm * sb).astype(BF16)
        nt = (((1,), (1,)), ((), ()))
        return (da, db, dm * a * sa * (1.0 - sa), dm * b * sb * (1.0 - sb),
                lax.dot_general(da, w_a, nt, preferred_element_type=F32), lax.dot_general(db, w_b, nt, preferred_element_type=F32))

    dya, dyb, dga, dgb, do_a, do_b = _mm("d_out_proj", dx1, wo, trans_b=True, out_dtypes=(BF16,) * 6, epi=merge_bwd,
                                         extras=((z, ZC_GA // 8), (z, ZC_GB // 8), ya, yb), consts=[woa, wob], tm=256)
    dw["w_oa"] = _unpad_heads_rows(_mm_tn("dw_oa", o_a, dya), H_A, V_DIM_A)
    dw["w_ob"] = _unpad_heads_rows(_mm_tn("dw_ob", o_b, dyb), H_B, HD_B)
    res_a = _attn_bwd("attn_a_bwd", q_a, kv_a, 0, kv_a, H_A, o_a, do_a, lse_a, heads=H_A, group=1, nseq=nb, seq=seq,
                      rider=ride and ride["scatter_a"](dw))
    dq_a, dk_a, dv_a = res_a[:3]
    if ride is not None:
        ride["out"]["parts_a"] = res_a[3:]

    def rope_bwd(scale):
        return lambda d, cos, sin, pm_t: ((d * cos + _perm(d * sin, pm_t)) * scale,)

    dqa = _rowwise("rope_qa_bwd", _per_head(rope_bwd(SCALE_A), H_A, 1, 1), [full(dq_a)], [heads_tile(H_A)],
                   pos=[cq_t, sq_t], consts=[pq_t], seq=seq)
    dw["w_qb"] = _unpad_heads_cols(_mm_tn("dw_qb", cq, dqa), H_A, QK_NOPE + QK_ROPE)
    dcq = _mm("d_q_up", dqa, wqb, trans_b=True)
    dq_lat, dg_qa = _rowwise("norm_qa_bwd", norm_bwd(Q_LORA, False), [full(dcq), (z, Q_LORA, ZC_QLAT // 2, False)],
                             [(Q_LORA, BF16, Q_LORA, False)], consts=[gains["g_qa"]], accs=[(1, Q_LORA)])
    dw["w_kvb"] = _wkv_unext(_mm_tn("dw_kv_k", kin, dk_a), _mm_tn("dw_kv_v", kin, dv_a))
    dq_b, dk_b, dv_b, *parts_b = _attn_bwd("attn_b_bwd", q_b, k_b, 0, z, ZC_VB, o_b, do_b, lse_b, heads=H_B, group=H_B // KV_B,
                                               nseq=nb, seq=seq, rider=ride and ride["scatter_b"](dw))
    if ride is not None:
        ride["out"]["parts_b"] = parts_b
    kv_w = H_A * HEAD_W
    dkin = _mm("d_kv_up", [dk_a, dv_a], [(wkv, 0, kv_w), (wkv, 1, kv_w)], trans_b=True)
    dckv, dg_kva = _rowwise("norm_kva_bwd", norm_bwd(KV_LORA, False), [(dkin, HEAD_W, 0, False), (z, HEAD_W, ZC_CKV, False)],
                            [(HEAD_W, BF16, HEAD_W, False)], consts=[gains["g_kva"]], accs=[(1, KV_LORA)])
    dkpe = _rowwise("rope_kpe_bwd", rope_bwd(1.0), [(dkin, HEAD_W, 1, False)], [(HEAD_W, BF16, HEAD_W, False)],
                    pos=[ck_t, sk_t], consts=[pk_t], seq=seq)

    def prep_bwd(scale):
        def fn(d, v, cos, sin, g, pm_t):
            dyv = (d * cos + _perm(d * sin, pm_t)) * scale
            return _rms_bwd(dyv, v, g, HD_B)
        return fn

    dqb, dg_qn = _rowwise("prep_qb_bwd", _per_head(prep_bwd(SCALE_B), H_B, 2, 1), [full(dq_b), z_qb], [heads_tile(H_B)],
                          pos=[cb_t, sb_t], consts=[g_qn, pb_t], accs=[(1, HEAD_W)], seq=seq)
    dkb, dg_kn = _rowwise("prep_kb_bwd", _per_head(prep_bwd(1.0), KV_B, 2, 1), [full(dk_b), z_kb], [heads_tile(KV_B)],
                          pos=[cb_t, sb_t], consts=[g_kn, pb_t], accs=[(1, HEAD_W)], seq=seq)

    dz = [dqb, jnp.concatenate([dq_lat, dckv, dkpe, dkb, dv_b], axis=1), dga, dgb]
    dw["w_in"] = _win_unext([_mm_tn("dw_in_%d" % j, h, blk) for j, blk in enumerate(dz)])
    dx0, dg_mix, *parts_in = _mm("d_in_proj", dz, [(win, j, D_MODEL) for j in range(4)], trans_b=True, epi=norm_res_bwd, extras=(x0, dx1), consts=[gains["g_mix"]],
                                 accs=[(1, D_MODEL)], tm=256, rider=ride and ride["scatter_in"](dw))
    if ride is not None:
        ride["out"]["parts_in"] = parts_in

    dg = {"g_mix": dg_mix, "g_qa": dg_qa, "g_kva": dg_kva, "g_qn": dg_qn[:, :HD_B], "g_kn": dg_kn[:, :HD_B],
          "g_mlp": dg_mlp, "g_ple": dg_ple, "g_final": dg_final}
    return loss_part, dx0.reshape(nb, seq, D_MODEL), dg, dw


def _pack_small(vals, loss_part=None):
    flat = jnp.concatenate([vals[n].reshape(1, -1) for n, _ in SMALL], axis=1)
    loss = jnp.zeros((1, 8 * 128), F32) if loss_part is None else loss_part
    gap = jnp.zeros((1, LOSS_ROW0 * 128 - SMALL_N), F32)
    return jnp.concatenate([flat, gap, loss], axis=1).reshape(SMALL_ROWS, 128)


def _unpack_small(slab, like):
    flat, out, off = slab.reshape(-1), {}, 0
    for n, k in SMALL:
        out[n] = flat[off:off + k].reshape(like[n].shape)
        off += k
    return out


def kernel(x, p, g_mix, w_in, g_qa, w_qb, g_kva, w_kvb, g_qn, g_kn, w_oa, w_ob, w_o, g_mlp, w_up, w_down, g_ple, w_ple_gate, w_ple, g_final, loss_target, m_g_mix, m_w_in, m_g_qa, m_w_qb, m_g_kva, m_w_kvb, m_g_qn, m_g_kn, m_w_oa, m_w_ob, m_w_o, m_g_mlp, m_w_up, m_w_down, m_g_ple, m_w_ple_gate, m_w_ple, m_g_final, v_g_mix, v_w_in, v_g_qa, v_w_qb, v_g_kva, v_w_kvb, v_g_qn, v_g_kn, v_w_oa, v_w_ob, v_w_o, v_g_mlp, v_w_up, v_w_down, v_g_ple, v_w_ple_gate, v_w_ple, v_g_final):
    given = dict(locals())
    order = ["g_mix", "w_in", "g_qa", "w_qb", "g_kva", "w_kvb", "g_qn", "g_kn", "w_oa", "w_ob", "w_o", "g_mlp", "w_up",
             "w_down", "g_ple", "w_ple_gate", "w_ple", "g_final"]
    big_names = [n for n, _, _, _ in BIG]
    local = lambda prefix, names: [given[prefix + n][0] for n in names]
    slab = lambda names: _pack_shards(names, local("", names), BF16)
    bf = lambda n: given[n][0].astype(BF16)
    cols_full = lambda g: g.transpose(1, 0, 2).reshape(g.shape[1], -1)
    rows_full = lambda g: g.reshape(-1, g.shape[2])
    shards_cols = lambda a: a.reshape(a.shape[0], 4, a.shape[1] // 4).transpose(1, 0, 2)
    shards_rows = lambda a: a.reshape(4, a.shape[0] // 4, a.shape[1])
    packed = lambda names, dw: _pack_full(names, [dw[n] for n in names], BF16)

    got_in, got_early = _gather_by_halves("weight_gather_early", [bf("w_in"), slab(SLAB_EARLY)])
    wts = {"w_in": cols_full(got_in), **dict(zip(SLAB_EARLY, _unpack_full(SLAB_EARLY, got_early)))}
    gains = {n: given[n].reshape(1, -1) for n, _ in SMALL}
    ride = {
        "gather": {
            "in_proj": (_Exchange("gather", [bf("w_o")]), lambda got: {"w_o": rows_full(got[0])}),
            "prep_qb": (_Exchange("gather", [slab(SLAB_LATE)]), lambda got: dict(zip(SLAB_LATE, _unpack_full(SLAB_LATE, got[0])))),
            "attn_a_fwd": (_Exchange("gather", [bf("w_up")]), lambda got: {"w_up": got[0]}),
            "attn_b_fwd": (_Exchange("gather", [bf("w_down")]), lambda got: {"w_down": rows_full(got[0])}),
            "mix_out": (_Exchange("gather", [bf("w_ple_gate")]), lambda got: {"w_ple_gate": rows_full(got[0])}),
        },
        "scatter_a": lambda dw: _Exchange("scatter", [dw["w_up"], shards_rows(dw["w_o"]), packed(SLAB_LATE, dw)]),
        "scatter_b": lambda dw: _Exchange("scatter", [shards_rows(dw["w_down"]), shards_rows(dw["w_ple_gate"]), packed(SLAB_EARLY, dw)]),
        "scatter_in": lambda dw: _Exchange("scatter", [_presum_halves("grad_presum_in", shards_cols(dw["w_in"]))]),
        "out": {},
    }
    loss_part, grad_x, dg, dw = _local_step(x, p[0], loss_target, gains, wts, ride)

    small = lambda prefix: _pack_small({n: given[prefix + n] for n, _ in SMALL})
    g_s, d_s, m_s, v_s, loss = _small_allreduce_adamw(_pack_small(dg, loss_part), small(""), small("m_"), small("v_"))

    parts = ride["out"]
    grads = {"w_up": _reduce_pair("grad_reduce_up", parts["parts_a"][0]), "w_o": _reduce_pair("grad_reduce_o", parts["parts_a"][1]),
             "w_down": _reduce_pair("grad_reduce_down", parts["parts_b"][0]),
             "w_ple_gate": _reduce_pair("grad_reduce_ple_gate", parts["parts_b"][1]),
             "w_in": _reduce_halves("grad_reduce_in", parts["parts_in"][0])}
    grads.update(zip(SLAB_LATE, _unpack_shards(SLAB_LATE, _reduce_pair("grad_reduce_slab_late", parts["parts_a"][2]))))
    grads.update(zip(SLAB_EARLY, _unpack_shards(SLAB_EARLY, _reduce_pair("grad_reduce_slab_early", parts["parts_b"][2]))))

    res = {}
    for key, slab in (("grad_", g_s), ("delta_", d_s), ("new_m_", m_s), ("new_v_", v_s)):
        for n, val in _unpack_small(slab, given).items():
            res[key + n] = val
    for n in big_names:
        res["grad_" + n], res["delta_" + n], res["new_m_" + n], res["new_v_" + n] = _adamw_shard(
            "adamw_" + n, grads[n], given[n], given["m_" + n], given["v_" + n])
    outs = [loss.reshape(()), grad_x]
    for key in ("grad_", "delta_", "new_m_", "new_v_"):
        outs += [res[key + n] for n in order]
    return tuple(outs)
```

```python
import functools

import numpy as np
import jax
import jax.numpy as jnp
from jax import lax
from jax.experimental import pallas as pl
from jax.experimental.pallas import tpu as pltpu

F32 = jnp.float32
BF16 = jnp.bfloat16
MESH = pl.DeviceIdType.MESH

D_MODEL = 1024
GRID_W = 64
ROPE_THETA = 10000.0
EPS = 1e-6
H_A, QK_NOPE, QK_ROPE, V_DIM_A, Q_LORA, KV_LORA = 8, 64, 32, 64, 256, 128
H_B, KV_B, HD_B = 8, 2, 64
D_FF = 4096
PLE_DIM = 256
HEAD_W = 128
SCALE_A = (QK_NOPE + QK_ROPE) ** -0.5
SCALE_B = HD_B ** -0.5

ADAM_LR, ADAM_B1, ADAM_B2, ADAM_EPS, ADAM_WD, ADAM_STEP = 0.001, 0.9, 0.999, 1e-08, 0.01, 10
M_HAT_DIV = 1.0 - ADAM_B1 ** ADAM_STEP
V_HAT_DIV = 1.0 - ADAM_B2 ** ADAM_STEP

VMEM_LIMIT_BYTES = 56 * 1024 * 1024

ZC_QB, ZC_QLAT, ZC_CKV, ZC_KPE, ZC_KB, ZC_VB, ZC_GA, ZC_GB = 0, 8, 10, 11, 12, 14, 16, 24
Z_WIDTH = 32 * HEAD_W

BIG = [
    ("w_in", 1024, 3232, 1), ("w_qb", 256, 768, 1), ("w_kvb", 128, 1024, 1), ("w_oa", 512, 1024, 1),
    ("w_ob", 512, 1024, 1), ("w_o", 1024, 1024, 0), ("w_up", 1024, 4096, 1), ("w_down", 4096, 1024, 0),
    ("w_ple_gate", 1024, 1024, 0), ("w_ple", 256, 1024, 1),
]
BIG_BY_NAME = {e[0]: e for e in BIG}
PACK_W = 1024
PACK_ALIGN = 64
SLAB_EARLY = ["w_qb", "w_kvb"]
SLAB_LATE = ["w_oa", "w_ob", "w_ple"]

SMALL = [("g_mix", 1024), ("g_qa", 256), ("g_kva", 128), ("g_qn", 64), ("g_kn", 64), ("g_mlp", 1024),
         ("g_ple", 1024), ("g_final", 1024)]
SMALL_N = sum(n for _, n in SMALL)
LOSS_ROW0 = 40
SMALL_ROWS = 48


def _params(sem):
    return pltpu.CompilerParams(dimension_semantics=sem, vmem_limit_bytes=VMEM_LIMIT_BYTES)


def _sigmoid(v):
    return 1.0 / (1.0 + jnp.exp(-v))


def _perm(v, p_ref):
    pm = p_ref[...]
    hi = v.astype(BF16)
    lo = (v - hi.astype(F32)).astype(BF16)
    return (jnp.dot(hi, pm, preferred_element_type=F32) + jnp.dot(lo, pm, preferred_element_type=F32))


def _rms(v, g, n):
    rs = lax.rsqrt(jnp.sum(v * v, axis=-1, keepdims=True) * (1.0 / n) + EPS)
    return v * rs * g


def _rms_bwd(dy, v, g, n):
    rs = lax.rsqrt(jnp.sum(v * v, axis=-1, keepdims=True) * (1.0 / n) + EPS)
    vh = v * rs
    dyg = dy * g
    dx = rs * (dyg - vh * (jnp.sum(dyg * vh, axis=-1, keepdims=True) * (1.0 / n)))
    return dx, jnp.sum(dy * vh, axis=0, keepdims=True)


def _ride(body, grid, rider):
    if rider is None:
        return body, [], [], [], []
    n_x, n_sem = len(rider.srcs), len(rider.scratch)

    def wrapped(*refs):
        ids = [pl.program_id(a) for a in range(len(grid))]
        n_in = len(refs) - n_sem - 2 * n_x - rider.n_core_out - rider.n_core_scratch
        core_in, srcs = refs[:n_in], refs[n_in:n_in + n_x]
        core_out = refs[n_in + n_x:n_in + n_x + rider.n_core_out]
        dsts = refs[n_in + n_x + rider.n_core_out:n_in + 2 * n_x + rider.n_core_out]
        core_scr = refs[n_in + 2 * n_x + rider.n_core_out:len(refs) - n_sem]
        sems = refs[len(refs) - n_sem:]

        @pl.when(functools.reduce(jnp.logical_and, [a == 0 for a in ids]))
        def _():
            rider.start(srcs, dsts, *sems)

        body(*core_in, *core_out, *core_scr)

        @pl.when(functools.reduce(jnp.logical_and, [a == n - 1 for a, n in zip(ids, grid)]))
        def _():
            rider.finish(srcs, dsts, *sems)

    hbm = pl.BlockSpec(memory_space=pl.ANY)
    return wrapped, list(rider.srcs), [hbm] * n_x, list(rider.out_shapes), list(rider.scratch)


def _mm(name, a, b, *, trans_b=False, b_slots=False, a_pre=None, out_dtypes=(F32,), epi=None, extras=(), consts=(), accs=(), tm=512,
        tn=None, rider=None):
    a_ops = [o if isinstance(o, tuple) else (o, 0, o.shape[1]) for o in (a if isinstance(a, list) else [a])]
    b_ops = b if isinstance(b, list) else [b]
    assert len(a_ops) == len(b_ops) and not (b_slots and (trans_b or len(b_ops) > 1))
    m = a_ops[0][0].shape[0]
    if b_slots:
        n, tn = b.shape[0] * b.shape[2], b.shape[2]
    else:
        first = b_ops[0][0] if isinstance(b_ops[0], tuple) else b_ops[0]
        n = first.shape[-2] if trans_b else first.shape[1]
        tn = n if tn is None else min(tn, n)
    tm = min(tm, m)
    assert m % tm == 0 and n % tn == 0
    extras = [e if isinstance(e, tuple) else (e, 0) for e in extras]
    n_p, n_ex, n_c, n_out, n_acc = len(a_ops), len(extras), len(consts), len(out_dtypes), len(accs)
    dims = (((1,), (1,)), ((), ())) if trans_b else (((1,), (0,)), ((), ()))

    def body(*refs):
        acc = None
        for a_ref, b_ref in zip(refs[:n_p], refs[n_p:2 * n_p]):
            av = a_ref[...] if a_pre is None else a_pre(a_ref[...].astype(F32))
            part = lax.dot_general(av.astype(BF16), b_ref[...].astype(BF16), dims, preferred_element_type=F32)
            acc = part if acc is None else acc + part
        rest = refs[2 * n_p:]
        res = (acc,) if epi is None else epi(acc, *[e[...] for e in rest[:n_ex + n_c]])
        o_refs = rest[n_ex + n_c:]
        for o_ref, r in zip(o_refs[:n_out], res[:n_out]):
            o_ref[...] = r.astype(o_ref.dtype)
        if n_acc:
            first_step = jnp.logical_and(pl.program_id(0) == 0, pl.program_id(1) == 0)

            @pl.when(first_step)
            def _():
                for o_ref, r in zip(o_refs[n_out:], res[n_out:]):
                    o_ref[...] = r

            @pl.when(jnp.logical_not(first_step))
            def _():
                for o_ref, r in zip(o_refs[n_out:], res[n_out:]):
                    o_ref[...] += r

    def b_spec(op, k_i):
        if b_slots:
            return pl.BlockSpec((None, k_i, tn), lambda j, i: (j, 0, 0))
        if not isinstance(op, tuple):
            return pl.BlockSpec((tn, k_i), lambda j, i: (j, 0)) if trans_b else pl.BlockSpec((k_i, tn), lambda j, i: (0, j))
        assert trans_b
        if len(op) == 2:
            return pl.BlockSpec((None, tn, k_i), lambda j, i, slot=op[1]: (slot, j, 0))
        return pl.BlockSpec((tn, k_i), lambda j, i, blk=op[1]: (j, blk))

    grid = (n // tn, m // tm)
    if rider is not None:
        rider.n_core_out, rider.n_core_scratch = n_out + n_acc, 0
    body, x_in, x_spec, x_out, x_scr = _ride(body, grid, rider)
    a_specs = [pl.BlockSpec((tm, k_i), lambda j, i, blk=blk: (i, blk)) for _, blk, k_i in a_ops]
    b_specs = [b_spec(op, k_i) for op, (_, _, k_i) in zip(b_ops, a_ops)]
    t_spec = pl.BlockSpec((tm, tn), lambda j, i: (i, j))
    e_specs = [pl.BlockSpec((tm, tn), lambda j, i, off=off: (i, j + off)) for _, off in extras]
    c_specs = [pl.BlockSpec(c.shape, lambda j, i: (0, 0)) for c in consts]
    acc_specs = [pl.BlockSpec(sh, lambda j, i: (0, 0)) for sh in accs]
    sem = ("parallel", "parallel") if rider is None and not n_acc else ("arbitrary", "arbitrary")
    outs = pl.pallas_call(
        body, out_shape=[jax.ShapeDtypeStruct((m, n), d) for d in out_dtypes] + [jax.ShapeDtypeStruct(sh, F32) for sh in accs] + x_out,
        grid=grid, in_specs=a_specs + b_specs + e_specs + c_specs + x_spec, out_specs=[t_spec] * n_out + acc_specs + x_spec,
        scratch_shapes=x_scr, compiler_params=_params(sem),
        name=name)(*[o[0] for o in a_ops], *[o[0] if isinstance(o, tuple) else o for o in b_ops], *[e for e, _ in extras], *consts, *x_in)
    return outs[0] if len(outs) == 1 else outs


def _per_head(fn, heads, n_tiled, n_out):
    def run(*args):
        res = [fn(*[a[:, hd * HEAD_W:(hd + 1) * HEAD_W] for a in args[:n_tiled]], *args[n_tiled:]) for hd in range(heads)]
        tiles = [jnp.concatenate([r[k] for r in res], axis=1) for k in range(n_out)]
        sums = [functools.reduce(lambda u, v: u + v, [r[k] for r in res]) for k in range(n_out, len(res[0]))]
        return (*tiles, *sums)
    return run


def _mm_tn(name, a, b, *, a_pre=None, out_dtype=BF16, out_slots=False, tk=1024, tn=1024, tt=4096):
    t, k = a.shape
    n = b.shape[1]
    tk, tn = min(tk, k), min(tn, n)
    if a.dtype == F32 or b.dtype == F32:
        tt = tt // 2
    if k == tk and n == tn:
        tt = tt // 2
    tt = min(tt, t)
    assert b.shape[0] == t and k % tk == 0 and n % tn == 0 and t % tt == 0
    nt = t // tt

    def body(a_ref, b_ref, o_ref, acc):
        av = a_ref[...] if a_pre is None else a_pre(a_ref[...].astype(F32))
        part = lax.dot_general(av.astype(BF16), b_ref[...].astype(BF16), (((0,), (0,)), ((), ())), preferred_element_type=F32)

        @pl.when(pl.program_id(2) == 0)
        def _():
            acc[...] = part

        @pl.when(pl.program_id(2) != 0)
        def _():
            acc[...] += part

        @pl.when(pl.program_id(2) == nt - 1)
        def _():
            o_ref[...] = acc[...].astype(o_ref.dtype)

    if out_slots:
        out_shape, out_spec = (n // tn, k, tn), pl.BlockSpec((None, tk, tn), lambda ki, ni, ti: (ni, ki, 0))
    else:
        out_shape, out_spec = (k, n), pl.BlockSpec((tk, tn), lambda ki, ni, ti: (ki, ni))
    return pl.pallas_call(
        body, out_shape=jax.ShapeDtypeStruct(out_shape, out_dtype), grid=(k // tk, n // tn, nt),
        in_specs=[pl.BlockSpec((tt, tk), lambda ki, ni, ti: (ti, ki)), pl.BlockSpec((tt, tn), lambda ki, ni, ti: (ti, ni))],
        out_specs=out_spec, scratch_shapes=[pltpu.VMEM((tk, tn), F32)],
        compiler_params=_params(("parallel", "parallel", "arbitrary")), name=name)(a, b)


def _rowwise(name, fn, ins, outs, *, consts=(), pos=(), accs=(), heads=1, tm=512, seq=None, rider=None):
    t = ins[0][0].shape[0]
    tm = min(tm, t if seq is None else seq)
    assert t % tm == 0 and (seq is None or seq % tm == 0)
    n_in, n_pos, n_c, n_out, n_acc = len(ins), len(pos), len(consts), len(outs), len(accs)

    def body(*refs):
        vals = [r[...] for r in refs[:n_in + n_pos + n_c]]
        res = fn(*vals)
        o_refs = refs[n_in + n_pos + n_c:]
        for o_ref, r in zip(o_refs[:n_out], res[:n_out]):
            o_ref[...] = r.astype(o_ref.dtype)
        if n_acc:
            first = jnp.logical_and(pl.program_id(0) == 0, pl.program_id(1) == 0)

            @pl.when(first)
            def _():
                for o_ref, r in zip(o_refs[n_out:], res[n_out:]):
                    o_ref[...] = r

            @pl.when(jnp.logical_not(first))
            def _():
                for o_ref, r in zip(o_refs[n_out:], res[n_out:]):
                    o_ref[...] += r

    def tiled(width, c0, per_head):
        return pl.BlockSpec((tm, width), (lambda h, i: (i, c0 + h)) if per_head else (lambda h, i: (i, c0)))

    in_specs = [tiled(w, c0, ph) for _, w, c0, ph in ins]
    if n_pos:
        nblk = seq // tm
        in_specs += [pl.BlockSpec((tm, a.shape[1]), lambda h, i: (i % nblk, 0)) for a in pos]
    in_specs += [pl.BlockSpec(a.shape, lambda h, i: (0, 0)) for a in consts]
    out_specs = [tiled(w, 0, ph) for _, _, w, ph in outs] + [pl.BlockSpec(s, lambda h, i: (0, 0)) for s in accs]
    out_shape = [jax.ShapeDtypeStruct((t, c), d) for c, d, _, _ in outs] + [jax.ShapeDtypeStruct(s, F32) for s in accs]
    sem = ("arbitrary", "arbitrary") if n_acc or rider is not None else ("parallel", "parallel")
    grid = (heads, t // tm)
    if rider is not None:
        rider.n_core_out, rider.n_core_scratch = n_out + n_acc, 0
    body, x_in, x_spec, x_out, x_scr = _ride(body, grid, rider)
    res = pl.pallas_call(body, out_shape=out_shape + x_out, grid=grid, in_specs=in_specs + x_spec, out_specs=out_specs + x_spec,
                         scratch_shapes=x_scr, compiler_params=_params(sem), name=name)(*[a for a, _, _, _ in ins], *pos, *consts, *x_in)
    return res[0] if len(res) == 1 else res


ATTN_HEADS_PER_STEP = 4


def _attn_fwd(name, q, k, kc0, v, vc0, *, heads, group, nseq, seq, tq=512, rider=None):
    tq = min(tq, seq)
    nq = seq // tq
    hp = ATTN_HEADS_PER_STEP
    grid = (heads // hp, nseq, nq)
    shared = group > 1
    assert group % hp == 0 if shared else (kc0 % hp == 0 and vc0 % hp == 0)

    def body(q_ref, k_ref, v_ref, o_ref, lse_ref):
        for j in range(hp):
            cols = slice(j * HEAD_W, (j + 1) * HEAD_W)
            kj = (k_ref[...] if shared else k_ref[:, cols]).astype(BF16)
            vj = (v_ref[...] if shared else v_ref[:, cols]).astype(BF16)
            s = lax.dot_general(q_ref[:, cols], kj, (((1,), (1,)), ((), ())), preferred_element_type=F32)
            m = jnp.max(s, axis=-1, keepdims=True)
            p = jnp.exp(s - m)
            l = jnp.sum(p, axis=-1, keepdims=True)
            o = jnp.dot(p.astype(BF16), vj, preferred_element_type=F32)
            o_ref[:, cols] = (o * (1.0 / l)).astype(o_ref.dtype)
            lse_ref[j] = m + jnp.log(l)

    if rider is not None:
        rider.n_core_out, rider.n_core_scratch = 2, 0
    body, x_in, x_spec, x_out, x_scr = _ride(body, grid, rider)
    q_spec = pl.BlockSpec((tq, hp * HEAD_W), lambda h, b, i: (b * nq + i, h))
    if shared:
        k_spec = pl.BlockSpec((seq, HEAD_W), lambda h, b, i: (b, kc0 + (h * hp) // group))
        v_spec = pl.BlockSpec((seq, HEAD_W), lambda h, b, i: (b, vc0 + (h * hp) // group))
    else:
        k_spec = pl.BlockSpec((seq, hp * HEAD_W), lambda h, b, i: (b, kc0 // hp + h))
        v_spec = pl.BlockSpec((seq, hp * HEAD_W), lambda h, b, i: (b, vc0 // hp + h))
    lse_spec = pl.BlockSpec((hp, tq, 1), lambda h, b, i: (h, b * nq + i, 0))
    sem = ("parallel",) * 3 if rider is None else ("arbitrary",) * 3
    return pl.pallas_call(
        body, out_shape=[jax.ShapeDtypeStruct(q.shape, BF16), jax.ShapeDtypeStruct((heads, q.shape[0], 1), F32)] + x_out,
        grid=grid, in_specs=[q_spec, k_spec, v_spec] + x_spec, out_specs=[q_spec, lse_spec] + x_spec, scratch_shapes=x_scr,
        compiler_params=_params(sem), name=name)(q, k, v, *x_in)


def _attn_bwd(name, q, k, kc0, v, vc0, o, do, lse, *, heads, group, nseq, seq, tq=1024, rider=None):
    tq = min(tq, seq)
    nq = seq // tq
    hk = heads // group
    t = q.shape[0]
    grid = (hk, nseq, group, nq)

    def body(q_ref, k_ref, v_ref, o_ref, do_ref, lse_ref, dq_ref, dk_ref, dv_ref, dk_acc, dv_acc):
        g, i = pl.program_id(2), pl.program_id(3)
        qv, kv, vv, dov = q_ref[...], k_ref[...].astype(BF16), v_ref[...].astype(BF16), do_ref[...]
        s = lax.dot_general(qv, kv, (((1,), (1,)), ((), ())), preferred_element_type=F32)
        pn = jnp.exp(s - lse_ref[...])
        dp = lax.dot_general(dov, vv, (((1,), (1,)), ((), ())), preferred_element_type=F32)
        delta = jnp.sum(dov.astype(F32) * o_ref[...].astype(F32), axis=-1, keepdims=True)
        ds = (pn * (dp - delta)).astype(BF16)
        dq_ref[...] = jnp.dot(ds, kv, preferred_element_type=F32)
        dk_part = lax.dot_general(ds, qv, (((0,), (0,)), ((), ())), preferred_element_type=F32)
        dv_part = lax.dot_general(pn.astype(BF16), dov, (((0,), (0,)), ((), ())), preferred_element_type=F32)
        first = jnp.logical_and(g == 0, i == 0)

        @pl.when(first)
        def _():
            dk_acc[...] = dk_part
            dv_acc[...] = dv_part

        @pl.when(jnp.logical_not(first))
        def _():
            dk_acc[...] += dk_part
            dv_acc[...] += dv_part

        @pl.when(jnp.logical_and(g == group - 1, i == nq - 1))
        def _():
            dk_ref[...] = dk_acc[...].astype(dk_ref.dtype)
            dv_ref[...] = dv_acc[...].astype(dv_ref.dtype)

    if rider is not None:
        rider.n_core_out, rider.n_core_scratch = 3, 2
    body, x_in, x_spec, x_out, x_scr = _ride(body, grid, rider)
    q_spec = pl.BlockSpec((tq, HEAD_W), lambda kh, b, g, i: (b * nq + i, kh * group + g))
    kv_out = pl.BlockSpec((seq, HEAD_W), lambda kh, b, g, i: (b, kh))
    lse_spec = pl.BlockSpec((None, tq, 1), lambda kh, b, g, i: (kh * group + g, b * nq + i, 0))
    sem = ("parallel", "parallel", "arbitrary", "arbitrary") if rider is None else ("arbitrary",) * 4
    return pl.pallas_call(
        body,
        out_shape=[jax.ShapeDtypeStruct(q.shape, F32), jax.ShapeDtypeStruct((t, hk * HEAD_W), BF16),
                   jax.ShapeDtypeStruct((t, hk * HEAD_W), BF16)] + x_out,
        grid=grid,
        in_specs=[q_spec, pl.BlockSpec((seq, HEAD_W), lambda kh, b, g, i: (b, kc0 + kh)),
                  pl.BlockSpec((seq, HEAD_W), lambda kh, b, g, i: (b, vc0 + kh)), q_spec, q_spec, lse_spec] + x_spec,
        out_specs=[q_spec, kv_out, kv_out] + x_spec,
        scratch_shapes=[pltpu.VMEM((seq, HEAD_W), F32), pltpu.VMEM((seq, HEAD_W), F32)] + x_scr,
        compiler_params=_params(sem), name=name)(q, k, v, o, do, lse, *x_in)


def _place():
    return lax.axis_index("x"), lax.axis_index("y"), lax.axis_index("c")


def _other_chips(x, y):
    return [(1 - x, y), (x, 1 - y), (1 - x, 1 - y)]


class _Exchange:
    def __init__(self, kind, srcs):
        assert kind in ("gather", "scatter")
        self.kind, self.srcs = kind, list(srcs)
        n = len(self.srcs)
        self.out_shapes = [jax.ShapeDtypeStruct((4, *a.shape[-2:]), a.dtype) for a in self.srcs]
        self.scratch = [pltpu.SemaphoreType.DMA((3 * n,)), pltpu.SemaphoreType.DMA((3 * n,)), pltpu.SemaphoreType.DMA((n,))]
        self.n_core_out = self.n_core_scratch = 0

    def _copies(self, j, src_ref, out_ref, send_sems, recv_sems, landing):
        x, y, c = _place()

        def remote(k, s, d, to):
            return pltpu.make_async_remote_copy(src_ref=s, dst_ref=d, send_sem=send_sems.at[3 * j + k], recv_sem=recv_sems.at[3 * j + k],
                                                device_id=to, device_id_type=MESH)

        me = 2 * x + y
        part = (lambda i: src_ref) if self.kind == "gather" else (lambda i: src_ref.at[i])
        if landing:
            return [remote(k, part(me), out_ref.at[2 * px + py], (px, py, c)) for k, (px, py) in enumerate(_other_chips(x, y))]
        return [remote(k, part(2 * px + py), out_ref.at[me], (px, py, c)) for k, (px, py) in enumerate(_other_chips(x, y))]

    def _local(self, j, src_ref, out_ref, local_sems):
        x, y, _ = _place()
        me = 2 * x + y
        return pltpu.make_async_copy(src_ref if self.kind == "gather" else src_ref.at[me], out_ref.at[me], local_sems.at[j])

    def start(self, src_refs, out_refs, send_sems, recv_sems, local_sems):
        for j, (src_ref, out_ref) in enumerate(zip(src_refs, out_refs)):
            self._local(j, src_ref, out_ref, local_sems).start()
            for mine in self._copies(j, src_ref, out_ref, send_sems, recv_sems, False):
                mine.start()

    def finish(self, src_refs, out_refs, send_sems, recv_sems, local_sems):
        for j, (src_ref, out_ref) in enumerate(zip(src_refs, out_refs)):
            for landed in self._copies(j, src_ref, out_ref, send_sems, recv_sems, True):
                landed.wait_recv()
        for j, (src_ref, out_ref) in enumerate(zip(src_refs, out_refs)):
            for mine in self._copies(j, src_ref, out_ref, send_sems, recv_sems, False):
                mine.wait_send()
            self._local(j, src_ref, out_ref, local_sems).wait()


def _gather_by_halves(name, srcs):
    n = len(srcs)

    def body(*refs):
        x, y, c = _place()
        me = 2 * x + y
        local_sems = refs[-1]
        copies = []
        for j in range(n):
            src_ref, out_ref, send_sems, recv_sems = refs[j], refs[n + j], refs[2 * n + 2 * j], refs[2 * n + 2 * j + 1]
            half = srcs[j].shape[0] // 2
            rows_c = pl.ds(pl.multiple_of(c * half, half), half)
            rows_s = pl.ds(pl.multiple_of((1 - c) * half, half), half)

            def remote(k, s_ref, d_ref, to, send_sems=send_sems, recv_sems=recv_sems):
                return pltpu.make_async_remote_copy(src_ref=s_ref, dst_ref=d_ref, send_sem=send_sems.at[k], recv_sem=recv_sems.at[k],
                                                    device_id=to, device_id_type=MESH)

            local = pltpu.make_async_copy(src_ref, out_ref.at[me], local_sems.at[j])
            local.start()
            chips = _other_chips(x, y)
            sent = [remote(k, src_ref.at[rows_c], out_ref.at[me, rows_c], (px, py, c)) for k, (px, py) in enumerate(chips)]
            landing = [remote(k, src_ref.at[rows_c], out_ref.at[2 * px + py, rows_c], (px, py, c)) for k, (px, py) in enumerate(chips)]
            passed = [remote(3 + k, out_ref.at[2 * px + py, rows_c], out_ref.at[2 * px + py, rows_c], (x, y, 1 - c))
                      for k, (px, py) in enumerate(chips)]
            from_sibling = [remote(3 + k, out_ref.at[2 * px + py, rows_s], out_ref.at[2 * px + py, rows_s], (x, y, 1 - c))
                            for k, (px, py) in enumerate(chips)]
            for cp in sent:
                cp.start()
            copies.append((local, sent, landing, passed, from_sibling))
        for local, sent, landing, passed, from_sibling in copies:
            for k in range(3):
                landing[k].wait_recv()
                passed[k].start()
        for local, sent, landing, passed, from_sibling in copies:
            for k in range(3):
                from_sibling[k].wait_recv()
            for cp in sent + passed:
                cp.wait_send()
            local.wait()

    sems = [pltpu.SemaphoreType.DMA((6,)) for _ in range(2 * n)] + [pltpu.SemaphoreType.DMA((n,))]
    return pl.pallas_call(
        body, out_shape=[jax.ShapeDtypeStruct((4, *a.shape), a.dtype) for a in srcs],
        in_specs=[pl.BlockSpec(memory_space=pl.ANY)] * n, out_specs=[pl.BlockSpec(memory_space=pltpu.VMEM)] * n,
        scratch_shapes=sems, compiler_params=pltpu.CompilerParams(vmem_limit_bytes=VMEM_LIMIT_BYTES), name=name)(*srcs)


def _adamw(w, g, m, v):
    m = ADAM_B1 * m + (1.0 - ADAM_B1) * g
    v = ADAM_B2 * v + (1.0 - ADAM_B2) * (g * g)
    delta = -ADAM_LR * ((m / M_HAT_DIV) / (jnp.sqrt(v / V_HAT_DIV) + ADAM_EPS) + ADAM_WD * w)
    return delta, m, v


def _small_allreduce_adamw(part, w, m, v):
    def body(part_ref, w_ref, m_ref, v_ref, g_out, d_out, m_out, v_out, loss_out, buf, send_sems, recv_sems):
        x, y, c = _place()
        me = 4 * x + 2 * y + c
        buf[me] = part_ref[...]

        def flip(k):
            fx, fy, fc = (k >> 2) & 1, (k >> 1) & 1, k & 1
            px, py, pc = (1 - x if fx else x), (1 - y if fy else y), (1 - c if fc else c)
            return (px, py, pc), 4 * px + 2 * py + pc

        def copy(k, slot):
            return pltpu.make_async_remote_copy(
                src_ref=part_ref, dst_ref=buf.at[slot], send_sem=send_sems.at[k - 1], recv_sem=recv_sems.at[k - 1],
                device_id=flip(k)[0], device_id_type=MESH)

        sent = [copy(k, me) for k in range(1, 8)]
        for cp in sent:
            cp.start()
        for k in range(1, 8):
            copy(k, flip(k)[1]).wait_recv()
        for cp in sent:
            cp.wait_send()
        tot = buf[0]
        for j in range(1, 8):
            tot = tot + buf[j]
        delta, m_new, v_new = _adamw(w_ref[...], tot, m_ref[...], v_ref[...])
        g_out[...] = tot
        d_out[...] = delta
        m_out[...] = m_new
        v_out[...] = v_new
        loss_out[...] = jnp.sum(tot[LOSS_ROW0:LOSS_ROW0 + 8, :]).reshape(1, 1)

    vm = pl.BlockSpec(memory_space=pltpu.VMEM)
    shp = jax.ShapeDtypeStruct((SMALL_ROWS, 128), F32)
    return pl.pallas_call(
        body, out_shape=[shp, shp, shp, shp, jax.ShapeDtypeStruct((1, 1), F32)],
        in_specs=[vm, vm, vm, vm], out_specs=[vm, vm, vm, vm, vm],
        scratch_shapes=[pltpu.VMEM((8, SMALL_ROWS, 128), F32), pltpu.SemaphoreType.DMA((7,)), pltpu.SemaphoreType.DMA((7,))],
        name="small_allreduce_adamw")(part, w, m, v)


def _row_tile(rows, cap):
    return max(t for t in range(16, min(rows, cap) + 1, 16) if rows % t == 0)


def _reduce_pair(name, parts):
    _, rows, w = parts.shape
    tr = _row_tile(rows, 576)
    nt = rows // tr

    def body(p_ref, o_ref, mine, theirs, send_sems, recv_sems):
        i = pl.program_id(0)
        x, y, c = _place()

        def copy(t):
            rows_t = pl.ds(pl.multiple_of(t * tr, tr), tr)
            return pltpu.make_async_remote_copy(src_ref=mine.at[rows_t], dst_ref=theirs.at[rows_t], send_sem=send_sems.at[t],
                                                recv_sem=recv_sems.at[t], device_id=(x, y, 1 - c), device_id_type=MESH)

        @pl.when(i < nt)
        def _():
            mine[pl.ds(pl.multiple_of(i * tr, tr), tr), :] = (
                (p_ref[0].astype(F32) + p_ref[1].astype(F32)) + p_ref[2].astype(F32)) + p_ref[3].astype(F32)
            copy(i).start()

        @pl.when(i >= nt)
        def _():
            t = i - nt
            copy(t).wait()
            rows_t = pl.ds(pl.multiple_of(t * tr, tr), tr)
            o_ref[...] = mine[rows_t, :] + theirs[rows_t, :]

    return pl.pallas_call(
        body, out_shape=jax.ShapeDtypeStruct((rows, w), F32), grid=(2 * nt,),
        in_specs=[pl.BlockSpec((4, tr, w), lambda i: (0, jnp.minimum(i, nt - 1), 0))],
        out_specs=pl.BlockSpec((tr, w), lambda i: (jnp.maximum(i - nt, 0), 0)),
        scratch_shapes=[pltpu.VMEM((rows, w), F32), pltpu.VMEM((rows, w), F32), pltpu.SemaphoreType.DMA((nt,)),
                        pltpu.SemaphoreType.DMA((nt,))],
        compiler_params=_params(("arbitrary",)), name=name)(parts)


def _presum_halves(name, shards):
    _, rows, w = shards.shape
    half = rows // 2

    def body(s_ref, o_ref, theirs, send_sems, recv_sems):
        x, y, c = _place()
        rows_c = pl.ds(pl.multiple_of(c * half, half), half)
        rows_s = pl.ds(pl.multiple_of((1 - c) * half, half), half)
        sent = [pltpu.make_async_remote_copy(src_ref=s_ref.at[j, rows_s], dst_ref=theirs.at[j], send_sem=send_sems.at[j],
                                             recv_sem=recv_sems.at[j], device_id=(x, y, 1 - c), device_id_type=MESH) for j in range(4)]
        for cp in sent:
            cp.start()
        for j, cp in enumerate(sent):
            cp.wait_recv()
            o_ref[j] = (s_ref[j, rows_c, :].astype(F32) + theirs[j].astype(F32)).astype(o_ref.dtype)
        for cp in sent:
            cp.wait_send()

    vm = pl.BlockSpec(memory_space=pltpu.VMEM)
    return pl.pallas_call(
        body, out_shape=jax.ShapeDtypeStruct((4, half, w), shards.dtype), in_specs=[vm], out_specs=vm,
        scratch_shapes=[pltpu.VMEM((4, half, w), shards.dtype), pltpu.SemaphoreType.DMA((4,)), pltpu.SemaphoreType.DMA((4,))],
        compiler_params=pltpu.CompilerParams(vmem_limit_bytes=VMEM_LIMIT_BYTES), name=name)(shards)


def _reduce_halves(name, parts):
    _, half, w = parts.shape

    def body(p_ref, o_ref, mine, send_sem, recv_sem):
        x, y, c = _place()
        rows_c = pl.ds(pl.multiple_of(c * half, half), half)
        rows_s = pl.ds(pl.multiple_of((1 - c) * half, half), half)
        mine[...] = ((p_ref[0].astype(F32) + p_ref[1].astype(F32)) + p_ref[2].astype(F32)) + p_ref[3].astype(F32)
        send = pltpu.make_async_remote_copy(src_ref=mine, dst_ref=o_ref.at[rows_c], send_sem=send_sem, recv_sem=recv_sem,
                                            device_id=(x, y, 1 - c), device_id_type=MESH)
        send.start()
        o_ref[rows_c, :] = mine[...]
        pltpu.make_async_remote_copy(src_ref=mine, dst_ref=o_ref.at[rows_s], send_sem=send_sem, recv_sem=recv_sem,
                                     device_id=(x, y, 1 - c), device_id_type=MESH).wait_recv()
        send.wait_send()

    vm = pl.BlockSpec(memory_space=pltpu.VMEM)
    return pl.pallas_call(
        body, out_shape=jax.ShapeDtypeStruct((2 * half, w), F32), in_specs=[vm], out_specs=vm,
        scratch_shapes=[pltpu.VMEM((half, w), F32), pltpu.SemaphoreType.DMA(()), pltpu.SemaphoreType.DMA(())],
        compiler_params=pltpu.CompilerParams(vmem_limit_bytes=VMEM_LIMIT_BYTES), name=name)(parts)


def _adamw_shard(name, g, w, m, v):
    _, rows, cols = w.shape
    tr = _row_tile(rows, 256)

    def body(g_ref, w_ref, m_ref, v_ref, g_out, d_out, m_out, v_out):
        gv = g_ref[...]
        delta, m_new, v_new = _adamw(w_ref[...], gv, m_ref[...], v_ref[...])
        g_out[...] = gv
        d_out[...] = delta
        m_out[...] = m_new
        v_out[...] = v_new

    t_spec = pl.BlockSpec((None, tr, cols), lambda i: (0, i, 0))
    shp = jax.ShapeDtypeStruct((1, rows, cols), F32)
    return pl.pallas_call(body, out_shape=[shp] * 4, grid=(rows // tr,), in_specs=[pl.BlockSpec((tr, cols), lambda i: (i, 0))] + [t_spec] * 3,
                          out_specs=[t_spec] * 4, compiler_params=_params(("parallel",)), name=name)(g, w, m, v)


def _shard_shape(name):
    _, r, c, ax = BIG_BY_NAME[name]
    return (r, c // 4) if ax == 1 else (r // 4, c)


def _pad_rows(a, axis):
    pad = [(0, 0)] * a.ndim
    pad[axis] = (0, -a.shape[axis] % PACK_ALIGN)
    return jnp.pad(a, pad)


def _pack_shards(names, shards, dtype):
    return _pad_rows(jnp.concatenate([s.astype(dtype).reshape(-1, PACK_W) for s in shards], axis=0), 0)


def _unpack_shards(names, slab):
    out, off = [], 0
    for name in names:
        rs, cs = _shard_shape(name)
        n = rs * cs // PACK_W
        out.append(slab[off:off + n].reshape(rs, cs))
        off += n
    return out


def _unpack_full(names, slabs):
    out, off = [], 0
    for name in names:
        _, r, c, ax = BIG_BY_NAME[name]
        n = r * c // 4 // PACK_W
        seg = slabs[:, off:off + n]
        out.append(seg.reshape(4, r, c // 4).transpose(1, 0, 2).reshape(r, c) if ax == 1 else seg.reshape(r, c))
        off += n
    return out


def _pack_full(names, mats, dtype):
    segs = []
    for name, a in zip(names, mats):
        _, r, c, ax = BIG_BY_NAME[name]
        a = a.astype(dtype)
        a = a.reshape(r, 4, c // 4).transpose(1, 0, 2) if ax == 1 else a
        segs.append(a.reshape(4, -1, PACK_W))
    return _pad_rows(jnp.concatenate(segs, axis=1), 1)


def _pad_heads_cols(wm, heads, d):
    k = wm.shape[0]
    return jnp.pad(wm.reshape(k, heads, d), ((0, 0), (0, 0), (0, HEAD_W - d))).reshape(k, heads * HEAD_W)


def _unpad_heads_cols(wm, heads, d):
    k = wm.shape[0]
    return wm.reshape(k, heads, HEAD_W)[:, :, :d].reshape(k, heads * d)


def _win_ext(w_in):
    o = np.cumsum([0, Q_LORA, KV_LORA, QK_ROPE, H_B * HD_B, KV_B * HD_B, KV_B * HD_B, D_MODEL, D_MODEL])
    pc = lambda a, n: jnp.pad(a, ((0, 0), (0, n - a.shape[1])))
    return jnp.concatenate([
        _pad_heads_cols(w_in[:, o[3]:o[4]], H_B, HD_B), w_in[:, o[0]:o[1]], w_in[:, o[1]:o[2]], pc(w_in[:, o[2]:o[3]], HEAD_W),
        _pad_heads_cols(w_in[:, o[4]:o[5]], KV_B, HD_B), _pad_heads_cols(w_in[:, o[5]:o[6]], KV_B, HD_B),
        w_in[:, o[6]:o[7]], w_in[:, o[7]:o[8]]], axis=1)


def _win_unext(blocks):
    c = HEAD_W
    qb, mid, ga, gb = blocks
    at = lambda zc: (zc - ZC_QLAT) * c
    return jnp.concatenate([
        mid[:, at(ZC_QLAT):at(ZC_CKV)], mid[:, at(ZC_CKV):at(ZC_KPE)], mid[:, at(ZC_KPE):at(ZC_KPE) + QK_ROPE],
        _unpad_heads_cols(qb, H_B, HD_B), _unpad_heads_cols(mid[:, at(ZC_KB):at(ZC_VB)], KV_B, HD_B),
        _unpad_heads_cols(mid[:, at(ZC_VB):at(ZC_GA)], KV_B, HD_B), ga, gb], axis=1)


def _wkv_ext(w_kvb):
    wk = w_kvb.reshape(KV_LORA, H_A, QK_NOPE + V_DIM_A)
    k_cols = jnp.pad(wk[:, :, :QK_NOPE], ((0, 0), (0, 0), (0, HEAD_W - QK_NOPE))).reshape(KV_LORA, H_A * HEAD_W)
    v_cols = jnp.pad(wk[:, :, QK_NOPE:], ((0, 0), (0, 0), (0, HEAD_W - V_DIM_A))).reshape(KV_LORA, H_A * HEAD_W)
    eye = jnp.pad(jnp.eye(QK_ROPE, dtype=w_kvb.dtype), ((0, 0), (QK_NOPE, HEAD_W - QK_NOPE - QK_ROPE)))
    pe_rows = jnp.concatenate([jnp.tile(eye, (1, H_A)), jnp.zeros((QK_ROPE, H_A * HEAD_W), w_kvb.dtype)], axis=1)
    top = jnp.concatenate([k_cols, v_cols], axis=1)
    return jnp.concatenate([top, pe_rows, jnp.zeros((2 * HEAD_W - KV_LORA - QK_ROPE, 2 * H_A * HEAD_W), w_kvb.dtype)], axis=0)


def _wkv_unext(k_block, v_block):
    k_cols = k_block[:KV_LORA].reshape(KV_LORA, H_A, HEAD_W)[:, :, :QK_NOPE]
    v_cols = v_block[:KV_LORA].reshape(KV_LORA, H_A, HEAD_W)[:, :, :V_DIM_A]
    return jnp.concatenate([k_cols, v_cols], axis=2).reshape(KV_LORA, H_A * (QK_NOPE + V_DIM_A))


def _pad_heads_rows(wm, heads, d):
    n = wm.shape[1]
    return jnp.pad(wm.reshape(heads, d, n), ((0, 0), (0, HEAD_W - d), (0, 0))).reshape(heads * HEAD_W, n)


def _unpad_heads_rows(wm, heads, d):
    n = wm.shape[1]
    return wm.reshape(heads, HEAD_W, n)[:, :d].reshape(heads * d, n)


def _rope_tables(seq):
    def ang(pos, dim):
        inv = np.float32(ROPE_THETA) ** (-np.arange(0, dim, 2, dtype=np.float32) / np.float32(dim))
        return pos.astype(np.float32)[:, None] * inv[None, :]

    def rot(dim):
        r = np.zeros((dim, dim), np.float32)
        half = dim // 2
        r[np.arange(half) + half, np.arange(half)] = -1.0
        r[np.arange(half), np.arange(half) + half] = 1.0
        return r

    def table(blocks):
        cos, sin = np.ones((seq, HEAD_W), np.float32), np.zeros((seq, HEAD_W), np.float32)
        pm = np.zeros((HEAD_W, HEAD_W), np.float32)
        for c0, a in blocks:
            d = 2 * a.shape[1]
            cos[:, c0:c0 + d] = np.concatenate([np.cos(a), np.cos(a)], axis=1)
            sin[:, c0:c0 + d] = np.concatenate([np.sin(a), np.sin(a)], axis=1)
            pm[c0:c0 + d, c0:c0 + d] = rot(d)
        return jnp.asarray(cos), jnp.asarray(sin), jnp.asarray(pm, BF16), jnp.asarray(pm.T, BF16)

    tok = np.arange(seq)
    a1 = ang(tok, QK_ROPE)
    arow, acol = ang(tok // GRID_W, HD_B // 2), ang(tok % GRID_W, HD_B // 2)
    return table([(QK_NOPE, a1)]), table([(0, a1)]), table([(0, arow), (HD_B // 2, acol)])


def _local_step(x, p, tgt, gains, wts, ride=None):
    nb, seq, _ = x.shape
    t = nb * seq
    x0 = x.reshape(t, D_MODEL)
    p2 = p.reshape(t, PLE_DIM)
    tg = tgt.reshape(t, D_MODEL)
    (cq_t, sq_t, pq, pq_t), (ck_t, sk_t, pk, pk_t), (cb_t, sb_t, pb, pb_t) = _rope_tables(seq)
    padg = lambda g: jnp.pad(g, ((0, 0), (0, HEAD_W - g.shape[1])))
    g_qn, g_kn = padg(gains["g_qn"]), padg(gains["g_kn"])

    win = _win_ext(wts["w_in"])
    wqb = _pad_heads_cols(wts["w_qb"], H_A, QK_NOPE + QK_ROPE)
    wkv = _wkv_ext(wts["w_kvb"])

    norm = lambda n: (lambda v, g: (_rms(v, g, n),))
    full = lambda a: (a, a.shape[1], 0, False)
    wts = dict(wts)
    rider_of = lambda kernel_name: None if ride is None else ride["gather"][kernel_name][0]

    def landed(kernel_name, got):
        if ride is not None:
            wts.update(ride["gather"][kernel_name][1](got))

    h = _rowwise("norm_mix", norm(D_MODEL), [full(x0)], [(D_MODEL, BF16, D_MODEL, False)], consts=[gains["g_mix"]])
    res = _mm("in_proj", h, win, tn=2048, rider=rider_of("in_proj"))
    z, got = (res, []) if ride is None else (res[0], res[1:])
    landed("in_proj", got)

    def rope_fwd(scale):
        return lambda v, cos, sin, pm: ((v * cos + _perm(v, pm) * sin) * scale,)

    heads_tile = lambda n: (n * HEAD_W, BF16, n * HEAD_W, False)
    z_qlat, z_ckv, z_kpe = (z, Q_LORA, ZC_QLAT // 2, False), (z, HEAD_W, ZC_CKV, False), (z, HEAD_W, ZC_KPE, False)

    def q_path(zq, cos, sin, g, w, pm):
        cqv = _rms(zq, g, Q_LORA).astype(BF16)
        qa = jnp.dot(cqv, w[...], preferred_element_type=F32)
        return (cqv, *_per_head(rope_fwd(SCALE_A), H_A, 1, 1)(qa, cos, sin, pm))

    cq, q_a = _rowwise("q_path", q_path, [z_qlat], [(Q_LORA, BF16, Q_LORA, False), heads_tile(H_A)],
                       pos=[cq_t, sq_t], consts=[gains["g_qa"], wqb, pq], seq=seq)

    def kv_path(ckv_raw, kpe_raw, cos, sin, g, w, pm):
        kinv = jnp.concatenate([_rms(ckv_raw, g, KV_LORA), *rope_fwd(1.0)(kpe_raw, cos, sin, pm)], axis=1).astype(BF16)
        return kinv, jnp.dot(kinv, w[...], preferred_element_type=F32)

    kin, kv_a = _rowwise("kv_path", kv_path, [z_ckv, z_kpe], [heads_tile(2), heads_tile(2 * H_A)],
                         pos=[ck_t, sk_t], consts=[gains["g_kva"], wkv, pk], seq=seq)
    o_a, lse_a, *got = _attn_fwd("attn_a_fwd", q_a, kv_a, 0, kv_a, H_A, heads=H_A, group=1, nseq=nb, seq=seq,
                                 rider=rider_of("attn_a_fwd"))
    landed("attn_a_fwd", got)

    def prep_fwd(scale):
        def fn(v, cos, sin, g, pm):
            yv = _rms(v, g, HD_B)
            return ((yv * cos + _perm(yv, pm) * sin) * scale,)
        return fn

    z_qb, z_kb = (z, H_B * HEAD_W, ZC_QB // H_B, False), (z, KV_B * HEAD_W, ZC_KB // KV_B, False)
    res = _rowwise("prep_qb", _per_head(prep_fwd(SCALE_B), H_B, 1, 1), [z_qb], [heads_tile(H_B)],
                   pos=[cb_t, sb_t], consts=[g_qn, pb], seq=seq, rider=rider_of("prep_qb"))
    q_b, got = (res, []) if ride is None else (res[0], res[1:])
    landed("prep_qb", got)
    k_b = _rowwise("prep_kb", _per_head(prep_fwd(1.0), KV_B, 1, 1), [z_kb], [heads_tile(KV_B)],
                   pos=[cb_t, sb_t], consts=[g_kn, pb], seq=seq)
    o_b, lse_b, *got = _attn_fwd("attn_b_fwd", q_b, k_b, 0, z, ZC_VB, heads=H_B, group=H_B // KV_B, nseq=nb, seq=seq,
                                 rider=rider_of("attn_b_fwd"))
    landed("attn_b_fwd", got)
    woa = _pad_heads_rows(wts["w_oa"], H_A, V_DIM_A)
    wob = _pad_heads_rows(wts["w_ob"], H_B, HD_B)
    wo, wup, wdown = wts["w_o"], wts["w_up"], wts["w_down"]

    def residual_norm(acc, r, g):
        xv = r + acc
        return xv, _rms(xv, g, D_MODEL)

    def mix_out(oa, ob, ga, gb, r, g, w_a, w_b, w_out):
        a = jnp.dot(oa, w_a[...], preferred_element_type=F32)
        b = jnp.dot(ob, w_b[...], preferred_element_type=F32)
        mg = (_sigmoid(ga) * a + _sigmoid(gb) * b).astype(BF16)
        return (a, b, mg, *residual_norm(jnp.dot(mg, w_out[...], preferred_element_type=F32), r, g))

    z_ga, z_gb = (z, D_MODEL, ZC_GA // 8, False), (z, D_MODEL, ZC_GB // 8, False)
    wide = lambda d: (D_MODEL, d, D_MODEL, False)
    ya, yb, merged, x1, h2, *got = _rowwise("mix_out", mix_out, [full(o_a), full(o_b), z_ga, z_gb, full(x0)],
                                            [wide(F32), wide(F32), wide(BF16), wide(F32), wide(BF16)],
                                            consts=[gains["g_mlp"], woa, wob, wo], tm=256, rider=rider_of("mix_out"))
    landed("mix_out", got)
    wpg, wple = wts["w_ple_gate"], wts["w_ple"]

    square = lambda v: v * v
    u = _mm("mlp_up", h2, wup, b_slots=True, out_dtypes=(BF16,), epi=lambda acc: (jnp.maximum(acc, 0.0),), tm=1024)
    x2, h3 = _mm("mlp_down", u, wdown, a_pre=square, out_dtypes=(F32, BF16), epi=residual_norm, extras=(x1,), consts=[gains["g_ple"]])

    def tail(x2v, h3v, pv, tv, gf, w_gate, w_emb):
        sg = _sigmoid(jnp.dot(h3v, w_gate[...], preferred_element_type=F32))
        pev = jnp.dot(pv.astype(BF16), w_emb[...], preferred_element_type=F32)
        x3 = x2v + sg * pev
        rs = lax.rsqrt(jnp.sum(x3 * x3, axis=-1, keepdims=True) * (1.0 / D_MODEL) + EPS)
        xh = x3 * rs
        err = xh * gf - tv
        dy = err * (1.0 / D_MODEL)
        dyg = dy * gf
        dx3 = rs * (dyg - xh * (jnp.sum(dyg * xh, axis=-1, keepdims=True) * (1.0 / D_MODEL)))
        return (dx3, dx3 * pev * sg * (1.0 - sg), dx3 * sg,
                jnp.sum(err * err, axis=0, keepdims=True) * (0.5 / D_MODEL), jnp.sum(dy * xh, axis=0, keepdims=True))

    dx3, dgpre, dpe, loss_part, dg_final = _rowwise(
        "tail", tail, [full(x2), full(h3), full(p2), full(tg)], [wide(F32), wide(BF16), wide(BF16)],
        consts=[gains["g_final"].reshape(1, D_MODEL), wpg, wple], accs=[(1, D_MODEL), (1, D_MODEL)], tm=256)

    def norm_res_bwd(dh, v, res, g):
        dx, dg = _rms_bwd(dh, v, g, D_MODEL)
        return dx + res, dg

    dw = {}
    dw["w_ple"] = _mm_tn("dw_ple", p2, dpe)
    dw["w_ple_gate"] = _mm_tn("dw_ple_gate", h3, dgpre)
    dx2, dg_ple = _mm("d_ple_gate", dgpre, wpg, trans_b=True, epi=norm_res_bwd, extras=(x2, dx3), consts=[gains["g_ple"]],
                      accs=[(1, D_MODEL)], tm=256)
    dw["w_down"] = _mm_tn("dw_down", u, dx2, a_pre=square)
    dupre = _mm("d_mlp_down", dx2, wdown, trans_b=True, out_dtypes=(BF16,), epi=lambda acc, uv: (acc * (2.0 * uv.astype(F32)),),
                extras=(u,), tn=2048)
    dw["w_up"] = _mm_tn("dw_up", h2, dupre, out_slots=True)
    n_up = wup.shape[0]
    dx1, dg_mlp = _mm("d_mlp_up", [(dupre, j, wup.shape[2]) for j in range(n_up)], [(wup, j) for j in range(n_up)], trans_b=True,
                      epi=norm_res_bwd, extras=(x1, dx2), consts=[gains["g_mlp"]],
                      accs=[(1, D_MODEL)], tm=256)
    dw["w_o"] = _mm_tn("dw_o", merged, dx1)

    def merge_bwd(dm, ga, gb, a, b, w_a, w_b):
        sa, sb = _sigmoid(ga), _sigmoid(gb)
        da, db = (dm * sa).astype(BF16), (dm * sb).astype(BF16)
        nt = (((1,), (1,)), ((), ()))
        return (da, db, dm * a * sa * (1.0 - sa), dm * b * sb * (1.0 - sb),
                lax.dot_general(da, w_a, nt, preferred_element_type=F32), lax.dot_general(db, w_b, nt, preferred_element_type=F32))

    dya, dyb, dga, dgb, do_a, do_b = _mm("d_out_proj", dx1, wo, trans_b=True, out_dtypes=(BF16,) * 6, epi=merge_bwd,
                                         extras=((z, ZC_GA // 8), (z, ZC_GB // 8), ya, yb), consts=[woa, wob], tm=256)
    dw["w_oa"] = _unpad_heads_rows(_mm_tn("dw_oa", o_a, dya), H_A, V_DIM_A)
    dw["w_ob"] = _unpad_heads_rows(_mm_tn("dw_ob", o_b, dyb), H_B, HD_B)
    res_a = _attn_bwd("attn_a_bwd", q_a, kv_a, 0, kv_a, H_A, o_a, do_a, lse_a, heads=H_A, group=1, nseq=nb, seq=seq,
                      rider=ride and ride["scatter_a"](dw))
    dq_a, dk_a, dv_a = res_a[:3]
    if ride is not None:
        ride["out"]["parts_a"] = res_a[3:]

    def rope_bwd(scale):
        return lambda d, cos, sin, pm_t: ((d * cos + _perm(d * sin, pm_t)) * scale,)

    nt_dims = (((1,), (1,)), ((), ()))

    def q_path_bwd(dq, zq, cos, sin, g, w, pm_t):
        dqav = _per_head(rope_bwd(SCALE_A), H_A, 1, 1)(dq, cos, sin, pm_t)[0].astype(BF16)
        dcq = lax.dot_general(dqav, w[...], nt_dims, preferred_element_type=F32)
        return (dqav, *_rms_bwd(dcq, zq, g, Q_LORA))

    dqa, dq_lat, dg_qa = _rowwise("q_path_bwd", q_path_bwd, [full(dq_a), z_qlat], [heads_tile(H_A), (Q_LORA, BF16, Q_LORA, False)],
                                  pos=[cq_t, sq_t], consts=[gains["g_qa"], wqb, pq_t], accs=[(1, Q_LORA)], seq=seq)
    dw["w_qb"] = _unpad_heads_cols(_mm_tn("dw_qb", cq, dqa), H_A, QK_NOPE + QK_ROPE)
    dw["w_kvb"] = _wkv_unext(_mm_tn("dw_kv_k", kin, dk_a), _mm_tn("dw_kv_v", kin, dv_a))
    dq_b, dk_b, dv_b, *parts_b = _attn_bwd("attn_b_bwd", q_b, k_b, 0, z, ZC_VB, o_b, do_b, lse_b, heads=H_B, group=H_B // KV_B,
                                               nseq=nb, seq=seq, rider=ride and ride["scatter_b"](dw))
    if ride is not None:
        ride["out"]["parts_b"] = parts_b

    def kv_path_bwd(dk, dv, ckv_raw, cos, sin, g, w, pm_t):
        kv_w = H_A * HEAD_W
        wv = w[...]
        dkin = (lax.dot_general(dk, wv[:, :kv_w], nt_dims, preferred_element_type=F32)
                + lax.dot_general(dv, wv[:, kv_w:], nt_dims, preferred_element_type=F32))
        dckv_raw, dg = _rms_bwd(dkin[:, :HEAD_W], ckv_raw, g, KV_LORA)
        return (dckv_raw, *rope_bwd(1.0)(dkin[:, HEAD_W:], cos, sin, pm_t), dg)

    dckv, dkpe, dg_kva = _rowwise("kv_path_bwd", kv_path_bwd, [full(dk_a), full(dv_a), z_ckv], [heads_tile(1), heads_tile(1)],
                                  pos=[ck_t, sk_t], consts=[gains["g_kva"], wkv, pk_t], accs=[(1, KV_LORA)], seq=seq)

    def prep_bwd(scale):
        def fn(d, v, cos, sin, g, pm_t):
            dyv = (d * cos + _perm(d * sin, pm_t)) * scale
            return _rms_bwd(dyv, v, g, HD_B)
        return fn

    dqb, dg_qn = _rowwise("prep_qb_bwd", _per_head(prep_bwd(SCALE_B), H_B, 2, 1), [full(dq_b), z_qb], [heads_tile(H_B)],
                          pos=[cb_t, sb_t], consts=[g_qn, pb_t], accs=[(1, HEAD_W)], seq=seq)
    dkb, dg_kn = _rowwise("prep_kb_bwd", _per_head(prep_bwd(1.0), KV_B, 2, 1), [full(dk_b), z_kb], [heads_tile(KV_B)],
                          pos=[cb_t, sb_t], consts=[g_kn, pb_t], accs=[(1, HEAD_W)], seq=seq)

    dz = [dqb, jnp.concatenate([dq_lat, dckv, dkpe, dkb, dv_b], axis=1), dga, dgb]
    dw["w_in"] = _win_unext([_mm_tn("dw_in_%d" % j, h, blk) for j, blk in enumerate(dz)])
    dx0, dg_mix, *parts_in = _mm("d_in_proj", dz, [(win, j, D_MODEL) for j in range(4)], trans_b=True, epi=norm_res_bwd, extras=(x0, dx1), consts=[gains["g_mix"]],
                                 accs=[(1, D_MODEL)], tm=256, rider=ride and ride["scatter_in"](dw))
    if ride is not None:
        ride["out"]["parts_in"] = parts_in

    dg = {"g_mix": dg_mix, "g_qa": dg_qa, "g_kva": dg_kva, "g_qn": dg_qn[:, :HD_B], "g_kn": dg_kn[:, :HD_B],
          "g_mlp": dg_mlp, "g_ple": dg_ple, "g_final": dg_final}
    return loss_part, dx0.reshape(nb, seq, D_MODEL), dg, dw


def _pack_small(vals, loss_part=None):
    flat = jnp.concatenate([vals[n].reshape(1, -1) for n, _ in SMALL], axis=1)
    loss = jnp.zeros((1, 8 * 128), F32) if loss_part is None else loss_part
    gap = jnp.zeros((1, LOSS_ROW0 * 128 - SMALL_N), F32)
    return jnp.concatenate([flat, gap, loss], axis=1).reshape(SMALL_ROWS, 128)


def _unpack_small(slab, like):
    flat, out, off = slab.reshape(-1), {}, 0
    for n, k in SMALL:
        out[n] = flat[off:off + k].reshape(like[n].shape)
        off += k
    return out


def kernel(x, p, g_mix, w_in, g_qa, w_qb, g_kva, w_kvb, g_qn, g_kn, w_oa, w_ob, w_o, g_mlp, w_up, w_down, g_ple, w_ple_gate, w_ple, g_final, loss_target, m_g_mix, m_w_in, m_g_qa, m_w_qb, m_g_kva, m_w_kvb, m_g_qn, m_g_kn, m_w_oa, m_w_ob, m_w_o, m_g_mlp, m_w_up, m_w_down, m_g_ple, m_w_ple_gate, m_w_ple, m_g_final, v_g_mix, v_w_in, v_g_qa, v_w_qb, v_g_kva, v_w_kvb, v_g_qn, v_g_kn, v_w_oa, v_w_ob, v_w_o, v_g_mlp, v_w_up, v_w_down, v_g_ple, v_w_ple_gate, v_w_ple, v_g_final):
    given = dict(locals())
    order = ["g_mix", "w_in", "g_qa", "w_qb", "g_kva", "w_kvb", "g_qn", "g_kn", "w_oa", "w_ob", "w_o", "g_mlp", "w_up",
             "w_down", "g_ple", "w_ple_gate", "w_ple", "g_final"]
    big_names = [n for n, _, _, _ in BIG]
    local = lambda prefix, names: [given[prefix + n][0] for n in names]
    slab = lambda names: _pack_shards(names, local("", names), BF16)
    bf = lambda n: given[n][0].astype(BF16)
    cols_full = lambda g: g.transpose(1, 0, 2).reshape(g.shape[1], -1)
    rows_full = lambda g: g.reshape(-1, g.shape[2])
    shards_cols = lambda a: a.reshape(a.shape[0], 4, a.shape[1] // 4).transpose(1, 0, 2)
    shards_rows = lambda a: a.reshape(4, a.shape[0] // 4, a.shape[1])
    packed = lambda names, dw: _pack_full(names, [dw[n] for n in names], BF16)
    branch_out = ["w_oa", "w_ob"]

    got_in, got_early = _gather_by_halves("weight_gather_early", [bf("w_in"), slab(SLAB_EARLY)])
    wts = {"w_in": cols_full(got_in), **dict(zip(SLAB_EARLY, _unpack_full(SLAB_EARLY, got_early)))}
    gains = {n: given[n].reshape(1, -1) for n, _ in SMALL}
    ride = {
        "gather": {
            "in_proj": (_Exchange("gather", [bf("w_o")]), lambda got: {"w_o": rows_full(got[0])}),
            "prep_qb": (_Exchange("gather", [slab(branch_out)]), lambda got: dict(zip(branch_out, _unpack_full(branch_out, got[0])))),
            "attn_a_fwd": (_Exchange("gather", [bf("w_up")]), lambda got: {"w_up": got[0]}),
            "attn_b_fwd": (_Exchange("gather", [bf("w_down")]), lambda got: {"w_down": rows_full(got[0])}),
            "mix_out": (_Exchange("gather", [bf("w_ple_gate"), bf("w_ple")]),
                        lambda got: {"w_ple_gate": rows_full(got[0]), "w_ple": cols_full(got[1])}),
        },
        "scatter_a": lambda dw: _Exchange("scatter", [dw["w_up"], shards_rows(dw["w_o"]), packed(SLAB_LATE, dw)]),
        "scatter_b": lambda dw: _Exchange("scatter", [shards_rows(dw["w_down"]), shards_rows(dw["w_ple_gate"]), packed(SLAB_EARLY, dw)]),
        "scatter_in": lambda dw: _Exchange("scatter", [_presum_halves("grad_presum_in", shards_cols(dw["w_in"]))]),
        "out": {},
    }
    loss_part, grad_x, dg, dw = _local_step(x, p[0], loss_target, gains, wts, ride)

    small = lambda prefix: _pack_small({n: given[prefix + n] for n, _ in SMALL})
    g_s, d_s, m_s, v_s, loss = _small_allreduce_adamw(_pack_small(dg, loss_part), small(""), small("m_"), small("v_"))

    parts = ride["out"]
    grads = {"w_up": _reduce_pair("grad_reduce_up", parts["parts_a"][0]), "w_o": _reduce_pair("grad_reduce_o", parts["parts_a"][1]),
             "w_down": _reduce_pair("grad_reduce_down", parts["parts_b"][0]),
             "w_ple_gate": _reduce_pair("grad_reduce_ple_gate", parts["parts_b"][1]),
             "w_in": _reduce_halves("grad_reduce_in", parts["parts_in"][0])}
    grads.update(zip(SLAB_LATE, _unpack_shards(SLAB_LATE, _reduce_pair("grad_reduce_slab_late", parts["parts_a"][2]))))
    grads.update(zip(SLAB_EARLY, _unpack_shards(SLAB_EARLY, _reduce_pair("grad_reduce_slab_early", parts["parts_b"][2]))))

    res = {}
    for key, slab in (("grad_", g_s), ("delta_", d_s), ("new_m_", m_s), ("new_v_", v_s)):
        for n, val in _unpack_small(slab, given).items():
            res[key + n] = val
    for n in big_names:
        res["grad_" + n], res["delta_" + n], res["new_m_" + n], res["new_v_" + n] = _adamw_shard(
            "adamw_" + n, grads[n], given[n], given["m_" + n], given["v_" + n])
    outs = [loss.reshape(()), grad_x]
    for key in ("grad_", "delta_", "new_m_", "new_v_"):
        outs += [res[key + n] for n in order]
    return tuple(outs)
```

```python
import functools

import numpy as np
import jax
import jax.numpy as jnp
from jax import lax
from jax.experimental import pallas as pl
from jax.experimental.pallas import tpu as pltpu

F32 = jnp.float32
BF16 = jnp.bfloat16
MESH = pl.DeviceIdType.MESH

D_MODEL = 1024
GRID_W = 64
ROPE_THETA = 10000.0
EPS = 1e-6
H_A, QK_NOPE, QK_ROPE, V_DIM_A, Q_LORA, KV_LORA = 8, 64, 32, 64, 256, 128
H_B, KV_B, HD_B = 8, 2, 64
D_FF = 4096
PLE_DIM = 256
HEAD_W = 128
SCALE_A = (QK_NOPE + QK_ROPE) ** -0.5
SCALE_B = HD_B ** -0.5

ADAM_LR, ADAM_B1, ADAM_B2, ADAM_EPS, ADAM_WD, ADAM_STEP = 0.001, 0.9, 0.999, 1e-08, 0.01, 10
M_HAT_DIV = 1.0 - ADAM_B1 ** ADAM_STEP
V_HAT_DIV = 1.0 - ADAM_B2 ** ADAM_STEP

VMEM_LIMIT_BYTES = 56 * 1024 * 1024

ZC_QB, ZC_QLAT, ZC_CKV, ZC_KPE, ZC_KB, ZC_VB, ZC_GA, ZC_GB = 0, 8, 10, 11, 12, 14, 16, 24
Z_WIDTH = 32 * HEAD_W

BIG = [
    ("w_in", 1024, 3232, 1), ("w_qb", 256, 768, 1), ("w_kvb", 128, 1024, 1), ("w_oa", 512, 1024, 1),
    ("w_ob", 512, 1024, 1), ("w_o", 1024, 1024, 0), ("w_up", 1024, 4096, 1), ("w_down", 4096, 1024, 0),
    ("w_ple_gate", 1024, 1024, 0), ("w_ple", 256, 1024, 1),
]
BIG_BY_NAME = {e[0]: e for e in BIG}
PACK_W = 1024
PACK_ALIGN = 64
SLAB_EARLY = ["w_qb", "w_kvb"]
SLAB_LATE = ["w_oa", "w_ob", "w_ple"]

SMALL = [("g_mix", 1024), ("g_qa", 256), ("g_kva", 128), ("g_qn", 64), ("g_kn", 64), ("g_mlp", 1024),
         ("g_ple", 1024), ("g_final", 1024)]
SMALL_N = sum(n for _, n in SMALL)
LOSS_ROW0 = 40
SMALL_ROWS = 48


def _params(sem):
    return pltpu.CompilerParams(dimension_semantics=sem, vmem_limit_bytes=VMEM_LIMIT_BYTES)


def _sigmoid(v):
    return 1.0 / (1.0 + jnp.exp(-v))


def _perm(v, p_ref):
    pm = p_ref[...]
    hi = v.astype(BF16)
    lo = (v - hi.astype(F32)).astype(BF16)
    return (jnp.dot(hi, pm, preferred_element_type=F32) + jnp.dot(lo, pm, preferred_element_type=F32))


def _rms(v, g, n):
    rs = lax.rsqrt(jnp.sum(v * v, axis=-1, keepdims=True) * (1.0 / n) + EPS)
    return v * rs * g


def _rms_bwd(dy, v, g, n):
    rs = lax.rsqrt(jnp.sum(v * v, axis=-1, keepdims=True) * (1.0 / n) + EPS)
    vh = v * rs
    dyg = dy * g
    dx = rs * (dyg - vh * (jnp.sum(dyg * vh, axis=-1, keepdims=True) * (1.0 / n)))
    return dx, jnp.sum(dy * vh, axis=0, keepdims=True)


def _ride(body, grid, rider):
    if rider is None:
        return body, [], [], [], []
    n_x, n_sem = len(rider.srcs), len(rider.scratch)

    def wrapped(*refs):
        ids = [pl.program_id(a) for a in range(len(grid))]
        n_in = len(refs) - n_sem - 2 * n_x - rider.n_core_out - rider.n_core_scratch
        core_in, srcs = refs[:n_in], refs[n_in:n_in + n_x]
        core_out = refs[n_in + n_x:n_in + n_x + rider.n_core_out]
        dsts = refs[n_in + n_x + rider.n_core_out:n_in + 2 * n_x + rider.n_core_out]
        core_scr = refs[n_in + 2 * n_x + rider.n_core_out:len(refs) - n_sem]
        sems = refs[len(refs) - n_sem:]

        @pl.when(functools.reduce(jnp.logical_and, [a == 0 for a in ids]))
        def _():
            rider.start(srcs, dsts, *sems)

        body(*core_in, *core_out, *core_scr)

        @pl.when(functools.reduce(jnp.logical_and, [a == n - 1 for a, n in zip(ids, grid)]))
        def _():
            rider.finish(srcs, dsts, *sems)

    hbm = pl.BlockSpec(memory_space=pl.ANY)
    return wrapped, list(rider.srcs), [hbm] * n_x, list(rider.out_shapes), list(rider.scratch)


def _mm(name, a, b, *, trans_b=False, b_slots=False, a_pre=None, out_dtypes=(F32,), epi=None, extras=(), consts=(), accs=(), tm=512,
        tn=None, rider=None):
    a_ops = [o if isinstance(o, tuple) else (o, 0, o.shape[1]) for o in (a if isinstance(a, list) else [a])]
    b_ops = b if isinstance(b, list) else [b]
    assert len(a_ops) == len(b_ops) and not (b_slots and (trans_b or len(b_ops) > 1))
    m = a_ops[0][0].shape[0]
    if b_slots:
        n, tn = b.shape[0] * b.shape[2], b.shape[2]
    else:
        first = b_ops[0][0] if isinstance(b_ops[0], tuple) else b_ops[0]
        n = first.shape[-2] if trans_b else first.shape[1]
        tn = n if tn is None else min(tn, n)
    tm = min(tm, m)
    assert m % tm == 0 and n % tn == 0
    extras = [e if isinstance(e, tuple) else (e, 0) for e in extras]
    n_p, n_ex, n_c, n_out, n_acc = len(a_ops), len(extras), len(consts), len(out_dtypes), len(accs)
    dims = (((1,), (1,)), ((), ())) if trans_b else (((1,), (0,)), ((), ()))

    def body(*refs):
        acc = None
        for a_ref, b_ref in zip(refs[:n_p], refs[n_p:2 * n_p]):
            av = a_ref[...] if a_pre is None else a_pre(a_ref[...].astype(F32))
            part = lax.dot_general(av.astype(BF16), b_ref[...].astype(BF16), dims, preferred_element_type=F32)
            acc = part if acc is None else acc + part
        rest = refs[2 * n_p:]
        res = (acc,) if epi is None else epi(acc, *[e[...] for e in rest[:n_ex + n_c]])
        o_refs = rest[n_ex + n_c:]
        for o_ref, r in zip(o_refs[:n_out], res[:n_out]):
            o_ref[...] = r.astype(o_ref.dtype)
        if n_acc:
            first_step = jnp.logical_and(pl.program_id(0) == 0, pl.program_id(1) == 0)

            @pl.when(first_step)
            def _():
                for o_ref, r in zip(o_refs[n_out:], res[n_out:]):
                    o_ref[...] = r

            @pl.when(jnp.logical_not(first_step))
            def _():
                for o_ref, r in zip(o_refs[n_out:], res[n_out:]):
                    o_ref[...] += r

    def b_spec(op, k_i):
        if b_slots:
            return pl.BlockSpec((None, k_i, tn), lambda j, i: (j, 0, 0))
        if not isinstance(op, tuple):
            return pl.BlockSpec((tn, k_i), lambda j, i: (j, 0)) if trans_b else pl.BlockSpec((k_i, tn), lambda j, i: (0, j))
        assert trans_b
        if len(op) == 2:
            return pl.BlockSpec((None, tn, k_i), lambda j, i, slot=op[1]: (slot, j, 0))
        return pl.BlockSpec((tn, k_i), lambda j, i, blk=op[1]: (j, blk))

    grid = (n // tn, m // tm)
    if rider is not None:
        rider.n_core_out, rider.n_core_scratch = n_out + n_acc, 0
    body, x_in, x_spec, x_out, x_scr = _ride(body, grid, rider)
    a_specs = [pl.BlockSpec((tm, k_i), lambda j, i, blk=blk: (i, blk)) for _, blk, k_i in a_ops]
    b_specs = [b_spec(op, k_i) for op, (_, _, k_i) in zip(b_ops, a_ops)]
    t_spec = pl.BlockSpec((tm, tn), lambda j, i: (i, j))
    e_specs = [pl.BlockSpec((tm, tn), lambda j, i, off=off: (i, j + off)) for _, off in extras]
    c_specs = [pl.BlockSpec(c.shape, lambda j, i: (0, 0)) for c in consts]
    acc_specs = [pl.BlockSpec(sh, lambda j, i: (0, 0)) for sh in accs]
    sem = ("parallel", "parallel") if rider is None and not n_acc else ("arbitrary", "arbitrary")
    outs = pl.pallas_call(
        body, out_shape=[jax.ShapeDtypeStruct((m, n), d) for d in out_dtypes] + [jax.ShapeDtypeStruct(sh, F32) for sh in accs] + x_out,
        grid=grid, in_specs=a_specs + b_specs + e_specs + c_specs + x_spec, out_specs=[t_spec] * n_out + acc_specs + x_spec,
        scratch_shapes=x_scr, compiler_params=_params(sem),
        name=name)(*[o[0] for o in a_ops], *[o[0] if isinstance(o, tuple) else o for o in b_ops], *[e for e, _ in extras], *consts, *x_in)
    return outs[0] if len(outs) == 1 else outs


def _per_head(fn, heads, n_tiled, n_out):
    def run(*args):
        res = [fn(*[a[:, hd * HEAD_W:(hd + 1) * HEAD_W] for a in args[:n_tiled]], *args[n_tiled:]) for hd in range(heads)]
        tiles = [jnp.concatenate([r[k] for r in res], axis=1) for k in range(n_out)]
        sums = [functools.reduce(lambda u, v: u + v, [r[k] for r in res]) for k in range(n_out, len(res[0]))]
        return (*tiles, *sums)
    return run


def _mm_tn(name, a, b, *, a_pre=None, out_dtype=BF16, out_slots=False, tk=1024, tn=1024, tt=4096):
    t, k = a.shape
    n = b.shape[1]
    tk, tn = min(tk, k), min(tn, n)
    if a.dtype == F32 or b.dtype == F32:
        tt = tt // 2
    if k == tk and n == tn:
        tt = tt // 2
    tt = min(tt, t)
    assert b.shape[0] == t and k % tk == 0 and n % tn == 0 and t % tt == 0
    nt = t // tt

    def body(a_ref, b_ref, o_ref, acc):
        av = a_ref[...] if a_pre is None else a_pre(a_ref[...].astype(F32))
        part = lax.dot_general(av.astype(BF16), b_ref[...].astype(BF16), (((0,), (0,)), ((), ())), preferred_element_type=F32)

        @pl.when(pl.program_id(2) == 0)
        def _():
            acc[...] = part

        @pl.when(pl.program_id(2) != 0)
        def _():
            acc[...] += part

        @pl.when(pl.program_id(2) == nt - 1)
        def _():
            o_ref[...] = acc[...].astype(o_ref.dtype)

    if out_slots:
        out_shape, out_spec = (n // tn, k, tn), pl.BlockSpec((None, tk, tn), lambda ki, ni, ti: (ni, ki, 0))
    else:
        out_shape, out_spec = (k, n), pl.BlockSpec((tk, tn), lambda ki, ni, ti: (ki, ni))
    return pl.pallas_call(
        body, out_shape=jax.ShapeDtypeStruct(out_shape, out_dtype), grid=(k // tk, n // tn, nt),
        in_specs=[pl.BlockSpec((tt, tk), lambda ki, ni, ti: (ti, ki)), pl.BlockSpec((tt, tn), lambda ki, ni, ti: (ti, ni))],
        out_specs=out_spec, scratch_shapes=[pltpu.VMEM((tk, tn), F32)],
        compiler_params=_params(("parallel", "parallel", "arbitrary")), name=name)(a, b)


def _rowwise(name, fn, ins, outs, *, consts=(), pos=(), accs=(), heads=1, tm=512, seq=None, rider=None):
    t = ins[0][0].shape[0]
    tm = min(tm, t if seq is None else seq)
    assert t % tm == 0 and (seq is None or seq % tm == 0)
    n_in, n_pos, n_c, n_out, n_acc = len(ins), len(pos), len(consts), len(outs), len(accs)

    def body(*refs):
        vals = [r[...] for r in refs[:n_in + n_pos + n_c]]
        res = fn(*vals)
        o_refs = refs[n_in + n_pos + n_c:]
        for o_ref, r in zip(o_refs[:n_out], res[:n_out]):
            o_ref[...] = r.astype(o_ref.dtype)
        if n_acc:
            first = jnp.logical_and(pl.program_id(0) == 0, pl.program_id(1) == 0)

            @pl.when(first)
            def _():
                for o_ref, r in zip(o_refs[n_out:], res[n_out:]):
                    o_ref[...] = r

            @pl.when(jnp.logical_not(first))
            def _():
                for o_ref, r in zip(o_refs[n_out:], res[n_out:]):
                    o_ref[...] += r

    def tiled(width, c0, per_head):
        return pl.BlockSpec((tm, width), (lambda h, i: (i, c0 + h)) if per_head else (lambda h, i: (i, c0)))

    in_specs = [tiled(w, c0, ph) for _, w, c0, ph in ins]
    if n_pos:
        nblk = seq // tm
        in_specs += [pl.BlockSpec((tm, a.shape[1]), lambda h, i: (i % nblk, 0)) for a in pos]
    in_specs += [pl.BlockSpec(a.shape, lambda h, i: (0, 0)) for a in consts]
    out_specs = [tiled(w, 0, ph) for _, _, w, ph in outs] + [pl.BlockSpec(s, lambda h, i: (0, 0)) for s in accs]
    out_shape = [jax.ShapeDtypeStruct((t, c), d) for c, d, _, _ in outs] + [jax.ShapeDtypeStruct(s, F32) for s in accs]
    sem = ("arbitrary", "arbitrary") if n_acc or rider is not None else ("parallel", "parallel")
    grid = (heads, t // tm)
    if rider is not None:
        rider.n_core_out, rider.n_core_scratch = n_out + n_acc, 0
    body, x_in, x_spec, x_out, x_scr = _ride(body, grid, rider)
    res = pl.pallas_call(body, out_shape=out_shape + x_out, grid=grid, in_specs=in_specs + x_spec, out_specs=out_specs + x_spec,
                         scratch_shapes=x_scr, compiler_params=_params(sem), name=name)(*[a for a, _, _, _ in ins], *pos, *consts, *x_in)
    return res[0] if len(res) == 1 else res


ATTN_HEADS_PER_STEP = 4


def _attn_fwd(name, q, k, kc0, v, vc0, *, heads, group, nseq, seq, tq=512, rider=None):
    tq = min(tq, seq)
    nq = seq // tq
    hp = ATTN_HEADS_PER_STEP
    grid = (heads // hp, nseq, nq)
    shared = group > 1
    assert group % hp == 0 if shared else (kc0 % hp == 0 and vc0 % hp == 0)

    def body(q_ref, k_ref, v_ref, o_ref, lse_ref):
        for j in range(hp):
            cols = slice(j * HEAD_W, (j + 1) * HEAD_W)
            kj = (k_ref[...] if shared else k_ref[:, cols]).astype(BF16)
            vj = (v_ref[...] if shared else v_ref[:, cols]).astype(BF16)
            s = lax.dot_general(q_ref[:, cols], kj, (((1,), (1,)), ((), ())), preferred_element_type=F32)
            m = jnp.max(s, axis=-1, keepdims=True)
            p = jnp.exp((s - m).astype(BF16))
            vj = jnp.where(lax.broadcasted_iota(jnp.int32, (1, HEAD_W), 1) == HEAD_W - 1, jnp.ones((), BF16), vj)
            o = jnp.dot(p, vj, preferred_element_type=F32)
            l = o[:, HEAD_W - 1:]
            o_ref[:, cols] = (o * (1.0 / l)).astype(o_ref.dtype)
            lse_ref[j] = m + jnp.log(l)

    if rider is not None:
        rider.n_core_out, rider.n_core_scratch = 2, 0
    body, x_in, x_spec, x_out, x_scr = _ride(body, grid, rider)
    q_spec = pl.BlockSpec((tq, hp * HEAD_W), lambda h, b, i: (b * nq + i, h))
    if shared:
        k_spec = pl.BlockSpec((seq, HEAD_W), lambda h, b, i: (b, kc0 + (h * hp) // group))
        v_spec = pl.BlockSpec((seq, HEAD_W), lambda h, b, i: (b, vc0 + (h * hp) // group))
    else:
        k_spec = pl.BlockSpec((seq, hp * HEAD_W), lambda h, b, i: (b, kc0 // hp + h))
        v_spec = pl.BlockSpec((seq, hp * HEAD_W), lambda h, b, i: (b, vc0 // hp + h))
    lse_spec = pl.BlockSpec((hp, tq, 1), lambda h, b, i: (h, b * nq + i, 0))
    sem = ("parallel",) * 3 if rider is None else ("arbitrary",) * 3
    return pl.pallas_call(
        body, out_shape=[jax.ShapeDtypeStruct(q.shape, BF16), jax.ShapeDtypeStruct((heads, q.shape[0], 1), F32)] + x_out,
        grid=grid, in_specs=[q_spec, k_spec, v_spec] + x_spec, out_specs=[q_spec, lse_spec] + x_spec, scratch_shapes=x_scr,
        compiler_params=_params(sem), name=name)(q, k, v, *x_in)


def _attn_bwd(name, q, k, kc0, v, vc0, o, do, lse, *, heads, group, nseq, seq, tq=1024, rider=None):
    tq = min(tq, seq)
    nq = seq // tq
    hk = heads // group
    t = q.shape[0]
    grid = (hk, nseq, group, nq)

    def body(q_ref, k_ref, v_ref, o_ref, do_ref, lse_ref, dq_ref, dk_ref, dv_ref, dk_acc, dv_acc):
        g, i = pl.program_id(2), pl.program_id(3)
        qv, kv, vv, dov = q_ref[...], k_ref[...].astype(BF16), v_ref[...].astype(BF16), do_ref[...]
        s = lax.dot_general(qv, kv, (((1,), (1,)), ((), ())), preferred_element_type=F32)
        pn = jnp.exp(s - lse_ref[...])
        dp = lax.dot_general(dov, vv, (((1,), (1,)), ((), ())), preferred_element_type=F32)
        delta = jnp.sum(dov.astype(F32) * o_ref[...].astype(F32), axis=-1, keepdims=True)
        ds = (pn * (dp - delta)).astype(BF16)
        dq_ref[...] = jnp.dot(ds, kv, preferred_element_type=F32)
        dk_part = lax.dot_general(ds, qv, (((0,), (0,)), ((), ())), preferred_element_type=F32)
        dv_part = lax.dot_general(pn.astype(BF16), dov, (((0,), (0,)), ((), ())), preferred_element_type=F32)
        first = jnp.logical_and(g == 0, i == 0)

        @pl.when(first)
        def _():
            dk_acc[...] = dk_part
            dv_acc[...] = dv_part

        @pl.when(jnp.logical_not(first))
        def _():
            dk_acc[...] += dk_part
            dv_acc[...] += dv_part

        @pl.when(jnp.logical_and(g == group - 1, i == nq - 1))
        def _():
            dk_ref[...] = dk_acc[...].astype(dk_ref.dtype)
            dv_ref[...] = dv_acc[...].astype(dv_ref.dtype)

    if rider is not None:
        rider.n_core_out, rider.n_core_scratch = 3, 2
    body, x_in, x_spec, x_out, x_scr = _ride(body, grid, rider)
    q_spec = pl.BlockSpec((tq, HEAD_W), lambda kh, b, g, i: (b * nq + i, kh * group + g))
    kv_out = pl.BlockSpec((seq, HEAD_W), lambda kh, b, g, i: (b, kh))
    lse_spec = pl.BlockSpec((None, tq, 1), lambda kh, b, g, i: (kh * group + g, b * nq + i, 0))
    sem = ("parallel", "parallel", "arbitrary", "arbitrary") if rider is None else ("arbitrary",) * 4
    return pl.pallas_call(
        body,
        out_shape=[jax.ShapeDtypeStruct(q.shape, F32), jax.ShapeDtypeStruct((t, hk * HEAD_W), BF16),
                   jax.ShapeDtypeStruct((t, hk * HEAD_W), BF16)] + x_out,
        grid=grid,
        in_specs=[q_spec, pl.BlockSpec((seq, HEAD_W), lambda kh, b, g, i: (b, kc0 + kh)),
                  pl.BlockSpec((seq, HEAD_W), lambda kh, b, g, i: (b, vc0 + kh)), q_spec, q_spec, lse_spec] + x_spec,
        out_specs=[q_spec, kv_out, kv_out] + x_spec,
        scratch_shapes=[pltpu.VMEM((seq, HEAD_W), F32), pltpu.VMEM((seq, HEAD_W), F32)] + x_scr,
        compiler_params=_params(sem), name=name)(q, k, v, o, do, lse, *x_in)


def _place():
    return lax.axis_index("x"), lax.axis_index("y"), lax.axis_index("c")


def _other_chips(x, y):
    return [(1 - x, y), (x, 1 - y), (1 - x, 1 - y)]


class _Exchange:
    def __init__(self, kind, srcs):
        assert kind in ("gather", "scatter")
        self.kind, self.srcs = kind, list(srcs)
        n = len(self.srcs)
        self.out_shapes = [jax.ShapeDtypeStruct((4, *a.shape[-2:]), a.dtype) for a in self.srcs]
        self.scratch = [pltpu.SemaphoreType.DMA((3 * n,)), pltpu.SemaphoreType.DMA((3 * n,)), pltpu.SemaphoreType.DMA((n,))]
        self.n_core_out = self.n_core_scratch = 0

    def _copies(self, j, src_ref, out_ref, send_sems, recv_sems, landing):
        x, y, c = _place()

        def remote(k, s, d, to):
            return pltpu.make_async_remote_copy(src_ref=s, dst_ref=d, send_sem=send_sems.at[3 * j + k], recv_sem=recv_sems.at[3 * j + k],
                                                device_id=to, device_id_type=MESH)

        me = 2 * x + y
        part = (lambda i: src_ref) if self.kind == "gather" else (lambda i: src_ref.at[i])
        if landing:
            return [remote(k, part(me), out_ref.at[2 * px + py], (px, py, c)) for k, (px, py) in enumerate(_other_chips(x, y))]
        return [remote(k, part(2 * px + py), out_ref.at[me], (px, py, c)) for k, (px, py) in enumerate(_other_chips(x, y))]

    def _local(self, j, src_ref, out_ref, local_sems):
        x, y, _ = _place()
        me = 2 * x + y
        return pltpu.make_async_copy(src_ref if self.kind == "gather" else src_ref.at[me], out_ref.at[me], local_sems.at[j])

    def start(self, src_refs, out_refs, send_sems, recv_sems, local_sems):
        for j, (src_ref, out_ref) in enumerate(zip(src_refs, out_refs)):
            self._local(j, src_ref, out_ref, local_sems).start()
            for mine in self._copies(j, src_ref, out_ref, send_sems, recv_sems, False):
                mine.start()

    def finish(self, src_refs, out_refs, send_sems, recv_sems, local_sems):
        for j, (src_ref, out_ref) in enumerate(zip(src_refs, out_refs)):
            for landed in self._copies(j, src_ref, out_ref, send_sems, recv_sems, True):
                landed.wait_recv()
        for j, (src_ref, out_ref) in enumerate(zip(src_refs, out_refs)):
            for mine in self._copies(j, src_ref, out_ref, send_sems, recv_sems, False):
                mine.wait_send()
            self._local(j, src_ref, out_ref, local_sems).wait()


def _gather_by_halves(name, srcs):
    n = len(srcs)

    def body(*refs):
        x, y, c = _place()
        me = 2 * x + y
        local_sems = refs[-1]
        copies = []
        for j in range(n):
            src_ref, out_ref, send_sems, recv_sems = refs[j], refs[n + j], refs[2 * n + 2 * j], refs[2 * n + 2 * j + 1]
            half = srcs[j].shape[0] // 2
            rows_c = pl.ds(pl.multiple_of(c * half, half), half)
            rows_s = pl.ds(pl.multiple_of((1 - c) * half, half), half)

            def remote(k, s_ref, d_ref, to, send_sems=send_sems, recv_sems=recv_sems):
                return pltpu.make_async_remote_copy(src_ref=s_ref, dst_ref=d_ref, send_sem=send_sems.at[k], recv_sem=recv_sems.at[k],
                                                    device_id=to, device_id_type=MESH)

            local = pltpu.make_async_copy(src_ref, out_ref.at[me], local_sems.at[j])
            local.start()
            chips = _other_chips(x, y)
            sent = [remote(k, src_ref.at[rows_c], out_ref.at[me, rows_c], (px, py, c)) for k, (px, py) in enumerate(chips)]
            landing = [remote(k, src_ref.at[rows_c], out_ref.at[2 * px + py, rows_c], (px, py, c)) for k, (px, py) in enumerate(chips)]
            passed = [remote(3 + k, out_ref.at[2 * px + py, rows_c], out_ref.at[2 * px + py, rows_c], (x, y, 1 - c))
                      for k, (px, py) in enumerate(chips)]
            from_sibling = [remote(3 + k, out_ref.at[2 * px + py, rows_s], out_ref.at[2 * px + py, rows_s], (x, y, 1 - c))
                            for k, (px, py) in enumerate(chips)]
            for cp in sent:
                cp.start()
            copies.append((local, sent, landing, passed, from_sibling))
        for local, sent, landing, passed, from_sibling in copies:
            for k in range(3):
                landing[k].wait_recv()
                passed[k].start()
        for local, sent, landing, passed, from_sibling in copies:
            for k in range(3):
                from_sibling[k].wait_recv()
            for cp in sent + passed:
                cp.wait_send()
            local.wait()

    sems = [pltpu.SemaphoreType.DMA((6,)) for _ in range(2 * n)] + [pltpu.SemaphoreType.DMA((n,))]
    return pl.pallas_call(
        body, out_shape=[jax.ShapeDtypeStruct((4, *a.shape), a.dtype) for a in srcs],
        in_specs=[pl.BlockSpec(memory_space=pl.ANY)] * n, out_specs=[pl.BlockSpec(memory_space=pltpu.VMEM)] * n,
        scratch_shapes=sems, compiler_params=pltpu.CompilerParams(vmem_limit_bytes=VMEM_LIMIT_BYTES), name=name)(*srcs)


def _adamw(w, g, m, v):
    m = ADAM_B1 * m + (1.0 - ADAM_B1) * g
    v = ADAM_B2 * v + (1.0 - ADAM_B2) * (g * g)
    delta = -ADAM_LR * ((m / M_HAT_DIV) / (jnp.sqrt(v / V_HAT_DIV) + ADAM_EPS) + ADAM_WD * w)
    return delta, m, v


def _small_allreduce_adamw(part, w, m, v):
    def body(part_ref, w_ref, m_ref, v_ref, g_out, d_out, m_out, v_out, loss_out, buf, send_sems, recv_sems):
        x, y, c = _place()
        me = 4 * x + 2 * y + c
        buf[me] = part_ref[...]

        def flip(k):
            fx, fy, fc = (k >> 2) & 1, (k >> 1) & 1, k & 1
            px, py, pc = (1 - x if fx else x), (1 - y if fy else y), (1 - c if fc else c)
            return (px, py, pc), 4 * px + 2 * py + pc

        def copy(k, slot):
            return pltpu.make_async_remote_copy(
                src_ref=part_ref, dst_ref=buf.at[slot], send_sem=send_sems.at[k - 1], recv_sem=recv_sems.at[k - 1],
                device_id=flip(k)[0], device_id_type=MESH)

        sent = [copy(k, me) for k in range(1, 8)]
        for cp in sent:
            cp.start()
        for k in range(1, 8):
            copy(k, flip(k)[1]).wait_recv()
        for cp in sent:
            cp.wait_send()
        tot = buf[0]
        for j in range(1, 8):
            tot = tot + buf[j]
        delta, m_new, v_new = _adamw(w_ref[...], tot, m_ref[...], v_ref[...])
        g_out[...] = tot
        d_out[...] = delta
        m_out[...] = m_new
        v_out[...] = v_new
        loss_out[...] = jnp.sum(tot[LOSS_ROW0:LOSS_ROW0 + 8, :]).reshape(1, 1)

    vm = pl.BlockSpec(memory_space=pltpu.VMEM)
    shp = jax.ShapeDtypeStruct((SMALL_ROWS, 128), F32)
    return pl.pallas_call(
        body, out_shape=[shp, shp, shp, shp, jax.ShapeDtypeStruct((1, 1), F32)],
        in_specs=[vm, vm, vm, vm], out_specs=[vm, vm, vm, vm, vm],
        scratch_shapes=[pltpu.VMEM((8, SMALL_ROWS, 128), F32), pltpu.SemaphoreType.DMA((7,)), pltpu.SemaphoreType.DMA((7,))],
        name="small_allreduce_adamw")(part, w, m, v)


def _row_tile(rows, cap):
    return max(t for t in range(16, min(rows, cap) + 1, 16) if rows % t == 0)


def _reduce_pair(name, parts):
    _, rows, w = parts.shape
    tr = _row_tile(rows, 576)
    nt = rows // tr

    def body(p_ref, o_ref, mine, theirs, send_sems, recv_sems):
        i = pl.program_id(0)
        x, y, c = _place()

        def copy(t):
            rows_t = pl.ds(pl.multiple_of(t * tr, tr), tr)
            return pltpu.make_async_remote_copy(src_ref=mine.at[rows_t], dst_ref=theirs.at[rows_t], send_sem=send_sems.at[t],
                                                recv_sem=recv_sems.at[t], device_id=(x, y, 1 - c), device_id_type=MESH)

        @pl.when(i < nt)
        def _():
            mine[pl.ds(pl.multiple_of(i * tr, tr), tr), :] = (
                (p_ref[0].astype(F32) + p_ref[1].astype(F32)) + p_ref[2].astype(F32)) + p_ref[3].astype(F32)
            copy(i).start()

        @pl.when(i >= nt)
        def _():
            t = i - nt
            copy(t).wait()
            rows_t = pl.ds(pl.multiple_of(t * tr, tr), tr)
            o_ref[...] = mine[rows_t, :] + theirs[rows_t, :]

    return pl.pallas_call(
        body, out_shape=jax.ShapeDtypeStruct((rows, w), F32), grid=(2 * nt,),
        in_specs=[pl.BlockSpec((4, tr, w), lambda i: (0, jnp.minimum(i, nt - 1), 0))],
        out_specs=pl.BlockSpec((tr, w), lambda i: (jnp.maximum(i - nt, 0), 0)),
        scratch_shapes=[pltpu.VMEM((rows, w), F32), pltpu.VMEM((rows, w), F32), pltpu.SemaphoreType.DMA((nt,)),
                        pltpu.SemaphoreType.DMA((nt,))],
        compiler_params=_params(("arbitrary",)), name=name)(parts)


def _presum_halves(name, shards):
    _, rows, w = shards.shape
    half = rows // 2

    def body(s_ref, o_ref, theirs, send_sems, recv_sems):
        x, y, c = _place()
        rows_c = pl.ds(pl.multiple_of(c * half, half), half)
        rows_s = pl.ds(pl.multiple_of((1 - c) * half, half), half)
        sent = [pltpu.make_async_remote_copy(src_ref=s_ref.at[j, rows_s], dst_ref=theirs.at[j], send_sem=send_sems.at[j],
                                             recv_sem=recv_sems.at[j], device_id=(x, y, 1 - c), device_id_type=MESH) for j in range(4)]
        for cp in sent:
            cp.start()
        for j, cp in enumerate(sent):
            cp.wait_recv()
            o_ref[j] = (s_ref[j, rows_c, :].astype(F32) + theirs[j].astype(F32)).astype(o_ref.dtype)
        for cp in sent:
            cp.wait_send()

    vm = pl.BlockSpec(memory_space=pltpu.VMEM)
    return pl.pallas_call(
        body, out_shape=jax.ShapeDtypeStruct((4, half, w), shards.dtype), in_specs=[vm], out_specs=vm,
        scratch_shapes=[pltpu.VMEM((4, half, w), shards.dtype), pltpu.SemaphoreType.DMA((4,)), pltpu.SemaphoreType.DMA((4,))],
        compiler_params=pltpu.CompilerParams(vmem_limit_bytes=VMEM_LIMIT_BYTES), name=name)(shards)


def _reduce_halves(name, parts):
    _, half, w = parts.shape

    def body(p_ref, o_ref, mine, send_sem, recv_sem):
        x, y, c = _place()
        rows_c = pl.ds(pl.multiple_of(c * half, half), half)
        rows_s = pl.ds(pl.multiple_of((1 - c) * half, half), half)
        mine[...] = ((p_ref[0].astype(F32) + p_ref[1].astype(F32)) + p_ref[2].astype(F32)) + p_ref[3].astype(F32)
        send = pltpu.make_async_remote_copy(src_ref=mine, dst_ref=o_ref.at[rows_c], send_sem=send_sem, recv_sem=recv_sem,
                                            device_id=(x, y, 1 - c), device_id_type=MESH)
        send.start()
        o_ref[rows_c, :] = mine[...]
        pltpu.make_async_remote_copy(src_ref=mine, dst_ref=o_ref.at[rows_s], send_sem=send_sem, recv_sem=recv_sem,
                                     device_id=(x, y, 1 - c), device_id_type=MESH).wait_recv()
        send.wait_send()

    vm = pl.BlockSpec(memory_space=pltpu.VMEM)
    return pl.pallas_call(
        body, out_shape=jax.ShapeDtypeStruct((2 * half, w), F32), in_specs=[vm], out_specs=vm,
        scratch_shapes=[pltpu.VMEM((half, w), F32), pltpu.SemaphoreType.DMA(()), pltpu.SemaphoreType.DMA(())],
        compiler_params=pltpu.CompilerParams(vmem_limit_bytes=VMEM_LIMIT_BYTES), name=name)(parts)


def _adamw_shard(name, g, w, m, v):
    _, rows, cols = w.shape
    tr = _row_tile(rows, 256)

    def body(g_ref, w_ref, m_ref, v_ref, g_out, d_out, m_out, v_out):
        gv = g_ref[...]
        delta, m_new, v_new = _adamw(w_ref[...], gv, m_ref[...], v_ref[...])
        g_out[...] = gv
        d_out[...] = delta
        m_out[...] = m_new
        v_out[...] = v_new

    t_spec = pl.BlockSpec((None, tr, cols), lambda i: (0, i, 0))
    shp = jax.ShapeDtypeStruct((1, rows, cols), F32)
    return pl.pallas_call(body, out_shape=[shp] * 4, grid=(rows // tr,), in_specs=[pl.BlockSpec((tr, cols), lambda i: (i, 0))] + [t_spec] * 3,
                          out_specs=[t_spec] * 4, compiler_params=_params(("parallel",)), name=name)(g, w, m, v)


def _shard_shape(name):
    _, r, c, ax = BIG_BY_NAME[name]
    return (r, c // 4) if ax == 1 else (r // 4, c)


def _pad_rows(a, axis):
    pad = [(0, 0)] * a.ndim
    pad[axis] = (0, -a.shape[axis] % PACK_ALIGN)
    return jnp.pad(a, pad)


def _pack_shards(names, shards, dtype):
    return _pad_rows(jnp.concatenate([s.astype(dtype).reshape(-1, PACK_W) for s in shards], axis=0), 0)


def _unpack_shards(names, slab):
    out, off = [], 0
    for name in names:
        rs, cs = _shard_shape(name)
        n = rs * cs // PACK_W
        out.append(slab[off:off + n].reshape(rs, cs))
        off += n
    return out


def _unpack_full(names, slabs):
    out, off = [], 0
    for name in names:
        _, r, c, ax = BIG_BY_NAME[name]
        n = r * c // 4 // PACK_W
        seg = slabs[:, off:off + n]
        out.append(seg.reshape(4, r, c // 4).transpose(1, 0, 2).reshape(r, c) if ax == 1 else seg.reshape(r, c))
        off += n
    return out


def _pack_full(names, mats, dtype):
    segs = []
    for name, a in zip(names, mats):
        _, r, c, ax = BIG_BY_NAME[name]
        a = a.astype(dtype)
        a = a.reshape(r, 4, c // 4).transpose(1, 0, 2) if ax == 1 else a
        segs.append(a.reshape(4, -1, PACK_W))
    return _pad_rows(jnp.concatenate(segs, axis=1), 1)


def _pad_heads_cols(wm, heads, d):
    k = wm.shape[0]
    return jnp.pad(wm.reshape(k, heads, d), ((0, 0), (0, 0), (0, HEAD_W - d))).reshape(k, heads * HEAD_W)


def _unpad_heads_cols(wm, heads, d):
    k = wm.shape[0]
    return wm.reshape(k, heads, HEAD_W)[:, :, :d].reshape(k, heads * d)


def _win_ext(w_in):
    o = np.cumsum([0, Q_LORA, KV_LORA, QK_ROPE, H_B * HD_B, KV_B * HD_B, KV_B * HD_B, D_MODEL, D_MODEL])
    pc = lambda a, n: jnp.pad(a, ((0, 0), (0, n - a.shape[1])))
    return jnp.concatenate([
        _pad_heads_cols(w_in[:, o[3]:o[4]], H_B, HD_B), w_in[:, o[0]:o[1]], w_in[:, o[1]:o[2]], pc(w_in[:, o[2]:o[3]], HEAD_W),
        _pad_heads_cols(w_in[:, o[4]:o[5]], KV_B, HD_B), _pad_heads_cols(w_in[:, o[5]:o[6]], KV_B, HD_B),
        w_in[:, o[6]:o[7]], w_in[:, o[7]:o[8]]], axis=1)


def _win_unext(blocks):
    c = HEAD_W
    qb, mid, ga, gb = blocks
    at = lambda zc: (zc - ZC_QLAT) * c
    return jnp.concatenate([
        mid[:, at(ZC_QLAT):at(ZC_CKV)], mid[:, at(ZC_CKV):at(ZC_KPE)], mid[:, at(ZC_KPE):at(ZC_KPE) + QK_ROPE],
        _unpad_heads_cols(qb, H_B, HD_B), _unpad_heads_cols(mid[:, at(ZC_KB):at(ZC_VB)], KV_B, HD_B),
        _unpad_heads_cols(mid[:, at(ZC_VB):at(ZC_GA)], KV_B, HD_B), ga, gb], axis=1)


def _wkv_ext(w_kvb):
    wk = w_kvb.reshape(KV_LORA, H_A, QK_NOPE + V_DIM_A)
    k_cols = jnp.pad(wk[:, :, :QK_NOPE], ((0, 0), (0, 0), (0, HEAD_W - QK_NOPE))).reshape(KV_LORA, H_A * HEAD_W)
    v_cols = jnp.pad(wk[:, :, QK_NOPE:], ((0, 0), (0, 0), (0, HEAD_W - V_DIM_A))).reshape(KV_LORA, H_A * HEAD_W)
    eye = jnp.pad(jnp.eye(QK_ROPE, dtype=w_kvb.dtype), ((0, 0), (QK_NOPE, HEAD_W - QK_NOPE - QK_ROPE)))
    pe_rows = jnp.concatenate([jnp.tile(eye, (1, H_A)), jnp.zeros((QK_ROPE, H_A * HEAD_W), w_kvb.dtype)], axis=1)
    top = jnp.concatenate([k_cols, v_cols], axis=1)
    return jnp.concatenate([top, pe_rows, jnp.zeros((2 * HEAD_W - KV_LORA - QK_ROPE, 2 * H_A * HEAD_W), w_kvb.dtype)], axis=0)


def _wkv_unext(k_block, v_block):
    k_cols = k_block[:KV_LORA].reshape(KV_LORA, H_A, HEAD_W)[:, :, :QK_NOPE]
    v_cols = v_block[:KV_LORA].reshape(KV_LORA, H_A, HEAD_W)[:, :, :V_DIM_A]
    return jnp.concatenate([k_cols, v_cols], axis=2).reshape(KV_LORA, H_A * (QK_NOPE + V_DIM_A))


def _pad_heads_rows(wm, heads, d):
    n = wm.shape[1]
    return jnp.pad(wm.reshape(heads, d, n), ((0, 0), (0, HEAD_W - d), (0, 0))).reshape(heads * HEAD_W, n)


def _unpad_heads_rows(wm, heads, d):
    n = wm.shape[1]
    return wm.reshape(heads, HEAD_W, n)[:, :d].reshape(heads * d, n)


def _rope_tables(seq):
    def ang(pos, dim):
        inv = np.float32(ROPE_THETA) ** (-np.arange(0, dim, 2, dtype=np.float32) / np.float32(dim))
        return pos.astype(np.float32)[:, None] * inv[None, :]

    def rot(dim):
        r = np.zeros((dim, dim), np.float32)
        half = dim // 2
        r[np.arange(half) + half, np.arange(half)] = -1.0
        r[np.arange(half), np.arange(half) + half] = 1.0
        return r

    def table(blocks):
        cos, sin = np.ones((seq, HEAD_W), np.float32), np.zeros((seq, HEAD_W), np.float32)
        pm = np.zeros((HEAD_W, HEAD_W), np.float32)
        for c0, a in blocks:
            d = 2 * a.shape[1]
            cos[:, c0:c0 + d] = np.concatenate([np.cos(a), np.cos(a)], axis=1)
            sin[:, c0:c0 + d] = np.concatenate([np.sin(a), np.sin(a)], axis=1)
            pm[c0:c0 + d, c0:c0 + d] = rot(d)
        return jnp.asarray(cos), jnp.asarray(sin), jnp.asarray(pm, BF16), jnp.asarray(pm.T, BF16)

    tok = np.arange(seq)
    a1 = ang(tok, QK_ROPE)
    arow, acol = ang(tok // GRID_W, HD_B // 2), ang(tok % GRID_W, HD_B // 2)
    return table([(QK_NOPE, a1)]), table([(0, a1)]), table([(0, arow), (HD_B // 2, acol)])


def _local_step(x, p, tgt, gains, wts, ride=None):
    nb, seq, _ = x.shape
    t = nb * seq
    x0 = x.reshape(t, D_MODEL)
    p2 = p.reshape(t, PLE_DIM)
    tg = tgt.reshape(t, D_MODEL)
    (cq_t, sq_t, pq, pq_t), (ck_t, sk_t, pk, pk_t), (cb_t, sb_t, pb, pb_t) = _rope_tables(seq)
    padg = lambda g: jnp.pad(g, ((0, 0), (0, HEAD_W - g.shape[1])))
    g_qn, g_kn = padg(gains["g_qn"]), padg(gains["g_kn"])

    win = _win_ext(wts["w_in"])
    wqb = _pad_heads_cols(wts["w_qb"], H_A, QK_NOPE + QK_ROPE)
    wkv = _wkv_ext(wts["w_kvb"])

    norm = lambda n: (lambda v, g: (_rms(v, g, n),))
    full = lambda a: (a, a.shape[1], 0, False)
    wts = dict(wts)
    rider_of = lambda kernel_name: None if ride is None else ride["gather"][kernel_name][0]

    def landed(kernel_name, got):
        if ride is not None:
            wts.update(ride["gather"][kernel_name][1](got))

    h = _rowwise("norm_mix", norm(D_MODEL), [full(x0)], [(D_MODEL, BF16, D_MODEL, False)], consts=[gains["g_mix"]])
    res = _mm("in_proj", h, win, tn=2048, rider=rider_of("in_proj"))
    z, got = (res, []) if ride is None else (res[0], res[1:])
    landed("in_proj", got)

    def rope_fwd(scale):
        return lambda v, cos, sin, pm: ((v * cos + _perm(v, pm) * sin) * scale,)

    heads_tile = lambda n: (n * HEAD_W, BF16, n * HEAD_W, False)
    z_qlat, z_ckv, z_kpe = (z, Q_LORA, ZC_QLAT // 2, False), (z, HEAD_W, ZC_CKV, False), (z, HEAD_W, ZC_KPE, False)

    def q_path(zq, cos, sin, g, w, pm):
        cqv = _rms(zq, g, Q_LORA).astype(BF16)
        qa = jnp.dot(cqv, w[...], preferred_element_type=F32)
        return (cqv, *_per_head(rope_fwd(SCALE_A), H_A, 1, 1)(qa, cos, sin, pm))

    cq, q_a = _rowwise("q_path", q_path, [z_qlat], [(Q_LORA, BF16, Q_LORA, False), heads_tile(H_A)],
                       pos=[cq_t, sq_t], consts=[gains["g_qa"], wqb, pq], seq=seq)

    def kv_path(ckv_raw, kpe_raw, cos, sin, g, w, pm):
        kinv = jnp.concatenate([_rms(ckv_raw, g, KV_LORA), *rope_fwd(1.0)(kpe_raw, cos, sin, pm)], axis=1).astype(BF16)
        return kinv, jnp.dot(kinv, w[...], preferred_element_type=F32)

    kin, kv_a = _rowwise("kv_path", kv_path, [z_ckv, z_kpe], [heads_tile(2), heads_tile(2 * H_A)],
                         pos=[ck_t, sk_t], consts=[gains["g_kva"], wkv, pk], seq=seq)
    o_a, lse_a, *got = _attn_fwd("attn_a_fwd", q_a, kv_a, 0, kv_a, H_A, heads=H_A, group=1, nseq=nb, seq=seq,
                                 rider=rider_of("attn_a_fwd"))
    landed("attn_a_fwd", got)

    def prep_fwd(scale):
        def fn(v, cos, sin, g, pm):
            yv = _rms(v, g, HD_B)
            return ((yv * cos + _perm(yv, pm) * sin) * scale,)
        return fn

    z_qb, z_kb = (z, H_B * HEAD_W, ZC_QB // H_B, False), (z, KV_B * HEAD_W, ZC_KB // KV_B, False)
    res = _rowwise("prep_qb", _per_head(prep_fwd(SCALE_B), H_B, 1, 1), [z_qb], [heads_tile(H_B)],
                   pos=[cb_t, sb_t], consts=[g_qn, pb], seq=seq, rider=rider_of("prep_qb"))
    q_b, got = (res, []) if ride is None else (res[0], res[1:])
    landed("prep_qb", got)
    k_b = _rowwise("prep_kb", _per_head(prep_fwd(1.0), KV_B, 1, 1), [z_kb], [heads_tile(KV_B)],
                   pos=[cb_t, sb_t], consts=[g_kn, pb], seq=seq)
    o_b, lse_b, *got = _attn_fwd("attn_b_fwd", q_b, k_b, 0, z, ZC_VB, heads=H_B, group=H_B // KV_B, nseq=nb, seq=seq,
                                 rider=rider_of("attn_b_fwd"))
    landed("attn_b_fwd", got)
    woa = _pad_heads_rows(wts["w_oa"], H_A, V_DIM_A)
    wob = _pad_heads_rows(wts["w_ob"], H_B, HD_B)
    wo, wup, wdown = wts["w_o"], wts["w_up"], wts["w_down"]

    def residual_norm(acc, r, g):
        xv = r + acc
        return xv, _rms(xv, g, D_MODEL)

    def mix_out(oa, ob, ga, gb, r, g, w_a, w_b, w_out):
        a = jnp.dot(oa, w_a[...], preferred_element_type=F32)
        b = jnp.dot(ob, w_b[...], preferred_element_type=F32)
        mg = (_sigmoid(ga) * a + _sigmoid(gb) * b).astype(BF16)
        return (a, b, mg, *residual_norm(jnp.dot(mg, w_out[...], preferred_element_type=F32), r, g))

    z_ga, z_gb = (z, D_MODEL, ZC_GA // 8, False), (z, D_MODEL, ZC_GB // 8, False)
    wide = lambda d: (D_MODEL, d, D_MODEL, False)
    ya, yb, merged, x1, h2, *got = _rowwise("mix_out", mix_out, [full(o_a), full(o_b), z_ga, z_gb, full(x0)],
                                            [wide(BF16), wide(BF16), wide(BF16), wide(F32), wide(BF16)],
                                            consts=[gains["g_mlp"], woa, wob, wo], tm=256, rider=rider_of("mix_out"))
    landed("mix_out", got)
    wpg, wple = wts["w_ple_gate"], wts["w_ple"]

    square = lambda v: v * v
    u = _mm("mlp_up", h2, wup, b_slots=True, out_dtypes=(BF16,), epi=lambda acc: (jnp.maximum(acc, 0.0),), tm=1024)
    x2, h3 = _mm("mlp_down", u, wdown, a_pre=square, out_dtypes=(F32, BF16), epi=residual_norm, extras=(x1,), consts=[gains["g_ple"]])

    def norm_res_bwd(dh, v, res, g):
        dx, dg = _rms_bwd(dh, v, g, D_MODEL)
        return dx + res, dg

    def tail(x2v, h3v, pv, tv, gf, gp, w_gate, w_emb):
        sg = _sigmoid(jnp.dot(h3v, w_gate[...], preferred_element_type=F32))
        pev = jnp.dot(pv.astype(BF16), w_emb[...], preferred_element_type=F32)
        x3 = x2v + sg * pev
        rs = lax.rsqrt(jnp.sum(x3 * x3, axis=-1, keepdims=True) * (1.0 / D_MODEL) + EPS)
        xh = x3 * rs
        err = xh * gf - tv
        dy = err * (1.0 / D_MODEL)
        dyg = dy * gf
        dx3 = rs * (dyg - xh * (jnp.sum(dyg * xh, axis=-1, keepdims=True) * (1.0 / D_MODEL)))
        dgp = (dx3 * pev * sg * (1.0 - sg)).astype(BF16)
        dh3 = lax.dot_general(dgp, w_gate[...], (((1,), (1,)), ((), ())), preferred_element_type=F32)
        dx2v, dgple = norm_res_bwd(dh3, x2v, dx3, gp)
        return (dx2v, dgp, dx3 * sg, jnp.sum(err * err, axis=0, keepdims=True) * (0.5 / D_MODEL),
                jnp.sum(dy * xh, axis=0, keepdims=True), dgple)

    dx2, dgpre, dpe, loss_part, dg_final, dg_ple = _rowwise(
        "tail", tail, [full(x2), full(h3), full(p2), full(tg)], [wide(F32), wide(BF16), wide(BF16)],
        consts=[gains["g_final"].reshape(1, D_MODEL), gains["g_ple"], wpg, wple], accs=[(1, D_MODEL)] * 3, tm=256)

    dw = {}
    dw["w_ple"] = _mm_tn("dw_ple", p2, dpe)
    dw["w_ple_gate"] = _mm_tn("dw_ple_gate", h3, dgpre)
    dw["w_down"] = _mm_tn("dw_down", u, dx2, a_pre=square)
    dupre = _mm("d_mlp_down", dx2, wdown, trans_b=True, out_dtypes=(BF16,), epi=lambda acc, uv: (acc * (2.0 * uv.astype(F32)),),
                extras=(u,), tn=2048)
    dw["w_up"] = _mm_tn("dw_up", h2, dupre, out_slots=True)
    n_up = wup.shape[0]
    dx1, dg_mlp = _mm("d_mlp_up", [(dupre, j, wup.shape[2]) for j in range(n_up)], [(wup, j) for j in range(n_up)], trans_b=True,
                      epi=norm_res_bwd, extras=(x1, dx2), consts=[gains["g_mlp"]],
                      accs=[(1, D_MODEL)], tm=256)
    dw["w_o"] = _mm_tn("dw_o", merged, dx1)

    def merge_bwd(dm, ga, gb, a, b, w_a, w_b):
        sa, sb = _sigmoid(ga), _sigmoid(gb)
        da, db = (dm * sa).astype(BF16), (dm * sb).astype(BF16)
        nt = (((1,), (1,)), ((), ()))
        return (da, db, dm * a * sa * (1.0 - sa), dm * b * sb * (1.0 - sb),
                lax.dot_general(da, w_a, nt, preferred_element_type=F32), lax.dot_general(db, w_b, nt, preferred_element_type=F32))

    dya, dyb, dga, dgb, do_a, do_b = _mm("d_out_proj", dx1, wo, trans_b=True, out_dtypes=(BF16,) * 6, epi=merge_bwd,
                                         extras=((z, ZC_GA // 8), (z, ZC_GB // 8), ya, yb), consts=[woa, wob], tm=256)
    dw["w_oa"] = _unpad_heads_rows(_mm_tn("dw_oa", o_a, dya), H_A, V_DIM_A)
    dw["w_ob"] = _unpad_heads_rows(_mm_tn("dw_ob", o_b, dyb), H_B, HD_B)
    res_a = _attn_bwd("attn_a_bwd", q_a, kv_a, 0, kv_a, H_A, o_a, do_a, lse_a, heads=H_A, group=1, nseq=nb, seq=seq,
                      rider=ride and ride["scatter_a"](dw))
    dq_a, dk_a, dv_a = res_a[:3]
    if ride is not None:
        ride["out"]["parts_a"] = res_a[3:]

    def rope_bwd(scale):
        return lambda d, cos, sin, pm_t: ((d * cos + _perm(d * sin, pm_t)) * scale,)

    nt_dims = (((1,), (1,)), ((), ()))

    def q_path_bwd(dq, zq, cos, sin, g, w, pm_t):
        dqav = _per_head(rope_bwd(SCALE_A), H_A, 1, 1)(dq, cos, sin, pm_t)[0].astype(BF16)
        dcq = lax.dot_general(dqav, w[...], nt_dims, preferred_element_type=F32)
        return (dqav, *_rms_bwd(dcq, zq, g, Q_LORA))

    dqa, dq_lat, dg_qa = _rowwise("q_path_bwd", q_path_bwd, [full(dq_a), z_qlat], [heads_tile(H_A), (Q_LORA, BF16, Q_LORA, False)],
                                  pos=[cq_t, sq_t], consts=[gains["g_qa"], wqb, pq_t], accs=[(1, Q_LORA)], seq=seq)
    dw["w_qb"] = _unpad_heads_cols(_mm_tn("dw_qb", cq, dqa), H_A, QK_NOPE + QK_ROPE)
    dw["w_kvb"] = _wkv_unext(_mm_tn("dw_kv_k", kin, dk_a), _mm_tn("dw_kv_v", kin, dv_a))
    dq_b, dk_b, dv_b, *parts_b = _attn_bwd("attn_b_bwd", q_b, k_b, 0, z, ZC_VB, o_b, do_b, lse_b, heads=H_B, group=H_B // KV_B,
                                               nseq=nb, seq=seq, rider=ride and ride["scatter_b"](dw))
    if ride is not None:
        ride["out"]["parts_b"] = parts_b

    def kv_path_bwd(dk, dv, ckv_raw, cos, sin, g, w, pm_t):
        kv_w = H_A * HEAD_W
        wv = w[...]
        dkin = (lax.dot_general(dk, wv[:, :kv_w], nt_dims, preferred_element_type=F32)
                + lax.dot_general(dv, wv[:, kv_w:], nt_dims, preferred_element_type=F32))
        dckv_raw, dg = _rms_bwd(dkin[:, :HEAD_W], ckv_raw, g, KV_LORA)
        return (dckv_raw, *rope_bwd(1.0)(dkin[:, HEAD_W:], cos, sin, pm_t), dg)

    dckv, dkpe, dg_kva = _rowwise("kv_path_bwd", kv_path_bwd, [full(dk_a), full(dv_a), z_ckv], [heads_tile(1), heads_tile(1)],
                                  pos=[ck_t, sk_t], consts=[gains["g_kva"], wkv, pk_t], accs=[(1, KV_LORA)], seq=seq)

    def prep_bwd(scale):
        def fn(d, v, cos, sin, g, pm_t):
            dyv = (d * cos + _perm(d * sin, pm_t)) * scale
            return _rms_bwd(dyv, v, g, HD_B)
        return fn

    dqb, dg_qn = _rowwise("prep_qb_bwd", _per_head(prep_bwd(SCALE_B), H_B, 2, 1), [full(dq_b), z_qb], [heads_tile(H_B)],
                          pos=[cb_t, sb_t], consts=[g_qn, pb_t], accs=[(1, HEAD_W)], seq=seq)
    dkb, dg_kn = _rowwise("prep_kb_bwd", _per_head(prep_bwd(1.0), KV_B, 2, 1), [full(dk_b), z_kb], [heads_tile(KV_B)],
                          pos=[cb_t, sb_t], consts=[g_kn, pb_t], accs=[(1, HEAD_W)], seq=seq)

    dz = [dqb, jnp.concatenate([dq_lat, dckv, dkpe, dkb, dv_b], axis=1), dga, dgb]
    dw["w_in"] = _win_unext([_mm_tn("dw_in_%d" % j, h, blk) for j, blk in enumerate(dz)])
    dx0, dg_mix, *parts_in = _mm("d_in_proj", dz, [(win, j, D_MODEL) for j in range(4)], trans_b=True, epi=norm_res_bwd, extras=(x0, dx1), consts=[gains["g_mix"]],
                                 accs=[(1, D_MODEL)], tm=256, rider=ride and ride["scatter_in"](dw))
    if ride is not None:
        ride["out"]["parts_in"] = parts_in

    dg = {"g_mix": dg_mix, "g_qa": dg_qa, "g_kva": dg_kva, "g_qn": dg_qn[:, :HD_B], "g_kn": dg_kn[:, :HD_B],
          "g_mlp": dg_mlp, "g_ple": dg_ple, "g_final": dg_final}
    return loss_part, dx0.reshape(nb, seq, D_MODEL), dg, dw


def _pack_small(vals, loss_part=None):
    flat = jnp.concatenate([vals[n].reshape(1, -1) for n, _ in SMALL], axis=1)
    loss = jnp.zeros((1, 8 * 128), F32) if loss_part is None else loss_part
    gap = jnp.zeros((1, LOSS_ROW0 * 128 - SMALL_N), F32)
    return jnp.concatenate([flat, gap, loss], axis=1).reshape(SMALL_ROWS, 128)


def _unpack_small(slab, like):
    flat, out, off = slab.reshape(-1), {}, 0
    for n, k in SMALL:
        out[n] = flat[off:off + k].reshape(like[n].shape)
        off += k
    return out


def kernel(x, p, g_mix, w_in, g_qa, w_qb, g_kva, w_kvb, g_qn, g_kn, w_oa, w_ob, w_o, g_mlp, w_up, w_down, g_ple, w_ple_gate, w_ple, g_final, loss_target, m_g_mix, m_w_in, m_g_qa, m_w_qb, m_g_kva, m_w_kvb, m_g_qn, m_g_kn, m_w_oa, m_w_ob, m_w_o, m_g_mlp, m_w_up, m_w_down, m_g_ple, m_w_ple_gate, m_w_ple, m_g_final, v_g_mix, v_w_in, v_g_qa, v_w_qb, v_g_kva, v_w_kvb, v_g_qn, v_g_kn, v_w_oa, v_w_ob, v_w_o, v_g_mlp, v_w_up, v_w_down, v_g_ple, v_w_ple_gate, v_w_ple, v_g_final):
    given = dict(locals())
    order = ["g_mix", "w_in", "g_qa", "w_qb", "g_kva", "w_kvb", "g_qn", "g_kn", "w_oa", "w_ob", "w_o", "g_mlp", "w_up",
             "w_down", "g_ple", "w_ple_gate", "w_ple", "g_final"]
    big_names = [n for n, _, _, _ in BIG]
    local = lambda prefix, names: [given[prefix + n][0] for n in names]
    slab = lambda names: _pack_shards(names, local("", names), BF16)
    bf = lambda n: given[n][0].astype(BF16)
    cols_full = lambda g: g.transpose(1, 0, 2).reshape(g.shape[1], -1)
    rows_full = lambda g: g.reshape(-1, g.shape[2])
    shards_cols = lambda a: a.reshape(a.shape[0], 4, a.shape[1] // 4).transpose(1, 0, 2)
    shards_rows = lambda a: a.reshape(4, a.shape[0] // 4, a.shape[1])
    packed = lambda names, dw: _pack_full(names, [dw[n] for n in names], BF16)
    branch_out = ["w_oa", "w_ob"]
    back_a, back_b = SLAB_LATE + ["w_o"], SLAB_EARLY + ["w_ple_gate"]

    got_in, got_early = _gather_by_halves("weight_gather_early", [bf("w_in"), slab(SLAB_EARLY)])
    wts = {"w_in": cols_full(got_in), **dict(zip(SLAB_EARLY, _unpack_full(SLAB_EARLY, got_early)))}
    gains = {n: given[n].reshape(1, -1) for n, _ in SMALL}
    ride = {
        "gather": {
            "in_proj": (_Exchange("gather", [bf("w_o")]), lambda got: {"w_o": rows_full(got[0])}),
            "prep_qb": (_Exchange("gather", [slab(branch_out)]), lambda got: dict(zip(branch_out, _unpack_full(branch_out, got[0])))),
            "attn_a_fwd": (_Exchange("gather", [bf("w_up")]), lambda got: {"w_up": got[0]}),
            "attn_b_fwd": (_Exchange("gather", [bf("w_down")]), lambda got: {"w_down": rows_full(got[0])}),
            "mix_out": (_Exchange("gather", [bf("w_ple_gate"), bf("w_ple")]),
                        lambda got: {"w_ple_gate": rows_full(got[0]), "w_ple": cols_full(got[1])}),
        },
        "scatter_a": lambda dw: _Exchange("scatter", [dw["w_up"], packed(back_a, dw)]),
        "scatter_b": lambda dw: _Exchange("scatter", [shards_rows(dw["w_down"]), packed(back_b, dw)]),
        "scatter_in": lambda dw: _Exchange("scatter", [_presum_halves("grad_presum_in", shards_cols(dw["w_in"]))]),
        "out": {},
    }
    loss_part, grad_x, dg, dw = _local_step(x, p[0], loss_target, gains, wts, ride)

    small = lambda prefix: _pack_small({n: given[prefix + n] for n, _ in SMALL})
    g_s, d_s, m_s, v_s, loss = _small_allreduce_adamw(_pack_small(dg, loss_part), small(""), small("m_"), small("v_"))

    parts = ride["out"]
    grads = {"w_up": _reduce_pair("grad_reduce_up", parts["parts_a"][0]), "w_down": _reduce_pair("grad_reduce_down", parts["parts_b"][0]),
             "w_in": _reduce_halves("grad_reduce_in", parts["parts_in"][0])}
    grads.update(zip(back_a, _unpack_shards(back_a, _reduce_pair("grad_reduce_slab_a", parts["parts_a"][1]))))
    grads.update(zip(back_b, _unpack_shards(back_b, _reduce_pair("grad_reduce_slab_b", parts["parts_b"][1]))))

    res = {}
    for key, slab in (("grad_", g_s), ("delta_", d_s), ("new_m_", m_s), ("new_v_", v_s)):
        for n, val in _unpack_small(slab, given).items():
            res[key + n] = val
    for n in big_names:
        res["grad_" + n], res["delta_" + n], res["new_m_" + n], res["new_v_" + n] = _adamw_shard(
            "adamw_" + n, grads[n], given[n], given["m_" + n], given["v_" + n])
    outs = [loss.reshape(()), grad_x]
    for key in ("grad_", "delta_", "new_m_", "new_v_"):
        outs += [res[key + n] for n in order]
    return tuple(outs)
```

```python
import functools

import numpy as np
import jax
import jax.numpy as jnp
from jax import lax
from jax.experimental import pallas as pl
from jax.experimental.pallas import tpu as pltpu

F32 = jnp.float32
BF16 = jnp.bfloat16
MESH = pl.DeviceIdType.MESH

D_MODEL = 1024
GRID_W = 64
ROPE_THETA = 10000.0
EPS = 1e-6
H_A, QK_NOPE, QK_ROPE, V_DIM_A, Q_LORA, KV_LORA = 8, 64, 32, 64, 256, 128
H_B, KV_B, HD_B = 8, 2, 64
D_FF = 4096
PLE_DIM = 256
HEAD_W = 128
SCALE_A = (QK_NOPE + QK_ROPE) ** -0.5
SCALE_B = HD_B ** -0.5

ADAM_LR, ADAM_B1, ADAM_B2, ADAM_EPS, ADAM_WD, ADAM_STEP = 0.001, 0.9, 0.999, 1e-08, 0.01, 10
M_HAT_DIV = 1.0 - ADAM_B1 ** ADAM_STEP
V_HAT_DIV = 1.0 - ADAM_B2 ** ADAM_STEP

VMEM_LIMIT_BYTES = 56 * 1024 * 1024

ZC_QB, ZC_QLAT, ZC_CKV, ZC_KPE, ZC_KB, ZC_VB, ZC_GA, ZC_GB = 0, 8, 10, 11, 12, 14, 16, 24
Z_WIDTH = 32 * HEAD_W

BIG = [
    ("w_in", 1024, 3232, 1), ("w_qb", 256, 768, 1), ("w_kvb", 128, 1024, 1), ("w_oa", 512, 1024, 1),
    ("w_ob", 512, 1024, 1), ("w_o", 1024, 1024, 0), ("w_up", 1024, 4096, 1), ("w_down", 4096, 1024, 0),
    ("w_ple_gate", 1024, 1024, 0), ("w_ple", 256, 1024, 1),
]
BIG_BY_NAME = {e[0]: e for e in BIG}
PACK_W = 1024
PACK_ALIGN = 64
SLAB_EARLY = ["w_qb", "w_kvb"]
SLAB_LATE = ["w_oa", "w_ob", "w_ple"]

SMALL = [("g_mix", 1024), ("g_qa", 256), ("g_kva", 128), ("g_qn", 64), ("g_kn", 64), ("g_mlp", 1024),
         ("g_ple", 1024), ("g_final", 1024)]
SMALL_N = sum(n for _, n in SMALL)
LOSS_ROW0 = 40
SMALL_ROWS = 48


def _params(sem):
    return pltpu.CompilerParams(dimension_semantics=sem, vmem_limit_bytes=VMEM_LIMIT_BYTES)


def _sigmoid(v):
    return 1.0 / (1.0 + jnp.exp(-v.astype(F32)))


def _perm(v, p_ref):
    pm = p_ref[...]
    hi = v.astype(BF16)
    lo = (v - hi.astype(F32)).astype(BF16)
    return (jnp.dot(hi, pm, preferred_element_type=F32) + jnp.dot(lo, pm, preferred_element_type=F32))


def _rms(v, g, n):
    v = v.astype(F32)
    rs = lax.rsqrt(jnp.sum(v * v, axis=-1, keepdims=True) * (1.0 / n) + EPS)
    return v * rs * g


def _rms_bwd(dy, v, g, n):
    v = v.astype(F32)
    rs = lax.rsqrt(jnp.sum(v * v, axis=-1, keepdims=True) * (1.0 / n) + EPS)
    vh = v * rs
    dyg = dy * g
    dx = rs * (dyg - vh * (jnp.sum(dyg * vh, axis=-1, keepdims=True) * (1.0 / n)))
    return dx, jnp.sum(dy * vh, axis=0, keepdims=True)


def _ride(body, grid, rider):
    if rider is None:
        return body, [], [], [], []
    n_x, n_sem = len(rider.srcs), len(rider.scratch)

    def wrapped(*refs):
        ids = [pl.program_id(a) for a in range(len(grid))]
        n_in = len(refs) - n_sem - 2 * n_x - rider.n_core_out - rider.n_core_scratch
        core_in, srcs = refs[:n_in], refs[n_in:n_in + n_x]
        core_out = refs[n_in + n_x:n_in + n_x + rider.n_core_out]
        dsts = refs[n_in + n_x + rider.n_core_out:n_in + 2 * n_x + rider.n_core_out]
        core_scr = refs[n_in + 2 * n_x + rider.n_core_out:len(refs) - n_sem]
        sems = refs[len(refs) - n_sem:]

        @pl.when(functools.reduce(jnp.logical_and, [a == 0 for a in ids]))
        def _():
            rider.start(srcs, dsts, *sems)

        body(*core_in, *core_out, *core_scr)

        @pl.when(functools.reduce(jnp.logical_and, [a == n - 1 for a, n in zip(ids, grid)]))
        def _():
            rider.finish(srcs, dsts, *sems)

    hbm = pl.BlockSpec(memory_space=pl.ANY)
    return wrapped, list(rider.srcs), [hbm] * n_x, list(rider.out_shapes), list(rider.scratch)


def _mm(name, a, b, *, trans_b=False, b_slots=False, a_pre=None, out_dtypes=(F32,), epi=None, extras=(), consts=(), accs=(), tm=512,
        tn=None, rider=None):
    a_ops = [o if isinstance(o, tuple) else (o, 0, o.shape[1]) for o in (a if isinstance(a, list) else [a])]
    b_ops = b if isinstance(b, list) else [b]
    assert len(a_ops) == len(b_ops) and not (b_slots and (trans_b or len(b_ops) > 1))
    m = a_ops[0][0].shape[0]
    if b_slots:
        n, tn = b.shape[0] * b.shape[2], b.shape[2]
    else:
        first = b_ops[0][0] if isinstance(b_ops[0], tuple) else b_ops[0]
        n = first.shape[-2] if trans_b else first.shape[1]
        tn = n if tn is None else min(tn, n)
    tm = min(tm, m)
    assert m % tm == 0 and n % tn == 0
    extras = [e if isinstance(e, tuple) else (e, 0) for e in extras]
    n_p, n_ex, n_c, n_out, n_acc = len(a_ops), len(extras), len(consts), len(out_dtypes), len(accs)
    dims = (((1,), (1,)), ((), ())) if trans_b else (((1,), (0,)), ((), ()))

    def body(*refs):
        acc = None
        for a_ref, b_ref in zip(refs[:n_p], refs[n_p:2 * n_p]):
            av = a_ref[...] if a_pre is None else a_pre(a_ref[...].astype(F32))
            part = lax.dot_general(av.astype(BF16), b_ref[...].astype(BF16), dims, preferred_element_type=F32)
            acc = part if acc is None else acc + part
        rest = refs[2 * n_p:]
        res = (acc,) if epi is None else epi(acc, *[e[...] for e in rest[:n_ex + n_c]])
        o_refs = rest[n_ex + n_c:]
        for o_ref, r in zip(o_refs[:n_out], res[:n_out]):
            o_ref[...] = r.astype(o_ref.dtype)
        if n_acc:
            first_step = jnp.logical_and(pl.program_id(0) == 0, pl.program_id(1) == 0)

            @pl.when(first_step)
            def _():
                for o_ref, r in zip(o_refs[n_out:], res[n_out:]):
                    o_ref[...] = r

            @pl.when(jnp.logical_not(first_step))
            def _():
                for o_ref, r in zip(o_refs[n_out:], res[n_out:]):
                    o_ref[...] += r

    def b_spec(op, k_i):
        if b_slots:
            return pl.BlockSpec((None, k_i, tn), lambda j, i: (j, 0, 0))
        if not isinstance(op, tuple):
            return pl.BlockSpec((tn, k_i), lambda j, i: (j, 0)) if trans_b else pl.BlockSpec((k_i, tn), lambda j, i: (0, j))
        assert trans_b
        if len(op) == 2:
            return pl.BlockSpec((None, tn, k_i), lambda j, i, slot=op[1]: (slot, j, 0))
        return pl.BlockSpec((tn, k_i), lambda j, i, blk=op[1]: (j, blk))

    grid = (n // tn, m // tm)
    if rider is not None:
        rider.n_core_out, rider.n_core_scratch = n_out + n_acc, 0
    body, x_in, x_spec, x_out, x_scr = _ride(body, grid, rider)
    a_specs = [pl.BlockSpec((tm, k_i), lambda j, i, blk=blk: (i, blk)) for _, blk, k_i in a_ops]
    b_specs = [b_spec(op, k_i) for op, (_, _, k_i) in zip(b_ops, a_ops)]
    t_spec = pl.BlockSpec((tm, tn), lambda j, i: (i, j))
    e_specs = [pl.BlockSpec((tm, tn), lambda j, i, off=off: (i, j + off)) for _, off in extras]
    c_specs = [pl.BlockSpec(c.shape, lambda j, i: (0, 0)) for c in consts]
    acc_specs = [pl.BlockSpec(sh, lambda j, i: (0, 0)) for sh in accs]
    sem = ("parallel", "parallel") if rider is None and not n_acc else ("arbitrary", "arbitrary")
    outs = pl.pallas_call(
        body, out_shape=[jax.ShapeDtypeStruct((m, n), d) for d in out_dtypes] + [jax.ShapeDtypeStruct(sh, F32) for sh in accs] + x_out,
        grid=grid, in_specs=a_specs + b_specs + e_specs + c_specs + x_spec, out_specs=[t_spec] * n_out + acc_specs + x_spec,
        scratch_shapes=x_scr, compiler_params=_params(sem),
        name=name)(*[o[0] for o in a_ops], *[o[0] if isinstance(o, tuple) else o for o in b_ops], *[e for e, _ in extras], *consts, *x_in)
    return outs[0] if len(outs) == 1 else outs


def _per_head(fn, heads, n_tiled, n_out):
    def run(*args):
        res = [fn(*[a[:, hd * HEAD_W:(hd + 1) * HEAD_W] for a in args[:n_tiled]], *args[n_tiled:]) for hd in range(heads)]
        tiles = [jnp.concatenate([r[k] for r in res], axis=1) for k in range(n_out)]
        sums = [functools.reduce(lambda u, v: u + v, [r[k] for r in res]) for k in range(n_out, len(res[0]))]
        return (*tiles, *sums)
    return run


def _mm_tn(name, a, b, *, a_pre=None, out_dtype=BF16, out_slots=False, tk=1024, tn=1024, tt=4096):
    t, k = a.shape
    n = b.shape[1]
    tk, tn = min(tk, k), min(tn, n)
    if a.dtype == F32 or b.dtype == F32:
        tt = tt // 2
    if k == tk and n == tn:
        tt = tt // 2
    tt = min(tt, t)
    assert b.shape[0] == t and k % tk == 0 and n % tn == 0 and t % tt == 0
    nt = t // tt

    def body(a_ref, b_ref, o_ref, acc):
        av = a_ref[...] if a_pre is None else a_pre(a_ref[...].astype(F32))
        part = lax.dot_general(av.astype(BF16), b_ref[...].astype(BF16), (((0,), (0,)), ((), ())), preferred_element_type=F32)

        @pl.when(pl.program_id(2) == 0)
        def _():
            acc[...] = part

        @pl.when(pl.program_id(2) != 0)
        def _():
            acc[...] += part

        @pl.when(pl.program_id(2) == nt - 1)
        def _():
            o_ref[...] = acc[...].astype(o_ref.dtype)

    if out_slots:
        out_shape, out_spec = (n // tn, k, tn), pl.BlockSpec((None, tk, tn), lambda ki, ni, ti: (ni, ki, 0))
    else:
        out_shape, out_spec = (k, n), pl.BlockSpec((tk, tn), lambda ki, ni, ti: (ki, ni))
    return pl.pallas_call(
        body, out_shape=jax.ShapeDtypeStruct(out_shape, out_dtype), grid=(k // tk, n // tn, nt),
        in_specs=[pl.BlockSpec((tt, tk), lambda ki, ni, ti: (ti, ki)), pl.BlockSpec((tt, tn), lambda ki, ni, ti: (ti, ni))],
        out_specs=out_spec, scratch_shapes=[pltpu.VMEM((tk, tn), F32)],
        compiler_params=_params(("parallel", "parallel", "arbitrary")), name=name)(a, b)


def _rowwise(name, fn, ins, outs, *, consts=(), pos=(), accs=(), heads=1, tm=512, seq=None, rider=None):
    t = ins[0][0].shape[0]
    tm = min(tm, t if seq is None else seq)
    assert t % tm == 0 and (seq is None or seq % tm == 0)
    n_in, n_pos, n_c, n_out, n_acc = len(ins), len(pos), len(consts), len(outs), len(accs)

    def body(*refs):
        vals = [r[...] for r in refs[:n_in + n_pos + n_c]]
        res = fn(*vals)
        o_refs = refs[n_in + n_pos + n_c:]
        for o_ref, r in zip(o_refs[:n_out], res[:n_out]):
            o_ref[...] = r.astype(o_ref.dtype)
        if n_acc:
            first = jnp.logical_and(pl.program_id(0) == 0, pl.program_id(1) == 0)

            @pl.when(first)
            def _():
                for o_ref, r in zip(o_refs[n_out:], res[n_out:]):
                    o_ref[...] = r

            @pl.when(jnp.logical_not(first))
            def _():
                for o_ref, r in zip(o_refs[n_out:], res[n_out:]):
                    o_ref[...] += r

    def tiled(width, c0, per_head):
        return pl.BlockSpec((tm, width), (lambda h, i: (i, c0 + h)) if per_head else (lambda h, i: (i, c0)))

    in_specs = [tiled(w, c0, ph) for _, w, c0, ph in ins]
    if n_pos:
        nblk = seq // tm
        in_specs += [pl.BlockSpec((tm, a.shape[1]), lambda h, i: (i % nblk, 0)) for a in pos]
    in_specs += [pl.BlockSpec(a.shape, lambda h, i: (0, 0)) for a in consts]
    out_specs = [tiled(w, 0, ph) for _, _, w, ph in outs] + [pl.BlockSpec(s, lambda h, i: (0, 0)) for s in accs]
    out_shape = [jax.ShapeDtypeStruct((t, c), d) for c, d, _, _ in outs] + [jax.ShapeDtypeStruct(s, F32) for s in accs]
    sem = ("arbitrary", "arbitrary") if n_acc or rider is not None else ("parallel", "parallel")
    grid = (heads, t // tm)
    if rider is not None:
        rider.n_core_out, rider.n_core_scratch = n_out + n_acc, 0
    body, x_in, x_spec, x_out, x_scr = _ride(body, grid, rider)
    res = pl.pallas_call(body, out_shape=out_shape + x_out, grid=grid, in_specs=in_specs + x_spec, out_specs=out_specs + x_spec,
                         scratch_shapes=x_scr, compiler_params=_params(sem), name=name)(*[a for a, _, _, _ in ins], *pos, *consts, *x_in)
    return res[0] if len(res) == 1 else res


ATTN_HEADS_PER_STEP = 4


def _attn_fwd(name, q, k, kc0, v, vc0, *, heads, group, nseq, seq, tq=512, rider=None):
    tq = min(tq, seq)
    nq = seq // tq
    hp = ATTN_HEADS_PER_STEP
    grid = (heads // hp, nseq, nq)
    shared = group > 1
    assert group % hp == 0 if shared else (kc0 % hp == 0 and vc0 % hp == 0)

    def body(q_ref, k_ref, v_ref, o_ref, lse_ref):
        for j in range(hp):
            cols = slice(j * HEAD_W, (j + 1) * HEAD_W)
            kj = (k_ref[...] if shared else k_ref[:, cols]).astype(BF16)
            vj = (v_ref[...] if shared else v_ref[:, cols]).astype(BF16)
            s = lax.dot_general(q_ref[:, cols], kj, (((1,), (1,)), ((), ())), preferred_element_type=F32)
            m = jnp.max(s, axis=-1, keepdims=True)
            p = jnp.exp((s - m).astype(BF16))
            vj = jnp.where(lax.broadcasted_iota(jnp.int32, (1, HEAD_W), 1) == HEAD_W - 1, jnp.ones((), BF16), vj)
            o = jnp.dot(p, vj, preferred_element_type=F32)
            l = o[:, HEAD_W - 1:]
            o_ref[:, cols] = (o * (1.0 / l)).astype(o_ref.dtype)
            lse_ref[j] = m + jnp.log(l)

    if rider is not None:
        rider.n_core_out, rider.n_core_scratch = 2, 0
    body, x_in, x_spec, x_out, x_scr = _ride(body, grid, rider)
    q_spec = pl.BlockSpec((tq, hp * HEAD_W), lambda h, b, i: (b * nq + i, h))
    if shared:
        k_spec = pl.BlockSpec((seq, HEAD_W), lambda h, b, i: (b, kc0 + (h * hp) // group))
        v_spec = pl.BlockSpec((seq, HEAD_W), lambda h, b, i: (b, vc0 + (h * hp) // group))
    else:
        k_spec = pl.BlockSpec((seq, hp * HEAD_W), lambda h, b, i: (b, kc0 // hp + h))
        v_spec = pl.BlockSpec((seq, hp * HEAD_W), lambda h, b, i: (b, vc0 // hp + h))
    lse_spec = pl.BlockSpec((hp, tq, 1), lambda h, b, i: (h, b * nq + i, 0))
    sem = ("parallel",) * 3 if rider is None else ("arbitrary",) * 3
    return pl.pallas_call(
        body, out_shape=[jax.ShapeDtypeStruct(q.shape, BF16), jax.ShapeDtypeStruct((heads, q.shape[0], 1), F32)] + x_out,
        grid=grid, in_specs=[q_spec, k_spec, v_spec] + x_spec, out_specs=[q_spec, lse_spec] + x_spec, scratch_shapes=x_scr,
        compiler_params=_params(sem), name=name)(q, k, v, *x_in)


def _attn_bwd(name, q, k, kc0, v, vc0, o, do, lse, *, heads, group, nseq, seq, tq=1024, rider=None):
    tq = min(tq, seq)
    nq = seq // tq
    hk = heads // group
    t = q.shape[0]
    grid = (hk, nseq, group, nq)

    def body(q_ref, k_ref, v_ref, o_ref, do_ref, lse_ref, dq_ref, dk_ref, dv_ref, dk_acc, dv_acc):
        g, i = pl.program_id(2), pl.program_id(3)
        qv, kv, vv, dov = q_ref[...], k_ref[...].astype(BF16), v_ref[...].astype(BF16), do_ref[...]
        s = lax.dot_general(qv, kv, (((1,), (1,)), ((), ())), preferred_element_type=F32)
        pn = jnp.exp(s - lse_ref[...])
        dp = lax.dot_general(dov, vv, (((1,), (1,)), ((), ())), preferred_element_type=F32)
        delta = jnp.sum(dov.astype(F32) * o_ref[...].astype(F32), axis=-1, keepdims=True)
        ds = (pn * (dp - delta)).astype(BF16)
        dq_ref[...] = jnp.dot(ds, kv, preferred_element_type=F32)
        dk_part = lax.dot_general(ds, qv, (((0,), (0,)), ((), ())), preferred_element_type=F32)
        dv_part = lax.dot_general(pn.astype(BF16), dov, (((0,), (0,)), ((), ())), preferred_element_type=F32)
        first = jnp.logical_and(g == 0, i == 0)

        @pl.when(first)
        def _():
            dk_acc[...] = dk_part
            dv_acc[...] = dv_part

        @pl.when(jnp.logical_not(first))
        def _():
            dk_acc[...] += dk_part
            dv_acc[...] += dv_part

        @pl.when(jnp.logical_and(g == group - 1, i == nq - 1))
        def _():
            dk_ref[...] = dk_acc[...].astype(dk_ref.dtype)
            dv_ref[...] = dv_acc[...].astype(dv_ref.dtype)

    if rider is not None:
        rider.n_core_out, rider.n_core_scratch = 3, 2
    body, x_in, x_spec, x_out, x_scr = _ride(body, grid, rider)
    q_spec = pl.BlockSpec((tq, HEAD_W), lambda kh, b, g, i: (b * nq + i, kh * group + g))
    kv_out = pl.BlockSpec((seq, HEAD_W), lambda kh, b, g, i: (b, kh))
    lse_spec = pl.BlockSpec((None, tq, 1), lambda kh, b, g, i: (kh * group + g, b * nq + i, 0))
    sem = ("parallel", "parallel", "arbitrary", "arbitrary") if rider is None else ("arbitrary",) * 4
    return pl.pallas_call(
        body,
        out_shape=[jax.ShapeDtypeStruct(q.shape, F32), jax.ShapeDtypeStruct((t, hk * HEAD_W), BF16),
                   jax.ShapeDtypeStruct((t, hk * HEAD_W), BF16)] + x_out,
        grid=grid,
        in_specs=[q_spec, pl.BlockSpec((seq, HEAD_W), lambda kh, b, g, i: (b, kc0 + kh)),
                  pl.BlockSpec((seq, HEAD_W), lambda kh, b, g, i: (b, vc0 + kh)), q_spec, q_spec, lse_spec] + x_spec,
        out_specs=[q_spec, kv_out, kv_out] + x_spec,
        scratch_shapes=[pltpu.VMEM((seq, HEAD_W), F32), pltpu.VMEM((seq, HEAD_W), F32)] + x_scr,
        compiler_params=_params(sem), name=name)(q, k, v, o, do, lse, *x_in)


def _place():
    return lax.axis_index("x"), lax.axis_index("y"), lax.axis_index("c")


def _other_chips(x, y):
    return [(1 - x, y), (x, 1 - y), (1 - x, 1 - y)]


class _Exchange:
    def __init__(self, kind, srcs):
        assert kind in ("gather", "scatter")
        self.kind, self.srcs = kind, list(srcs)
        n = len(self.srcs)
        self.out_shapes = [jax.ShapeDtypeStruct((4, *a.shape[-2:]), a.dtype) for a in self.srcs]
        self.scratch = [pltpu.SemaphoreType.DMA((3 * n,)), pltpu.SemaphoreType.DMA((3 * n,)), pltpu.SemaphoreType.DMA((n,))]
        self.n_core_out = self.n_core_scratch = 0

    def _copies(self, j, src_ref, out_ref, send_sems, recv_sems, landing):
        x, y, c = _place()

        def remote(k, s, d, to):
            return pltpu.make_async_remote_copy(src_ref=s, dst_ref=d, send_sem=send_sems.at[3 * j + k], recv_sem=recv_sems.at[3 * j + k],
                                                device_id=to, device_id_type=MESH)

        me = 2 * x + y
        part = (lambda i: src_ref) if self.kind == "gather" else (lambda i: src_ref.at[i])
        if landing:
            return [remote(k, part(me), out_ref.at[2 * px + py], (px, py, c)) for k, (px, py) in enumerate(_other_chips(x, y))]
        return [remote(k, part(2 * px + py), out_ref.at[me], (px, py, c)) for k, (px, py) in enumerate(_other_chips(x, y))]

    def _local(self, j, src_ref, out_ref, local_sems):
        x, y, _ = _place()
        me = 2 * x + y
        return pltpu.make_async_copy(src_ref if self.kind == "gather" else src_ref.at[me], out_ref.at[me], local_sems.at[j])

    def start(self, src_refs, out_refs, send_sems, recv_sems, local_sems):
        for j, (src_ref, out_ref) in enumerate(zip(src_refs, out_refs)):
            self._local(j, src_ref, out_ref, local_sems).start()
            for mine in self._copies(j, src_ref, out_ref, send_sems, recv_sems, False):
                mine.start()

    def finish(self, src_refs, out_refs, send_sems, recv_sems, local_sems):
        for j, (src_ref, out_ref) in enumerate(zip(src_refs, out_refs)):
            for landed in self._copies(j, src_ref, out_ref, send_sems, recv_sems, True):
                landed.wait_recv()
        for j, (src_ref, out_ref) in enumerate(zip(src_refs, out_refs)):
            for mine in self._copies(j, src_ref, out_ref, send_sems, recv_sems, False):
                mine.wait_send()
            self._local(j, src_ref, out_ref, local_sems).wait()


def _gather_by_halves(name, srcs):
    n = len(srcs)

    def body(*refs):
        x, y, c = _place()
        me = 2 * x + y
        local_sems = refs[-1]
        copies = []
        for j in range(n):
            src_ref, out_ref, send_sems, recv_sems = refs[j], refs[n + j], refs[2 * n + 2 * j], refs[2 * n + 2 * j + 1]
            half = srcs[j].shape[0] // 2
            rows_c = pl.ds(pl.multiple_of(c * half, half), half)
            rows_s = pl.ds(pl.multiple_of((1 - c) * half, half), half)

            def remote(k, s_ref, d_ref, to, send_sems=send_sems, recv_sems=recv_sems):
                return pltpu.make_async_remote_copy(src_ref=s_ref, dst_ref=d_ref, send_sem=send_sems.at[k], recv_sem=recv_sems.at[k],
                                                    device_id=to, device_id_type=MESH)

            local = pltpu.make_async_copy(src_ref, out_ref.at[me], local_sems.at[j])
            local.start()
            chips = _other_chips(x, y)
            sent = [remote(k, src_ref.at[rows_c], out_ref.at[me, rows_c], (px, py, c)) for k, (px, py) in enumerate(chips)]
            landing = [remote(k, src_ref.at[rows_c], out_ref.at[2 * px + py, rows_c], (px, py, c)) for k, (px, py) in enumerate(chips)]
            passed = [remote(3 + k, out_ref.at[2 * px + py, rows_c], out_ref.at[2 * px + py, rows_c], (x, y, 1 - c))
                      for k, (px, py) in enumerate(chips)]
            from_sibling = [remote(3 + k, out_ref.at[2 * px + py, rows_s], out_ref.at[2 * px + py, rows_s], (x, y, 1 - c))
                            for k, (px, py) in enumerate(chips)]
            for cp in sent:
                cp.start()
            copies.append((local, sent, landing, passed, from_sibling))
        for local, sent, landing, passed, from_sibling in copies:
            for k in range(3):
                landing[k].wait_recv()
                passed[k].start()
        for local, sent, landing, passed, from_sibling in copies:
            for k in range(3):
                from_sibling[k].wait_recv()
            for cp in sent + passed:
                cp.wait_send()
            local.wait()

    sems = [pltpu.SemaphoreType.DMA((6,)) for _ in range(2 * n)] + [pltpu.SemaphoreType.DMA((n,))]
    return pl.pallas_call(
        body, out_shape=[jax.ShapeDtypeStruct((4, *a.shape), a.dtype) for a in srcs],
        in_specs=[pl.BlockSpec(memory_space=pl.ANY)] * n, out_specs=[pl.BlockSpec(memory_space=pltpu.VMEM)] * n,
        scratch_shapes=sems, compiler_params=pltpu.CompilerParams(vmem_limit_bytes=VMEM_LIMIT_BYTES), name=name)(*srcs)


def _adamw(w, g, m, v):
    m = ADAM_B1 * m + (1.0 - ADAM_B1) * g
    v = ADAM_B2 * v + (1.0 - ADAM_B2) * (g * g)
    delta = -ADAM_LR * ((m / M_HAT_DIV) / (jnp.sqrt(v / V_HAT_DIV) + ADAM_EPS) + ADAM_WD * w)
    return delta, m, v


def _small_allreduce_adamw(part, w, m, v):
    def body(part_ref, w_ref, m_ref, v_ref, g_out, d_out, m_out, v_out, loss_out, buf, send_sems, recv_sems):
        x, y, c = _place()
        me = 4 * x + 2 * y + c
        buf[me] = part_ref[...]

        def flip(k):
            fx, fy, fc = (k >> 2) & 1, (k >> 1) & 1, k & 1
            px, py, pc = (1 - x if fx else x), (1 - y if fy else y), (1 - c if fc else c)
            return (px, py, pc), 4 * px + 2 * py + pc

        def copy(k, slot):
            return pltpu.make_async_remote_copy(
                src_ref=part_ref, dst_ref=buf.at[slot], send_sem=send_sems.at[k - 1], recv_sem=recv_sems.at[k - 1],
                device_id=flip(k)[0], device_id_type=MESH)

        sent = [copy(k, me) for k in range(1, 8)]
        for cp in sent:
            cp.start()
        for k in range(1, 8):
            copy(k, flip(k)[1]).wait_recv()
        for cp in sent:
            cp.wait_send()
        tot = buf[0]
        for j in range(1, 8):
            tot = tot + buf[j]
        delta, m_new, v_new = _adamw(w_ref[...], tot, m_ref[...], v_ref[...])
        g_out[...] = tot
        d_out[...] = delta
        m_out[...] = m_new
        v_out[...] = v_new
        loss_out[...] = jnp.sum(tot[LOSS_ROW0:LOSS_ROW0 + 8, :]).reshape(1, 1)

    vm = pl.BlockSpec(memory_space=pltpu.VMEM)
    shp = jax.ShapeDtypeStruct((SMALL_ROWS, 128), F32)
    return pl.pallas_call(
        body, out_shape=[shp, shp, shp, shp, jax.ShapeDtypeStruct((1, 1), F32)],
        in_specs=[vm, vm, vm, vm], out_specs=[vm, vm, vm, vm, vm],
        scratch_shapes=[pltpu.VMEM((8, SMALL_ROWS, 128), F32), pltpu.SemaphoreType.DMA((7,)), pltpu.SemaphoreType.DMA((7,))],
        name="small_allreduce_adamw")(part, w, m, v)


def _row_tile(rows, cap):
    return max(t for t in range(16, min(rows, cap) + 1, 16) if rows % t == 0)


def _reduce_pair(name, parts, adam=None):
    _, rows, w = parts.shape
    tr = _row_tile(rows, 576)
    nt = rows // tr
    n_adam = 0 if adam is None else 3

    def body(p_ref, *refs):
        adam_refs, o_refs = refs[:n_adam], refs[n_adam:len(refs) - 4]
        mine, theirs, send_sems, recv_sems = refs[len(refs) - 4:]
        i = pl.program_id(0)
        x, y, c = _place()

        def copy(t):
            rows_t = pl.ds(pl.multiple_of(t * tr, tr), tr)
            return pltpu.make_async_remote_copy(src_ref=mine.at[rows_t], dst_ref=theirs.at[rows_t], send_sem=send_sems.at[t],
                                                recv_sem=recv_sems.at[t], device_id=(x, y, 1 - c), device_id_type=MESH)

        @pl.when(i < nt)
        def _():
            mine[pl.ds(pl.multiple_of(i * tr, tr), tr), :] = (
                (p_ref[0].astype(F32) + p_ref[1].astype(F32)) + p_ref[2].astype(F32)) + p_ref[3].astype(F32)
            copy(i).start()

        @pl.when(i >= nt)
        def _():
            t = i - nt
            copy(t).wait()
            rows_t = pl.ds(pl.multiple_of(t * tr, tr), tr)
            g = mine[rows_t, :] + theirs[rows_t, :]
            o_refs[0][...] = g
            if adam is not None:
                w_ref, m_ref, v_ref = adam_refs
                for o_ref, val in zip(o_refs[1:], _adamw(w_ref[...], g, m_ref[...], v_ref[...])):
                    o_ref[...] = val

    if adam is None:
        out_shape, out_specs = jax.ShapeDtypeStruct((rows, w), F32), pl.BlockSpec((tr, w), lambda i: (jnp.maximum(i - nt, 0), 0))
        adam_specs = []
    else:
        t_spec = pl.BlockSpec((None, tr, w), lambda i: (0, jnp.maximum(i - nt, 0), 0))
        out_shape, out_specs, adam_specs = [jax.ShapeDtypeStruct((1, rows, w), F32)] * 4, [t_spec] * 4, [t_spec] * 3
    return pl.pallas_call(
        body, out_shape=out_shape, grid=(2 * nt,),
        in_specs=[pl.BlockSpec((4, tr, w), lambda i: (0, jnp.minimum(i, nt - 1), 0))] + adam_specs, out_specs=out_specs,
        scratch_shapes=[pltpu.VMEM((rows, w), F32), pltpu.VMEM((rows, w), F32), pltpu.SemaphoreType.DMA((nt,)),
                        pltpu.SemaphoreType.DMA((nt,))],
        compiler_params=_params(("arbitrary",)), name=name)(parts, *(adam or ()))


def _presum_halves(name, shards):
    _, rows, w = shards.shape
    half = rows // 2

    def body(s_ref, o_ref, theirs, send_sems, recv_sems):
        x, y, c = _place()
        rows_c = pl.ds(pl.multiple_of(c * half, half), half)
        rows_s = pl.ds(pl.multiple_of((1 - c) * half, half), half)
        sent = [pltpu.make_async_remote_copy(src_ref=s_ref.at[j, rows_s], dst_ref=theirs.at[j], send_sem=send_sems.at[j],
                                             recv_sem=recv_sems.at[j], device_id=(x, y, 1 - c), device_id_type=MESH) for j in range(4)]
        for cp in sent:
            cp.start()
        for j, cp in enumerate(sent):
            cp.wait_recv()
            o_ref[j] = (s_ref[j, rows_c, :].astype(F32) + theirs[j].astype(F32)).astype(o_ref.dtype)
        for cp in sent:
            cp.wait_send()

    vm = pl.BlockSpec(memory_space=pltpu.VMEM)
    return pl.pallas_call(
        body, out_shape=jax.ShapeDtypeStruct((4, half, w), shards.dtype), in_specs=[vm], out_specs=vm,
        scratch_shapes=[pltpu.VMEM((4, half, w), shards.dtype), pltpu.SemaphoreType.DMA((4,)), pltpu.SemaphoreType.DMA((4,))],
        compiler_params=pltpu.CompilerParams(vmem_limit_bytes=VMEM_LIMIT_BYTES), name=name)(shards)


def _reduce_halves(name, parts):
    _, half, w = parts.shape

    def body(p_ref, o_ref, mine, send_sem, recv_sem):
        x, y, c = _place()
        rows_c = pl.ds(pl.multiple_of(c * half, half), half)
        rows_s = pl.ds(pl.multiple_of((1 - c) * half, half), half)
        mine[...] = ((p_ref[0].astype(F32) + p_ref[1].astype(F32)) + p_ref[2].astype(F32)) + p_ref[3].astype(F32)
        send = pltpu.make_async_remote_copy(src_ref=mine, dst_ref=o_ref.at[rows_c], send_sem=send_sem, recv_sem=recv_sem,
                                            device_id=(x, y, 1 - c), device_id_type=MESH)
        send.start()
        o_ref[rows_c, :] = mine[...]
        pltpu.make_async_remote_copy(src_ref=mine, dst_ref=o_ref.at[rows_s], send_sem=send_sem, recv_sem=recv_sem,
                                     device_id=(x, y, 1 - c), device_id_type=MESH).wait_recv()
        send.wait_send()

    vm = pl.BlockSpec(memory_space=pltpu.VMEM)
    return pl.pallas_call(
        body, out_shape=jax.ShapeDtypeStruct((2 * half, w), F32), in_specs=[vm], out_specs=vm,
        scratch_shapes=[pltpu.VMEM((half, w), F32), pltpu.SemaphoreType.DMA(()), pltpu.SemaphoreType.DMA(())],
        compiler_params=pltpu.CompilerParams(vmem_limit_bytes=VMEM_LIMIT_BYTES), name=name)(parts)


def _adamw_shard(name, g, w, m, v):
    _, rows, cols = w.shape
    tr = _row_tile(rows, 256)

    def body(g_ref, w_ref, m_ref, v_ref, g_out, d_out, m_out, v_out):
        gv = g_ref[...]
        delta, m_new, v_new = _adamw(w_ref[...], gv, m_ref[...], v_ref[...])
        g_out[...] = gv
        d_out[...] = delta
        m_out[...] = m_new
        v_out[...] = v_new

    t_spec = pl.BlockSpec((None, tr, cols), lambda i: (0, i, 0))
    shp = jax.ShapeDtypeStruct((1, rows, cols), F32)
    return pl.pallas_call(body, out_shape=[shp] * 4, grid=(rows // tr,), in_specs=[pl.BlockSpec((tr, cols), lambda i: (i, 0))] + [t_spec] * 3,
                          out_specs=[t_spec] * 4, compiler_params=_params(("parallel",)), name=name)(g, w, m, v)


def _shard_shape(name):
    _, r, c, ax = BIG_BY_NAME[name]
    return (r, c // 4) if ax == 1 else (r // 4, c)


def _pad_rows(a, axis):
    pad = [(0, 0)] * a.ndim
    pad[axis] = (0, -a.shape[axis] % PACK_ALIGN)
    return jnp.pad(a, pad)


def _pack_shards(names, shards, dtype):
    return _pad_rows(jnp.concatenate([s.astype(dtype).reshape(-1, PACK_W) for s in shards], axis=0), 0)


def _unpack_shards(names, slab):
    out, off = [], 0
    for name in names:
        rs, cs = _shard_shape(name)
        n = rs * cs // PACK_W
        out.append(slab[off:off + n].reshape(rs, cs))
        off += n
    return out


def _unpack_full(names, slabs):
    out, off = [], 0
    for name in names:
        _, r, c, ax = BIG_BY_NAME[name]
        n = r * c // 4 // PACK_W
        seg = slabs[:, off:off + n]
        out.append(seg.reshape(4, r, c // 4).transpose(1, 0, 2).reshape(r, c) if ax == 1 else seg.reshape(r, c))
        off += n
    return out


def _pack_full(names, mats, dtype):
    segs = []
    for name, a in zip(names, mats):
        _, r, c, ax = BIG_BY_NAME[name]
        a = a.astype(dtype)
        a = a.reshape(r, 4, c // 4).transpose(1, 0, 2) if ax == 1 else a
        segs.append(a.reshape(4, -1, PACK_W))
    return _pad_rows(jnp.concatenate(segs, axis=1), 1)


def _pad_heads_cols(wm, heads, d):
    k = wm.shape[0]
    return jnp.pad(wm.reshape(k, heads, d), ((0, 0), (0, 0), (0, HEAD_W - d))).reshape(k, heads * HEAD_W)


def _unpad_heads_cols(wm, heads, d):
    k = wm.shape[0]
    return wm.reshape(k, heads, HEAD_W)[:, :, :d].reshape(k, heads * d)


def _win_ext(w_in):
    o = np.cumsum([0, Q_LORA, KV_LORA, QK_ROPE, H_B * HD_B, KV_B * HD_B, KV_B * HD_B, D_MODEL, D_MODEL])
    pc = lambda a, n: jnp.pad(a, ((0, 0), (0, n - a.shape[1])))
    return jnp.concatenate([
        _pad_heads_cols(w_in[:, o[3]:o[4]], H_B, HD_B), w_in[:, o[0]:o[1]], w_in[:, o[1]:o[2]], pc(w_in[:, o[2]:o[3]], HEAD_W),
        _pad_heads_cols(w_in[:, o[4]:o[5]], KV_B, HD_B), _pad_heads_cols(w_in[:, o[5]:o[6]], KV_B, HD_B),
        w_in[:, o[6]:o[7]], w_in[:, o[7]:o[8]]], axis=1)


def _win_unext(blocks):
    c = HEAD_W
    qb, mid, ga, gb = blocks
    at = lambda zc: (zc - ZC_QLAT) * c
    return jnp.concatenate([
        mid[:, at(ZC_QLAT):at(ZC_CKV)], mid[:, at(ZC_CKV):at(ZC_KPE)], mid[:, at(ZC_KPE):at(ZC_KPE) + QK_ROPE],
        _unpad_heads_cols(qb, H_B, HD_B), _unpad_heads_cols(mid[:, at(ZC_KB):at(ZC_VB)], KV_B, HD_B),
        _unpad_heads_cols(mid[:, at(ZC_VB):at(ZC_GA)], KV_B, HD_B), ga, gb], axis=1)


def _wkv_ext(w_kvb):
    wk = w_kvb.reshape(KV_LORA, H_A, QK_NOPE + V_DIM_A)
    k_cols = jnp.pad(wk[:, :, :QK_NOPE], ((0, 0), (0, 0), (0, HEAD_W - QK_NOPE))).reshape(KV_LORA, H_A * HEAD_W)
    v_cols = jnp.pad(wk[:, :, QK_NOPE:], ((0, 0), (0, 0), (0, HEAD_W - V_DIM_A))).reshape(KV_LORA, H_A * HEAD_W)
    eye = jnp.pad(jnp.eye(QK_ROPE, dtype=w_kvb.dtype), ((0, 0), (QK_NOPE, HEAD_W - QK_NOPE - QK_ROPE)))
    pe_rows = jnp.concatenate([jnp.tile(eye, (1, H_A)), jnp.zeros((QK_ROPE, H_A * HEAD_W), w_kvb.dtype)], axis=1)
    top = jnp.concatenate([k_cols, v_cols], axis=1)
    return jnp.concatenate([top, pe_rows, jnp.zeros((2 * HEAD_W - KV_LORA - QK_ROPE, 2 * H_A * HEAD_W), w_kvb.dtype)], axis=0)


def _wkv_unext(k_block, v_block):
    k_cols = k_block[:KV_LORA].reshape(KV_LORA, H_A, HEAD_W)[:, :, :QK_NOPE]
    v_cols = v_block[:KV_LORA].reshape(KV_LORA, H_A, HEAD_W)[:, :, :V_DIM_A]
    return jnp.concatenate([k_cols, v_cols], axis=2).reshape(KV_LORA, H_A * (QK_NOPE + V_DIM_A))


def _pad_heads_rows(wm, heads, d):
    n = wm.shape[1]
    return jnp.pad(wm.reshape(heads, d, n), ((0, 0), (0, HEAD_W - d), (0, 0))).reshape(heads * HEAD_W, n)


def _unpad_heads_rows(wm, heads, d):
    n = wm.shape[1]
    return wm.reshape(heads, HEAD_W, n)[:, :d].reshape(heads * d, n)


def _rope_tables(seq):
    def ang(pos, dim):
        inv = np.float32(ROPE_THETA) ** (-np.arange(0, dim, 2, dtype=np.float32) / np.float32(dim))
        return pos.astype(np.float32)[:, None] * inv[None, :]

    def rot(dim):
        r = np.zeros((dim, dim), np.float32)
        half = dim // 2
        r[np.arange(half) + half, np.arange(half)] = -1.0
        r[np.arange(half), np.arange(half) + half] = 1.0
        return r

    def table(blocks):
        cos, sin = np.ones((seq, HEAD_W), np.float32), np.zeros((seq, HEAD_W), np.float32)
        pm = np.zeros((HEAD_W, HEAD_W), np.float32)
        for c0, a in blocks:
            d = 2 * a.shape[1]
            cos[:, c0:c0 + d] = np.concatenate([np.cos(a), np.cos(a)], axis=1)
            sin[:, c0:c0 + d] = np.concatenate([np.sin(a), np.sin(a)], axis=1)
            pm[c0:c0 + d, c0:c0 + d] = rot(d)
        return jnp.asarray(cos), jnp.asarray(sin), jnp.asarray(pm, BF16), jnp.asarray(pm.T, BF16)

    tok = np.arange(seq)
    a1 = ang(tok, QK_ROPE)
    arow, acol = ang(tok // GRID_W, HD_B // 2), ang(tok % GRID_W, HD_B // 2)
    return table([(QK_NOPE, a1)]), table([(0, a1)]), table([(0, arow), (HD_B // 2, acol)])


def _local_step(x, p, tgt, gains, wts, ride=None):
    nb, seq, _ = x.shape
    t = nb * seq
    x0 = x.reshape(t, D_MODEL)
    p2 = p.reshape(t, PLE_DIM)
    tg = tgt.reshape(t, D_MODEL)
    (cq_t, sq_t, pq, pq_t), (ck_t, sk_t, pk, pk_t), (cb_t, sb_t, pb, pb_t) = _rope_tables(seq)
    padg = lambda g: jnp.pad(g, ((0, 0), (0, HEAD_W - g.shape[1])))
    g_qn, g_kn = padg(gains["g_qn"]), padg(gains["g_kn"])

    win = _win_ext(wts["w_in"])
    wqb = _pad_heads_cols(wts["w_qb"], H_A, QK_NOPE + QK_ROPE)
    wkv = _wkv_ext(wts["w_kvb"])

    norm = lambda n: (lambda v, g: (_rms(v, g, n),))
    full = lambda a: (a, a.shape[1], 0, False)
    wts = dict(wts)
    rider_of = lambda kernel_name: None if ride is None else ride["gather"][kernel_name][0]

    def landed(kernel_name, got):
        if ride is not None:
            wts.update(ride["gather"][kernel_name][1](got))

    h = _rowwise("norm_mix", norm(D_MODEL), [full(x0)], [(D_MODEL, BF16, D_MODEL, False)], consts=[gains["g_mix"]])
    res = _mm("in_proj", h, win, out_dtypes=(BF16,), tn=2048, rider=rider_of("in_proj"))
    z, got = (res, []) if ride is None else (res[0], res[1:])
    landed("in_proj", got)

    def rope_fwd(scale):
        return lambda v, cos, sin, pm: ((v * cos + _perm(v, pm) * sin) * scale,)

    heads_tile = lambda n: (n * HEAD_W, BF16, n * HEAD_W, False)
    z_qlat, z_ckv, z_kpe = (z, Q_LORA, ZC_QLAT // 2, False), (z, HEAD_W, ZC_CKV, False), (z, HEAD_W, ZC_KPE, False)

    def q_path(zq, cos, sin, g, w, pm):
        cqv = _rms(zq, g, Q_LORA).astype(BF16)
        qa = jnp.dot(cqv, w[...], preferred_element_type=F32)
        return (cqv, *_per_head(rope_fwd(SCALE_A), H_A, 1, 1)(qa, cos, sin, pm))

    cq, q_a = _rowwise("q_path", q_path, [z_qlat], [(Q_LORA, BF16, Q_LORA, False), heads_tile(H_A)],
                       pos=[cq_t, sq_t], consts=[gains["g_qa"], wqb, pq], seq=seq)

    def kv_path(ckv_raw, kpe_raw, cos, sin, g, w, pm):
        kinv = jnp.concatenate([_rms(ckv_raw, g, KV_LORA), *rope_fwd(1.0)(kpe_raw, cos, sin, pm)], axis=1).astype(BF16)
        return kinv, jnp.dot(kinv, w[...], preferred_element_type=F32)

    kin, kv_a = _rowwise("kv_path", kv_path, [z_ckv, z_kpe], [heads_tile(2), heads_tile(2 * H_A)],
                         pos=[ck_t, sk_t], consts=[gains["g_kva"], wkv, pk], seq=seq)
    o_a, lse_a, *got = _attn_fwd("attn_a_fwd", q_a, kv_a, 0, kv_a, H_A, heads=H_A, group=1, nseq=nb, seq=seq,
                                 rider=rider_of("attn_a_fwd"))
    landed("attn_a_fwd", got)

    def prep_fwd(scale):
        def fn(v, cos, sin, g, pm):
            yv = _rms(v, g, HD_B)
            return ((yv * cos + _perm(yv, pm) * sin) * scale,)
        return fn

    z_qb, z_kb = (z, H_B * HEAD_W, ZC_QB // H_B, False), (z, KV_B * HEAD_W, ZC_KB // KV_B, False)
    res = _rowwise("prep_qb", _per_head(prep_fwd(SCALE_B), H_B, 1, 1), [z_qb], [heads_tile(H_B)],
                   pos=[cb_t, sb_t], consts=[g_qn, pb], seq=seq, rider=rider_of("prep_qb"))
    q_b, got = (res, []) if ride is None else (res[0], res[1:])
    landed("prep_qb", got)
    k_b = _rowwise("prep_kb", _per_head(prep_fwd(1.0), KV_B, 1, 1), [z_kb], [heads_tile(KV_B)],
                   pos=[cb_t, sb_t], consts=[g_kn, pb], seq=seq)
    o_b, lse_b, *got = _attn_fwd("attn_b_fwd", q_b, k_b, 0, z, ZC_VB, heads=H_B, group=H_B // KV_B, nseq=nb, seq=seq,
                                 rider=rider_of("attn_b_fwd"))
    landed("attn_b_fwd", got)
    woa = _pad_heads_rows(wts["w_oa"], H_A, V_DIM_A)
    wob = _pad_heads_rows(wts["w_ob"], H_B, HD_B)
    wo, wup, wdown = wts["w_o"], wts["w_up"], wts["w_down"]

    def residual_norm(acc, r, g):
        xv = r + acc
        return xv, _rms(xv, g, D_MODEL)

    def mix_out(oa, ob, ga, gb, r, g, w_a, w_b, w_out):
        a = jnp.dot(oa, w_a[...], preferred_element_type=F32)
        b = jnp.dot(ob, w_b[...], preferred_element_type=F32)
        mg = (_sigmoid(ga) * a + _sigmoid(gb) * b).astype(BF16)
        return (a, b, mg, *residual_norm(jnp.dot(mg, w_out[...], preferred_element_type=F32), r, g))

    z_ga, z_gb = (z, D_MODEL, ZC_GA // 8, False), (z, D_MODEL, ZC_GB // 8, False)
    wide = lambda d: (D_MODEL, d, D_MODEL, False)
    ya, yb, merged, x1, h2, *got = _rowwise("mix_out", mix_out, [full(o_a), full(o_b), z_ga, z_gb, full(x0)],
                                            [wide(BF16), wide(BF16), wide(BF16), wide(F32), wide(BF16)],
                                            consts=[gains["g_mlp"], woa, wob, wo], tm=256, rider=rider_of("mix_out"))
    landed("mix_out", got)
    wpg, wple = wts["w_ple_gate"], wts["w_ple"]

    square = lambda v: v * v
    u = _mm("mlp_up", h2, wup, b_slots=True, out_dtypes=(BF16,), epi=lambda acc: (jnp.maximum(acc, 0.0),), tm=1024)
    x2, h3 = _mm("mlp_down", u, wdown, a_pre=square, out_dtypes=(F32, BF16), epi=residual_norm, extras=(x1,), consts=[gains["g_ple"]])

    def norm_res_bwd(dh, v, res, g):
        dx, dg = _rms_bwd(dh, v, g, D_MODEL)
        return dx + res, dg

    def tail(x2v, h3v, pv, tv, gf, gp, w_gate, w_emb):
        sg = _sigmoid(jnp.dot(h3v, w_gate[...], preferred_element_type=F32))
        pev = jnp.dot(pv.astype(BF16), w_emb[...], preferred_element_type=F32)
        x3 = x2v + sg * pev
        rs = lax.rsqrt(jnp.sum(x3 * x3, axis=-1, keepdims=True) * (1.0 / D_MODEL) + EPS)
        xh = x3 * rs
        err = xh * gf - tv
        dy = err * (1.0 / D_MODEL)
        dyg = dy * gf
        dx3 = rs * (dyg - xh * (jnp.sum(dyg * xh, axis=-1, keepdims=True) * (1.0 / D_MODEL)))
        dgp = (dx3 * pev * sg * (1.0 - sg)).astype(BF16)
        dh3 = lax.dot_general(dgp, w_gate[...], (((1,), (1,)), ((), ())), preferred_element_type=F32)
        dx2v, dgple = norm_res_bwd(dh3, x2v, dx3, gp)
        return (dx2v, dgp, dx3 * sg, jnp.sum(err * err, axis=0, keepdims=True) * (0.5 / D_MODEL),
                jnp.sum(dy * xh, axis=0, keepdims=True), dgple)

    dx2, dgpre, dpe, loss_part, dg_final, dg_ple = _rowwise(
        "tail", tail, [full(x2), full(h3), full(p2), full(tg)], [wide(F32), wide(BF16), wide(BF16)],
        consts=[gains["g_final"].reshape(1, D_MODEL), gains["g_ple"], wpg, wple], accs=[(1, D_MODEL)] * 3, tm=256)

    dw = {}
    dw["w_ple"] = _mm_tn("dw_ple", p2, dpe)
    dw["w_ple_gate"] = _mm_tn("dw_ple_gate", h3, dgpre)
    dw["w_down"] = _mm_tn("dw_down", u, dx2, a_pre=square)
    dupre = _mm("d_mlp_down", dx2, wdown, trans_b=True, out_dtypes=(BF16,), epi=lambda acc, uv: (acc * (2.0 * uv.astype(F32)),),
                extras=(u,), tn=2048)
    dw["w_up"] = _mm_tn("dw_up", h2, dupre, out_slots=True)
    n_up = wup.shape[0]
    dx1, dg_mlp = _mm("d_mlp_up", [(dupre, j, wup.shape[2]) for j in range(n_up)], [(wup, j) for j in range(n_up)], trans_b=True,
                      epi=norm_res_bwd, extras=(x1, dx2), consts=[gains["g_mlp"]],
                      accs=[(1, D_MODEL)], tm=256)
    dw["w_o"] = _mm_tn("dw_o", merged, dx1)

    def merge_bwd(dm, ga, gb, a, b, w_a, w_b):
        sa, sb = _sigmoid(ga), _sigmoid(gb)
        da, db = (dm * sa).astype(BF16), (dm * sb).astype(BF16)
        nt = (((1,), (1,)), ((), ()))
        return (da, db, dm * a * sa * (1.0 - sa), dm * b * sb * (1.0 - sb),
                lax.dot_general(da, w_a, nt, preferred_element_type=F32), lax.dot_general(db, w_b, nt, preferred_element_type=F32))

    dya, dyb, dga, dgb, do_a, do_b = _mm("d_out_proj", dx1, wo, trans_b=True, out_dtypes=(BF16,) * 6, epi=merge_bwd,
                                         extras=((z, ZC_GA // 8), (z, ZC_GB // 8), ya, yb), consts=[woa, wob], tm=256)
    dw["w_oa"] = _unpad_heads_rows(_mm_tn("dw_oa", o_a, dya), H_A, V_DIM_A)
    dw["w_ob"] = _unpad_heads_rows(_mm_tn("dw_ob", o_b, dyb), H_B, HD_B)
    res_a = _attn_bwd("attn_a_bwd", q_a, kv_a, 0, kv_a, H_A, o_a, do_a, lse_a, heads=H_A, group=1, nseq=nb, seq=seq,
                      rider=ride and ride["scatter_a"](dw))
    dq_a, dk_a, dv_a = res_a[:3]
    if ride is not None:
        ride["out"]["parts_a"] = res_a[3:]

    def rope_bwd(scale):
        return lambda d, cos, sin, pm_t: ((d * cos + _perm(d * sin, pm_t)) * scale,)

    nt_dims = (((1,), (1,)), ((), ()))

    def q_path_bwd(dq, zq, cos, sin, g, w, pm_t):
        dqav = _per_head(rope_bwd(SCALE_A), H_A, 1, 1)(dq, cos, sin, pm_t)[0].astype(BF16)
        dcq = lax.dot_general(dqav, w[...], nt_dims, preferred_element_type=F32)
        return (dqav, *_rms_bwd(dcq, zq, g, Q_LORA))

    dqa, dq_lat, dg_qa = _rowwise("q_path_bwd", q_path_bwd, [full(dq_a), z_qlat], [heads_tile(H_A), (Q_LORA, BF16, Q_LORA, False)],
                                  pos=[cq_t, sq_t], consts=[gains["g_qa"], wqb, pq_t], accs=[(1, Q_LORA)], seq=seq)
    dw["w_qb"] = _unpad_heads_cols(_mm_tn("dw_qb", cq, dqa), H_A, QK_NOPE + QK_ROPE)
    dw["w_kvb"] = _wkv_unext(_mm_tn("dw_kv_k", kin, dk_a), _mm_tn("dw_kv_v", kin, dv_a))
    dq_b, dk_b, dv_b, *parts_b = _attn_bwd("attn_b_bwd", q_b, k_b, 0, z, ZC_VB, o_b, do_b, lse_b, heads=H_B, group=H_B // KV_B,
                                               nseq=nb, seq=seq, rider=ride and ride["scatter_b"](dw))
    if ride is not None:
        ride["out"]["parts_b"] = parts_b

    def kv_path_bwd(dk, dv, ckv_raw, cos, sin, g, w, pm_t):
        kv_w = H_A * HEAD_W
        wv = w[...]
        dkin = (lax.dot_general(dk, wv[:, :kv_w], nt_dims, preferred_element_type=F32)
                + lax.dot_general(dv, wv[:, kv_w:], nt_dims, preferred_element_type=F32))
        dckv_raw, dg = _rms_bwd(dkin[:, :HEAD_W], ckv_raw, g, KV_LORA)
        return (dckv_raw, *rope_bwd(1.0)(dkin[:, HEAD_W:], cos, sin, pm_t), dg)

    dckv, dkpe, dg_kva = _rowwise("kv_path_bwd", kv_path_bwd, [full(dk_a), full(dv_a), z_ckv], [heads_tile(1), heads_tile(1)],
                                  pos=[ck_t, sk_t], consts=[gains["g_kva"], wkv, pk_t], accs=[(1, KV_LORA)], seq=seq)

    def prep_bwd(scale):
        def fn(d, v, cos, sin, g, pm_t):
            dyv = (d * cos + _perm(d * sin, pm_t)) * scale
            return _rms_bwd(dyv, v, g, HD_B)
        return fn

    dqb, dg_qn = _rowwise("prep_qb_bwd", _per_head(prep_bwd(SCALE_B), H_B, 2, 1), [full(dq_b), z_qb], [heads_tile(H_B)],
                          pos=[cb_t, sb_t], consts=[g_qn, pb_t], accs=[(1, HEAD_W)], seq=seq)
    dkb, dg_kn = _rowwise("prep_kb_bwd", _per_head(prep_bwd(1.0), KV_B, 2, 1), [full(dk_b), z_kb], [heads_tile(KV_B)],
                          pos=[cb_t, sb_t], consts=[g_kn, pb_t], accs=[(1, HEAD_W)], seq=seq)

    dz = [dqb, jnp.concatenate([dq_lat, dckv, dkpe, dkb, dv_b], axis=1), dga, dgb]
    dw["w_in"] = _win_unext([_mm_tn("dw_in_%d" % j, h, blk) for j, blk in enumerate(dz)])
    dx0, dg_mix, *parts_in = _mm("d_in_proj", dz, [(win, j, D_MODEL) for j in range(4)], trans_b=True, epi=norm_res_bwd, extras=(x0, dx1), consts=[gains["g_mix"]],
                                 accs=[(1, D_MODEL)], tm=256, rider=ride and ride["scatter_in"](dw))
    if ride is not None:
        ride["out"]["parts_in"] = parts_in

    dg = {"g_mix": dg_mix, "g_qa": dg_qa, "g_kva": dg_kva, "g_qn": dg_qn[:, :HD_B], "g_kn": dg_kn[:, :HD_B],
          "g_mlp": dg_mlp, "g_ple": dg_ple, "g_final": dg_final}
    return loss_part, dx0.reshape(nb, seq, D_MODEL), dg, dw


def _pack_small(vals, loss_part=None):
    flat = jnp.concatenate([vals[n].reshape(1, -1) for n, _ in SMALL], axis=1)
    loss = jnp.zeros((1, 8 * 128), F32) if loss_part is None else loss_part
    gap = jnp.zeros((1, LOSS_ROW0 * 128 - SMALL_N), F32)
    return jnp.concatenate([flat, gap, loss], axis=1).reshape(SMALL_ROWS, 128)


def _unpack_small(slab, like):
    flat, out, off = slab.reshape(-1), {}, 0
    for n, k in SMALL:
        out[n] = flat[off:off + k].reshape(like[n].shape)
        off += k
    return out


def kernel(x, p, g_mix, w_in, g_qa, w_qb, g_kva, w_kvb, g_qn, g_kn, w_oa, w_ob, w_o, g_mlp, w_up, w_down, g_ple, w_ple_gate, w_ple, g_final, loss_target, m_g_mix, m_w_in, m_g_qa, m_w_qb, m_g_kva, m_w_kvb, m_g_qn, m_g_kn, m_w_oa, m_w_ob, m_w_o, m_g_mlp, m_w_up, m_w_down, m_g_ple, m_w_ple_gate, m_w_ple, m_g_final, v_g_mix, v_w_in, v_g_qa, v_w_qb, v_g_kva, v_w_kvb, v_g_qn, v_g_kn, v_w_oa, v_w_ob, v_w_o, v_g_mlp, v_w_up, v_w_down, v_g_ple, v_w_ple_gate, v_w_ple, v_g_final):
    given = dict(locals())
    order = ["g_mix", "w_in", "g_qa", "w_qb", "g_kva", "w_kvb", "g_qn", "g_kn", "w_oa", "w_ob", "w_o", "g_mlp", "w_up",
             "w_down", "g_ple", "w_ple_gate", "w_ple", "g_final"]
    big_names = [n for n, _, _, _ in BIG]
    local = lambda prefix, names: [given[prefix + n][0] for n in names]
    slab = lambda names: _pack_shards(names, local("", names), BF16)
    bf = lambda n: given[n][0].astype(BF16)
    cols_full = lambda g: g.transpose(1, 0, 2).reshape(g.shape[1], -1)
    rows_full = lambda g: g.reshape(-1, g.shape[2])
    shards_cols = lambda a: a.reshape(a.shape[0], 4, a.shape[1] // 4).transpose(1, 0, 2)
    shards_rows = lambda a: a.reshape(4, a.shape[0] // 4, a.shape[1])
    packed = lambda names, dw: _pack_full(names, [dw[n] for n in names], BF16)
    branch_out = ["w_oa", "w_ob"]
    back_a, back_b = SLAB_LATE + ["w_o"], SLAB_EARLY + ["w_ple_gate"]

    got_in, got_early = _gather_by_halves("weight_gather_early", [bf("w_in"), slab(SLAB_EARLY)])
    wts = {"w_in": cols_full(got_in), **dict(zip(SLAB_EARLY, _unpack_full(SLAB_EARLY, got_early)))}
    gains = {n: given[n].reshape(1, -1) for n, _ in SMALL}
    ride = {
        "gather": {
            "in_proj": (_Exchange("gather", [bf("w_o")]), lambda got: {"w_o": rows_full(got[0])}),
            "prep_qb": (_Exchange("gather", [slab(branch_out)]), lambda got: dict(zip(branch_out, _unpack_full(branch_out, got[0])))),
            "attn_a_fwd": (_Exchange("gather", [bf("w_up")]), lambda got: {"w_up": got[0]}),
            "attn_b_fwd": (_Exchange("gather", [bf("w_down")]), lambda got: {"w_down": rows_full(got[0])}),
            "mix_out": (_Exchange("gather", [bf("w_ple_gate"), bf("w_ple")]),
                        lambda got: {"w_ple_gate": rows_full(got[0]), "w_ple": cols_full(got[1])}),
        },
        "scatter_a": lambda dw: _Exchange("scatter", [dw["w_up"], packed(back_a, dw)]),
        "scatter_b": lambda dw: _Exchange("scatter", [shards_rows(dw["w_down"]), packed(back_b, dw)]),
        "scatter_in": lambda dw: _Exchange("scatter", [_presum_halves("grad_presum_in", shards_cols(dw["w_in"]))]),
        "out": {},
    }
    loss_part, grad_x, dg, dw = _local_step(x, p[0], loss_target, gains, wts, ride)

    small = lambda prefix: _pack_small({n: given[prefix + n] for n, _ in SMALL})
    g_s, d_s, m_s, v_s, loss = _small_allreduce_adamw(_pack_small(dg, loss_part), small(""), small("m_"), small("v_"))

    parts = ride["out"]
    state = lambda n: (given[n], given["m_" + n], given["v_" + n])
    done = {"w_up": _reduce_pair("grad_reduce_adamw_up", parts["parts_a"][0], adam=state("w_up")),
            "w_down": _reduce_pair("grad_reduce_adamw_down", parts["parts_b"][0], adam=state("w_down"))}
    grads = {"w_in": _reduce_halves("grad_reduce_in", parts["parts_in"][0])}
    grads.update(zip(back_a, _unpack_shards(back_a, _reduce_pair("grad_reduce_slab_a", parts["parts_a"][1]))))
    grads.update(zip(back_b, _unpack_shards(back_b, _reduce_pair("grad_reduce_slab_b", parts["parts_b"][1]))))

    res = {}
    for key, slab in (("grad_", g_s), ("delta_", d_s), ("new_m_", m_s), ("new_v_", v_s)):
        for n, val in _unpack_small(slab, given).items():
            res[key + n] = val
    for n in big_names:
        res["grad_" + n], res["delta_" + n], res["new_m_" + n], res["new_v_" + n] = done[n] if n in done else _adamw_shard(
            "adamw_" + n, grads[n], *state(n))
    outs = [loss.reshape(()), grad_x]
    for key in ("grad_", "delta_", "new_m_", "new_v_"):
        outs += [res[key + n] for n in order]
    return tuple(outs)
```

```python
import functools

import numpy as np
import jax
import jax.numpy as jnp
from jax import lax
from jax.experimental import pallas as pl
from jax.experimental.pallas import tpu as pltpu

F32 = jnp.float32
BF16 = jnp.bfloat16
MESH = pl.DeviceIdType.MESH

D_MODEL = 1024
GRID_W = 64
ROPE_THETA = 10000.0
EPS = 1e-6
H_A, QK_NOPE, QK_ROPE, V_DIM_A, Q_LORA, KV_LORA = 8, 64, 32, 64, 256, 128
H_B, KV_B, HD_B = 8, 2, 64
D_FF = 4096
PLE_DIM = 256
HEAD_W = 128
SCALE_A = (QK_NOPE + QK_ROPE) ** -0.5
SCALE_B = HD_B ** -0.5

ADAM_LR, ADAM_B1, ADAM_B2, ADAM_EPS, ADAM_WD, ADAM_STEP = 0.001, 0.9, 0.999, 1e-08, 0.01, 10
M_HAT_DIV = 1.0 - ADAM_B1 ** ADAM_STEP
V_HAT_DIV = 1.0 - ADAM_B2 ** ADAM_STEP

VMEM_LIMIT_BYTES = 56 * 1024 * 1024

ZC_QB, ZC_QLAT, ZC_CKV, ZC_KPE, ZC_KB, ZC_VB, ZC_GA, ZC_GB = 0, 8, 10, 11, 12, 14, 16, 24
Z_WIDTH = 32 * HEAD_W

BIG = [
    ("w_in", 1024, 3232, 1), ("w_qb", 256, 768, 1), ("w_kvb", 128, 1024, 1), ("w_oa", 512, 1024, 1),
    ("w_ob", 512, 1024, 1), ("w_o", 1024, 1024, 0), ("w_up", 1024, 4096, 1), ("w_down", 4096, 1024, 0),
    ("w_ple_gate", 1024, 1024, 0), ("w_ple", 256, 1024, 1),
]
BIG_BY_NAME = {e[0]: e for e in BIG}
PACK_W = 1024
PACK_ALIGN = 64
SLAB_EARLY = ["w_qb", "w_kvb"]
SLAB_LATE = ["w_oa", "w_ob", "w_ple"]

SMALL = [("g_mix", 1024), ("g_qa", 256), ("g_kva", 128), ("g_qn", 64), ("g_kn", 64), ("g_mlp", 1024),
         ("g_ple", 1024), ("g_final", 1024)]
SMALL_N = sum(n for _, n in SMALL)
LOSS_ROW0 = 40
SMALL_ROWS = 48


def _params(sem):
    return pltpu.CompilerParams(dimension_semantics=sem, vmem_limit_bytes=VMEM_LIMIT_BYTES)


def _sigmoid(v):
    return 1.0 / (1.0 + jnp.exp(-v.astype(F32)))


def _perm(v, p_ref):
    pm = p_ref[...]
    hi = v.astype(BF16)
    lo = (v - hi.astype(F32)).astype(BF16)
    return (jnp.dot(hi, pm, preferred_element_type=F32) + jnp.dot(lo, pm, preferred_element_type=F32))


def _rms(v, g, n):
    v = v.astype(F32)
    rs = lax.rsqrt(jnp.sum(v * v, axis=-1, keepdims=True) * (1.0 / n) + EPS)
    return v * rs * g


def _rms_bwd(dy, v, g, n):
    v = v.astype(F32)
    rs = lax.rsqrt(jnp.sum(v * v, axis=-1, keepdims=True) * (1.0 / n) + EPS)
    vh = v * rs
    dyg = dy * g
    dx = rs * (dyg - vh * (jnp.sum(dyg * vh, axis=-1, keepdims=True) * (1.0 / n)))
    return dx, jnp.sum(dy * vh, axis=0, keepdims=True)


def _ride(body, grid, rider):
    if rider is None:
        return body, [], [], [], []
    n_x, n_sem = len(rider.srcs), len(rider.scratch)

    def wrapped(*refs):
        ids = [pl.program_id(a) for a in range(len(grid))]
        n_in = len(refs) - n_sem - 2 * n_x - rider.n_core_out - rider.n_core_scratch
        core_in, srcs = refs[:n_in], refs[n_in:n_in + n_x]
        core_out = refs[n_in + n_x:n_in + n_x + rider.n_core_out]
        dsts = refs[n_in + n_x + rider.n_core_out:n_in + 2 * n_x + rider.n_core_out]
        core_scr = refs[n_in + 2 * n_x + rider.n_core_out:len(refs) - n_sem]
        sems = refs[len(refs) - n_sem:]

        @pl.when(functools.reduce(jnp.logical_and, [a == 0 for a in ids]))
        def _():
            rider.start(srcs, dsts, *sems)

        body(*core_in, *core_out, *core_scr)

        @pl.when(functools.reduce(jnp.logical_and, [a == n - 1 for a, n in zip(ids, grid)]))
        def _():
            rider.finish(srcs, dsts, *sems)

    hbm = pl.BlockSpec(memory_space=pl.ANY)
    return wrapped, list(rider.srcs), [hbm] * n_x, list(rider.out_shapes), list(rider.scratch)


def _mm(name, a, b, *, trans_b=False, b_slots=False, a_pre=None, out_dtypes=(F32,), epi=None, extras=(), consts=(), accs=(), tm=512,
        tn=None, rider=None):
    a_ops = [o if isinstance(o, tuple) else (o, 0, o.shape[1]) for o in (a if isinstance(a, list) else [a])]
    b_ops = b if isinstance(b, list) else [b]
    assert len(a_ops) == len(b_ops) and not (b_slots and (trans_b or len(b_ops) > 1))
    m = a_ops[0][0].shape[0]
    if b_slots:
        n, tn = b.shape[0] * b.shape[2], b.shape[2]
    else:
        first = b_ops[0][0] if isinstance(b_ops[0], tuple) else b_ops[0]
        n = first.shape[-2] if trans_b else first.shape[1]
        tn = n if tn is None else min(tn, n)
    tm = min(tm, m)
    assert m % tm == 0 and n % tn == 0
    extras = [e if isinstance(e, tuple) else (e, 0) for e in extras]
    n_p, n_ex, n_c, n_out, n_acc = len(a_ops), len(extras), len(consts), len(out_dtypes), len(accs)
    dims = (((1,), (1,)), ((), ())) if trans_b else (((1,), (0,)), ((), ()))

    def body(*refs):
        acc = None
        for a_ref, b_ref in zip(refs[:n_p], refs[n_p:2 * n_p]):
            av = a_ref[...] if a_pre is None else a_pre(a_ref[...].astype(F32))
            part = lax.dot_general(av.astype(BF16), b_ref[...].astype(BF16), dims, preferred_element_type=F32)
            acc = part if acc is None else acc + part
        rest = refs[2 * n_p:]
        res = (acc,) if epi is None else epi(acc, *[e[...] for e in rest[:n_ex + n_c]])
        o_refs = rest[n_ex + n_c:]
        for o_ref, r in zip(o_refs[:n_out], res[:n_out]):
            o_ref[...] = r.astype(o_ref.dtype)
        if n_acc:
            first_step = jnp.logical_and(pl.program_id(0) == 0, pl.program_id(1) == 0)

            @pl.when(first_step)
            def _():
                for o_ref, r in zip(o_refs[n_out:], res[n_out:]):
                    o_ref[...] = r

            @pl.when(jnp.logical_not(first_step))
            def _():
                for o_ref, r in zip(o_refs[n_out:], res[n_out:]):
                    o_ref[...] += r

    def b_spec(op, k_i):
        if b_slots:
            return pl.BlockSpec((None, k_i, tn), lambda j, i: (j, 0, 0))
        if not isinstance(op, tuple):
            return pl.BlockSpec((tn, k_i), lambda j, i: (j, 0)) if trans_b else pl.BlockSpec((k_i, tn), lambda j, i: (0, j))
        assert trans_b
        if len(op) == 2:
            return pl.BlockSpec((None, tn, k_i), lambda j, i, slot=op[1]: (slot, j, 0))
        return pl.BlockSpec((tn, k_i), lambda j, i, blk=op[1]: (j, blk))

    grid = (n // tn, m // tm)
    if rider is not None:
        rider.n_core_out, rider.n_core_scratch = n_out + n_acc, 0
    body, x_in, x_spec, x_out, x_scr = _ride(body, grid, rider)
    a_specs = [pl.BlockSpec((tm, k_i), lambda j, i, blk=blk: (i, blk)) for _, blk, k_i in a_ops]
    b_specs = [b_spec(op, k_i) for op, (_, _, k_i) in zip(b_ops, a_ops)]
    t_spec = pl.BlockSpec((tm, tn), lambda j, i: (i, j))
    e_specs = [pl.BlockSpec((tm, tn), lambda j, i, off=off: (i, j + off)) for _, off in extras]
    c_specs = [pl.BlockSpec(c.shape, lambda j, i: (0, 0)) for c in consts]
    acc_specs = [pl.BlockSpec(sh, lambda j, i: (0, 0)) for sh in accs]
    sem = ("parallel", "parallel") if rider is None and not n_acc else ("arbitrary", "arbitrary")
    outs = pl.pallas_call(
        body, out_shape=[jax.ShapeDtypeStruct((m, n), d) for d in out_dtypes] + [jax.ShapeDtypeStruct(sh, F32) for sh in accs] + x_out,
        grid=grid, in_specs=a_specs + b_specs + e_specs + c_specs + x_spec, out_specs=[t_spec] * n_out + acc_specs + x_spec,
        scratch_shapes=x_scr, compiler_params=_params(sem),
        name=name)(*[o[0] for o in a_ops], *[o[0] if isinstance(o, tuple) else o for o in b_ops], *[e for e, _ in extras], *consts, *x_in)
    return outs[0] if len(outs) == 1 else outs


def _per_head(fn, heads, n_tiled, n_out):
    def run(*args):
        res = [fn(*[a[:, hd * HEAD_W:(hd + 1) * HEAD_W] for a in args[:n_tiled]], *args[n_tiled:]) for hd in range(heads)]
        tiles = [jnp.concatenate([r[k] for r in res], axis=1) for k in range(n_out)]
        sums = [functools.reduce(lambda u, v: u + v, [r[k] for r in res]) for k in range(n_out, len(res[0]))]
        return (*tiles, *sums)
    return run


def _mm_tn(name, a, b, *, a_pre=None, out_dtype=BF16, out_slots=False, tk=1024, tn=1024, tt=4096):
    t, k = a.shape
    n = b.shape[1]
    tk, tn = min(tk, k), min(tn, n)
    if a.dtype == F32 or b.dtype == F32:
        tt = tt // 2
    if k == tk and n == tn:
        tt = tt // 2
    tt = min(tt, t)
    assert b.shape[0] == t and k % tk == 0 and n % tn == 0 and t % tt == 0
    nt = t // tt

    def body(a_ref, b_ref, o_ref, acc):
        av = a_ref[...] if a_pre is None else a_pre(a_ref[...].astype(F32))
        part = lax.dot_general(av.astype(BF16), b_ref[...].astype(BF16), (((0,), (0,)), ((), ())), preferred_element_type=F32)

        @pl.when(pl.program_id(2) == 0)
        def _():
            acc[...] = part

        @pl.when(pl.program_id(2) != 0)
        def _():
            acc[...] += part

        @pl.when(pl.program_id(2) == nt - 1)
        def _():
            o_ref[...] = acc[...].astype(o_ref.dtype)

    if out_slots:
        out_shape, out_spec = (n // tn, k, tn), pl.BlockSpec((None, tk, tn), lambda ki, ni, ti: (ni, ki, 0))
    else:
        out_shape, out_spec = (k, n), pl.BlockSpec((tk, tn), lambda ki, ni, ti: (ki, ni))
    return pl.pallas_call(
        body, out_shape=jax.ShapeDtypeStruct(out_shape, out_dtype), grid=(k // tk, n // tn, nt),
        in_specs=[pl.BlockSpec((tt, tk), lambda ki, ni, ti: (ti, ki)), pl.BlockSpec((tt, tn), lambda ki, ni, ti: (ti, ni))],
        out_specs=out_spec, scratch_shapes=[pltpu.VMEM((tk, tn), F32)],
        compiler_params=_params(("parallel", "parallel", "arbitrary")), name=name)(a, b)


def _rowwise(name, fn, ins, outs, *, consts=(), pos=(), accs=(), heads=1, tm=512, seq=None, rider=None):
    t = ins[0][0].shape[0]
    tm = min(tm, t if seq is None else seq)
    assert t % tm == 0 and (seq is None or seq % tm == 0)
    n_in, n_pos, n_c, n_out, n_acc = len(ins), len(pos), len(consts), len(outs), len(accs)

    def body(*refs):
        vals = [r[...] for r in refs[:n_in + n_pos + n_c]]
        res = fn(*vals)
        o_refs = refs[n_in + n_pos + n_c:]
        for o_ref, r in zip(o_refs[:n_out], res[:n_out]):
            o_ref[...] = r.astype(o_ref.dtype)
        if n_acc:
            first = jnp.logical_and(pl.program_id(0) == 0, pl.program_id(1) == 0)

            @pl.when(first)
            def _():
                for o_ref, r in zip(o_refs[n_out:], res[n_out:]):
                    o_ref[...] = r

            @pl.when(jnp.logical_not(first))
            def _():
                for o_ref, r in zip(o_refs[n_out:], res[n_out:]):
                    o_ref[...] += r

    def tiled(width, c0, per_head):
        return pl.BlockSpec((tm, width), (lambda h, i: (i, c0 + h)) if per_head else (lambda h, i: (i, c0)))

    in_specs = [tiled(w, c0, ph) for _, w, c0, ph in ins]
    if n_pos:
        nblk = seq // tm
        in_specs += [pl.BlockSpec((tm, a.shape[1]), lambda h, i: (i % nblk, 0)) for a in pos]
    in_specs += [pl.BlockSpec(a.shape, lambda h, i: (0, 0)) for a in consts]
    out_specs = [tiled(w, 0, ph) for _, _, w, ph in outs] + [pl.BlockSpec(s, lambda h, i: (0, 0)) for s in accs]
    out_shape = [jax.ShapeDtypeStruct((t, c), d) for c, d, _, _ in outs] + [jax.ShapeDtypeStruct(s, F32) for s in accs]
    sem = ("arbitrary", "arbitrary") if n_acc or rider is not None else ("parallel", "parallel")
    grid = (heads, t // tm)
    if rider is not None:
        rider.n_core_out, rider.n_core_scratch = n_out + n_acc, 0
    body, x_in, x_spec, x_out, x_scr = _ride(body, grid, rider)
    res = pl.pallas_call(body, out_shape=out_shape + x_out, grid=grid, in_specs=in_specs + x_spec, out_specs=out_specs + x_spec,
                         scratch_shapes=x_scr, compiler_params=_params(sem), name=name)(*[a for a, _, _, _ in ins], *pos, *consts, *x_in)
    return res[0] if len(res) == 1 else res


ATTN_HEADS_PER_STEP = 4


def _attn_fwd(name, q, k, kc0, v, vc0, *, heads, group, nseq, seq, tq=512, rider=None):
    tq = min(tq, seq)
    nq = seq // tq
    hp = ATTN_HEADS_PER_STEP
    grid = (heads // hp, nseq, nq)
    shared = group > 1
    assert group % hp == 0 if shared else (kc0 % hp == 0 and vc0 % hp == 0)

    def body(q_ref, k_ref, v_ref, o_ref, lse_ref):
        for j in range(hp):
            cols = slice(j * HEAD_W, (j + 1) * HEAD_W)
            kj = (k_ref[...] if shared else k_ref[:, cols]).astype(BF16)
            vj = (v_ref[...] if shared else v_ref[:, cols]).astype(BF16)
            s = lax.dot_general(q_ref[:, cols], kj, (((1,), (1,)), ((), ())), preferred_element_type=F32)
            m = jnp.max(s, axis=-1, keepdims=True)
            p = jnp.exp((s - m).astype(BF16))
            vj = jnp.where(lax.broadcasted_iota(jnp.int32, (1, HEAD_W), 1) == HEAD_W - 1, jnp.ones((), BF16), vj)
            o = jnp.dot(p, vj, preferred_element_type=F32)
            l = o[:, HEAD_W - 1:]
            o_ref[:, cols] = (o * (1.0 / l)).astype(o_ref.dtype)
            lse_ref[j] = m + jnp.log(l)

    if rider is not None:
        rider.n_core_out, rider.n_core_scratch = 2, 0
    body, x_in, x_spec, x_out, x_scr = _ride(body, grid, rider)
    q_spec = pl.BlockSpec((tq, hp * HEAD_W), lambda h, b, i: (b * nq + i, h))
    if shared:
        k_spec = pl.BlockSpec((seq, HEAD_W), lambda h, b, i: (b, kc0 + (h * hp) // group))
        v_spec = pl.BlockSpec((seq, HEAD_W), lambda h, b, i: (b, vc0 + (h * hp) // group))
    else:
        k_spec = pl.BlockSpec((seq, hp * HEAD_W), lambda h, b, i: (b, kc0 // hp + h))
        v_spec = pl.BlockSpec((seq, hp * HEAD_W), lambda h, b, i: (b, vc0 // hp + h))
    lse_spec = pl.BlockSpec((hp, tq, 1), lambda h, b, i: (h, b * nq + i, 0))
    sem = ("parallel",) * 3 if rider is None else ("arbitrary",) * 3
    return pl.pallas_call(
        body, out_shape=[jax.ShapeDtypeStruct(q.shape, BF16), jax.ShapeDtypeStruct((heads, q.shape[0], 1), F32)] + x_out,
        grid=grid, in_specs=[q_spec, k_spec, v_spec] + x_spec, out_specs=[q_spec, lse_spec] + x_spec, scratch_shapes=x_scr,
        compiler_params=_params(sem), name=name)(q, k, v, *x_in)


def _attn_bwd(name, q, k, kc0, v, vc0, o, do, lse, *, heads, group, nseq, seq, tq=1024, rider=None):
    tq = min(tq, seq)
    nq = seq // tq
    hk = heads // group
    t = q.shape[0]
    grid = (hk, nseq, group, nq)

    def body(q_ref, k_ref, v_ref, o_ref, do_ref, lse_ref, dq_ref, dk_ref, dv_ref, dk_acc, dv_acc):
        g, i = pl.program_id(2), pl.program_id(3)
        qv, kv, vv, dov = q_ref[...], k_ref[...].astype(BF16), v_ref[...].astype(BF16), do_ref[...]
        s = lax.dot_general(qv, kv, (((1,), (1,)), ((), ())), preferred_element_type=F32)
        pn = jnp.exp(s - lse_ref[...])
        dp = lax.dot_general(dov, vv, (((1,), (1,)), ((), ())), preferred_element_type=F32)
        delta = jnp.sum(dov.astype(F32) * o_ref[...].astype(F32), axis=-1, keepdims=True)
        ds = (pn * (dp - delta)).astype(BF16)
        dq_ref[...] = jnp.dot(ds, kv, preferred_element_type=F32)
        dk_part = lax.dot_general(ds, qv, (((0,), (0,)), ((), ())), preferred_element_type=F32)
        dv_part = lax.dot_general(pn.astype(BF16), dov, (((0,), (0,)), ((), ())), preferred_element_type=F32)
        first = jnp.logical_and(g == 0, i == 0)

        @pl.when(first)
        def _():
            dk_acc[...] = dk_part
            dv_acc[...] = dv_part

        @pl.when(jnp.logical_not(first))
        def _():
            dk_acc[...] += dk_part
            dv_acc[...] += dv_part

        @pl.when(jnp.logical_and(g == group - 1, i == nq - 1))
        def _():
            dk_ref[...] = dk_acc[...].astype(dk_ref.dtype)
            dv_ref[...] = dv_acc[...].astype(dv_ref.dtype)

    if rider is not None:
        rider.n_core_out, rider.n_core_scratch = 3, 2
    body, x_in, x_spec, x_out, x_scr = _ride(body, grid, rider)
    q_spec = pl.BlockSpec((tq, HEAD_W), lambda kh, b, g, i: (b * nq + i, kh * group + g))
    kv_out = pl.BlockSpec((seq, HEAD_W), lambda kh, b, g, i: (b, kh))
    lse_spec = pl.BlockSpec((None, tq, 1), lambda kh, b, g, i: (kh * group + g, b * nq + i, 0))
    sem = ("parallel", "parallel", "arbitrary", "arbitrary") if rider is None else ("arbitrary",) * 4
    return pl.pallas_call(
        body,
        out_shape=[jax.ShapeDtypeStruct(q.shape, F32), jax.ShapeDtypeStruct((t, hk * HEAD_W), BF16),
                   jax.ShapeDtypeStruct((t, hk * HEAD_W), BF16)] + x_out,
        grid=grid,
        in_specs=[q_spec, pl.BlockSpec((seq, HEAD_W), lambda kh, b, g, i: (b, kc0 + kh)),
                  pl.BlockSpec((seq, HEAD_W), lambda kh, b, g, i: (b, vc0 + kh)), q_spec, q_spec, lse_spec] + x_spec,
        out_specs=[q_spec, kv_out, kv_out] + x_spec,
        scratch_shapes=[pltpu.VMEM((seq, HEAD_W), F32), pltpu.VMEM((seq, HEAD_W), F32)] + x_scr,
        compiler_params=_params(sem), name=name)(q, k, v, o, do, lse, *x_in)


def _place():
    return lax.axis_index("x"), lax.axis_index("y"), lax.axis_index("c")


def _other_chips(x, y):
    return [(1 - x, y), (x, 1 - y), (1 - x, 1 - y)]


class _Exchange:
    def __init__(self, kind, srcs):
        assert kind in ("gather", "scatter")
        self.kind, self.srcs = kind, list(srcs)
        n = len(self.srcs)
        self.out_shapes = [jax.ShapeDtypeStruct((4, *a.shape[-2:]), a.dtype) for a in self.srcs]
        self.scratch = [pltpu.SemaphoreType.DMA((3 * n,)), pltpu.SemaphoreType.DMA((3 * n,)), pltpu.SemaphoreType.DMA((n,))]
        self.n_core_out = self.n_core_scratch = 0

    def _copies(self, j, src_ref, out_ref, send_sems, recv_sems, landing):
        x, y, c = _place()

        def remote(k, s, d, to):
            return pltpu.make_async_remote_copy(src_ref=s, dst_ref=d, send_sem=send_sems.at[3 * j + k], recv_sem=recv_sems.at[3 * j + k],
                                                device_id=to, device_id_type=MESH)

        me = 2 * x + y
        part = (lambda i: src_ref) if self.kind == "gather" else (lambda i: src_ref.at[i])
        if landing:
            return [remote(k, part(me), out_ref.at[2 * px + py], (px, py, c)) for k, (px, py) in enumerate(_other_chips(x, y))]
        return [remote(k, part(2 * px + py), out_ref.at[me], (px, py, c)) for k, (px, py) in enumerate(_other_chips(x, y))]

    def _local(self, j, src_ref, out_ref, local_sems):
        x, y, _ = _place()
        me = 2 * x + y
        return pltpu.make_async_copy(src_ref if self.kind == "gather" else src_ref.at[me], out_ref.at[me], local_sems.at[j])

    def start(self, src_refs, out_refs, send_sems, recv_sems, local_sems):
        for j, (src_ref, out_ref) in enumerate(zip(src_refs, out_refs)):
            self._local(j, src_ref, out_ref, local_sems).start()
            for mine in self._copies(j, src_ref, out_ref, send_sems, recv_sems, False):
                mine.start()

    def finish(self, src_refs, out_refs, send_sems, recv_sems, local_sems):
        for j, (src_ref, out_ref) in enumerate(zip(src_refs, out_refs)):
            for landed in self._copies(j, src_ref, out_ref, send_sems, recv_sems, True):
                landed.wait_recv()
        for j, (src_ref, out_ref) in enumerate(zip(src_refs, out_refs)):
            for mine in self._copies(j, src_ref, out_ref, send_sems, recv_sems, False):
                mine.wait_send()
            self._local(j, src_ref, out_ref, local_sems).wait()


def _gather_by_halves(name, srcs):
    n = len(srcs)

    def body(*refs):
        x, y, c = _place()
        me = 2 * x + y
        local_sems = refs[-1]
        copies = []
        for j in range(n):
            src_ref, out_ref, send_sems, recv_sems = refs[j], refs[n + j], refs[2 * n + 2 * j], refs[2 * n + 2 * j + 1]
            half = srcs[j].shape[0] // 2
            rows_c = pl.ds(pl.multiple_of(c * half, half), half)
            rows_s = pl.ds(pl.multiple_of((1 - c) * half, half), half)

            def remote(k, s_ref, d_ref, to, send_sems=send_sems, recv_sems=recv_sems):
                return pltpu.make_async_remote_copy(src_ref=s_ref, dst_ref=d_ref, send_sem=send_sems.at[k], recv_sem=recv_sems.at[k],
                                                    device_id=to, device_id_type=MESH)

            local = pltpu.make_async_copy(src_ref, out_ref.at[me], local_sems.at[j])
            local.start()
            chips = _other_chips(x, y)
            sent = [remote(k, src_ref.at[rows_c], out_ref.at[me, rows_c], (px, py, c)) for k, (px, py) in enumerate(chips)]
            landing = [remote(k, src_ref.at[rows_c], out_ref.at[2 * px + py, rows_c], (px, py, c)) for k, (px, py) in enumerate(chips)]
            passed = [remote(3 + k, out_ref.at[2 * px + py, rows_c], out_ref.at[2 * px + py, rows_c], (x, y, 1 - c))
                      for k, (px, py) in enumerate(chips)]
            from_sibling = [remote(3 + k, out_ref.at[2 * px + py, rows_s], out_ref.at[2 * px + py, rows_s], (x, y, 1 - c))
                            for k, (px, py) in enumerate(chips)]
            for cp in sent:
                cp.start()
            copies.append((local, sent, landing, passed, from_sibling))
        for local, sent, landing, passed, from_sibling in copies:
            for k in range(3):
                landing[k].wait_recv()
                passed[k].start()
        for local, sent, landing, passed, from_sibling in copies:
            for k in range(3):
                from_sibling[k].wait_recv()
            for cp in sent + passed:
                cp.wait_send()
            local.wait()

    sems = [pltpu.SemaphoreType.DMA((6,)) for _ in range(2 * n)] + [pltpu.SemaphoreType.DMA((n,))]
    return pl.pallas_call(
        body, out_shape=[jax.ShapeDtypeStruct((4, *a.shape), a.dtype) for a in srcs],
        in_specs=[pl.BlockSpec(memory_space=pl.ANY)] * n, out_specs=[pl.BlockSpec(memory_space=pltpu.VMEM)] * n,
        scratch_shapes=sems, compiler_params=pltpu.CompilerParams(vmem_limit_bytes=VMEM_LIMIT_BYTES), name=name)(*srcs)


def _adamw(w, g, m, v):
    m = ADAM_B1 * m + (1.0 - ADAM_B1) * g
    v = ADAM_B2 * v + (1.0 - ADAM_B2) * (g * g)
    delta = -ADAM_LR * ((m / M_HAT_DIV) / (jnp.sqrt(v / V_HAT_DIV) + ADAM_EPS) + ADAM_WD * w)
    return delta, m, v


def _small_allreduce_adamw(part, w, m, v):
    def body(part_ref, w_ref, m_ref, v_ref, g_out, d_out, m_out, v_out, loss_out, buf, send_sems, recv_sems):
        x, y, c = _place()
        me = 4 * x + 2 * y + c
        buf[me] = part_ref[...]

        def flip(k):
            fx, fy, fc = (k >> 2) & 1, (k >> 1) & 1, k & 1
            px, py, pc = (1 - x if fx else x), (1 - y if fy else y), (1 - c if fc else c)
            return (px, py, pc), 4 * px + 2 * py + pc

        def copy(k, slot):
            return pltpu.make_async_remote_copy(
                src_ref=part_ref, dst_ref=buf.at[slot], send_sem=send_sems.at[k - 1], recv_sem=recv_sems.at[k - 1],
                device_id=flip(k)[0], device_id_type=MESH)

        sent = [copy(k, me) for k in range(1, 8)]
        for cp in sent:
            cp.start()
        for k in range(1, 8):
            copy(k, flip(k)[1]).wait_recv()
        for cp in sent:
            cp.wait_send()
        tot = buf[0]
        for j in range(1, 8):
            tot = tot + buf[j]
        delta, m_new, v_new = _adamw(w_ref[...], tot, m_ref[...], v_ref[...])
        g_out[...] = tot
        d_out[...] = delta
        m_out[...] = m_new
        v_out[...] = v_new
        loss_out[...] = jnp.sum(tot[LOSS_ROW0:LOSS_ROW0 + 8, :]).reshape(1, 1)

    vm = pl.BlockSpec(memory_space=pltpu.VMEM)
    shp = jax.ShapeDtypeStruct((SMALL_ROWS, 128), F32)
    return pl.pallas_call(
        body, out_shape=[shp, shp, shp, shp, jax.ShapeDtypeStruct((1, 1), F32)],
        in_specs=[vm, vm, vm, vm], out_specs=[vm, vm, vm, vm, vm],
        scratch_shapes=[pltpu.VMEM((8, SMALL_ROWS, 128), F32), pltpu.SemaphoreType.DMA((7,)), pltpu.SemaphoreType.DMA((7,))],
        name="small_allreduce_adamw")(part, w, m, v)


def _row_tile(rows, cap):
    return max(t for t in range(16, min(rows, cap) + 1, 16) if rows % t == 0)


def _reduce_pair(name, parts):
    _, rows, w = parts.shape
    tr = _row_tile(rows, 576)
    nt = rows // tr

    def body(p_ref, o_ref, mine, theirs, send_sems, recv_sems):
        i = pl.program_id(0)
        x, y, c = _place()

        def copy(t):
            rows_t = pl.ds(pl.multiple_of(t * tr, tr), tr)
            return pltpu.make_async_remote_copy(src_ref=mine.at[rows_t], dst_ref=theirs.at[rows_t], send_sem=send_sems.at[t],
                                                recv_sem=recv_sems.at[t], device_id=(x, y, 1 - c), device_id_type=MESH)

        @pl.when(i < nt)
        def _():
            mine[pl.ds(pl.multiple_of(i * tr, tr), tr), :] = (
                (p_ref[0].astype(F32) + p_ref[1].astype(F32)) + p_ref[2].astype(F32)) + p_ref[3].astype(F32)
            copy(i).start()

        @pl.when(i >= nt)
        def _():
            t = i - nt
            copy(t).wait()
            rows_t = pl.ds(pl.multiple_of(t * tr, tr), tr)
            o_ref[...] = mine[rows_t, :] + theirs[rows_t, :]

    return pl.pallas_call(
        body, out_shape=jax.ShapeDtypeStruct((rows, w), F32), grid=(2 * nt,),
        in_specs=[pl.BlockSpec((4, tr, w), lambda i: (0, jnp.minimum(i, nt - 1), 0))],
        out_specs=pl.BlockSpec((tr, w), lambda i: (jnp.maximum(i - nt, 0), 0)),
        scratch_shapes=[pltpu.VMEM((rows, w), F32), pltpu.VMEM((rows, w), F32), pltpu.SemaphoreType.DMA((nt,)),
                        pltpu.SemaphoreType.DMA((nt,))],
        compiler_params=_params(("arbitrary",)), name=name)(parts)


def _presum_halves(name, shards):
    _, rows, w = shards.shape
    half = rows // 2

    def body(s_ref, o_ref, theirs, send_sems, recv_sems):
        x, y, c = _place()
        rows_c = pl.ds(pl.multiple_of(c * half, half), half)
        rows_s = pl.ds(pl.multiple_of((1 - c) * half, half), half)
        sent = [pltpu.make_async_remote_copy(src_ref=s_ref.at[j, rows_s], dst_ref=theirs.at[j], send_sem=send_sems.at[j],
                                             recv_sem=recv_sems.at[j], device_id=(x, y, 1 - c), device_id_type=MESH) for j in range(4)]
        for cp in sent:
            cp.start()
        for j, cp in enumerate(sent):
            cp.wait_recv()
            o_ref[j] = (s_ref[j, rows_c, :].astype(F32) + theirs[j].astype(F32)).astype(o_ref.dtype)
        for cp in sent:
            cp.wait_send()

    vm = pl.BlockSpec(memory_space=pltpu.VMEM)
    return pl.pallas_call(
        body, out_shape=jax.ShapeDtypeStruct((4, half, w), shards.dtype), in_specs=[vm], out_specs=vm,
        scratch_shapes=[pltpu.VMEM((4, half, w), shards.dtype), pltpu.SemaphoreType.DMA((4,)), pltpu.SemaphoreType.DMA((4,))],
        compiler_params=pltpu.CompilerParams(vmem_limit_bytes=VMEM_LIMIT_BYTES), name=name)(shards)


def _reduce_halves(name, parts):
    _, half, w = parts.shape

    def body(p_ref, o_ref, mine, send_sem, recv_sem):
        x, y, c = _place()
        rows_c = pl.ds(pl.multiple_of(c * half, half), half)
        rows_s = pl.ds(pl.multiple_of((1 - c) * half, half), half)
        mine[...] = ((p_ref[0].astype(F32) + p_ref[1].astype(F32)) + p_ref[2].astype(F32)) + p_ref[3].astype(F32)
        send = pltpu.make_async_remote_copy(src_ref=mine, dst_ref=o_ref.at[rows_c], send_sem=send_sem, recv_sem=recv_sem,
                                            device_id=(x, y, 1 - c), device_id_type=MESH)
        send.start()
        o_ref[rows_c, :] = mine[...]
        pltpu.make_async_remote_copy(src_ref=mine, dst_ref=o_ref.at[rows_s], send_sem=send_sem, recv_sem=recv_sem,
                                     device_id=(x, y, 1 - c), device_id_type=MESH).wait_recv()
        send.wait_send()

    vm = pl.BlockSpec(memory_space=pltpu.VMEM)
    return pl.pallas_call(
        body, out_shape=jax.ShapeDtypeStruct((2 * half, w), F32), in_specs=[vm], out_specs=vm,
        scratch_shapes=[pltpu.VMEM((half, w), F32), pltpu.SemaphoreType.DMA(()), pltpu.SemaphoreType.DMA(())],
        compiler_params=pltpu.CompilerParams(vmem_limit_bytes=VMEM_LIMIT_BYTES), name=name)(parts)


def _adamw_shard(name, g, w, m, v):
    _, rows, cols = w.shape
    tr = _row_tile(rows, 256)

    def body(g_ref, w_ref, m_ref, v_ref, g_out, d_out, m_out, v_out):
        gv = g_ref[...]
        delta, m_new, v_new = _adamw(w_ref[...], gv, m_ref[...], v_ref[...])
        g_out[...] = gv
        d_out[...] = delta
        m_out[...] = m_new
        v_out[...] = v_new

    t_spec = pl.BlockSpec((None, tr, cols), lambda i: (0, i, 0))
    shp = jax.ShapeDtypeStruct((1, rows, cols), F32)
    return pl.pallas_call(body, out_shape=[shp] * 4, grid=(rows // tr,), in_specs=[pl.BlockSpec((tr, cols), lambda i: (i, 0))] + [t_spec] * 3,
                          out_specs=[t_spec] * 4, compiler_params=_params(("parallel",)), name=name)(g, w, m, v)


def _shard_shape(name):
    _, r, c, ax = BIG_BY_NAME[name]
    return (r, c // 4) if ax == 1 else (r // 4, c)


def _pad_rows(a, axis):
    pad = [(0, 0)] * a.ndim
    pad[axis] = (0, -a.shape[axis] % PACK_ALIGN)
    return jnp.pad(a, pad)


def _pack_shards(names, shards, dtype):
    return _pad_rows(jnp.concatenate([s.astype(dtype).reshape(-1, PACK_W) for s in shards], axis=0), 0)


def _unpack_shards(names, slab):
    out, off = [], 0
    for name in names:
        rs, cs = _shard_shape(name)
        n = rs * cs // PACK_W
        out.append(slab[off:off + n].reshape(rs, cs))
        off += n
    return out


def _unpack_full(names, slabs):
    out, off = [], 0
    for name in names:
        _, r, c, ax = BIG_BY_NAME[name]
        n = r * c // 4 // PACK_W
        seg = slabs[:, off:off + n]
        out.append(seg.reshape(4, r, c // 4).transpose(1, 0, 2).reshape(r, c) if ax == 1 else seg.reshape(r, c))
        off += n
    return out


def _pack_full(names, mats, dtype):
    segs = []
    for name, a in zip(names, mats):
        _, r, c, ax = BIG_BY_NAME[name]
        a = a.astype(dtype)
        a = a.reshape(r, 4, c // 4).transpose(1, 0, 2) if ax == 1 else a
        segs.append(a.reshape(4, -1, PACK_W))
    return _pad_rows(jnp.concatenate(segs, axis=1), 1)


def _pad_heads_cols(wm, heads, d):
    k = wm.shape[0]
    return jnp.pad(wm.reshape(k, heads, d), ((0, 0), (0, 0), (0, HEAD_W - d))).reshape(k, heads * HEAD_W)


def _unpad_heads_cols(wm, heads, d):
    k = wm.shape[0]
    return wm.reshape(k, heads, HEAD_W)[:, :, :d].reshape(k, heads * d)


def _win_ext(w_in):
    o = np.cumsum([0, Q_LORA, KV_LORA, QK_ROPE, H_B * HD_B, KV_B * HD_B, KV_B * HD_B, D_MODEL, D_MODEL])
    pc = lambda a, n: jnp.pad(a, ((0, 0), (0, n - a.shape[1])))
    return jnp.concatenate([
        _pad_heads_cols(w_in[:, o[3]:o[4]], H_B, HD_B), w_in[:, o[0]:o[1]], w_in[:, o[1]:o[2]], pc(w_in[:, o[2]:o[3]], HEAD_W),
        _pad_heads_cols(w_in[:, o[4]:o[5]], KV_B, HD_B), _pad_heads_cols(w_in[:, o[5]:o[6]], KV_B, HD_B),
        w_in[:, o[6]:o[7]], w_in[:, o[7]:o[8]]], axis=1)


def _win_unext(blocks):
    c = HEAD_W
    qb, mid, ga, gb = blocks
    at = lambda zc: (zc - ZC_QLAT) * c
    return jnp.concatenate([
        mid[:, at(ZC_QLAT):at(ZC_CKV)], mid[:, at(ZC_CKV):at(ZC_KPE)], mid[:, at(ZC_KPE):at(ZC_KPE) + QK_ROPE],
        _unpad_heads_cols(qb, H_B, HD_B), _unpad_heads_cols(mid[:, at(ZC_KB):at(ZC_VB)], KV_B, HD_B),
        _unpad_heads_cols(mid[:, at(ZC_VB):at(ZC_GA)], KV_B, HD_B), ga, gb], axis=1)


def _wkv_ext(w_kvb):
    wk = w_kvb.reshape(KV_LORA, H_A, QK_NOPE + V_DIM_A)
    k_cols = jnp.pad(wk[:, :, :QK_NOPE], ((0, 0), (0, 0), (0, HEAD_W - QK_NOPE))).reshape(KV_LORA, H_A * HEAD_W)
    v_cols = jnp.pad(wk[:, :, QK_NOPE:], ((0, 0), (0, 0), (0, HEAD_W - V_DIM_A))).reshape(KV_LORA, H_A * HEAD_W)
    eye = jnp.pad(jnp.eye(QK_ROPE, dtype=w_kvb.dtype), ((0, 0), (QK_NOPE, HEAD_W - QK_NOPE - QK_ROPE)))
    pe_rows = jnp.concatenate([jnp.tile(eye, (1, H_A)), jnp.zeros((QK_ROPE, H_A * HEAD_W), w_kvb.dtype)], axis=1)
    top = jnp.concatenate([k_cols, v_cols], axis=1)
    return jnp.concatenate([top, pe_rows, jnp.zeros((2 * HEAD_W - KV_LORA - QK_ROPE, 2 * H_A * HEAD_W), w_kvb.dtype)], axis=0)


def _wkv_unext(k_block, v_block):
    k_cols = k_block[:KV_LORA].reshape(KV_LORA, H_A, HEAD_W)[:, :, :QK_NOPE]
    v_cols = v_block[:KV_LORA].reshape(KV_LORA, H_A, HEAD_W)[:, :, :V_DIM_A]
    return jnp.concatenate([k_cols, v_cols], axis=2).reshape(KV_LORA, H_A * (QK_NOPE + V_DIM_A))


def _pad_heads_rows(wm, heads, d):
    n = wm.shape[1]
    return jnp.pad(wm.reshape(heads, d, n), ((0, 0), (0, HEAD_W - d), (0, 0))).reshape(heads * HEAD_W, n)


def _unpad_heads_rows(wm, heads, d):
    n = wm.shape[1]
    return wm.reshape(heads, HEAD_W, n)[:, :d].reshape(heads * d, n)


def _rope_tables(seq):
    def ang(pos, dim):
        inv = np.float32(ROPE_THETA) ** (-np.arange(0, dim, 2, dtype=np.float32) / np.float32(dim))
        return pos.astype(np.float32)[:, None] * inv[None, :]

    def rot(dim):
        r = np.zeros((dim, dim), np.float32)
        half = dim // 2
        r[np.arange(half) + half, np.arange(half)] = -1.0
        r[np.arange(half), np.arange(half) + half] = 1.0
        return r

    def table(blocks):
        cos, sin = np.ones((seq, HEAD_W), np.float32), np.zeros((seq, HEAD_W), np.float32)
        pm = np.zeros((HEAD_W, HEAD_W), np.float32)
        for c0, a in blocks:
            d = 2 * a.shape[1]
            cos[:, c0:c0 + d] = np.concatenate([np.cos(a), np.cos(a)], axis=1)
            sin[:, c0:c0 + d] = np.concatenate([np.sin(a), np.sin(a)], axis=1)
            pm[c0:c0 + d, c0:c0 + d] = rot(d)
        return jnp.asarray(cos), jnp.asarray(sin), jnp.asarray(pm, BF16), jnp.asarray(pm.T, BF16)

    tok = np.arange(seq)
    a1 = ang(tok, QK_ROPE)
    arow, acol = ang(tok // GRID_W, HD_B // 2), ang(tok % GRID_W, HD_B // 2)
    return table([(QK_NOPE, a1)]), table([(0, a1)]), table([(0, arow), (HD_B // 2, acol)])


def _local_step(x, p, tgt, gains, wts, ride=None):
    nb, seq, _ = x.shape
    t = nb * seq
    x0 = x.reshape(t, D_MODEL)
    p2 = p.reshape(t, PLE_DIM)
    tg = tgt.reshape(t, D_MODEL)
    (cq_t, sq_t, pq, pq_t), (ck_t, sk_t, pk, pk_t), (cb_t, sb_t, pb, pb_t) = _rope_tables(seq)
    padg = lambda g: jnp.pad(g, ((0, 0), (0, HEAD_W - g.shape[1])))
    g_qn, g_kn = padg(gains["g_qn"]), padg(gains["g_kn"])

    win = _win_ext(wts["w_in"])
    wqb = _pad_heads_cols(wts["w_qb"], H_A, QK_NOPE + QK_ROPE)
    wkv = _wkv_ext(wts["w_kvb"])

    norm = lambda n: (lambda v, g: (_rms(v, g, n),))
    full = lambda a: (a, a.shape[1], 0, False)
    wts = dict(wts)
    rider_of = lambda kernel_name: None if ride is None else ride["gather"][kernel_name][0]

    def landed(kernel_name, got):
        if ride is not None:
            wts.update(ride["gather"][kernel_name][1](got))

    h = _rowwise("norm_mix", norm(D_MODEL), [full(x0)], [(D_MODEL, BF16, D_MODEL, False)], consts=[gains["g_mix"]])
    res = _mm("in_proj", h, win, out_dtypes=(BF16,), tn=2048, rider=rider_of("in_proj"))
    z, got = (res, []) if ride is None else (res[0], res[1:])
    landed("in_proj", got)

    def rope_fwd(scale):
        return lambda v, cos, sin, pm: ((v * cos + _perm(v, pm) * sin) * scale,)

    heads_tile = lambda n: (n * HEAD_W, BF16, n * HEAD_W, False)
    z_qlat, z_ckv, z_kpe = (z, Q_LORA, ZC_QLAT // 2, False), (z, HEAD_W, ZC_CKV, False), (z, HEAD_W, ZC_KPE, False)

    def q_path(zq, cos, sin, g, w, pm):
        cqv = _rms(zq, g, Q_LORA).astype(BF16)
        qa = jnp.dot(cqv, w[...], preferred_element_type=F32)
        return (cqv, *_per_head(rope_fwd(SCALE_A), H_A, 1, 1)(qa, cos, sin, pm))

    cq, q_a = _rowwise("q_path", q_path, [z_qlat], [(Q_LORA, BF16, Q_LORA, False), heads_tile(H_A)],
                       pos=[cq_t, sq_t], consts=[gains["g_qa"], wqb, pq], seq=seq)

    def kv_path(ckv_raw, kpe_raw, cos, sin, g, w, pm):
        kinv = jnp.concatenate([_rms(ckv_raw, g, KV_LORA), *rope_fwd(1.0)(kpe_raw, cos, sin, pm)], axis=1).astype(BF16)
        return kinv, jnp.dot(kinv, w[...], preferred_element_type=F32)

    kin, kv_a = _rowwise("kv_path", kv_path, [z_ckv, z_kpe], [heads_tile(2), heads_tile(2 * H_A)],
                         pos=[ck_t, sk_t], consts=[gains["g_kva"], wkv, pk], seq=seq)
    o_a, lse_a, *got = _attn_fwd("attn_a_fwd", q_a, kv_a, 0, kv_a, H_A, heads=H_A, group=1, nseq=nb, seq=seq,
                                 rider=rider_of("attn_a_fwd"))
    landed("attn_a_fwd", got)

    def prep_fwd(scale):
        def fn(v, cos, sin, g, pm):
            yv = _rms(v, g, HD_B)
            return ((yv * cos + _perm(yv, pm) * sin) * scale,)
        return fn

    z_qb, z_kb = (z, H_B * HEAD_W, ZC_QB // H_B, False), (z, KV_B * HEAD_W, ZC_KB // KV_B, False)
    res = _rowwise("prep_qb", _per_head(prep_fwd(SCALE_B), H_B, 1, 1), [z_qb], [heads_tile(H_B)],
                   pos=[cb_t, sb_t], consts=[g_qn, pb], seq=seq, rider=rider_of("prep_qb"))
    q_b, got = (res, []) if ride is None else (res[0], res[1:])
    landed("prep_qb", got)
    k_b = _rowwise("prep_kb", _per_head(prep_fwd(1.0), KV_B, 1, 1), [z_kb], [heads_tile(KV_B)],
                   pos=[cb_t, sb_t], consts=[g_kn, pb], seq=seq)
    o_b, lse_b, *got = _attn_fwd("attn_b_fwd", q_b, k_b, 0, z, ZC_VB, heads=H_B, group=H_B // KV_B, nseq=nb, seq=seq,
                                 rider=rider_of("attn_b_fwd"))
    landed("attn_b_fwd", got)
    woa = _pad_heads_rows(wts["w_oa"], H_A, V_DIM_A)
    wob = _pad_heads_rows(wts["w_ob"], H_B, HD_B)
    wo, wup, wdown = wts["w_o"], wts["w_up"], wts["w_down"]

    def residual_norm(acc, r, g):
        xv = r + acc
        return xv, _rms(xv, g, D_MODEL)

    def mix_out(oa, ob, ga, gb, r, g, w_a, w_b, w_out):
        a = jnp.dot(oa, w_a[...], preferred_element_type=F32)
        b = jnp.dot(ob, w_b[...], preferred_element_type=F32)
        mg = (_sigmoid(ga) * a + _sigmoid(gb) * b).astype(BF16)
        return (a, b, mg, *residual_norm(jnp.dot(mg, w_out[...], preferred_element_type=F32), r, g))

    z_ga, z_gb = (z, D_MODEL, ZC_GA // 8, False), (z, D_MODEL, ZC_GB // 8, False)
    wide = lambda d: (D_MODEL, d, D_MODEL, False)
    ya, yb, merged, x1, h2, *got = _rowwise("mix_out", mix_out, [full(o_a), full(o_b), z_ga, z_gb, full(x0)],
                                            [wide(BF16), wide(BF16), wide(BF16), wide(F32), wide(BF16)],
                                            consts=[gains["g_mlp"], woa, wob, wo], tm=256, rider=rider_of("mix_out"))
    landed("mix_out", got)
    wpg, wple = wts["w_ple_gate"], wts["w_ple"]

    square = lambda v: v * v
    u = _mm("mlp_up", h2, wup, b_slots=True, out_dtypes=(BF16,), epi=lambda acc: (jnp.maximum(acc, 0.0),), tm=1024)
    x2, h3 = _mm("mlp_down", u, wdown, a_pre=square, out_dtypes=(F32, BF16), epi=residual_norm, extras=(x1,), consts=[gains["g_ple"]])

    def norm_res_bwd(dh, v, res, g):
        dx, dg = _rms_bwd(dh, v, g, D_MODEL)
        return dx + res, dg

    def tail(x2v, h3v, pv, tv, gf, gp, w_gate, w_emb):
        sg = _sigmoid(jnp.dot(h3v, w_gate[...], preferred_element_type=F32))
        pev = jnp.dot(pv.astype(BF16), w_emb[...], preferred_element_type=F32)
        x3 = x2v + sg * pev
        rs = lax.rsqrt(jnp.sum(x3 * x3, axis=-1, keepdims=True) * (1.0 / D_MODEL) + EPS)
        xh = x3 * rs
        err = xh * gf - tv
        dy = err * (1.0 / D_MODEL)
        dyg = dy * gf
        dx3 = rs * (dyg - xh * (jnp.sum(dyg * xh, axis=-1, keepdims=True) * (1.0 / D_MODEL)))
        dgp = (dx3 * pev * sg * (1.0 - sg)).astype(BF16)
        dh3 = lax.dot_general(dgp, w_gate[...], (((1,), (1,)), ((), ())), preferred_element_type=F32)
        dx2v, dgple = norm_res_bwd(dh3, x2v, dx3, gp)
        return (dx2v, dgp, dx3 * sg, jnp.sum(err * err, axis=0, keepdims=True) * (0.5 / D_MODEL),
                jnp.sum(dy * xh, axis=0, keepdims=True), dgple)

    dx2, dgpre, dpe, loss_part, dg_final, dg_ple = _rowwise(
        "tail", tail, [full(x2), full(h3), full(p2), full(tg)], [wide(F32), wide(BF16), wide(BF16)],
        consts=[gains["g_final"].reshape(1, D_MODEL), gains["g_ple"], wpg, wple], accs=[(1, D_MODEL)] * 3, tm=256)

    dw = {}
    dw["w_ple"] = _mm_tn("dw_ple", p2, dpe)
    dw["w_ple_gate"] = _mm_tn("dw_ple_gate", h3, dgpre)
    dw["w_down"] = _mm_tn("dw_down", u, dx2, a_pre=square)
    dupre = _mm("d_mlp_down", dx2, wdown, trans_b=True, out_dtypes=(BF16,), epi=lambda acc, uv: (acc * (2.0 * uv.astype(F32)),),
                extras=(u,), tn=2048)
    dw["w_up"] = _mm_tn("dw_up", h2, dupre, out_slots=True)
    n_up = wup.shape[0]
    dx1, dg_mlp = _mm("d_mlp_up", [(dupre, j, wup.shape[2]) for j in range(n_up)], [(wup, j) for j in range(n_up)], trans_b=True,
                      epi=norm_res_bwd, extras=(x1, dx2), consts=[gains["g_mlp"]],
                      accs=[(1, D_MODEL)], tm=512)
    dw["w_o"] = _mm_tn("dw_o", merged, dx1)

    def merge_bwd(dm, ga, gb, a, b, w_a, w_b):
        sa, sb = _sigmoid(ga), _sigmoid(gb)
        da, db = (dm * sa).astype(BF16), (dm * sb).astype(BF16)
        nt = (((1,), (1,)), ((), ()))
        return (da, db, dm * a * sa * (1.0 - sa), dm * b * sb * (1.0 - sb),
                lax.dot_general(da, w_a, nt, preferred_element_type=F32), lax.dot_general(db, w_b, nt, preferred_element_type=F32))

    dya, dyb, dga, dgb, do_a, do_b = _mm("d_out_proj", dx1, wo, trans_b=True, out_dtypes=(BF16,) * 6, epi=merge_bwd,
                                         extras=((z, ZC_GA // 8), (z, ZC_GB // 8), ya, yb), consts=[woa, wob], tm=256)
    dw["w_oa"] = _unpad_heads_rows(_mm_tn("dw_oa", o_a, dya), H_A, V_DIM_A)
    dw["w_ob"] = _unpad_heads_rows(_mm_tn("dw_ob", o_b, dyb), H_B, HD_B)
    res_a = _attn_bwd("attn_a_bwd", q_a, kv_a, 0, kv_a, H_A, o_a, do_a, lse_a, heads=H_A, group=1, nseq=nb, seq=seq,
                      rider=ride and ride["scatter_a"](dw))
    dq_a, dk_a, dv_a = res_a[:3]
    if ride is not None:
        ride["out"]["parts_a"] = res_a[3:]

    def rope_bwd(scale):
        return lambda d, cos, sin, pm_t: ((d * cos + _perm(d * sin, pm_t)) * scale,)

    nt_dims = (((1,), (1,)), ((), ()))

    def q_path_bwd(dq, zq, cos, sin, g, w, pm_t):
        dqav = _per_head(rope_bwd(SCALE_A), H_A, 1, 1)(dq, cos, sin, pm_t)[0].astype(BF16)
        dcq = lax.dot_general(dqav, w[...], nt_dims, preferred_element_type=F32)
        return (dqav, *_rms_bwd(dcq, zq, g, Q_LORA))

    dqa, dq_lat, dg_qa = _rowwise("q_path_bwd", q_path_bwd, [full(dq_a), z_qlat], [heads_tile(H_A), (Q_LORA, BF16, Q_LORA, False)],
                                  pos=[cq_t, sq_t], consts=[gains["g_qa"], wqb, pq_t], accs=[(1, Q_LORA)], seq=seq)
    dw["w_qb"] = _unpad_heads_cols(_mm_tn("dw_qb", cq, dqa), H_A, QK_NOPE + QK_ROPE)
    dw["w_kvb"] = _wkv_unext(_mm_tn("dw_kv_k", kin, dk_a), _mm_tn("dw_kv_v", kin, dv_a))
    dq_b, dk_b, dv_b, *parts_b = _attn_bwd("attn_b_bwd", q_b, k_b, 0, z, ZC_VB, o_b, do_b, lse_b, heads=H_B, group=H_B // KV_B,
                                               nseq=nb, seq=seq, rider=ride and ride["scatter_b"](dw))
    if ride is not None:
        ride["out"]["parts_b"] = parts_b

    def kv_path_bwd(dk, dv, ckv_raw, cos, sin, g, w, pm_t):
        kv_w = H_A * HEAD_W
        wv = w[...]
        dkin = (lax.dot_general(dk, wv[:, :kv_w], nt_dims, preferred_element_type=F32)
                + lax.dot_general(dv, wv[:, kv_w:], nt_dims, preferred_element_type=F32))
        dckv_raw, dg = _rms_bwd(dkin[:, :HEAD_W], ckv_raw, g, KV_LORA)
        return (dckv_raw, *rope_bwd(1.0)(dkin[:, HEAD_W:], cos, sin, pm_t), dg)

    dckv, dkpe, dg_kva = _rowwise("kv_path_bwd", kv_path_bwd, [full(dk_a), full(dv_a), z_ckv], [heads_tile(1), heads_tile(1)],
                                  pos=[ck_t, sk_t], consts=[gains["g_kva"], wkv, pk_t], accs=[(1, KV_LORA)], seq=seq)

    def prep_bwd(scale):
        def fn(d, v, cos, sin, g, pm_t):
            dyv = (d * cos + _perm(d * sin, pm_t)) * scale
            return _rms_bwd(dyv, v, g, HD_B)
        return fn

    dqb, dg_qn = _rowwise("prep_qb_bwd", _per_head(prep_bwd(SCALE_B), H_B, 2, 1), [full(dq_b), z_qb], [heads_tile(H_B)],
                          pos=[cb_t, sb_t], consts=[g_qn, pb_t], accs=[(1, HEAD_W)], seq=seq)
    dkb, dg_kn = _rowwise("prep_kb_bwd", _per_head(prep_bwd(1.0), KV_B, 2, 1), [full(dk_b), z_kb], [heads_tile(KV_B)],
                          pos=[cb_t, sb_t], consts=[g_kn, pb_t], accs=[(1, HEAD_W)], seq=seq)

    dz = [dqb, jnp.concatenate([dq_lat, dckv, dkpe, dkb, dv_b], axis=1), dga, dgb]
    dw["w_in"] = _win_unext([_mm_tn("dw_in_%d" % j, h, blk) for j, blk in enumerate(dz)])
    dx0, dg_mix, *parts_in = _mm("d_in_proj", dz, [(win, j, D_MODEL) for j in range(4)], trans_b=True, epi=norm_res_bwd, extras=(x0, dx1), consts=[gains["g_mix"]],
                                 accs=[(1, D_MODEL)], tm=256, rider=ride and ride["scatter_in"](dw))
    if ride is not None:
        ride["out"]["parts_in"] = parts_in

    dg = {"g_mix": dg_mix, "g_qa": dg_qa, "g_kva": dg_kva, "g_qn": dg_qn[:, :HD_B], "g_kn": dg_kn[:, :HD_B],
          "g_mlp": dg_mlp, "g_ple": dg_ple, "g_final": dg_final}
    return loss_part, dx0.reshape(nb, seq, D_MODEL), dg, dw


def _pack_small(vals, loss_part=None):
    flat = jnp.concatenate([vals[n].reshape(1, -1) for n, _ in SMALL], axis=1)
    loss = jnp.zeros((1, 8 * 128), F32) if loss_part is None else loss_part
    gap = jnp.zeros((1, LOSS_ROW0 * 128 - SMALL_N), F32)
    return jnp.concatenate([flat, gap, loss], axis=1).reshape(SMALL_ROWS, 128)


def _unpack_small(slab, like):
    flat, out, off = slab.reshape(-1), {}, 0
    for n, k in SMALL:
        out[n] = flat[off:off + k].reshape(like[n].shape)
        off += k
    return out


def kernel(x, p, g_mix, w_in, g_qa, w_qb, g_kva, w_kvb, g_qn, g_kn, w_oa, w_ob, w_o, g_mlp, w_up, w_down, g_ple, w_ple_gate, w_ple, g_final, loss_target, m_g_mix, m_w_in, m_g_qa, m_w_qb, m_g_kva, m_w_kvb, m_g_qn, m_g_kn, m_w_oa, m_w_ob, m_w_o, m_g_mlp, m_w_up, m_w_down, m_g_ple, m_w_ple_gate, m_w_ple, m_g_final, v_g_mix, v_w_in, v_g_qa, v_w_qb, v_g_kva, v_w_kvb, v_g_qn, v_g_kn, v_w_oa, v_w_ob, v_w_o, v_g_mlp, v_w_up, v_w_down, v_g_ple, v_w_ple_gate, v_w_ple, v_g_final):
    given = dict(locals())
    order = ["g_mix", "w_in", "g_qa", "w_qb", "g_kva", "w_kvb", "g_qn", "g_kn", "w_oa", "w_ob", "w_o", "g_mlp", "w_up",
             "w_down", "g_ple", "w_ple_gate", "w_ple", "g_final"]
    big_names = [n for n, _, _, _ in BIG]
    local = lambda prefix, names: [given[prefix + n][0] for n in names]
    slab = lambda names: _pack_shards(names, local("", names), BF16)
    bf = lambda n: given[n][0].astype(BF16)
    cols_full = lambda g: g.transpose(1, 0, 2).reshape(g.shape[1], -1)
    rows_full = lambda g: g.reshape(-1, g.shape[2])
    shards_cols = lambda a: a.reshape(a.shape[0], 4, a.shape[1] // 4).transpose(1, 0, 2)
    shards_rows = lambda a: a.reshape(4, a.shape[0] // 4, a.shape[1])
    packed = lambda names, dw: _pack_full(names, [dw[n] for n in names], BF16)
    branch_out = ["w_oa", "w_ob"]
    back_a, back_b = SLAB_LATE + ["w_o"], SLAB_EARLY + ["w_ple_gate"]

    got_in, got_early = _gather_by_halves("weight_gather_early", [bf("w_in"), slab(SLAB_EARLY)])
    wts = {"w_in": cols_full(got_in), **dict(zip(SLAB_EARLY, _unpack_full(SLAB_EARLY, got_early)))}
    gains = {n: given[n].reshape(1, -1) for n, _ in SMALL}
    ride = {
        "gather": {
            "in_proj": (_Exchange("gather", [bf("w_o")]), lambda got: {"w_o": rows_full(got[0])}),
            "prep_qb": (_Exchange("gather", [slab(branch_out)]), lambda got: dict(zip(branch_out, _unpack_full(branch_out, got[0])))),
            "attn_a_fwd": (_Exchange("gather", [bf("w_up")]), lambda got: {"w_up": got[0]}),
            "attn_b_fwd": (_Exchange("gather", [bf("w_down")]), lambda got: {"w_down": rows_full(got[0])}),
            "mix_out": (_Exchange("gather", [bf("w_ple_gate"), bf("w_ple")]),
                        lambda got: {"w_ple_gate": rows_full(got[0]), "w_ple": cols_full(got[1])}),
        },
        "scatter_a": lambda dw: _Exchange("scatter", [dw["w_up"], packed(back_a, dw)]),
        "scatter_b": lambda dw: _Exchange("scatter", [shards_rows(dw["w_down"]), packed(back_b, dw)]),
        "scatter_in": lambda dw: _Exchange("scatter", [_presum_halves("grad_presum_in", shards_cols(dw["w_in"]))]),
        "out": {},
    }
    loss_part, grad_x, dg, dw = _local_step(x, p[0], loss_target, gains, wts, ride)

    small = lambda prefix: _pack_small({n: given[prefix + n] for n, _ in SMALL})
    g_s, d_s, m_s, v_s, loss = _small_allreduce_adamw(_pack_small(dg, loss_part), small(""), small("m_"), small("v_"))

    parts = ride["out"]
    grads = {"w_up": _reduce_pair("grad_reduce_up", parts["parts_a"][0]), "w_down": _reduce_pair("grad_reduce_down", parts["parts_b"][0]),
             "w_in": _reduce_halves("grad_reduce_in", parts["parts_in"][0])}
    grads.update(zip(back_a, _unpack_shards(back_a, _reduce_pair("grad_reduce_slab_a", parts["parts_a"][1]))))
    grads.update(zip(back_b, _unpack_shards(back_b, _reduce_pair("grad_reduce_slab_b", parts["parts_b"][1]))))

    res = {}
    for key, slab in (("grad_", g_s), ("delta_", d_s), ("new_m_", m_s), ("new_v_", v_s)):
        for n, val in _unpack_small(slab, given).items():
            res[key + n] = val
    for n in big_names:
        res["grad_" + n], res["delta_" + n], res["new_m_" + n], res["new_v_" + n] = _adamw_shard(
            "adamw_" + n, grads[n], given[n], given["m_" + n], given["v_" + n])
    outs = [loss.reshape(()), grad_x]
    for key in ("grad_", "delta_", "new_m_", "new_v_"):
        outs += [res[key + n] for n in order]
    return tuple(outs)
```

```python
import functools

import numpy as np
import jax
import jax.numpy as jnp
from jax import lax
from jax.experimental import pallas as pl
from jax.experimental.pallas import tpu as pltpu

F32 = jnp.float32
BF16 = jnp.bfloat16
MESH = pl.DeviceIdType.MESH

D_MODEL = 1024
GRID_W = 64
ROPE_THETA = 10000.0
EPS = 1e-6
H_A, QK_NOPE, QK_ROPE, V_DIM_A, Q_LORA, KV_LORA = 8, 64, 32, 64, 256, 128
H_B, KV_B, HD_B = 8, 2, 64
D_FF = 4096
PLE_DIM = 256
HEAD_W = 128
SCALE_A = (QK_NOPE + QK_ROPE) ** -0.5
SCALE_B = HD_B ** -0.5

ADAM_LR, ADAM_B1, ADAM_B2, ADAM_EPS, ADAM_WD, ADAM_STEP = 0.001, 0.9, 0.999, 1e-08, 0.01, 10
M_HAT_DIV = 1.0 - ADAM_B1 ** ADAM_STEP
V_HAT_DIV = 1.0 - ADAM_B2 ** ADAM_STEP

VMEM_LIMIT_BYTES = 56 * 1024 * 1024

ZC_QB, ZC_QLAT, ZC_CKV, ZC_KPE, ZC_KB, ZC_VB, ZC_GA, ZC_GB = 0, 8, 10, 11, 12, 14, 16, 24
Z_WIDTH = 32 * HEAD_W

BIG = [
    ("w_in", 1024, 3232, 1), ("w_qb", 256, 768, 1), ("w_kvb", 128, 1024, 1), ("w_oa", 512, 1024, 1),
    ("w_ob", 512, 1024, 1), ("w_o", 1024, 1024, 0), ("w_up", 1024, 4096, 1), ("w_down", 4096, 1024, 0),
    ("w_ple_gate", 1024, 1024, 0), ("w_ple", 256, 1024, 1),
]
BIG_BY_NAME = {e[0]: e for e in BIG}
PACK_W = 1024
PACK_ALIGN = 64
SLAB_EARLY = ["w_qb", "w_kvb"]
SLAB_LATE = ["w_oa", "w_ob", "w_ple"]

SMALL = [("g_mix", 1024), ("g_qa", 256), ("g_kva", 128), ("g_qn", 64), ("g_kn", 64), ("g_mlp", 1024),
         ("g_ple", 1024), ("g_final", 1024)]
SMALL_N = sum(n for _, n in SMALL)
LOSS_ROW0 = 40
SMALL_ROWS = 48


def _params(sem):
    return pltpu.CompilerParams(dimension_semantics=sem, vmem_limit_bytes=VMEM_LIMIT_BYTES)


def _sigmoid(v):
    return 1.0 / (1.0 + jnp.exp(-v.astype(F32)))


def _perm(v, p_ref):
    pm = p_ref[...]
    hi = v.astype(BF16)
    lo = (v - hi.astype(F32)).astype(BF16)
    return (jnp.dot(hi, pm, preferred_element_type=F32) + jnp.dot(lo, pm, preferred_element_type=F32))


def _rms(v, g, n):
    v = v.astype(F32)
    rs = lax.rsqrt(jnp.sum(v * v, axis=-1, keepdims=True) * (1.0 / n) + EPS)
    return v * rs * g


def _rms_bwd(dy, v, g, n):
    v = v.astype(F32)
    rs = lax.rsqrt(jnp.sum(v * v, axis=-1, keepdims=True) * (1.0 / n) + EPS)
    vh = v * rs
    dyg = dy * g
    dx = rs * (dyg - vh * (jnp.sum(dyg * vh, axis=-1, keepdims=True) * (1.0 / n)))
    return dx, jnp.sum(dy * vh, axis=0, keepdims=True)


def _ride(body, grid, rider):
    if rider is None:
        return body, [], [], [], []
    n_x, n_sem = len(rider.srcs), len(rider.scratch)

    def wrapped(*refs):
        ids = [pl.program_id(a) for a in range(len(grid))]
        n_in = len(refs) - n_sem - 2 * n_x - rider.n_core_out - rider.n_core_scratch
        core_in, srcs = refs[:n_in], refs[n_in:n_in + n_x]
        core_out = refs[n_in + n_x:n_in + n_x + rider.n_core_out]
        dsts = refs[n_in + n_x + rider.n_core_out:n_in + 2 * n_x + rider.n_core_out]
        core_scr = refs[n_in + 2 * n_x + rider.n_core_out:len(refs) - n_sem]
        sems = refs[len(refs) - n_sem:]

        @pl.when(functools.reduce(jnp.logical_and, [a == 0 for a in ids]))
        def _():
            rider.start(srcs, dsts, *sems)

        body(*core_in, *core_out, *core_scr)

        @pl.when(functools.reduce(jnp.logical_and, [a == n - 1 for a, n in zip(ids, grid)]))
        def _():
            rider.finish(srcs, dsts, *sems)

    hbm = pl.BlockSpec(memory_space=pl.ANY)
    return wrapped, list(rider.srcs), [hbm] * n_x, list(rider.out_shapes), list(rider.scratch)


def _mm(name, a, b, *, trans_b=False, b_slots=False, a_pre=None, out_dtypes=(F32,), epi=None, extras=(), consts=(), accs=(), tm=512,
        tn=None, rider=None):
    a_ops = [o if isinstance(o, tuple) else (o, 0, o.shape[1]) for o in (a if isinstance(a, list) else [a])]
    b_ops = b if isinstance(b, list) else [b]
    assert len(a_ops) == len(b_ops) and not (b_slots and (trans_b or len(b_ops) > 1))
    m = a_ops[0][0].shape[0]
    if b_slots:
        n, tn = b.shape[0] * b.shape[2], b.shape[2]
    else:
        first = b_ops[0][0] if isinstance(b_ops[0], tuple) else b_ops[0]
        n = first.shape[-2] if trans_b else first.shape[1]
        tn = n if tn is None else min(tn, n)
    tm = min(tm, m)
    assert m % tm == 0 and n % tn == 0
    extras = [e if isinstance(e, tuple) else (e, 0) for e in extras]
    n_p, n_ex, n_c, n_out, n_acc = len(a_ops), len(extras), len(consts), len(out_dtypes), len(accs)
    dims = (((1,), (1,)), ((), ())) if trans_b else (((1,), (0,)), ((), ()))

    def body(*refs):
        acc = None
        for a_ref, b_ref in zip(refs[:n_p], refs[n_p:2 * n_p]):
            av = a_ref[...] if a_pre is None else a_pre(a_ref[...].astype(F32))
            part = lax.dot_general(av.astype(BF16), b_ref[...].astype(BF16), dims, preferred_element_type=F32)
            acc = part if acc is None else acc + part
        rest = refs[2 * n_p:]
        res = (acc,) if epi is None else epi(acc, *[e[...] for e in rest[:n_ex + n_c]])
        o_refs = rest[n_ex + n_c:]
        for o_ref, r in zip(o_refs[:n_out], res[:n_out]):
            o_ref[...] = r.astype(o_ref.dtype)
        if n_acc:
            first_step = jnp.logical_and(pl.program_id(0) == 0, pl.program_id(1) == 0)

            @pl.when(first_step)
            def _():
                for o_ref, r in zip(o_refs[n_out:], res[n_out:]):
                    o_ref[...] = r

            @pl.when(jnp.logical_not(first_step))
            def _():
                for o_ref, r in zip(o_refs[n_out:], res[n_out:]):
                    o_ref[...] += r

    def b_spec(op, k_i):
        if b_slots:
            return pl.BlockSpec((None, k_i, tn), lambda j, i: (j, 0, 0))
        if not isinstance(op, tuple):
            return pl.BlockSpec((tn, k_i), lambda j, i: (j, 0)) if trans_b else pl.BlockSpec((k_i, tn), lambda j, i: (0, j))
        assert trans_b
        if len(op) == 2:
            return pl.BlockSpec((None, tn, k_i), lambda j, i, slot=op[1]: (slot, j, 0))
        return pl.BlockSpec((tn, k_i), lambda j, i, blk=op[1]: (j, blk))

    grid = (n // tn, m // tm)
    if rider is not None:
        rider.n_core_out, rider.n_core_scratch = n_out + n_acc, 0
    body, x_in, x_spec, x_out, x_scr = _ride(body, grid, rider)
    a_specs = [pl.BlockSpec((tm, k_i), lambda j, i, blk=blk: (i, blk)) for _, blk, k_i in a_ops]
    b_specs = [b_spec(op, k_i) for op, (_, _, k_i) in zip(b_ops, a_ops)]
    t_spec = pl.BlockSpec((tm, tn), lambda j, i: (i, j))
    e_specs = [pl.BlockSpec((tm, tn), lambda j, i, off=off: (i, j + off)) for _, off in extras]
    c_specs = [pl.BlockSpec(c.shape, lambda j, i: (0, 0)) for c in consts]
    acc_specs = [pl.BlockSpec(sh, lambda j, i: (0, 0)) for sh in accs]
    sem = ("parallel", "parallel") if rider is None and not n_acc else ("arbitrary", "arbitrary")
    outs = pl.pallas_call(
        body, out_shape=[jax.ShapeDtypeStruct((m, n), d) for d in out_dtypes] + [jax.ShapeDtypeStruct(sh, F32) for sh in accs] + x_out,
        grid=grid, in_specs=a_specs + b_specs + e_specs + c_specs + x_spec, out_specs=[t_spec] * n_out + acc_specs + x_spec,
        scratch_shapes=x_scr, compiler_params=_params(sem),
        name=name)(*[o[0] for o in a_ops], *[o[0] if isinstance(o, tuple) else o for o in b_ops], *[e for e, _ in extras], *consts, *x_in)
    return outs[0] if len(outs) == 1 else outs


def _per_head(fn, heads, n_tiled, n_out):
    def run(*args):
        res = [fn(*[a[:, hd * HEAD_W:(hd + 1) * HEAD_W] for a in args[:n_tiled]], *args[n_tiled:]) for hd in range(heads)]
        tiles = [jnp.concatenate([r[k] for r in res], axis=1) for k in range(n_out)]
        sums = [functools.reduce(lambda u, v: u + v, [r[k] for r in res]) for k in range(n_out, len(res[0]))]
        return (*tiles, *sums)
    return run


def _mm_tn(name, a, b, *, a_pre=None, out_dtype=BF16, out_slots=False, tk=1024, tn=1024, tt=4096):
    t, k = a.shape
    n = b.shape[1]
    tk, tn = min(tk, k), min(tn, n)
    if a.dtype == F32 or b.dtype == F32:
        tt = tt // 2
    if k == tk and n == tn:
        tt = tt // 2
    tt = min(tt, t)
    assert b.shape[0] == t and k % tk == 0 and n % tn == 0 and t % tt == 0
    nt = t // tt

    def body(a_ref, b_ref, o_ref, acc):
        av = a_ref[...] if a_pre is None else a_pre(a_ref[...].astype(F32))
        part = lax.dot_general(av.astype(BF16), b_ref[...].astype(BF16), (((0,), (0,)), ((), ())), preferred_element_type=F32)

        @pl.when(pl.program_id(2) == 0)
        def _():
            acc[...] = part

        @pl.when(pl.program_id(2) != 0)
        def _():
            acc[...] += part

        @pl.when(pl.program_id(2) == nt - 1)
        def _():
            o_ref[...] = acc[...].astype(o_ref.dtype)

    if out_slots:
        out_shape, out_spec = (n // tn, k, tn), pl.BlockSpec((None, tk, tn), lambda ki, ni, ti: (ni, ki, 0))
    else:
        out_shape, out_spec = (k, n), pl.BlockSpec((tk, tn), lambda ki, ni, ti: (ki, ni))
    return pl.pallas_call(
        body, out_shape=jax.ShapeDtypeStruct(out_shape, out_dtype), grid=(k // tk, n // tn, nt),
        in_specs=[pl.BlockSpec((tt, tk), lambda ki, ni, ti: (ti, ki)), pl.BlockSpec((tt, tn), lambda ki, ni, ti: (ti, ni))],
        out_specs=out_spec, scratch_shapes=[pltpu.VMEM((tk, tn), F32)],
        compiler_params=_params(("parallel", "parallel", "arbitrary")), name=name)(a, b)


def _rowwise(name, fn, ins, outs, *, consts=(), pos=(), accs=(), heads=1, tm=512, seq=None, rider=None):
    t = ins[0][0].shape[0]
    tm = min(tm, t if seq is None else seq)
    assert t % tm == 0 and (seq is None or seq % tm == 0)
    n_in, n_pos, n_c, n_out, n_acc = len(ins), len(pos), len(consts), len(outs), len(accs)

    def body(*refs):
        vals = [r[...] for r in refs[:n_in + n_pos + n_c]]
        res = fn(*vals)
        o_refs = refs[n_in + n_pos + n_c:]
        for o_ref, r in zip(o_refs[:n_out], res[:n_out]):
            o_ref[...] = r.astype(o_ref.dtype)
        if n_acc:
            first = jnp.logical_and(pl.program_id(0) == 0, pl.program_id(1) == 0)

            @pl.when(first)
            def _():
                for o_ref, r in zip(o_refs[n_out:], res[n_out:]):
                    o_ref[...] = r

            @pl.when(jnp.logical_not(first))
            def _():
                for o_ref, r in zip(o_refs[n_out:], res[n_out:]):
                    o_ref[...] += r

    def tiled(width, c0, per_head):
        return pl.BlockSpec((tm, width), (lambda h, i: (i, c0 + h)) if per_head else (lambda h, i: (i, c0)))

    in_specs = [tiled(w, c0, ph) for _, w, c0, ph in ins]
    if n_pos:
        nblk = seq // tm
        in_specs += [pl.BlockSpec((tm, a.shape[1]), lambda h, i: (i % nblk, 0)) for a in pos]
    in_specs += [pl.BlockSpec(a.shape, lambda h, i: (0, 0)) for a in consts]
    out_specs = [tiled(w, 0, ph) for _, _, w, ph in outs] + [pl.BlockSpec(s, lambda h, i: (0, 0)) for s in accs]
    out_shape = [jax.ShapeDtypeStruct((t, c), d) for c, d, _, _ in outs] + [jax.ShapeDtypeStruct(s, F32) for s in accs]
    sem = ("arbitrary", "arbitrary") if n_acc or rider is not None else ("parallel", "parallel")
    grid = (heads, t // tm)
    if rider is not None:
        rider.n_core_out, rider.n_core_scratch = n_out + n_acc, 0
    body, x_in, x_spec, x_out, x_scr = _ride(body, grid, rider)
    res = pl.pallas_call(body, out_shape=out_shape + x_out, grid=grid, in_specs=in_specs + x_spec, out_specs=out_specs + x_spec,
                         scratch_shapes=x_scr, compiler_params=_params(sem), name=name)(*[a for a, _, _, _ in ins], *pos, *consts, *x_in)
    return res[0] if len(res) == 1 else res


ATTN_HEADS_PER_STEP = 4


def _attn_fwd(name, q, k, kc0, v, vc0, *, heads, group, nseq, seq, tq=512, rider=None):
    tq = min(tq, seq)
    nq = seq // tq
    hp = ATTN_HEADS_PER_STEP
    grid = (heads // hp, nseq, nq)
    shared = group > 1
    assert group % hp == 0 if shared else (kc0 % hp == 0 and vc0 % hp == 0)

    def body(q_ref, k_ref, v_ref, o_ref, lse_ref):
        for j in range(hp):
            cols = slice(j * HEAD_W, (j + 1) * HEAD_W)
            kj = (k_ref[...] if shared else k_ref[:, cols]).astype(BF16)
            vj = (v_ref[...] if shared else v_ref[:, cols]).astype(BF16)
            s = lax.dot_general(q_ref[:, cols], kj, (((1,), (1,)), ((), ())), preferred_element_type=F32)
            m = jnp.max(s, axis=-1, keepdims=True)
            p = jnp.exp((s - m).astype(BF16))
            vj = jnp.where(lax.broadcasted_iota(jnp.int32, (1, HEAD_W), 1) == HEAD_W - 1, jnp.ones((), BF16), vj)
            o = jnp.dot(p, vj, preferred_element_type=F32)
            l = o[:, HEAD_W - 1:]
            o_ref[:, cols] = (o * (1.0 / l)).astype(o_ref.dtype)
            lse_ref[j] = m + jnp.log(l)

    if rider is not None:
        rider.n_core_out, rider.n_core_scratch = 2, 0
    body, x_in, x_spec, x_out, x_scr = _ride(body, grid, rider)
    q_spec = pl.BlockSpec((tq, hp * HEAD_W), lambda h, b, i: (b * nq + i, h))
    if shared:
        k_spec = pl.BlockSpec((seq, HEAD_W), lambda h, b, i: (b, kc0 + (h * hp) // group))
        v_spec = pl.BlockSpec((seq, HEAD_W), lambda h, b, i: (b, vc0 + (h * hp) // group))
    else:
        k_spec = pl.BlockSpec((seq, hp * HEAD_W), lambda h, b, i: (b, kc0 // hp + h))
        v_spec = pl.BlockSpec((seq, hp * HEAD_W), lambda h, b, i: (b, vc0 // hp + h))
    lse_spec = pl.BlockSpec((hp, tq, 1), lambda h, b, i: (h, b * nq + i, 0))
    sem = ("parallel",) * 3 if rider is None else ("arbitrary",) * 3
    return pl.pallas_call(
        body, out_shape=[jax.ShapeDtypeStruct(q.shape, BF16), jax.ShapeDtypeStruct((heads, q.shape[0], 1), F32)] + x_out,
        grid=grid, in_specs=[q_spec, k_spec, v_spec] + x_spec, out_specs=[q_spec, lse_spec] + x_spec, scratch_shapes=x_scr,
        compiler_params=_params(sem), name=name)(q, k, v, *x_in)


def _attn_bwd(name, q, k, kc0, v, vc0, o, do, lse, *, heads, group, nseq, seq, tq=1024, rider=None):
    tq = min(tq, seq)
    nq = seq // tq
    hk = heads // group
    t = q.shape[0]
    grid = (hk, nseq, group, nq)

    def body(q_ref, k_ref, v_ref, o_ref, do_ref, lse_ref, dq_ref, dk_ref, dv_ref, dk_acc, dv_acc):
        g, i = pl.program_id(2), pl.program_id(3)
        qv, kv, vv, dov = q_ref[...], k_ref[...].astype(BF16), v_ref[...].astype(BF16), do_ref[...]
        s = lax.dot_general(qv, kv, (((1,), (1,)), ((), ())), preferred_element_type=F32)
        pn = jnp.exp(s - lse_ref[...])
        dp = lax.dot_general(dov, vv, (((1,), (1,)), ((), ())), preferred_element_type=F32)
        delta = jnp.sum(dov.astype(F32) * o_ref[...].astype(F32), axis=-1, keepdims=True)
        ds = (pn * (dp - delta)).astype(BF16)
        dq_ref[...] = jnp.dot(ds, kv, preferred_element_type=F32)
        dk_part = lax.dot_general(ds, qv, (((0,), (0,)), ((), ())), preferred_element_type=F32)
        dv_part = lax.dot_general(pn.astype(BF16), dov, (((0,), (0,)), ((), ())), preferred_element_type=F32)
        first = jnp.logical_and(g == 0, i == 0)

        @pl.when(first)
        def _():
            dk_acc[...] = dk_part
            dv_acc[...] = dv_part

        @pl.when(jnp.logical_not(first))
        def _():
            dk_acc[...] += dk_part
            dv_acc[...] += dv_part

        @pl.when(jnp.logical_and(g == group - 1, i == nq - 1))
        def _():
            dk_ref[...] = dk_acc[...].astype(dk_ref.dtype)
            dv_ref[...] = dv_acc[...].astype(dv_ref.dtype)

    if rider is not None:
        rider.n_core_out, rider.n_core_scratch = 3, 2
    body, x_in, x_spec, x_out, x_scr = _ride(body, grid, rider)
    q_spec = pl.BlockSpec((tq, HEAD_W), lambda kh, b, g, i: (b * nq + i, kh * group + g))
    kv_out = pl.BlockSpec((seq, HEAD_W), lambda kh, b, g, i: (b, kh))
    lse_spec = pl.BlockSpec((None, tq, 1), lambda kh, b, g, i: (kh * group + g, b * nq + i, 0))
    sem = ("parallel", "parallel", "arbitrary", "arbitrary") if rider is None else ("arbitrary",) * 4
    return pl.pallas_call(
        body,
        out_shape=[jax.ShapeDtypeStruct(q.shape, F32), jax.ShapeDtypeStruct((t, hk * HEAD_W), BF16),
                   jax.ShapeDtypeStruct((t, hk * HEAD_W), BF16)] + x_out,
        grid=grid,
        in_specs=[q_spec, pl.BlockSpec((seq, HEAD_W), lambda kh, b, g, i: (b, kc0 + kh)),
                  pl.BlockSpec((seq, HEAD_W), lambda kh, b, g, i: (b, vc0 + kh)), q_spec, q_spec, lse_spec] + x_spec,
        out_specs=[q_spec, kv_out, kv_out] + x_spec,
        scratch_shapes=[pltpu.VMEM((seq, HEAD_W), F32), pltpu.VMEM((seq, HEAD_W), F32)] + x_scr,
        compiler_params=_params(sem), name=name)(q, k, v, o, do, lse, *x_in)


def _place():
    return lax.axis_index("x"), lax.axis_index("y"), lax.axis_index("c")


def _other_chips(x, y):
    return [(1 - x, y), (x, 1 - y), (1 - x, 1 - y)]


class _Exchange:
    def __init__(self, kind, srcs):
        assert kind in ("gather", "scatter")
        self.kind, self.srcs = kind, list(srcs)
        n = len(self.srcs)
        self.out_shapes = [jax.ShapeDtypeStruct((4, *a.shape[-2:]), a.dtype) for a in self.srcs]
        self.scratch = [pltpu.SemaphoreType.DMA((3 * n,)), pltpu.SemaphoreType.DMA((3 * n,)), pltpu.SemaphoreType.DMA((n,))]
        self.n_core_out = self.n_core_scratch = 0

    def _copies(self, j, src_ref, out_ref, send_sems, recv_sems, landing):
        x, y, c = _place()

        def remote(k, s, d, to):
            return pltpu.make_async_remote_copy(src_ref=s, dst_ref=d, send_sem=send_sems.at[3 * j + k], recv_sem=recv_sems.at[3 * j + k],
                                                device_id=to, device_id_type=MESH)

        me = 2 * x + y
        part = (lambda i: src_ref) if self.kind == "gather" else (lambda i: src_ref.at[i])
        if landing:
            return [remote(k, part(me), out_ref.at[2 * px + py], (px, py, c)) for k, (px, py) in enumerate(_other_chips(x, y))]
        return [remote(k, part(2 * px + py), out_ref.at[me], (px, py, c)) for k, (px, py) in enumerate(_other_chips(x, y))]

    def _local(self, j, src_ref, out_ref, local_sems):
        x, y, _ = _place()
        me = 2 * x + y
        return pltpu.make_async_copy(src_ref if self.kind == "gather" else src_ref.at[me], out_ref.at[me], local_sems.at[j])

    def start(self, src_refs, out_refs, send_sems, recv_sems, local_sems):
        for j, (src_ref, out_ref) in enumerate(zip(src_refs, out_refs)):
            self._local(j, src_ref, out_ref, local_sems).start()
            for mine in self._copies(j, src_ref, out_ref, send_sems, recv_sems, False):
                mine.start()

    def finish(self, src_refs, out_refs, send_sems, recv_sems, local_sems):
        for j, (src_ref, out_ref) in enumerate(zip(src_refs, out_refs)):
            for landed in self._copies(j, src_ref, out_ref, send_sems, recv_sems, True):
                landed.wait_recv()
        for j, (src_ref, out_ref) in enumerate(zip(src_refs, out_refs)):
            for mine in self._copies(j, src_ref, out_ref, send_sems, recv_sems, False):
                mine.wait_send()
            self._local(j, src_ref, out_ref, local_sems).wait()


def _gather_by_halves(name, srcs):
    n = len(srcs)

    def body(*refs):
        x, y, c = _place()
        me = 2 * x + y
        local_sems = refs[-1]
        copies = []
        for j in range(n):
            src_ref, out_ref, send_sems, recv_sems = refs[j], refs[n + j], refs[2 * n + 2 * j], refs[2 * n + 2 * j + 1]
            half = srcs[j].shape[0] // 2
            rows_c = pl.ds(pl.multiple_of(c * half, half), half)
            rows_s = pl.ds(pl.multiple_of((1 - c) * half, half), half)

            def remote(k, s_ref, d_ref, to, send_sems=send_sems, recv_sems=recv_sems):
                return pltpu.make_async_remote_copy(src_ref=s_ref, dst_ref=d_ref, send_sem=send_sems.at[k], recv_sem=recv_sems.at[k],
                                                    device_id=to, device_id_type=MESH)

            local = pltpu.make_async_copy(src_ref, out_ref.at[me], local_sems.at[j])
            local.start()
            chips = _other_chips(x, y)
            sent = [remote(k, src_ref.at[rows_c], out_ref.at[me, rows_c], (px, py, c)) for k, (px, py) in enumerate(chips)]
            landing = [remote(k, src_ref.at[rows_c], out_ref.at[2 * px + py, rows_c], (px, py, c)) for k, (px, py) in enumerate(chips)]
            passed = [remote(3 + k, out_ref.at[2 * px + py, rows_c], out_ref.at[2 * px + py, rows_c], (x, y, 1 - c))
                      for k, (px, py) in enumerate(chips)]
            from_sibling = [remote(3 + k, out_ref.at[2 * px + py, rows_s], out_ref.at[2 * px + py, rows_s], (x, y, 1 - c))
                            for k, (px, py) in enumerate(chips)]
            for cp in sent:
                cp.start()
            copies.append((local, sent, landing, passed, from_sibling))
        for local, sent, landing, passed, from_sibling in copies:
            for k in range(3):
                landing[k].wait_recv()
                passed[k].start()
        for local, sent, landing, passed, from_sibling in copies:
            for k in range(3):
                from_sibling[k].wait_recv()
            for cp in sent + passed:
                cp.wait_send()
            local.wait()

    sems = [pltpu.SemaphoreType.DMA((6,)) for _ in range(2 * n)] + [pltpu.SemaphoreType.DMA((n,))]
    return pl.pallas_call(
        body, out_shape=[jax.ShapeDtypeStruct((4, *a.shape), a.dtype) for a in srcs],
        in_specs=[pl.BlockSpec(memory_space=pl.ANY)] * n, out_specs=[pl.BlockSpec(memory_space=pltpu.VMEM)] * n,
        scratch_shapes=sems, compiler_params=pltpu.CompilerParams(vmem_limit_bytes=VMEM_LIMIT_BYTES), name=name)(*srcs)


def _adamw(w, g, m, v):
    m = ADAM_B1 * m + (1.0 - ADAM_B1) * g
    v = ADAM_B2 * v + (1.0 - ADAM_B2) * (g * g)
    delta = -ADAM_LR * ((m / M_HAT_DIV) / (jnp.sqrt(v / V_HAT_DIV) + ADAM_EPS) + ADAM_WD * w)
    return delta, m, v


def _small_allreduce_adamw(part, w, m, v):
    def body(part_ref, w_ref, m_ref, v_ref, g_out, d_out, m_out, v_out, loss_out, buf, send_sems, recv_sems):
        x, y, c = _place()
        me = 4 * x + 2 * y + c
        buf[me] = part_ref[...]

        def flip(k):
            fx, fy, fc = (k >> 2) & 1, (k >> 1) & 1, k & 1
            px, py, pc = (1 - x if fx else x), (1 - y if fy else y), (1 - c if fc else c)
            return (px, py, pc), 4 * px + 2 * py + pc

        def copy(k, slot):
            return pltpu.make_async_remote_copy(
                src_ref=part_ref, dst_ref=buf.at[slot], send_sem=send_sems.at[k - 1], recv_sem=recv_sems.at[k - 1],
                device_id=flip(k)[0], device_id_type=MESH)

        sent = [copy(k, me) for k in range(1, 8)]
        for cp in sent:
            cp.start()
        for k in range(1, 8):
            copy(k, flip(k)[1]).wait_recv()
        for cp in sent:
            cp.wait_send()
        tot = buf[0]
        for j in range(1, 8):
            tot = tot + buf[j]
        delta, m_new, v_new = _adamw(w_ref[...], tot, m_ref[...], v_ref[...])
        g_out[...] = tot
        d_out[...] = delta
        m_out[...] = m_new
        v_out[...] = v_new
        loss_out[...] = jnp.sum(tot[LOSS_ROW0:LOSS_ROW0 + 8, :]).reshape(1, 1)

    vm = pl.BlockSpec(memory_space=pltpu.VMEM)
    shp = jax.ShapeDtypeStruct((SMALL_ROWS, 128), F32)
    return pl.pallas_call(
        body, out_shape=[shp, shp, shp, shp, jax.ShapeDtypeStruct((1, 1), F32)],
        in_specs=[vm, vm, vm, vm], out_specs=[vm, vm, vm, vm, vm],
        scratch_shapes=[pltpu.VMEM((8, SMALL_ROWS, 128), F32), pltpu.SemaphoreType.DMA((7,)), pltpu.SemaphoreType.DMA((7,))],
        name="small_allreduce_adamw")(part, w, m, v)


def _row_tile(rows, cap):
    return max(t for t in range(16, min(rows, cap) + 1, 16) if rows % t == 0)


def _reduce_pair(name, parts):
    _, rows, w = parts.shape
    tr = _row_tile(rows, 576)
    nt = rows // tr

    def body(p_ref, o_ref, mine, theirs, send_sems, recv_sems):
        i = pl.program_id(0)
        x, y, c = _place()

        def copy(t):
            rows_t = pl.ds(pl.multiple_of(t * tr, tr), tr)
            return pltpu.make_async_remote_copy(src_ref=mine.at[rows_t], dst_ref=theirs.at[rows_t], send_sem=send_sems.at[t],
                                                recv_sem=recv_sems.at[t], device_id=(x, y, 1 - c), device_id_type=MESH)

        @pl.when(i < nt)
        def _():
            mine[pl.ds(pl.multiple_of(i * tr, tr), tr), :] = (
                (p_ref[0].astype(F32) + p_ref[1].astype(F32)) + p_ref[2].astype(F32)) + p_ref[3].astype(F32)
            copy(i).start()

        @pl.when(i >= nt)
        def _():
            t = i - nt
            copy(t).wait()
            rows_t = pl.ds(pl.multiple_of(t * tr, tr), tr)
            o_ref[...] = mine[rows_t, :] + theirs[rows_t, :]

    return pl.pallas_call(
        body, out_shape=jax.ShapeDtypeStruct((rows, w), F32), grid=(2 * nt,),
        in_specs=[pl.BlockSpec((4, tr, w), lambda i: (0, jnp.minimum(i, nt - 1), 0))],
        out_specs=pl.BlockSpec((tr, w), lambda i: (jnp.maximum(i - nt, 0), 0)),
        scratch_shapes=[pltpu.VMEM((rows, w), F32), pltpu.VMEM((rows, w), F32), pltpu.SemaphoreType.DMA((nt,)),
                        pltpu.SemaphoreType.DMA((nt,))],
        compiler_params=_params(("arbitrary",)), name=name)(parts)


def _presum_halves(name, shards):
    _, rows, w = shards.shape
    half = rows // 2

    def body(s_ref, o_ref, theirs, send_sems, recv_sems):
        x, y, c = _place()
        rows_c = pl.ds(pl.multiple_of(c * half, half), half)
        rows_s = pl.ds(pl.multiple_of((1 - c) * half, half), half)
        sent = [pltpu.make_async_remote_copy(src_ref=s_ref.at[j, rows_s], dst_ref=theirs.at[j], send_sem=send_sems.at[j],
                                             recv_sem=recv_sems.at[j], device_id=(x, y, 1 - c), device_id_type=MESH) for j in range(4)]
        for cp in sent:
            cp.start()
        for j, cp in enumerate(sent):
            cp.wait_recv()
            o_ref[j] = (s_ref[j, rows_c, :].astype(F32) + theirs[j].astype(F32)).astype(o_ref.dtype)
        for cp in sent:
            cp.wait_send()

    vm = pl.BlockSpec(memory_space=pltpu.VMEM)
    return pl.pallas_call(
        body, out_shape=jax.ShapeDtypeStruct((4, half, w), shards.dtype), in_specs=[vm], out_specs=vm,
        scratch_shapes=[pltpu.VMEM((4, half, w), shards.dtype), pltpu.SemaphoreType.DMA((4,)), pltpu.SemaphoreType.DMA((4,))],
        compiler_params=pltpu.CompilerParams(vmem_limit_bytes=VMEM_LIMIT_BYTES), name=name)(shards)


def _reduce_halves(name, parts):
    _, half, w = parts.shape

    def body(p_ref, o_ref, mine, send_sem, recv_sem):
        x, y, c = _place()
        rows_c = pl.ds(pl.multiple_of(c * half, half), half)
        rows_s = pl.ds(pl.multiple_of((1 - c) * half, half), half)
        mine[...] = ((p_ref[0].astype(F32) + p_ref[1].astype(F32)) + p_ref[2].astype(F32)) + p_ref[3].astype(F32)
        send = pltpu.make_async_remote_copy(src_ref=mine, dst_ref=o_ref.at[rows_c], send_sem=send_sem, recv_sem=recv_sem,
                                            device_id=(x, y, 1 - c), device_id_type=MESH)
        send.start()
        o_ref[rows_c, :] = mine[...]
        pltpu.make_async_remote_copy(src_ref=mine, dst_ref=o_ref.at[rows_s], send_sem=send_sem, recv_sem=recv_sem,
                                     device_id=(x, y, 1 - c), device_id_type=MESH).wait_recv()
        send.wait_send()

    vm = pl.BlockSpec(memory_space=pltpu.VMEM)
    return pl.pallas_call(
        body, out_shape=jax.ShapeDtypeStruct((2 * half, w), F32), in_specs=[vm], out_specs=vm,
        scratch_shapes=[pltpu.VMEM((half, w), F32), pltpu.SemaphoreType.DMA(()), pltpu.SemaphoreType.DMA(())],
        compiler_params=pltpu.CompilerParams(vmem_limit_bytes=VMEM_LIMIT_BYTES), name=name)(parts)


def _adamw_shard(name, g, w, m, v):
    _, rows, cols = w.shape
    tr = _row_tile(rows, 256)

    def body(g_ref, w_ref, m_ref, v_ref, g_out, d_out, m_out, v_out):
        gv = g_ref[...]
        delta, m_new, v_new = _adamw(w_ref[...], gv, m_ref[...], v_ref[...])
        g_out[...] = gv
        d_out[...] = delta
        m_out[...] = m_new
        v_out[...] = v_new

    t_spec = pl.BlockSpec((None, tr, cols), lambda i: (0, i, 0))
    shp = jax.ShapeDtypeStruct((1, rows, cols), F32)
    return pl.pallas_call(body, out_shape=[shp] * 4, grid=(rows // tr,), in_specs=[pl.BlockSpec((tr, cols), lambda i: (i, 0))] + [t_spec] * 3,
                          out_specs=[t_spec] * 4, compiler_params=_params(("parallel",)), name=name)(g, w, m, v)


def _shard_shape(name):
    _, r, c, ax = BIG_BY_NAME[name]
    return (r, c // 4) if ax == 1 else (r // 4, c)


def _pad_rows(a, axis):
    pad = [(0, 0)] * a.ndim
    pad[axis] = (0, -a.shape[axis] % PACK_ALIGN)
    return jnp.pad(a, pad)


def _pack_shards(names, shards, dtype):
    return _pad_rows(jnp.concatenate([s.astype(dtype).reshape(-1, PACK_W) for s in shards], axis=0), 0)


def _unpack_shards(names, slab):
    out, off = [], 0
    for name in names:
        rs, cs = _shard_shape(name)
        n = rs * cs // PACK_W
        out.append(slab[off:off + n].reshape(rs, cs))
        off += n
    return out


def _unpack_full(names, slabs):
    out, off = [], 0
    for name in names:
        _, r, c, ax = BIG_BY_NAME[name]
        n = r * c // 4 // PACK_W
        seg = slabs[:, off:off + n]
        out.append(seg.reshape(4, r, c // 4).transpose(1, 0, 2).reshape(r, c) if ax == 1 else seg.reshape(r, c))
        off += n
    return out


def _pack_full(names, mats, dtype):
    segs = []
    for name, a in zip(names, mats):
        _, r, c, ax = BIG_BY_NAME[name]
        a = a.astype(dtype)
        a = a.reshape(r, 4, c // 4).transpose(1, 0, 2) if ax == 1 else a
        segs.append(a.reshape(4, -1, PACK_W))
    return _pad_rows(jnp.concatenate(segs, axis=1), 1)


def _pad_heads_cols(wm, heads, d):
    k = wm.shape[0]
    return jnp.pad(wm.reshape(k, heads, d), ((0, 0), (0, 0), (0, HEAD_W - d))).reshape(k, heads * HEAD_W)


def _unpad_heads_cols(wm, heads, d):
    k = wm.shape[0]
    return wm.reshape(k, heads, HEAD_W)[:, :, :d].reshape(k, heads * d)


def _win_ext(w_in):
    o = np.cumsum([0, Q_LORA, KV_LORA, QK_ROPE, H_B * HD_B, KV_B * HD_B, KV_B * HD_B, D_MODEL, D_MODEL])
    pc = lambda a, n: jnp.pad(a, ((0, 0), (0, n - a.shape[1])))
    return jnp.concatenate([
        _pad_heads_cols(w_in[:, o[3]:o[4]], H_B, HD_B), w_in[:, o[0]:o[1]], w_in[:, o[1]:o[2]], pc(w_in[:, o[2]:o[3]], HEAD_W),
        _pad_heads_cols(w_in[:, o[4]:o[5]], KV_B, HD_B), _pad_heads_cols(w_in[:, o[5]:o[6]], KV_B, HD_B),
        w_in[:, o[6]:o[7]], w_in[:, o[7]:o[8]]], axis=1)


def _win_unext(blocks):
    c = HEAD_W
    qb, mid, ga, gb = blocks
    at = lambda zc: (zc - ZC_QLAT) * c
    return jnp.concatenate([
        mid[:, at(ZC_QLAT):at(ZC_CKV)], mid[:, at(ZC_CKV):at(ZC_KPE)], mid[:, at(ZC_KPE):at(ZC_KPE) + QK_ROPE],
        _unpad_heads_cols(qb, H_B, HD_B), _unpad_heads_cols(mid[:, at(ZC_KB):at(ZC_VB)], KV_B, HD_B),
        _unpad_heads_cols(mid[:, at(ZC_VB):at(ZC_GA)], KV_B, HD_B), ga, gb], axis=1)


def _wkv_ext(w_kvb):
    wk = w_kvb.reshape(KV_LORA, H_A, QK_NOPE + V_DIM_A)
    k_cols = jnp.pad(wk[:, :, :QK_NOPE], ((0, 0), (0, 0), (0, HEAD_W - QK_NOPE))).reshape(KV_LORA, H_A * HEAD_W)
    v_cols = jnp.pad(wk[:, :, QK_NOPE:], ((0, 0), (0, 0), (0, HEAD_W - V_DIM_A))).reshape(KV_LORA, H_A * HEAD_W)
    eye = jnp.pad(jnp.eye(QK_ROPE, dtype=w_kvb.dtype), ((0, 0), (QK_NOPE, HEAD_W - QK_NOPE - QK_ROPE)))
    pe_rows = jnp.concatenate([jnp.tile(eye, (1, H_A)), jnp.zeros((QK_ROPE, H_A * HEAD_W), w_kvb.dtype)], axis=1)
    top = jnp.concatenate([k_cols, v_cols], axis=1)
    return jnp.concatenate([top, pe_rows, jnp.zeros((2 * HEAD_W - KV_LORA - QK_ROPE, 2 * H_A * HEAD_W), w_kvb.dtype)], axis=0)


def _wkv_unext(k_block, v_block):
    k_cols = k_block[:KV_LORA].reshape(KV_LORA, H_A, HEAD_W)[:, :, :QK_NOPE]
    v_cols = v_block[:KV_LORA].reshape(KV_LORA, H_A, HEAD_W)[:, :, :V_DIM_A]
    return jnp.concatenate([k_cols, v_cols], axis=2).reshape(KV_LORA, H_A * (QK_NOPE + V_DIM_A))


def _pad_heads_rows(wm, heads, d):
    n = wm.shape[1]
    return jnp.pad(wm.reshape(heads, d, n), ((0, 0), (0, HEAD_W - d), (0, 0))).reshape(heads * HEAD_W, n)


def _unpad_heads_rows(wm, heads, d):
    n = wm.shape[1]
    return wm.reshape(heads, HEAD_W, n)[:, :d].reshape(heads * d, n)


def _rope_tables(seq):
    def ang(pos, dim):
        inv = np.float32(ROPE_THETA) ** (-np.arange(0, dim, 2, dtype=np.float32) / np.float32(dim))
        return pos.astype(np.float32)[:, None] * inv[None, :]

    def rot(dim):
        r = np.zeros((dim, dim), np.float32)
        half = dim // 2
        r[np.arange(half) + half, np.arange(half)] = -1.0
        r[np.arange(half), np.arange(half) + half] = 1.0
        return r

    def table(blocks):
        cos, sin = np.ones((seq, HEAD_W), np.float32), np.zeros((seq, HEAD_W), np.float32)
        pm = np.zeros((HEAD_W, HEAD_W), np.float32)
        for c0, a in blocks:
            d = 2 * a.shape[1]
            cos[:, c0:c0 + d] = np.concatenate([np.cos(a), np.cos(a)], axis=1)
            sin[:, c0:c0 + d] = np.concatenate([np.sin(a), np.sin(a)], axis=1)
            pm[c0:c0 + d, c0:c0 + d] = rot(d)
        return jnp.asarray(cos), jnp.asarray(sin), jnp.asarray(pm, BF16), jnp.asarray(pm.T, BF16)

    tok = np.arange(seq)
    a1 = ang(tok, QK_ROPE)
    arow, acol = ang(tok // GRID_W, HD_B // 2), ang(tok % GRID_W, HD_B // 2)
    return table([(QK_NOPE, a1)]), table([(0, a1)]), table([(0, arow), (HD_B // 2, acol)])


def _local_step(x, p, tgt, gains, wts, ride=None):
    nb, seq, _ = x.shape
    t = nb * seq
    x0 = x.reshape(t, D_MODEL)
    p2 = p.reshape(t, PLE_DIM)
    tg = tgt.reshape(t, D_MODEL)
    (cq_t, sq_t, pq, pq_t), (ck_t, sk_t, pk, pk_t), (cb_t, sb_t, pb, pb_t) = _rope_tables(seq)
    padg = lambda g: jnp.pad(g, ((0, 0), (0, HEAD_W - g.shape[1])))
    g_qn, g_kn = padg(gains["g_qn"]), padg(gains["g_kn"])

    win = _win_ext(wts["w_in"])
    wqb = _pad_heads_cols(wts["w_qb"], H_A, QK_NOPE + QK_ROPE)
    wkv = _wkv_ext(wts["w_kvb"])

    norm = lambda n: (lambda v, g: (_rms(v, g, n),))
    full = lambda a: (a, a.shape[1], 0, False)
    wts = dict(wts)
    rider_of = lambda kernel_name: None if ride is None else ride["gather"][kernel_name][0]

    def landed(kernel_name, got):
        if ride is not None:
            wts.update(ride["gather"][kernel_name][1](got))

    h = _rowwise("norm_mix", norm(D_MODEL), [full(x0)], [(D_MODEL, BF16, D_MODEL, False)], consts=[gains["g_mix"]])
    res = _mm("in_proj", h, win, out_dtypes=(BF16,), tn=2048, rider=rider_of("in_proj"))
    z, got = (res, []) if ride is None else (res[0], res[1:])
    landed("in_proj", got)

    def rope_fwd(scale):
        return lambda v, cos, sin, pm: ((v * cos + _perm(v, pm) * sin) * scale,)

    heads_tile = lambda n: (n * HEAD_W, BF16, n * HEAD_W, False)
    z_qlat, z_ckv, z_kpe = (z, Q_LORA, ZC_QLAT // 2, False), (z, HEAD_W, ZC_CKV, False), (z, HEAD_W, ZC_KPE, False)

    def q_path(zq, cos, sin, g, w, pm):
        cqv = _rms(zq, g, Q_LORA).astype(BF16)
        qa = jnp.dot(cqv, w[...], preferred_element_type=F32)
        return (cqv, *_per_head(rope_fwd(SCALE_A), H_A, 1, 1)(qa, cos, sin, pm))

    cq, q_a = _rowwise("q_path", q_path, [z_qlat], [(Q_LORA, BF16, Q_LORA, False), heads_tile(H_A)],
                       pos=[cq_t, sq_t], consts=[gains["g_qa"], wqb, pq], seq=seq)

    def kv_path(ckv_raw, kpe_raw, cos, sin, g, w, pm):
        kinv = jnp.concatenate([_rms(ckv_raw, g, KV_LORA), *rope_fwd(1.0)(kpe_raw, cos, sin, pm)], axis=1).astype(BF16)
        return kinv, jnp.dot(kinv, w[...], preferred_element_type=F32)

    kin, kv_a = _rowwise("kv_path", kv_path, [z_ckv, z_kpe], [heads_tile(2), heads_tile(2 * H_A)],
                         pos=[ck_t, sk_t], consts=[gains["g_kva"], wkv, pk], seq=seq)
    o_a, lse_a, *got = _attn_fwd("attn_a_fwd", q_a, kv_a, 0, kv_a, H_A, heads=H_A, group=1, nseq=nb, seq=seq,
                                 rider=rider_of("attn_a_fwd"))
    landed("attn_a_fwd", got)

    def prep_fwd(scale):
        def fn(v, cos, sin, g, pm):
            yv = _rms(v, g, HD_B)
            return ((yv * cos + _perm(yv, pm) * sin) * scale,)
        return fn

    z_qb, z_kb = (z, H_B * HEAD_W, ZC_QB // H_B, False), (z, KV_B * HEAD_W, ZC_KB // KV_B, False)
    def prep_b(zq, zk, cos, sin, gq, gk, pm):
        return (*_per_head(prep_fwd(SCALE_B), H_B, 1, 1)(zq, cos, sin, gq, pm), *_per_head(prep_fwd(1.0), KV_B, 1, 1)(zk, cos, sin, gk, pm))

    q_b, k_b, *got = _rowwise("prep_b", prep_b, [z_qb, z_kb], [heads_tile(H_B), heads_tile(KV_B)],
                              pos=[cb_t, sb_t], consts=[g_qn, g_kn, pb], seq=seq, rider=rider_of("prep_b"))
    landed("prep_b", got)
    o_b, lse_b, *got = _attn_fwd("attn_b_fwd", q_b, k_b, 0, z, ZC_VB, heads=H_B, group=H_B // KV_B, nseq=nb, seq=seq,
                                 rider=rider_of("attn_b_fwd"))
    landed("attn_b_fwd", got)
    woa = _pad_heads_rows(wts["w_oa"], H_A, V_DIM_A)
    wob = _pad_heads_rows(wts["w_ob"], H_B, HD_B)
    wo, wup, wdown = wts["w_o"], wts["w_up"], wts["w_down"]

    def residual_norm(acc, r, g):
        xv = r + acc
        return xv, _rms(xv, g, D_MODEL)

    def mix_out(oa, ob, ga, gb, r, g, w_a, w_b, w_out):
        a = jnp.dot(oa, w_a[...], preferred_element_type=F32)
        b = jnp.dot(ob, w_b[...], preferred_element_type=F32)
        mg = (_sigmoid(ga) * a + _sigmoid(gb) * b).astype(BF16)
        return (a, b, mg, *residual_norm(jnp.dot(mg, w_out[...], preferred_element_type=F32), r, g))

    z_ga, z_gb = (z, D_MODEL, ZC_GA // 8, False), (z, D_MODEL, ZC_GB // 8, False)
    wide = lambda d: (D_MODEL, d, D_MODEL, False)
    ya, yb, merged, x1, h2, *got = _rowwise("mix_out", mix_out, [full(o_a), full(o_b), z_ga, z_gb, full(x0)],
                                            [wide(BF16), wide(BF16), wide(BF16), wide(F32), wide(BF16)],
                                            consts=[gains["g_mlp"], woa, wob, wo], tm=256, rider=rider_of("mix_out"))
    landed("mix_out", got)
    wpg, wple = wts["w_ple_gate"], wts["w_ple"]

    square = lambda v: v * v
    u = _mm("mlp_up", h2, wup, b_slots=True, out_dtypes=(BF16,), epi=lambda acc: (jnp.maximum(acc, 0.0),), tm=1024)
    x2, h3 = _mm("mlp_down", u, wdown, a_pre=square, out_dtypes=(F32, BF16), epi=residual_norm, extras=(x1,), consts=[gains["g_ple"]])

    def norm_res_bwd(dh, v, res, g):
        dx, dg = _rms_bwd(dh, v, g, D_MODEL)
        return dx + res, dg

    def tail(x2v, h3v, pv, tv, gf, gp, w_gate, w_emb):
        sg = _sigmoid(jnp.dot(h3v, w_gate[...], preferred_element_type=F32))
        pev = jnp.dot(pv.astype(BF16), w_emb[...], preferred_element_type=F32)
        x3 = x2v + sg * pev
        rs = lax.rsqrt(jnp.sum(x3 * x3, axis=-1, keepdims=True) * (1.0 / D_MODEL) + EPS)
        xh = x3 * rs
        err = xh * gf - tv
        dy = err * (1.0 / D_MODEL)
        dyg = dy * gf
        dx3 = rs * (dyg - xh * (jnp.sum(dyg * xh, axis=-1, keepdims=True) * (1.0 / D_MODEL)))
        dgp = (dx3 * pev * sg * (1.0 - sg)).astype(BF16)
        dh3 = lax.dot_general(dgp, w_gate[...], (((1,), (1,)), ((), ())), preferred_element_type=F32)
        dx2v, dgple = norm_res_bwd(dh3, x2v, dx3, gp)
        return (dx2v, dgp, dx3 * sg, jnp.sum(err * err, axis=0, keepdims=True) * (0.5 / D_MODEL),
                jnp.sum(dy * xh, axis=0, keepdims=True), dgple)

    dx2, dgpre, dpe, loss_part, dg_final, dg_ple = _rowwise(
        "tail", tail, [full(x2), full(h3), full(p2), full(tg)], [wide(F32), wide(BF16), wide(BF16)],
        consts=[gains["g_final"].reshape(1, D_MODEL), gains["g_ple"], wpg, wple], accs=[(1, D_MODEL)] * 3, tm=256)

    dw = {}
    dw["w_ple"] = _mm_tn("dw_ple", p2, dpe)
    dw["w_ple_gate"] = _mm_tn("dw_ple_gate", h3, dgpre)
    dw["w_down"] = _mm_tn("dw_down", u, dx2, a_pre=square)
    dupre = _mm("d_mlp_down", dx2, wdown, trans_b=True, out_dtypes=(BF16,), epi=lambda acc, uv: (acc * (2.0 * uv.astype(F32)),),
                extras=(u,), tn=2048)
    dw["w_up"] = _mm_tn("dw_up", h2, dupre, out_slots=True)
    n_up = wup.shape[0]
    dx1, dg_mlp = _mm("d_mlp_up", [(dupre, j, wup.shape[2]) for j in range(n_up)], [(wup, j) for j in range(n_up)], trans_b=True,
                      epi=norm_res_bwd, extras=(x1, dx2), consts=[gains["g_mlp"]],
                      accs=[(1, D_MODEL)], tm=512)
    dw["w_o"] = _mm_tn("dw_o", merged, dx1)

    def merge_bwd(dm, ga, gb, a, b, w_a, w_b):
        sa, sb = _sigmoid(ga), _sigmoid(gb)
        da, db = (dm * sa).astype(BF16), (dm * sb).astype(BF16)
        nt = (((1,), (1,)), ((), ()))
        return (da, db, dm * a * sa * (1.0 - sa), dm * b * sb * (1.0 - sb),
                lax.dot_general(da, w_a, nt, preferred_element_type=F32), lax.dot_general(db, w_b, nt, preferred_element_type=F32))

    dya, dyb, dga, dgb, do_a, do_b = _mm("d_out_proj", dx1, wo, trans_b=True, out_dtypes=(BF16,) * 6, epi=merge_bwd,
                                         extras=((z, ZC_GA // 8), (z, ZC_GB // 8), ya, yb), consts=[woa, wob], tm=256)
    dw["w_oa"] = _unpad_heads_rows(_mm_tn("dw_oa", o_a, dya), H_A, V_DIM_A)
    dw["w_ob"] = _unpad_heads_rows(_mm_tn("dw_ob", o_b, dyb), H_B, HD_B)
    res_a = _attn_bwd("attn_a_bwd", q_a, kv_a, 0, kv_a, H_A, o_a, do_a, lse_a, heads=H_A, group=1, nseq=nb, seq=seq,
                      rider=ride and ride["scatter_a"](dw))
    dq_a, dk_a, dv_a = res_a[:3]
    if ride is not None:
        ride["out"]["parts_a"] = res_a[3:]

    def rope_bwd(scale):
        return lambda d, cos, sin, pm_t: ((d * cos + _perm(d * sin, pm_t)) * scale,)

    nt_dims = (((1,), (1,)), ((), ()))

    def q_path_bwd(dq, zq, cos, sin, g, w, pm_t):
        dqav = _per_head(rope_bwd(SCALE_A), H_A, 1, 1)(dq, cos, sin, pm_t)[0].astype(BF16)
        dcq = lax.dot_general(dqav, w[...], nt_dims, preferred_element_type=F32)
        return (dqav, *_rms_bwd(dcq, zq, g, Q_LORA))

    dqa, dq_lat, dg_qa = _rowwise("q_path_bwd", q_path_bwd, [full(dq_a), z_qlat], [heads_tile(H_A), (Q_LORA, BF16, Q_LORA, False)],
                                  pos=[cq_t, sq_t], consts=[gains["g_qa"], wqb, pq_t], accs=[(1, Q_LORA)], seq=seq)
    dw["w_qb"] = _unpad_heads_cols(_mm_tn("dw_qb", cq, dqa), H_A, QK_NOPE + QK_ROPE)
    dw["w_kvb"] = _wkv_unext(_mm_tn("dw_kv_k", kin, dk_a), _mm_tn("dw_kv_v", kin, dv_a))
    dq_b, dk_b, dv_b, *parts_b = _attn_bwd("attn_b_bwd", q_b, k_b, 0, z, ZC_VB, o_b, do_b, lse_b, heads=H_B, group=H_B // KV_B,
                                               nseq=nb, seq=seq, rider=ride and ride["scatter_b"](dw))
    if ride is not None:
        ride["out"]["parts_b"] = parts_b

    def kv_path_bwd(dk, dv, ckv_raw, cos, sin, g, w, pm_t):
        kv_w = H_A * HEAD_W
        wv = w[...]
        dkin = (lax.dot_general(dk, wv[:, :kv_w], nt_dims, preferred_element_type=F32)
                + lax.dot_general(dv, wv[:, kv_w:], nt_dims, preferred_element_type=F32))
        dckv_raw, dg = _rms_bwd(dkin[:, :HEAD_W], ckv_raw, g, KV_LORA)
        return (dckv_raw, *rope_bwd(1.0)(dkin[:, HEAD_W:], cos, sin, pm_t), dg)

    dckv, dkpe, dg_kva = _rowwise("kv_path_bwd", kv_path_bwd, [full(dk_a), full(dv_a), z_ckv], [heads_tile(1), heads_tile(1)],
                                  pos=[ck_t, sk_t], consts=[gains["g_kva"], wkv, pk_t], accs=[(1, KV_LORA)], seq=seq)

    def prep_bwd(scale):
        def fn(d, v, cos, sin, g, pm_t):
            dyv = (d * cos + _perm(d * sin, pm_t)) * scale
            return _rms_bwd(dyv, v, g, HD_B)
        return fn

    def prep_b_bwd(dq, dk, zq, zk, cos, sin, gq, gk, pm_t):
        dq_raw, dgq = _per_head(prep_bwd(SCALE_B), H_B, 2, 1)(dq, zq, cos, sin, gq, pm_t)
        dk_raw, dgk = _per_head(prep_bwd(1.0), KV_B, 2, 1)(dk, zk, cos, sin, gk, pm_t)
        return dq_raw, dk_raw, dgq, dgk

    dqb, dkb, dg_qn, dg_kn = _rowwise("prep_b_bwd", prep_b_bwd, [full(dq_b), full(dk_b), z_qb, z_kb], [heads_tile(H_B), heads_tile(KV_B)],
                                      pos=[cb_t, sb_t], consts=[g_qn, g_kn, pb_t], accs=[(1, HEAD_W)] * 2, seq=seq)

    dz = [dqb, jnp.concatenate([dq_lat, dckv, dkpe, dkb, dv_b], axis=1), dga, dgb]
    dw["w_in"] = _win_unext([_mm_tn("dw_in_%d" % j, h, blk) for j, blk in enumerate(dz)])
    dx0, dg_mix, *parts_in = _mm("d_in_proj", dz, [(win, j, D_MODEL) for j in range(4)], trans_b=True, epi=norm_res_bwd, extras=(x0, dx1), consts=[gains["g_mix"]],
                                 accs=[(1, D_MODEL)], tm=256, rider=ride and ride["scatter_in"](dw))
    if ride is not None:
        ride["out"]["parts_in"] = parts_in

    dg = {"g_mix": dg_mix, "g_qa": dg_qa, "g_kva": dg_kva, "g_qn": dg_qn[:, :HD_B], "g_kn": dg_kn[:, :HD_B],
          "g_mlp": dg_mlp, "g_ple": dg_ple, "g_final": dg_final}
    return loss_part, dx0.reshape(nb, seq, D_MODEL), dg, dw


def _pack_small(vals, loss_part=None):
    flat = jnp.concatenate([vals[n].reshape(1, -1) for n, _ in SMALL], axis=1)
    loss = jnp.zeros((1, 8 * 128), F32) if loss_part is None else loss_part
    gap = jnp.zeros((1, LOSS_ROW0 * 128 - SMALL_N), F32)
    return jnp.concatenate([flat, gap, loss], axis=1).reshape(SMALL_ROWS, 128)


def _unpack_small(slab, like):
    flat, out, off = slab.reshape(-1), {}, 0
    for n, k in SMALL:
        out[n] = flat[off:off + k].reshape(like[n].shape)
        off += k
    return out


def kernel(x, p, g_mix, w_in, g_qa, w_qb, g_kva, w_kvb, g_qn, g_kn, w_oa, w_ob, w_o, g_mlp, w_up, w_down, g_ple, w_ple_gate, w_ple, g_final, loss_target, m_g_mix, m_w_in, m_g_qa, m_w_qb, m_g_kva, m_w_kvb, m_g_qn, m_g_kn, m_w_oa, m_w_ob, m_w_o, m_g_mlp, m_w_up, m_w_down, m_g_ple, m_w_ple_gate, m_w_ple, m_g_final, v_g_mix, v_w_in, v_g_qa, v_w_qb, v_g_kva, v_w_kvb, v_g_qn, v_g_kn, v_w_oa, v_w_ob, v_w_o, v_g_mlp, v_w_up, v_w_down, v_g_ple, v_w_ple_gate, v_w_ple, v_g_final):
    given = dict(locals())
    order = ["g_mix", "w_in", "g_qa", "w_qb", "g_kva", "w_kvb", "g_qn", "g_kn", "w_oa", "w_ob", "w_o", "g_mlp", "w_up",
             "w_down", "g_ple", "w_ple_gate", "w_ple", "g_final"]
    big_names = [n for n, _, _, _ in BIG]
    local = lambda prefix, names: [given[prefix + n][0] for n in names]
    slab = lambda names: _pack_shards(names, local("", names), BF16)
    bf = lambda n: given[n][0].astype(BF16)
    cols_full = lambda g: g.transpose(1, 0, 2).reshape(g.shape[1], -1)
    rows_full = lambda g: g.reshape(-1, g.shape[2])
    shards_cols = lambda a: a.reshape(a.shape[0], 4, a.shape[1] // 4).transpose(1, 0, 2)
    shards_rows = lambda a: a.reshape(4, a.shape[0] // 4, a.shape[1])
    packed = lambda names, dw: _pack_full(names, [dw[n] for n in names], BF16)
    branch_out = ["w_oa", "w_ob"]
    back_a, back_b = SLAB_LATE + ["w_o"], SLAB_EARLY + ["w_ple_gate"]

    got_in, got_early = _gather_by_halves("weight_gather_early", [bf("w_in"), slab(SLAB_EARLY)])
    wts = {"w_in": cols_full(got_in), **dict(zip(SLAB_EARLY, _unpack_full(SLAB_EARLY, got_early)))}
    gains = {n: given[n].reshape(1, -1) for n, _ in SMALL}
    ride = {
        "gather": {
            "in_proj": (_Exchange("gather", [bf("w_o")]), lambda got: {"w_o": rows_full(got[0])}),
            "prep_b": (_Exchange("gather", [slab(branch_out)]), lambda got: dict(zip(branch_out, _unpack_full(branch_out, got[0])))),
            "attn_a_fwd": (_Exchange("gather", [bf("w_up")]), lambda got: {"w_up": got[0]}),
            "attn_b_fwd": (_Exchange("gather", [bf("w_down")]), lambda got: {"w_down": rows_full(got[0])}),
            "mix_out": (_Exchange("gather", [bf("w_ple_gate"), bf("w_ple")]),
                        lambda got: {"w_ple_gate": rows_full(got[0]), "w_ple": cols_full(got[1])}),
        },
        "scatter_a": lambda dw: _Exchange("scatter", [dw["w_up"], packed(back_a, dw)]),
        "scatter_b": lambda dw: _Exchange("scatter", [shards_rows(dw["w_down"]), packed(back_b, dw)]),
        "scatter_in": lambda dw: _Exchange("scatter", [_presum_halves("grad_presum_in", shards_cols(dw["w_in"]))]),
        "out": {},
    }
    loss_part, grad_x, dg, dw = _local_step(x, p[0], loss_target, gains, wts, ride)

    small = lambda prefix: _pack_small({n: given[prefix + n] for n, _ in SMALL})
    g_s, d_s, m_s, v_s, loss = _small_allreduce_adamw(_pack_small(dg, loss_part), small(""), small("m_"), small("v_"))

    parts = ride["out"]
    grads = {"w_up": _reduce_pair("grad_reduce_up", parts["parts_a"][0]), "w_down": _reduce_pair("grad_reduce_down", parts["parts_b"][0]),
             "w_in": _reduce_halves("grad_reduce_in", parts["parts_in"][0])}
    grads.update(zip(back_a, _unpack_shards(back_a, _reduce_pair("grad_reduce_slab_a", parts["parts_a"][1]))))
    grads.update(zip(back_b, _unpack_shards(back_b, _reduce_pair("grad_reduce_slab_b", parts["parts_b"][1]))))

    res = {}
    for key, slab in (("grad_", g_s), ("delta_", d_s), ("new_m_", m_s), ("new_v_", v_s)):
        for n, val in _unpack_small(slab, given).items():
            res[key + n] = val
    for n in big_names:
        res["grad_" + n], res["delta_" + n], res["new_m_" + n], res["new_v_" + n] = _adamw_shard(
            "adamw_" + n, grads[n], given[n], given["m_" + n], given["v_" + n])
    outs = [loss.reshape(()), grad_x]
    for key in ("grad_", "delta_", "new_m_", "new_v_"):
        outs += [res[key + n] for n in order]
    return tuple(outs)
```

```python
import functools

import numpy as np
import jax
import jax.numpy as jnp
from jax import lax
from jax.experimental import pallas as pl
from jax.experimental.pallas import tpu as pltpu

F32 = jnp.float32
BF16 = jnp.bfloat16
MESH = pl.DeviceIdType.MESH

D_MODEL = 1024
GRID_W = 64
ROPE_THETA = 10000.0
EPS = 1e-6
H_A, QK_NOPE, QK_ROPE, V_DIM_A, Q_LORA, KV_LORA = 8, 64, 32, 64, 256, 128
H_B, KV_B, HD_B = 8, 2, 64
D_FF = 4096
PLE_DIM = 256
HEAD_W = 128
SCALE_A = (QK_NOPE + QK_ROPE) ** -0.5
SCALE_B = HD_B ** -0.5

ADAM_LR, ADAM_B1, ADAM_B2, ADAM_EPS, ADAM_WD, ADAM_STEP = 0.001, 0.9, 0.999, 1e-08, 0.01, 10
M_HAT_DIV = 1.0 - ADAM_B1 ** ADAM_STEP
V_HAT_DIV = 1.0 - ADAM_B2 ** ADAM_STEP

VMEM_LIMIT_BYTES = 56 * 1024 * 1024

ZC_QB, ZC_QLAT, ZC_CKV, ZC_KPE, ZC_KB, ZC_VB, ZC_GA, ZC_GB = 0, 8, 10, 11, 12, 14, 16, 24
Z_WIDTH = 32 * HEAD_W

BIG = [
    ("w_in", 1024, 3232, 1), ("w_qb", 256, 768, 1), ("w_kvb", 128, 1024, 1), ("w_oa", 512, 1024, 1),
    ("w_ob", 512, 1024, 1), ("w_o", 1024, 1024, 0), ("w_up", 1024, 4096, 1), ("w_down", 4096, 1024, 0),
    ("w_ple_gate", 1024, 1024, 0), ("w_ple", 256, 1024, 1),
]
BIG_BY_NAME = {e[0]: e for e in BIG}
PACK_W = 1024
PACK_ALIGN = 64
SLAB_EARLY = ["w_qb", "w_kvb"]
SLAB_LATE = ["w_oa", "w_ob", "w_ple"]

SMALL = [("g_mix", 1024), ("g_qa", 256), ("g_kva", 128), ("g_qn", 64), ("g_kn", 64), ("g_mlp", 1024),
         ("g_ple", 1024), ("g_final", 1024)]
SMALL_N = sum(n for _, n in SMALL)
LOSS_ROW0 = 40
SMALL_ROWS = 48


def _params(sem):
    return pltpu.CompilerParams(dimension_semantics=sem, vmem_limit_bytes=VMEM_LIMIT_BYTES)


def _sigmoid(v):
    return 1.0 / (1.0 + jnp.exp(-v.astype(F32)))


def _perm(v, p_ref):
    pm = p_ref[...]
    hi = v.astype(BF16)
    lo = (v - hi.astype(F32)).astype(BF16)
    return (jnp.dot(hi, pm, preferred_element_type=F32) + jnp.dot(lo, pm, preferred_element_type=F32))


def _rms(v, g, n):
    v = v.astype(F32)
    rs = lax.rsqrt(jnp.sum(v * v, axis=-1, keepdims=True) * (1.0 / n) + EPS)
    return v * rs * g


def _rms_bwd(dy, v, g, n):
    v = v.astype(F32)
    rs = lax.rsqrt(jnp.sum(v * v, axis=-1, keepdims=True) * (1.0 / n) + EPS)
    vh = v * rs
    dyg = dy * g
    dx = rs * (dyg - vh * (jnp.sum(dyg * vh, axis=-1, keepdims=True) * (1.0 / n)))
    return dx, jnp.sum(dy * vh, axis=0, keepdims=True)


def _ride(body, grid, rider):
    if rider is None:
        return body, [], [], [], []
    n_x, n_sem = len(rider.srcs), len(rider.scratch)

    def wrapped(*refs):
        ids = [pl.program_id(a) for a in range(len(grid))]
        n_in = len(refs) - n_sem - 2 * n_x - rider.n_core_out - rider.n_core_scratch
        core_in, srcs = refs[:n_in], refs[n_in:n_in + n_x]
        core_out = refs[n_in + n_x:n_in + n_x + rider.n_core_out]
        dsts = refs[n_in + n_x + rider.n_core_out:n_in + 2 * n_x + rider.n_core_out]
        core_scr = refs[n_in + 2 * n_x + rider.n_core_out:len(refs) - n_sem]
        sems = refs[len(refs) - n_sem:]

        @pl.when(functools.reduce(jnp.logical_and, [a == 0 for a in ids]))
        def _():
            rider.start(srcs, dsts, *sems)

        body(*core_in, *core_out, *core_scr)

        @pl.when(functools.reduce(jnp.logical_and, [a == n - 1 for a, n in zip(ids, grid)]))
        def _():
            rider.finish(srcs, dsts, *sems)

    hbm = pl.BlockSpec(memory_space=pl.ANY)
    return wrapped, list(rider.srcs), [hbm] * n_x, list(rider.out_shapes), list(rider.scratch)


def _mm(name, a, b, *, trans_b=False, b_slots=False, a_pre=None, out_dtypes=(F32,), epi=None, extras=(), consts=(), accs=(), tm=512,
        tn=None, rider=None):
    a_ops = [o if isinstance(o, tuple) else (o, 0, o.shape[1]) for o in (a if isinstance(a, list) else [a])]
    b_ops = b if isinstance(b, list) else [b]
    assert len(a_ops) == len(b_ops) and not (b_slots and (trans_b or len(b_ops) > 1))
    m = a_ops[0][0].shape[0]
    if b_slots:
        n, tn = b.shape[0] * b.shape[2], b.shape[2]
    else:
        first = b_ops[0][0] if isinstance(b_ops[0], tuple) else b_ops[0]
        n = first.shape[-2] if trans_b else first.shape[1]
        tn = n if tn is None else min(tn, n)
    tm = min(tm, m)
    assert m % tm == 0 and n % tn == 0
    extras = [e if isinstance(e, tuple) else (e, 0) for e in extras]
    n_p, n_ex, n_c, n_out, n_acc = len(a_ops), len(extras), len(consts), len(out_dtypes), len(accs)
    dims = (((1,), (1,)), ((), ())) if trans_b else (((1,), (0,)), ((), ()))

    def body(*refs):
        acc = None
        for a_ref, b_ref in zip(refs[:n_p], refs[n_p:2 * n_p]):
            av = a_ref[...] if a_pre is None else a_pre(a_ref[...].astype(F32))
            part = lax.dot_general(av.astype(BF16), b_ref[...].astype(BF16), dims, preferred_element_type=F32)
            acc = part if acc is None else acc + part
        rest = refs[2 * n_p:]
        res = (acc,) if epi is None else epi(acc, *[e[...] for e in rest[:n_ex + n_c]])
        o_refs = rest[n_ex + n_c:]
        for o_ref, r in zip(o_refs[:n_out], res[:n_out]):
            o_ref[...] = r.astype(o_ref.dtype)
        if n_acc:
            first_step = jnp.logical_and(pl.program_id(0) == 0, pl.program_id(1) == 0)

            @pl.when(first_step)
            def _():
                for o_ref, r in zip(o_refs[n_out:], res[n_out:]):
                    o_ref[...] = r

            @pl.when(jnp.logical_not(first_step))
            def _():
                for o_ref, r in zip(o_refs[n_out:], res[n_out:]):
                    o_ref[...] += r

    def b_spec(op, k_i):
        if b_slots:
            return pl.BlockSpec((None, k_i, tn), lambda j, i: (j, 0, 0))
        if not isinstance(op, tuple):
            return pl.BlockSpec((tn, k_i), lambda j, i: (j, 0)) if trans_b else pl.BlockSpec((k_i, tn), lambda j, i: (0, j))
        assert trans_b
        if len(op) == 2:
            return pl.BlockSpec((None, tn, k_i), lambda j, i, slot=op[1]: (slot, j, 0))
        return pl.BlockSpec((tn, k_i), lambda j, i, blk=op[1]: (j, blk))

    grid = (n // tn, m // tm)
    if rider is not None:
        rider.n_core_out, rider.n_core_scratch = n_out + n_acc, 0
    body, x_in, x_spec, x_out, x_scr = _ride(body, grid, rider)
    a_specs = [pl.BlockSpec((tm, k_i), lambda j, i, blk=blk: (i, blk)) for _, blk, k_i in a_ops]
    b_specs = [b_spec(op, k_i) for op, (_, _, k_i) in zip(b_ops, a_ops)]
    t_spec = pl.BlockSpec((tm, tn), lambda j, i: (i, j))
    e_specs = [pl.BlockSpec((tm, tn), lambda j, i, off=off: (i, j + off)) for _, off in extras]
    c_specs = [pl.BlockSpec(c.shape, lambda j, i: (0, 0)) for c in consts]
    acc_specs = [pl.BlockSpec(sh, lambda j, i: (0, 0)) for sh in accs]
    sem = ("parallel", "parallel") if rider is None and not n_acc else ("arbitrary", "arbitrary")
    outs = pl.pallas_call(
        body, out_shape=[jax.ShapeDtypeStruct((m, n), d) for d in out_dtypes] + [jax.ShapeDtypeStruct(sh, F32) for sh in accs] + x_out,
        grid=grid, in_specs=a_specs + b_specs + e_specs + c_specs + x_spec, out_specs=[t_spec] * n_out + acc_specs + x_spec,
        scratch_shapes=x_scr, compiler_params=_params(sem),
        name=name)(*[o[0] for o in a_ops], *[o[0] if isinstance(o, tuple) else o for o in b_ops], *[e for e, _ in extras], *consts, *x_in)
    return outs[0] if len(outs) == 1 else outs


def _per_head(fn, heads, n_tiled, n_out):
    def run(*args):
        res = [fn(*[a[:, hd * HEAD_W:(hd + 1) * HEAD_W] for a in args[:n_tiled]], *args[n_tiled:]) for hd in range(heads)]
        tiles = [jnp.concatenate([r[k] for r in res], axis=1) for k in range(n_out)]
        sums = [functools.reduce(lambda u, v: u + v, [r[k] for r in res]) for k in range(n_out, len(res[0]))]
        return (*tiles, *sums)
    return run


def _mm_tn(name, a, b, *, a_pre=None, out_dtype=BF16, out_slots=False, tk=1024, tn=1024, tt=4096):
    t, k = a.shape
    n = b.shape[1]
    tk, tn = min(tk, k), min(tn, n)
    if a.dtype == F32 or b.dtype == F32:
        tt = tt // 2
    if k == tk and n == tn:
        tt = tt // 2
    tt = min(tt, t)
    assert b.shape[0] == t and k % tk == 0 and n % tn == 0 and t % tt == 0
    nt = t // tt

    def body(a_ref, b_ref, o_ref, acc):
        av = a_ref[...] if a_pre is None else a_pre(a_ref[...].astype(F32))
        part = lax.dot_general(av.astype(BF16), b_ref[...].astype(BF16), (((0,), (0,)), ((), ())), preferred_element_type=F32)

        @pl.when(pl.program_id(2) == 0)
        def _():
            acc[...] = part

        @pl.when(pl.program_id(2) != 0)
        def _():
            acc[...] += part

        @pl.when(pl.program_id(2) == nt - 1)
        def _():
            o_ref[...] = acc[...].astype(o_ref.dtype)

    if out_slots:
        out_shape, out_spec = (n // tn, k, tn), pl.BlockSpec((None, tk, tn), lambda ki, ni, ti: (ni, ki, 0))
    else:
        out_shape, out_spec = (k, n), pl.BlockSpec((tk, tn), lambda ki, ni, ti: (ki, ni))
    return pl.pallas_call(
        body, out_shape=jax.ShapeDtypeStruct(out_shape, out_dtype), grid=(k // tk, n // tn, nt),
        in_specs=[pl.BlockSpec((tt, tk), lambda ki, ni, ti: (ti, ki)), pl.BlockSpec((tt, tn), lambda ki, ni, ti: (ti, ni))],
        out_specs=out_spec, scratch_shapes=[pltpu.VMEM((tk, tn), F32)],
        compiler_params=_params(("parallel", "parallel", "arbitrary")), name=name)(a, b)


def _mm_tn_shared(name, a, bs):
    t, k = a.shape
    n, nb = bs[0].shape[1], len(bs)

    def body(a_ref, *refs):
        b_refs, o_ref, buf, sems = refs[:nb], refs[nb], refs[nb + 1], refs[nb + 2]
        j = pl.program_id(0)

        def fetch(jj):
            return pltpu.make_async_copy(b_refs[jj], buf.at[jj % 2], sems.at[jj % 2])

        @pl.when(j == 0)
        def _():
            fetch(0).start()

        for jj in range(nb):
            @pl.when(j == jj)
            def _(jj=jj):
                if jj + 1 < nb:
                    fetch(jj + 1).start()
                fetch(jj).wait()

        o_ref[...] = lax.dot_general(a_ref[...].astype(BF16), buf[j % 2], (((0,), (0,)), ((), ())),
                                     preferred_element_type=F32).astype(o_ref.dtype)

    return pl.pallas_call(
        body, out_shape=jax.ShapeDtypeStruct((nb, k, n), BF16), grid=(nb,),
        in_specs=[pl.BlockSpec((t, k), lambda j: (0, 0))] + [pl.BlockSpec(memory_space=pl.ANY)] * nb,
        out_specs=pl.BlockSpec((None, k, n), lambda j: (j, 0, 0)),
        scratch_shapes=[pltpu.VMEM((2, t, n), BF16), pltpu.SemaphoreType.DMA((2,))],
        compiler_params=_params(("arbitrary",)), name=name)(a, *bs)


def _rowwise(name, fn, ins, outs, *, consts=(), pos=(), accs=(), heads=1, tm=512, seq=None, rider=None):
    t = ins[0][0].shape[0]
    tm = min(tm, t if seq is None else seq)
    assert t % tm == 0 and (seq is None or seq % tm == 0)
    n_in, n_pos, n_c, n_out, n_acc = len(ins), len(pos), len(consts), len(outs), len(accs)

    def body(*refs):
        vals = [r[...] for r in refs[:n_in + n_pos + n_c]]
        res = fn(*vals)
        o_refs = refs[n_in + n_pos + n_c:]
        for o_ref, r in zip(o_refs[:n_out], res[:n_out]):
            o_ref[...] = r.astype(o_ref.dtype)
        if n_acc:
            first = jnp.logical_and(pl.program_id(0) == 0, pl.program_id(1) == 0)

            @pl.when(first)
            def _():
                for o_ref, r in zip(o_refs[n_out:], res[n_out:]):
                    o_ref[...] = r

            @pl.when(jnp.logical_not(first))
            def _():
                for o_ref, r in zip(o_refs[n_out:], res[n_out:]):
                    o_ref[...] += r

    def tiled(width, c0, per_head):
        return pl.BlockSpec((tm, width), (lambda h, i: (i, c0 + h)) if per_head else (lambda h, i: (i, c0)))

    in_specs = [tiled(w, c0, ph) for _, w, c0, ph in ins]
    if n_pos:
        nblk = seq // tm
        in_specs += [pl.BlockSpec((tm, a.shape[1]), lambda h, i: (i % nblk, 0)) for a in pos]
    in_specs += [pl.BlockSpec(a.shape, lambda h, i: (0, 0)) for a in consts]
    out_specs = [tiled(w, 0, ph) for _, _, w, ph in outs] + [pl.BlockSpec(s, lambda h, i: (0, 0)) for s in accs]
    out_shape = [jax.ShapeDtypeStruct((t, c), d) for c, d, _, _ in outs] + [jax.ShapeDtypeStruct(s, F32) for s in accs]
    sem = ("arbitrary", "arbitrary") if n_acc or rider is not None else ("parallel", "parallel")
    grid = (heads, t // tm)
    if rider is not None:
        rider.n_core_out, rider.n_core_scratch = n_out + n_acc, 0
    body, x_in, x_spec, x_out, x_scr = _ride(body, grid, rider)
    res = pl.pallas_call(body, out_shape=out_shape + x_out, grid=grid, in_specs=in_specs + x_spec, out_specs=out_specs + x_spec,
                         scratch_shapes=x_scr, compiler_params=_params(sem), name=name)(*[a for a, _, _, _ in ins], *pos, *consts, *x_in)
    return res[0] if len(res) == 1 else res


ATTN_HEADS_PER_STEP = 4


def _attn_fwd(name, q, k, kc0, v, vc0, *, heads, group, nseq, seq, tq=512, rider=None):
    tq = min(tq, seq)
    nq = seq // tq
    hp = ATTN_HEADS_PER_STEP
    grid = (heads // hp, nseq, nq)
    shared = group > 1
    assert group % hp == 0 if shared else (kc0 % hp == 0 and vc0 % hp == 0)

    def body(q_ref, k_ref, v_ref, o_ref, lse_ref):
        for j in range(hp):
            cols = slice(j * HEAD_W, (j + 1) * HEAD_W)
            kj = (k_ref[...] if shared else k_ref[:, cols]).astype(BF16)
            vj = (v_ref[...] if shared else v_ref[:, cols]).astype(BF16)
            s = lax.dot_general(q_ref[:, cols], kj, (((1,), (1,)), ((), ())), preferred_element_type=F32)
            m = jnp.max(s, axis=-1, keepdims=True)
            p = jnp.exp((s - m).astype(BF16))
            vj = jnp.where(lax.broadcasted_iota(jnp.int32, (1, HEAD_W), 1) == HEAD_W - 1, jnp.ones((), BF16), vj)
            o = jnp.dot(p, vj, preferred_element_type=F32)
            l = o[:, HEAD_W - 1:]
            o_ref[:, cols] = (o * (1.0 / l)).astype(o_ref.dtype)
            lse_ref[j] = m + jnp.log(l)

    if rider is not None:
        rider.n_core_out, rider.n_core_scratch = 2, 0
    body, x_in, x_spec, x_out, x_scr = _ride(body, grid, rider)
    q_spec = pl.BlockSpec((tq, hp * HEAD_W), lambda h, b, i: (b * nq + i, h))
    if shared:
        k_spec = pl.BlockSpec((seq, HEAD_W), lambda h, b, i: (b, kc0 + (h * hp) // group))
        v_spec = pl.BlockSpec((seq, HEAD_W), lambda h, b, i: (b, vc0 + (h * hp) // group))
    else:
        k_spec = pl.BlockSpec((seq, hp * HEAD_W), lambda h, b, i: (b, kc0 // hp + h))
        v_spec = pl.BlockSpec((seq, hp * HEAD_W), lambda h, b, i: (b, vc0 // hp + h))
    lse_spec = pl.BlockSpec((hp, tq, 1), lambda h, b, i: (h, b * nq + i, 0))
    sem = ("parallel",) * 3 if rider is None else ("arbitrary",) * 3
    return pl.pallas_call(
        body, out_shape=[jax.ShapeDtypeStruct(q.shape, BF16), jax.ShapeDtypeStruct((heads, q.shape[0], 1), F32)] + x_out,
        grid=grid, in_specs=[q_spec, k_spec, v_spec] + x_spec, out_specs=[q_spec, lse_spec] + x_spec, scratch_shapes=x_scr,
        compiler_params=_params(sem), name=name)(q, k, v, *x_in)


def _attn_bwd(name, q, k, kc0, v, vc0, o, do, lse, *, heads, group, nseq, seq, tq=1024, rider=None):
    tq = min(tq, seq)
    nq = seq // tq
    hk = heads // group
    t = q.shape[0]
    grid = (hk, nseq, group, nq)

    def body(q_ref, k_ref, v_ref, o_ref, do_ref, lse_ref, dq_ref, dk_ref, dv_ref, dk_acc, dv_acc):
        g, i = pl.program_id(2), pl.program_id(3)
        qv, kv, vv, dov = q_ref[...], k_ref[...].astype(BF16), v_ref[...].astype(BF16), do_ref[...]
        s = lax.dot_general(qv, kv, (((1,), (1,)), ((), ())), preferred_element_type=F32)
        pn = jnp.exp(s - lse_ref[...])
        dp = lax.dot_general(dov, vv, (((1,), (1,)), ((), ())), preferred_element_type=F32)
        delta = jnp.sum(dov.astype(F32) * o_ref[...].astype(F32), axis=-1, keepdims=True)
        ds = (pn * (dp - delta)).astype(BF16)
        dq_ref[...] = jnp.dot(ds, kv, preferred_element_type=F32)
        dk_part = lax.dot_general(ds, qv, (((0,), (0,)), ((), ())), preferred_element_type=F32)
        dv_part = lax.dot_general(pn.astype(BF16), dov, (((0,), (0,)), ((), ())), preferred_element_type=F32)
        first = jnp.logical_and(g == 0, i == 0)

        @pl.when(first)
        def _():
            dk_acc[...] = dk_part
            dv_acc[...] = dv_part

        @pl.when(jnp.logical_not(first))
        def _():
            dk_acc[...] += dk_part
            dv_acc[...] += dv_part

        @pl.when(jnp.logical_and(g == group - 1, i == nq - 1))
        def _():
            dk_ref[...] = dk_acc[...].astype(dk_ref.dtype)
            dv_ref[...] = dv_acc[...].astype(dv_ref.dtype)

    if rider is not None:
        rider.n_core_out, rider.n_core_scratch = 3, 2
    body, x_in, x_spec, x_out, x_scr = _ride(body, grid, rider)
    q_spec = pl.BlockSpec((tq, HEAD_W), lambda kh, b, g, i: (b * nq + i, kh * group + g))
    kv_out = pl.BlockSpec((seq, HEAD_W), lambda kh, b, g, i: (b, kh))
    lse_spec = pl.BlockSpec((None, tq, 1), lambda kh, b, g, i: (kh * group + g, b * nq + i, 0))
    sem = ("parallel", "parallel", "arbitrary", "arbitrary") if rider is None else ("arbitrary",) * 4
    return pl.pallas_call(
        body,
        out_shape=[jax.ShapeDtypeStruct(q.shape, F32), jax.ShapeDtypeStruct((t, hk * HEAD_W), BF16),
                   jax.ShapeDtypeStruct((t, hk * HEAD_W), BF16)] + x_out,
        grid=grid,
        in_specs=[q_spec, pl.BlockSpec((seq, HEAD_W), lambda kh, b, g, i: (b, kc0 + kh)),
                  pl.BlockSpec((seq, HEAD_W), lambda kh, b, g, i: (b, vc0 + kh)), q_spec, q_spec, lse_spec] + x_spec,
        out_specs=[q_spec, kv_out, kv_out] + x_spec,
        scratch_shapes=[pltpu.VMEM((seq, HEAD_W), F32), pltpu.VMEM((seq, HEAD_W), F32)] + x_scr,
        compiler_params=_params(sem), name=name)(q, k, v, o, do, lse, *x_in)


def _place():
    return lax.axis_index("x"), lax.axis_index("y"), lax.axis_index("c")


def _other_chips(x, y):
    return [(1 - x, y), (x, 1 - y), (1 - x, 1 - y)]


class _Exchange:
    def __init__(self, kind, srcs):
        assert kind in ("gather", "scatter")
        self.kind, self.srcs = kind, list(srcs)
        n = len(self.srcs)
        self.out_shapes = [jax.ShapeDtypeStruct((4, *a.shape[-2:]), a.dtype) for a in self.srcs]
        self.scratch = [pltpu.SemaphoreType.DMA((3 * n,)), pltpu.SemaphoreType.DMA((3 * n,)), pltpu.SemaphoreType.DMA((n,))]
        self.n_core_out = self.n_core_scratch = 0

    def _copies(self, j, src_ref, out_ref, send_sems, recv_sems, landing):
        x, y, c = _place()

        def remote(k, s, d, to):
            return pltpu.make_async_remote_copy(src_ref=s, dst_ref=d, send_sem=send_sems.at[3 * j + k], recv_sem=recv_sems.at[3 * j + k],
                                                device_id=to, device_id_type=MESH)

        me = 2 * x + y
        part = (lambda i: src_ref) if self.kind == "gather" else (lambda i: src_ref.at[i])
        if landing:
            return [remote(k, part(me), out_ref.at[2 * px + py], (px, py, c)) for k, (px, py) in enumerate(_other_chips(x, y))]
        return [remote(k, part(2 * px + py), out_ref.at[me], (px, py, c)) for k, (px, py) in enumerate(_other_chips(x, y))]

    def _local(self, j, src_ref, out_ref, local_sems):
        x, y, _ = _place()
        me = 2 * x + y
        return pltpu.make_async_copy(src_ref if self.kind == "gather" else src_ref.at[me], out_ref.at[me], local_sems.at[j])

    def start(self, src_refs, out_refs, send_sems, recv_sems, local_sems):
        for j, (src_ref, out_ref) in enumerate(zip(src_refs, out_refs)):
            self._local(j, src_ref, out_ref, local_sems).start()
            for mine in self._copies(j, src_ref, out_ref, send_sems, recv_sems, False):
                mine.start()

    def finish(self, src_refs, out_refs, send_sems, recv_sems, local_sems):
        for j, (src_ref, out_ref) in enumerate(zip(src_refs, out_refs)):
            for landed in self._copies(j, src_ref, out_ref, send_sems, recv_sems, True):
                landed.wait_recv()
        for j, (src_ref, out_ref) in enumerate(zip(src_refs, out_refs)):
            for mine in self._copies(j, src_ref, out_ref, send_sems, recv_sems, False):
                mine.wait_send()
            self._local(j, src_ref, out_ref, local_sems).wait()


def _gather_by_halves(name, srcs):
    n = len(srcs)

    def body(*refs):
        x, y, c = _place()
        me = 2 * x + y
        local_sems = refs[-1]
        copies = []
        for j in range(n):
            src_ref, out_ref, send_sems, recv_sems = refs[j], refs[n + j], refs[2 * n + 2 * j], refs[2 * n + 2 * j + 1]
            half = srcs[j].shape[0] // 2
            rows_c = pl.ds(pl.multiple_of(c * half, half), half)
            rows_s = pl.ds(pl.multiple_of((1 - c) * half, half), half)

            def remote(k, s_ref, d_ref, to, send_sems=send_sems, recv_sems=recv_sems):
                return pltpu.make_async_remote_copy(src_ref=s_ref, dst_ref=d_ref, send_sem=send_sems.at[k], recv_sem=recv_sems.at[k],
                                                    device_id=to, device_id_type=MESH)

            local = pltpu.make_async_copy(src_ref, out_ref.at[me], local_sems.at[j])
            local.start()
            chips = _other_chips(x, y)
            sent = [remote(k, src_ref.at[rows_c], out_ref.at[me, rows_c], (px, py, c)) for k, (px, py) in enumerate(chips)]
            landing = [remote(k, src_ref.at[rows_c], out_ref.at[2 * px + py, rows_c], (px, py, c)) for k, (px, py) in enumerate(chips)]
            passed = [remote(3 + k, out_ref.at[2 * px + py, rows_c], out_ref.at[2 * px + py, rows_c], (x, y, 1 - c))
                      for k, (px, py) in enumerate(chips)]
            from_sibling = [remote(3 + k, out_ref.at[2 * px + py, rows_s], out_ref.at[2 * px + py, rows_s], (x, y, 1 - c))
                            for k, (px, py) in enumerate(chips)]
            for cp in sent:
                cp.start()
            copies.append((local, sent, landing, passed, from_sibling))
        for local, sent, landing, passed, from_sibling in copies:
            for k in range(3):
                landing[k].wait_recv()
                passed[k].start()
        for local, sent, landing, passed, from_sibling in copies:
            for k in range(3):
                from_sibling[k].wait_recv()
            for cp in sent + passed:
                cp.wait_send()
            local.wait()

    sems = [pltpu.SemaphoreType.DMA((6,)) for _ in range(2 * n)] + [pltpu.SemaphoreType.DMA((n,))]
    return pl.pallas_call(
        body, out_shape=[jax.ShapeDtypeStruct((4, *a.shape), a.dtype) for a in srcs],
        in_specs=[pl.BlockSpec(memory_space=pl.ANY)] * n, out_specs=[pl.BlockSpec(memory_space=pltpu.VMEM)] * n,
        scratch_shapes=sems, compiler_params=pltpu.CompilerParams(vmem_limit_bytes=VMEM_LIMIT_BYTES), name=name)(*srcs)


def _adamw(w, g, m, v):
    m = ADAM_B1 * m + (1.0 - ADAM_B1) * g
    v = ADAM_B2 * v + (1.0 - ADAM_B2) * (g * g)
    delta = -ADAM_LR * ((m / M_HAT_DIV) / (jnp.sqrt(v / V_HAT_DIV) + ADAM_EPS) + ADAM_WD * w)
    return delta, m, v


def _small_allreduce_adamw(part, w, m, v):
    def body(part_ref, w_ref, m_ref, v_ref, g_out, d_out, m_out, v_out, loss_out, buf, send_sems, recv_sems):
        x, y, c = _place()
        me = 4 * x + 2 * y + c
        buf[me] = part_ref[...]

        def flip(k):
            fx, fy, fc = (k >> 2) & 1, (k >> 1) & 1, k & 1
            px, py, pc = (1 - x if fx else x), (1 - y if fy else y), (1 - c if fc else c)
            return (px, py, pc), 4 * px + 2 * py + pc

        def copy(k, slot):
            return pltpu.make_async_remote_copy(
                src_ref=part_ref, dst_ref=buf.at[slot], send_sem=send_sems.at[k - 1], recv_sem=recv_sems.at[k - 1],
                device_id=flip(k)[0], device_id_type=MESH)

        sent = [copy(k, me) for k in range(1, 8)]
        for cp in sent:
            cp.start()
        for k in range(1, 8):
            copy(k, flip(k)[1]).wait_recv()
        for cp in sent:
            cp.wait_send()
        tot = buf[0]
        for j in range(1, 8):
            tot = tot + buf[j]
        delta, m_new, v_new = _adamw(w_ref[...], tot, m_ref[...], v_ref[...])
        g_out[...] = tot
        d_out[...] = delta
        m_out[...] = m_new
        v_out[...] = v_new
        loss_out[...] = jnp.sum(tot[LOSS_ROW0:LOSS_ROW0 + 8, :]).reshape(1, 1)

    vm = pl.BlockSpec(memory_space=pltpu.VMEM)
    shp = jax.ShapeDtypeStruct((SMALL_ROWS, 128), F32)
    return pl.pallas_call(
        body, out_shape=[shp, shp, shp, shp, jax.ShapeDtypeStruct((1, 1), F32)],
        in_specs=[vm, vm, vm, vm], out_specs=[vm, vm, vm, vm, vm],
        scratch_shapes=[pltpu.VMEM((8, SMALL_ROWS, 128), F32), pltpu.SemaphoreType.DMA((7,)), pltpu.SemaphoreType.DMA((7,))],
        name="small_allreduce_adamw")(part, w, m, v)


def _row_tile(rows, cap):
    return max(t for t in range(16, min(rows, cap) + 1, 16) if rows % t == 0)


def _reduce_pair(name, parts):
    _, rows, w = parts.shape
    tr = _row_tile(rows, 576)
    nt = rows // tr

    def body(p_ref, o_ref, mine, theirs, send_sems, recv_sems):
        i = pl.program_id(0)
        x, y, c = _place()

        def copy(t):
            rows_t = pl.ds(pl.multiple_of(t * tr, tr), tr)
            return pltpu.make_async_remote_copy(src_ref=mine.at[rows_t], dst_ref=theirs.at[rows_t], send_sem=send_sems.at[t],
                                                recv_sem=recv_sems.at[t], device_id=(x, y, 1 - c), device_id_type=MESH)

        @pl.when(i < nt)
        def _():
            mine[pl.ds(pl.multiple_of(i * tr, tr), tr), :] = (
                (p_ref[0].astype(F32) + p_ref[1].astype(F32)) + p_ref[2].astype(F32)) + p_ref[3].astype(F32)
            copy(i).start()

        @pl.when(i >= nt)
        def _():
            t = i - nt
            copy(t).wait()
            rows_t = pl.ds(pl.multiple_of(t * tr, tr), tr)
            o_ref[...] = mine[rows_t, :] + theirs[rows_t, :]

    return pl.pallas_call(
        body, out_shape=jax.ShapeDtypeStruct((rows, w), F32), grid=(2 * nt,),
        in_specs=[pl.BlockSpec((4, tr, w), lambda i: (0, jnp.minimum(i, nt - 1), 0))],
        out_specs=pl.BlockSpec((tr, w), lambda i: (jnp.maximum(i - nt, 0), 0)),
        scratch_shapes=[pltpu.VMEM((rows, w), F32), pltpu.VMEM((rows, w), F32), pltpu.SemaphoreType.DMA((nt,)),
                        pltpu.SemaphoreType.DMA((nt,))],
        compiler_params=_params(("arbitrary",)), name=name)(parts)


def _presum_halves(name, shards):
    _, rows, w = shards.shape
    half = rows // 2

    def body(s_ref, o_ref, theirs, send_sems, recv_sems):
        x, y, c = _place()
        rows_c = pl.ds(pl.multiple_of(c * half, half), half)
        rows_s = pl.ds(pl.multiple_of((1 - c) * half, half), half)
        sent = [pltpu.make_async_remote_copy(src_ref=s_ref.at[j, rows_s], dst_ref=theirs.at[j], send_sem=send_sems.at[j],
                                             recv_sem=recv_sems.at[j], device_id=(x, y, 1 - c), device_id_type=MESH) for j in range(4)]
        for cp in sent:
            cp.start()
        for j, cp in enumerate(sent):
            cp.wait_recv()
            o_ref[j] = (s_ref[j, rows_c, :].astype(F32) + theirs[j].astype(F32)).astype(o_ref.dtype)
        for cp in sent:
            cp.wait_send()

    vm = pl.BlockSpec(memory_space=pltpu.VMEM)
    return pl.pallas_call(
        body, out_shape=jax.ShapeDtypeStruct((4, half, w), shards.dtype), in_specs=[vm], out_specs=vm,
        scratch_shapes=[pltpu.VMEM((4, half, w), shards.dtype), pltpu.SemaphoreType.DMA((4,)), pltpu.SemaphoreType.DMA((4,))],
        compiler_params=pltpu.CompilerParams(vmem_limit_bytes=VMEM_LIMIT_BYTES), name=name)(shards)


def _reduce_halves(name, parts):
    _, half, w = parts.shape

    def body(p_ref, o_ref, mine, send_sem, recv_sem):
        x, y, c = _place()
        rows_c = pl.ds(pl.multiple_of(c * half, half), half)
        rows_s = pl.ds(pl.multiple_of((1 - c) * half, half), half)
        mine[...] = ((p_ref[0].astype(F32) + p_ref[1].astype(F32)) + p_ref[2].astype(F32)) + p_ref[3].astype(F32)
        send = pltpu.make_async_remote_copy(src_ref=mine, dst_ref=o_ref.at[rows_c], send_sem=send_sem, recv_sem=recv_sem,
                                            device_id=(x, y, 1 - c), device_id_type=MESH)
        send.start()
        o_ref[rows_c, :] = mine[...]
        pltpu.make_async_remote_copy(src_ref=mine, dst_ref=o_ref.at[rows_s], send_sem=send_sem, recv_sem=recv_sem,
                                     device_id=(x, y, 1 - c), device_id_type=MESH).wait_recv()
        send.wait_send()

    vm = pl.BlockSpec(memory_space=pltpu.VMEM)
    return pl.pallas_call(
        body, out_shape=jax.ShapeDtypeStruct((2 * half, w), F32), in_specs=[vm], out_specs=vm,
        scratch_shapes=[pltpu.VMEM((half, w), F32), pltpu.SemaphoreType.DMA(()), pltpu.SemaphoreType.DMA(())],
        compiler_params=pltpu.CompilerParams(vmem_limit_bytes=VMEM_LIMIT_BYTES), name=name)(parts)


def _adamw_shard(name, g, w, m, v):
    _, rows, cols = w.shape
    tr = _row_tile(rows, 256)

    def body(g_ref, w_ref, m_ref, v_ref, g_out, d_out, m_out, v_out):
        gv = g_ref[...]
        delta, m_new, v_new = _adamw(w_ref[...], gv, m_ref[...], v_ref[...])
        g_out[...] = gv
        d_out[...] = delta
        m_out[...] = m_new
        v_out[...] = v_new

    t_spec = pl.BlockSpec((None, tr, cols), lambda i: (0, i, 0))
    shp = jax.ShapeDtypeStruct((1, rows, cols), F32)
    return pl.pallas_call(body, out_shape=[shp] * 4, grid=(rows // tr,), in_specs=[pl.BlockSpec((tr, cols), lambda i: (i, 0))] + [t_spec] * 3,
                          out_specs=[t_spec] * 4, compiler_params=_params(("parallel",)), name=name)(g, w, m, v)


def _shard_shape(name):
    _, r, c, ax = BIG_BY_NAME[name]
    return (r, c // 4) if ax == 1 else (r // 4, c)


def _pad_rows(a, axis):
    pad = [(0, 0)] * a.ndim
    pad[axis] = (0, -a.shape[axis] % PACK_ALIGN)
    return jnp.pad(a, pad)


def _pack_shards(names, shards, dtype):
    return _pad_rows(jnp.concatenate([s.astype(dtype).reshape(-1, PACK_W) for s in shards], axis=0), 0)


def _unpack_shards(names, slab):
    out, off = [], 0
    for name in names:
        rs, cs = _shard_shape(name)
        n = rs * cs // PACK_W
        out.append(slab[off:off + n].reshape(rs, cs))
        off += n
    return out


def _unpack_full(names, slabs):
    out, off = [], 0
    for name in names:
        _, r, c, ax = BIG_BY_NAME[name]
        n = r * c // 4 // PACK_W
        seg = slabs[:, off:off + n]
        out.append(seg.reshape(4, r, c // 4).transpose(1, 0, 2).reshape(r, c) if ax == 1 else seg.reshape(r, c))
        off += n
    return out


def _pack_full(names, mats, dtype):
    segs = []
    for name, a in zip(names, mats):
        _, r, c, ax = BIG_BY_NAME[name]
        a = a.astype(dtype)
        a = a.reshape(r, 4, c // 4).transpose(1, 0, 2) if ax == 1 else a
        segs.append(a.reshape(4, -1, PACK_W))
    return _pad_rows(jnp.concatenate(segs, axis=1), 1)


def _pad_heads_cols(wm, heads, d):
    k = wm.shape[0]
    return jnp.pad(wm.reshape(k, heads, d), ((0, 0), (0, 0), (0, HEAD_W - d))).reshape(k, heads * HEAD_W)


def _unpad_heads_cols(wm, heads, d):
    k = wm.shape[0]
    return wm.reshape(k, heads, HEAD_W)[:, :, :d].reshape(k, heads * d)


def _win_ext(w_in):
    o = np.cumsum([0, Q_LORA, KV_LORA, QK_ROPE, H_B * HD_B, KV_B * HD_B, KV_B * HD_B, D_MODEL, D_MODEL])
    pc = lambda a, n: jnp.pad(a, ((0, 0), (0, n - a.shape[1])))
    return jnp.concatenate([
        _pad_heads_cols(w_in[:, o[3]:o[4]], H_B, HD_B), w_in[:, o[0]:o[1]], w_in[:, o[1]:o[2]], pc(w_in[:, o[2]:o[3]], HEAD_W),
        _pad_heads_cols(w_in[:, o[4]:o[5]], KV_B, HD_B), _pad_heads_cols(w_in[:, o[5]:o[6]], KV_B, HD_B),
        w_in[:, o[6]:o[7]], w_in[:, o[7]:o[8]]], axis=1)


def _win_unext(blocks):
    c = HEAD_W
    qb, mid, ga, gb = blocks
    at = lambda zc: (zc - ZC_QLAT) * c
    return jnp.concatenate([
        mid[:, at(ZC_QLAT):at(ZC_CKV)], mid[:, at(ZC_CKV):at(ZC_KPE)], mid[:, at(ZC_KPE):at(ZC_KPE) + QK_ROPE],
        _unpad_heads_cols(qb, H_B, HD_B), _unpad_heads_cols(mid[:, at(ZC_KB):at(ZC_VB)], KV_B, HD_B),
        _unpad_heads_cols(mid[:, at(ZC_VB):at(ZC_GA)], KV_B, HD_B), ga, gb], axis=1)


def _wkv_ext(w_kvb):
    wk = w_kvb.reshape(KV_LORA, H_A, QK_NOPE + V_DIM_A)
    k_cols = jnp.pad(wk[:, :, :QK_NOPE], ((0, 0), (0, 0), (0, HEAD_W - QK_NOPE))).reshape(KV_LORA, H_A * HEAD_W)
    v_cols = jnp.pad(wk[:, :, QK_NOPE:], ((0, 0), (0, 0), (0, HEAD_W - V_DIM_A))).reshape(KV_LORA, H_A * HEAD_W)
    eye = jnp.pad(jnp.eye(QK_ROPE, dtype=w_kvb.dtype), ((0, 0), (QK_NOPE, HEAD_W - QK_NOPE - QK_ROPE)))
    pe_rows = jnp.concatenate([jnp.tile(eye, (1, H_A)), jnp.zeros((QK_ROPE, H_A * HEAD_W), w_kvb.dtype)], axis=1)
    top = jnp.concatenate([k_cols, v_cols], axis=1)
    return jnp.concatenate([top, pe_rows, jnp.zeros((2 * HEAD_W - KV_LORA - QK_ROPE, 2 * H_A * HEAD_W), w_kvb.dtype)], axis=0)


def _wkv_unext(k_block, v_block):
    k_cols = k_block[:KV_LORA].reshape(KV_LORA, H_A, HEAD_W)[:, :, :QK_NOPE]
    v_cols = v_block[:KV_LORA].reshape(KV_LORA, H_A, HEAD_W)[:, :, :V_DIM_A]
    return jnp.concatenate([k_cols, v_cols], axis=2).reshape(KV_LORA, H_A * (QK_NOPE + V_DIM_A))


def _pad_heads_rows(wm, heads, d):
    n = wm.shape[1]
    return jnp.pad(wm.reshape(heads, d, n), ((0, 0), (0, HEAD_W - d), (0, 0))).reshape(heads * HEAD_W, n)


def _unpad_heads_rows(wm, heads, d):
    n = wm.shape[1]
    return wm.reshape(heads, HEAD_W, n)[:, :d].reshape(heads * d, n)


def _rope_tables(seq):
    def ang(pos, dim):
        inv = np.float32(ROPE_THETA) ** (-np.arange(0, dim, 2, dtype=np.float32) / np.float32(dim))
        return pos.astype(np.float32)[:, None] * inv[None, :]

    def rot(dim):
        r = np.zeros((dim, dim), np.float32)
        half = dim // 2
        r[np.arange(half) + half, np.arange(half)] = -1.0
        r[np.arange(half), np.arange(half) + half] = 1.0
        return r

    def table(blocks):
        cos, sin = np.ones((seq, HEAD_W), np.float32), np.zeros((seq, HEAD_W), np.float32)
        pm = np.zeros((HEAD_W, HEAD_W), np.float32)
        for c0, a in blocks:
            d = 2 * a.shape[1]
            cos[:, c0:c0 + d] = np.concatenate([np.cos(a), np.cos(a)], axis=1)
            sin[:, c0:c0 + d] = np.concatenate([np.sin(a), np.sin(a)], axis=1)
            pm[c0:c0 + d, c0:c0 + d] = rot(d)
        return jnp.asarray(cos), jnp.asarray(sin), jnp.asarray(pm, BF16), jnp.asarray(pm.T, BF16)

    tok = np.arange(seq)
    a1 = ang(tok, QK_ROPE)
    arow, acol = ang(tok // GRID_W, HD_B // 2), ang(tok % GRID_W, HD_B // 2)
    return table([(QK_NOPE, a1)]), table([(0, a1)]), table([(0, arow), (HD_B // 2, acol)])


def _local_step(x, p, tgt, gains, wts, ride=None):
    nb, seq, _ = x.shape
    t = nb * seq
    x0 = x.reshape(t, D_MODEL)
    p2 = p.reshape(t, PLE_DIM)
    tg = tgt.reshape(t, D_MODEL)
    (cq_t, sq_t, pq, pq_t), (ck_t, sk_t, pk, pk_t), (cb_t, sb_t, pb, pb_t) = _rope_tables(seq)
    padg = lambda g: jnp.pad(g, ((0, 0), (0, HEAD_W - g.shape[1])))
    g_qn, g_kn = padg(gains["g_qn"]), padg(gains["g_kn"])

    win = _win_ext(wts["w_in"])
    wqb = _pad_heads_cols(wts["w_qb"], H_A, QK_NOPE + QK_ROPE)
    wkv = _wkv_ext(wts["w_kvb"])

    norm = lambda n: (lambda v, g: (_rms(v, g, n),))
    full = lambda a: (a, a.shape[1], 0, False)
    wts = dict(wts)
    rider_of = lambda kernel_name: None if ride is None else ride["gather"][kernel_name][0]

    def landed(kernel_name, got):
        if ride is not None:
            wts.update(ride["gather"][kernel_name][1](got))

    h = _rowwise("norm_mix", norm(D_MODEL), [full(x0)], [(D_MODEL, BF16, D_MODEL, False)], consts=[gains["g_mix"]])
    res = _mm("in_proj", h, win, out_dtypes=(BF16,), tn=2048, rider=rider_of("in_proj"))
    z, got = (res, []) if ride is None else (res[0], res[1:])
    landed("in_proj", got)

    def rope_fwd(scale):
        return lambda v, cos, sin, pm: ((v * cos + _perm(v, pm) * sin) * scale,)

    heads_tile = lambda n: (n * HEAD_W, BF16, n * HEAD_W, False)
    z_qlat, z_ckv, z_kpe = (z, Q_LORA, ZC_QLAT // 2, False), (z, HEAD_W, ZC_CKV, False), (z, HEAD_W, ZC_KPE, False)

    def q_path(zq, cos, sin, g, w, pm):
        cqv = _rms(zq, g, Q_LORA).astype(BF16)
        qa = jnp.dot(cqv, w[...], preferred_element_type=F32)
        return (cqv, *_per_head(rope_fwd(SCALE_A), H_A, 1, 1)(qa, cos, sin, pm))

    cq, q_a = _rowwise("q_path", q_path, [z_qlat], [(Q_LORA, BF16, Q_LORA, False), heads_tile(H_A)],
                       pos=[cq_t, sq_t], consts=[gains["g_qa"], wqb, pq], seq=seq)

    def kv_path(ckv_raw, kpe_raw, cos, sin, g, w, pm):
        kinv = jnp.concatenate([_rms(ckv_raw, g, KV_LORA), *rope_fwd(1.0)(kpe_raw, cos, sin, pm)], axis=1).astype(BF16)
        return kinv, jnp.dot(kinv, w[...], preferred_element_type=F32)

    kin, kv_a = _rowwise("kv_path", kv_path, [z_ckv, z_kpe], [heads_tile(2), heads_tile(2 * H_A)],
                         pos=[ck_t, sk_t], consts=[gains["g_kva"], wkv, pk], seq=seq)
    o_a, lse_a, *got = _attn_fwd("attn_a_fwd", q_a, kv_a, 0, kv_a, H_A, heads=H_A, group=1, nseq=nb, seq=seq,
                                 rider=rider_of("attn_a_fwd"))
    landed("attn_a_fwd", got)

    def prep_fwd(scale):
        def fn(v, cos, sin, g, pm):
            yv = _rms(v, g, HD_B)
            return ((yv * cos + _perm(yv, pm) * sin) * scale,)
        return fn

    z_qb, z_kb = (z, H_B * HEAD_W, ZC_QB // H_B, False), (z, KV_B * HEAD_W, ZC_KB // KV_B, False)
    def prep_b(zq, zk, cos, sin, gq, gk, pm):
        return (*_per_head(prep_fwd(SCALE_B), H_B, 1, 1)(zq, cos, sin, gq, pm), *_per_head(prep_fwd(1.0), KV_B, 1, 1)(zk, cos, sin, gk, pm))

    q_b, k_b, *got = _rowwise("prep_b", prep_b, [z_qb, z_kb], [heads_tile(H_B), heads_tile(KV_B)],
                              pos=[cb_t, sb_t], consts=[g_qn, g_kn, pb], seq=seq, rider=rider_of("prep_b"))
    landed("prep_b", got)
    o_b, lse_b, *got = _attn_fwd("attn_b_fwd", q_b, k_b, 0, z, ZC_VB, heads=H_B, group=H_B // KV_B, nseq=nb, seq=seq,
                                 rider=rider_of("attn_b_fwd"))
    landed("attn_b_fwd", got)
    woa = _pad_heads_rows(wts["w_oa"], H_A, V_DIM_A)
    wob = _pad_heads_rows(wts["w_ob"], H_B, HD_B)
    wo, wup, wdown = wts["w_o"], wts["w_up"], wts["w_down"]

    def residual_norm(acc, r, g):
        xv = r + acc
        return xv, _rms(xv, g, D_MODEL)

    def mix_out(oa, ob, ga, gb, r, g, w_a, w_b, w_out):
        a = jnp.dot(oa, w_a[...], preferred_element_type=F32)
        b = jnp.dot(ob, w_b[...], preferred_element_type=F32)
        mg = (_sigmoid(ga) * a + _sigmoid(gb) * b).astype(BF16)
        return (a, b, mg, *residual_norm(jnp.dot(mg, w_out[...], preferred_element_type=F32), r, g))

    z_ga, z_gb = (z, D_MODEL, ZC_GA // 8, False), (z, D_MODEL, ZC_GB // 8, False)
    wide = lambda d: (D_MODEL, d, D_MODEL, False)
    ya, yb, merged, x1, h2, *got = _rowwise("mix_out", mix_out, [full(o_a), full(o_b), z_ga, z_gb, full(x0)],
                                            [wide(BF16), wide(BF16), wide(BF16), wide(F32), wide(BF16)],
                                            consts=[gains["g_mlp"], woa, wob, wo], tm=256, rider=rider_of("mix_out"))
    landed("mix_out", got)
    wpg, wple = wts["w_ple_gate"], wts["w_ple"]

    square = lambda v: v * v
    u = _mm("mlp_up", h2, wup, b_slots=True, out_dtypes=(BF16,), epi=lambda acc: (jnp.maximum(acc, 0.0),), tm=1024)
    x2, h3 = _mm("mlp_down", u, wdown, a_pre=square, out_dtypes=(F32, BF16), epi=residual_norm, extras=(x1,), consts=[gains["g_ple"]])

    def norm_res_bwd(dh, v, res, g):
        dx, dg = _rms_bwd(dh, v, g, D_MODEL)
        return dx + res, dg

    def tail(x2v, h3v, pv, tv, gf, gp, w_gate, w_emb):
        sg = _sigmoid(jnp.dot(h3v, w_gate[...], preferred_element_type=F32))
        pev = jnp.dot(pv.astype(BF16), w_emb[...], preferred_element_type=F32)
        x3 = x2v + sg * pev
        rs = lax.rsqrt(jnp.sum(x3 * x3, axis=-1, keepdims=True) * (1.0 / D_MODEL) + EPS)
        xh = x3 * rs
        err = xh * gf - tv
        dy = err * (1.0 / D_MODEL)
        dyg = dy * gf
        dx3 = rs * (dyg - xh * (jnp.sum(dyg * xh, axis=-1, keepdims=True) * (1.0 / D_MODEL)))
        dgp = (dx3 * pev * sg * (1.0 - sg)).astype(BF16)
        dh3 = lax.dot_general(dgp, w_gate[...], (((1,), (1,)), ((), ())), preferred_element_type=F32)
        dx2v, dgple = norm_res_bwd(dh3, x2v, dx3, gp)
        return (dx2v, dgp, dx3 * sg, jnp.sum(err * err, axis=0, keepdims=True) * (0.5 / D_MODEL),
                jnp.sum(dy * xh, axis=0, keepdims=True), dgple)

    dx2, dgpre, dpe, loss_part, dg_final, dg_ple = _rowwise(
        "tail", tail, [full(x2), full(h3), full(p2), full(tg)], [wide(F32), wide(BF16), wide(BF16)],
        consts=[gains["g_final"].reshape(1, D_MODEL), gains["g_ple"], wpg, wple], accs=[(1, D_MODEL)] * 3, tm=256)

    dw = {}
    dw["w_ple"] = _mm_tn("dw_ple", p2, dpe)
    dw["w_ple_gate"] = _mm_tn("dw_ple_gate", h3, dgpre)
    dw["w_down"] = _mm_tn("dw_down", u, dx2, a_pre=square)
    dupre = _mm("d_mlp_down", dx2, wdown, trans_b=True, out_dtypes=(BF16,), epi=lambda acc, uv: (acc * (2.0 * uv.astype(F32)),),
                extras=(u,), tn=2048)
    dw["w_up"] = _mm_tn("dw_up", h2, dupre, out_slots=True)
    n_up = wup.shape[0]
    dx1, dg_mlp = _mm("d_mlp_up", [(dupre, j, wup.shape[2]) for j in range(n_up)], [(wup, j) for j in range(n_up)], trans_b=True,
                      epi=norm_res_bwd, extras=(x1, dx2), consts=[gains["g_mlp"]],
                      accs=[(1, D_MODEL)], tm=512)
    dw["w_o"] = _mm_tn("dw_o", merged, dx1)

    def merge_bwd(dm, ga, gb, a, b, w_a, w_b):
        sa, sb = _sigmoid(ga), _sigmoid(gb)
        da, db = (dm * sa).astype(BF16), (dm * sb).astype(BF16)
        nt = (((1,), (1,)), ((), ()))
        return (da, db, dm * a * sa * (1.0 - sa), dm * b * sb * (1.0 - sb),
                lax.dot_general(da, w_a, nt, preferred_element_type=F32), lax.dot_general(db, w_b, nt, preferred_element_type=F32))

    dya, dyb, dga, dgb, do_a, do_b = _mm("d_out_proj", dx1, wo, trans_b=True, out_dtypes=(BF16,) * 6, epi=merge_bwd,
                                         extras=((z, ZC_GA // 8), (z, ZC_GB // 8), ya, yb), consts=[woa, wob], tm=256)
    dw["w_oa"] = _unpad_heads_rows(_mm_tn("dw_oa", o_a, dya), H_A, V_DIM_A)
    dw["w_ob"] = _unpad_heads_rows(_mm_tn("dw_ob", o_b, dyb), H_B, HD_B)
    res_a = _attn_bwd("attn_a_bwd", q_a, kv_a, 0, kv_a, H_A, o_a, do_a, lse_a, heads=H_A, group=1, nseq=nb, seq=seq,
                      rider=ride and ride["scatter_a"](dw))
    dq_a, dk_a, dv_a = res_a[:3]
    if ride is not None:
        ride["out"]["parts_a"] = res_a[3:]

    def rope_bwd(scale):
        return lambda d, cos, sin, pm_t: ((d * cos + _perm(d * sin, pm_t)) * scale,)

    nt_dims = (((1,), (1,)), ((), ()))

    def q_path_bwd(dq, zq, cos, sin, g, w, pm_t):
        dqav = _per_head(rope_bwd(SCALE_A), H_A, 1, 1)(dq, cos, sin, pm_t)[0].astype(BF16)
        dcq = lax.dot_general(dqav, w[...], nt_dims, preferred_element_type=F32)
        return (dqav, *_rms_bwd(dcq, zq, g, Q_LORA))

    dqa, dq_lat, dg_qa = _rowwise("q_path_bwd", q_path_bwd, [full(dq_a), z_qlat], [heads_tile(H_A), (Q_LORA, BF16, Q_LORA, False)],
                                  pos=[cq_t, sq_t], consts=[gains["g_qa"], wqb, pq_t], accs=[(1, Q_LORA)], seq=seq)
    dw["w_qb"] = _unpad_heads_cols(_mm_tn("dw_qb", cq, dqa), H_A, QK_NOPE + QK_ROPE)
    dw_kv_blocks = _mm_tn_shared("dw_kv", kin, [dk_a, dv_a])
    dw["w_kvb"] = _wkv_unext(dw_kv_blocks[0], dw_kv_blocks[1])
    dq_b, dk_b, dv_b, *parts_b = _attn_bwd("attn_b_bwd", q_b, k_b, 0, z, ZC_VB, o_b, do_b, lse_b, heads=H_B, group=H_B // KV_B,
                                               nseq=nb, seq=seq, rider=ride and ride["scatter_b"](dw))
    if ride is not None:
        ride["out"]["parts_b"] = parts_b

    def kv_path_bwd(dk, dv, ckv_raw, cos, sin, g, w, pm_t):
        kv_w = H_A * HEAD_W
        wv = w[...]
        dkin = (lax.dot_general(dk, wv[:, :kv_w], nt_dims, preferred_element_type=F32)
                + lax.dot_general(dv, wv[:, kv_w:], nt_dims, preferred_element_type=F32))
        dckv_raw, dg = _rms_bwd(dkin[:, :HEAD_W], ckv_raw, g, KV_LORA)
        return (dckv_raw, *rope_bwd(1.0)(dkin[:, HEAD_W:], cos, sin, pm_t), dg)

    dckv, dkpe, dg_kva = _rowwise("kv_path_bwd", kv_path_bwd, [full(dk_a), full(dv_a), z_ckv], [heads_tile(1), heads_tile(1)],
                                  pos=[ck_t, sk_t], consts=[gains["g_kva"], wkv, pk_t], accs=[(1, KV_LORA)], seq=seq)

    def prep_bwd(scale):
        def fn(d, v, cos, sin, g, pm_t):
            dyv = (d * cos + _perm(d * sin, pm_t)) * scale
            return _rms_bwd(dyv, v, g, HD_B)
        return fn

    def prep_b_bwd(dq, dk, zq, zk, cos, sin, gq, gk, pm_t):
        dq_raw, dgq = _per_head(prep_bwd(SCALE_B), H_B, 2, 1)(dq, zq, cos, sin, gq, pm_t)
        dk_raw, dgk = _per_head(prep_bwd(1.0), KV_B, 2, 1)(dk, zk, cos, sin, gk, pm_t)
        return dq_raw, dk_raw, dgq, dgk

    dqb, dkb, dg_qn, dg_kn = _rowwise("prep_b_bwd", prep_b_bwd, [full(dq_b), full(dk_b), z_qb, z_kb], [heads_tile(H_B), heads_tile(KV_B)],
                                      pos=[cb_t, sb_t], consts=[g_qn, g_kn, pb_t], accs=[(1, HEAD_W)] * 2, seq=seq)

    dz = [dqb, jnp.concatenate([dq_lat, dckv, dkpe, dkb, dv_b], axis=1), dga, dgb]
    dw_in_blocks = _mm_tn_shared("dw_in", h, dz)
    dw["w_in"] = _win_unext([dw_in_blocks[j] for j in range(4)])
    dx0, dg_mix, *parts_in = _mm("d_in_proj", dz, [(win, j, D_MODEL) for j in range(4)], trans_b=True, epi=norm_res_bwd, extras=(x0, dx1), consts=[gains["g_mix"]],
                                 accs=[(1, D_MODEL)], tm=256, rider=ride and ride["scatter_in"](dw))
    if ride is not None:
        ride["out"]["parts_in"] = parts_in

    dg = {"g_mix": dg_mix, "g_qa": dg_qa, "g_kva": dg_kva, "g_qn": dg_qn[:, :HD_B], "g_kn": dg_kn[:, :HD_B],
          "g_mlp": dg_mlp, "g_ple": dg_ple, "g_final": dg_final}
    return loss_part, dx0.reshape(nb, seq, D_MODEL), dg, dw


def _pack_small(vals, loss_part=None):
    flat = jnp.concatenate([vals[n].reshape(1, -1) for n, _ in SMALL], axis=1)
    loss = jnp.zeros((1, 8 * 128), F32) if loss_part is None else loss_part
    gap = jnp.zeros((1, LOSS_ROW0 * 128 - SMALL_N), F32)
    return jnp.concatenate([flat, gap, loss], axis=1).reshape(SMALL_ROWS, 128)


def _unpack_small(slab, like):
    flat, out, off = slab.reshape(-1), {}, 0
    for n, k in SMALL:
        out[n] = flat[off:off + k].reshape(like[n].shape)
        off += k
    return out


def kernel(x, p, g_mix, w_in, g_qa, w_qb, g_kva, w_kvb, g_qn, g_kn, w_oa, w_ob, w_o, g_mlp, w_up, w_down, g_ple, w_ple_gate, w_ple, g_final, loss_target, m_g_mix, m_w_in, m_g_qa, m_w_qb, m_g_kva, m_w_kvb, m_g_qn, m_g_kn, m_w_oa, m_w_ob, m_w_o, m_g_mlp, m_w_up, m_w_down, m_g_ple, m_w_ple_gate, m_w_ple, m_g_final, v_g_mix, v_w_in, v_g_qa, v_w_qb, v_g_kva, v_w_kvb, v_g_qn, v_g_kn, v_w_oa, v_w_ob, v_w_o, v_g_mlp, v_w_up, v_w_down, v_g_ple, v_w_ple_gate, v_w_ple, v_g_final):
    given = dict(locals())
    order = ["g_mix", "w_in", "g_qa", "w_qb", "g_kva", "w_kvb", "g_qn", "g_kn", "w_oa", "w_ob", "w_o", "g_mlp", "w_up",
             "w_down", "g_ple", "w_ple_gate", "w_ple", "g_final"]
    big_names = [n for n, _, _, _ in BIG]
    local = lambda prefix, names: [given[prefix + n][0] for n in names]
    slab = lambda names: _pack_shards(names, local("", names), BF16)
    bf = lambda n: given[n][0].astype(BF16)
    cols_full = lambda g: g.transpose(1, 0, 2).reshape(g.shape[1], -1)
    rows_full = lambda g: g.reshape(-1, g.shape[2])
    shards_cols = lambda a: a.reshape(a.shape[0], 4, a.shape[1] // 4).transpose(1, 0, 2)
    shards_rows = lambda a: a.reshape(4, a.shape[0] // 4, a.shape[1])
    packed = lambda names, dw: _pack_full(names, [dw[n] for n in names], BF16)
    branch_out = ["w_oa", "w_ob"]
    back_a, back_b = SLAB_LATE + ["w_o"], SLAB_EARLY + ["w_ple_gate"]

    got_in, got_early = _gather_by_halves("weight_gather_early", [bf("w_in"), slab(SLAB_EARLY)])
    wts = {"w_in": cols_full(got_in), **dict(zip(SLAB_EARLY, _unpack_full(SLAB_EARLY, got_early)))}
    gains = {n: given[n].reshape(1, -1) for n, _ in SMALL}
    ride = {
        "gather": {
            "in_proj": (_Exchange("gather", [bf("w_o")]), lambda got: {"w_o": rows_full(got[0])}),
            "prep_b": (_Exchange("gather", [slab(branch_out)]), lambda got: dict(zip(branch_out, _unpack_full(branch_out, got[0])))),
            "attn_a_fwd": (_Exchange("gather", [bf("w_up")]), lambda got: {"w_up": got[0]}),
            "attn_b_fwd": (_Exchange("gather", [bf("w_down")]), lambda got: {"w_down": rows_full(got[0])}),
            "mix_out": (_Exchange("gather", [bf("w_ple_gate"), bf("w_ple")]),
                        lambda got: {"w_ple_gate": rows_full(got[0]), "w_ple": cols_full(got[1])}),
        },
        "scatter_a": lambda dw: _Exchange("scatter", [dw["w_up"], packed(back_a, dw)]),
        "scatter_b": lambda dw: _Exchange("scatter", [shards_rows(dw["w_down"]), packed(back_b, dw)]),
        "scatter_in": lambda dw: _Exchange("scatter", [_presum_halves("grad_presum_in", shards_cols(dw["w_in"]))]),
        "out": {},
    }
    loss_part, grad_x, dg, dw = _local_step(x, p[0], loss_target, gains, wts, ride)

    small = lambda prefix: _pack_small({n: given[prefix + n] for n, _ in SMALL})
    g_s, d_s, m_s, v_s, loss = _small_allreduce_adamw(_pack_small(dg, loss_part), small(""), small("m_"), small("v_"))

    parts = ride["out"]
    grads = {"w_up": _reduce_pair("grad_reduce_up", parts["parts_a"][0]), "w_down": _reduce_pair("grad_reduce_down", parts["parts_b"][0]),
             "w_in": _reduce_halves("grad_reduce_in", parts["parts_in"][0])}
    grads.update(zip(back_a, _unpack_shards(back_a, _reduce_pair("grad_reduce_slab_a", parts["parts_a"][1]))))
    grads.update(zip(back_b, _unpack_shards(back_b, _reduce_pair("grad_reduce_slab_b", parts["parts_b"][1]))))

    res = {}
    for key, slab in (("grad_", g_s), ("delta_", d_s), ("new_m_", m_s), ("new_v_", v_s)):
        for n, val in _unpack_small(slab, given).items():
            res[key + n] = val
    for n in big_names:
        res["grad_" + n], res["delta_" + n], res["new_m_" + n], res["new_v_" + n] = _adamw_shard(
            "adamw_" + n, grads[n], given[n], given["m_" + n], given["v_" + n])
    outs = [loss.reshape(()), grad_x]
    for key in ("grad_", "delta_", "new_m_", "new_v_"):
        outs += [res[key + n] for n in order]
    return tuple(outs)
```

```python
import functools

import numpy as np
import jax
import jax.numpy as jnp
from jax import lax
from jax.experimental import pallas as pl
from jax.experimental.pallas import tpu as pltpu

F32 = jnp.float32
BF16 = jnp.bfloat16
MESH = pl.DeviceIdType.MESH

D_MODEL = 1024
GRID_W = 64
ROPE_THETA = 10000.0
EPS = 1e-6
H_A, QK_NOPE, QK_ROPE, V_DIM_A, Q_LORA, KV_LORA = 8, 64, 32, 64, 256, 128
H_B, KV_B, HD_B = 8, 2, 64
D_FF = 4096
PLE_DIM = 256
HEAD_W = 128
SCALE_A = (QK_NOPE + QK_ROPE) ** -0.5
SCALE_B = HD_B ** -0.5

ADAM_LR, ADAM_B1, ADAM_B2, ADAM_EPS, ADAM_WD, ADAM_STEP = 0.001, 0.9, 0.999, 1e-08, 0.01, 10
M_HAT_DIV = 1.0 - ADAM_B1 ** ADAM_STEP
V_HAT_DIV = 1.0 - ADAM_B2 ** ADAM_STEP

VMEM_LIMIT_BYTES = 56 * 1024 * 1024

ZC_QB, ZC_QLAT, ZC_CKV, ZC_KPE, ZC_KB, ZC_VB, ZC_GA, ZC_GB = 0, 8, 10, 11, 12, 14, 16, 24
Z_WIDTH = 32 * HEAD_W

BIG = [
    ("w_in", 1024, 3232, 1), ("w_qb", 256, 768, 1), ("w_kvb", 128, 1024, 1), ("w_oa", 512, 1024, 1),
    ("w_ob", 512, 1024, 1), ("w_o", 1024, 1024, 0), ("w_up", 1024, 4096, 1), ("w_down", 4096, 1024, 0),
    ("w_ple_gate", 1024, 1024, 0), ("w_ple", 256, 1024, 1),
]
BIG_BY_NAME = {e[0]: e for e in BIG}
PACK_W = 1024
PACK_ALIGN = 64
SLAB_EARLY = ["w_qb", "w_kvb"]
SLAB_LATE = ["w_oa", "w_ob", "w_ple"]

SMALL = [("g_mix", 1024), ("g_qa", 256), ("g_kva", 128), ("g_qn", 64), ("g_kn", 64), ("g_mlp", 1024),
         ("g_ple", 1024), ("g_final", 1024)]
SMALL_N = sum(n for _, n in SMALL)
LOSS_ROW0 = 40
SMALL_ROWS = 48


def _params(sem):
    return pltpu.CompilerParams(dimension_semantics=sem, vmem_limit_bytes=VMEM_LIMIT_BYTES)


def _sigmoid(v):
    return 1.0 / (1.0 + jnp.exp(-v.astype(F32)))


def _perm(v, p_ref):
    pm = p_ref[...]
    hi = v.astype(BF16)
    lo = (v - hi.astype(F32)).astype(BF16)
    return (jnp.dot(hi, pm, preferred_element_type=F32) + jnp.dot(lo, pm, preferred_element_type=F32))


def _rms(v, g, n):
    v = v.astype(F32)
    rs = lax.rsqrt(jnp.sum(v * v, axis=-1, keepdims=True) * (1.0 / n) + EPS)
    return v * rs * g


def _rms_bwd(dy, v, g, n):
    v = v.astype(F32)
    rs = lax.rsqrt(jnp.sum(v * v, axis=-1, keepdims=True) * (1.0 / n) + EPS)
    vh = v * rs
    dyg = dy * g
    dx = rs * (dyg - vh * (jnp.sum(dyg * vh, axis=-1, keepdims=True) * (1.0 / n)))
    return dx, jnp.sum(dy * vh, axis=0, keepdims=True)


def _ride(body, grid, rider):
    if rider is None:
        return body, [], [], [], []
    n_x, n_sem = len(rider.srcs), len(rider.scratch)

    def wrapped(*refs):
        ids = [pl.program_id(a) for a in range(len(grid))]
        n_in = len(refs) - n_sem - 2 * n_x - rider.n_core_out - rider.n_core_scratch
        core_in, srcs = refs[:n_in], refs[n_in:n_in + n_x]
        core_out = refs[n_in + n_x:n_in + n_x + rider.n_core_out]
        dsts = refs[n_in + n_x + rider.n_core_out:n_in + 2 * n_x + rider.n_core_out]
        core_scr = refs[n_in + 2 * n_x + rider.n_core_out:len(refs) - n_sem]
        sems = refs[len(refs) - n_sem:]

        @pl.when(functools.reduce(jnp.logical_and, [a == 0 for a in ids]))
        def _():
            rider.start(srcs, dsts, *sems)

        body(*core_in, *core_out, *core_scr)

        @pl.when(functools.reduce(jnp.logical_and, [a == n - 1 for a, n in zip(ids, grid)]))
        def _():
            rider.finish(srcs, dsts, *sems)

    hbm = pl.BlockSpec(memory_space=pl.ANY)
    return wrapped, list(rider.srcs), [hbm] * n_x, list(rider.out_shapes), list(rider.scratch)


def _mm(name, a, b, *, trans_b=False, b_slots=False, a_pre=None, out_dtypes=(F32,), epi=None, extras=(), consts=(), accs=(), tm=512,
        tn=None, rider=None):
    a_ops = [o if isinstance(o, tuple) else (o, 0, o.shape[1]) for o in (a if isinstance(a, list) else [a])]
    b_ops = b if isinstance(b, list) else [b]
    assert len(a_ops) == len(b_ops) and not (b_slots and (trans_b or len(b_ops) > 1))
    m = a_ops[0][0].shape[0]
    if b_slots:
        n, tn = b.shape[0] * b.shape[2], b.shape[2]
    else:
        first = b_ops[0][0] if isinstance(b_ops[0], tuple) else b_ops[0]
        n = first.shape[-2] if trans_b else first.shape[1]
        tn = n if tn is None else min(tn, n)
    tm = min(tm, m)
    assert m % tm == 0 and n % tn == 0
    extras = [e if isinstance(e, tuple) else (e, 0) for e in extras]
    n_p, n_ex, n_c, n_out, n_acc = len(a_ops), len(extras), len(consts), len(out_dtypes), len(accs)
    dims = (((1,), (1,)), ((), ())) if trans_b else (((1,), (0,)), ((), ()))

    def body(*refs):
        acc = None
        for a_ref, b_ref in zip(refs[:n_p], refs[n_p:2 * n_p]):
            av = a_ref[...] if a_pre is None else a_pre(a_ref[...].astype(F32))
            part = lax.dot_general(av.astype(BF16), b_ref[...].astype(BF16), dims, preferred_element_type=F32)
            acc = part if acc is None else acc + part
        rest = refs[2 * n_p:]
        res = (acc,) if epi is None else epi(acc, *[e[...] for e in rest[:n_ex + n_c]])
        o_refs = rest[n_ex + n_c:]
        for o_ref, r in zip(o_refs[:n_out], res[:n_out]):
            o_ref[...] = r.astype(o_ref.dtype)
        if n_acc:
            first_step = jnp.logical_and(pl.program_id(0) == 0, pl.program_id(1) == 0)

            @pl.when(first_step)
            def _():
                for o_ref, r in zip(o_refs[n_out:], res[n_out:]):
                    o_ref[...] = r

            @pl.when(jnp.logical_not(first_step))
            def _():
                for o_ref, r in zip(o_refs[n_out:], res[n_out:]):
                    o_ref[...] += r

    def b_spec(op, k_i):
        if b_slots:
            return pl.BlockSpec((None, k_i, tn), lambda j, i: (j, 0, 0))
        if not isinstance(op, tuple):
            return pl.BlockSpec((tn, k_i), lambda j, i: (j, 0)) if trans_b else pl.BlockSpec((k_i, tn), lambda j, i: (0, j))
        assert trans_b
        if len(op) == 2:
            return pl.BlockSpec((None, tn, k_i), lambda j, i, slot=op[1]: (slot, j, 0))
        return pl.BlockSpec((tn, k_i), lambda j, i, blk=op[1]: (j, blk))

    grid = (n // tn, m // tm)
    if rider is not None:
        rider.n_core_out, rider.n_core_scratch = n_out + n_acc, 0
    body, x_in, x_spec, x_out, x_scr = _ride(body, grid, rider)
    a_specs = [pl.BlockSpec((tm, k_i), lambda j, i, blk=blk: (i, blk)) for _, blk, k_i in a_ops]
    b_specs = [b_spec(op, k_i) for op, (_, _, k_i) in zip(b_ops, a_ops)]
    t_spec = pl.BlockSpec((tm, tn), lambda j, i: (i, j))
    e_specs = [pl.BlockSpec((tm, tn), lambda j, i, off=off: (i, j + off)) for _, off in extras]
    c_specs = [pl.BlockSpec(c.shape, lambda j, i: (0, 0)) for c in consts]
    acc_specs = [pl.BlockSpec(sh, lambda j, i: (0, 0)) for sh in accs]
    sem = ("parallel", "parallel") if rider is None and not n_acc else ("arbitrary", "arbitrary")
    outs = pl.pallas_call(
        body, out_shape=[jax.ShapeDtypeStruct((m, n), d) for d in out_dtypes] + [jax.ShapeDtypeStruct(sh, F32) for sh in accs] + x_out,
        grid=grid, in_specs=a_specs + b_specs + e_specs + c_specs + x_spec, out_specs=[t_spec] * n_out + acc_specs + x_spec,
        scratch_shapes=x_scr, compiler_params=_params(sem),
        name=name)(*[o[0] for o in a_ops], *[o[0] if isinstance(o, tuple) else o for o in b_ops], *[e for e, _ in extras], *consts, *x_in)
    return outs[0] if len(outs) == 1 else outs


def _per_head(fn, heads, n_tiled, n_out):
    def run(*args):
        res = [fn(*[a[:, hd * HEAD_W:(hd + 1) * HEAD_W] for a in args[:n_tiled]], *args[n_tiled:]) for hd in range(heads)]
        tiles = [jnp.concatenate([r[k] for r in res], axis=1) for k in range(n_out)]
        sums = [functools.reduce(lambda u, v: u + v, [r[k] for r in res]) for k in range(n_out, len(res[0]))]
        return (*tiles, *sums)
    return run


def _mm_tn(name, a, b, *, a_pre=None, out_dtype=BF16, out_slots=False, tk=1024, tn=1024, tt=4096):
    t, k = a.shape
    n = b.shape[1]
    tk, tn = min(tk, k), min(tn, n)
    if a.dtype == F32 or b.dtype == F32:
        tt = tt // 2
    if k == tk and n == tn:
        tt = tt // 2
    tt = min(tt, t)
    assert b.shape[0] == t and k % tk == 0 and n % tn == 0 and t % tt == 0
    nt = t // tt

    def body(a_ref, b_ref, o_ref, acc):
        av = a_ref[...] if a_pre is None else a_pre(a_ref[...].astype(F32))
        part = lax.dot_general(av.astype(BF16), b_ref[...].astype(BF16), (((0,), (0,)), ((), ())), preferred_element_type=F32)

        @pl.when(pl.program_id(2) == 0)
        def _():
            acc[...] = part

        @pl.when(pl.program_id(2) != 0)
        def _():
            acc[...] += part

        @pl.when(pl.program_id(2) == nt - 1)
        def _():
            o_ref[...] = acc[...].astype(o_ref.dtype)

    if out_slots:
        out_shape, out_spec = (n // tn, k, tn), pl.BlockSpec((None, tk, tn), lambda ki, ni, ti: (ni, ki, 0))
    else:
        out_shape, out_spec = (k, n), pl.BlockSpec((tk, tn), lambda ki, ni, ti: (ki, ni))
    return pl.pallas_call(
        body, out_shape=jax.ShapeDtypeStruct(out_shape, out_dtype), grid=(k // tk, n // tn, nt),
        in_specs=[pl.BlockSpec((tt, tk), lambda ki, ni, ti: (ti, ki)), pl.BlockSpec((tt, tn), lambda ki, ni, ti: (ti, ni))],
        out_specs=out_spec, scratch_shapes=[pltpu.VMEM((tk, tn), F32)],
        compiler_params=_params(("parallel", "parallel", "arbitrary")), name=name)(a, b)


def _mm_tn_stream(name, a, bs):
    a_list = a if isinstance(a, list) else None
    t, k = (a_list[0] if a_list else a).shape
    n, nb = bs[0].shape[1], len(bs)
    n_a = nb if a_list else 1

    def body(*refs):
        a_refs, b_refs, o_ref = refs[:n_a], refs[n_a:n_a + nb], refs[n_a + nb]
        scratch = refs[n_a + nb + 1:]
        b_buf, b_sems = scratch[0], scratch[1]
        j = pl.program_id(0)

        def fetches(jj):
            copies = [pltpu.make_async_copy(b_refs[jj], b_buf.at[jj % 2], b_sems.at[jj % 2])]
            if a_list:
                copies.append(pltpu.make_async_copy(a_refs[jj], scratch[2].at[jj % 2], scratch[3].at[jj % 2]))
            return copies

        @pl.when(j == 0)
        def _():
            for cp in fetches(0):
                cp.start()

        for jj in range(nb):
            @pl.when(j == jj)
            def _(jj=jj):
                if jj + 1 < nb:
                    for cp in fetches(jj + 1):
                        cp.start()
                for cp in fetches(jj):
                    cp.wait()

        av = scratch[2][j % 2] if a_list else a_refs[0][...].astype(BF16)
        o_ref[...] = lax.dot_general(av, b_buf[j % 2], (((0,), (0,)), ((), ())), preferred_element_type=F32).astype(o_ref.dtype)

    hbm = pl.BlockSpec(memory_space=pl.ANY)
    stream = lambda w: [pltpu.VMEM((2, t, w), BF16), pltpu.SemaphoreType.DMA((2,))]
    return pl.pallas_call(
        body, out_shape=jax.ShapeDtypeStruct((nb, k, n), BF16), grid=(nb,),
        in_specs=([hbm] * nb if a_list else [pl.BlockSpec((t, k), lambda j: (0, 0))]) + [hbm] * nb,
        out_specs=pl.BlockSpec((None, k, n), lambda j: (j, 0, 0)),
        scratch_shapes=stream(n) + (stream(k) if a_list else []),
        compiler_params=_params(("arbitrary",)), name=name)(*(a_list or [a]), *bs)


def _rowwise(name, fn, ins, outs, *, consts=(), pos=(), accs=(), heads=1, tm=512, seq=None, rider=None):
    t = ins[0][0].shape[0]
    tm = min(tm, t if seq is None else seq)
    assert t % tm == 0 and (seq is None or seq % tm == 0)
    n_in, n_pos, n_c, n_out, n_acc = len(ins), len(pos), len(consts), len(outs), len(accs)

    def body(*refs):
        vals = [r[...] for r in refs[:n_in + n_pos + n_c]]
        res = fn(*vals)
        o_refs = refs[n_in + n_pos + n_c:]
        for o_ref, r in zip(o_refs[:n_out], res[:n_out]):
            o_ref[...] = r.astype(o_ref.dtype)
        if n_acc:
            first = jnp.logical_and(pl.program_id(0) == 0, pl.program_id(1) == 0)

            @pl.when(first)
            def _():
                for o_ref, r in zip(o_refs[n_out:], res[n_out:]):
                    o_ref[...] = r

            @pl.when(jnp.logical_not(first))
            def _():
                for o_ref, r in zip(o_refs[n_out:], res[n_out:]):
                    o_ref[...] += r

    def tiled(width, c0, per_head):
        return pl.BlockSpec((tm, width), (lambda h, i: (i, c0 + h)) if per_head else (lambda h, i: (i, c0)))

    in_specs = [tiled(w, c0, ph) for _, w, c0, ph in ins]
    if n_pos:
        nblk = seq // tm
        in_specs += [pl.BlockSpec((tm, a.shape[1]), lambda h, i: (i % nblk, 0)) for a in pos]
    in_specs += [pl.BlockSpec(a.shape, lambda h, i: (0, 0)) for a in consts]
    out_specs = [tiled(w, 0, ph) for _, _, w, ph in outs] + [pl.BlockSpec(s, lambda h, i: (0, 0)) for s in accs]
    out_shape = [jax.ShapeDtypeStruct((t, c), d) for c, d, _, _ in outs] + [jax.ShapeDtypeStruct(s, F32) for s in accs]
    sem = ("arbitrary", "arbitrary") if n_acc or rider is not None else ("parallel", "parallel")
    grid = (heads, t // tm)
    if rider is not None:
        rider.n_core_out, rider.n_core_scratch = n_out + n_acc, 0
    body, x_in, x_spec, x_out, x_scr = _ride(body, grid, rider)
    res = pl.pallas_call(body, out_shape=out_shape + x_out, grid=grid, in_specs=in_specs + x_spec, out_specs=out_specs + x_spec,
                         scratch_shapes=x_scr, compiler_params=_params(sem), name=name)(*[a for a, _, _, _ in ins], *pos, *consts, *x_in)
    return res[0] if len(res) == 1 else res


ATTN_HEADS_PER_STEP = 4


def _attn_fwd(name, q, k, kc0, v, vc0, *, heads, group, nseq, seq, tq=512, rider=None):
    tq = min(tq, seq)
    nq = seq // tq
    hp = ATTN_HEADS_PER_STEP
    grid = (heads // hp, nseq, nq)
    shared = group > 1
    assert group % hp == 0 if shared else (kc0 % hp == 0 and vc0 % hp == 0)

    def body(q_ref, k_ref, v_ref, o_ref, lse_ref):
        for j in range(hp):
            cols = slice(j * HEAD_W, (j + 1) * HEAD_W)
            kj = (k_ref[...] if shared else k_ref[:, cols]).astype(BF16)
            vj = (v_ref[...] if shared else v_ref[:, cols]).astype(BF16)
            s = lax.dot_general(q_ref[:, cols], kj, (((1,), (1,)), ((), ())), preferred_element_type=F32)
            m = jnp.max(s, axis=-1, keepdims=True)
            p = jnp.exp((s - m).astype(BF16))
            vj = jnp.where(lax.broadcasted_iota(jnp.int32, (1, HEAD_W), 1) == HEAD_W - 1, jnp.ones((), BF16), vj)
            o = jnp.dot(p, vj, preferred_element_type=F32)
            l = o[:, HEAD_W - 1:]
            o_ref[:, cols] = (o * (1.0 / l)).astype(o_ref.dtype)
            lse_ref[j] = m + jnp.log(l)

    if rider is not None:
        rider.n_core_out, rider.n_core_scratch = 2, 0
    body, x_in, x_spec, x_out, x_scr = _ride(body, grid, rider)
    q_spec = pl.BlockSpec((tq, hp * HEAD_W), lambda h, b, i: (b * nq + i, h))
    if shared:
        k_spec = pl.BlockSpec((seq, HEAD_W), lambda h, b, i: (b, kc0 + (h * hp) // group))
        v_spec = pl.BlockSpec((seq, HEAD_W), lambda h, b, i: (b, vc0 + (h * hp) // group))
    else:
        k_spec = pl.BlockSpec((seq, hp * HEAD_W), lambda h, b, i: (b, kc0 // hp + h))
        v_spec = pl.BlockSpec((seq, hp * HEAD_W), lambda h, b, i: (b, vc0 // hp + h))
    lse_spec = pl.BlockSpec((hp, tq, 1), lambda h, b, i: (h, b * nq + i, 0))
    sem = ("parallel",) * 3 if rider is None else ("arbitrary",) * 3
    return pl.pallas_call(
        body, out_shape=[jax.ShapeDtypeStruct(q.shape, BF16), jax.ShapeDtypeStruct((heads, q.shape[0], 1), F32)] + x_out,
        grid=grid, in_specs=[q_spec, k_spec, v_spec] + x_spec, out_specs=[q_spec, lse_spec] + x_spec, scratch_shapes=x_scr,
        compiler_params=_params(sem), name=name)(q, k, v, *x_in)


def _attn_bwd(name, q, k, kc0, v, vc0, o, do, lse, *, heads, group, nseq, seq, tq=1024, rider=None):
    tq = min(tq, seq)
    nq = seq // tq
    hk = heads // group
    t = q.shape[0]
    grid = (hk, nseq, group, nq)

    def body(q_ref, k_ref, v_ref, o_ref, do_ref, lse_ref, dq_ref, dk_ref, dv_ref, dk_acc, dv_acc):
        g, i = pl.program_id(2), pl.program_id(3)
        qv, kv, vv, dov = q_ref[...], k_ref[...].astype(BF16), v_ref[...].astype(BF16), do_ref[...]
        s = lax.dot_general(qv, kv, (((1,), (1,)), ((), ())), preferred_element_type=F32)
        pn = jnp.exp(s - lse_ref[...])
        dp = lax.dot_general(dov, vv, (((1,), (1,)), ((), ())), preferred_element_type=F32)
        delta = jnp.sum(dov.astype(F32) * o_ref[...].astype(F32), axis=-1, keepdims=True)
        ds = (pn * (dp - delta)).astype(BF16)
        dq_ref[...] = jnp.dot(ds, kv, preferred_element_type=F32)
        dk_part = lax.dot_general(ds, qv, (((0,), (0,)), ((), ())), preferred_element_type=F32)
        dv_part = lax.dot_general(pn.astype(BF16), dov, (((0,), (0,)), ((), ())), preferred_element_type=F32)
        first = jnp.logical_and(g == 0, i == 0)

        @pl.when(first)
        def _():
            dk_acc[...] = dk_part
            dv_acc[...] = dv_part

        @pl.when(jnp.logical_not(first))
        def _():
            dk_acc[...] += dk_part
            dv_acc[...] += dv_part

        @pl.when(jnp.logical_and(g == group - 1, i == nq - 1))
        def _():
            dk_ref[...] = dk_acc[...].astype(dk_ref.dtype)
            dv_ref[...] = dv_acc[...].astype(dv_ref.dtype)

    if rider is not None:
        rider.n_core_out, rider.n_core_scratch = 3, 2
    body, x_in, x_spec, x_out, x_scr = _ride(body, grid, rider)
    q_spec = pl.BlockSpec((tq, HEAD_W), lambda kh, b, g, i: (b * nq + i, kh * group + g))
    kv_out = pl.BlockSpec((seq, HEAD_W), lambda kh, b, g, i: (b, kh))
    lse_spec = pl.BlockSpec((None, tq, 1), lambda kh, b, g, i: (kh * group + g, b * nq + i, 0))
    sem = ("parallel", "parallel", "arbitrary", "arbitrary") if rider is None else ("arbitrary",) * 4
    return pl.pallas_call(
        body,
        out_shape=[jax.ShapeDtypeStruct(q.shape, F32), jax.ShapeDtypeStruct((t, hk * HEAD_W), BF16),
                   jax.ShapeDtypeStruct((t, hk * HEAD_W), BF16)] + x_out,
        grid=grid,
        in_specs=[q_spec, pl.BlockSpec((seq, HEAD_W), lambda kh, b, g, i: (b, kc0 + kh)),
                  pl.BlockSpec((seq, HEAD_W), lambda kh, b, g, i: (b, vc0 + kh)), q_spec, q_spec, lse_spec] + x_spec,
        out_specs=[q_spec, kv_out, kv_out] + x_spec,
        scratch_shapes=[pltpu.VMEM((seq, HEAD_W), F32), pltpu.VMEM((seq, HEAD_W), F32)] + x_scr,
        compiler_params=_params(sem), name=name)(q, k, v, o, do, lse, *x_in)


def _place():
    return lax.axis_index("x"), lax.axis_index("y"), lax.axis_index("c")


def _other_chips(x, y):
    return [(1 - x, y), (x, 1 - y), (1 - x, 1 - y)]


class _Exchange:
    def __init__(self, kind, srcs):
        assert kind in ("gather", "scatter")
        self.kind, self.srcs = kind, list(srcs)
        n = len(self.srcs)
        self.out_shapes = [jax.ShapeDtypeStruct((4, *a.shape[-2:]), a.dtype) for a in self.srcs]
        self.scratch = [pltpu.SemaphoreType.DMA((3 * n,)), pltpu.SemaphoreType.DMA((3 * n,)), pltpu.SemaphoreType.DMA((n,))]
        self.n_core_out = self.n_core_scratch = 0

    def _copies(self, j, src_ref, out_ref, send_sems, recv_sems, landing):
        x, y, c = _place()

        def remote(k, s, d, to):
            return pltpu.make_async_remote_copy(src_ref=s, dst_ref=d, send_sem=send_sems.at[3 * j + k], recv_sem=recv_sems.at[3 * j + k],
                                                device_id=to, device_id_type=MESH)

        me = 2 * x + y
        part = (lambda i: src_ref) if self.kind == "gather" else (lambda i: src_ref.at[i])
        if landing:
            return [remote(k, part(me), out_ref.at[2 * px + py], (px, py, c)) for k, (px, py) in enumerate(_other_chips(x, y))]
        return [remote(k, part(2 * px + py), out_ref.at[me], (px, py, c)) for k, (px, py) in enumerate(_other_chips(x, y))]

    def _local(self, j, src_ref, out_ref, local_sems):
        x, y, _ = _place()
        me = 2 * x + y
        return pltpu.make_async_copy(src_ref if self.kind == "gather" else src_ref.at[me], out_ref.at[me], local_sems.at[j])

    def start(self, src_refs, out_refs, send_sems, recv_sems, local_sems):
        for j, (src_ref, out_ref) in enumerate(zip(src_refs, out_refs)):
            self._local(j, src_ref, out_ref, local_sems).start()
            for mine in self._copies(j, src_ref, out_ref, send_sems, recv_sems, False):
                mine.start()

    def finish(self, src_refs, out_refs, send_sems, recv_sems, local_sems):
        for j, (src_ref, out_ref) in enumerate(zip(src_refs, out_refs)):
            for landed in self._copies(j, src_ref, out_ref, send_sems, recv_sems, True):
                landed.wait_recv()
        for j, (src_ref, out_ref) in enumerate(zip(src_refs, out_refs)):
            for mine in self._copies(j, src_ref, out_ref, send_sems, recv_sems, False):
                mine.wait_send()
            self._local(j, src_ref, out_ref, local_sems).wait()


def _gather_by_halves(name, srcs):
    n = len(srcs)

    def body(*refs):
        x, y, c = _place()
        me = 2 * x + y
        local_sems = refs[-1]
        copies = []
        for j in range(n):
            src_ref, out_ref, send_sems, recv_sems = refs[j], refs[n + j], refs[2 * n + 2 * j], refs[2 * n + 2 * j + 1]
            half = srcs[j].shape[0] // 2
            rows_c = pl.ds(pl.multiple_of(c * half, half), half)
            rows_s = pl.ds(pl.multiple_of((1 - c) * half, half), half)

            def remote(k, s_ref, d_ref, to, send_sems=send_sems, recv_sems=recv_sems):
                return pltpu.make_async_remote_copy(src_ref=s_ref, dst_ref=d_ref, send_sem=send_sems.at[k], recv_sem=recv_sems.at[k],
                                                    device_id=to, device_id_type=MESH)

            local = pltpu.make_async_copy(src_ref, out_ref.at[me], local_sems.at[j])
            local.start()
            chips = _other_chips(x, y)
            sent = [remote(k, src_ref.at[rows_c], out_ref.at[me, rows_c], (px, py, c)) for k, (px, py) in enumerate(chips)]
            landing = [remote(k, src_ref.at[rows_c], out_ref.at[2 * px + py, rows_c], (px, py, c)) for k, (px, py) in enumerate(chips)]
            passed = [remote(3 + k, out_ref.at[2 * px + py, rows_c], out_ref.at[2 * px + py, rows_c], (x, y, 1 - c))
                      for k, (px, py) in enumerate(chips)]
            from_sibling = [remote(3 + k, out_ref.at[2 * px + py, rows_s], out_ref.at[2 * px + py, rows_s], (x, y, 1 - c))
                            for k, (px, py) in enumerate(chips)]
            for cp in sent:
                cp.start()
            copies.append((local, sent, landing, passed, from_sibling))
        for local, sent, landing, passed, from_sibling in copies:
            for k in range(3):
                landing[k].wait_recv()
                passed[k].start()
        for local, sent, landing, passed, from_sibling in copies:
            for k in range(3):
                from_sibling[k].wait_recv()
            for cp in sent + passed:
                cp.wait_send()
            local.wait()

    sems = [pltpu.SemaphoreType.DMA((6,)) for _ in range(2 * n)] + [pltpu.SemaphoreType.DMA((n,))]
    return pl.pallas_call(
        body, out_shape=[jax.ShapeDtypeStruct((4, *a.shape), a.dtype) for a in srcs],
        in_specs=[pl.BlockSpec(memory_space=pl.ANY)] * n, out_specs=[pl.BlockSpec(memory_space=pltpu.VMEM)] * n,
        scratch_shapes=sems, compiler_params=pltpu.CompilerParams(vmem_limit_bytes=VMEM_LIMIT_BYTES), name=name)(*srcs)


def _adamw(w, g, m, v):
    m = ADAM_B1 * m + (1.0 - ADAM_B1) * g
    v = ADAM_B2 * v + (1.0 - ADAM_B2) * (g * g)
    delta = -ADAM_LR * ((m / M_HAT_DIV) / (jnp.sqrt(v / V_HAT_DIV) + ADAM_EPS) + ADAM_WD * w)
    return delta, m, v


def _small_allreduce_adamw(part, w, m, v):
    def body(part_ref, w_ref, m_ref, v_ref, g_out, d_out, m_out, v_out, loss_out, buf, send_sems, recv_sems):
        x, y, c = _place()
        me = 4 * x + 2 * y + c
        buf[me] = part_ref[...]

        def flip(k):
            fx, fy, fc = (k >> 2) & 1, (k >> 1) & 1, k & 1
            px, py, pc = (1 - x if fx else x), (1 - y if fy else y), (1 - c if fc else c)
            return (px, py, pc), 4 * px + 2 * py + pc

        def copy(k, slot):
            return pltpu.make_async_remote_copy(
                src_ref=part_ref, dst_ref=buf.at[slot], send_sem=send_sems.at[k - 1], recv_sem=recv_sems.at[k - 1],
                device_id=flip(k)[0], device_id_type=MESH)

        sent = [copy(k, me) for k in range(1, 8)]
        for cp in sent:
            cp.start()
        for k in range(1, 8):
            copy(k, flip(k)[1]).wait_recv()
        for cp in sent:
            cp.wait_send()
        tot = buf[0]
        for j in range(1, 8):
            tot = tot + buf[j]
        delta, m_new, v_new = _adamw(w_ref[...], tot, m_ref[...], v_ref[...])
        g_out[...] = tot
        d_out[...] = delta
        m_out[...] = m_new
        v_out[...] = v_new
        loss_out[...] = jnp.sum(tot[LOSS_ROW0:LOSS_ROW0 + 8, :]).reshape(1, 1)

    vm = pl.BlockSpec(memory_space=pltpu.VMEM)
    shp = jax.ShapeDtypeStruct((SMALL_ROWS, 128), F32)
    return pl.pallas_call(
        body, out_shape=[shp, shp, shp, shp, jax.ShapeDtypeStruct((1, 1), F32)],
        in_specs=[vm, vm, vm, vm], out_specs=[vm, vm, vm, vm, vm],
        scratch_shapes=[pltpu.VMEM((8, SMALL_ROWS, 128), F32), pltpu.SemaphoreType.DMA((7,)), pltpu.SemaphoreType.DMA((7,))],
        name="small_allreduce_adamw")(part, w, m, v)


def _row_tile(rows, cap):
    return max(t for t in range(16, min(rows, cap) + 1, 16) if rows % t == 0)


def _reduce_pair(name, parts):
    _, rows, w = parts.shape
    tr = _row_tile(rows, 576)
    nt = rows // tr

    def body(p_ref, o_ref, mine, theirs, send_sems, recv_sems):
        i = pl.program_id(0)
        x, y, c = _place()

        def copy(t):
            rows_t = pl.ds(pl.multiple_of(t * tr, tr), tr)
            return pltpu.make_async_remote_copy(src_ref=mine.at[rows_t], dst_ref=theirs.at[rows_t], send_sem=send_sems.at[t],
                                                recv_sem=recv_sems.at[t], device_id=(x, y, 1 - c), device_id_type=MESH)

        @pl.when(i < nt)
        def _():
            mine[pl.ds(pl.multiple_of(i * tr, tr), tr), :] = (
                (p_ref[0].astype(F32) + p_ref[1].astype(F32)) + p_ref[2].astype(F32)) + p_ref[3].astype(F32)
            copy(i).start()

        @pl.when(i >= nt)
        def _():
            t = i - nt
            copy(t).wait()
            rows_t = pl.ds(pl.multiple_of(t * tr, tr), tr)
            o_ref[...] = mine[rows_t, :] + theirs[rows_t, :]

    return pl.pallas_call(
        body, out_shape=jax.ShapeDtypeStruct((rows, w), F32), grid=(2 * nt,),
        in_specs=[pl.BlockSpec((4, tr, w), lambda i: (0, jnp.minimum(i, nt - 1), 0))],
        out_specs=pl.BlockSpec((tr, w), lambda i: (jnp.maximum(i - nt, 0), 0)),
        scratch_shapes=[pltpu.VMEM((rows, w), F32), pltpu.VMEM((rows, w), F32), pltpu.SemaphoreType.DMA((nt,)),
                        pltpu.SemaphoreType.DMA((nt,))],
        compiler_params=_params(("arbitrary",)), name=name)(parts)


def _presum_halves(name, shards):
    _, rows, w = shards.shape
    half = rows // 2

    def body(s_ref, o_ref, theirs, send_sems, recv_sems):
        x, y, c = _place()
        rows_c = pl.ds(pl.multiple_of(c * half, half), half)
        rows_s = pl.ds(pl.multiple_of((1 - c) * half, half), half)
        sent = [pltpu.make_async_remote_copy(src_ref=s_ref.at[j, rows_s], dst_ref=theirs.at[j], send_sem=send_sems.at[j],
                                             recv_sem=recv_sems.at[j], device_id=(x, y, 1 - c), device_id_type=MESH) for j in range(4)]
        for cp in sent:
            cp.start()
        for j, cp in enumerate(sent):
            cp.wait_recv()
            o_ref[j] = (s_ref[j, rows_c, :].astype(F32) + theirs[j].astype(F32)).astype(o_ref.dtype)
        for cp in sent:
            cp.wait_send()

    vm = pl.BlockSpec(memory_space=pltpu.VMEM)
    return pl.pallas_call(
        body, out_shape=jax.ShapeDtypeStruct((4, half, w), shards.dtype), in_specs=[vm], out_specs=vm,
        scratch_shapes=[pltpu.VMEM((4, half, w), shards.dtype), pltpu.SemaphoreType.DMA((4,)), pltpu.SemaphoreType.DMA((4,))],
        compiler_params=pltpu.CompilerParams(vmem_limit_bytes=VMEM_LIMIT_BYTES), name=name)(shards)


def _reduce_halves(name, parts):
    _, half, w = parts.shape

    def body(p_ref, o_ref, mine, send_sem, recv_sem):
        x, y, c = _place()
        rows_c = pl.ds(pl.multiple_of(c * half, half), half)
        rows_s = pl.ds(pl.multiple_of((1 - c) * half, half), half)
        mine[...] = ((p_ref[0].astype(F32) + p_ref[1].astype(F32)) + p_ref[2].astype(F32)) + p_ref[3].astype(F32)
        send = pltpu.make_async_remote_copy(src_ref=mine, dst_ref=o_ref.at[rows_c], send_sem=send_sem, recv_sem=recv_sem,
                                            device_id=(x, y, 1 - c), device_id_type=MESH)
        send.start()
        o_ref[rows_c, :] = mine[...]
        pltpu.make_async_remote_copy(src_ref=mine, dst_ref=o_ref.at[rows_s], send_sem=send_sem, recv_sem=recv_sem,
                                     device_id=(x, y, 1 - c), device_id_type=MESH).wait_recv()
        send.wait_send()

    vm = pl.BlockSpec(memory_space=pltpu.VMEM)
    return pl.pallas_call(
        body, out_shape=jax.ShapeDtypeStruct((2 * half, w), F32), in_specs=[vm], out_specs=vm,
        scratch_shapes=[pltpu.VMEM((half, w), F32), pltpu.SemaphoreType.DMA(()), pltpu.SemaphoreType.DMA(())],
        compiler_params=pltpu.CompilerParams(vmem_limit_bytes=VMEM_LIMIT_BYTES), name=name)(parts)


def _adamw_shard(name, g, w, m, v):
    _, rows, cols = w.shape
    tr = _row_tile(rows, 256)

    def body(g_ref, w_ref, m_ref, v_ref, g_out, d_out, m_out, v_out):
        gv = g_ref[...]
        delta, m_new, v_new = _adamw(w_ref[...], gv, m_ref[...], v_ref[...])
        g_out[...] = gv
        d_out[...] = delta
        m_out[...] = m_new
        v_out[...] = v_new

    t_spec = pl.BlockSpec((None, tr, cols), lambda i: (0, i, 0))
    shp = jax.ShapeDtypeStruct((1, rows, cols), F32)
    return pl.pallas_call(body, out_shape=[shp] * 4, grid=(rows // tr,), in_specs=[pl.BlockSpec((tr, cols), lambda i: (i, 0))] + [t_spec] * 3,
                          out_specs=[t_spec] * 4, compiler_params=_params(("parallel",)), name=name)(g, w, m, v)


def _shard_shape(name):
    _, r, c, ax = BIG_BY_NAME[name]
    return (r, c // 4) if ax == 1 else (r // 4, c)


def _pad_rows(a, axis):
    pad = [(0, 0)] * a.ndim
    pad[axis] = (0, -a.shape[axis] % PACK_ALIGN)
    return jnp.pad(a, pad)


def _pack_shards(names, shards, dtype):
    return _pad_rows(jnp.concatenate([s.astype(dtype).reshape(-1, PACK_W) for s in shards], axis=0), 0)


def _unpack_shards(names, slab):
    out, off = [], 0
    for name in names:
        rs, cs = _shard_shape(name)
        n = rs * cs // PACK_W
        out.append(slab[off:off + n].reshape(rs, cs))
        off += n
    return out


def _unpack_full(names, slabs):
    out, off = [], 0
    for name in names:
        _, r, c, ax = BIG_BY_NAME[name]
        n = r * c // 4 // PACK_W
        seg = slabs[:, off:off + n]
        out.append(seg.reshape(4, r, c // 4).transpose(1, 0, 2).reshape(r, c) if ax == 1 else seg.reshape(r, c))
        off += n
    return out


def _pack_full(names, mats, dtype):
    segs = []
    for name, a in zip(names, mats):
        _, r, c, ax = BIG_BY_NAME[name]
        a = a.astype(dtype)
        a = a.reshape(r, 4, c // 4).transpose(1, 0, 2) if ax == 1 else a
        segs.append(a.reshape(4, -1, PACK_W))
    return _pad_rows(jnp.concatenate(segs, axis=1), 1)


def _pad_heads_cols(wm, heads, d):
    k = wm.shape[0]
    return jnp.pad(wm.reshape(k, heads, d), ((0, 0), (0, 0), (0, HEAD_W - d))).reshape(k, heads * HEAD_W)


def _unpad_heads_cols(wm, heads, d):
    k = wm.shape[0]
    return wm.reshape(k, heads, HEAD_W)[:, :, :d].reshape(k, heads * d)


def _win_ext(w_in):
    o = np.cumsum([0, Q_LORA, KV_LORA, QK_ROPE, H_B * HD_B, KV_B * HD_B, KV_B * HD_B, D_MODEL, D_MODEL])
    pc = lambda a, n: jnp.pad(a, ((0, 0), (0, n - a.shape[1])))
    return jnp.concatenate([
        _pad_heads_cols(w_in[:, o[3]:o[4]], H_B, HD_B), w_in[:, o[0]:o[1]], w_in[:, o[1]:o[2]], pc(w_in[:, o[2]:o[3]], HEAD_W),
        _pad_heads_cols(w_in[:, o[4]:o[5]], KV_B, HD_B), _pad_heads_cols(w_in[:, o[5]:o[6]], KV_B, HD_B),
        w_in[:, o[6]:o[7]], w_in[:, o[7]:o[8]]], axis=1)


def _win_unext(blocks):
    c = HEAD_W
    qb, mid, ga, gb = blocks
    at = lambda zc: (zc - ZC_QLAT) * c
    return jnp.concatenate([
        mid[:, at(ZC_QLAT):at(ZC_CKV)], mid[:, at(ZC_CKV):at(ZC_KPE)], mid[:, at(ZC_KPE):at(ZC_KPE) + QK_ROPE],
        _unpad_heads_cols(qb, H_B, HD_B), _unpad_heads_cols(mid[:, at(ZC_KB):at(ZC_VB)], KV_B, HD_B),
        _unpad_heads_cols(mid[:, at(ZC_VB):at(ZC_GA)], KV_B, HD_B), ga, gb], axis=1)


def _wkv_ext(w_kvb):
    wk = w_kvb.reshape(KV_LORA, H_A, QK_NOPE + V_DIM_A)
    k_cols = jnp.pad(wk[:, :, :QK_NOPE], ((0, 0), (0, 0), (0, HEAD_W - QK_NOPE))).reshape(KV_LORA, H_A * HEAD_W)
    v_cols = jnp.pad(wk[:, :, QK_NOPE:], ((0, 0), (0, 0), (0, HEAD_W - V_DIM_A))).reshape(KV_LORA, H_A * HEAD_W)
    eye = jnp.pad(jnp.eye(QK_ROPE, dtype=w_kvb.dtype), ((0, 0), (QK_NOPE, HEAD_W - QK_NOPE - QK_ROPE)))
    pe_rows = jnp.concatenate([jnp.tile(eye, (1, H_A)), jnp.zeros((QK_ROPE, H_A * HEAD_W), w_kvb.dtype)], axis=1)
    top = jnp.concatenate([k_cols, v_cols], axis=1)
    return jnp.concatenate([top, pe_rows, jnp.zeros((2 * HEAD_W - KV_LORA - QK_ROPE, 2 * H_A * HEAD_W), w_kvb.dtype)], axis=0)


def _wkv_unext(k_block, v_block):
    k_cols = k_block[:KV_LORA].reshape(KV_LORA, H_A, HEAD_W)[:, :, :QK_NOPE]
    v_cols = v_block[:KV_LORA].reshape(KV_LORA, H_A, HEAD_W)[:, :, :V_DIM_A]
    return jnp.concatenate([k_cols, v_cols], axis=2).reshape(KV_LORA, H_A * (QK_NOPE + V_DIM_A))


def _pad_heads_rows(wm, heads, d):
    n = wm.shape[1]
    return jnp.pad(wm.reshape(heads, d, n), ((0, 0), (0, HEAD_W - d), (0, 0))).reshape(heads * HEAD_W, n)


def _unpad_heads_rows(wm, heads, d):
    n = wm.shape[1]
    return wm.reshape(heads, HEAD_W, n)[:, :d].reshape(heads * d, n)


def _rope_tables(seq):
    def ang(pos, dim):
        inv = np.float32(ROPE_THETA) ** (-np.arange(0, dim, 2, dtype=np.float32) / np.float32(dim))
        return pos.astype(np.float32)[:, None] * inv[None, :]

    def rot(dim):
        r = np.zeros((dim, dim), np.float32)
        half = dim // 2
        r[np.arange(half) + half, np.arange(half)] = -1.0
        r[np.arange(half), np.arange(half) + half] = 1.0
        return r

    def table(blocks):
        cos, sin = np.ones((seq, HEAD_W), np.float32), np.zeros((seq, HEAD_W), np.float32)
        pm = np.zeros((HEAD_W, HEAD_W), np.float32)
        for c0, a in blocks:
            d = 2 * a.shape[1]
            cos[:, c0:c0 + d] = np.concatenate([np.cos(a), np.cos(a)], axis=1)
            sin[:, c0:c0 + d] = np.concatenate([np.sin(a), np.sin(a)], axis=1)
            pm[c0:c0 + d, c0:c0 + d] = rot(d)
        return jnp.asarray(cos), jnp.asarray(sin), jnp.asarray(pm, BF16), jnp.asarray(pm.T, BF16)

    tok = np.arange(seq)
    a1 = ang(tok, QK_ROPE)
    arow, acol = ang(tok // GRID_W, HD_B // 2), ang(tok % GRID_W, HD_B // 2)
    return table([(QK_NOPE, a1)]), table([(0, a1)]), table([(0, arow), (HD_B // 2, acol)])


def _local_step(x, p, tgt, gains, wts, ride=None):
    nb, seq, _ = x.shape
    t = nb * seq
    x0 = x.reshape(t, D_MODEL)
    p2 = p.reshape(t, PLE_DIM)
    tg = tgt.reshape(t, D_MODEL)
    (cq_t, sq_t, pq, pq_t), (ck_t, sk_t, pk, pk_t), (cb_t, sb_t, pb, pb_t) = _rope_tables(seq)
    padg = lambda g: jnp.pad(g, ((0, 0), (0, HEAD_W - g.shape[1])))
    g_qn, g_kn = padg(gains["g_qn"]), padg(gains["g_kn"])

    win = _win_ext(wts["w_in"])
    wqb = _pad_heads_cols(wts["w_qb"], H_A, QK_NOPE + QK_ROPE)
    wkv = _wkv_ext(wts["w_kvb"])

    norm = lambda n: (lambda v, g: (_rms(v, g, n),))
    full = lambda a: (a, a.shape[1], 0, False)
    wts = dict(wts)
    rider_of = lambda kernel_name: None if ride is None else ride["gather"][kernel_name][0]

    def landed(kernel_name, got):
        if ride is not None:
            wts.update(ride["gather"][kernel_name][1](got))

    h = _rowwise("norm_mix", norm(D_MODEL), [full(x0)], [(D_MODEL, BF16, D_MODEL, False)], consts=[gains["g_mix"]])
    res = _mm("in_proj", h, win, out_dtypes=(BF16,), tn=2048, rider=rider_of("in_proj"))
    z, got = (res, []) if ride is None else (res[0], res[1:])
    landed("in_proj", got)

    def rope_fwd(scale):
        return lambda v, cos, sin, pm: ((v * cos + _perm(v, pm) * sin) * scale,)

    heads_tile = lambda n: (n * HEAD_W, BF16, n * HEAD_W, False)
    z_qlat, z_ckv, z_kpe = (z, Q_LORA, ZC_QLAT // 2, False), (z, HEAD_W, ZC_CKV, False), (z, HEAD_W, ZC_KPE, False)

    def q_path(zq, cos, sin, g, w, pm):
        cqv = _rms(zq, g, Q_LORA).astype(BF16)
        qa = jnp.dot(cqv, w[...], preferred_element_type=F32)
        return (cqv, *_per_head(rope_fwd(SCALE_A), H_A, 1, 1)(qa, cos, sin, pm))

    cq, q_a = _rowwise("q_path", q_path, [z_qlat], [(Q_LORA, BF16, Q_LORA, False), heads_tile(H_A)],
                       pos=[cq_t, sq_t], consts=[gains["g_qa"], wqb, pq], seq=seq)

    def kv_path(ckv_raw, kpe_raw, cos, sin, g, w, pm):
        kinv = jnp.concatenate([_rms(ckv_raw, g, KV_LORA), *rope_fwd(1.0)(kpe_raw, cos, sin, pm)], axis=1).astype(BF16)
        return kinv, jnp.dot(kinv, w[...], preferred_element_type=F32)

    kin, kv_a = _rowwise("kv_path", kv_path, [z_ckv, z_kpe], [heads_tile(2), heads_tile(2 * H_A)],
                         pos=[ck_t, sk_t], consts=[gains["g_kva"], wkv, pk], seq=seq)
    o_a, lse_a, *got = _attn_fwd("attn_a_fwd", q_a, kv_a, 0, kv_a, H_A, heads=H_A, group=1, nseq=nb, seq=seq,
                                 rider=rider_of("attn_a_fwd"))
    landed("attn_a_fwd", got)

    def prep_fwd(scale):
        def fn(v, cos, sin, g, pm):
            yv = _rms(v, g, HD_B)
            return ((yv * cos + _perm(yv, pm) * sin) * scale,)
        return fn

    z_qb, z_kb = (z, H_B * HEAD_W, ZC_QB // H_B, False), (z, KV_B * HEAD_W, ZC_KB // KV_B, False)
    def prep_b(zq, zk, cos, sin, gq, gk, pm):
        return (*_per_head(prep_fwd(SCALE_B), H_B, 1, 1)(zq, cos, sin, gq, pm), *_per_head(prep_fwd(1.0), KV_B, 1, 1)(zk, cos, sin, gk, pm))

    q_b, k_b, *got = _rowwise("prep_b", prep_b, [z_qb, z_kb], [heads_tile(H_B), heads_tile(KV_B)],
                              pos=[cb_t, sb_t], consts=[g_qn, g_kn, pb], seq=seq, rider=rider_of("prep_b"))
    landed("prep_b", got)
    o_b, lse_b, *got = _attn_fwd("attn_b_fwd", q_b, k_b, 0, z, ZC_VB, heads=H_B, group=H_B // KV_B, nseq=nb, seq=seq,
                                 rider=rider_of("attn_b_fwd"))
    landed("attn_b_fwd", got)
    woa = _pad_heads_rows(wts["w_oa"], H_A, V_DIM_A)
    wob = _pad_heads_rows(wts["w_ob"], H_B, HD_B)
    wo, wup, wdown = wts["w_o"], wts["w_up"], wts["w_down"]

    def residual_norm(acc, r, g):
        xv = r + acc
        return xv, _rms(xv, g, D_MODEL)

    def mix_out(oa, ob, ga, gb, r, g, w_a, w_b, w_out):
        a = jnp.dot(oa, w_a[...], preferred_element_type=F32)
        b = jnp.dot(ob, w_b[...], preferred_element_type=F32)
        mg = (_sigmoid(ga) * a + _sigmoid(gb) * b).astype(BF16)
        return (a, b, mg, *residual_norm(jnp.dot(mg, w_out[...], preferred_element_type=F32), r, g))

    z_ga, z_gb = (z, D_MODEL, ZC_GA // 8, False), (z, D_MODEL, ZC_GB // 8, False)
    wide = lambda d: (D_MODEL, d, D_MODEL, False)
    ya, yb, merged, x1, h2, *got = _rowwise("mix_out", mix_out, [full(o_a), full(o_b), z_ga, z_gb, full(x0)],
                                            [wide(BF16), wide(BF16), wide(BF16), wide(F32), wide(BF16)],
                                            consts=[gains["g_mlp"], woa, wob, wo], tm=256, rider=rider_of("mix_out"))
    landed("mix_out", got)
    wpg, wple = wts["w_ple_gate"], wts["w_ple"]

    square = lambda v: v * v
    u = _mm("mlp_up", h2, wup, b_slots=True, out_dtypes=(BF16,), epi=lambda acc: (jnp.maximum(acc, 0.0),), tm=1024)
    x2, h3 = _mm("mlp_down", u, wdown, a_pre=square, out_dtypes=(F32, BF16), epi=residual_norm, extras=(x1,), consts=[gains["g_ple"]])

    def norm_res_bwd(dh, v, res, g):
        dx, dg = _rms_bwd(dh, v, g, D_MODEL)
        return dx + res, dg

    def tail(x2v, h3v, pv, tv, gf, gp, w_gate, w_emb):
        sg = _sigmoid(jnp.dot(h3v, w_gate[...], preferred_element_type=F32))
        pev = jnp.dot(pv.astype(BF16), w_emb[...], preferred_element_type=F32)
        x3 = x2v + sg * pev
        rs = lax.rsqrt(jnp.sum(x3 * x3, axis=-1, keepdims=True) * (1.0 / D_MODEL) + EPS)
        xh = x3 * rs
        err = xh * gf - tv
        dy = err * (1.0 / D_MODEL)
        dyg = dy * gf
        dx3 = rs * (dyg - xh * (jnp.sum(dyg * xh, axis=-1, keepdims=True) * (1.0 / D_MODEL)))
        dgp = (dx3 * pev * sg * (1.0 - sg)).astype(BF16)
        dh3 = lax.dot_general(dgp, w_gate[...], (((1,), (1,)), ((), ())), preferred_element_type=F32)
        dx2v, dgple = norm_res_bwd(dh3, x2v, dx3, gp)
        return (dx2v, dgp, dx3 * sg, jnp.sum(err * err, axis=0, keepdims=True) * (0.5 / D_MODEL),
                jnp.sum(dy * xh, axis=0, keepdims=True), dgple)

    dx2, dgpre, dpe, loss_part, dg_final, dg_ple = _rowwise(
        "tail", tail, [full(x2), full(h3), full(p2), full(tg)], [wide(F32), wide(BF16), wide(BF16)],
        consts=[gains["g_final"].reshape(1, D_MODEL), gains["g_ple"], wpg, wple], accs=[(1, D_MODEL)] * 3, tm=256)

    dw = {}
    dw["w_ple"] = _mm_tn("dw_ple", p2, dpe)
    dw["w_down"] = _mm_tn("dw_down", u, dx2, a_pre=square)
    dupre = _mm("d_mlp_down", dx2, wdown, trans_b=True, out_dtypes=(BF16,), epi=lambda acc, uv: (acc * (2.0 * uv.astype(F32)),),
                extras=(u,), tn=2048)
    dw["w_up"] = _mm_tn("dw_up", h2, dupre, out_slots=True)
    n_up = wup.shape[0]
    dx1, dg_mlp = _mm("d_mlp_up", [(dupre, j, wup.shape[2]) for j in range(n_up)], [(wup, j) for j in range(n_up)], trans_b=True,
                      epi=norm_res_bwd, extras=(x1, dx2), consts=[gains["g_mlp"]],
                      accs=[(1, D_MODEL)], tm=512)
    dw["w_o"] = _mm_tn("dw_o", merged, dx1)

    def merge_bwd(dm, ga, gb, a, b, w_a, w_b):
        sa, sb = _sigmoid(ga), _sigmoid(gb)
        da, db = (dm * sa).astype(BF16), (dm * sb).astype(BF16)
        nt = (((1,), (1,)), ((), ()))
        return (da, db, dm * a * sa * (1.0 - sa), dm * b * sb * (1.0 - sb),
                lax.dot_general(da, w_a, nt, preferred_element_type=F32), lax.dot_general(db, w_b, nt, preferred_element_type=F32))

    dya, dyb, dga, dgb, do_a, do_b = _mm("d_out_proj", dx1, wo, trans_b=True, out_dtypes=(BF16,) * 6, epi=merge_bwd,
                                         extras=((z, ZC_GA // 8), (z, ZC_GB // 8), ya, yb), consts=[woa, wob], tm=256)
    dw_three = _mm_tn_stream("dw_ple_gate_oa_ob", [h3, o_a, o_b], [dgpre, dya, dyb])
    dw["w_ple_gate"] = dw_three[0]
    dw["w_oa"], dw["w_ob"] = _unpad_heads_rows(dw_three[1], H_A, V_DIM_A), _unpad_heads_rows(dw_three[2], H_B, HD_B)
    res_a = _attn_bwd("attn_a_bwd", q_a, kv_a, 0, kv_a, H_A, o_a, do_a, lse_a, heads=H_A, group=1, nseq=nb, seq=seq,
                      rider=ride and ride["scatter_a"](dw))
    dq_a, dk_a, dv_a = res_a[:3]
    if ride is not None:
        ride["out"]["parts_a"] = res_a[3:]

    def rope_bwd(scale):
        return lambda d, cos, sin, pm_t: ((d * cos + _perm(d * sin, pm_t)) * scale,)

    nt_dims = (((1,), (1,)), ((), ()))

    def q_path_bwd(dq, zq, cos, sin, g, w, pm_t):
        dqav = _per_head(rope_bwd(SCALE_A), H_A, 1, 1)(dq, cos, sin, pm_t)[0].astype(BF16)
        dcq = lax.dot_general(dqav, w[...], nt_dims, preferred_element_type=F32)
        return (dqav, *_rms_bwd(dcq, zq, g, Q_LORA))

    dqa, dq_lat, dg_qa = _rowwise("q_path_bwd", q_path_bwd, [full(dq_a), z_qlat], [heads_tile(H_A), (Q_LORA, BF16, Q_LORA, False)],
                                  pos=[cq_t, sq_t], consts=[gains["g_qa"], wqb, pq_t], accs=[(1, Q_LORA)], seq=seq)
    dw["w_qb"] = _unpad_heads_cols(_mm_tn("dw_qb", cq, dqa), H_A, QK_NOPE + QK_ROPE)
    dw_kv_blocks = _mm_tn_stream("dw_kv", kin, [dk_a, dv_a])
    dw["w_kvb"] = _wkv_unext(dw_kv_blocks[0], dw_kv_blocks[1])
    dq_b, dk_b, dv_b, *parts_b = _attn_bwd("attn_b_bwd", q_b, k_b, 0, z, ZC_VB, o_b, do_b, lse_b, heads=H_B, group=H_B // KV_B,
                                               nseq=nb, seq=seq, rider=ride and ride["scatter_b"](dw))
    if ride is not None:
        ride["out"]["parts_b"] = parts_b

    def kv_path_bwd(dk, dv, ckv_raw, cos, sin, g, w, pm_t):
        kv_w = H_A * HEAD_W
        wv = w[...]
        dkin = (lax.dot_general(dk, wv[:, :kv_w], nt_dims, preferred_element_type=F32)
                + lax.dot_general(dv, wv[:, kv_w:], nt_dims, preferred_element_type=F32))
        dckv_raw, dg = _rms_bwd(dkin[:, :HEAD_W], ckv_raw, g, KV_LORA)
        return (dckv_raw, *rope_bwd(1.0)(dkin[:, HEAD_W:], cos, sin, pm_t), dg)

    dckv, dkpe, dg_kva = _rowwise("kv_path_bwd", kv_path_bwd, [full(dk_a), full(dv_a), z_ckv], [heads_tile(1), heads_tile(1)],
                                  pos=[ck_t, sk_t], consts=[gains["g_kva"], wkv, pk_t], accs=[(1, KV_LORA)], seq=seq)

    def prep_bwd(scale):
        def fn(d, v, cos, sin, g, pm_t):
            dyv = (d * cos + _perm(d * sin, pm_t)) * scale
            return _rms_bwd(dyv, v, g, HD_B)
        return fn

    def prep_b_bwd(dq, dk, zq, zk, cos, sin, gq, gk, pm_t):
        dq_raw, dgq = _per_head(prep_bwd(SCALE_B), H_B, 2, 1)(dq, zq, cos, sin, gq, pm_t)
        dk_raw, dgk = _per_head(prep_bwd(1.0), KV_B, 2, 1)(dk, zk, cos, sin, gk, pm_t)
        return dq_raw, dk_raw, dgq, dgk

    dqb, dkb, dg_qn, dg_kn = _rowwise("prep_b_bwd", prep_b_bwd, [full(dq_b), full(dk_b), z_qb, z_kb], [heads_tile(H_B), heads_tile(KV_B)],
                                      pos=[cb_t, sb_t], consts=[g_qn, g_kn, pb_t], accs=[(1, HEAD_W)] * 2, seq=seq)

    dz = [dqb, jnp.concatenate([dq_lat, dckv, dkpe, dkb, dv_b], axis=1), dga, dgb]
    dw_in_blocks = _mm_tn_stream("dw_in", h, dz)
    dw["w_in"] = _win_unext([dw_in_blocks[j] for j in range(4)])
    dx0, dg_mix, *parts_in = _mm("d_in_proj", dz, [(win, j, D_MODEL) for j in range(4)], trans_b=True, epi=norm_res_bwd, extras=(x0, dx1), consts=[gains["g_mix"]],
                                 accs=[(1, D_MODEL)], tm=256, rider=ride and ride["scatter_in"](dw))
    if ride is not None:
        ride["out"]["parts_in"] = parts_in

    dg = {"g_mix": dg_mix, "g_qa": dg_qa, "g_kva": dg_kva, "g_qn": dg_qn[:, :HD_B], "g_kn": dg_kn[:, :HD_B],
          "g_mlp": dg_mlp, "g_ple": dg_ple, "g_final": dg_final}
    return loss_part, dx0.reshape(nb, seq, D_MODEL), dg, dw


def _pack_small(vals, loss_part=None):
    flat = jnp.concatenate([vals[n].reshape(1, -1) for n, _ in SMALL], axis=1)
    loss = jnp.zeros((1, 8 * 128), F32) if loss_part is None else loss_part
    gap = jnp.zeros((1, LOSS_ROW0 * 128 - SMALL_N), F32)
    return jnp.concatenate([flat, gap, loss], axis=1).reshape(SMALL_ROWS, 128)


def _unpack_small(slab, like):
    flat, out, off = slab.reshape(-1), {}, 0
    for n, k in SMALL:
        out[n] = flat[off:off + k].reshape(like[n].shape)
        off += k
    return out


def kernel(x, p, g_mix, w_in, g_qa, w_qb, g_kva, w_kvb, g_qn, g_kn, w_oa, w_ob, w_o, g_mlp, w_up, w_down, g_ple, w_ple_gate, w_ple, g_final, loss_target, m_g_mix, m_w_in, m_g_qa, m_w_qb, m_g_kva, m_w_kvb, m_g_qn, m_g_kn, m_w_oa, m_w_ob, m_w_o, m_g_mlp, m_w_up, m_w_down, m_g_ple, m_w_ple_gate, m_w_ple, m_g_final, v_g_mix, v_w_in, v_g_qa, v_w_qb, v_g_kva, v_w_kvb, v_g_qn, v_g_kn, v_w_oa, v_w_ob, v_w_o, v_g_mlp, v_w_up, v_w_down, v_g_ple, v_w_ple_gate, v_w_ple, v_g_final):
    given = dict(locals())
    order = ["g_mix", "w_in", "g_qa", "w_qb", "g_kva", "w_kvb", "g_qn", "g_kn", "w_oa", "w_ob", "w_o", "g_mlp", "w_up",
             "w_down", "g_ple", "w_ple_gate", "w_ple", "g_final"]
    big_names = [n for n, _, _, _ in BIG]
    local = lambda prefix, names: [given[prefix + n][0] for n in names]
    slab = lambda names: _pack_shards(names, local("", names), BF16)
    bf = lambda n: given[n][0].astype(BF16)
    cols_full = lambda g: g.transpose(1, 0, 2).reshape(g.shape[1], -1)
    rows_full = lambda g: g.reshape(-1, g.shape[2])
    shards_cols = lambda a: a.reshape(a.shape[0], 4, a.shape[1] // 4).transpose(1, 0, 2)
    shards_rows = lambda a: a.reshape(4, a.shape[0] // 4, a.shape[1])
    packed = lambda names, dw: _pack_full(names, [dw[n] for n in names], BF16)
    branch_out = ["w_oa", "w_ob"]
    back_a, back_b = SLAB_LATE + ["w_o"], SLAB_EARLY + ["w_ple_gate"]

    got_in, got_early = _gather_by_halves("weight_gather_early", [bf("w_in"), slab(SLAB_EARLY)])
    wts = {"w_in": cols_full(got_in), **dict(zip(SLAB_EARLY, _unpack_full(SLAB_EARLY, got_early)))}
    gains = {n: given[n].reshape(1, -1) for n, _ in SMALL}
    ride = {
        "gather": {
            "in_proj": (_Exchange("gather", [bf("w_o")]), lambda got: {"w_o": rows_full(got[0])}),
            "prep_b": (_Exchange("gather", [slab(branch_out)]), lambda got: dict(zip(branch_out, _unpack_full(branch_out, got[0])))),
            "attn_a_fwd": (_Exchange("gather", [bf("w_up")]), lambda got: {"w_up": got[0]}),
            "attn_b_fwd": (_Exchange("gather", [bf("w_down")]), lambda got: {"w_down": rows_full(got[0])}),
            "mix_out": (_Exchange("gather", [bf("w_ple_gate"), bf("w_ple")]),
                        lambda got: {"w_ple_gate": rows_full(got[0]), "w_ple": cols_full(got[1])}),
        },
        "scatter_a": lambda dw: _Exchange("scatter", [dw["w_up"], packed(back_a, dw)]),
        "scatter_b": lambda dw: _Exchange("scatter", [shards_rows(dw["w_down"]), packed(back_b, dw)]),
        "scatter_in": lambda dw: _Exchange("scatter", [_presum_halves("grad_presum_in", shards_cols(dw["w_in"]))]),
        "out": {},
    }
    loss_part, grad_x, dg, dw = _local_step(x, p[0], loss_target, gains, wts, ride)

    small = lambda prefix: _pack_small({n: given[prefix + n] for n, _ in SMALL})
    g_s, d_s, m_s, v_s, loss = _small_allreduce_adamw(_pack_small(dg, loss_part), small(""), small("m_"), small("v_"))

    parts = ride["out"]
    grads = {"w_up": _reduce_pair("grad_reduce_up", parts["parts_a"][0]), "w_down": _reduce_pair("grad_reduce_down", parts["parts_b"][0]),
             "w_in": _reduce_halves("grad_reduce_in", parts["parts_in"][0])}
    grads.update(zip(back_a, _unpack_shards(back_a, _reduce_pair("grad_reduce_slab_a", parts["parts_a"][1]))))
    grads.update(zip(back_b, _unpack_shards(back_b, _reduce_pair("grad_reduce_slab_b", parts["parts_b"][1]))))

    res = {}
    for key, slab in (("grad_", g_s), ("delta_", d_s), ("new_m_", m_s), ("new_v_", v_s)):
        for n, val in _unpack_small(slab, given).items():
            res[key + n] = val
    for n in big_names:
        res["grad_" + n], res["delta_" + n], res["new_m_" + n], res["new_v_" + n] = _adamw_shard(
            "adamw_" + n, grads[n], given[n], given["m_" + n], given["v_" + n])
    outs = [loss.reshape(()), grad_x]
    for key in ("grad_", "delta_", "new_m_", "new_v_"):
        outs += [res[key + n] for n in order]
    return tuple(outs)
```

```python
import functools

import numpy as np
import jax
import jax.numpy as jnp
from jax import lax
from jax.experimental import pallas as pl
from jax.experimental.pallas import tpu as pltpu

F32 = jnp.float32
BF16 = jnp.bfloat16
MESH = pl.DeviceIdType.MESH

D_MODEL = 1024
GRID_W = 64
ROPE_THETA = 10000.0
EPS = 1e-6
H_A, QK_NOPE, QK_ROPE, V_DIM_A, Q_LORA, KV_LORA = 8, 64, 32, 64, 256, 128
H_B, KV_B, HD_B = 8, 2, 64
D_FF = 4096
PLE_DIM = 256
HEAD_W = 128
SCALE_A = (QK_NOPE + QK_ROPE) ** -0.5
SCALE_B = HD_B ** -0.5

ADAM_LR, ADAM_B1, ADAM_B2, ADAM_EPS, ADAM_WD, ADAM_STEP = 0.001, 0.9, 0.999, 1e-08, 0.01, 10
M_HAT_DIV = 1.0 - ADAM_B1 ** ADAM_STEP
V_HAT_DIV = 1.0 - ADAM_B2 ** ADAM_STEP

VMEM_LIMIT_BYTES = 56 * 1024 * 1024

ZC_QB, ZC_QLAT, ZC_CKV, ZC_KPE, ZC_KB, ZC_VB, ZC_GA, ZC_GB = 0, 8, 10, 11, 12, 14, 16, 24
Z_WIDTH = 32 * HEAD_W

BIG = [
    ("w_in", 1024, 3232, 1), ("w_qb", 256, 768, 1), ("w_kvb", 128, 1024, 1), ("w_oa", 512, 1024, 1),
    ("w_ob", 512, 1024, 1), ("w_o", 1024, 1024, 0), ("w_up", 1024, 4096, 1), ("w_down", 4096, 1024, 0),
    ("w_ple_gate", 1024, 1024, 0), ("w_ple", 256, 1024, 1),
]
BIG_BY_NAME = {e[0]: e for e in BIG}
PACK_W = 1024
PACK_ALIGN = 64
SLAB_EARLY = ["w_qb", "w_kvb"]
SLAB_LATE = ["w_oa", "w_ob", "w_ple"]

SMALL = [("g_mix", 1024), ("g_qa", 256), ("g_kva", 128), ("g_qn", 64), ("g_kn", 64), ("g_mlp", 1024),
         ("g_ple", 1024), ("g_final", 1024)]
SMALL_N = sum(n for _, n in SMALL)
LOSS_ROW0 = 40
SMALL_ROWS = 48


def _params(sem):
    return pltpu.CompilerParams(dimension_semantics=sem, vmem_limit_bytes=VMEM_LIMIT_BYTES)


def _sigmoid(v):
    return 1.0 / (1.0 + jnp.exp(-v.astype(F32)))


def _perm(v, p_ref):
    pm = p_ref[...]
    hi = v.astype(BF16)
    lo = (v - hi.astype(F32)).astype(BF16)
    return (jnp.dot(hi, pm, preferred_element_type=F32) + jnp.dot(lo, pm, preferred_element_type=F32))


def _rms(v, g, n):
    v = v.astype(F32)
    rs = lax.rsqrt(jnp.sum(v * v, axis=-1, keepdims=True) * (1.0 / n) + EPS)
    return v * rs * g


def _rms_bwd(dy, v, g, n):
    v = v.astype(F32)
    rs = lax.rsqrt(jnp.sum(v * v, axis=-1, keepdims=True) * (1.0 / n) + EPS)
    vh = v * rs
    dyg = dy * g
    dx = rs * (dyg - vh * (jnp.sum(dyg * vh, axis=-1, keepdims=True) * (1.0 / n)))
    return dx, jnp.sum(dy * vh, axis=0, keepdims=True)


def _ride(body, grid, rider):
    if rider is None:
        return body, [], [], [], []
    n_x, n_sem = len(rider.srcs), len(rider.scratch)

    def wrapped(*refs):
        ids = [pl.program_id(a) for a in range(len(grid))]
        n_in = len(refs) - n_sem - 2 * n_x - rider.n_core_out - rider.n_core_scratch
        core_in, srcs = refs[:n_in], refs[n_in:n_in + n_x]
        core_out = refs[n_in + n_x:n_in + n_x + rider.n_core_out]
        dsts = refs[n_in + n_x + rider.n_core_out:n_in + 2 * n_x + rider.n_core_out]
        core_scr = refs[n_in + 2 * n_x + rider.n_core_out:len(refs) - n_sem]
        sems = refs[len(refs) - n_sem:]

        @pl.when(functools.reduce(jnp.logical_and, [a == 0 for a in ids]))
        def _():
            rider.start(srcs, dsts, *sems)

        body(*core_in, *core_out, *core_scr)

        @pl.when(functools.reduce(jnp.logical_and, [a == n - 1 for a, n in zip(ids, grid)]))
        def _():
            rider.finish(srcs, dsts, *sems)

    hbm = pl.BlockSpec(memory_space=pl.ANY)
    return wrapped, list(rider.srcs), [hbm] * n_x, list(rider.out_shapes), list(rider.scratch)


def _mm(name, a, b, *, trans_b=False, b_slots=False, a_pre=None, out_dtypes=(F32,), epi=None, extras=(), consts=(), accs=(), tm=512,
        tn=None, rider=None):
    a_ops = [o if isinstance(o, tuple) else (o, 0, o.shape[1]) for o in (a if isinstance(a, list) else [a])]
    b_ops = b if isinstance(b, list) else [b]
    assert len(a_ops) == len(b_ops) and not (b_slots and (trans_b or len(b_ops) > 1))
    m = a_ops[0][0].shape[0]
    if b_slots:
        n, tn = b.shape[0] * b.shape[2], b.shape[2]
    else:
        first = b_ops[0][0] if isinstance(b_ops[0], tuple) else b_ops[0]
        n = first.shape[-2] if trans_b else first.shape[1]
        tn = n if tn is None else min(tn, n)
    tm = min(tm, m)
    assert m % tm == 0 and n % tn == 0
    extras = [e if isinstance(e, tuple) else (e, 0) for e in extras]
    n_p, n_ex, n_c, n_out, n_acc = len(a_ops), len(extras), len(consts), len(out_dtypes), len(accs)
    dims = (((1,), (1,)), ((), ())) if trans_b else (((1,), (0,)), ((), ()))

    def body(*refs):
        acc = None
        for a_ref, b_ref in zip(refs[:n_p], refs[n_p:2 * n_p]):
            av = a_ref[...] if a_pre is None else a_pre(a_ref[...].astype(F32))
            part = lax.dot_general(av.astype(BF16), b_ref[...].astype(BF16), dims, preferred_element_type=F32)
            acc = part if acc is None else acc + part
        rest = refs[2 * n_p:]
        res = (acc,) if epi is None else epi(acc, *[e[...] for e in rest[:n_ex + n_c]])
        o_refs = rest[n_ex + n_c:]
        for o_ref, r in zip(o_refs[:n_out], res[:n_out]):
            o_ref[...] = r.astype(o_ref.dtype)
        if n_acc:
            first_step = jnp.logical_and(pl.program_id(0) == 0, pl.program_id(1) == 0)

            @pl.when(first_step)
            def _():
                for o_ref, r in zip(o_refs[n_out:], res[n_out:]):
                    o_ref[...] = r

            @pl.when(jnp.logical_not(first_step))
            def _():
                for o_ref, r in zip(o_refs[n_out:], res[n_out:]):
                    o_ref[...] += r

    def b_spec(op, k_i):
        if b_slots:
            return pl.BlockSpec((None, k_i, tn), lambda j, i: (j, 0, 0))
        if not isinstance(op, tuple):
            return pl.BlockSpec((tn, k_i), lambda j, i: (j, 0)) if trans_b else pl.BlockSpec((k_i, tn), lambda j, i: (0, j))
        assert trans_b
        if len(op) == 2:
            return pl.BlockSpec((None, tn, k_i), lambda j, i, slot=op[1]: (slot, j, 0))
        return pl.BlockSpec((tn, k_i), lambda j, i, blk=op[1]: (j, blk))

    grid = (n // tn, m // tm)
    if rider is not None:
        rider.n_core_out, rider.n_core_scratch = n_out + n_acc, 0
    body, x_in, x_spec, x_out, x_scr = _ride(body, grid, rider)
    a_specs = [pl.BlockSpec((tm, k_i), lambda j, i, blk=blk: (i, blk)) for _, blk, k_i in a_ops]
    b_specs = [b_spec(op, k_i) for op, (_, _, k_i) in zip(b_ops, a_ops)]
    t_spec = pl.BlockSpec((tm, tn), lambda j, i: (i, j))
    e_specs = [pl.BlockSpec((tm, tn), lambda j, i, off=off: (i, j + off)) for _, off in extras]
    c_specs = [pl.BlockSpec(c.shape, lambda j, i: (0, 0)) for c in consts]
    acc_specs = [pl.BlockSpec(sh, lambda j, i: (0, 0)) for sh in accs]
    sem = ("parallel", "parallel") if rider is None and not n_acc else ("arbitrary", "arbitrary")
    outs = pl.pallas_call(
        body, out_shape=[jax.ShapeDtypeStruct((m, n), d) for d in out_dtypes] + [jax.ShapeDtypeStruct(sh, F32) for sh in accs] + x_out,
        grid=grid, in_specs=a_specs + b_specs + e_specs + c_specs + x_spec, out_specs=[t_spec] * n_out + acc_specs + x_spec,
        scratch_shapes=x_scr, compiler_params=_params(sem),
        name=name)(*[o[0] for o in a_ops], *[o[0] if isinstance(o, tuple) else o for o in b_ops], *[e for e, _ in extras], *consts, *x_in)
    return outs[0] if len(outs) == 1 else outs


def _per_head(fn, heads, n_tiled, n_out):
    def run(*args):
        res = [fn(*[a[:, hd * HEAD_W:(hd + 1) * HEAD_W] for a in args[:n_tiled]], *args[n_tiled:]) for hd in range(heads)]
        tiles = [jnp.concatenate([r[k] for r in res], axis=1) for k in range(n_out)]
        sums = [functools.reduce(lambda u, v: u + v, [r[k] for r in res]) for k in range(n_out, len(res[0]))]
        return (*tiles, *sums)
    return run


def _mm_tn(name, a, b, *, a_pre=None, out_dtype=BF16, out_slots=False, tk=1024, tn=1024, tt=4096):
    t, k = a.shape
    n = b.shape[1]
    tk, tn = min(tk, k), min(tn, n)
    if a.dtype == F32 or b.dtype == F32:
        tt = tt // 2
    if k == tk and n == tn:
        tt = tt // 2
    tt = min(tt, t)
    assert b.shape[0] == t and k % tk == 0 and n % tn == 0 and t % tt == 0
    nt = t // tt

    def body(a_ref, b_ref, o_ref, acc):
        av = a_ref[...] if a_pre is None else a_pre(a_ref[...].astype(F32))
        part = lax.dot_general(av.astype(BF16), b_ref[...].astype(BF16), (((0,), (0,)), ((), ())), preferred_element_type=F32)

        @pl.when(pl.program_id(2) == 0)
        def _():
            acc[...] = part

        @pl.when(pl.program_id(2) != 0)
        def _():
            acc[...] += part

        @pl.when(pl.program_id(2) == nt - 1)
        def _():
            o_ref[...] = acc[...].astype(o_ref.dtype)

    if out_slots:
        out_shape, out_spec = (n // tn, k, tn), pl.BlockSpec((None, tk, tn), lambda ki, ni, ti: (ni, ki, 0))
    else:
        out_shape, out_spec = (k, n), pl.BlockSpec((tk, tn), lambda ki, ni, ti: (ki, ni))
    return pl.pallas_call(
        body, out_shape=jax.ShapeDtypeStruct(out_shape, out_dtype), grid=(k // tk, n // tn, nt),
        in_specs=[pl.BlockSpec((tt, tk), lambda ki, ni, ti: (ti, ki)), pl.BlockSpec((tt, tn), lambda ki, ni, ti: (ti, ni))],
        out_specs=out_spec, scratch_shapes=[pltpu.VMEM((tk, tn), F32)],
        compiler_params=_params(("parallel", "parallel", "arbitrary")), name=name)(a, b)


def _mm_tn_stream(name, a, bs):
    a_list = a if isinstance(a, list) else None
    t, k = (a_list[0] if a_list else a).shape
    n, nb = bs[0].shape[1], len(bs)
    n_a, n_o = (nb, nb) if a_list else (1, 1)

    def body(*refs):
        a_refs, b_refs, o_refs = refs[:n_a], refs[n_a:n_a + nb], refs[n_a + nb:n_a + nb + n_o]
        scratch = refs[n_a + nb + n_o:]
        b_buf, b_sems = scratch[0], scratch[1]
        j = pl.program_id(0)

        def fetches(jj):
            copies = [pltpu.make_async_copy(b_refs[jj], b_buf.at[jj % 2], b_sems.at[jj % 2])]
            if a_list:
                copies.append(pltpu.make_async_copy(a_refs[jj], scratch[2].at[jj % 2], scratch[3].at[jj % 2]))
            return copies

        @pl.when(j == 0)
        def _():
            for cp in fetches(0):
                cp.start()

        for jj in range(nb):
            @pl.when(j == jj)
            def _(jj=jj):
                if jj + 1 < nb:
                    for cp in fetches(jj + 1):
                        cp.start()
                for cp in fetches(jj):
                    cp.wait()

        av = scratch[2][j % 2] if a_list else a_refs[0][...].astype(BF16)
        res = lax.dot_general(av, b_buf[j % 2], (((0,), (0,)), ((), ())), preferred_element_type=F32).astype(BF16)
        if a_list:
            for jj in range(nb):
                @pl.when(j == jj)
                def _(jj=jj):
                    o_refs[jj][...] = res
        else:
            o_refs[0][...] = res

    hbm = pl.BlockSpec(memory_space=pl.ANY)
    stream = lambda w: [pltpu.VMEM((2, t, w), BF16), pltpu.SemaphoreType.DMA((2,))]
    return pl.pallas_call(
        body, out_shape=[jax.ShapeDtypeStruct((k, n), BF16)] * nb if a_list else jax.ShapeDtypeStruct((nb, k, n), BF16), grid=(nb,),
        in_specs=([hbm] * nb if a_list else [pl.BlockSpec((t, k), lambda j: (0, 0))]) + [hbm] * nb,
        out_specs=[pl.BlockSpec((k, n), lambda j: (0, 0))] * nb if a_list else pl.BlockSpec((None, k, n), lambda j: (j, 0, 0)),
        scratch_shapes=stream(n) + (stream(k) if a_list else []),
        compiler_params=_params(("arbitrary",)), name=name)(*(a_list or [a]), *bs)


def _rowwise(name, fn, ins, outs, *, consts=(), pos=(), accs=(), heads=1, tm=512, seq=None, rider=None):
    t = ins[0][0].shape[0]
    tm = min(tm, t if seq is None else seq)
    assert t % tm == 0 and (seq is None or seq % tm == 0)
    n_in, n_pos, n_c, n_out, n_acc = len(ins), len(pos), len(consts), len(outs), len(accs)

    def body(*refs):
        vals = [r[...] for r in refs[:n_in + n_pos + n_c]]
        res = fn(*vals)
        o_refs = refs[n_in + n_pos + n_c:]
        for o_ref, r in zip(o_refs[:n_out], res[:n_out]):
            o_ref[...] = r.astype(o_ref.dtype)
        if n_acc:
            first = jnp.logical_and(pl.program_id(0) == 0, pl.program_id(1) == 0)

            @pl.when(first)
            def _():
                for o_ref, r in zip(o_refs[n_out:], res[n_out:]):
                    o_ref[...] = r

            @pl.when(jnp.logical_not(first))
            def _():
                for o_ref, r in zip(o_refs[n_out:], res[n_out:]):
                    o_ref[...] += r

    def tiled(width, c0, per_head):
        return pl.BlockSpec((tm, width), (lambda h, i: (i, c0 + h)) if per_head else (lambda h, i: (i, c0)))

    in_specs = [tiled(w, c0, ph) for _, w, c0, ph in ins]
    if n_pos:
        nblk = seq // tm
        in_specs += [pl.BlockSpec((tm, a.shape[1]), lambda h, i: (i % nblk, 0)) for a in pos]
    in_specs += [pl.BlockSpec(a.shape, lambda h, i: (0, 0)) for a in consts]
    out_specs = [tiled(w, 0, ph) for _, _, w, ph in outs] + [pl.BlockSpec(s, lambda h, i: (0, 0)) for s in accs]
    out_shape = [jax.ShapeDtypeStruct((t, c), d) for c, d, _, _ in outs] + [jax.ShapeDtypeStruct(s, F32) for s in accs]
    sem = ("arbitrary", "arbitrary") if n_acc or rider is not None else ("parallel", "parallel")
    grid = (heads, t // tm)
    if rider is not None:
        rider.n_core_out, rider.n_core_scratch = n_out + n_acc, 0
    body, x_in, x_spec, x_out, x_scr = _ride(body, grid, rider)
    res = pl.pallas_call(body, out_shape=out_shape + x_out, grid=grid, in_specs=in_specs + x_spec, out_specs=out_specs + x_spec,
                         scratch_shapes=x_scr, compiler_params=_params(sem), name=name)(*[a for a, _, _, _ in ins], *pos, *consts, *x_in)
    return res[0] if len(res) == 1 else res


ATTN_HEADS_PER_STEP = 4


def _attn_fwd(name, q, k, kc0, v, vc0, *, heads, group, nseq, seq, tq=512, rider=None):
    tq = min(tq, seq)
    nq = seq // tq
    hp = ATTN_HEADS_PER_STEP
    grid = (heads // hp, nseq, nq)
    shared = group > 1
    assert group % hp == 0 if shared else (kc0 % hp == 0 and vc0 % hp == 0)

    def body(q_ref, k_ref, v_ref, o_ref, lse_ref):
        for j in range(hp):
            cols = slice(j * HEAD_W, (j + 1) * HEAD_W)
            kj = (k_ref[...] if shared else k_ref[:, cols]).astype(BF16)
            vj = (v_ref[...] if shared else v_ref[:, cols]).astype(BF16)
            s = lax.dot_general(q_ref[:, cols], kj, (((1,), (1,)), ((), ())), preferred_element_type=F32)
            m = jnp.max(s, axis=-1, keepdims=True)
            p = jnp.exp((s - m).astype(BF16))
            vj = jnp.where(lax.broadcasted_iota(jnp.int32, (1, HEAD_W), 1) == HEAD_W - 1, jnp.ones((), BF16), vj)
            o = jnp.dot(p, vj, preferred_element_type=F32)
            l = o[:, HEAD_W - 1:]
            o_ref[:, cols] = (o * (1.0 / l)).astype(o_ref.dtype)
            lse_ref[j] = m + jnp.log(l)

    if rider is not None:
        rider.n_core_out, rider.n_core_scratch = 2, 0
    body, x_in, x_spec, x_out, x_scr = _ride(body, grid, rider)
    q_spec = pl.BlockSpec((tq, hp * HEAD_W), lambda h, b, i: (b * nq + i, h))
    if shared:
        k_spec = pl.BlockSpec((seq, HEAD_W), lambda h, b, i: (b, kc0 + (h * hp) // group))
        v_spec = pl.BlockSpec((seq, HEAD_W), lambda h, b, i: (b, vc0 + (h * hp) // group))
    else:
        k_spec = pl.BlockSpec((seq, hp * HEAD_W), lambda h, b, i: (b, kc0 // hp + h))
        v_spec = pl.BlockSpec((seq, hp * HEAD_W), lambda h, b, i: (b, vc0 // hp + h))
    lse_spec = pl.BlockSpec((hp, tq, 1), lambda h, b, i: (h, b * nq + i, 0))
    sem = ("parallel",) * 3 if rider is None else ("arbitrary",) * 3
    return pl.pallas_call(
        body, out_shape=[jax.ShapeDtypeStruct(q.shape, BF16), jax.ShapeDtypeStruct((heads, q.shape[0], 1), F32)] + x_out,
        grid=grid, in_specs=[q_spec, k_spec, v_spec] + x_spec, out_specs=[q_spec, lse_spec] + x_spec, scratch_shapes=x_scr,
        compiler_params=_params(sem), name=name)(q, k, v, *x_in)


def _attn_bwd(name, q, k, kc0, v, vc0, o, do, lse, *, heads, group, nseq, seq, tq=1024, rider=None):
    tq = min(tq, seq)
    nq = seq // tq
    hk = heads // group
    t = q.shape[0]
    grid = (hk, nseq, group, nq)

    def body(q_ref, k_ref, v_ref, o_ref, do_ref, lse_ref, dq_ref, dk_ref, dv_ref, dk_acc, dv_acc):
        g, i = pl.program_id(2), pl.program_id(3)
        qv, kv, vv, dov = q_ref[...], k_ref[...].astype(BF16), v_ref[...].astype(BF16), do_ref[...]
        s = lax.dot_general(qv, kv, (((1,), (1,)), ((), ())), preferred_element_type=F32)
        pn = jnp.exp(s - lse_ref[...])
        dp = lax.dot_general(dov, vv, (((1,), (1,)), ((), ())), preferred_element_type=F32)
        delta = jnp.sum(dov.astype(F32) * o_ref[...].astype(F32), axis=-1, keepdims=True)
        ds = (pn * (dp - delta)).astype(BF16)
        dq_ref[...] = jnp.dot(ds, kv, preferred_element_type=F32)
        dk_part = lax.dot_general(ds, qv, (((0,), (0,)), ((), ())), preferred_element_type=F32)
        dv_part = lax.dot_general(pn.astype(BF16), dov, (((0,), (0,)), ((), ())), preferred_element_type=F32)
        first = jnp.logical_and(g == 0, i == 0)

        @pl.when(first)
        def _():
            dk_acc[...] = dk_part
            dv_acc[...] = dv_part

        @pl.when(jnp.logical_not(first))
        def _():
            dk_acc[...] += dk_part
            dv_acc[...] += dv_part

        @pl.when(jnp.logical_and(g == group - 1, i == nq - 1))
        def _():
            dk_ref[...] = dk_acc[...].astype(dk_ref.dtype)
            dv_ref[...] = dv_acc[...].astype(dv_ref.dtype)

    if rider is not None:
        rider.n_core_out, rider.n_core_scratch = 3, 2
    body, x_in, x_spec, x_out, x_scr = _ride(body, grid, rider)
    q_spec = pl.BlockSpec((tq, HEAD_W), lambda kh, b, g, i: (b * nq + i, kh * group + g))
    kv_out = pl.BlockSpec((seq, HEAD_W), lambda kh, b, g, i: (b, kh))
    lse_spec = pl.BlockSpec((None, tq, 1), lambda kh, b, g, i: (kh * group + g, b * nq + i, 0))
    sem = ("parallel", "parallel", "arbitrary", "arbitrary") if rider is None else ("arbitrary",) * 4
    return pl.pallas_call(
        body,
        out_shape=[jax.ShapeDtypeStruct(q.shape, F32), jax.ShapeDtypeStruct((t, hk * HEAD_W), BF16),
                   jax.ShapeDtypeStruct((t, hk * HEAD_W), BF16)] + x_out,
        grid=grid,
        in_specs=[q_spec, pl.BlockSpec((seq, HEAD_W), lambda kh, b, g, i: (b, kc0 + kh)),
                  pl.BlockSpec((seq, HEAD_W), lambda kh, b, g, i: (b, vc0 + kh)), q_spec, q_spec, lse_spec] + x_spec,
        out_specs=[q_spec, kv_out, kv_out] + x_spec,
        scratch_shapes=[pltpu.VMEM((seq, HEAD_W), F32), pltpu.VMEM((seq, HEAD_W), F32)] + x_scr,
        compiler_params=_params(sem), name=name)(q, k, v, o, do, lse, *x_in)


def _place():
    return lax.axis_index("x"), lax.axis_index("y"), lax.axis_index("c")


def _other_chips(x, y):
    return [(1 - x, y), (x, 1 - y), (1 - x, 1 - y)]


class _Exchange:
    def __init__(self, kind, srcs):
        assert kind in ("gather", "scatter")
        self.kind, self.srcs = kind, list(srcs)
        n = len(self.srcs)
        self.out_shapes = [jax.ShapeDtypeStruct((4, *a.shape[-2:]), a.dtype) for a in self.srcs]
        self.scratch = [pltpu.SemaphoreType.DMA((3 * n,)), pltpu.SemaphoreType.DMA((3 * n,)), pltpu.SemaphoreType.DMA((n,))]
        self.n_core_out = self.n_core_scratch = 0

    def _copies(self, j, src_ref, out_ref, send_sems, recv_sems, landing):
        x, y, c = _place()

        def remote(k, s, d, to):
            return pltpu.make_async_remote_copy(src_ref=s, dst_ref=d, send_sem=send_sems.at[3 * j + k], recv_sem=recv_sems.at[3 * j + k],
                                                device_id=to, device_id_type=MESH)

        me = 2 * x + y
        part = (lambda i: src_ref) if self.kind == "gather" else (lambda i: src_ref.at[i])
        if landing:
            return [remote(k, part(me), out_ref.at[2 * px + py], (px, py, c)) for k, (px, py) in enumerate(_other_chips(x, y))]
        return [remote(k, part(2 * px + py), out_ref.at[me], (px, py, c)) for k, (px, py) in enumerate(_other_chips(x, y))]

    def _local(self, j, src_ref, out_ref, local_sems):
        x, y, _ = _place()
        me = 2 * x + y
        return pltpu.make_async_copy(src_ref if self.kind == "gather" else src_ref.at[me], out_ref.at[me], local_sems.at[j])

    def start(self, src_refs, out_refs, send_sems, recv_sems, local_sems):
        for j, (src_ref, out_ref) in enumerate(zip(src_refs, out_refs)):
            self._local(j, src_ref, out_ref, local_sems).start()
            for mine in self._copies(j, src_ref, out_ref, send_sems, recv_sems, False):
                mine.start()

    def finish(self, src_refs, out_refs, send_sems, recv_sems, local_sems):
        for j, (src_ref, out_ref) in enumerate(zip(src_refs, out_refs)):
            for landed in self._copies(j, src_ref, out_ref, send_sems, recv_sems, True):
                landed.wait_recv()
        for j, (src_ref, out_ref) in enumerate(zip(src_refs, out_refs)):
            for mine in self._copies(j, src_ref, out_ref, send_sems, recv_sems, False):
                mine.wait_send()
            self._local(j, src_ref, out_ref, local_sems).wait()


def _gather_by_halves(name, srcs):
    n = len(srcs)

    def body(*refs):
        x, y, c = _place()
        me = 2 * x + y
        local_sems = refs[-1]
        copies = []
        for j in range(n):
            src_ref, out_ref, send_sems, recv_sems = refs[j], refs[n + j], refs[2 * n + 2 * j], refs[2 * n + 2 * j + 1]
            half = srcs[j].shape[0] // 2
            rows_c = pl.ds(pl.multiple_of(c * half, half), half)
            rows_s = pl.ds(pl.multiple_of((1 - c) * half, half), half)

            def remote(k, s_ref, d_ref, to, send_sems=send_sems, recv_sems=recv_sems):
                return pltpu.make_async_remote_copy(src_ref=s_ref, dst_ref=d_ref, send_sem=send_sems.at[k], recv_sem=recv_sems.at[k],
                                                    device_id=to, device_id_type=MESH)

            local = pltpu.make_async_copy(src_ref, out_ref.at[me], local_sems.at[j])
            local.start()
            chips = _other_chips(x, y)
            sent = [remote(k, src_ref.at[rows_c], out_ref.at[me, rows_c], (px, py, c)) for k, (px, py) in enumerate(chips)]
            landing = [remote(k, src_ref.at[rows_c], out_ref.at[2 * px + py, rows_c], (px, py, c)) for k, (px, py) in enumerate(chips)]
            passed = [remote(3 + k, out_ref.at[2 * px + py, rows_c], out_ref.at[2 * px + py, rows_c], (x, y, 1 - c))
                      for k, (px, py) in enumerate(chips)]
            from_sibling = [remote(3 + k, out_ref.at[2 * px + py, rows_s], out_ref.at[2 * px + py, rows_s], (x, y, 1 - c))
                            for k, (px, py) in enumerate(chips)]
            for cp in sent:
                cp.start()
            copies.append((local, sent, landing, passed, from_sibling))
        for local, sent, landing, passed, from_sibling in copies:
            for k in range(3):
                landing[k].wait_recv()
                passed[k].start()
        for local, sent, landing, passed, from_sibling in copies:
            for k in range(3):
                from_sibling[k].wait_recv()
            for cp in sent + passed:
                cp.wait_send()
            local.wait()

    sems = [pltpu.SemaphoreType.DMA((6,)) for _ in range(2 * n)] + [pltpu.SemaphoreType.DMA((n,))]
    return pl.pallas_call(
        body, out_shape=[jax.ShapeDtypeStruct((4, *a.shape), a.dtype) for a in srcs],
        in_specs=[pl.BlockSpec(memory_space=pl.ANY)] * n, out_specs=[pl.BlockSpec(memory_space=pltpu.VMEM)] * n,
        scratch_shapes=sems, compiler_params=pltpu.CompilerParams(vmem_limit_bytes=VMEM_LIMIT_BYTES), name=name)(*srcs)


def _adamw(w, g, m, v):
    m = ADAM_B1 * m + (1.0 - ADAM_B1) * g
    v = ADAM_B2 * v + (1.0 - ADAM_B2) * (g * g)
    delta = -ADAM_LR * ((m / M_HAT_DIV) / (jnp.sqrt(v / V_HAT_DIV) + ADAM_EPS) + ADAM_WD * w)
    return delta, m, v


def _small_allreduce_adamw(part, w, m, v):
    def body(part_ref, w_ref, m_ref, v_ref, g_out, d_out, m_out, v_out, loss_out, buf, send_sems, recv_sems):
        x, y, c = _place()
        me = 4 * x + 2 * y + c
        buf[me] = part_ref[...]

        def flip(k):
            fx, fy, fc = (k >> 2) & 1, (k >> 1) & 1, k & 1
            px, py, pc = (1 - x if fx else x), (1 - y if fy else y), (1 - c if fc else c)
            return (px, py, pc), 4 * px + 2 * py + pc

        def copy(k, slot):
            return pltpu.make_async_remote_copy(
                src_ref=part_ref, dst_ref=buf.at[slot], send_sem=send_sems.at[k - 1], recv_sem=recv_sems.at[k - 1],
                device_id=flip(k)[0], device_id_type=MESH)

        sent = [copy(k, me) for k in range(1, 8)]
        for cp in sent:
            cp.start()
        for k in range(1, 8):
            copy(k, flip(k)[1]).wait_recv()
        for cp in sent:
            cp.wait_send()
        tot = buf[0]
        for j in range(1, 8):
            tot = tot + buf[j]
        delta, m_new, v_new = _adamw(w_ref[...], tot, m_ref[...], v_ref[...])
        g_out[...] = tot
        d_out[...] = delta
        m_out[...] = m_new
        v_out[...] = v_new
        loss_out[...] = jnp.sum(tot[LOSS_ROW0:LOSS_ROW0 + 8, :]).reshape(1, 1)

    vm = pl.BlockSpec(memory_space=pltpu.VMEM)
    shp = jax.ShapeDtypeStruct((SMALL_ROWS, 128), F32)
    return pl.pallas_call(
        body, out_shape=[shp, shp, shp, shp, jax.ShapeDtypeStruct((1, 1), F32)],
        in_specs=[vm, vm, vm, vm], out_specs=[vm, vm, vm, vm, vm],
        scratch_shapes=[pltpu.VMEM((8, SMALL_ROWS, 128), F32), pltpu.SemaphoreType.DMA((7,)), pltpu.SemaphoreType.DMA((7,))],
        name="small_allreduce_adamw")(part, w, m, v)


def _row_tile(rows, cap):
    return max(t for t in range(16, min(rows, cap) + 1, 16) if rows % t == 0)


def _reduce_pair(name, parts):
    _, rows, w = parts.shape
    tr = _row_tile(rows, 576)
    nt = rows // tr

    def body(p_ref, o_ref, mine, theirs, send_sems, recv_sems):
        i = pl.program_id(0)
        x, y, c = _place()

        def copy(t):
            rows_t = pl.ds(pl.multiple_of(t * tr, tr), tr)
            return pltpu.make_async_remote_copy(src_ref=mine.at[rows_t], dst_ref=theirs.at[rows_t], send_sem=send_sems.at[t],
                                                recv_sem=recv_sems.at[t], device_id=(x, y, 1 - c), device_id_type=MESH)

        @pl.when(i < nt)
        def _():
            mine[pl.ds(pl.multiple_of(i * tr, tr), tr), :] = (
                (p_ref[0].astype(F32) + p_ref[1].astype(F32)) + p_ref[2].astype(F32)) + p_ref[3].astype(F32)
            copy(i).start()

        @pl.when(i >= nt)
        def _():
            t = i - nt
            copy(t).wait()
            rows_t = pl.ds(pl.multiple_of(t * tr, tr), tr)
            o_ref[...] = mine[rows_t, :] + theirs[rows_t, :]

    return pl.pallas_call(
        body, out_shape=jax.ShapeDtypeStruct((rows, w), F32), grid=(2 * nt,),
        in_specs=[pl.BlockSpec((4, tr, w), lambda i: (0, jnp.minimum(i, nt - 1), 0))],
        out_specs=pl.BlockSpec((tr, w), lambda i: (jnp.maximum(i - nt, 0), 0)),
        scratch_shapes=[pltpu.VMEM((rows, w), F32), pltpu.VMEM((rows, w), F32), pltpu.SemaphoreType.DMA((nt,)),
                        pltpu.SemaphoreType.DMA((nt,))],
        compiler_params=_params(("arbitrary",)), name=name)(parts)


def _presum_halves(name, shards):
    _, rows, w = shards.shape
    half = rows // 2

    def body(s_ref, o_ref, theirs, send_sems, recv_sems):
        x, y, c = _place()
        rows_c = pl.ds(pl.multiple_of(c * half, half), half)
        rows_s = pl.ds(pl.multiple_of((1 - c) * half, half), half)
        sent = [pltpu.make_async_remote_copy(src_ref=s_ref.at[j, rows_s], dst_ref=theirs.at[j], send_sem=send_sems.at[j],
                                             recv_sem=recv_sems.at[j], device_id=(x, y, 1 - c), device_id_type=MESH) for j in range(4)]
        for cp in sent:
            cp.start()
        for j, cp in enumerate(sent):
            cp.wait_recv()
            o_ref[j] = (s_ref[j, rows_c, :].astype(F32) + theirs[j].astype(F32)).astype(o_ref.dtype)
        for cp in sent:
            cp.wait_send()

    vm = pl.BlockSpec(memory_space=pltpu.VMEM)
    return pl.pallas_call(
        body, out_shape=jax.ShapeDtypeStruct((4, half, w), shards.dtype), in_specs=[vm], out_specs=vm,
        scratch_shapes=[pltpu.VMEM((4, half, w), shards.dtype), pltpu.SemaphoreType.DMA((4,)), pltpu.SemaphoreType.DMA((4,))],
        compiler_params=pltpu.CompilerParams(vmem_limit_bytes=VMEM_LIMIT_BYTES), name=name)(shards)


def _reduce_halves(name, parts):
    _, half, w = parts.shape

    def body(p_ref, o_ref, mine, send_sem, recv_sem):
        x, y, c = _place()
        rows_c = pl.ds(pl.multiple_of(c * half, half), half)
        rows_s = pl.ds(pl.multiple_of((1 - c) * half, half), half)
        mine[...] = ((p_ref[0].astype(F32) + p_ref[1].astype(F32)) + p_ref[2].astype(F32)) + p_ref[3].astype(F32)
        send = pltpu.make_async_remote_copy(src_ref=mine, dst_ref=o_ref.at[rows_c], send_sem=send_sem, recv_sem=recv_sem,
                                            device_id=(x, y, 1 - c), device_id_type=MESH)
        send.start()
        o_ref[rows_c, :] = mine[...]
        pltpu.make_async_remote_copy(src_ref=mine, dst_ref=o_ref.at[rows_s], send_sem=send_sem, recv_sem=recv_sem,
                                     device_id=(x, y, 1 - c), device_id_type=MESH).wait_recv()
        send.wait_send()

    vm = pl.BlockSpec(memory_space=pltpu.VMEM)
    return pl.pallas_call(
        body, out_shape=jax.ShapeDtypeStruct((2 * half, w), F32), in_specs=[vm], out_specs=vm,
        scratch_shapes=[pltpu.VMEM((half, w), F32), pltpu.SemaphoreType.DMA(()), pltpu.SemaphoreType.DMA(())],
        compiler_params=pltpu.CompilerParams(vmem_limit_bytes=VMEM_LIMIT_BYTES), name=name)(parts)


def _adamw_shard(name, g, w, m, v):
    _, rows, cols = w.shape
    tr = _row_tile(rows, 256)

    def body(g_ref, w_ref, m_ref, v_ref, g_out, d_out, m_out, v_out):
        gv = g_ref[...]
        delta, m_new, v_new = _adamw(w_ref[...], gv, m_ref[...], v_ref[...])
        g_out[...] = gv
        d_out[...] = delta
        m_out[...] = m_new
        v_out[...] = v_new

    t_spec = pl.BlockSpec((None, tr, cols), lambda i: (0, i, 0))
    shp = jax.ShapeDtypeStruct((1, rows, cols), F32)
    return pl.pallas_call(body, out_shape=[shp] * 4, grid=(rows // tr,), in_specs=[pl.BlockSpec((tr, cols), lambda i: (i, 0))] + [t_spec] * 3,
                          out_specs=[t_spec] * 4, compiler_params=_params(("parallel",)), name=name)(g, w, m, v)


def _shard_shape(name):
    _, r, c, ax = BIG_BY_NAME[name]
    return (r, c // 4) if ax == 1 else (r // 4, c)


def _pad_rows(a, axis):
    pad = [(0, 0)] * a.ndim
    pad[axis] = (0, -a.shape[axis] % PACK_ALIGN)
    return jnp.pad(a, pad)


def _pack_shards(names, shards, dtype):
    return _pad_rows(jnp.concatenate([s.astype(dtype).reshape(-1, PACK_W) for s in shards], axis=0), 0)


def _unpack_shards(names, slab):
    out, off = [], 0
    for name in names:
        rs, cs = _shard_shape(name)
        n = rs * cs // PACK_W
        out.append(slab[off:off + n].reshape(rs, cs))
        off += n
    return out


def _unpack_full(names, slabs):
    out, off = [], 0
    for name in names:
        _, r, c, ax = BIG_BY_NAME[name]
        n = r * c // 4 // PACK_W
        seg = slabs[:, off:off + n]
        out.append(seg.reshape(4, r, c // 4).transpose(1, 0, 2).reshape(r, c) if ax == 1 else seg.reshape(r, c))
        off += n
    return out


def _pack_full(names, mats, dtype):
    segs = []
    for name, a in zip(names, mats):
        _, r, c, ax = BIG_BY_NAME[name]
        a = a.astype(dtype)
        a = a.reshape(r, 4, c // 4).transpose(1, 0, 2) if ax == 1 else a
        segs.append(a.reshape(4, -1, PACK_W))
    return _pad_rows(jnp.concatenate(segs, axis=1), 1)


def _pad_heads_cols(wm, heads, d):
    k = wm.shape[0]
    return jnp.pad(wm.reshape(k, heads, d), ((0, 0), (0, 0), (0, HEAD_W - d))).reshape(k, heads * HEAD_W)


def _unpad_heads_cols(wm, heads, d):
    k = wm.shape[0]
    return wm.reshape(k, heads, HEAD_W)[:, :, :d].reshape(k, heads * d)


def _win_ext(w_in):
    o = np.cumsum([0, Q_LORA, KV_LORA, QK_ROPE, H_B * HD_B, KV_B * HD_B, KV_B * HD_B, D_MODEL, D_MODEL])
    pc = lambda a, n: jnp.pad(a, ((0, 0), (0, n - a.shape[1])))
    return jnp.concatenate([
        _pad_heads_cols(w_in[:, o[3]:o[4]], H_B, HD_B), w_in[:, o[0]:o[1]], w_in[:, o[1]:o[2]], pc(w_in[:, o[2]:o[3]], HEAD_W),
        _pad_heads_cols(w_in[:, o[4]:o[5]], KV_B, HD_B), _pad_heads_cols(w_in[:, o[5]:o[6]], KV_B, HD_B),
        w_in[:, o[6]:o[7]], w_in[:, o[7]:o[8]]], axis=1)


def _win_unext(blocks):
    c = HEAD_W
    qb, mid, ga, gb = blocks
    at = lambda zc: (zc - ZC_QLAT) * c
    return jnp.concatenate([
        mid[:, at(ZC_QLAT):at(ZC_CKV)], mid[:, at(ZC_CKV):at(ZC_KPE)], mid[:, at(ZC_KPE):at(ZC_KPE) + QK_ROPE],
        _unpad_heads_cols(qb, H_B, HD_B), _unpad_heads_cols(mid[:, at(ZC_KB):at(ZC_VB)], KV_B, HD_B),
        _unpad_heads_cols(mid[:, at(ZC_VB):at(ZC_GA)], KV_B, HD_B), ga, gb], axis=1)


def _wkv_ext(w_kvb):
    wk = w_kvb.reshape(KV_LORA, H_A, QK_NOPE + V_DIM_A)
    k_cols = jnp.pad(wk[:, :, :QK_NOPE], ((0, 0), (0, 0), (0, HEAD_W - QK_NOPE))).reshape(KV_LORA, H_A * HEAD_W)
    v_cols = jnp.pad(wk[:, :, QK_NOPE:], ((0, 0), (0, 0), (0, HEAD_W - V_DIM_A))).reshape(KV_LORA, H_A * HEAD_W)
    eye = jnp.pad(jnp.eye(QK_ROPE, dtype=w_kvb.dtype), ((0, 0), (QK_NOPE, HEAD_W - QK_NOPE - QK_ROPE)))
    pe_rows = jnp.concatenate([jnp.tile(eye, (1, H_A)), jnp.zeros((QK_ROPE, H_A * HEAD_W), w_kvb.dtype)], axis=1)
    top = jnp.concatenate([k_cols, v_cols], axis=1)
    return jnp.concatenate([top, pe_rows, jnp.zeros((2 * HEAD_W - KV_LORA - QK_ROPE, 2 * H_A * HEAD_W), w_kvb.dtype)], axis=0)


def _wkv_unext(k_block, v_block):
    k_cols = k_block[:KV_LORA].reshape(KV_LORA, H_A, HEAD_W)[:, :, :QK_NOPE]
    v_cols = v_block[:KV_LORA].reshape(KV_LORA, H_A, HEAD_W)[:, :, :V_DIM_A]
    return jnp.concatenate([k_cols, v_cols], axis=2).reshape(KV_LORA, H_A * (QK_NOPE + V_DIM_A))


def _pad_heads_rows(wm, heads, d):
    n = wm.shape[1]
    return jnp.pad(wm.reshape(heads, d, n), ((0, 0), (0, HEAD_W - d), (0, 0))).reshape(heads * HEAD_W, n)


def _unpad_heads_rows(wm, heads, d):
    n = wm.shape[1]
    return wm.reshape(heads, HEAD_W, n)[:, :d].reshape(heads * d, n)


def _rope_tables(seq):
    def ang(pos, dim):
        inv = np.float32(ROPE_THETA) ** (-np.arange(0, dim, 2, dtype=np.float32) / np.float32(dim))
        return pos.astype(np.float32)[:, None] * inv[None, :]

    def rot(dim):
        r = np.zeros((dim, dim), np.float32)
        half = dim // 2
        r[np.arange(half) + half, np.arange(half)] = -1.0
        r[np.arange(half), np.arange(half) + half] = 1.0
        return r

    def table(blocks):
        cos, sin = np.ones((seq, HEAD_W), np.float32), np.zeros((seq, HEAD_W), np.float32)
        pm = np.zeros((HEAD_W, HEAD_W), np.float32)
        for c0, a in blocks:
            d = 2 * a.shape[1]
            cos[:, c0:c0 + d] = np.concatenate([np.cos(a), np.cos(a)], axis=1)
            sin[:, c0:c0 + d] = np.concatenate([np.sin(a), np.sin(a)], axis=1)
            pm[c0:c0 + d, c0:c0 + d] = rot(d)
        return jnp.asarray(cos), jnp.asarray(sin), jnp.asarray(pm, BF16), jnp.asarray(pm.T, BF16)

    tok = np.arange(seq)
    a1 = ang(tok, QK_ROPE)
    arow, acol = ang(tok // GRID_W, HD_B // 2), ang(tok % GRID_W, HD_B // 2)
    return table([(QK_NOPE, a1)]), table([(0, a1)]), table([(0, arow), (HD_B // 2, acol)])


def _local_step(x, p, tgt, gains, wts, ride=None):
    nb, seq, _ = x.shape
    t = nb * seq
    x0 = x.reshape(t, D_MODEL)
    p2 = p.reshape(t, PLE_DIM)
    tg = tgt.reshape(t, D_MODEL)
    (cq_t, sq_t, pq, pq_t), (ck_t, sk_t, pk, pk_t), (cb_t, sb_t, pb, pb_t) = _rope_tables(seq)
    padg = lambda g: jnp.pad(g, ((0, 0), (0, HEAD_W - g.shape[1])))
    g_qn, g_kn = padg(gains["g_qn"]), padg(gains["g_kn"])

    win = _win_ext(wts["w_in"])
    wqb = _pad_heads_cols(wts["w_qb"], H_A, QK_NOPE + QK_ROPE)
    wkv = _wkv_ext(wts["w_kvb"])

    norm = lambda n: (lambda v, g: (_rms(v, g, n),))
    full = lambda a: (a, a.shape[1], 0, False)
    wts = dict(wts)
    rider_of = lambda kernel_name: None if ride is None else ride["gather"][kernel_name][0]

    def landed(kernel_name, got):
        if ride is not None:
            wts.update(ride["gather"][kernel_name][1](got))

    h = _rowwise("norm_mix", norm(D_MODEL), [full(x0)], [(D_MODEL, BF16, D_MODEL, False)], consts=[gains["g_mix"]])
    res = _mm("in_proj", h, win, out_dtypes=(BF16,), tn=2048, rider=rider_of("in_proj"))
    z, got = (res, []) if ride is None else (res[0], res[1:])
    landed("in_proj", got)

    def rope_fwd(scale):
        return lambda v, cos, sin, pm: ((v * cos + _perm(v, pm) * sin) * scale,)

    heads_tile = lambda n: (n * HEAD_W, BF16, n * HEAD_W, False)
    z_qlat, z_ckv, z_kpe = (z, Q_LORA, ZC_QLAT // 2, False), (z, HEAD_W, ZC_CKV, False), (z, HEAD_W, ZC_KPE, False)

    def q_path(zq, cos, sin, g, w, pm):
        cqv = _rms(zq, g, Q_LORA).astype(BF16)
        qa = jnp.dot(cqv, w[...], preferred_element_type=F32)
        return (cqv, *_per_head(rope_fwd(SCALE_A), H_A, 1, 1)(qa, cos, sin, pm))

    cq, q_a = _rowwise("q_path", q_path, [z_qlat], [(Q_LORA, BF16, Q_LORA, False), heads_tile(H_A)],
                       pos=[cq_t, sq_t], consts=[gains["g_qa"], wqb, pq], seq=seq)

    def kv_path(ckv_raw, kpe_raw, cos, sin, g, w, pm):
        kinv = jnp.concatenate([_rms(ckv_raw, g, KV_LORA), *rope_fwd(1.0)(kpe_raw, cos, sin, pm)], axis=1).astype(BF16)
        return kinv, jnp.dot(kinv, w[...], preferred_element_type=F32)

    kin, kv_a = _rowwise("kv_path", kv_path, [z_ckv, z_kpe], [heads_tile(2), heads_tile(2 * H_A)],
                         pos=[ck_t, sk_t], consts=[gains["g_kva"], wkv, pk], seq=seq)
    o_a, lse_a, *got = _attn_fwd("attn_a_fwd", q_a, kv_a, 0, kv_a, H_A, heads=H_A, group=1, nseq=nb, seq=seq,
                                 rider=rider_of("attn_a_fwd"))
    landed("attn_a_fwd", got)

    def prep_fwd(scale):
        def fn(v, cos, sin, g, pm):
            yv = _rms(v, g, HD_B)
            return ((yv * cos + _perm(yv, pm) * sin) * scale,)
        return fn

    z_qb, z_kb = (z, H_B * HEAD_W, ZC_QB // H_B, False), (z, KV_B * HEAD_W, ZC_KB // KV_B, False)
    def prep_b(zq, zk, cos, sin, gq, gk, pm):
        return (*_per_head(prep_fwd(SCALE_B), H_B, 1, 1)(zq, cos, sin, gq, pm), *_per_head(prep_fwd(1.0), KV_B, 1, 1)(zk, cos, sin, gk, pm))

    q_b, k_b, *got = _rowwise("prep_b", prep_b, [z_qb, z_kb], [heads_tile(H_B), heads_tile(KV_B)],
                              pos=[cb_t, sb_t], consts=[g_qn, g_kn, pb], seq=seq, rider=rider_of("prep_b"))
    landed("prep_b", got)
    o_b, lse_b, *got = _attn_fwd("attn_b_fwd", q_b, k_b, 0, z, ZC_VB, heads=H_B, group=H_B // KV_B, nseq=nb, seq=seq,
                                 rider=rider_of("attn_b_fwd"))
    landed("attn_b_fwd", got)
    woa = _pad_heads_rows(wts["w_oa"], H_A, V_DIM_A)
    wob = _pad_heads_rows(wts["w_ob"], H_B, HD_B)
    wo, wup, wdown = wts["w_o"], wts["w_up"], wts["w_down"]

    def residual_norm(acc, r, g):
        xv = r + acc
        return xv, _rms(xv, g, D_MODEL)

    def mix_out(oa, ob, ga, gb, r, g, w_a, w_b, w_out):
        a = jnp.dot(oa, w_a[...], preferred_element_type=F32)
        b = jnp.dot(ob, w_b[...], preferred_element_type=F32)
        mg = (_sigmoid(ga) * a + _sigmoid(gb) * b).astype(BF16)
        return (a, b, mg, *residual_norm(jnp.dot(mg, w_out[...], preferred_element_type=F32), r, g))

    z_ga, z_gb = (z, D_MODEL, ZC_GA // 8, False), (z, D_MODEL, ZC_GB // 8, False)
    wide = lambda d: (D_MODEL, d, D_MODEL, False)
    ya, yb, merged, x1, h2, *got = _rowwise("mix_out", mix_out, [full(o_a), full(o_b), z_ga, z_gb, full(x0)],
                                            [wide(BF16), wide(BF16), wide(BF16), wide(F32), wide(BF16)],
                                            consts=[gains["g_mlp"], woa, wob, wo], tm=256, rider=rider_of("mix_out"))
    landed("mix_out", got)
    wpg, wple = wts["w_ple_gate"], wts["w_ple"]

    square = lambda v: v * v
    u = _mm("mlp_up", h2, wup, b_slots=True, out_dtypes=(BF16,), epi=lambda acc: (jnp.maximum(acc, 0.0),), tm=1024)
    x2, h3 = _mm("mlp_down", u, wdown, a_pre=square, out_dtypes=(F32, BF16), epi=residual_norm, extras=(x1,), consts=[gains["g_ple"]])

    def norm_res_bwd(dh, v, res, g):
        dx, dg = _rms_bwd(dh, v, g, D_MODEL)
        return dx + res, dg

    def tail(x2v, h3v, pv, tv, gf, gp, w_gate, w_emb):
        sg = _sigmoid(jnp.dot(h3v, w_gate[...], preferred_element_type=F32))
        pev = jnp.dot(pv.astype(BF16), w_emb[...], preferred_element_type=F32)
        x3 = x2v + sg * pev
        rs = lax.rsqrt(jnp.sum(x3 * x3, axis=-1, keepdims=True) * (1.0 / D_MODEL) + EPS)
        xh = x3 * rs
        err = xh * gf - tv
        dy = err * (1.0 / D_MODEL)
        dyg = dy * gf
        dx3 = rs * (dyg - xh * (jnp.sum(dyg * xh, axis=-1, keepdims=True) * (1.0 / D_MODEL)))
        dgp = (dx3 * pev * sg * (1.0 - sg)).astype(BF16)
        dh3 = lax.dot_general(dgp, w_gate[...], (((1,), (1,)), ((), ())), preferred_element_type=F32)
        dx2v, dgple = norm_res_bwd(dh3, x2v, dx3, gp)
        return (dx2v, dgp, dx3 * sg, jnp.sum(err * err, axis=0, keepdims=True) * (0.5 / D_MODEL),
                jnp.sum(dy * xh, axis=0, keepdims=True), dgple)

    dx2, dgpre, dpe, loss_part, dg_final, dg_ple = _rowwise(
        "tail", tail, [full(x2), full(h3), full(p2), full(tg)], [wide(F32), wide(BF16), wide(BF16)],
        consts=[gains["g_final"].reshape(1, D_MODEL), gains["g_ple"], wpg, wple], accs=[(1, D_MODEL)] * 3, tm=256)

    dw = {}
    dw["w_ple"] = _mm_tn("dw_ple", p2, dpe)
    dw["w_down"] = _mm_tn("dw_down", u, dx2, a_pre=square)
    dupre = _mm("d_mlp_down", dx2, wdown, trans_b=True, out_dtypes=(BF16,), epi=lambda acc, uv: (acc * (2.0 * uv.astype(F32)),),
                extras=(u,), tn=2048)
    dw["w_up"] = _mm_tn("dw_up", h2, dupre, out_slots=True)
    n_up = wup.shape[0]
    dx1, dg_mlp = _mm("d_mlp_up", [(dupre, j, wup.shape[2]) for j in range(n_up)], [(wup, j) for j in range(n_up)], trans_b=True,
                      epi=norm_res_bwd, extras=(x1, dx2), consts=[gains["g_mlp"]],
                      accs=[(1, D_MODEL)], tm=512)
    dw["w_o"] = _mm_tn("dw_o", merged, dx1)

    def merge_bwd(dm, ga, gb, a, b, w_a, w_b):
        sa, sb = _sigmoid(ga), _sigmoid(gb)
        da, db = (dm * sa).astype(BF16), (dm * sb).astype(BF16)
        nt = (((1,), (1,)), ((), ()))
        return (da, db, dm * a * sa * (1.0 - sa), dm * b * sb * (1.0 - sb),
                lax.dot_general(da, w_a, nt, preferred_element_type=F32), lax.dot_general(db, w_b, nt, preferred_element_type=F32))

    dya, dyb, dga, dgb, do_a, do_b = _mm("d_out_proj", dx1, wo, trans_b=True, out_dtypes=(BF16,) * 6, epi=merge_bwd,
                                         extras=((z, ZC_GA // 8), (z, ZC_GB // 8), ya, yb), consts=[woa, wob], tm=256)
    dw_three = _mm_tn_stream("dw_ple_gate_oa_ob", [h3, o_a, o_b], [dgpre, dya, dyb])
    dw["w_ple_gate"] = dw_three[0]
    dw["w_oa"], dw["w_ob"] = _unpad_heads_rows(dw_three[1], H_A, V_DIM_A), _unpad_heads_rows(dw_three[2], H_B, HD_B)
    res_a = _attn_bwd("attn_a_bwd", q_a, kv_a, 0, kv_a, H_A, o_a, do_a, lse_a, heads=H_A, group=1, nseq=nb, seq=seq,
                      rider=ride and ride["scatter_a"](dw))
    dq_a, dk_a, dv_a = res_a[:3]
    if ride is not None:
        ride["out"]["parts_a"] = res_a[3:]

    def rope_bwd(scale):
        return lambda d, cos, sin, pm_t: ((d * cos + _perm(d * sin, pm_t)) * scale,)

    nt_dims = (((1,), (1,)), ((), ()))

    def q_path_bwd(dq, zq, cos, sin, g, w, pm_t):
        dqav = _per_head(rope_bwd(SCALE_A), H_A, 1, 1)(dq, cos, sin, pm_t)[0].astype(BF16)
        dcq = lax.dot_general(dqav, w[...], nt_dims, preferred_element_type=F32)
        return (dqav, *_rms_bwd(dcq, zq, g, Q_LORA))

    dqa, dq_lat, dg_qa = _rowwise("q_path_bwd", q_path_bwd, [full(dq_a), z_qlat], [heads_tile(H_A), (Q_LORA, BF16, Q_LORA, False)],
                                  pos=[cq_t, sq_t], consts=[gains["g_qa"], wqb, pq_t], accs=[(1, Q_LORA)], seq=seq)
    dw["w_qb"] = _unpad_heads_cols(_mm_tn("dw_qb", cq, dqa), H_A, QK_NOPE + QK_ROPE)
    dw_kv_blocks = _mm_tn_stream("dw_kv", kin, [dk_a, dv_a])
    dw["w_kvb"] = _wkv_unext(dw_kv_blocks[0], dw_kv_blocks[1])
    dq_b, dk_b, dv_b, *parts_b = _attn_bwd("attn_b_bwd", q_b, k_b, 0, z, ZC_VB, o_b, do_b, lse_b, heads=H_B, group=H_B // KV_B,
                                               nseq=nb, seq=seq, rider=ride and ride["scatter_b"](dw))
    if ride is not None:
        ride["out"]["parts_b"] = parts_b

    def kv_path_bwd(dk, dv, ckv_raw, cos, sin, g, w, pm_t):
        kv_w = H_A * HEAD_W
        wv = w[...]
        dkin = (lax.dot_general(dk, wv[:, :kv_w], nt_dims, preferred_element_type=F32)
                + lax.dot_general(dv, wv[:, kv_w:], nt_dims, preferred_element_type=F32))
        dckv_raw, dg = _rms_bwd(dkin[:, :HEAD_W], ckv_raw, g, KV_LORA)
        return (dckv_raw, *rope_bwd(1.0)(dkin[:, HEAD_W:], cos, sin, pm_t), dg)

    dckv, dkpe, dg_kva = _rowwise("kv_path_bwd", kv_path_bwd, [full(dk_a), full(dv_a), z_ckv], [heads_tile(1), heads_tile(1)],
                                  pos=[ck_t, sk_t], consts=[gains["g_kva"], wkv, pk_t], accs=[(1, KV_LORA)], seq=seq)

    def prep_bwd(scale):
        def fn(d, v, cos, sin, g, pm_t):
            dyv = (d * cos + _perm(d * sin, pm_t)) * scale
            return _rms_bwd(dyv, v, g, HD_B)
        return fn

    def prep_b_bwd(dq, dk, zq, zk, cos, sin, gq, gk, pm_t):
        dq_raw, dgq = _per_head(prep_bwd(SCALE_B), H_B, 2, 1)(dq, zq, cos, sin, gq, pm_t)
        dk_raw, dgk = _per_head(prep_bwd(1.0), KV_B, 2, 1)(dk, zk, cos, sin, gk, pm_t)
        return dq_raw, dk_raw, dgq, dgk

    dqb, dkb, dg_qn, dg_kn = _rowwise("prep_b_bwd", prep_b_bwd, [full(dq_b), full(dk_b), z_qb, z_kb], [heads_tile(H_B), heads_tile(KV_B)],
                                      pos=[cb_t, sb_t], consts=[g_qn, g_kn, pb_t], accs=[(1, HEAD_W)] * 2, seq=seq)

    dz = [dqb, jnp.concatenate([dq_lat, dckv, dkpe, dkb, dv_b], axis=1), dga, dgb]
    dw_in_blocks = _mm_tn_stream("dw_in", h, dz)
    dw["w_in"] = _win_unext([dw_in_blocks[j] for j in range(4)])
    dx0, dg_mix, *parts_in = _mm("d_in_proj", dz, [(win, j, D_MODEL) for j in range(4)], trans_b=True, epi=norm_res_bwd, extras=(x0, dx1), consts=[gains["g_mix"]],
                                 accs=[(1, D_MODEL)], tm=256, rider=ride and ride["scatter_in"](dw))
    if ride is not None:
        ride["out"]["parts_in"] = parts_in

    dg = {"g_mix": dg_mix, "g_qa": dg_qa, "g_kva": dg_kva, "g_qn": dg_qn[:, :HD_B], "g_kn": dg_kn[:, :HD_B],
          "g_mlp": dg_mlp, "g_ple": dg_ple, "g_final": dg_final}
    return loss_part, dx0.reshape(nb, seq, D_MODEL), dg, dw


def _pack_small(vals, loss_part=None):
    flat = jnp.concatenate([vals[n].reshape(1, -1) for n, _ in SMALL], axis=1)
    loss = jnp.zeros((1, 8 * 128), F32) if loss_part is None else loss_part
    gap = jnp.zeros((1, LOSS_ROW0 * 128 - SMALL_N), F32)
    return jnp.concatenate([flat, gap, loss], axis=1).reshape(SMALL_ROWS, 128)


def _unpack_small(slab, like):
    flat, out, off = slab.reshape(-1), {}, 0
    for n, k in SMALL:
        out[n] = flat[off:off + k].reshape(like[n].shape)
        off += k
    return out


def kernel(x, p, g_mix, w_in, g_qa, w_qb, g_kva, w_kvb, g_qn, g_kn, w_oa, w_ob, w_o, g_mlp, w_up, w_down, g_ple, w_ple_gate, w_ple, g_final, loss_target, m_g_mix, m_w_in, m_g_qa, m_w_qb, m_g_kva, m_w_kvb, m_g_qn, m_g_kn, m_w_oa, m_w_ob, m_w_o, m_g_mlp, m_w_up, m_w_down, m_g_ple, m_w_ple_gate, m_w_ple, m_g_final, v_g_mix, v_w_in, v_g_qa, v_w_qb, v_g_kva, v_w_kvb, v_g_qn, v_g_kn, v_w_oa, v_w_ob, v_w_o, v_g_mlp, v_w_up, v_w_down, v_g_ple, v_w_ple_gate, v_w_ple, v_g_final):
    given = dict(locals())
    order = ["g_mix", "w_in", "g_qa", "w_qb", "g_kva", "w_kvb", "g_qn", "g_kn", "w_oa", "w_ob", "w_o", "g_mlp", "w_up",
             "w_down", "g_ple", "w_ple_gate", "w_ple", "g_final"]
    big_names = [n for n, _, _, _ in BIG]
    local = lambda prefix, names: [given[prefix + n][0] for n in names]
    slab = lambda names: _pack_shards(names, local("", names), BF16)
    bf = lambda n: given[n][0].astype(BF16)
    cols_full = lambda g: g.transpose(1, 0, 2).reshape(g.shape[1], -1)
    rows_full = lambda g: g.reshape(-1, g.shape[2])
    shards_cols = lambda a: a.reshape(a.shape[0], 4, a.shape[1] // 4).transpose(1, 0, 2)
    shards_rows = lambda a: a.reshape(4, a.shape[0] // 4, a.shape[1])
    packed = lambda names, dw: _pack_full(names, [dw[n] for n in names], BF16)
    branch_out = ["w_oa", "w_ob"]
    back_a, back_b = SLAB_LATE + ["w_o"], SLAB_EARLY + ["w_ple_gate"]

    got_in, got_early = _gather_by_halves("weight_gather_early", [bf("w_in"), slab(SLAB_EARLY)])
    wts = {"w_in": cols_full(got_in), **dict(zip(SLAB_EARLY, _unpack_full(SLAB_EARLY, got_early)))}
    gains = {n: given[n].reshape(1, -1) for n, _ in SMALL}
    ride = {
        "gather": {
            "in_proj": (_Exchange("gather", [bf("w_o")]), lambda got: {"w_o": rows_full(got[0])}),
            "prep_b": (_Exchange("gather", [slab(branch_out)]), lambda got: dict(zip(branch_out, _unpack_full(branch_out, got[0])))),
            "attn_a_fwd": (_Exchange("gather", [bf("w_up")]), lambda got: {"w_up": got[0]}),
            "attn_b_fwd": (_Exchange("gather", [bf("w_down")]), lambda got: {"w_down": rows_full(got[0])}),
            "mix_out": (_Exchange("gather", [bf("w_ple_gate"), bf("w_ple")]),
                        lambda got: {"w_ple_gate": rows_full(got[0]), "w_ple": cols_full(got[1])}),
        },
        "scatter_a": lambda dw: _Exchange("scatter", [dw["w_up"], packed(back_a, dw)]),
        "scatter_b": lambda dw: _Exchange("scatter", [shards_rows(dw["w_down"]), packed(back_b, dw)]),
        "scatter_in": lambda dw: _Exchange("scatter", [_presum_halves("grad_presum_in", shards_cols(dw["w_in"]))]),
        "out": {},
    }
    loss_part, grad_x, dg, dw = _local_step(x, p[0], loss_target, gains, wts, ride)

    small = lambda prefix: _pack_small({n: given[prefix + n] for n, _ in SMALL})
    g_s, d_s, m_s, v_s, loss = _small_allreduce_adamw(_pack_small(dg, loss_part), small(""), small("m_"), small("v_"))

    parts = ride["out"]
    grads = {"w_up": _reduce_pair("grad_reduce_up", parts["parts_a"][0]), "w_down": _reduce_pair("grad_reduce_down", parts["parts_b"][0]),
             "w_in": _reduce_halves("grad_reduce_in", parts["parts_in"][0])}
    grads.update(zip(back_a, _unpack_shards(back_a, _reduce_pair("grad_reduce_slab_a", parts["parts_a"][1]))))
    grads.update(zip(back_b, _unpack_shards(back_b, _reduce_pair("grad_reduce_slab_b", parts["parts_b"][1]))))

    res = {}
    for key, slab in (("grad_", g_s), ("delta_", d_s), ("new_m_", m_s), ("new_v_", v_s)):
        for n, val in _unpack_small(slab, given).items():
            res[key + n] = val
    for n in big_names:
        res["grad_" + n], res["delta_" + n], res["new_m_" + n], res["new_v_" + n] = _adamw_shard(
            "adamw_" + n, grads[n], given[n], given["m_" + n], given["v_" + n])
    outs = [loss.reshape(()), grad_x]
    for key in ("grad_", "delta_", "new_m_", "new_v_"):
        outs += [res[key + n] for n in order]
    return tuple(outs)
```
